```python
import math
import jax, jax.numpy as jnp
from jax import lax
import numpy as np

D_MODEL = 2048
BATCH = 8
SEQ = 8192
DEPTH = 1

HEAD_DIM = 128
HEADS_PER_GROUP = 4
ATTN_GROUPS = ((128, 1), (512, 4), (2048, 16))
N_ATTN_HEADS = HEADS_PER_GROUP * len(ATTN_GROUPS)
ATTN_OUT_WIDTH = HEADS_PER_GROUP * HEAD_DIM
HQ = N_ATTN_HEADS * HEAD_DIM
SSM_WIDTH = D_MODEL // 2
SSM_GROUP = 16
SSM_GROUPS = SSM_WIDTH // SSM_GROUP
SSM_STATE = 64
D_FF = -(-8 * D_MODEL // (3 * 256)) * 256
IN_SIZES = (HQ, HQ, HQ, SSM_WIDTH, D_MODEL, D_MODEL)
IN_WIDTH = sum(IN_SIZES)
IN_SPLITS = tuple(int(i) for i in np.cumsum(IN_SIZES)[:-1])
EPS = 1e-6
DT_MIN = 1e-3
DT_MAX = 1e-1

kernel_name = 'hybrid_dilated_attn_s5_gated'


def rms_norm(x, gain):
    xf = x.astype(jnp.float32)
    y = xf * lax.rsqrt(jnp.mean(xf * xf, axis=-1, keepdims=True) + EPS)
    return (y * gain.astype(jnp.float32)).astype(x.dtype)


def alibi_slopes(n):
    return jnp.exp2(-8.0 * jnp.arange(1, n + 1, dtype=jnp.float32) / n)


def dilated_window_attention(q, k, v, slopes, window, dilation):
    b, s, h, e = q.shape
    blk = window // dilation
    sub_len = -(-s // dilation)
    n_blk = -(-sub_len // blk)
    s_pad = n_blk * blk * dilation

    def to_blocks(t):
        t = jnp.pad(t, ((0, 0), (0, s_pad - s), (0, 0), (0, 0)))
        t = t.reshape(b, n_blk * blk, dilation, h, e).transpose(0, 2, 1, 3, 4)
        return t.reshape(b, dilation, n_blk, blk, h, e)

    def with_prev(t):
        prev = jnp.pad(t, ((0, 0), (0, 0), (1, 0), (0, 0), (0, 0), (0, 0)))[:, :, :-1]
        return jnp.concatenate([prev, t], axis=3)

    def from_blocks(t):
        f = t.shape[-1]
        t = t.reshape(b, dilation, n_blk * blk, h, f).transpose(0, 2, 1, 3, 4)
        return t.reshape(b, s_pad, h, f)[:, :s]

    qb = to_blocks(q)
    kw = with_prev(to_blocks(k))
    vw = with_prev(to_blocks(v))
    scores = jnp.einsum('brnqhe,brnkhe->brnhqk', qb, kw).astype(jnp.float32) * (e ** -0.5)
    qi = jnp.arange(blk)[:, None]
    ki = jnp.arange(2 * blk)[None, :]
    dist = blk + qi - ki
    blk_idx = jnp.arange(n_blk)[:, None, None]
    valid = ((dist >= 0) & (dist <= blk))[None] & (blk_idx * blk - blk + ki[None] >= 0)
    bias = -slopes[:, None, None] * (dist * dilation).astype(jnp.float32)[None]
    scores = scores + bias[None, None, None]
    scores = jnp.where(valid[None, None, :, None], scores, -jnp.inf)
    lse = jax.nn.logsumexp(scores, axis=-1)
    probs = jnp.exp(scores - lse[..., None])
    out = jnp.einsum('brnhqk,brnkhe->brnqhe', probs.astype(v.dtype), vw)
    lse = from_blocks(lse.transpose(0, 1, 2, 4, 3)[..., None])[..., 0]
    return from_blocks(out), lse


def s5_ssm(u, a_re, a_im, log_dt, b_re, b_im, c_re, c_im, d_skip):
    bsz, s, _ = u.shape
    uf = u.astype(jnp.float32)
    ug = uf.reshape(bsz, s, SSM_GROUPS, SSM_GROUP)
    lam = lax.complex(a_re.astype(jnp.float32), a_im.astype(jnp.float32))
    dt = jnp.exp(log_dt.astype(jnp.float32))[:, None]
    lam_bar = jnp.exp(lam * dt)
    b_cplx = lax.complex(b_re.astype(jnp.float32), b_im.astype(jnp.float32))
    b_bar = ((lam_bar - 1.0) / lam)[..., None] * b_cplx
    bu = lax.complex(jnp.einsum('bsgc,gpc->sbgp', ug, jnp.real(b_bar)),
                     jnp.einsum('bsgc,gpc->sbgp', ug, jnp.imag(b_bar)))
    a_seq = jnp.broadcast_to(lam_bar[None, None], (s, 1) + lam_bar.shape)

    def combine(left, right):
        a_l, b_l = left
        a_r, b_r = right
        return a_r * a_l, a_r * b_l + b_r

    _, states = lax.associative_scan(combine, (a_seq, bu), axis=0)
    y = (jnp.einsum('sbgp,gcp->bsgc', jnp.real(states), c_re.astype(jnp.float32))
         - jnp.einsum('sbgp,gcp->bsgc', jnp.imag(states), c_im.astype(jnp.float32)))
    y = y.reshape(bsz, s, SSM_WIDTH) + d_skip.astype(jnp.float32) * uf
    return y.astype(u.dtype)


def hybrid_layer(x, norm_mix_pre, w_in, w_attn_up, ssm_a_re, ssm_a_im, ssm_log_dt, ssm_b_re, ssm_b_im,
                 ssm_c_re, ssm_c_im, ssm_d, w_glu_v, w_glu_g, w_out, norm_mix_post, norm_ffn_pre,
                 w_ffn_gate, w_ffn_up, w_ffn_down, norm_ffn_post):
    bsz, s, _ = x.shape
    h = rms_norm(x, norm_mix_pre)
    q, k, v, u, gate_a, gate_s = jnp.split(h @ w_in, IN_SPLITS, axis=-1)
    q = q.reshape(bsz, s, N_ATTN_HEADS, HEAD_DIM)
    k = k.reshape(bsz, s, N_ATTN_HEADS, HEAD_DIM)
    v = v.reshape(bsz, s, N_ATTN_HEADS, HEAD_DIM)
    slopes = alibi_slopes(N_ATTN_HEADS)
    outs, lses = [], []
    for g, (window, dilation) in enumerate(ATTN_GROUPS):
        hs = slice(g * HEADS_PER_GROUP, (g + 1) * HEADS_PER_GROUP)
        o, l = dilated_window_attention(q[:, :, hs], k[:, :, hs], v[:, :, hs], slopes[hs], window, dilation)
        outs.append(o)
        lses.append(l)
    mix_w = jax.nn.softmax(jnp.stack(lses), axis=0)
    attn = jnp.sum(mix_w[..., None] * jnp.stack(outs).astype(jnp.float32), axis=0)
    attn_branch = attn.reshape(bsz, s, ATTN_OUT_WIDTH).astype(x.dtype) @ w_attn_up
    y = jax.nn.gelu(s5_ssm(u, ssm_a_re, ssm_a_im, ssm_log_dt, ssm_b_re, ssm_b_im, ssm_c_re, ssm_c_im, ssm_d))
    ssm_branch = (y @ w_glu_v) * jax.nn.sigmoid(y @ w_glu_g)
    merged = jax.nn.sigmoid(gate_a) * attn_branch + jax.nn.sigmoid(gate_s) * ssm_branch
    x = x + rms_norm(merged @ w_out, norm_mix_post)
    h = rms_norm(x, norm_ffn_pre)
    f = (jax.nn.silu(h @ w_ffn_gate) * (h @ w_ffn_up)) @ w_ffn_down
    return x + rms_norm(f, norm_ffn_post)


def _fwd_setup_inputs(seed: int = 0) -> dict:
    key = jax.random.key(seed)
    ks = jax.random.split(key, 24)

    def nrm(k, shape, scale):
        return jax.random.normal(k, shape, jnp.float32) * scale

    def gain(k, n):
        return 1.0 + 0.02 * jax.random.normal(k, (DEPTH, n), jnp.float32)

    g, p, c = SSM_GROUPS, SSM_STATE, SSM_GROUP
    return {
        'x': nrm(ks[0], (BATCH, SEQ, D_MODEL), 1.0),
        'norm_mix_pre': gain(ks[1], D_MODEL),
        'w_in': nrm(ks[2], (DEPTH, D_MODEL, IN_WIDTH), D_MODEL ** -0.5),
        'w_attn_up': nrm(ks[3], (DEPTH, ATTN_OUT_WIDTH, D_MODEL), ATTN_OUT_WIDTH ** -0.5),
        'ssm_a_re': -0.5 + 0.01 * jax.random.normal(ks[4], (DEPTH, g, p), jnp.float32),
        'ssm_a_im': jnp.pi * jnp.arange(p, dtype=jnp.float32)[None, None, :] + 0.01 * jax.random.normal(ks[5], (DEPTH, g, p), jnp.float32),
        'ssm_log_dt': jax.random.uniform(ks[6], (DEPTH, g), jnp.float32, math.log(DT_MIN), math.log(DT_MAX)),
        'ssm_b_re': nrm(ks[7], (DEPTH, g, p, c), (2 * c) ** -0.5),
        'ssm_b_im': nrm(ks[8], (DEPTH, g, p, c), (2 * c) ** -0.5),
        'ssm_c_re': nrm(ks[9], (DEPTH, g, c, p), (2 * p) ** -0.5 * 4.0),
        'ssm_c_im': nrm(ks[10], (DEPTH, g, c, p), (2 * p) ** -0.5 * 4.0),
        'ssm_d': nrm(ks[11], (DEPTH, SSM_WIDTH), 1.0),
        'w_glu_v': nrm(ks[12], (DEPTH, SSM_WIDTH, D_MODEL), SSM_WIDTH ** -0.5),
        'w_glu_g': nrm(ks[13], (DEPTH, SSM_WIDTH, D_MODEL), SSM_WIDTH ** -0.5),
        'w_out': nrm(ks[14], (DEPTH, D_MODEL, D_MODEL), D_MODEL ** -0.5),
        'norm_mix_post': gain(ks[15], D_MODEL),
        'norm_ffn_pre': gain(ks[16], D_MODEL),
        'w_ffn_gate': nrm(ks[17], (DEPTH, D_MODEL, D_FF), D_MODEL ** -0.5),
        'w_ffn_up': nrm(ks[18], (DEPTH, D_MODEL, D_FF), D_MODEL ** -0.5),
        'w_ffn_down': nrm(ks[19], (DEPTH, D_FF, D_MODEL), D_FF ** -0.5),
        'norm_ffn_post': gain(ks[20], D_MODEL),
    }


def _fwd_reference(x, norm_mix_pre, w_in, w_attn_up, ssm_a_re, ssm_a_im, ssm_log_dt, ssm_b_re, ssm_b_im,
              ssm_c_re, ssm_c_im, ssm_d, w_glu_v, w_glu_g, w_out, norm_mix_post, norm_ffn_pre,
              w_ffn_gate, w_ffn_up, w_ffn_down, norm_ffn_post):
    for i in range(DEPTH):
        x = hybrid_layer(x, norm_mix_pre[i], w_in[i], w_attn_up[i], ssm_a_re[i], ssm_a_im[i], ssm_log_dt[i],
                         ssm_b_re[i], ssm_b_im[i], ssm_c_re[i], ssm_c_im[i], ssm_d[i], w_glu_v[i], w_glu_g[i],
                         w_out[i], norm_mix_post[i], norm_ffn_pre[i], w_ffn_gate[i], w_ffn_up[i],
                         w_ffn_down[i], norm_ffn_post[i])
    return x


import jax as _jax
import jax.numpy as _jnp

TWIN_FORMAT = 'train_step'
FWD_PARAMS = ['x', 'norm_mix_pre', 'w_in', 'w_attn_up', 'ssm_a_re', 'ssm_a_im', 'ssm_log_dt', 'ssm_b_re', 'ssm_b_im', 'ssm_c_re', 'ssm_c_im', 'ssm_d', 'w_glu_v', 'w_glu_g', 'w_out', 'norm_mix_post', 'norm_ffn_pre', 'w_ffn_gate', 'w_ffn_up', 'w_ffn_down', 'norm_ffn_post']
TWIN_WEIGHTS = ['norm_mix_pre', 'w_in', 'w_attn_up', 'ssm_a_re', 'ssm_a_im', 'ssm_log_dt', 'ssm_b_re', 'ssm_b_im', 'ssm_c_re', 'ssm_c_im', 'ssm_d', 'w_glu_v', 'w_glu_g', 'w_out', 'norm_mix_post', 'norm_ffn_pre', 'w_ffn_gate', 'w_ffn_up', 'w_ffn_down', 'norm_ffn_post']
TWIN_DIFF_INPUT = 'x'
TWIN_INPUTS = ['x', 'norm_mix_pre', 'w_in', 'w_attn_up', 'ssm_a_re', 'ssm_a_im', 'ssm_log_dt', 'ssm_b_re', 'ssm_b_im', 'ssm_c_re', 'ssm_c_im', 'ssm_d', 'w_glu_v', 'w_glu_g', 'w_out', 'norm_mix_post', 'norm_ffn_pre', 'w_ffn_gate', 'w_ffn_up', 'w_ffn_down', 'norm_ffn_post', 'loss_target', 'm_norm_mix_pre', 'm_w_in', 'm_w_attn_up', 'm_ssm_a_re', 'm_ssm_a_im', 'm_ssm_log_dt', 'm_ssm_b_re', 'm_ssm_b_im', 'm_ssm_c_re', 'm_ssm_c_im', 'm_ssm_d', 'm_w_glu_v', 'm_w_glu_g', 'm_w_out', 'm_norm_mix_post', 'm_norm_ffn_pre', 'm_w_ffn_gate', 'm_w_ffn_up', 'm_w_ffn_down', 'm_norm_ffn_post', 'v_norm_mix_pre', 'v_w_in', 'v_w_attn_up', 'v_ssm_a_re', 'v_ssm_a_im', 'v_ssm_log_dt', 'v_ssm_b_re', 'v_ssm_b_im', 'v_ssm_c_re', 'v_ssm_c_im', 'v_ssm_d', 'v_w_glu_v', 'v_w_glu_g', 'v_w_out', 'v_norm_mix_post', 'v_norm_ffn_pre', 'v_w_ffn_gate', 'v_w_ffn_up', 'v_w_ffn_down', 'v_norm_ffn_post']
TWIN_OUTPUTS = ['loss', 'grad_x', 'grad_norm_mix_pre', 'grad_w_in', 'grad_w_attn_up', 'grad_ssm_a_re', 'grad_ssm_a_im', 'grad_ssm_log_dt', 'grad_ssm_b_re', 'grad_ssm_b_im', 'grad_ssm_c_re', 'grad_ssm_c_im', 'grad_ssm_d', 'grad_w_glu_v', 'grad_w_glu_g', 'grad_w_out', 'grad_norm_mix_post', 'grad_norm_ffn_pre', 'grad_w_ffn_gate', 'grad_w_ffn_up', 'grad_w_ffn_down', 'grad_norm_ffn_post', 'delta_norm_mix_pre', 'delta_w_in', 'delta_w_attn_up', 'delta_ssm_a_re', 'delta_ssm_a_im', 'delta_ssm_log_dt', 'delta_ssm_b_re', 'delta_ssm_b_im', 'delta_ssm_c_re', 'delta_ssm_c_im', 'delta_ssm_d', 'delta_w_glu_v', 'delta_w_glu_g', 'delta_w_out', 'delta_norm_mix_post', 'delta_norm_ffn_pre', 'delta_w_ffn_gate', 'delta_w_ffn_up', 'delta_w_ffn_down', 'delta_norm_ffn_post', 'new_m_norm_mix_pre', 'new_m_w_in', 'new_m_w_attn_up', 'new_m_ssm_a_re', 'new_m_ssm_a_im', 'new_m_ssm_log_dt', 'new_m_ssm_b_re', 'new_m_ssm_b_im', 'new_m_ssm_c_re', 'new_m_ssm_c_im', 'new_m_ssm_d', 'new_m_w_glu_v', 'new_m_w_glu_g', 'new_m_w_out', 'new_m_norm_mix_post', 'new_m_norm_ffn_pre', 'new_m_w_ffn_gate', 'new_m_w_ffn_up', 'new_m_w_ffn_down', 'new_m_norm_ffn_post', 'new_v_norm_mix_pre', 'new_v_w_in', 'new_v_w_attn_up', 'new_v_ssm_a_re', 'new_v_ssm_a_im', 'new_v_ssm_log_dt', 'new_v_ssm_b_re', 'new_v_ssm_b_im', 'new_v_ssm_c_re', 'new_v_ssm_c_im', 'new_v_ssm_d', 'new_v_w_glu_v', 'new_v_w_glu_g', 'new_v_w_out', 'new_v_norm_mix_post', 'new_v_norm_ffn_pre', 'new_v_w_ffn_gate', 'new_v_w_ffn_up', 'new_v_w_ffn_down', 'new_v_norm_ffn_post']
TWIN_LEAF_KINDS = {'loss': 'loss', 'grad_x': 'grad_x', 'grad_norm_mix_pre': 'grad_w', 'grad_w_in': 'grad_w', 'grad_w_attn_up': 'grad_w', 'grad_ssm_a_re': 'grad_w', 'grad_ssm_a_im': 'grad_w', 'grad_ssm_log_dt': 'grad_w', 'grad_ssm_b_re': 'grad_w', 'grad_ssm_b_im': 'grad_w', 'grad_ssm_c_re': 'grad_w', 'grad_ssm_c_im': 'grad_w', 'grad_ssm_d': 'grad_w', 'grad_w_glu_v': 'grad_w', 'grad_w_glu_g': 'grad_w', 'grad_w_out': 'grad_w', 'grad_norm_mix_post': 'grad_w', 'grad_norm_ffn_pre': 'grad_w', 'grad_w_ffn_gate': 'grad_w', 'grad_w_ffn_up': 'grad_w', 'grad_w_ffn_down': 'grad_w', 'grad_norm_ffn_post': 'grad_w', 'delta_norm_mix_pre': 'delta_w', 'delta_w_in': 'delta_w', 'delta_w_attn_up': 'delta_w', 'delta_ssm_a_re': 'delta_w', 'delta_ssm_a_im': 'delta_w', 'delta_ssm_log_dt': 'delta_w', 'delta_ssm_b_re': 'delta_w', 'delta_ssm_b_im': 'delta_w', 'delta_ssm_c_re': 'delta_w', 'delta_ssm_c_im': 'delta_w', 'delta_ssm_d': 'delta_w', 'delta_w_glu_v': 'delta_w', 'delta_w_glu_g': 'delta_w', 'delta_w_out': 'delta_w', 'delta_norm_mix_post': 'delta_w', 'delta_norm_ffn_pre': 'delta_w', 'delta_w_ffn_gate': 'delta_w', 'delta_w_ffn_up': 'delta_w', 'delta_w_ffn_down': 'delta_w', 'delta_norm_ffn_post': 'delta_w', 'new_m_norm_mix_pre': 'new_m', 'new_m_w_in': 'new_m', 'new_m_w_attn_up': 'new_m', 'new_m_ssm_a_re': 'new_m', 'new_m_ssm_a_im': 'new_m', 'new_m_ssm_log_dt': 'new_m', 'new_m_ssm_b_re': 'new_m', 'new_m_ssm_b_im': 'new_m', 'new_m_ssm_c_re': 'new_m', 'new_m_ssm_c_im': 'new_m', 'new_m_ssm_d': 'new_m', 'new_m_w_glu_v': 'new_m', 'new_m_w_glu_g': 'new_m', 'new_m_w_out': 'new_m', 'new_m_norm_mix_post': 'new_m', 'new_m_norm_ffn_pre': 'new_m', 'new_m_w_ffn_gate': 'new_m', 'new_m_w_ffn_up': 'new_m', 'new_m_w_ffn_down': 'new_m', 'new_m_norm_ffn_post': 'new_m', 'new_v_norm_mix_pre': 'new_v', 'new_v_w_in': 'new_v', 'new_v_w_attn_up': 'new_v', 'new_v_ssm_a_re': 'new_v', 'new_v_ssm_a_im': 'new_v', 'new_v_ssm_log_dt': 'new_v', 'new_v_ssm_b_re': 'new_v', 'new_v_ssm_b_im': 'new_v', 'new_v_ssm_c_re': 'new_v', 'new_v_ssm_c_im': 'new_v', 'new_v_ssm_d': 'new_v', 'new_v_w_glu_v': 'new_v', 'new_v_w_glu_g': 'new_v', 'new_v_w_out': 'new_v', 'new_v_norm_mix_post': 'new_v', 'new_v_norm_ffn_pre': 'new_v', 'new_v_w_ffn_gate': 'new_v', 'new_v_w_ffn_up': 'new_v', 'new_v_w_ffn_down': 'new_v', 'new_v_norm_ffn_post': 'new_v'}


def _forward(args):
    return _fwd_reference(*[args[k] for k in FWD_PARAMS])


def _output_shape():
    def fwd():
        inp = _fwd_setup_inputs(0)
        return _fwd_reference(*[inp[k] for k in FWD_PARAMS])
    out = _jax.eval_shape(fwd)
    return out.shape, out.dtype

N_MICROBATCH = 1
ADAM_LR = 0.001
ADAM_B1 = 0.9
ADAM_B2 = 0.999
ADAM_EPS = 1e-08
ADAM_WD = 0.01
ADAM_STEP = 10
PER_EXAMPLE_BATCH_AXIS = {'x': 0, 'loss_target': 0}
SHARED_INPUTS = []
_WEIGHT_DTYPES = {'norm_mix_pre': _jnp.float32, 'w_in': _jnp.float32, 'w_attn_up': _jnp.float32, 'ssm_a_re': _jnp.float32, 'ssm_a_im': _jnp.float32, 'ssm_log_dt': _jnp.float32, 'ssm_b_re': _jnp.float32, 'ssm_b_im': _jnp.float32, 'ssm_c_re': _jnp.float32, 'ssm_c_im': _jnp.float32, 'ssm_d': _jnp.float32, 'w_glu_v': _jnp.float32, 'w_glu_g': _jnp.float32, 'w_out': _jnp.float32, 'norm_mix_post': _jnp.float32, 'norm_ffn_pre': _jnp.float32, 'w_ffn_gate': _jnp.float32, 'w_ffn_up': _jnp.float32, 'w_ffn_down': _jnp.float32, 'norm_ffn_post': _jnp.float32}
MOMENT_SCALE = {'norm_mix_pre': 5.198283e-01, 'w_in': 2.197842e-01, 'w_attn_up': 2.334349e-01, 'ssm_a_re': 8.376648e-02, 'ssm_a_im': 8.217452e-02, 'ssm_log_dt': 8.713547e+01, 'ssm_b_re': 5.478267e-02, 'ssm_b_im': 5.500518e-02, 'ssm_c_re': 2.797359e-02, 'ssm_c_im': 2.754138e-02, 'ssm_d': 1.993872e+00, 'w_glu_v': 1.444100e+00, 'w_glu_g': 2.080947e-01, 'w_out': 1.381568e+00, 'norm_mix_post': 3.249835e+01, 'norm_ffn_pre': 1.147110e+00, 'w_ffn_gate': 3.147027e-01, 'w_ffn_up': 5.984578e-01, 'w_ffn_down': 1.015656e+00, 'norm_ffn_post': 3.204477e+01}


def _to_microbatches(a, axis):
    t = _jnp.moveaxis(a, axis, 0)
    t = t.reshape((N_MICROBATCH, t.shape[0] // N_MICROBATCH) + t.shape[1:])
    return _jnp.moveaxis(t, 1, axis + 1)


def setup_inputs(seed: int = 0) -> dict:
    inp = _fwd_setup_inputs(seed)
    key = _jax.random.fold_in(_jax.random.key(seed), 7919)
    shape, _ = _output_shape()
    out = dict(inp)
    out["loss_target"] = _jax.random.normal(_jax.random.fold_in(key, 0), shape, _jnp.float32)
    for i, name in enumerate(TWIN_WEIGHTS):
        w = inp[name].astype(_jnp.float32)
        if MOMENT_SCALE is None:
            s = _jnp.sqrt(_jnp.mean(_jnp.square(w)) + 1e-30)
        else:
            s = MOMENT_SCALE[name]
        km, kv = _jax.random.split(_jax.random.fold_in(key, i + 1))
        out[name] = w
        out["m_" + name] = s * _jax.random.normal(km, w.shape, _jnp.float32)
        out["v_" + name] = (s * s) * _jax.random.uniform(kv, w.shape, _jnp.float32, 0.5, 1.5)
    if N_MICROBATCH > 1:
        for name, axis in PER_EXAMPLE_BATCH_AXIS.items():
            out[name] = _to_microbatches(out[name], axis)
    return {'x': out['x'], 'norm_mix_pre': out['norm_mix_pre'], 'w_in': out['w_in'], 'w_attn_up': out['w_attn_up'], 'ssm_a_re': out['ssm_a_re'], 'ssm_a_im': out['ssm_a_im'], 'ssm_log_dt': out['ssm_log_dt'], 'ssm_b_re': out['ssm_b_re'], 'ssm_b_im': out['ssm_b_im'], 'ssm_c_re': out['ssm_c_re'], 'ssm_c_im': out['ssm_c_im'], 'ssm_d': out['ssm_d'], 'w_glu_v': out['w_glu_v'], 'w_glu_g': out['w_glu_g'], 'w_out': out['w_out'], 'norm_mix_post': out['norm_mix_post'], 'norm_ffn_pre': out['norm_ffn_pre'], 'w_ffn_gate': out['w_ffn_gate'], 'w_ffn_up': out['w_ffn_up'], 'w_ffn_down': out['w_ffn_down'], 'norm_ffn_post': out['norm_ffn_post'], 'loss_target': out['loss_target'], 'm_norm_mix_pre': out['m_norm_mix_pre'], 'm_w_in': out['m_w_in'], 'm_w_attn_up': out['m_w_attn_up'], 'm_ssm_a_re': out['m_ssm_a_re'], 'm_ssm_a_im': out['m_ssm_a_im'], 'm_ssm_log_dt': out['m_ssm_log_dt'], 'm_ssm_b_re': out['m_ssm_b_re'], 'm_ssm_b_im': out['m_ssm_b_im'], 'm_ssm_c_re': out['m_ssm_c_re'], 'm_ssm_c_im': out['m_ssm_c_im'], 'm_ssm_d': out['m_ssm_d'], 'm_w_glu_v': out['m_w_glu_v'], 'm_w_glu_g': out['m_w_glu_g'], 'm_w_out': out['m_w_out'], 'm_norm_mix_post': out['m_norm_mix_post'], 'm_norm_ffn_pre': out['m_norm_ffn_pre'], 'm_w_ffn_gate': out['m_w_ffn_gate'], 'm_w_ffn_up': out['m_w_ffn_up'], 'm_w_ffn_down': out['m_w_ffn_down'], 'm_norm_ffn_post': out['m_norm_ffn_post'], 'v_norm_mix_pre': out['v_norm_mix_pre'], 'v_w_in': out['v_w_in'], 'v_w_attn_up': out['v_w_attn_up'], 'v_ssm_a_re': out['v_ssm_a_re'], 'v_ssm_a_im': out['v_ssm_a_im'], 'v_ssm_log_dt': out['v_ssm_log_dt'], 'v_ssm_b_re': out['v_ssm_b_re'], 'v_ssm_b_im': out['v_ssm_b_im'], 'v_ssm_c_re': out['v_ssm_c_re'], 'v_ssm_c_im': out['v_ssm_c_im'], 'v_ssm_d': out['v_ssm_d'], 'v_w_glu_v': out['v_w_glu_v'], 'v_w_glu_g': out['v_w_glu_g'], 'v_w_out': out['v_w_out'], 'v_norm_mix_post': out['v_norm_mix_post'], 'v_norm_ffn_pre': out['v_norm_ffn_pre'], 'v_w_ffn_gate': out['v_w_ffn_gate'], 'v_w_ffn_up': out['v_w_ffn_up'], 'v_w_ffn_down': out['v_w_ffn_down'], 'v_norm_ffn_post': out['v_norm_ffn_post']}


def _loss(weights, diff, rest, loss_target):
    with _jax.named_scope("forward"):
        args = {**rest, TWIN_DIFF_INPUT: diff, **{k: w.astype(_WEIGHT_DTYPES[k]) for k, w in weights.items()}}
        y = _forward(args)
    with _jax.named_scope("loss_head"):
        err = _jnp.square(y.astype(_jnp.float32) - loss_target)
        return 0.5 * _jnp.sum(_jnp.mean(err, axis=-1)) if err.ndim else 0.5 * err


def _adamw(w, g, m, v):
    m = ADAM_B1 * m + (1.0 - ADAM_B1) * g
    v = ADAM_B2 * v + (1.0 - ADAM_B2) * _jnp.square(g)
    m_hat = m / (1.0 - ADAM_B1 ** ADAM_STEP)
    v_hat = v / (1.0 - ADAM_B2 ** ADAM_STEP)
    delta = -ADAM_LR * (m_hat / (_jnp.sqrt(v_hat) + ADAM_EPS) + ADAM_WD * w)
    return delta, m, v


def reference(x, norm_mix_pre, w_in, w_attn_up, ssm_a_re, ssm_a_im, ssm_log_dt, ssm_b_re, ssm_b_im, ssm_c_re, ssm_c_im, ssm_d, w_glu_v, w_glu_g, w_out, norm_mix_post, norm_ffn_pre, w_ffn_gate, w_ffn_up, w_ffn_down, norm_ffn_post, loss_target, m_norm_mix_pre, m_w_in, m_w_attn_up, m_ssm_a_re, m_ssm_a_im, m_ssm_log_dt, m_ssm_b_re, m_ssm_b_im, m_ssm_c_re, m_ssm_c_im, m_ssm_d, m_w_glu_v, m_w_glu_g, m_w_out, m_norm_mix_post, m_norm_ffn_pre, m_w_ffn_gate, m_w_ffn_up, m_w_ffn_down, m_norm_ffn_post, v_norm_mix_pre, v_w_in, v_w_attn_up, v_ssm_a_re, v_ssm_a_im, v_ssm_log_dt, v_ssm_b_re, v_ssm_b_im, v_ssm_c_re, v_ssm_c_im, v_ssm_d, v_w_glu_v, v_w_glu_g, v_w_out, v_norm_mix_post, v_norm_ffn_pre, v_w_ffn_gate, v_w_ffn_up, v_w_ffn_down, v_norm_ffn_post):
    given = dict(x=x, norm_mix_pre=norm_mix_pre, w_in=w_in, w_attn_up=w_attn_up, ssm_a_re=ssm_a_re, ssm_a_im=ssm_a_im, ssm_log_dt=ssm_log_dt, ssm_b_re=ssm_b_re, ssm_b_im=ssm_b_im, ssm_c_re=ssm_c_re, ssm_c_im=ssm_c_im, ssm_d=ssm_d, w_glu_v=w_glu_v, w_glu_g=w_glu_g, w_out=w_out, norm_mix_post=norm_mix_post, norm_ffn_pre=norm_ffn_pre, w_ffn_gate=w_ffn_gate, w_ffn_up=w_ffn_up, w_ffn_down=w_ffn_down, norm_ffn_post=norm_ffn_post, loss_target=loss_target, m_norm_mix_pre=m_norm_mix_pre, m_w_in=m_w_in, m_w_attn_up=m_w_attn_up, m_ssm_a_re=m_ssm_a_re, m_ssm_a_im=m_ssm_a_im, m_ssm_log_dt=m_ssm_log_dt, m_ssm_b_re=m_ssm_b_re, m_ssm_b_im=m_ssm_b_im, m_ssm_c_re=m_ssm_c_re, m_ssm_c_im=m_ssm_c_im, m_ssm_d=m_ssm_d, m_w_glu_v=m_w_glu_v, m_w_glu_g=m_w_glu_g, m_w_out=m_w_out, m_norm_mix_post=m_norm_mix_post, m_norm_ffn_pre=m_norm_ffn_pre, m_w_ffn_gate=m_w_ffn_gate, m_w_ffn_up=m_w_ffn_up, m_w_ffn_down=m_w_ffn_down, m_norm_ffn_post=m_norm_ffn_post, v_norm_mix_pre=v_norm_mix_pre, v_w_in=v_w_in, v_w_attn_up=v_w_attn_up, v_ssm_a_re=v_ssm_a_re, v_ssm_a_im=v_ssm_a_im, v_ssm_log_dt=v_ssm_log_dt, v_ssm_b_re=v_ssm_b_re, v_ssm_b_im=v_ssm_b_im, v_ssm_c_re=v_ssm_c_re, v_ssm_c_im=v_ssm_c_im, v_ssm_d=v_ssm_d, v_w_glu_v=v_w_glu_v, v_w_glu_g=v_w_glu_g, v_w_out=v_w_out, v_norm_mix_post=v_norm_mix_post, v_norm_ffn_pre=v_norm_ffn_pre, v_w_ffn_gate=v_w_ffn_gate, v_w_ffn_up=v_w_ffn_up, v_w_ffn_down=v_w_ffn_down, v_norm_ffn_post=v_norm_ffn_post)
    weights = {n: given[n] for n in TWIN_WEIGHTS}
    shared = {n: given[n] for n in SHARED_INPUTS}
    per_example = {n: given[n] for n in ['x']}
    grad_fn = _jax.value_and_grad(_loss, argnums=(0, 1))

    def one_microbatch(ex, loss_target):
        ex = dict(ex)
        diff = ex.pop(TWIN_DIFF_INPUT)
        return grad_fn(weights, diff, {**shared, **ex}, loss_target)

    if N_MICROBATCH == 1:
        loss, (grad_w, grad_x) = one_microbatch(per_example, given["loss_target"])
    else:
        def body(carry, xs):
            loss_sum, grad_sum = carry
            l_k, (gw_k, gx_k) = one_microbatch(xs[0], xs[1])
            with _jax.named_scope("update"):
                return (loss_sum + l_k, _jax.tree.map(_jnp.add, grad_sum, gw_k)), gx_k

        init = (_jnp.zeros((), _jnp.float32), _jax.tree.map(_jnp.zeros_like, weights))
        (loss, grad_w), grad_x = _jax.lax.scan(body, init, (per_example, given["loss_target"]))
    with _jax.named_scope("update"):
        delta_w, new_m, new_v = {}, {}, {}
        for n in TWIN_WEIGHTS:
            delta_w[n], new_m[n], new_v[n] = _adamw(weights[n], grad_w[n], given["m_" + n], given["v_" + n])
    return (loss, grad_x, *[grad_w[n] for n in TWIN_WEIGHTS], *[delta_w[n] for n in TWIN_WEIGHTS],
            *[new_m[n] for n in TWIN_WEIGHTS], *[new_v[n] for n in TWIN_WEIGHTS])
```

```python
import functools
import math

import jax
import jax.numpy as jnp
from jax import lax
from jax.experimental import pallas as pl
from jax.experimental.pallas import tpu as pltpu

F32 = jnp.float32
BF16 = jnp.bfloat16

EPS = 1e-6
HEAD_DIM = 128
HEADS_PER_GROUP = 4
ATTN_DILATIONS = (1, 4, 16)
ATTN_BLK = 128
N_ATTN_HEADS = HEADS_PER_GROUP * len(ATTN_DILATIONS)
GROUP_W = HEADS_PER_GROUP * HEAD_DIM
HQ = N_ATTN_HEADS * HEAD_DIM
SSM_GROUP = 16
SSM_STATE = 64
SSM_TILE_CH = 128
SSM_TILE_ST = SSM_TILE_CH // SSM_GROUP * SSM_STATE
ADAM_LR = 0.001
ADAM_B1 = 0.9
ADAM_B2 = 0.999
ADAM_EPS = 1e-08
ADAM_WD = 0.01
ADAM_STEP = 10
NEG_BIG = -1e30
V7X_VMEM_LIMIT = 56 * 1024 * 1024
MESH_AXES = ("x", "y", "c")
N_CHIPS = 4


def _pick(n, cands):
    for c in cands:
        if n % c == 0:
            return c
    raise ValueError(f"no tile of {cands} divides {n}")


def _params(sem):
    return pltpu.CompilerParams(dimension_semantics=sem, vmem_limit_bytes=V7X_VMEM_LIMIT)


_DOT_DIMS = {"nn": (((1,), (0,)), ((), ())), "nt": (((1,), (1,)), ((), ())), "tn": (((0,), (0,)), ((), ()))}


def _mm(a, b, mode, out_dtype, name, add=None):
    if mode == "nn":
        (M, K), (_, N) = a.shape, b.shape
    elif mode == "nt":
        (M, K), (N, _) = a.shape, b.shape
    else:
        (K, M), (_, N) = a.shape, b.shape
    tm = _pick(M, (1024, 512, 256, 128))
    tn = _pick(N, (1024, 512, 256, 128))
    tk = _pick(K, (512, 256, 128))
    nk = K // tk
    dims = _DOT_DIMS[mode]

    def body(*refs):
        if add is None:
            a_ref, b_ref, o_ref, acc_ref = refs
        else:
            a_ref, b_ref, add_ref, o_ref, acc_ref = refs
        k = pl.program_id(2)

        @pl.when(k == 0)
        def _():
            acc_ref[...] = jnp.zeros_like(acc_ref)

        acc_ref[...] += lax.dot_general(a_ref[...], b_ref[...], dims, preferred_element_type=F32)

        @pl.when(k == nk - 1)
        def _():
            r = acc_ref[...]
            if add is not None:
                r = r + add_ref[...].astype(F32)
            o_ref[...] = r.astype(o_ref.dtype)

    a_spec = pl.BlockSpec((tk, tm), lambda i, j, k: (k, i)) if mode == "tn" else pl.BlockSpec((tm, tk), lambda i, j, k: (i, k))
    b_spec = pl.BlockSpec((tn, tk), lambda i, j, k: (j, k)) if mode == "nt" else pl.BlockSpec((tk, tn), lambda i, j, k: (k, j))
    o_spec = pl.BlockSpec((tm, tn), lambda i, j, k: (i, j))
    in_specs = [a_spec, b_spec] + ([o_spec] if add is not None else [])
    args = (a, b) + ((add,) if add is not None else ())
    return pl.pallas_call(
        body, name=name, grid=(M // tm, N // tn, nk), in_specs=in_specs, out_specs=o_spec,
        out_shape=jax.ShapeDtypeStruct((M, N), out_dtype), scratch_shapes=[pltpu.VMEM((tm, tn), F32)],
        compiler_params=_params(("parallel", "parallel", "arbitrary")),
    )(*args)


def _sigmoid(v):
    return 1.0 / (1.0 + jnp.exp(-v))


_GELU_C = math.sqrt(2.0 / math.pi)


def _gelu(v):
    return 0.5 * v * (1.0 + jnp.tanh(_GELU_C * (v + 0.044715 * v * v * v)))


def _gelu_grad(v):
    t = jnp.tanh(_GELU_C * (v + 0.044715 * v * v * v))
    return 0.5 * (1.0 + t) + 0.5 * v * (1.0 - t * t) * _GELU_C * (1.0 + 3.0 * 0.044715 * v * v)


def _rms(v, gain):
    r = lax.rsqrt(jnp.mean(v * v, axis=-1, keepdims=True) + EPS)
    return v * r * gain


def _rms_bwd(v, gain, dy):
    r = lax.rsqrt(jnp.mean(v * v, axis=-1, keepdims=True) + EPS)
    a = dy * gain
    dv = r * a - v * (r * r * r) * jnp.mean(a * v, axis=-1, keepdims=True)
    return dv, dy * v * r


def _row_tile(s):
    return _pick(s, (256, 128, 64, 8))


def _norm_in(x, gain):
    s, d = x.shape
    tr = _row_tile(s)

    def body(x_ref, g_ref, h_ref):
        h_ref[...] = _rms(x_ref[...], g_ref[...]).astype(BF16)

    row = pl.BlockSpec((tr, d), lambda i: (i, 0))
    vec = pl.BlockSpec((1, d), lambda i: (0, 0))
    return pl.pallas_call(body, name="norm_in", grid=(s // tr,), in_specs=[row, vec], out_specs=row,
                          out_shape=jax.ShapeDtypeStruct((s, d), BF16), compiler_params=_params(("parallel",)))(x, gain)


def _norm_mid(x, mo, g_post, g_pre):
    s, d = x.shape
    tr = _row_tile(s)

    def body(x_ref, mo_ref, g2_ref, g3_ref, x2_ref, h2_ref):
        x2 = x_ref[...] + _rms(mo_ref[...], g2_ref[...])
        x2_ref[...] = x2
        h2_ref[...] = _rms(x2, g3_ref[...]).astype(BF16)

    row = pl.BlockSpec((tr, d), lambda i: (i, 0))
    vec = pl.BlockSpec((1, d), lambda i: (0, 0))
    return pl.pallas_call(
        body, name="norm_mid", grid=(s // tr,), in_specs=[row, row, vec, vec], out_specs=[row, row],
        out_shape=[jax.ShapeDtypeStruct((s, d), F32), jax.ShapeDtypeStruct((s, d), BF16)],
        compiler_params=_params(("parallel",)))(x, mo, g_post, g_pre)


def _loss_head(x2, f, g_post, target):
    s, d = x2.shape
    tr = _row_tile(s)

    def body(x2_ref, f_ref, g_ref, t_ref, loss_ref, dout_ref, df_ref, dg_ref):
        @pl.when(pl.program_id(0) == 0)
        def _():
            loss_ref[...] = jnp.zeros_like(loss_ref)
            dg_ref[...] = jnp.zeros_like(dg_ref)

        fv = f_ref[...]
        g = g_ref[...]
        err = x2_ref[...] + _rms(fv, g) - t_ref[...]
        loss_ref[...] += 0.5 * jnp.sum(jnp.mean(err * err, axis=-1, keepdims=True), axis=0, keepdims=True)
        dout = err * (1.0 / d)
        dout_ref[...] = dout
        df, dg = _rms_bwd(fv, g, dout)
        df_ref[...] = df.astype(BF16)
        dg_ref[...] += jnp.sum(dg, axis=0, keepdims=True)

    row = pl.BlockSpec((tr, d), lambda i: (i, 0))
    vec = pl.BlockSpec((1, d), lambda i: (0, 0))
    one = pl.BlockSpec((1, 1), lambda i: (0, 0))
    return pl.pallas_call(
        body, name="loss_head", grid=(s // tr,), in_specs=[row, row, vec, row], out_specs=[one, row, row, vec],
        out_shape=[jax.ShapeDtypeStruct((1, 1), F32), jax.ShapeDtypeStruct((s, d), F32),
                   jax.ShapeDtypeStruct((s, d), BF16), jax.ShapeDtypeStruct((1, d), F32)],
        compiler_params=_params(("arbitrary",)))(x2, f, g_post, target)


def _norm_mid_bwd(x2, mo, g_post, g_pre, dout, dh2):
    s, d = x2.shape
    tr = _row_tile(s)

    def body(x2_ref, mo_ref, g2_ref, g3_ref, dout_ref, dh2_ref, dx2_ref, dmo_ref, dg2_ref, dg3_ref):
        @pl.when(pl.program_id(0) == 0)
        def _():
            dg2_ref[...] = jnp.zeros_like(dg2_ref)
            dg3_ref[...] = jnp.zeros_like(dg3_ref)

        dv, dg3 = _rms_bwd(x2_ref[...], g3_ref[...], dh2_ref[...])
        dx2 = dout_ref[...] + dv
        dx2_ref[...] = dx2
        dmo, dg2 = _rms_bwd(mo_ref[...], g2_ref[...], dx2)
        dmo_ref[...] = dmo.astype(BF16)
        dg2_ref[...] += jnp.sum(dg2, axis=0, keepdims=True)
        dg3_ref[...] += jnp.sum(dg3, axis=0, keepdims=True)

    row = pl.BlockSpec((tr, d), lambda i: (i, 0))
    vec = pl.BlockSpec((1, d), lambda i: (0, 0))
    return pl.pallas_call(
        body, name="norm_mid_bwd", grid=(s // tr,), in_specs=[row, row, vec, vec, row, row],
        out_specs=[row, row, vec, vec],
        out_shape=[jax.ShapeDtypeStruct((s, d), F32), jax.ShapeDtypeStruct((s, d), BF16),
                   jax.ShapeDtypeStruct((1, d), F32), jax.ShapeDtypeStruct((1, d), F32)],
        compiler_params=_params(("arbitrary",)))(x2, mo, g_post, g_pre, dout, dh2)


def _norm_in_bwd(x, gain, dh, dx2):
    s, d = x.shape
    tr = _row_tile(s)

    def body(x_ref, g_ref, dh_ref, dx2_ref, dx_ref, dg_ref):
        @pl.when(pl.program_id(0) == 0)
        def _():
            dg_ref[...] = jnp.zeros_like(dg_ref)

        dv, dg = _rms_bwd(x_ref[...], g_ref[...], dh_ref[...])
        dx_ref[...] = dx2_ref[...] + dv
        dg_ref[...] += jnp.sum(dg, axis=0, keepdims=True)

    row = pl.BlockSpec((tr, d), lambda i: (i, 0))
    vec = pl.BlockSpec((1, d), lambda i: (0, 0))
    return pl.pallas_call(
        body, name="norm_in_bwd", grid=(s // tr,), in_specs=[row, vec, row, row], out_specs=[row, vec],
        out_shape=[jax.ShapeDtypeStruct((s, d), F32), jax.ShapeDtypeStruct((1, d), F32)],
        compiler_params=_params(("arbitrary",)))(x, gain, dh, dx2)


def _swiglu(fg, fu):
    s, n = fg.shape
    tr, tc = _row_tile(s), _pick(n, (1024, 512, 256, 128))

    def body(g_ref, u_ref, o_ref):
        g = g_ref[...]
        o_ref[...] = (g * _sigmoid(g) * u_ref[...]).astype(BF16)

    blk = pl.BlockSpec((tr, tc), lambda i, j: (i, j))
    return pl.pallas_call(body, name="swiglu", grid=(s // tr, n // tc), in_specs=[blk, blk], out_specs=blk,
                          out_shape=jax.ShapeDtypeStruct((s, n), BF16),
                          compiler_params=_params(("parallel", "parallel")))(fg, fu)


def _swiglu_bwd(fg, fu, dact):
    s, n = fg.shape
    tr, tc = _row_tile(s), _pick(n, (1024, 512, 256, 128))

    def body(g_ref, u_ref, d_ref, dg_ref, du_ref):
        g = g_ref[...]
        sg = _sigmoid(g)
        d = d_ref[...]
        du_ref[...] = (d * g * sg).astype(BF16)
        dg_ref[...] = (d * u_ref[...] * sg * (1.0 + g * (1.0 - sg))).astype(BF16)

    blk = pl.BlockSpec((tr, tc), lambda i, j: (i, j))
    return pl.pallas_call(body, name="swiglu_bwd", grid=(s // tr, n // tc), in_specs=[blk, blk, blk],
                          out_specs=[blk, blk],
                          out_shape=[jax.ShapeDtypeStruct((s, n), BF16), jax.ShapeDtypeStruct((s, n), BF16)],
                          compiler_params=_params(("parallel", "parallel")))(fg, fu, dact)


def _gate_cols(d, gate_off):
    tc = _pick(math.gcd(d, gate_off), (512, 256, 128))
    return tc, gate_off // tc, d // tc


def _merge_gates(z, ab, gv, gg, gate_off):
    s, d = ab.shape
    tr = _row_tile(s)
    tc, off, nd = _gate_cols(d, gate_off)

    def body(ga_ref, gs_ref, ab_ref, gv_ref, gg_ref, o_ref):
        sb = gv_ref[...] * _sigmoid(gg_ref[...])
        o_ref[...] = (_sigmoid(ga_ref[...]) * ab_ref[...] + _sigmoid(gs_ref[...]) * sb).astype(BF16)

    blk = pl.BlockSpec((tr, tc), lambda i, j: (i, j))
    ga = pl.BlockSpec((tr, tc), lambda i, j: (i, off + j))
    gs = pl.BlockSpec((tr, tc), lambda i, j: (i, off + nd + j))
    return pl.pallas_call(body, name="merge_gates", grid=(s // tr, nd), in_specs=[ga, gs, blk, blk, blk],
                          out_specs=blk, out_shape=jax.ShapeDtypeStruct((s, d), BF16),
                          compiler_params=_params(("parallel", "parallel")))(z, z, ab, gv, gg)


def _merge_gate_a_bwd(z, ab, dmerged, gate_off):
    s, d = ab.shape
    tr = _row_tile(s)
    tc, off, nd = _gate_cols(d, gate_off)

    def body(ga_ref, ab_ref, dm_ref, dga_ref, dab_ref):
        dm = dm_ref[...]
        sa = _sigmoid(ga_ref[...])
        dga_ref[...] = (dm * ab_ref[...] * sa * (1.0 - sa)).astype(BF16)
        dab_ref[...] = (dm * sa).astype(BF16)

    blk = pl.BlockSpec((tr, tc), lambda i, j: (i, j))
    ga = pl.BlockSpec((tr, tc), lambda i, j: (i, off + j))
    return pl.pallas_call(
        body, name="merge_gate_a_bwd", grid=(s // tr, nd), in_specs=[ga, blk, blk], out_specs=[ga, blk],
        out_shape=[jax.ShapeDtypeStruct(z.shape, BF16), jax.ShapeDtypeStruct((s, d), BF16)],
        compiler_params=_params(("parallel", "parallel")))(z, ab, dmerged)


def _merge_gate_s_bwd(z, gv, gg, dmerged, dz, gate_off):
    s, d = gv.shape
    tr = _row_tile(s)
    tc, off, nd = _gate_cols(d, gate_off)

    def body(gs_ref, gv_ref, gg_ref, dm_ref, dz_ref, dgs_ref, dgv_ref, dgg_ref):
        del dz_ref
        ss = _sigmoid(gs_ref[...])
        sg = _sigmoid(gg_ref[...])
        gv_ = gv_ref[...]
        dm = dm_ref[...]
        dgs_ref[...] = (dm * gv_ * sg * ss * (1.0 - ss)).astype(BF16)
        dsb = dm * ss
        dgv_ref[...] = (dsb * sg).astype(BF16)
        dgg_ref[...] = (dsb * gv_ * sg * (1.0 - sg)).astype(BF16)

    blk = pl.BlockSpec((tr, tc), lambda i, j: (i, j))
    gs = pl.BlockSpec((tr, tc), lambda i, j: (i, off + nd + j))
    return pl.pallas_call(
        body, name="merge_gate_s_bwd", grid=(s // tr, nd),
        in_specs=[gs, blk, blk, blk, pl.BlockSpec(memory_space=pl.ANY)], out_specs=[gs, blk, blk],
        out_shape=[jax.ShapeDtypeStruct(z.shape, BF16)] + [jax.ShapeDtypeStruct((s, d), BF16)] * 2,
        input_output_aliases={4: 0},
        compiler_params=_params(("parallel", "parallel")))(z, gv, gg, dmerged, dz)


def _put_cols(dz, src, col_off):
    s, w = src.shape
    tr = _row_tile(s)
    tc = _pick(math.gcd(w, col_off), (512, 256, 128))
    off = col_off // tc

    def body(src_ref, dz_ref, o_ref):
        del dz_ref
        o_ref[...] = src_ref[...].astype(o_ref.dtype)

    return pl.pallas_call(
        body, name="put_cols", grid=(s // tr, w // tc),
        in_specs=[pl.BlockSpec((tr, tc), lambda i, j: (i, j)), pl.BlockSpec(memory_space=pl.ANY)],
        out_specs=pl.BlockSpec((tr, tc), lambda i, j: (i, off + j)),
        out_shape=jax.ShapeDtypeStruct(dz.shape, dz.dtype), input_output_aliases={1: 0},
        compiler_params=_params(("parallel", "parallel")))(src, dz)


ATTN_ROWS = 2048


def _dilate_qkv(z, g, d):
    s = z.shape[0]
    tm = ATTN_ROWS
    per = tm // d
    nh = HEADS_PER_GROUP

    def body(z_ref, o_ref):
        for r in range(d):
            rows = z_ref[...] if d == 1 else z_ref[pl.ds(r, per, stride=d), :]
            o_ref[0, r] = rows.astype(BF16)

    return pl.pallas_call(
        body, name=f"dilate_qkv_{g}", grid=(s // tm, 3, nh),
        in_specs=[pl.BlockSpec((tm, HEAD_DIM), lambda i, w, h: (i, (3 * w + g) * nh + h))],
        out_specs=pl.BlockSpec((1, d, per, HEAD_DIM), lambda i, w, h: (w, 0, i, h)),
        out_shape=jax.ShapeDtypeStruct((3, d, s // d, GROUP_W), BF16),
        compiler_params=_params(("parallel", "parallel", "parallel")))(z)


def _undilate_dqkv(dqkv, dz, g, d):
    s = dz.shape[0]
    tm = ATTN_ROWS
    per = tm // d
    nh = HEADS_PER_GROUP

    def body(i_ref, dz_ref, o_ref, nat_ref):
        del dz_ref
        if d == 1:
            o_ref[...] = i_ref[0, 0]
        else:
            for r in range(d):
                nat_ref[pl.ds(r, per, stride=d), :] = i_ref[0, r].astype(F32)
            o_ref[...] = nat_ref[...].astype(BF16)

    return pl.pallas_call(
        body, name=f"undilate_dqkv_{g}", grid=(s // tm, 3, nh),
        in_specs=[pl.BlockSpec((1, d, per, HEAD_DIM), lambda i, w, h: (w, 0, i, h)),
                  pl.BlockSpec(memory_space=pl.ANY)],
        out_specs=pl.BlockSpec((tm, HEAD_DIM), lambda i, w, h: (i, (3 * w + g) * nh + h)),
        out_shape=jax.ShapeDtypeStruct(dz.shape, dz.dtype), input_output_aliases={1: 0},
        scratch_shapes=[pltpu.VMEM((tm, HEAD_DIM), F32)],
        compiler_params=_params(("parallel", "parallel", "parallel")))(dqkv, dz)


def _alibi_slope(head):
    return 2.0 ** (-8.0 * (head + 1) / N_ATTN_HEADS)


def _band_masks(n, nb):
    qi = lax.broadcasted_iota(jnp.int32, (ATTN_BLK, ATTN_BLK), 0)
    ki = lax.broadcasted_iota(jnp.int32, (ATTN_BLK, ATTN_BLK), 1)
    dist_cur = qi - ki
    dist_prev = ATTN_BLK + qi - ki
    return dist_cur.astype(F32), dist_cur >= 0, dist_prev.astype(F32), dist_prev <= ATTN_BLK


def _dot_nt(a, b):
    return lax.dot_general(a, b, _DOT_DIMS["nt"], preferred_element_type=F32)


def _dot_tn(a, b):
    return lax.dot_general(a, b, _DOT_DIMS["tn"], preferred_element_type=F32)


def _dot(a, b):
    return jnp.dot(a, b, preferred_element_type=F32)


def _scores(q, k, slope_d, dist, valid):
    s = _dot_nt(q, k) * (HEAD_DIM ** -0.5) - slope_d * dist
    return jnp.where(valid, s, NEG_BIG)


def _attn_fwd(qkv, g, d):
    _, _, L, _ = qkv.shape
    nb = L // ATTN_BLK

    def body(q_ref, kc_ref, kp_ref, vc_ref, vp_ref, o_ref, lse_ref):
        n = pl.program_id(1)
        dist_c, valid_c, dist_p, valid_p = _band_masks(n, nb)
        valid_p = jnp.logical_and(valid_p, n > 0)
        for hh in range(HEADS_PER_GROUP):
            cols = slice(hh * HEAD_DIM, (hh + 1) * HEAD_DIM)
            slope_d = _alibi_slope(g * HEADS_PER_GROUP + hh) * d
            q = q_ref[0, 0, :, cols]
            sc = _scores(q, kc_ref[0, 0, :, cols], slope_d, dist_c, valid_c)
            sp = _scores(q, kp_ref[0, 0, :, cols], slope_d, dist_p, valid_p)
            m = jnp.maximum(jnp.max(sc, axis=-1, keepdims=True), jnp.max(sp, axis=-1, keepdims=True))
            ec = jnp.exp(sc - m)
            ep = jnp.exp(sp - m)
            l = jnp.sum(ec, axis=-1, keepdims=True) + jnp.sum(ep, axis=-1, keepdims=True)
            inv = 1.0 / l
            o = _dot((ec * inv).astype(BF16), vc_ref[0, 0, :, cols]) + _dot((ep * inv).astype(BF16), vp_ref[0, 0, :, cols])
            o_ref[0, :, cols] = o
            lse_ref[0, :, cols] = jnp.broadcast_to(m + jnp.log(l), (ATTN_BLK, HEAD_DIM))

    def spec(w, shift):
        return pl.BlockSpec((1, 1, ATTN_BLK, GROUP_W), lambda r, n: (w, r, jnp.maximum(n + shift, 0), 0))

    out = pl.BlockSpec((1, ATTN_BLK, GROUP_W), lambda r, n: (r, n, 0))
    return pl.pallas_call(
        body, name=f"attn_fwd_{g}", grid=(d, nb),
        in_specs=[spec(0, 0), spec(1, 0), spec(1, -1), spec(2, 0), spec(2, -1)], out_specs=[out, out],
        out_shape=[jax.ShapeDtypeStruct((d, L, GROUP_W), F32)] * 2,
        compiler_params=_params(("parallel", "parallel")))(qkv, qkv, qkv, qkv, qkv)


def _attn_bwd(qkv, do, lse, cc, g, d):
    _, _, L, _ = qkv.shape
    nb = L // ATTN_BLK
    scale = HEAD_DIM ** -0.5

    def body(q0_ref, q1_ref, k0_ref, kp_ref, v0_ref, vp_ref, do0_ref, do1_ref, l0_ref, l1_ref, c0_ref, c1_ref, o_ref):
        n = pl.program_id(1)
        dist_c, valid_c, dist_p, valid_p = _band_masks(n, nb)
        valid_b = jnp.logical_and(valid_p, n > 0)
        valid_n = jnp.logical_and(valid_p, n < nb - 1)
        for hh in range(HEADS_PER_GROUP):
            cols = slice(hh * HEAD_DIM, (hh + 1) * HEAD_DIM)
            slope_d = _alibi_slope(g * HEADS_PER_GROUP + hh) * d
            q0, q1 = q0_ref[0, 0, :, cols], q1_ref[0, 0, :, cols]
            k0, kp = k0_ref[0, 0, :, cols], kp_ref[0, 0, :, cols]
            v0, vp = v0_ref[0, 0, :, cols], vp_ref[0, 0, :, cols]
            do0, do1 = do0_ref[0, :, cols], do1_ref[0, :, cols]
            l0, l1 = l0_ref[0, :, cols], l1_ref[0, :, cols]
            c0, c1 = c0_ref[0, :, cols], c1_ref[0, :, cols]
            pa = jnp.exp(_scores(q0, k0, slope_d, dist_c, valid_c) - l0)
            dsa = (pa * (_dot_nt(do0, v0) + c0)).astype(BF16)
            pb = jnp.exp(_scores(q0, kp, slope_d, dist_p, valid_b) - l0)
            dsb = (pb * (_dot_nt(do0, vp) + c0)).astype(BF16)
            pc = jnp.exp(_scores(q1, k0, slope_d, dist_p, valid_n) - l1)
            dsc = (pc * (_dot_nt(do1, v0) + c1)).astype(BF16)
            o_ref[0, 0, :, cols] = ((_dot(dsa, k0) + _dot(dsb, kp)) * scale).astype(BF16)
            o_ref[1, 0, :, cols] = ((_dot_tn(dsa, q0) + _dot_tn(dsc, q1)) * scale).astype(BF16)
            o_ref[2, 0, :, cols] = (_dot_tn(pa.astype(BF16), do0) + _dot_tn(pc.astype(BF16), do1)).astype(BF16)

    def spec(w, shift):
        return pl.BlockSpec((1, 1, ATTN_BLK, GROUP_W), lambda r, n: (w, r, jnp.clip(n + shift, 0, nb - 1), 0))

    def spec3(shift):
        return pl.BlockSpec((1, ATTN_BLK, GROUP_W), lambda r, n: (r, jnp.clip(n + shift, 0, nb - 1), 0))

    return pl.pallas_call(
        body, name=f"attn_bwd_{g}", grid=(d, nb),
        in_specs=[spec(0, 0), spec(0, 1), spec(1, 0), spec(1, -1), spec(2, 0), spec(2, -1),
                  spec3(0), spec3(1), spec3(0), spec3(1), spec3(0), spec3(1)],
        out_specs=pl.BlockSpec((3, 1, ATTN_BLK, GROUP_W), lambda r, n: (0, r, n, 0)),
        out_shape=jax.ShapeDtypeStruct((3, d, L, GROUP_W), BF16),
        compiler_params=_params(("parallel", "parallel")))(qkv, qkv, qkv, qkv, qkv, qkv, do, do, lse, lse, cc, cc)


def _load_natural(refs, nat_refs):
    for g, d in enumerate(ATTN_DILATIONS):
        if d == 1:
            nat_refs[g][...] = refs[g][0]
        else:
            per = ATTN_ROWS // d
            for r in range(d):
                nat_refs[g][pl.ds(r, per, stride=d), :] = refs[g][r]


def _mix_weights(lse_nat):
    l0, l1, l2 = lse_nat[0][...], lse_nat[1][...], lse_nat[2][...]
    m = jnp.maximum(jnp.maximum(l0, l1), l2)
    e0, e1, e2 = jnp.exp(l0 - m), jnp.exp(l1 - m), jnp.exp(l2 - m)
    inv = 1.0 / (e0 + e1 + e2)
    return e0 * inv, e1 * inv, e2 * inv


def _dilated_specs(s):
    return [pl.BlockSpec((d, ATTN_ROWS // d, HEAD_DIM), lambda i, h: (0, i, h)) for d in ATTN_DILATIONS]


NATURAL_SCRATCH = [pltpu.VMEM((ATTN_ROWS, HEAD_DIM), F32)] * (2 * len(ATTN_DILATIONS))


def _attn_merge(outs, lses):
    s = outs[0].shape[0] * outs[0].shape[1]

    def body(o0, o1, o2, l0, l1, l2, a_ref, *nat):
        onat, lnat = nat[:3], nat[3:]
        _load_natural((o0, o1, o2), onat)
        _load_natural((l0, l1, l2), lnat)
        w0, w1, w2 = _mix_weights(lnat)
        a_ref[...] = (w0 * onat[0][...] + w1 * onat[1][...] + w2 * onat[2][...]).astype(BF16)

    return pl.pallas_call(
        body, name="attn_merge", grid=(s // ATTN_ROWS, HEADS_PER_GROUP), in_specs=_dilated_specs(s) * 2,
        out_specs=pl.BlockSpec((ATTN_ROWS, HEAD_DIM), lambda i, h: (i, h)),
        out_shape=jax.ShapeDtypeStruct((s, GROUP_W), BF16), scratch_shapes=NATURAL_SCRATCH,
        compiler_params=_params(("parallel", "parallel")))(*outs, *lses)


def _attn_merge_bwd(outs, lses, dattn):
    s = dattn.shape[0]

    def body(o0, o1, o2, l0, l1, l2, da_ref, do0, do1, do2, c0, c1, c2, *nat):
        onat, lnat = nat[:3], nat[3:]
        _load_natural((o0, o1, o2), onat)
        _load_natural((l0, l1, l2), lnat)
        ws = _mix_weights(lnat)
        da = da_ref[...]
        attn = ws[0] * onat[0][...] + ws[1] * onat[1][...] + ws[2] * onat[2][...]
        tot = jnp.broadcast_to(jnp.sum(da * attn, axis=-1, keepdims=True), (ATTN_ROWS, HEAD_DIM))
        for g, (d, do_ref, c_ref) in enumerate(zip(ATTN_DILATIONS, (do0, do1, do2), (c0, c1, c2))):
            if d == 1:
                do_ref[0] = (ws[g] * da).astype(BF16)
                c_ref[0] = -ws[g] * tot
            else:
                onat[g][...] = ws[g] * da
                lnat[g][...] = -ws[g] * tot
                per = ATTN_ROWS // d
                for r in range(d):
                    do_ref[r] = onat[g][pl.ds(r, per, stride=d), :].astype(BF16)
                    c_ref[r] = lnat[g][pl.ds(r, per, stride=d), :]

    dil = _dilated_specs(s)
    shapes = [jax.ShapeDtypeStruct(o.shape, BF16) for o in outs] + [jax.ShapeDtypeStruct(o.shape, F32) for o in outs]
    return pl.pallas_call(
        body, name="attn_merge_bwd", grid=(s // ATTN_ROWS, HEADS_PER_GROUP),
        in_specs=dil * 2 + [pl.BlockSpec((ATTN_ROWS, HEAD_DIM), lambda i, h: (i, h))], out_specs=dil * 2,
        out_shape=shapes, scratch_shapes=NATURAL_SCRATCH,
        compiler_params=_params(("parallel", "parallel")))(*outs, *lses, dattn)


def _ssm_prepare(a_re, a_im, log_dt, b_re, b_im, c_re, c_im):
    n_g = a_re.shape[0]
    nj = n_g * SSM_GROUP // SSM_TILE_CH
    gpt = SSM_TILE_CH // SSM_GROUP
    dt = jnp.exp(log_dt)[:, None]
    mag = jnp.exp(a_re * dt)
    lr, li = mag * jnp.cos(a_im * dt), mag * jnp.sin(a_im * dt)
    den = a_re * a_re + a_im * a_im
    cr = ((lr - 1.0) * a_re + li * a_im) / den
    ci = (li * a_re - (lr - 1.0) * a_im) / den
    bb_re = cr[..., None] * b_re - ci[..., None] * b_im
    bb_im = cr[..., None] * b_im + ci[..., None] * b_re
    eye = jnp.eye(gpt, dtype=F32)

    def b_tiles(t):
        t = t.transpose(0, 2, 1).reshape(nj, gpt, SSM_GROUP, SSM_STATE)
        return jnp.einsum("jgcp,gh->jgchp", t, eye).reshape(nj, SSM_TILE_CH, SSM_TILE_ST)

    def c_tiles(t):
        t = t.reshape(nj, gpt, SSM_GROUP, SSM_STATE)
        return jnp.einsum("jgcp,gh->jhpgc", t, eye).reshape(nj, SSM_TILE_ST, SSM_TILE_CH)

    lam = jnp.stack([lr.reshape(-1), li.reshape(-1)])
    bmat = jnp.concatenate([b_tiles(bb_re), b_tiles(bb_im)], axis=2)
    cmat = jnp.concatenate([c_tiles(c_re), -c_tiles(c_im)], axis=1)
    return lam, bmat, cmat


def _scan_chunk(xr, xi, lr, li, cr, ci, row):
    t_rows = xr.shape[0]
    first = row == 0
    xr = xr + jnp.where(first, lr * cr - li * ci, 0.0)
    xi = xi + jnp.where(first, lr * ci + li * cr, 0.0)
    pr, pi = lr, li
    shift = 1
    while shift < t_rows:
        keep = row >= shift
        sr = jnp.where(keep, pltpu.roll(xr, shift, 0), 0.0)
        si = jnp.where(keep, pltpu.roll(xi, shift, 0), 0.0)
        xr, xi = xr + pr * sr - pi * si, xi + pr * si + pi * sr
        pr, pi = pr * pr - pi * pi, 2.0 * pr * pi
        shift *= 2
    return xr, xi


def _rscan_chunk(gr, gi, lr, li, cr, ci, row):
    t_rows = gr.shape[0]
    last = row == t_rows - 1
    gr = gr + jnp.where(last, lr * cr + li * ci, 0.0)
    gi = gi + jnp.where(last, lr * ci - li * cr, 0.0)
    pr, pi = lr, li
    shift = 1
    while shift < t_rows:
        keep = row < t_rows - shift
        sr = jnp.where(keep, pltpu.roll(gr, t_rows - shift, 0), 0.0)
        si = jnp.where(keep, pltpu.roll(gi, t_rows - shift, 0), 0.0)
        gr, gi = gr + pr * sr + pi * si, gi + pr * si - pi * sr
        pr, pi = pr * pr - pi * pi, 2.0 * pr * pi
        shift *= 2
    return gr, gi


def _ssm_dims(z, bmat, u_off):
    s = z.shape[0]
    nj = bmat.shape[0]
    w = nj * SSM_TILE_CH
    t_rows = _pick(s, (256, 128))
    ub = _pick(math.gcd(w, u_off), (512, 256, 128))
    return s, nj, w, nj * SSM_TILE_ST, t_rows, ub


def _u_tile(u_refs, ub, j):
    col = j * SSM_TILE_CH
    return u_refs[col // ub][:, col % ub:col % ub + SSM_TILE_CH]


def _ssm_fwd(z, bmat, cmat, lam, dskip, u_off):
    s, nj, w, ns, t_rows, ub = _ssm_dims(z, bmat, u_off)
    nub = w // ub

    def body(*refs):
        u_refs = refs[:nub]
        b_ref, c_ref, lam_ref, d_ref, y_ref, yg_ref, xin_ref, carry_ref = refs[nub:]

        @pl.when(pl.program_id(0) == 0)
        def _():
            carry_ref[...] = jnp.zeros_like(carry_ref)

        xin_ref[0] = carry_ref[...]
        row = lax.broadcasted_iota(jnp.int32, (t_rows, SSM_TILE_ST), 0)
        for j in range(nj):
            st = slice(j * SSM_TILE_ST, (j + 1) * SSM_TILE_ST)
            ch = slice(j * SSM_TILE_CH, (j + 1) * SSM_TILE_CH)
            uj = _u_tile(u_refs, ub, j)
            bu = _dot(uj.astype(BF16), b_ref[j])
            xr, xi = _scan_chunk(bu[:, :SSM_TILE_ST], bu[:, SSM_TILE_ST:], lam_ref[0:1, st], lam_ref[1:2, st],
                                 carry_ref[0:1, st], carry_ref[1:2, st], row)
            carry_ref[0:1, st] = xr[t_rows - 1:t_rows, :]
            carry_ref[1:2, st] = xi[t_rows - 1:t_rows, :]
            xs = jnp.concatenate([xr, xi], axis=1).astype(BF16)
            yj = _dot(xs, c_ref[j]) + d_ref[:, ch] * uj
            y_ref[:, ch] = yj
            yg_ref[:, ch] = _gelu(yj).astype(BF16)

    u_specs = [pl.BlockSpec((t_rows, ub), lambda c, k=k: (c, u_off // ub + k)) for k in range(nub)]
    full3 = lambda shape: pl.BlockSpec(shape, lambda c: (0, 0, 0))
    full2 = lambda shape: pl.BlockSpec(shape, lambda c: (0, 0))
    rows = pl.BlockSpec((t_rows, w), lambda c: (c, 0))
    return pl.pallas_call(
        body, name="ssm_fwd", grid=(s // t_rows,),
        in_specs=u_specs + [full3(bmat.shape), full3(cmat.shape), full2(lam.shape), full2(dskip.shape)],
        out_specs=[rows, rows, pl.BlockSpec((1, 2, ns), lambda c: (c, 0, 0))],
        out_shape=[jax.ShapeDtypeStruct((s, w), F32), jax.ShapeDtypeStruct((s, w), BF16),
                   jax.ShapeDtypeStruct((s // t_rows, 2, ns), F32)],
        scratch_shapes=[pltpu.VMEM((2, ns), F32)],
        compiler_params=_params(("arbitrary",)))(*([z] * nub), bmat, cmat, lam, dskip)


def _ssm_bwd(z, y, dyg, xin, bmat, cmat, lam, dskip, u_off):
    s, nj, w, ns, t_rows, ub = _ssm_dims(z, bmat, u_off)
    nub = w // ub
    nc = s // t_rows

    def body(*refs):
        u_refs = refs[:nub]
        (y_ref, dyg_ref, xin_ref, b_ref, c_ref, lam_ref, d_ref,
         du_ref, db_ref, dc_ref, dlam_ref, dd_ref, carry_ref) = refs[nub:]

        @pl.when(pl.program_id(0) == 0)
        def _():
            carry_ref[...] = jnp.zeros_like(carry_ref)
            db_ref[...] = jnp.zeros_like(db_ref)
            dc_ref[...] = jnp.zeros_like(dc_ref)
            dlam_ref[...] = jnp.zeros_like(dlam_ref)
            dd_ref[...] = jnp.zeros_like(dd_ref)

        row = lax.broadcasted_iota(jnp.int32, (t_rows, SSM_TILE_ST), 0)
        for j in range(nj):
            st = slice(j * SSM_TILE_ST, (j + 1) * SSM_TILE_ST)
            ch = slice(j * SSM_TILE_CH, (j + 1) * SSM_TILE_CH)
            uj = _u_tile(u_refs, ub, j)
            ujb = uj.astype(BF16)
            lr, li = lam_ref[0:1, st], lam_ref[1:2, st]
            cr, ci = xin_ref[0, 0:1, st], xin_ref[0, 1:2, st]
            bu = _dot(ujb, b_ref[j])
            xr, xi = _scan_chunk(bu[:, :SSM_TILE_ST], bu[:, SSM_TILE_ST:], lr, li, cr, ci, row)
            dyj = dyg_ref[:, ch] * _gelu_grad(y_ref[:, ch])
            dyb = dyj.astype(BF16)
            gin = _dot_nt(dyb, c_ref[j])
            gr, gi = _rscan_chunk(gin[:, :SSM_TILE_ST], gin[:, SSM_TILE_ST:], lr, li,
                                  carry_ref[0:1, st], carry_ref[1:2, st], row)
            carry_ref[0:1, st] = gr[0:1, :]
            carry_ref[1:2, st] = gi[0:1, :]
            first = row == 0
            pxr = jnp.where(first, cr, pltpu.roll(xr, 1, 0))
            pxi = jnp.where(first, ci, pltpu.roll(xi, 1, 0))
            dlam_ref[0:1, st] += jnp.sum(gr * pxr + gi * pxi, axis=0, keepdims=True)
            dlam_ref[1:2, st] += jnp.sum(gi * pxr - gr * pxi, axis=0, keepdims=True)
            gx = jnp.concatenate([gr, gi], axis=1).astype(BF16)
            xs = jnp.concatenate([xr, xi], axis=1).astype(BF16)
            du_ref[:, ch] = (_dot_nt(gx, b_ref[j]) + d_ref[:, ch] * dyj).astype(BF16)
            db_ref[j] += _dot_tn(ujb, gx)
            dc_ref[j] += _dot_tn(xs, dyb)
            dd_ref[:, ch] += jnp.sum(dyj * uj, axis=0, keepdims=True)

    rev = lambda c: nc - 1 - c
    u_specs = [pl.BlockSpec((t_rows, ub), lambda c, k=k: (rev(c), u_off // ub + k)) for k in range(nub)]
    full3 = lambda shape: pl.BlockSpec(shape, lambda c: (0, 0, 0))
    full2 = lambda shape: pl.BlockSpec(shape, lambda c: (0, 0))
    rows = pl.BlockSpec((t_rows, w), lambda c: (rev(c), 0))
    return pl.pallas_call(
        body, name="ssm_bwd", grid=(nc,),
        in_specs=u_specs + [rows, rows, pl.BlockSpec((1, 2, ns), lambda c: (rev(c), 0, 0)),
                            full3(bmat.shape), full3(cmat.shape), full2(lam.shape), full2(dskip.shape)],
        out_specs=[rows, full3(bmat.shape), full3(cmat.shape), full2(lam.shape), full2(dskip.shape)],
        out_shape=[jax.ShapeDtypeStruct((s, w), BF16), jax.ShapeDtypeStruct(bmat.shape, F32),
                   jax.ShapeDtypeStruct(cmat.shape, F32), jax.ShapeDtypeStruct(lam.shape, F32),
                   jax.ShapeDtypeStruct(dskip.shape, F32)],
        scratch_shapes=[pltpu.VMEM((2, ns), F32)],
        compiler_params=_params(("arbitrary",)))(*([z] * nub), y, dyg, xin, bmat, cmat, lam, dskip)


def _adam_math(w, g, m, v):
    m = ADAM_B1 * m + (1.0 - ADAM_B1) * g
    v = ADAM_B2 * v + (1.0 - ADAM_B2) * (g * g)
    m_hat = m / (1.0 - ADAM_B1 ** ADAM_STEP)
    v_hat = v / (1.0 - ADAM_B2 ** ADAM_STEP)
    delta = -ADAM_LR * (m_hat / (jnp.sqrt(v_hat) + ADAM_EPS) + ADAM_WD * w)
    return delta, m, v


def _adam_rows(r, c):
    for tr in (512, 256, 128, 64, 32, 16, 8):
        if r % tr == 0 and tr * c * 4 <= (1 << 20):
            return tr
    return r


def _adamw_big(w, p_mine, p_sib, m, v, name):
    r, c = w.shape
    tr = _adam_rows(r, c)

    def body(w_ref, a_ref, b_ref, m_ref, v_ref, g_ref, d_ref, nm_ref, nv_ref):
        g = a_ref[...] + b_ref[...]
        g_ref[...] = g
        d_ref[...], nm_ref[...], nv_ref[...] = _adam_math(w_ref[...], g, m_ref[...], v_ref[...])

    blk = pl.BlockSpec((tr, c), lambda i: (i, 0))
    return pl.pallas_call(body, name=f"adamw_{name}", grid=(r // tr,), in_specs=[blk] * 5, out_specs=[blk] * 4,
                          out_shape=[jax.ShapeDtypeStruct((r, c), F32)] * 4,
                          compiler_params=_params(("parallel",)))(w, p_mine, p_sib, m, v)


def _adamw_small(w, parts, m, v):
    r, c = w.shape
    n_dev = parts.shape[0]

    def body(w_ref, p_ref, m_ref, v_ref, g_ref, d_ref, nm_ref, nv_ref):
        g = p_ref[0]
        for k in range(1, n_dev):
            g = g + p_ref[k]
        g_ref[...] = g
        d_ref[...], nm_ref[...], nv_ref[...] = _adam_math(w_ref[...], g, m_ref[...], v_ref[...])

    blk = pl.BlockSpec((r, c), lambda i: (0, 0))
    return pl.pallas_call(body, name="adamw_small", grid=(1,),
                          in_specs=[blk, pl.BlockSpec((n_dev, r, c), lambda i: (0, 0, 0)), blk, blk],
                          out_specs=[blk] * 4, out_shape=[jax.ShapeDtypeStruct((r, c), F32)] * 4,
                          compiler_params=_params(("arbitrary",)))(w, parts, m, v)


def _cast_bf16(w, name):
    r, c = w.shape
    tr = _adam_rows(r, c)

    def body(w_ref, o_ref):
        o_ref[...] = w_ref[...].astype(BF16)

    blk = pl.BlockSpec((tr, c), lambda i: (i, 0))
    return pl.pallas_call(body, name=f"cast_{name}", grid=(r // tr,), in_specs=[blk], out_specs=blk,
                          out_shape=jax.ShapeDtypeStruct((r, c), BF16), compiler_params=_params(("parallel",)))(w)


def _sum_slots(recv, name):
    _, r, c = recv.shape
    tr = _adam_rows(r, c)

    def body(p_ref, o_ref):
        acc = p_ref[0].astype(F32)
        for k in range(1, N_CHIPS):
            acc = acc + p_ref[k].astype(F32)
        o_ref[...] = acc

    return pl.pallas_call(body, name=f"sum_{name}", grid=(r // tr,),
                          in_specs=[pl.BlockSpec((N_CHIPS, tr, c), lambda i: (0, i, 0))],
                          out_specs=pl.BlockSpec((tr, c), lambda i: (i, 0)),
                          out_shape=jax.ShapeDtypeStruct((r, c), F32), compiler_params=_params(("parallel",)))(recv)


BIG_WEIGHTS = ("w_in", "w_attn_up", "w_glu_v", "w_glu_g", "w_out", "w_ffn_gate", "w_ffn_up", "w_ffn_down")
COL_SHARDED = ("w_in", "w_attn_up", "w_glu_v", "w_glu_g", "w_ffn_gate", "w_ffn_up")
HBM = pl.BlockSpec(memory_space=pl.ANY)
MESH = pl.DeviceIdType.MESH


def _shard_of(ref, name, j, shard_shape):
    r, c = shard_shape
    if name in COL_SHARDED:
        return ref.at[:, pl.ds(pl.multiple_of(j * c, 128), c)]
    return ref.at[pl.ds(pl.multiple_of(j * r, 8), r), :]


def _other_chips():
    x, y = lax.axis_index("x"), lax.axis_index("y")
    return [(1 - x, y), (x, 1 - y), (1 - x, 1 - y)]


def _gather_weights(shards):
    names = list(shards)
    n = len(names)
    full_shapes = []
    for k in names:
        r, c = shards[k].shape
        full_shapes.append((r, c * N_CHIPS) if k in COL_SHARDED else (r * N_CHIPS, c))

    def body(*refs):
        src, dst = refs[:n], refs[n:2 * n]
        send_sems, recv_sems, local_sems = refs[2 * n:]
        x, y, c = lax.axis_index("x"), lax.axis_index("y"), lax.axis_index("c")
        me = 2 * x + y
        locals_, sends, arrivals = [], [], []
        for i, k in enumerate(names):
            shape = shards[k].shape
            own = pltpu.make_async_copy(src[i], _shard_of(dst[i], k, me, shape), local_sems.at[i])
            own.start()
            locals_.append(own)
            for p, (px, py) in enumerate(_other_chips()):
                out = pltpu.make_async_remote_copy(
                    src_ref=src[i], dst_ref=_shard_of(dst[i], k, me, shape), send_sem=send_sems.at[i, p],
                    recv_sem=recv_sems.at[i, p], device_id=(px, py, c), device_id_type=MESH)
                out.start()
                sends.append(out)
                arrivals.append(pltpu.make_async_remote_copy(
                    src_ref=src[i], dst_ref=_shard_of(dst[i], k, 2 * px + py, shape), send_sem=send_sems.at[i, p],
                    recv_sem=recv_sems.at[i, p], device_id=(px, py, c), device_id_type=MESH))
        for a in arrivals:
            a.wait_recv()
        for cp in sends:
            cp.wait_send()
        for cp in locals_:
            cp.wait()

    outs = pl.pallas_call(
        body, name="gather_weights", in_specs=[HBM] * n, out_specs=[HBM] * n,
        out_shape=[jax.ShapeDtypeStruct(s, BF16) for s in full_shapes],
        scratch_shapes=[pltpu.SemaphoreType.DMA((n, 3)), pltpu.SemaphoreType.DMA((n, 3)), pltpu.SemaphoreType.DMA((n,))],
        compiler_params=pltpu.CompilerParams(has_side_effects=True),
    )(*[shards[k] for k in names])
    return dict(zip(names, outs))


def _scatter_grads(grads, shard_shapes):
    names = list(grads)
    n = len(names)

    def body(*refs):
        src, dst = refs[:n], refs[n:2 * n]
        send_sems, recv_sems, local_sems = refs[2 * n:]
        x, y, c = lax.axis_index("x"), lax.axis_index("y"), lax.axis_index("c")
        me = 2 * x + y
        locals_, sends, arrivals = [], [], []
        for i, k in enumerate(names):
            shape = shard_shapes[k]
            own = pltpu.make_async_copy(_shard_of(src[i], k, me, shape), dst[i].at[me], local_sems.at[i])
            own.start()
            locals_.append(own)
            for p, (px, py) in enumerate(_other_chips()):
                peer = 2 * px + py
                out = pltpu.make_async_remote_copy(
                    src_ref=_shard_of(src[i], k, peer, shape), dst_ref=dst[i].at[me], send_sem=send_sems.at[i, p],
                    recv_sem=recv_sems.at[i, p], device_id=(px, py, c), device_id_type=MESH)
                out.start()
                sends.append(out)
                arrivals.append(pltpu.make_async_remote_copy(
                    src_ref=_shard_of(src[i], k, peer, shape), dst_ref=dst[i].at[peer], send_sem=send_sems.at[i, p],
                    recv_sem=recv_sems.at[i, p], device_id=(px, py, c), device_id_type=MESH))
        for a in arrivals:
            a.wait_recv()
        for cp in sends:
            cp.wait_send()
        for cp in locals_:
            cp.wait()

    outs = pl.pallas_call(
        body, name="scatter_grads", in_specs=[HBM] * n, out_specs=[HBM] * n,
        out_shape=[jax.ShapeDtypeStruct((N_CHIPS,) + tuple(shard_shapes[k]), BF16) for k in names],
        scratch_shapes=[pltpu.SemaphoreType.DMA((n, 3)), pltpu.SemaphoreType.DMA((n, 3)), pltpu.SemaphoreType.DMA((n,))],
        compiler_params=pltpu.CompilerParams(has_side_effects=True),
    )(*[grads[k] for k in names])
    return dict(zip(names, outs))


def _swap_with_sibling(parts):
    names = list(parts)
    n = len(names)

    def body(*refs):
        src, dst = refs[:n], refs[n:2 * n]
        send_sems, recv_sems = refs[2 * n:]
        sibling = (lax.axis_index("x"), lax.axis_index("y"), 1 - lax.axis_index("c"))
        copies = []
        for i in range(n):
            cp = pltpu.make_async_remote_copy(src_ref=src[i], dst_ref=dst[i], send_sem=send_sems.at[i],
                                              recv_sem=recv_sems.at[i], device_id=sibling, device_id_type=MESH)
            cp.start()
            copies.append(cp)
        for cp in copies:
            cp.wait_recv()
        for cp in copies:
            cp.wait_send()

    outs = pl.pallas_call(
        body, name="swap_with_sibling", in_specs=[HBM] * n, out_specs=[HBM] * n,
        out_shape=[jax.ShapeDtypeStruct(parts[k].shape, F32) for k in names],
        scratch_shapes=[pltpu.SemaphoreType.DMA((n,)), pltpu.SemaphoreType.DMA((n,))],
        compiler_params=pltpu.CompilerParams(has_side_effects=True),
    )(*[parts[k] for k in names])
    return dict(zip(names, outs))


def _share_small(packed):
    r, c = packed.shape
    n_dev = 8

    def body(src, dst, send_sems, recv_sems, local_sem):
        x, y, cc = lax.axis_index("x"), lax.axis_index("y"), lax.axis_index("c")
        me = 4 * x + 2 * y + cc
        own = pltpu.make_async_copy(src, dst.at[me], local_sem)
        own.start()
        sends, arrivals = [], []
        p = 0
        for fx in range(2):
            for fy in range(2):
                for fc in range(2):
                    if fx == fy == fc == 0:
                        continue
                    px, py, pc = x ^ fx, y ^ fy, cc ^ fc
                    out = pltpu.make_async_remote_copy(src_ref=src, dst_ref=dst.at[me], send_sem=send_sems.at[p],
                                                       recv_sem=recv_sems.at[p], device_id=(px, py, pc), device_id_type=MESH)
                    out.start()
                    sends.append(out)
                    arrivals.append(pltpu.make_async_remote_copy(
                        src_ref=src, dst_ref=dst.at[4 * px + 2 * py + pc], send_sem=send_sems.at[p],
                        recv_sem=recv_sems.at[p], device_id=(px, py, pc), device_id_type=MESH))
                    p += 1
        for a in arrivals:
            a.wait_recv()
        for cp in sends:
            cp.wait_send()
        own.wait()

    return pl.pallas_call(
        body, name="share_small", in_specs=[HBM], out_specs=HBM,
        out_shape=jax.ShapeDtypeStruct((n_dev, r, c), F32),
        scratch_shapes=[pltpu.SemaphoreType.DMA((7,)), pltpu.SemaphoreType.DMA((7,)), pltpu.SemaphoreType.DMA],
        compiler_params=pltpu.CompilerParams(has_side_effects=True),
    )(packed)


SMALL_WEIGHTS = ("norm_mix_pre", "ssm_a_re", "ssm_a_im", "ssm_log_dt", "ssm_b_re", "ssm_b_im", "ssm_c_re", "ssm_c_im",
                 "ssm_d", "norm_mix_post", "norm_ffn_pre", "norm_ffn_post")
WEIGHT_ORDER = ("norm_mix_pre", "w_in", "w_attn_up", "ssm_a_re", "ssm_a_im", "ssm_log_dt", "ssm_b_re", "ssm_b_im",
                "ssm_c_re", "ssm_c_im", "ssm_d", "w_glu_v", "w_glu_g", "w_out", "norm_mix_post", "norm_ffn_pre",
                "w_ffn_gate", "w_ffn_up", "w_ffn_down", "norm_ffn_post")
PACK_LANES = 128
PACK_ROWS = 8


def _pack_small(arrs):
    flat = jnp.concatenate([arrs[k].reshape(-1) for k in SMALL_WEIGHTS])
    pad = -flat.shape[0] % (PACK_LANES * PACK_ROWS)
    return jnp.pad(flat, (0, pad)).reshape(-1, PACK_LANES)


def _unpack_small(packed, like):
    flat = packed.reshape(-1)
    out, pos = {}, 0
    for k in SMALL_WEIGHTS:
        n = like[k].size
        out[k] = flat[pos:pos + n].reshape(like[k].shape)
        pos += n
    return out


def _local_step(x, target, big, small):
    s, d = x.shape
    u_off = 3 * HQ
    gate_off = u_off + d // 2
    g1, g2, g3, g4 = (small[k][0:1] for k in ("norm_mix_pre", "norm_mix_post", "norm_ffn_pre", "norm_ffn_post"))
    ssm_names = ("ssm_a_re", "ssm_a_im", "ssm_log_dt", "ssm_b_re", "ssm_b_im", "ssm_c_re", "ssm_c_im")
    (lam, bmat, cmat), ssm_vjp = jax.vjp(_ssm_prepare, *[small[k][0] for k in ssm_names])
    bmat, cmat = bmat.astype(BF16), cmat.astype(BF16)
    dskip = small["ssm_d"][0:1]

    h1 = _norm_in(x, g1)
    z = _mm(h1, big["w_in"], "nn", F32, "in_proj")
    qkv = [_dilate_qkv(z, g, dil) for g, dil in enumerate(ATTN_DILATIONS)]
    outs, lses = zip(*[_attn_fwd(qkv[g], g, dil) for g, dil in enumerate(ATTN_DILATIONS)])
    attn = _attn_merge(outs, lses)
    ab = _mm(attn, big["w_attn_up"], "nn", F32, "attn_up")
    y, yg, xin = _ssm_fwd(z, bmat, cmat, lam, dskip, u_off)
    gv = _mm(yg, big["w_glu_v"], "nn", F32, "glu_v")
    gg = _mm(yg, big["w_glu_g"], "nn", F32, "glu_g")
    merged = _merge_gates(z, ab, gv, gg, gate_off)
    mo = _mm(merged, big["w_out"], "nn", F32, "mix_out")
    x2, h2 = _norm_mid(x, mo, g2, g3)
    fg = _mm(h2, big["w_ffn_gate"], "nn", F32, "ffn_gate")
    fu = _mm(h2, big["w_ffn_up"], "nn", F32, "ffn_up")
    act = _swiglu(fg, fu)
    f = _mm(act, big["w_ffn_down"], "nn", F32, "ffn_down")
    loss, dout, df, dg4 = _loss_head(x2, f, g4, target)

    grads = {}
    dact = _mm(df, big["w_ffn_down"], "nt", F32, "d_act")
    grads["w_ffn_down"] = _mm(act, df, "tn", BF16, "dw_ffn_down")
    dfg, dfu = _swiglu_bwd(fg, fu, dact)
    dh2 = _mm(dfg, big["w_ffn_gate"], "nt", F32, "d_h2_gate")
    dh2 = _mm(dfu, big["w_ffn_up"], "nt", F32, "d_h2_up", add=dh2)
    grads["w_ffn_gate"] = _mm(h2, dfg, "tn", BF16, "dw_ffn_gate")
    grads["w_ffn_up"] = _mm(h2, dfu, "tn", BF16, "dw_ffn_up")
    dx2, dmo, dg2, dg3 = _norm_mid_bwd(x2, mo, g2, g3, dout, dh2)
    dmerged = _mm(dmo, big["w_out"], "nt", F32, "d_merged")
    grads["w_out"] = _mm(merged, dmo, "tn", BF16, "dw_out")
    dz, dab = _merge_gate_a_bwd(z, ab, dmerged, gate_off)
    dz, dgv, dgg = _merge_gate_s_bwd(z, gv, gg, dmerged, dz, gate_off)
    dyg = _mm(dgv, big["w_glu_v"], "nt", F32, "d_yg_v")
    dyg = _mm(dgg, big["w_glu_g"], "nt", F32, "d_yg_g", add=dyg)
    grads["w_glu_v"] = _mm(yg, dgv, "tn", BF16, "dw_glu_v")
    grads["w_glu_g"] = _mm(yg, dgg, "tn", BF16, "dw_glu_g")
    du, dbmat, dcmat, dlam, dd = _ssm_bwd(z, y, dyg, xin, bmat, cmat, lam, dskip, u_off)
    dz = _put_cols(dz, du, u_off)
    dattn = _mm(dab, big["w_attn_up"], "nt", F32, "d_attn")
    grads["w_attn_up"] = _mm(attn, dab, "tn", BF16, "dw_attn_up")
    merged_bwd = _attn_merge_bwd(outs, lses, dattn)
    for g, dil in enumerate(ATTN_DILATIONS):
        dqkv = _attn_bwd(qkv[g], merged_bwd[g], lses[g], merged_bwd[3 + g], g, dil)
        dz = _undilate_dqkv(dqkv, dz, g, dil)
    dh1 = _mm(dz, big["w_in"], "nt", F32, "d_h1")
    grads["w_in"] = _mm(h1, dz, "tn", BF16, "dw_in")
    grad_x, dg1 = _norm_in_bwd(x, g1, dh1, dx2)

    small_grads = dict(zip(ssm_names, (t[None] for t in ssm_vjp((dlam, dbmat, dcmat)))))
    small_grads.update(norm_mix_pre=dg1, norm_mix_post=dg2, norm_ffn_pre=dg3, norm_ffn_post=dg4, ssm_d=dd)
    return loss[0, 0], grad_x, grads, small_grads


def kernel(x, norm_mix_pre, w_in, w_attn_up, ssm_a_re, ssm_a_im, ssm_log_dt, ssm_b_re, ssm_b_im, ssm_c_re, ssm_c_im, ssm_d, w_glu_v, w_glu_g, w_out, norm_mix_post, norm_ffn_pre, w_ffn_gate, w_ffn_up, w_ffn_down, norm_ffn_post, loss_target, m_norm_mix_pre, m_w_in, m_w_attn_up, m_ssm_a_re, m_ssm_a_im, m_ssm_log_dt, m_ssm_b_re, m_ssm_b_im, m_ssm_c_re, m_ssm_c_im, m_ssm_d, m_w_glu_v, m_w_glu_g, m_w_out, m_norm_mix_post, m_norm_ffn_pre, m_w_ffn_gate, m_w_ffn_up, m_w_ffn_down, m_norm_ffn_post, v_norm_mix_pre, v_w_in, v_w_attn_up, v_ssm_a_re, v_ssm_a_im, v_ssm_log_dt, v_ssm_b_re, v_ssm_b_im, v_ssm_c_re, v_ssm_c_im, v_ssm_d, v_w_glu_v, v_w_glu_g, v_w_out, v_norm_mix_post, v_norm_ffn_pre, v_w_ffn_gate, v_w_ffn_up, v_w_ffn_down, v_norm_ffn_post):
    given = dict(locals())
    w = {k: given[k] for k in WEIGHT_ORDER}
    m = {k: given["m_" + k] for k in WEIGHT_ORDER}
    v = {k: given["v_" + k] for k in WEIGHT_ORDER}

    shards = {k: _cast_bf16(w[k][0], k) for k in BIG_WEIGHTS}
    big = _gather_weights(shards)

    loss, grad_x, grads, small_grads = _local_step(x[0], loss_target[0], big, {k: w[k] for k in SMALL_WEIGHTS})
    loss = lax.psum(loss, MESH_AXES)

    shard_shapes = {k: w[k].shape[1:] for k in BIG_WEIGHTS}
    slots = _scatter_grads(grads, shard_shapes)
    mine = {k: _sum_slots(slots[k], k) for k in BIG_WEIGHTS}
    theirs = _swap_with_sibling(mine)
    out_g, out_d, out_m, out_v = {}, {}, {}, {}
    for k in BIG_WEIGHTS:
        res = _adamw_big(w[k][0], mine[k], theirs[k], m[k][0], v[k][0], k)
        out_g[k], out_d[k], out_m[k], out_v[k] = (t[None] for t in res)

    pick = lambda tree: {k: tree[k] for k in SMALL_WEIGHTS}
    parts = _share_small(_pack_small(small_grads))
    res = _adamw_small(_pack_small(pick(w)), parts, _pack_small(pick(m)), _pack_small(pick(v)))
    for dst, packed in zip((out_g, out_d, out_m, out_v), res):
        dst.update(_unpack_small(packed, pick(w)))

    return (loss, grad_x[None], *[out_g[k] for k in WEIGHT_ORDER], *[out_d[k] for k in WEIGHT_ORDER],
            *[out_m[k] for k in WEIGHT_ORDER], *[out_v[k] for k in WEIGHT_ORDER])
```

```python
import functools
import math

import jax
import jax.numpy as jnp
from jax import lax
from jax.experimental import pallas as pl
from jax.experimental.pallas import tpu as pltpu

F32 = jnp.float32
BF16 = jnp.bfloat16

EPS = 1e-6
HEAD_DIM = 128
HEADS_PER_GROUP = 4
ATTN_DILATIONS = (1, 4, 16)
ATTN_BLK = 128
N_ATTN_HEADS = HEADS_PER_GROUP * len(ATTN_DILATIONS)
GROUP_W = HEADS_PER_GROUP * HEAD_DIM
HQ = N_ATTN_HEADS * HEAD_DIM
SSM_GROUP = 16
SSM_STATE = 64
SSM_TILE_CH = 128
SSM_TILE_ST = SSM_TILE_CH // SSM_GROUP * SSM_STATE
ADAM_LR = 0.001
ADAM_B1 = 0.9
ADAM_B2 = 0.999
ADAM_EPS = 1e-08
ADAM_WD = 0.01
ADAM_STEP = 10
NEG_BIG = -1e30
V7X_VMEM_LIMIT = 56 * 1024 * 1024
MESH_AXES = ("x", "y", "c")
N_CHIPS = 4


def _pick(n, cands):
    for c in cands:
        if n % c == 0:
            return c
    raise ValueError(f"no tile of {cands} divides {n}")


def _params(sem):
    return pltpu.CompilerParams(dimension_semantics=sem, vmem_limit_bytes=V7X_VMEM_LIMIT)


_DOT_DIMS = {"nn": (((1,), (0,)), ((), ())), "nt": (((1,), (1,)), ((), ())), "tn": (((0,), (0,)), ((), ()))}


MM_VMEM_BUDGET = 44 * 1024 * 1024
MM_STEP_BYTES = 1 << 20


def _size(dtype):
    return jnp.dtype(dtype).itemsize


def _mm_fused(as_, bs, pairs, mode, out_dtypes, name, extras=(), epilogue=None):
    M = as_[0].shape[0]
    N = bs[0].shape[1] if mode == "nn" else bs[0].shape[0]
    ks_a = [a.shape[1] for a in as_]
    ks_b = [b.shape[0] if mode == "nn" else b.shape[1] for b in bs]
    if epilogue is None:
        epilogue = lambda rs, es: rs
    offs = [off for _, off in extras]
    best = None
    for tm in (2048, 1024, 512, 256, 128):
        for tn in (2048, 1024, 512, 256, 128):
            if M % tm or N % tn or any(off % tn for off in offs):
                continue
            vmem = (sum(2 * tm * k * 2 for k in ks_a) + sum(2 * k * tn * 2 for k in ks_b)
                    + sum(2 * tm * tn * _size(d) for d in out_dtypes) + sum(2 * tm * tn * _size(e.dtype) for e, _ in extras)
                    + len(pairs) * tm * tn * 4)
            cost = sum(k * N * 2 for k in ks_b) * (M // tm) + (M // tm) * (N // tn) * MM_STEP_BYTES
            if vmem <= MM_VMEM_BUDGET and (best is None or cost < best[0]):
                best = (cost, tm, tn)
    _, tm, tn = best
    na, nb, ne, no = len(as_), len(bs), len(extras), len(out_dtypes)
    dims = _DOT_DIMS[mode]

    def body(*refs):
        a_refs, b_refs = refs[:na], refs[na:na + nb]
        e_refs, o_refs = refs[na + nb:na + nb + ne], refs[na + nb + ne:]
        rs = [lax.dot_general(a_refs[ai][...], b_refs[bi][...], dims, preferred_element_type=F32) for ai, bi in pairs]
        outs = epilogue(rs, [e[...] for e in e_refs])
        for o_ref, o in zip(o_refs, outs):
            o_ref[...] = o.astype(o_ref.dtype)

    a_specs = [pl.BlockSpec((tm, k), lambda i, j: (i, 0)) for k in ks_a]
    if mode == "nn":
        b_specs = [pl.BlockSpec((k, tn), lambda i, j: (0, j)) for k in ks_b]
    else:
        b_specs = [pl.BlockSpec((tn, k), lambda i, j: (j, 0)) for k in ks_b]
    e_specs = [pl.BlockSpec((tm, tn), lambda i, j, o=off // tn: (i, o + j)) for off in offs]
    o_spec = pl.BlockSpec((tm, tn), lambda i, j: (i, j))
    outs = pl.pallas_call(
        body, name=name, grid=(M // tm, N // tn), in_specs=a_specs + b_specs + e_specs, out_specs=[o_spec] * no,
        out_shape=[jax.ShapeDtypeStruct((M, N), d) for d in out_dtypes],
        compiler_params=_params(("parallel", "arbitrary")),
    )(*as_, *bs, *[e for e, _ in extras])
    return outs


def _mm(a, b, mode, out_dtype, name, add=None):
    if add is None:
        return _mm_fused([a], [b], [(0, 0)], mode, [out_dtype], name)[0]
    return _mm_fused([a], [b], [(0, 0)], mode, [out_dtype], name, extras=[(add, 0)],
                     epilogue=lambda rs, es: [rs[0] + es[0].astype(F32)])[0]


def _mm_kloop(a, b, mode, out_dtype, name, add=None):
    if mode == "nn":
        (M, K), (_, N) = a.shape, b.shape
    elif mode == "nt":
        (M, K), (N, _) = a.shape, b.shape
    else:
        (K, M), (_, N) = a.shape, b.shape
    best = None
    for tm in (2816, 2048, 1408, 1024, 512, 256, 128):
        for tn in (2816, 2432, 2048, 1408, 1024, 512, 256, 128):
            for tk in (1024, 512, 256, 128):
                if M % tm or N % tn or K % tk:
                    continue
                vmem = (2 * tm * tn * 4 + 2 * tm * tn * _size(out_dtype) + 2 * tk * (tm + tn) * 2
                        + (2 * tm * tn * 4 if add is not None else 0))
                steps = (M // tm) * (N // tn) * (K // tk)
                cost = K * M * 2 * (N // tn) + K * N * 2 * (M // tm) + steps * MM_STEP_BYTES
                if vmem <= MM_VMEM_BUDGET and (best is None or cost < best[0]):
                    best = (cost, tm, tn, tk)
    _, tm, tn, tk = best
    nk = K // tk
    dims = _DOT_DIMS[mode]

    def body(*refs):
        if add is None:
            a_ref, b_ref, o_ref, acc_ref = refs
        else:
            a_ref, b_ref, add_ref, o_ref, acc_ref = refs
        k = pl.program_id(2)

        @pl.when(k == 0)
        def _():
            acc_ref[...] = jnp.zeros_like(acc_ref)

        acc_ref[...] += lax.dot_general(a_ref[...], b_ref[...], dims, preferred_element_type=F32)

        @pl.when(k == nk - 1)
        def _():
            r = acc_ref[...]
            if add is not None:
                r = r + add_ref[...]
            o_ref[...] = r.astype(o_ref.dtype)

    a_spec = pl.BlockSpec((tk, tm), lambda i, j, k: (k, i)) if mode == "tn" else pl.BlockSpec((tm, tk), lambda i, j, k: (i, k))
    b_spec = pl.BlockSpec((tn, tk), lambda i, j, k: (j, k)) if mode == "nt" else pl.BlockSpec((tk, tn), lambda i, j, k: (k, j))
    o_spec = pl.BlockSpec((tm, tn), lambda i, j, k: (i, j))
    return pl.pallas_call(
        body, name=name, grid=(M // tm, N // tn, nk),
        in_specs=[a_spec, b_spec] + ([o_spec] if add is not None else []), out_specs=o_spec,
        out_shape=jax.ShapeDtypeStruct((M, N), out_dtype), scratch_shapes=[pltpu.VMEM((tm, tn), F32)],
        compiler_params=_params(("parallel", "parallel", "arbitrary")),
    )(*((a, b) + ((add,) if add is not None else ())))


def _sigmoid(v):
    return 1.0 / (1.0 + jnp.exp(-v))


_GELU_C = math.sqrt(2.0 / math.pi)


def _gelu(v):
    return 0.5 * v * (1.0 + jnp.tanh(_GELU_C * (v + 0.044715 * v * v * v)))


def _gelu_grad(v):
    t = jnp.tanh(_GELU_C * (v + 0.044715 * v * v * v))
    return 0.5 * (1.0 + t) + 0.5 * v * (1.0 - t * t) * _GELU_C * (1.0 + 3.0 * 0.044715 * v * v)


def _rms(v, gain):
    r = lax.rsqrt(jnp.mean(v * v, axis=-1, keepdims=True) + EPS)
    return v * r * gain


def _rms_bwd(v, gain, dy):
    r = lax.rsqrt(jnp.mean(v * v, axis=-1, keepdims=True) + EPS)
    a = dy * gain
    dv = r * a - v * (r * r * r) * jnp.mean(a * v, axis=-1, keepdims=True)
    return dv, dy * v * r


def _row_tile(s):
    return _pick(s, (256, 128, 64, 8))


def _norm_in(x, gain):
    s, d = x.shape
    tr = _row_tile(s)

    def body(x_ref, g_ref, h_ref):
        h_ref[...] = _rms(x_ref[...], g_ref[...]).astype(BF16)

    row = pl.BlockSpec((tr, d), lambda i: (i, 0))
    vec = pl.BlockSpec((1, d), lambda i: (0, 0))
    return pl.pallas_call(body, name="norm_in", grid=(s // tr,), in_specs=[row, vec], out_specs=row,
                          out_shape=jax.ShapeDtypeStruct((s, d), BF16), compiler_params=_params(("parallel",)))(x, gain)


def _norm_mid(x, mo, g_post, g_pre):
    s, d = x.shape
    tr = _row_tile(s)

    def body(x_ref, mo_ref, g2_ref, g3_ref, x2_ref, h2_ref):
        x2 = x_ref[...] + _rms(mo_ref[...], g2_ref[...])
        x2_ref[...] = x2
        h2_ref[...] = _rms(x2, g3_ref[...]).astype(BF16)

    row = pl.BlockSpec((tr, d), lambda i: (i, 0))
    vec = pl.BlockSpec((1, d), lambda i: (0, 0))
    return pl.pallas_call(
        body, name="norm_mid", grid=(s // tr,), in_specs=[row, row, vec, vec], out_specs=[row, row],
        out_shape=[jax.ShapeDtypeStruct((s, d), F32), jax.ShapeDtypeStruct((s, d), BF16)],
        compiler_params=_params(("parallel",)))(x, mo, g_post, g_pre)


def _loss_head(x2, f, g_post, target):
    s, d = x2.shape
    tr = _row_tile(s)

    def body(x2_ref, f_ref, g_ref, t_ref, loss_ref, dout_ref, df_ref, dg_ref):
        @pl.when(pl.program_id(0) == 0)
        def _():
            loss_ref[...] = jnp.zeros_like(loss_ref)
            dg_ref[...] = jnp.zeros_like(dg_ref)

        fv = f_ref[...]
        g = g_ref[...]
        err = x2_ref[...] + _rms(fv, g) - t_ref[...]
        loss_ref[...] += 0.5 * jnp.sum(jnp.mean(err * err, axis=-1, keepdims=True), axis=0, keepdims=True)
        dout = err * (1.0 / d)
        dout_ref[...] = dout
        df, dg = _rms_bwd(fv, g, dout)
        df_ref[...] = df.astype(BF16)
        dg_ref[...] += jnp.sum(dg, axis=0, keepdims=True)

    row = pl.BlockSpec((tr, d), lambda i: (i, 0))
    vec = pl.BlockSpec((1, d), lambda i: (0, 0))
    one = pl.BlockSpec((1, 1), lambda i: (0, 0))
    return pl.pallas_call(
        body, name="loss_head", grid=(s // tr,), in_specs=[row, row, vec, row], out_specs=[one, row, row, vec],
        out_shape=[jax.ShapeDtypeStruct((1, 1), F32), jax.ShapeDtypeStruct((s, d), F32),
                   jax.ShapeDtypeStruct((s, d), BF16), jax.ShapeDtypeStruct((1, d), F32)],
        compiler_params=_params(("arbitrary",)))(x2, f, g_post, target)


def _norm_mid_bwd(x2, mo, g_post, g_pre, dout, dh2):
    s, d = x2.shape
    tr = _row_tile(s)

    def body(x2_ref, mo_ref, g2_ref, g3_ref, dout_ref, dh2_ref, dx2_ref, dmo_ref, dg2_ref, dg3_ref):
        @pl.when(pl.program_id(0) == 0)
        def _():
            dg2_ref[...] = jnp.zeros_like(dg2_ref)
            dg3_ref[...] = jnp.zeros_like(dg3_ref)

        dv, dg3 = _rms_bwd(x2_ref[...], g3_ref[...], dh2_ref[...])
        dx2 = dout_ref[...] + dv
        dx2_ref[...] = dx2
        dmo, dg2 = _rms_bwd(mo_ref[...], g2_ref[...], dx2)
        dmo_ref[...] = dmo.astype(BF16)
        dg2_ref[...] += jnp.sum(dg2, axis=0, keepdims=True)
        dg3_ref[...] += jnp.sum(dg3, axis=0, keepdims=True)

    row = pl.BlockSpec((tr, d), lambda i: (i, 0))
    vec = pl.BlockSpec((1, d), lambda i: (0, 0))
    return pl.pallas_call(
        body, name="norm_mid_bwd", grid=(s // tr,), in_specs=[row, row, vec, vec, row, row],
        out_specs=[row, row, vec, vec],
        out_shape=[jax.ShapeDtypeStruct((s, d), F32), jax.ShapeDtypeStruct((s, d), BF16),
                   jax.ShapeDtypeStruct((1, d), F32), jax.ShapeDtypeStruct((1, d), F32)],
        compiler_params=_params(("arbitrary",)))(x2, mo, g_post, g_pre, dout, dh2)


def _norm_in_bwd(x, gain, dh, dx2):
    s, d = x.shape
    tr = _row_tile(s)

    def body(x_ref, g_ref, dh_ref, dx2_ref, dx_ref, dg_ref):
        @pl.when(pl.program_id(0) == 0)
        def _():
            dg_ref[...] = jnp.zeros_like(dg_ref)

        dv, dg = _rms_bwd(x_ref[...], g_ref[...], dh_ref[...])
        dx_ref[...] = dx2_ref[...] + dv
        dg_ref[...] += jnp.sum(dg, axis=0, keepdims=True)

    row = pl.BlockSpec((tr, d), lambda i: (i, 0))
    vec = pl.BlockSpec((1, d), lambda i: (0, 0))
    return pl.pallas_call(
        body, name="norm_in_bwd", grid=(s // tr,), in_specs=[row, vec, row, row], out_specs=[row, vec],
        out_shape=[jax.ShapeDtypeStruct((s, d), F32), jax.ShapeDtypeStruct((1, d), F32)],
        compiler_params=_params(("arbitrary",)))(x, gain, dh, dx2)


def _swiglu_epilogue(rs, es):
    g, u = rs
    return [g * _sigmoid(g) * u, g, u]


def _swiglu_bwd_epilogue(rs, es):
    d = rs[0]
    g, u = es[0].astype(F32), es[1].astype(F32)
    sg = _sigmoid(g)
    return [d * u * sg * (1.0 + g * (1.0 - sg)), d * g * sg]


def _sum_epilogue(rs, es):
    return [rs[0] + rs[1]]


def _gates_epilogue(rs, es):
    ab, gv, gg = rs
    ga, gs = es
    return [_sigmoid(ga) * ab + _sigmoid(gs) * gv * _sigmoid(gg), ab, gv, gg]


def _gate_cols(d, gate_off):
    tc = _pick(math.gcd(d, gate_off), (512, 256, 128))
    return tc, gate_off // tc, d // tc


def _merge_gate_a_bwd(z, ab, dmerged, gate_off):
    s, d = ab.shape
    tr = _row_tile(s)
    tc, off, nd = _gate_cols(d, gate_off)

    def body(ga_ref, ab_ref, dm_ref, dga_ref, dab_ref):
        dm = dm_ref[...]
        sa = _sigmoid(ga_ref[...])
        dga_ref[...] = (dm * ab_ref[...] * sa * (1.0 - sa)).astype(BF16)
        dab_ref[...] = (dm * sa).astype(BF16)

    blk = pl.BlockSpec((tr, tc), lambda i, j: (i, j))
    ga = pl.BlockSpec((tr, tc), lambda i, j: (i, off + j))
    return pl.pallas_call(
        body, name="merge_gate_a_bwd", grid=(s // tr, nd), in_specs=[ga, blk, blk], out_specs=[ga, blk],
        out_shape=[jax.ShapeDtypeStruct(z.shape, BF16), jax.ShapeDtypeStruct((s, d), BF16)],
        compiler_params=_params(("parallel", "parallel")))(z, ab, dmerged)


def _merge_gate_s_bwd(z, gv, gg, dmerged, dz, gate_off):
    s, d = gv.shape
    tr = _row_tile(s)
    tc, off, nd = _gate_cols(d, gate_off)

    def body(gs_ref, gv_ref, gg_ref, dm_ref, dz_ref, dgs_ref, dgv_ref, dgg_ref):
        del dz_ref
        ss = _sigmoid(gs_ref[...])
        sg = _sigmoid(gg_ref[...])
        gv_ = gv_ref[...]
        dm = dm_ref[...]
        dgs_ref[...] = (dm * gv_ * sg * ss * (1.0 - ss)).astype(BF16)
        dsb = dm * ss
        dgv_ref[...] = (dsb * sg).astype(BF16)
        dgg_ref[...] = (dsb * gv_ * sg * (1.0 - sg)).astype(BF16)

    blk = pl.BlockSpec((tr, tc), lambda i, j: (i, j))
    gs = pl.BlockSpec((tr, tc), lambda i, j: (i, off + nd + j))
    return pl.pallas_call(
        body, name="merge_gate_s_bwd", grid=(s // tr, nd),
        in_specs=[gs, blk, blk, blk, pl.BlockSpec(memory_space=pl.ANY)], out_specs=[gs, blk, blk],
        out_shape=[jax.ShapeDtypeStruct(z.shape, BF16)] + [jax.ShapeDtypeStruct((s, d), BF16)] * 2,
        input_output_aliases={4: 0},
        compiler_params=_params(("parallel", "parallel")))(z, gv, gg, dmerged, dz)


def _put_cols(dz, src, col_off):
    s, w = src.shape
    tr = _row_tile(s)
    tc = _pick(math.gcd(w, col_off), (512, 256, 128))
    off = col_off // tc

    def body(src_ref, dz_ref, o_ref):
        del dz_ref
        o_ref[...] = src_ref[...].astype(o_ref.dtype)

    return pl.pallas_call(
        body, name="put_cols", grid=(s // tr, w // tc),
        in_specs=[pl.BlockSpec((tr, tc), lambda i, j: (i, j)), pl.BlockSpec(memory_space=pl.ANY)],
        out_specs=pl.BlockSpec((tr, tc), lambda i, j: (i, off + j)),
        out_shape=jax.ShapeDtypeStruct(dz.shape, dz.dtype), input_output_aliases={1: 0},
        compiler_params=_params(("parallel", "parallel")))(src, dz)


ATTN_ROWS = 2048


def _dilate_qkv(z, g, d):
    s = z.shape[0]
    tm = ATTN_ROWS
    per = tm // d
    nh = HEADS_PER_GROUP

    def body(z_ref, o_ref):
        for r in range(d):
            rows = z_ref[...] if d == 1 else z_ref[pl.ds(r, per, stride=d), :]
            o_ref[0, r] = rows.astype(BF16)

    return pl.pallas_call(
        body, name=f"dilate_qkv_{g}", grid=(s // tm, 3, nh),
        in_specs=[pl.BlockSpec((tm, HEAD_DIM), lambda i, w, h: (i, (3 * w + g) * nh + h))],
        out_specs=pl.BlockSpec((1, d, per, HEAD_DIM), lambda i, w, h: (w, 0, i, h)),
        out_shape=jax.ShapeDtypeStruct((3, d, s // d, GROUP_W), BF16),
        compiler_params=_params(("parallel", "parallel", "parallel")))(z)


def _undilate_dqkv(dqkv, dz, g, d):
    s = dz.shape[0]
    tm = ATTN_ROWS
    per = tm // d
    nh = HEADS_PER_GROUP

    def body(i_ref, dz_ref, o_ref, nat_ref):
        del dz_ref
        if d == 1:
            o_ref[...] = i_ref[0, 0]
        else:
            for r in range(d):
                nat_ref[pl.ds(r, per, stride=d), :] = i_ref[0, r].astype(F32)
            o_ref[...] = nat_ref[...].astype(BF16)

    return pl.pallas_call(
        body, name=f"undilate_dqkv_{g}", grid=(s // tm, 3, nh),
        in_specs=[pl.BlockSpec((1, d, per, HEAD_DIM), lambda i, w, h: (w, 0, i, h)),
                  pl.BlockSpec(memory_space=pl.ANY)],
        out_specs=pl.BlockSpec((tm, HEAD_DIM), lambda i, w, h: (i, (3 * w + g) * nh + h)),
        out_shape=jax.ShapeDtypeStruct(dz.shape, dz.dtype), input_output_aliases={1: 0},
        scratch_shapes=[pltpu.VMEM((tm, HEAD_DIM), F32)],
        compiler_params=_params(("parallel", "parallel", "parallel")))(dqkv, dz)


def _alibi_slope(head):
    return 2.0 ** (-8.0 * (head + 1) / N_ATTN_HEADS)


def _band_masks(n, nb):
    qi = lax.broadcasted_iota(jnp.int32, (ATTN_BLK, ATTN_BLK), 0)
    ki = lax.broadcasted_iota(jnp.int32, (ATTN_BLK, ATTN_BLK), 1)
    dist_cur = qi - ki
    dist_prev = ATTN_BLK + qi - ki
    return dist_cur.astype(F32), dist_cur >= 0, dist_prev.astype(F32), dist_prev <= ATTN_BLK


def _dot_nt(a, b):
    return lax.dot_general(a, b, _DOT_DIMS["nt"], preferred_element_type=F32)


def _dot_tn(a, b):
    return lax.dot_general(a, b, _DOT_DIMS["tn"], preferred_element_type=F32)


def _dot(a, b):
    return jnp.dot(a, b, preferred_element_type=F32)


def _scores(q, k, slope_d, dist, valid):
    s = _dot_nt(q, k) * (HEAD_DIM ** -0.5) - slope_d * dist
    return jnp.where(valid, s, NEG_BIG)


def _attn_fwd(qkv, g, d):
    _, _, L, _ = qkv.shape
    nb = L // ATTN_BLK

    def body(q_ref, kc_ref, kp_ref, vc_ref, vp_ref, o_ref, lse_ref):
        n = pl.program_id(1)
        dist_c, valid_c, dist_p, valid_p = _band_masks(n, nb)
        valid_p = jnp.logical_and(valid_p, n > 0)
        for hh in range(HEADS_PER_GROUP):
            cols = slice(hh * HEAD_DIM, (hh + 1) * HEAD_DIM)
            slope_d = _alibi_slope(g * HEADS_PER_GROUP + hh) * d
            q = q_ref[0, 0, :, cols]
            sc = _scores(q, kc_ref[0, 0, :, cols], slope_d, dist_c, valid_c)
            sp = _scores(q, kp_ref[0, 0, :, cols], slope_d, dist_p, valid_p)
            m = jnp.maximum(jnp.max(sc, axis=-1, keepdims=True), jnp.max(sp, axis=-1, keepdims=True))
            ec = jnp.exp(sc - m)
            ep = jnp.exp(sp - m)
            l = jnp.sum(ec, axis=-1, keepdims=True) + jnp.sum(ep, axis=-1, keepdims=True)
            inv = 1.0 / l
            o = _dot((ec * inv).astype(BF16), vc_ref[0, 0, :, cols]) + _dot((ep * inv).astype(BF16), vp_ref[0, 0, :, cols])
            o_ref[0, :, cols] = o
            lse_ref[0, :, cols] = jnp.broadcast_to(m + jnp.log(l), (ATTN_BLK, HEAD_DIM))

    def spec(w, shift):
        return pl.BlockSpec((1, 1, ATTN_BLK, GROUP_W), lambda r, n: (w, r, jnp.maximum(n + shift, 0), 0))

    out = pl.BlockSpec((1, ATTN_BLK, GROUP_W), lambda r, n: (r, n, 0))
    return pl.pallas_call(
        body, name=f"attn_fwd_{g}", grid=(d, nb),
        in_specs=[spec(0, 0), spec(1, 0), spec(1, -1), spec(2, 0), spec(2, -1)], out_specs=[out, out],
        out_shape=[jax.ShapeDtypeStruct((d, L, GROUP_W), F32)] * 2,
        compiler_params=_params(("parallel", "parallel")))(qkv, qkv, qkv, qkv, qkv)


def _attn_bwd(qkv, do, lse, cc, g, d):
    _, _, L, _ = qkv.shape
    nb = L // ATTN_BLK
    scale = HEAD_DIM ** -0.5

    def body(q0_ref, q1_ref, k0_ref, kp_ref, v0_ref, vp_ref, do0_ref, do1_ref, l0_ref, l1_ref, c0_ref, c1_ref, o_ref):
        n = pl.program_id(1)
        dist_c, valid_c, dist_p, valid_p = _band_masks(n, nb)
        valid_b = jnp.logical_and(valid_p, n > 0)
        valid_n = jnp.logical_and(valid_p, n < nb - 1)
        for hh in range(HEADS_PER_GROUP):
            cols = slice(hh * HEAD_DIM, (hh + 1) * HEAD_DIM)
            slope_d = _alibi_slope(g * HEADS_PER_GROUP + hh) * d
            q0, q1 = q0_ref[0, 0, :, cols], q1_ref[0, 0, :, cols]
            k0, kp = k0_ref[0, 0, :, cols], kp_ref[0, 0, :, cols]
            v0, vp = v0_ref[0, 0, :, cols], vp_ref[0, 0, :, cols]
            do0, do1 = do0_ref[0, :, cols], do1_ref[0, :, cols]
            l0, l1 = l0_ref[0, :, cols], l1_ref[0, :, cols]
            c0, c1 = c0_ref[0, :, cols], c1_ref[0, :, cols]
            pa = jnp.exp(_scores(q0, k0, slope_d, dist_c, valid_c) - l0)
            dsa = (pa * (_dot_nt(do0, v0) + c0)).astype(BF16)
            pb = jnp.exp(_scores(q0, kp, slope_d, dist_p, valid_b) - l0)
            dsb = (pb * (_dot_nt(do0, vp) + c0)).astype(BF16)
            pc = jnp.exp(_scores(q1, k0, slope_d, dist_p, valid_n) - l1)
            dsc = (pc * (_dot_nt(do1, v0) + c1)).astype(BF16)
            o_ref[0, 0, :, cols] = ((_dot(dsa, k0) + _dot(dsb, kp)) * scale).astype(BF16)
            o_ref[1, 0, :, cols] = ((_dot_tn(dsa, q0) + _dot_tn(dsc, q1)) * scale).astype(BF16)
            o_ref[2, 0, :, cols] = (_dot_tn(pa.astype(BF16), do0) + _dot_tn(pc.astype(BF16), do1)).astype(BF16)

    def spec(w, shift):
        return pl.BlockSpec((1, 1, ATTN_BLK, GROUP_W), lambda r, n: (w, r, jnp.clip(n + shift, 0, nb - 1), 0))

    def spec3(shift):
        return pl.BlockSpec((1, ATTN_BLK, GROUP_W), lambda r, n: (r, jnp.clip(n + shift, 0, nb - 1), 0))

    return pl.pallas_call(
        body, name=f"attn_bwd_{g}", grid=(d, nb),
        in_specs=[spec(0, 0), spec(0, 1), spec(1, 0), spec(1, -1), spec(2, 0), spec(2, -1),
                  spec3(0), spec3(1), spec3(0), spec3(1), spec3(0), spec3(1)],
        out_specs=pl.BlockSpec((3, 1, ATTN_BLK, GROUP_W), lambda r, n: (0, r, n, 0)),
        out_shape=jax.ShapeDtypeStruct((3, d, L, GROUP_W), BF16),
        compiler_params=_params(("parallel", "parallel")))(qkv, qkv, qkv, qkv, qkv, qkv, do, do, lse, lse, cc, cc)


def _load_natural(refs, nat_refs):
    for g, d in enumerate(ATTN_DILATIONS):
        if d == 1:
            nat_refs[g][...] = refs[g][0]
        else:
            per = ATTN_ROWS // d
            for r in range(d):
                nat_refs[g][pl.ds(r, per, stride=d), :] = refs[g][r]


def _mix_weights(lse_nat):
    l0, l1, l2 = lse_nat[0][...], lse_nat[1][...], lse_nat[2][...]
    m = jnp.maximum(jnp.maximum(l0, l1), l2)
    e0, e1, e2 = jnp.exp(l0 - m), jnp.exp(l1 - m), jnp.exp(l2 - m)
    inv = 1.0 / (e0 + e1 + e2)
    return e0 * inv, e1 * inv, e2 * inv


def _dilated_specs(s):
    return [pl.BlockSpec((d, ATTN_ROWS // d, HEAD_DIM), lambda i, h: (0, i, h)) for d in ATTN_DILATIONS]


NATURAL_SCRATCH = [pltpu.VMEM((ATTN_ROWS, HEAD_DIM), F32)] * (2 * len(ATTN_DILATIONS))


def _attn_merge(outs, lses):
    s = outs[0].shape[0] * outs[0].shape[1]

    def body(o0, o1, o2, l0, l1, l2, a_ref, *nat):
        onat, lnat = nat[:3], nat[3:]
        _load_natural((o0, o1, o2), onat)
        _load_natural((l0, l1, l2), lnat)
        w0, w1, w2 = _mix_weights(lnat)
        a_ref[...] = (w0 * onat[0][...] + w1 * onat[1][...] + w2 * onat[2][...]).astype(BF16)

    return pl.pallas_call(
        body, name="attn_merge", grid=(s // ATTN_ROWS, HEADS_PER_GROUP), in_specs=_dilated_specs(s) * 2,
        out_specs=pl.BlockSpec((ATTN_ROWS, HEAD_DIM), lambda i, h: (i, h)),
        out_shape=jax.ShapeDtypeStruct((s, GROUP_W), BF16), scratch_shapes=NATURAL_SCRATCH,
        compiler_params=_params(("parallel", "parallel")))(*outs, *lses)


def _attn_merge_bwd(outs, lses, dattn):
    s = dattn.shape[0]

    def body(o0, o1, o2, l0, l1, l2, da_ref, do0, do1, do2, c0, c1, c2, *nat):
        onat, lnat = nat[:3], nat[3:]
        _load_natural((o0, o1, o2), onat)
        _load_natural((l0, l1, l2), lnat)
        ws = _mix_weights(lnat)
        da = da_ref[...]
        attn = ws[0] * onat[0][...] + ws[1] * onat[1][...] + ws[2] * onat[2][...]
        tot = jnp.broadcast_to(jnp.sum(da * attn, axis=-1, keepdims=True), (ATTN_ROWS, HEAD_DIM))
        for g, (d, do_ref, c_ref) in enumerate(zip(ATTN_DILATIONS, (do0, do1, do2), (c0, c1, c2))):
            if d == 1:
                do_ref[0] = (ws[g] * da).astype(BF16)
                c_ref[0] = -ws[g] * tot
            else:
                onat[g][...] = ws[g] * da
                lnat[g][...] = -ws[g] * tot
                per = ATTN_ROWS // d
                for r in range(d):
                    do_ref[r] = onat[g][pl.ds(r, per, stride=d), :].astype(BF16)
                    c_ref[r] = lnat[g][pl.ds(r, per, stride=d), :]

    dil = _dilated_specs(s)
    shapes = [jax.ShapeDtypeStruct(o.shape, BF16) for o in outs] + [jax.ShapeDtypeStruct(o.shape, F32) for o in outs]
    return pl.pallas_call(
        body, name="attn_merge_bwd", grid=(s // ATTN_ROWS, HEADS_PER_GROUP),
        in_specs=dil * 2 + [pl.BlockSpec((ATTN_ROWS, HEAD_DIM), lambda i, h: (i, h))], out_specs=dil * 2,
        out_shape=shapes, scratch_shapes=NATURAL_SCRATCH,
        compiler_params=_params(("parallel", "parallel")))(*outs, *lses, dattn)


def _ssm_prepare(a_re, a_im, log_dt, b_re, b_im, c_re, c_im):
    n_g = a_re.shape[0]
    nj = n_g * SSM_GROUP // SSM_TILE_CH
    gpt = SSM_TILE_CH // SSM_GROUP
    dt = jnp.exp(log_dt)[:, None]
    mag = jnp.exp(a_re * dt)
    lr, li = mag * jnp.cos(a_im * dt), mag * jnp.sin(a_im * dt)
    den = a_re * a_re + a_im * a_im
    cr = ((lr - 1.0) * a_re + li * a_im) / den
    ci = (li * a_re - (lr - 1.0) * a_im) / den
    bb_re = cr[..., None] * b_re - ci[..., None] * b_im
    bb_im = cr[..., None] * b_im + ci[..., None] * b_re
    eye = jnp.eye(gpt, dtype=F32)

    def b_tiles(t):
        t = t.transpose(0, 2, 1).reshape(nj, gpt, SSM_GROUP, SSM_STATE)
        return jnp.einsum("jgcp,gh->jgchp", t, eye).reshape(nj, SSM_TILE_CH, SSM_TILE_ST)

    def c_tiles(t):
        t = t.reshape(nj, gpt, SSM_GROUP, SSM_STATE)
        return jnp.einsum("jgcp,gh->jhpgc", t, eye).reshape(nj, SSM_TILE_ST, SSM_TILE_CH)

    lam = jnp.stack([lr.reshape(-1), li.reshape(-1)])
    bmat = jnp.concatenate([b_tiles(bb_re), b_tiles(bb_im)], axis=2)
    cmat = jnp.concatenate([c_tiles(c_re), -c_tiles(c_im)], axis=1)
    return lam, bmat, cmat


def _scan_chunk(xr, xi, lr, li, cr, ci, row):
    t_rows = xr.shape[0]
    first = row == 0
    xr = xr + jnp.where(first, lr * cr - li * ci, 0.0)
    xi = xi + jnp.where(first, lr * ci + li * cr, 0.0)
    pr, pi = lr, li
    shift = 1
    while shift < t_rows:
        keep = row >= shift
        sr = jnp.where(keep, pltpu.roll(xr, shift, 0), 0.0)
        si = jnp.where(keep, pltpu.roll(xi, shift, 0), 0.0)
        xr, xi = xr + pr * sr - pi * si, xi + pr * si + pi * sr
        pr, pi = pr * pr - pi * pi, 2.0 * pr * pi
        shift *= 2
    return xr, xi


def _rscan_chunk(gr, gi, lr, li, cr, ci, row):
    t_rows = gr.shape[0]
    last = row == t_rows - 1
    gr = gr + jnp.where(last, lr * cr + li * ci, 0.0)
    gi = gi + jnp.where(last, lr * ci - li * cr, 0.0)
    pr, pi = lr, li
    shift = 1
    while shift < t_rows:
        keep = row < t_rows - shift
        sr = jnp.where(keep, pltpu.roll(gr, t_rows - shift, 0), 0.0)
        si = jnp.where(keep, pltpu.roll(gi, t_rows - shift, 0), 0.0)
        gr, gi = gr + pr * sr + pi * si, gi + pr * si - pi * sr
        pr, pi = pr * pr - pi * pi, 2.0 * pr * pi
        shift *= 2
    return gr, gi


def _ssm_dims(z, bmat, u_off):
    s = z.shape[0]
    nj = bmat.shape[0]
    w = nj * SSM_TILE_CH
    t_rows = _pick(s, (256, 128))
    ub = _pick(math.gcd(w, u_off), (512, 256, 128))
    return s, nj, w, nj * SSM_TILE_ST, t_rows, ub


def _u_tile(u_refs, ub, j):
    col = j * SSM_TILE_CH
    return u_refs[col // ub][:, col % ub:col % ub + SSM_TILE_CH]


def _ssm_fwd(z, bmat, cmat, lam, dskip, u_off):
    s, nj, w, ns, t_rows, ub = _ssm_dims(z, bmat, u_off)
    nub = w // ub

    def body(*refs):
        u_refs = refs[:nub]
        b_ref, c_ref, lam_ref, d_ref, y_ref, yg_ref, xin_ref, carry_ref = refs[nub:]

        @pl.when(pl.program_id(0) == 0)
        def _():
            carry_ref[...] = jnp.zeros_like(carry_ref)

        xin_ref[0] = carry_ref[...]
        row = lax.broadcasted_iota(jnp.int32, (t_rows, SSM_TILE_ST), 0)
        for j in range(nj):
            st = slice(j * SSM_TILE_ST, (j + 1) * SSM_TILE_ST)
            ch = slice(j * SSM_TILE_CH, (j + 1) * SSM_TILE_CH)
            uj = _u_tile(u_refs, ub, j)
            bu = _dot(uj.astype(BF16), b_ref[j])
            xr, xi = _scan_chunk(bu[:, :SSM_TILE_ST], bu[:, SSM_TILE_ST:], lam_ref[0:1, st], lam_ref[1:2, st],
                                 carry_ref[0:1, st], carry_ref[1:2, st], row)
            carry_ref[0:1, st] = xr[t_rows - 1:t_rows, :]
            carry_ref[1:2, st] = xi[t_rows - 1:t_rows, :]
            xs = jnp.concatenate([xr, xi], axis=1).astype(BF16)
            yj = _dot(xs, c_ref[j]) + d_ref[:, ch] * uj
            y_ref[:, ch] = yj
            yg_ref[:, ch] = _gelu(yj).astype(BF16)

    u_specs = [pl.BlockSpec((t_rows, ub), lambda c, k=k: (c, u_off // ub + k)) for k in range(nub)]
    full3 = lambda shape: pl.BlockSpec(shape, lambda c: (0, 0, 0))
    full2 = lambda shape: pl.BlockSpec(shape, lambda c: (0, 0))
    rows = pl.BlockSpec((t_rows, w), lambda c: (c, 0))
    return pl.pallas_call(
        body, name="ssm_fwd", grid=(s // t_rows,),
        in_specs=u_specs + [full3(bmat.shape), full3(cmat.shape), full2(lam.shape), full2(dskip.shape)],
        out_specs=[rows, rows, pl.BlockSpec((1, 2, ns), lambda c: (c, 0, 0))],
        out_shape=[jax.ShapeDtypeStruct((s, w), F32), jax.ShapeDtypeStruct((s, w), BF16),
                   jax.ShapeDtypeStruct((s // t_rows, 2, ns), F32)],
        scratch_shapes=[pltpu.VMEM((2, ns), F32)],
        compiler_params=_params(("arbitrary",)))(*([z] * nub), bmat, cmat, lam, dskip)


def _ssm_bwd(z, y, dyg, xin, bmat, cmat, lam, dskip, u_off):
    s, nj, w, ns, t_rows, ub = _ssm_dims(z, bmat, u_off)
    nub = w // ub
    nc = s // t_rows

    def body(*refs):
        u_refs = refs[:nub]
        (y_ref, dyg_ref, xin_ref, b_ref, c_ref, lam_ref, d_ref,
         du_ref, db_ref, dc_ref, dlam_ref, dd_ref, carry_ref) = refs[nub:]

        @pl.when(pl.program_id(0) == 0)
        def _():
            carry_ref[...] = jnp.zeros_like(carry_ref)
            db_ref[...] = jnp.zeros_like(db_ref)
            dc_ref[...] = jnp.zeros_like(dc_ref)
            dlam_ref[...] = jnp.zeros_like(dlam_ref)
            dd_ref[...] = jnp.zeros_like(dd_ref)

        row = lax.broadcasted_iota(jnp.int32, (t_rows, SSM_TILE_ST), 0)
        for j in range(nj):
            st = slice(j * SSM_TILE_ST, (j + 1) * SSM_TILE_ST)
            ch = slice(j * SSM_TILE_CH, (j + 1) * SSM_TILE_CH)
            uj = _u_tile(u_refs, ub, j)
            ujb = uj.astype(BF16)
            lr, li = lam_ref[0:1, st], lam_ref[1:2, st]
            cr, ci = xin_ref[0, 0:1, st], xin_ref[0, 1:2, st]
            bu = _dot(ujb, b_ref[j])
            xr, xi = _scan_chunk(bu[:, :SSM_TILE_ST], bu[:, SSM_TILE_ST:], lr, li, cr, ci, row)
            dyj = dyg_ref[:, ch] * _gelu_grad(y_ref[:, ch])
            dyb = dyj.astype(BF16)
            gin = _dot_nt(dyb, c_ref[j])
            gr, gi = _rscan_chunk(gin[:, :SSM_TILE_ST], gin[:, SSM_TILE_ST:], lr, li,
                                  carry_ref[0:1, st], carry_ref[1:2, st], row)
            carry_ref[0:1, st] = gr[0:1, :]
            carry_ref[1:2, st] = gi[0:1, :]
            first = row == 0
            pxr = jnp.where(first, cr, pltpu.roll(xr, 1, 0))
            pxi = jnp.where(first, ci, pltpu.roll(xi, 1, 0))
            dlam_ref[0:1, st] += jnp.sum(gr * pxr + gi * pxi, axis=0, keepdims=True)
            dlam_ref[1:2, st] += jnp.sum(gi * pxr - gr * pxi, axis=0, keepdims=True)
            gx = jnp.concatenate([gr, gi], axis=1).astype(BF16)
            xs = jnp.concatenate([xr, xi], axis=1).astype(BF16)
            du_ref[:, ch] = (_dot_nt(gx, b_ref[j]) + d_ref[:, ch] * dyj).astype(BF16)
            db_ref[j] += _dot_tn(ujb, gx)
            dc_ref[j] += _dot_tn(xs, dyb)
            dd_ref[:, ch] += jnp.sum(dyj * uj, axis=0, keepdims=True)

    rev = lambda c: nc - 1 - c
    u_specs = [pl.BlockSpec((t_rows, ub), lambda c, k=k: (rev(c), u_off // ub + k)) for k in range(nub)]
    full3 = lambda shape: pl.BlockSpec(shape, lambda c: (0, 0, 0))
    full2 = lambda shape: pl.BlockSpec(shape, lambda c: (0, 0))
    rows = pl.BlockSpec((t_rows, w), lambda c: (rev(c), 0))
    return pl.pallas_call(
        body, name="ssm_bwd", grid=(nc,),
        in_specs=u_specs + [rows, rows, pl.BlockSpec((1, 2, ns), lambda c: (rev(c), 0, 0)),
                            full3(bmat.shape), full3(cmat.shape), full2(lam.shape), full2(dskip.shape)],
        out_specs=[rows, full3(bmat.shape), full3(cmat.shape), full2(lam.shape), full2(dskip.shape)],
        out_shape=[jax.ShapeDtypeStruct((s, w), BF16), jax.ShapeDtypeStruct(bmat.shape, F32),
                   jax.ShapeDtypeStruct(cmat.shape, F32), jax.ShapeDtypeStruct(lam.shape, F32),
                   jax.ShapeDtypeStruct(dskip.shape, F32)],
        scratch_shapes=[pltpu.VMEM((2, ns), F32)],
        compiler_params=_params(("arbitrary",)))(*([z] * nub), y, dyg, xin, bmat, cmat, lam, dskip)


def _adam_math(w, g, m, v):
    m = ADAM_B1 * m + (1.0 - ADAM_B1) * g
    v = ADAM_B2 * v + (1.0 - ADAM_B2) * (g * g)
    m_hat = m / (1.0 - ADAM_B1 ** ADAM_STEP)
    v_hat = v / (1.0 - ADAM_B2 ** ADAM_STEP)
    delta = -ADAM_LR * (m_hat / (jnp.sqrt(v_hat) + ADAM_EPS) + ADAM_WD * w)
    return delta, m, v


def _adam_rows(r, c):
    for tr in (512, 256, 128, 64, 32, 16, 8):
        if r % tr == 0 and tr * c * 4 <= (1 << 20):
            return tr
    return r


def _adamw_big(w, p_mine, p_sib, m, v, name):
    r, c = w.shape
    tr = _adam_rows(r, c)

    def body(w_ref, a_ref, b_ref, m_ref, v_ref, g_ref, d_ref, nm_ref, nv_ref):
        g = a_ref[...] + b_ref[...]
        g_ref[...] = g
        d_ref[...], nm_ref[...], nv_ref[...] = _adam_math(w_ref[...], g, m_ref[...], v_ref[...])

    blk = pl.BlockSpec((tr, c), lambda i: (i, 0))
    return pl.pallas_call(body, name=f"adamw_{name}", grid=(r // tr,), in_specs=[blk] * 5, out_specs=[blk] * 4,
                          out_shape=[jax.ShapeDtypeStruct((r, c), F32)] * 4,
                          compiler_params=_params(("parallel",)))(w, p_mine, p_sib, m, v)


def _adamw_small(w, parts, m, v):
    r, c = w.shape
    n_dev = parts.shape[0]

    def body(w_ref, p_ref, m_ref, v_ref, g_ref, d_ref, nm_ref, nv_ref):
        g = p_ref[0]
        for k in range(1, n_dev):
            g = g + p_ref[k]
        g_ref[...] = g
        d_ref[...], nm_ref[...], nv_ref[...] = _adam_math(w_ref[...], g, m_ref[...], v_ref[...])

    blk = pl.BlockSpec((r, c), lambda i: (0, 0))
    return pl.pallas_call(body, name="adamw_small", grid=(1,),
                          in_specs=[blk, pl.BlockSpec((n_dev, r, c), lambda i: (0, 0, 0)), blk, blk],
                          out_specs=[blk] * 4, out_shape=[jax.ShapeDtypeStruct((r, c), F32)] * 4,
                          compiler_params=_params(("arbitrary",)))(w, parts, m, v)


def _cast_bf16(w, name):
    r, c = w.shape
    tr = _adam_rows(r, c)

    def body(w_ref, o_ref):
        o_ref[...] = w_ref[...].astype(BF16)

    blk = pl.BlockSpec((tr, c), lambda i: (i, 0))
    return pl.pallas_call(body, name=f"cast_{name}", grid=(r // tr,), in_specs=[blk], out_specs=blk,
                          out_shape=jax.ShapeDtypeStruct((r, c), BF16), compiler_params=_params(("parallel",)))(w)


def _sum_slots(recv, name):
    _, r, c = recv.shape
    tr = _adam_rows(r, c)

    def body(p_ref, o_ref):
        acc = p_ref[0].astype(F32)
        for k in range(1, N_CHIPS):
            acc = acc + p_ref[k].astype(F32)
        o_ref[...] = acc

    return pl.pallas_call(body, name=f"sum_{name}", grid=(r // tr,),
                          in_specs=[pl.BlockSpec((N_CHIPS, tr, c), lambda i: (0, i, 0))],
                          out_specs=pl.BlockSpec((tr, c), lambda i: (i, 0)),
                          out_shape=jax.ShapeDtypeStruct((r, c), F32), compiler_params=_params(("parallel",)))(recv)


BIG_WEIGHTS = ("w_in", "w_attn_up", "w_glu_v", "w_glu_g", "w_out", "w_ffn_gate", "w_ffn_up", "w_ffn_down")
COL_SHARDED = ("w_in", "w_attn_up", "w_glu_v", "w_glu_g", "w_ffn_gate", "w_ffn_up")
HBM = pl.BlockSpec(memory_space=pl.ANY)
MESH = pl.DeviceIdType.MESH


def _shard_of(ref, name, j, shard_shape):
    r, c = shard_shape
    if name in COL_SHARDED:
        return ref.at[:, pl.ds(pl.multiple_of(j * c, 128), c)]
    return ref.at[pl.ds(pl.multiple_of(j * r, 8), r), :]


def _other_chips():
    x, y = lax.axis_index("x"), lax.axis_index("y")
    return [(1 - x, y), (x, 1 - y), (1 - x, 1 - y)]


def _gather_weights(shards):
    names = list(shards)
    n = len(names)
    full_shapes = []
    for k in names:
        r, c = shards[k].shape
        full_shapes.append((r, c * N_CHIPS) if k in COL_SHARDED else (r * N_CHIPS, c))

    def body(*refs):
        src, dst = refs[:n], refs[n:2 * n]
        send_sems, recv_sems, local_sems = refs[2 * n:]
        x, y, c = lax.axis_index("x"), lax.axis_index("y"), lax.axis_index("c")
        me = 2 * x + y
        locals_, sends, arrivals = [], [], []
        for i, k in enumerate(names):
            shape = shards[k].shape
            own = pltpu.make_async_copy(src[i], _shard_of(dst[i], k, me, shape), local_sems.at[i])
            own.start()
            locals_.append(own)
            for p, (px, py) in enumerate(_other_chips()):
                out = pltpu.make_async_remote_copy(
                    src_ref=src[i], dst_ref=_shard_of(dst[i], k, me, shape), send_sem=send_sems.at[i, p],
                    recv_sem=recv_sems.at[i, p], device_id=(px, py, c), device_id_type=MESH)
                out.start()
                sends.append(out)
                arrivals.append(pltpu.make_async_remote_copy(
                    src_ref=src[i], dst_ref=_shard_of(dst[i], k, 2 * px + py, shape), send_sem=send_sems.at[i, p],
                    recv_sem=recv_sems.at[i, p], device_id=(px, py, c), device_id_type=MESH))
        for a in arrivals:
            a.wait_recv()
        for cp in sends:
            cp.wait_send()
        for cp in locals_:
            cp.wait()

    outs = pl.pallas_call(
        body, name="gather_weights", in_specs=[HBM] * n, out_specs=[HBM] * n,
        out_shape=[jax.ShapeDtypeStruct(s, BF16) for s in full_shapes],
        scratch_shapes=[pltpu.SemaphoreType.DMA((n, 3)), pltpu.SemaphoreType.DMA((n, 3)), pltpu.SemaphoreType.DMA((n,))],
        compiler_params=pltpu.CompilerParams(has_side_effects=True),
    )(*[shards[k] for k in names])
    return dict(zip(names, outs))


def _scatter_grads(grads, shard_shapes):
    names = list(grads)
    n = len(names)

    def body(*refs):
        src, dst = refs[:n], refs[n:2 * n]
        send_sems, recv_sems, local_sems = refs[2 * n:]
        x, y, c = lax.axis_index("x"), lax.axis_index("y"), lax.axis_index("c")
        me = 2 * x + y
        locals_, sends, arrivals = [], [], []
        for i, k in enumerate(names):
            shape = shard_shapes[k]
            own = pltpu.make_async_copy(_shard_of(src[i], k, me, shape), dst[i].at[me], local_sems.at[i])
            own.start()
            locals_.append(own)
            for p, (px, py) in enumerate(_other_chips()):
                peer = 2 * px + py
                out = pltpu.make_async_remote_copy(
                    src_ref=_shard_of(src[i], k, peer, shape), dst_ref=dst[i].at[me], send_sem=send_sems.at[i, p],
                    recv_sem=recv_sems.at[i, p], device_id=(px, py, c), device_id_type=MESH)
                out.start()
                sends.append(out)
                arrivals.append(pltpu.make_async_remote_copy(
                    src_ref=_shard_of(src[i], k, peer, shape), dst_ref=dst[i].at[peer], send_sem=send_sems.at[i, p],
                    recv_sem=recv_sems.at[i, p], device_id=(px, py, c), device_id_type=MESH))
        for a in arrivals:
            a.wait_recv()
        for cp in sends:
            cp.wait_send()
        for cp in locals_:
            cp.wait()

    outs = pl.pallas_call(
        body, name="scatter_grads", in_specs=[HBM] * n, out_specs=[HBM] * n,
        out_shape=[jax.ShapeDtypeStruct((N_CHIPS,) + tuple(shard_shapes[k]), BF16) for k in names],
        scratch_shapes=[pltpu.SemaphoreType.DMA((n, 3)), pltpu.SemaphoreType.DMA((n, 3)), pltpu.SemaphoreType.DMA((n,))],
        compiler_params=pltpu.CompilerParams(has_side_effects=True),
    )(*[grads[k] for k in names])
    return dict(zip(names, outs))


def _swap_with_sibling(parts):
    names = list(parts)
    n = len(names)

    def body(*refs):
        src, dst = refs[:n], refs[n:2 * n]
        send_sems, recv_sems = refs[2 * n:]
        sibling = (lax.axis_index("x"), lax.axis_index("y"), 1 - lax.axis_index("c"))
        copies = []
        for i in range(n):
            cp = pltpu.make_async_remote_copy(src_ref=src[i], dst_ref=dst[i], send_sem=send_sems.at[i],
                                              recv_sem=recv_sems.at[i], device_id=sibling, device_id_type=MESH)
            cp.start()
            copies.append(cp)
        for cp in copies:
            cp.wait_recv()
        for cp in copies:
            cp.wait_send()

    outs = pl.pallas_call(
        body, name="swap_with_sibling", in_specs=[HBM] * n, out_specs=[HBM] * n,
        out_shape=[jax.ShapeDtypeStruct(parts[k].shape, F32) for k in names],
        scratch_shapes=[pltpu.SemaphoreType.DMA((n,)), pltpu.SemaphoreType.DMA((n,))],
        compiler_params=pltpu.CompilerParams(has_side_effects=True),
    )(*[parts[k] for k in names])
    return dict(zip(names, outs))


def _share_small(packed):
    r, c = packed.shape
    n_dev = 8

    def body(src, dst, send_sems, recv_sems, local_sem):
        x, y, cc = lax.axis_index("x"), lax.axis_index("y"), lax.axis_index("c")
        me = 4 * x + 2 * y + cc
        own = pltpu.make_async_copy(src, dst.at[me], local_sem)
        own.start()
        sends, arrivals = [], []
        p = 0
        for fx in range(2):
            for fy in range(2):
                for fc in range(2):
                    if fx == fy == fc == 0:
                        continue
                    px, py, pc = x ^ fx, y ^ fy, cc ^ fc
                    out = pltpu.make_async_remote_copy(src_ref=src, dst_ref=dst.at[me], send_sem=send_sems.at[p],
                                                       recv_sem=recv_sems.at[p], device_id=(px, py, pc), device_id_type=MESH)
                    out.start()
                    sends.append(out)
                    arrivals.append(pltpu.make_async_remote_copy(
                        src_ref=src, dst_ref=dst.at[4 * px + 2 * py + pc], send_sem=send_sems.at[p],
                        recv_sem=recv_sems.at[p], device_id=(px, py, pc), device_id_type=MESH))
                    p += 1
        for a in arrivals:
            a.wait_recv()
        for cp in sends:
            cp.wait_send()
        own.wait()

    return pl.pallas_call(
        body, name="share_small", in_specs=[HBM], out_specs=HBM,
        out_shape=jax.ShapeDtypeStruct((n_dev, r, c), F32),
        scratch_shapes=[pltpu.SemaphoreType.DMA((7,)), pltpu.SemaphoreType.DMA((7,)), pltpu.SemaphoreType.DMA],
        compiler_params=pltpu.CompilerParams(has_side_effects=True),
    )(packed)


SMALL_WEIGHTS = ("norm_mix_pre", "ssm_a_re", "ssm_a_im", "ssm_log_dt", "ssm_b_re", "ssm_b_im", "ssm_c_re", "ssm_c_im",
                 "ssm_d", "norm_mix_post", "norm_ffn_pre", "norm_ffn_post")
WEIGHT_ORDER = ("norm_mix_pre", "w_in", "w_attn_up", "ssm_a_re", "ssm_a_im", "ssm_log_dt", "ssm_b_re", "ssm_b_im",
                "ssm_c_re", "ssm_c_im", "ssm_d", "w_glu_v", "w_glu_g", "w_out", "norm_mix_post", "norm_ffn_pre",
                "w_ffn_gate", "w_ffn_up", "w_ffn_down", "norm_ffn_post")
PACK_LANES = 128
PACK_ROWS = 8


def _pack_small(arrs):
    flat = jnp.concatenate([arrs[k].reshape(-1) for k in SMALL_WEIGHTS])
    pad = -flat.shape[0] % (PACK_LANES * PACK_ROWS)
    return jnp.pad(flat, (0, pad)).reshape(-1, PACK_LANES)


def _unpack_small(packed, like):
    flat = packed.reshape(-1)
    out, pos = {}, 0
    for k in SMALL_WEIGHTS:
        n = like[k].size
        out[k] = flat[pos:pos + n].reshape(like[k].shape)
        pos += n
    return out


def _local_step(x, target, big, small):
    s, d = x.shape
    u_off = 3 * HQ
    gate_off = u_off + d // 2
    g1, g2, g3, g4 = (small[k][0:1] for k in ("norm_mix_pre", "norm_mix_post", "norm_ffn_pre", "norm_ffn_post"))
    ssm_names = ("ssm_a_re", "ssm_a_im", "ssm_log_dt", "ssm_b_re", "ssm_b_im", "ssm_c_re", "ssm_c_im")
    (lam, bmat, cmat), ssm_vjp = jax.vjp(_ssm_prepare, *[small[k][0] for k in ssm_names])
    bmat, cmat = bmat.astype(BF16), cmat.astype(BF16)
    dskip = small["ssm_d"][0:1]

    h1 = _norm_in(x, g1)
    z = _mm(h1, big["w_in"], "nn", F32, "in_proj")
    qkv = [_dilate_qkv(z, g, dil) for g, dil in enumerate(ATTN_DILATIONS)]
    outs, lses = zip(*[_attn_fwd(qkv[g], g, dil) for g, dil in enumerate(ATTN_DILATIONS)])
    attn = _attn_merge(outs, lses)
    y, yg, xin = _ssm_fwd(z, bmat, cmat, lam, dskip, u_off)
    merged, ab, gv, gg = _mm_fused(
        [attn, yg], [big["w_attn_up"], big["w_glu_v"], big["w_glu_g"]], [(0, 0), (1, 1), (1, 2)], "nn",
        [BF16, F32, F32, F32], "branches_merge", extras=[(z, gate_off), (z, gate_off + d)], epilogue=_gates_epilogue)
    mo = _mm(merged, big["w_out"], "nn", F32, "mix_out")
    x2, h2 = _norm_mid(x, mo, g2, g3)
    act, fg, fu = _mm_fused([h2], [big["w_ffn_gate"], big["w_ffn_up"]], [(0, 0), (0, 1)], "nn", [BF16, BF16, BF16],
                            "ffn_up_act", epilogue=_swiglu_epilogue)
    f = _mm(act, big["w_ffn_down"], "nn", F32, "ffn_down")
    loss, dout, df, dg4 = _loss_head(x2, f, g4, target)

    grads = {}
    dfg, dfu = _mm_fused([df], [big["w_ffn_down"]], [(0, 0)], "nt", [BF16, BF16], "d_ffn_act",
                         extras=[(fg, 0), (fu, 0)], epilogue=_swiglu_bwd_epilogue)
    grads["w_ffn_down"] = _mm_kloop(act, df, "tn", BF16, "dw_ffn_down")
    dh2 = _mm_kloop(dfg, big["w_ffn_gate"], "nt", F32, "d_h2_gate")
    dh2 = _mm_kloop(dfu, big["w_ffn_up"], "nt", F32, "d_h2_up", add=dh2)
    grads["w_ffn_gate"] = _mm_kloop(h2, dfg, "tn", BF16, "dw_ffn_gate")
    grads["w_ffn_up"] = _mm_kloop(h2, dfu, "tn", BF16, "dw_ffn_up")
    dx2, dmo, dg2, dg3 = _norm_mid_bwd(x2, mo, g2, g3, dout, dh2)
    dmerged = _mm(dmo, big["w_out"], "nt", F32, "d_merged")
    grads["w_out"] = _mm_kloop(merged, dmo, "tn", BF16, "dw_out")
    dz, dab = _merge_gate_a_bwd(z, ab, dmerged, gate_off)
    dz, dgv, dgg = _merge_gate_s_bwd(z, gv, gg, dmerged, dz, gate_off)
    dyg = _mm_fused([dgv, dgg], [big["w_glu_v"], big["w_glu_g"]], [(0, 0), (1, 1)], "nt", [F32], "d_yg",
                    epilogue=_sum_epilogue)[0]
    grads["w_glu_v"] = _mm_kloop(yg, dgv, "tn", BF16, "dw_glu_v")
    grads["w_glu_g"] = _mm_kloop(yg, dgg, "tn", BF16, "dw_glu_g")
    du, dbmat, dcmat, dlam, dd = _ssm_bwd(z, y, dyg, xin, bmat, cmat, lam, dskip, u_off)
    dz = _put_cols(dz, du, u_off)
    dattn = _mm(dab, big["w_attn_up"], "nt", F32, "d_attn")
    grads["w_attn_up"] = _mm_kloop(attn, dab, "tn", BF16, "dw_attn_up")
    merged_bwd = _attn_merge_bwd(outs, lses, dattn)
    for g, dil in enumerate(ATTN_DILATIONS):
        dqkv = _attn_bwd(qkv[g], merged_bwd[g], lses[g], merged_bwd[3 + g], g, dil)
        dz = _undilate_dqkv(dqkv, dz, g, dil)
    grads["w_in"] = _mm_kloop(h1, dz, "tn", BF16, "dw_in")
    dh1 = _mm_kloop(dz, big["w_in"], "nt", F32, "d_h1")
    grad_x, dg1 = _norm_in_bwd(x, g1, dh1, dx2)

    small_grads = dict(zip(ssm_names, (t[None] for t in ssm_vjp((dlam, dbmat, dcmat)))))
    small_grads.update(norm_mix_pre=dg1, norm_mix_post=dg2, norm_ffn_pre=dg3, norm_ffn_post=dg4, ssm_d=dd)
    return loss[0, 0], grad_x, grads, small_grads


def kernel(x, norm_mix_pre, w_in, w_attn_up, ssm_a_re, ssm_a_im, ssm_log_dt, ssm_b_re, ssm_b_im, ssm_c_re, ssm_c_im, ssm_d, w_glu_v, w_glu_g, w_out, norm_mix_post, norm_ffn_pre, w_ffn_gate, w_ffn_up, w_ffn_down, norm_ffn_post, loss_target, m_norm_mix_pre, m_w_in, m_w_attn_up, m_ssm_a_re, m_ssm_a_im, m_ssm_log_dt, m_ssm_b_re, m_ssm_b_im, m_ssm_c_re, m_ssm_c_im, m_ssm_d, m_w_glu_v, m_w_glu_g, m_w_out, m_norm_mix_post, m_norm_ffn_pre, m_w_ffn_gate, m_w_ffn_up, m_w_ffn_down, m_norm_ffn_post, v_norm_mix_pre, v_w_in, v_w_attn_up, v_ssm_a_re, v_ssm_a_im, v_ssm_log_dt, v_ssm_b_re, v_ssm_b_im, v_ssm_c_re, v_ssm_c_im, v_ssm_d, v_w_glu_v, v_w_glu_g, v_w_out, v_norm_mix_post, v_norm_ffn_pre, v_w_ffn_gate, v_w_ffn_up, v_w_ffn_down, v_norm_ffn_post):
    given = dict(locals())
    w = {k: given[k] for k in WEIGHT_ORDER}
    m = {k: given["m_" + k] for k in WEIGHT_ORDER}
    v = {k: given["v_" + k] for k in WEIGHT_ORDER}

    shards = {k: _cast_bf16(w[k][0], k) for k in BIG_WEIGHTS}
    big = _gather_weights(shards)

    loss, grad_x, grads, small_grads = _local_step(x[0], loss_target[0], big, {k: w[k] for k in SMALL_WEIGHTS})
    loss = lax.psum(loss, MESH_AXES)

    shard_shapes = {k: w[k].shape[1:] for k in BIG_WEIGHTS}
    slots = _scatter_grads(grads, shard_shapes)
    mine = {k: _sum_slots(slots[k], k) for k in BIG_WEIGHTS}
    theirs = _swap_with_sibling(mine)
    out_g, out_d, out_m, out_v = {}, {}, {}, {}
    for k in BIG_WEIGHTS:
        res = _adamw_big(w[k][0], mine[k], theirs[k], m[k][0], v[k][0], k)
        out_g[k], out_d[k], out_m[k], out_v[k] = (t[None] for t in res)

    pick = lambda tree: {k: tree[k] for k in SMALL_WEIGHTS}
    parts = _share_small(_pack_small(small_grads))
    res = _adamw_small(_pack_small(pick(w)), parts, _pack_small(pick(m)), _pack_small(pick(v)))
    for dst, packed in zip((out_g, out_d, out_m, out_v), res):
        dst.update(_unpack_small(packed, pick(w)))

    return (loss, grad_x[None], *[out_g[k] for k in WEIGHT_ORDER], *[out_d[k] for k in WEIGHT_ORDER],
            *[out_m[k] for k in WEIGHT_ORDER], *[out_v[k] for k in WEIGHT_ORDER])
```

```python
import functools
import math

import jax
import jax.numpy as jnp
from jax import lax
from jax.experimental import pallas as pl
from jax.experimental.pallas import tpu as pltpu

F32 = jnp.float32
BF16 = jnp.bfloat16

EPS = 1e-6
HEAD_DIM = 128
HEADS_PER_GROUP = 4
ATTN_DILATIONS = (1, 4, 16)
ATTN_BLK = 128
N_ATTN_HEADS = HEADS_PER_GROUP * len(ATTN_DILATIONS)
GROUP_W = HEADS_PER_GROUP * HEAD_DIM
HQ = N_ATTN_HEADS * HEAD_DIM
SSM_GROUP = 16
SSM_STATE = 64
SSM_TILE_CH = 128
SSM_TILE_ST = SSM_TILE_CH // SSM_GROUP * SSM_STATE
ADAM_LR = 0.001
ADAM_B1 = 0.9
ADAM_B2 = 0.999
ADAM_EPS = 1e-08
ADAM_WD = 0.01
ADAM_STEP = 10
NEG_BIG = -1e30
V7X_VMEM_LIMIT = 56 * 1024 * 1024
MESH_AXES = ("x", "y", "c")
N_CHIPS = 4


def _pick(n, cands):
    for c in cands:
        if n % c == 0:
            return c
    raise ValueError(f"no tile of {cands} divides {n}")


def _params(sem):
    return pltpu.CompilerParams(dimension_semantics=sem, vmem_limit_bytes=V7X_VMEM_LIMIT)


HBM = pl.BlockSpec(memory_space=pl.ANY)
MESH = pl.DeviceIdType.MESH


class _Side:
    def __init__(self, srcs, out_shapes, sem_shapes, build):
        self.srcs, self.out_shapes, self.sem_shapes, self.build = list(srcs), list(out_shapes), list(sem_shapes), build

    def start(self, src, dst, sems):
        local, sends, _ = self.build(src, dst, sems)
        for cp in local + sends:
            cp.start()

    def wait(self, src, dst, sems):
        local, sends, arrivals = self.build(src, dst, sems)
        for cp in arrivals:
            cp.wait_recv()
        for cp in sends:
            cp.wait_send()
        for cp in local:
            cp.wait()


def _call(body, *, name, grid, in_specs, out_specs, out_shape, semantics, args, scratch_shapes=(), side=None, **kw):
    in_specs, out_specs, out_shape, scratch_shapes = list(in_specs), list(out_specs), list(out_shape), list(scratch_shapes)
    if side is None:
        res = pl.pallas_call(body, name=name, grid=grid, in_specs=in_specs, out_specs=out_specs, out_shape=out_shape,
                             scratch_shapes=scratch_shapes, compiler_params=_params(semantics), **kw)(*args)
        return list(res), []
    n_in, n_out, n_scr = len(in_specs), len(out_specs), len(scratch_shapes)
    ns_in, ns_out = len(side.srcs), len(side.out_shapes)

    def carrying(*refs):
        ins, s_in = refs[:n_in], refs[n_in:n_in + ns_in]
        o0 = n_in + ns_in
        outs, s_out = refs[o0:o0 + n_out], refs[o0 + n_out:o0 + n_out + ns_out]
        c0 = o0 + n_out + ns_out
        scr, sems = refs[c0:c0 + n_scr], refs[c0 + n_scr:]
        ids = [pl.program_id(a) for a in range(len(grid))]
        first = functools.reduce(jnp.logical_and, [i == 0 for i in ids])
        last = functools.reduce(jnp.logical_and, [i == g - 1 for i, g in zip(ids, grid)])

        @pl.when(first)
        def _():
            side.start(s_in, s_out, sems)

        body(*ins, *outs, *scr)

        @pl.when(last)
        def _():
            side.wait(s_in, s_out, sems)

    res = pl.pallas_call(
        carrying, name=name, grid=grid, in_specs=in_specs + [HBM] * ns_in, out_specs=out_specs + [HBM] * ns_out,
        out_shape=out_shape + side.out_shapes, scratch_shapes=scratch_shapes + side.sem_shapes,
        compiler_params=pltpu.CompilerParams(dimension_semantics=("arbitrary",) * len(grid),
                                             vmem_limit_bytes=V7X_VMEM_LIMIT, has_side_effects=True), **kw,
    )(*args, *side.srcs)
    return list(res[:n_out]), list(res[n_out:])


def _run_side(side, name):
    ns, no = len(side.srcs), len(side.out_shapes)

    def body(*refs):
        src, dst, sems = refs[:ns], refs[ns:ns + no], refs[ns + no:]
        side.start(src, dst, sems)
        side.wait(src, dst, sems)

    return list(pl.pallas_call(body, name=name, in_specs=[HBM] * ns, out_specs=[HBM] * no, out_shape=side.out_shapes,
                               scratch_shapes=side.sem_shapes,
                               compiler_params=pltpu.CompilerParams(has_side_effects=True))(*side.srcs))


_DOT_DIMS = {"nn": (((1,), (0,)), ((), ())), "nt": (((1,), (1,)), ((), ())), "tn": (((0,), (0,)), ((), ()))}


MM_VMEM_BUDGET = 44 * 1024 * 1024
MM_STEP_BYTES = 1 << 20


def _size(dtype):
    return jnp.dtype(dtype).itemsize


def _mm_fused(as_, bs, pairs, mode, out_dtypes, name, extras=(), epilogue=None, side=None):
    M = as_[0].shape[0]
    N = bs[0].shape[1] if mode == "nn" else bs[0].shape[0]
    ks_a = [a.shape[1] for a in as_]
    ks_b = [b.shape[0] if mode == "nn" else b.shape[1] for b in bs]
    if epilogue is None:
        epilogue = lambda rs, es: rs
    offs = [off for _, off in extras]
    best = None
    for tm in (2048, 1024, 512, 256, 128):
        for tn in (2048, 1024, 512, 256, 128):
            if M % tm or N % tn or any(off % tn for off in offs):
                continue
            vmem = (sum(2 * tm * k * 2 for k in ks_a) + sum(2 * k * tn * 2 for k in ks_b)
                    + sum(2 * tm * tn * _size(d) for d in out_dtypes) + sum(2 * tm * tn * _size(e.dtype) for e, _ in extras)
                    + len(pairs) * tm * tn * 4)
            cost = sum(k * N * 2 for k in ks_b) * (M // tm) + (M // tm) * (N // tn) * MM_STEP_BYTES
            if vmem <= MM_VMEM_BUDGET and (best is None or cost < best[0]):
                best = (cost, tm, tn)
    _, tm, tn = best
    na, nb, ne, no = len(as_), len(bs), len(extras), len(out_dtypes)
    dims = _DOT_DIMS[mode]

    def body(*refs):
        a_refs, b_refs = refs[:na], refs[na:na + nb]
        e_refs, o_refs = refs[na + nb:na + nb + ne], refs[na + nb + ne:]
        rs = [lax.dot_general(a_refs[ai][...], b_refs[bi][...], dims, preferred_element_type=F32) for ai, bi in pairs]
        outs = epilogue(rs, [e[...] for e in e_refs])
        for o_ref, o in zip(o_refs, outs):
            o_ref[...] = o.astype(o_ref.dtype)

    a_specs = [pl.BlockSpec((tm, k), lambda i, j: (i, 0)) for k in ks_a]
    if mode == "nn":
        b_specs = [pl.BlockSpec((k, tn), lambda i, j: (0, j)) for k in ks_b]
    else:
        b_specs = [pl.BlockSpec((tn, k), lambda i, j: (j, 0)) for k in ks_b]
    e_specs = [pl.BlockSpec((tm, tn), lambda i, j, o=off // tn: (i, o + j)) for off in offs]
    o_spec = pl.BlockSpec((tm, tn), lambda i, j: (i, j))
    outs, carried = _call(
        body, name=name, grid=(M // tm, N // tn), in_specs=a_specs + b_specs + e_specs, out_specs=[o_spec] * no,
        out_shape=[jax.ShapeDtypeStruct((M, N), d) for d in out_dtypes], semantics=("parallel", "arbitrary"),
        args=[*as_, *bs, *[e for e, _ in extras]], side=side)
    return outs if side is None else (outs, carried)


def _mm(a, b, mode, out_dtype, name, side=None):
    res = _mm_fused([a], [b], [(0, 0)], mode, [out_dtype], name, side=side)
    return res[0] if side is None else (res[0][0], res[1])


def _mm_kloop(a, b, mode, out_dtype, name, add=None, side=None):
    if mode == "nn":
        (M, K), (_, N) = a.shape, b.shape
    elif mode == "nt":
        (M, K), (N, _) = a.shape, b.shape
    else:
        (K, M), (_, N) = a.shape, b.shape
    best = None
    for tm in (2816, 2048, 1408, 1024, 512, 256, 128):
        for tn in (2816, 2432, 2048, 1408, 1024, 512, 256, 128):
            for tk in (1024, 512, 256, 128):
                if M % tm or N % tn or K % tk:
                    continue
                vmem = (2 * tm * tn * 4 + 2 * tm * tn * _size(out_dtype) + 2 * tk * (tm + tn) * 2
                        + (2 * tm * tn * 4 if add is not None else 0))
                steps = (M // tm) * (N // tn) * (K // tk)
                cost = K * M * 2 * (N // tn) + K * N * 2 * (M // tm) + steps * MM_STEP_BYTES
                if vmem <= MM_VMEM_BUDGET and (best is None or cost < best[0]):
                    best = (cost, tm, tn, tk)
    _, tm, tn, tk = best
    nk = K // tk
    dims = _DOT_DIMS[mode]

    def body(*refs):
        if add is None:
            a_ref, b_ref, o_ref, acc_ref = refs
        else:
            a_ref, b_ref, add_ref, o_ref, acc_ref = refs
        k = pl.program_id(2)

        @pl.when(k == 0)
        def _():
            acc_ref[...] = jnp.zeros_like(acc_ref)

        acc_ref[...] += lax.dot_general(a_ref[...], b_ref[...], dims, preferred_element_type=F32)

        @pl.when(k == nk - 1)
        def _():
            r = acc_ref[...]
            if add is not None:
                r = r + add_ref[...]
            o_ref[...] = r.astype(o_ref.dtype)

    a_spec = pl.BlockSpec((tk, tm), lambda i, j, k: (k, i)) if mode == "tn" else pl.BlockSpec((tm, tk), lambda i, j, k: (i, k))
    b_spec = pl.BlockSpec((tn, tk), lambda i, j, k: (j, k)) if mode == "nt" else pl.BlockSpec((tk, tn), lambda i, j, k: (k, j))
    o_spec = pl.BlockSpec((tm, tn), lambda i, j, k: (i, j))
    outs, carried = _call(
        body, name=name, grid=(M // tm, N // tn, nk),
        in_specs=[a_spec, b_spec] + ([o_spec] if add is not None else []), out_specs=[o_spec],
        out_shape=[jax.ShapeDtypeStruct((M, N), out_dtype)], scratch_shapes=[pltpu.VMEM((tm, tn), F32)],
        semantics=("parallel", "parallel", "arbitrary"), args=(a, b) + ((add,) if add is not None else ()), side=side)
    return outs[0] if side is None else (outs[0], carried)


def _sigmoid(v):
    return 1.0 / (1.0 + jnp.exp(-v))


_GELU_C = math.sqrt(2.0 / math.pi)


def _gelu(v):
    return 0.5 * v * (1.0 + jnp.tanh(_GELU_C * (v + 0.044715 * v * v * v)))


def _gelu_grad(v):
    t = jnp.tanh(_GELU_C * (v + 0.044715 * v * v * v))
    return 0.5 * (1.0 + t) + 0.5 * v * (1.0 - t * t) * _GELU_C * (1.0 + 3.0 * 0.044715 * v * v)


def _rms(v, gain):
    r = lax.rsqrt(jnp.mean(v * v, axis=-1, keepdims=True) + EPS)
    return v * r * gain


def _rms_bwd(v, gain, dy):
    r = lax.rsqrt(jnp.mean(v * v, axis=-1, keepdims=True) + EPS)
    a = dy * gain
    dv = r * a - v * (r * r * r) * jnp.mean(a * v, axis=-1, keepdims=True)
    return dv, dy * v * r


def _row_tile(s):
    return _pick(s, (256, 128, 64, 8))


def _norm_in(x, gain):
    s, d = x.shape
    tr = _row_tile(s)

    def body(x_ref, g_ref, h_ref):
        h_ref[...] = _rms(x_ref[...], g_ref[...]).astype(BF16)

    row = pl.BlockSpec((tr, d), lambda i: (i, 0))
    vec = pl.BlockSpec((1, d), lambda i: (0, 0))
    return pl.pallas_call(body, name="norm_in", grid=(s // tr,), in_specs=[row, vec], out_specs=row,
                          out_shape=jax.ShapeDtypeStruct((s, d), BF16), compiler_params=_params(("parallel",)))(x, gain)


def _norm_mid(x, mo, g_post, g_pre):
    s, d = x.shape
    tr = _row_tile(s)

    def body(x_ref, mo_ref, g2_ref, g3_ref, x2_ref, h2_ref):
        x2 = x_ref[...] + _rms(mo_ref[...], g2_ref[...])
        x2_ref[...] = x2
        h2_ref[...] = _rms(x2, g3_ref[...]).astype(BF16)

    row = pl.BlockSpec((tr, d), lambda i: (i, 0))
    vec = pl.BlockSpec((1, d), lambda i: (0, 0))
    return pl.pallas_call(
        body, name="norm_mid", grid=(s // tr,), in_specs=[row, row, vec, vec], out_specs=[row, row],
        out_shape=[jax.ShapeDtypeStruct((s, d), F32), jax.ShapeDtypeStruct((s, d), BF16)],
        compiler_params=_params(("parallel",)))(x, mo, g_post, g_pre)


def _loss_head(x2, f, g_post, target):
    s, d = x2.shape
    tr = _row_tile(s)

    def body(x2_ref, f_ref, g_ref, t_ref, loss_ref, dout_ref, df_ref, dg_ref):
        @pl.when(pl.program_id(0) == 0)
        def _():
            loss_ref[...] = jnp.zeros_like(loss_ref)
            dg_ref[...] = jnp.zeros_like(dg_ref)

        fv = f_ref[...]
        g = g_ref[...]
        err = x2_ref[...] + _rms(fv, g) - t_ref[...]
        loss_ref[...] += 0.5 * jnp.sum(jnp.mean(err * err, axis=-1, keepdims=True), axis=0, keepdims=True)
        dout = err * (1.0 / d)
        dout_ref[...] = dout
        df, dg = _rms_bwd(fv, g, dout)
        df_ref[...] = df.astype(BF16)
        dg_ref[...] += jnp.sum(dg, axis=0, keepdims=True)

    row = pl.BlockSpec((tr, d), lambda i: (i, 0))
    vec = pl.BlockSpec((1, d), lambda i: (0, 0))
    one = pl.BlockSpec((1, 1), lambda i: (0, 0))
    return pl.pallas_call(
        body, name="loss_head", grid=(s // tr,), in_specs=[row, row, vec, row], out_specs=[one, row, row, vec],
        out_shape=[jax.ShapeDtypeStruct((1, 1), F32), jax.ShapeDtypeStruct((s, d), F32),
                   jax.ShapeDtypeStruct((s, d), BF16), jax.ShapeDtypeStruct((1, d), F32)],
        compiler_params=_params(("arbitrary",)))(x2, f, g_post, target)


def _norm_mid_bwd(x2, mo, g_post, g_pre, dout, dh2):
    s, d = x2.shape
    tr = _row_tile(s)

    def body(x2_ref, mo_ref, g2_ref, g3_ref, dout_ref, dh2_ref, dx2_ref, dmo_ref, dg2_ref, dg3_ref):
        @pl.when(pl.program_id(0) == 0)
        def _():
            dg2_ref[...] = jnp.zeros_like(dg2_ref)
            dg3_ref[...] = jnp.zeros_like(dg3_ref)

        dv, dg3 = _rms_bwd(x2_ref[...], g3_ref[...], dh2_ref[...])
        dx2 = dout_ref[...] + dv
        dx2_ref[...] = dx2
        dmo, dg2 = _rms_bwd(mo_ref[...], g2_ref[...], dx2)
        dmo_ref[...] = dmo.astype(BF16)
        dg2_ref[...] += jnp.sum(dg2, axis=0, keepdims=True)
        dg3_ref[...] += jnp.sum(dg3, axis=0, keepdims=True)

    row = pl.BlockSpec((tr, d), lambda i: (i, 0))
    vec = pl.BlockSpec((1, d), lambda i: (0, 0))
    return pl.pallas_call(
        body, name="norm_mid_bwd", grid=(s // tr,), in_specs=[row, row, vec, vec, row, row],
        out_specs=[row, row, vec, vec],
        out_shape=[jax.ShapeDtypeStruct((s, d), F32), jax.ShapeDtypeStruct((s, d), BF16),
                   jax.ShapeDtypeStruct((1, d), F32), jax.ShapeDtypeStruct((1, d), F32)],
        compiler_params=_params(("arbitrary",)))(x2, mo, g_post, g_pre, dout, dh2)


def _norm_in_bwd(x, gain, dh, dx2):
    s, d = x.shape
    tr = _row_tile(s)

    def body(x_ref, g_ref, dh_ref, dx2_ref, dx_ref, dg_ref):
        @pl.when(pl.program_id(0) == 0)
        def _():
            dg_ref[...] = jnp.zeros_like(dg_ref)

        dv, dg = _rms_bwd(x_ref[...], g_ref[...], dh_ref[...])
        dx_ref[...] = dx2_ref[...] + dv
        dg_ref[...] += jnp.sum(dg, axis=0, keepdims=True)

    row = pl.BlockSpec((tr, d), lambda i: (i, 0))
    vec = pl.BlockSpec((1, d), lambda i: (0, 0))
    return pl.pallas_call(
        body, name="norm_in_bwd", grid=(s // tr,), in_specs=[row, vec, row, row], out_specs=[row, vec],
        out_shape=[jax.ShapeDtypeStruct((s, d), F32), jax.ShapeDtypeStruct((1, d), F32)],
        compiler_params=_params(("arbitrary",)))(x, gain, dh, dx2)


def _swiglu_epilogue(rs, es):
    g, u = rs
    return [g * _sigmoid(g) * u, g, u]


def _swiglu_bwd_epilogue(rs, es):
    d = rs[0]
    g, u = es[0].astype(F32), es[1].astype(F32)
    sg = _sigmoid(g)
    return [d * u * sg * (1.0 + g * (1.0 - sg)), d * g * sg]


def _sum_epilogue(rs, es):
    return [rs[0] + rs[1]]


def _gates_epilogue(rs, es):
    ab, gv, gg = rs
    ga, gs = es
    return [_sigmoid(ga) * ab + _sigmoid(gs) * gv * _sigmoid(gg), ab, gv, gg]


def _gate_cols(d, gate_off):
    tc = _pick(math.gcd(d, gate_off), (512, 256, 128))
    return tc, gate_off // tc, d // tc


def _merge_gate_a_bwd(z, ab, dmerged, gate_off):
    s, d = ab.shape
    tr = _row_tile(s)
    tc, off, nd = _gate_cols(d, gate_off)

    def body(ga_ref, ab_ref, dm_ref, dga_ref, dab_ref):
        dm = dm_ref[...]
        sa = _sigmoid(ga_ref[...])
        dga_ref[...] = (dm * ab_ref[...] * sa * (1.0 - sa)).astype(BF16)
        dab_ref[...] = (dm * sa).astype(BF16)

    blk = pl.BlockSpec((tr, tc), lambda i, j: (i, j))
    ga = pl.BlockSpec((tr, tc), lambda i, j: (i, off + j))
    return pl.pallas_call(
        body, name="merge_gate_a_bwd", grid=(s // tr, nd), in_specs=[ga, blk, blk], out_specs=[ga, blk],
        out_shape=[jax.ShapeDtypeStruct(z.shape, BF16), jax.ShapeDtypeStruct((s, d), BF16)],
        compiler_params=_params(("parallel", "parallel")))(z, ab, dmerged)


def _merge_gate_s_bwd(z, gv, gg, dmerged, dz, gate_off):
    s, d = gv.shape
    tr = _row_tile(s)
    tc, off, nd = _gate_cols(d, gate_off)

    def body(gs_ref, gv_ref, gg_ref, dm_ref, dz_ref, dgs_ref, dgv_ref, dgg_ref):
        del dz_ref
        ss = _sigmoid(gs_ref[...])
        sg = _sigmoid(gg_ref[...])
        gv_ = gv_ref[...]
        dm = dm_ref[...]
        dgs_ref[...] = (dm * gv_ * sg * ss * (1.0 - ss)).astype(BF16)
        dsb = dm * ss
        dgv_ref[...] = (dsb * sg).astype(BF16)
        dgg_ref[...] = (dsb * gv_ * sg * (1.0 - sg)).astype(BF16)

    blk = pl.BlockSpec((tr, tc), lambda i, j: (i, j))
    gs = pl.BlockSpec((tr, tc), lambda i, j: (i, off + nd + j))
    return pl.pallas_call(
        body, name="merge_gate_s_bwd", grid=(s // tr, nd),
        in_specs=[gs, blk, blk, blk, pl.BlockSpec(memory_space=pl.ANY)], out_specs=[gs, blk, blk],
        out_shape=[jax.ShapeDtypeStruct(z.shape, BF16)] + [jax.ShapeDtypeStruct((s, d), BF16)] * 2,
        input_output_aliases={4: 0},
        compiler_params=_params(("parallel", "parallel")))(z, gv, gg, dmerged, dz)


def _put_cols(dz, src, col_off):
    s, w = src.shape
    tr = _row_tile(s)
    tc = _pick(math.gcd(w, col_off), (512, 256, 128))
    off = col_off // tc

    def body(src_ref, dz_ref, o_ref):
        del dz_ref
        o_ref[...] = src_ref[...].astype(o_ref.dtype)

    return pl.pallas_call(
        body, name="put_cols", grid=(s // tr, w // tc),
        in_specs=[pl.BlockSpec((tr, tc), lambda i, j: (i, j)), pl.BlockSpec(memory_space=pl.ANY)],
        out_specs=pl.BlockSpec((tr, tc), lambda i, j: (i, off + j)),
        out_shape=jax.ShapeDtypeStruct(dz.shape, dz.dtype), input_output_aliases={1: 0},
        compiler_params=_params(("parallel", "parallel")))(src, dz)


ATTN_ROWS = 2048


def _dilate_qkv(z, g, d):
    s = z.shape[0]
    tm = ATTN_ROWS
    per = tm // d
    nh = HEADS_PER_GROUP

    def body(z_ref, o_ref):
        for r in range(d):
            rows = z_ref[...] if d == 1 else z_ref[pl.ds(r, per, stride=d), :]
            o_ref[0, r] = rows.astype(BF16)

    return pl.pallas_call(
        body, name=f"dilate_qkv_{g}", grid=(s // tm, 3, nh),
        in_specs=[pl.BlockSpec((tm, HEAD_DIM), lambda i, w, h: (i, (3 * w + g) * nh + h))],
        out_specs=pl.BlockSpec((1, d, per, HEAD_DIM), lambda i, w, h: (w, 0, i, h)),
        out_shape=jax.ShapeDtypeStruct((3, d, s // d, GROUP_W), BF16),
        compiler_params=_params(("parallel", "parallel", "parallel")))(z)


def _undilate_dqkv(dqkv, dz, g, d):
    s = dz.shape[0]
    tm = ATTN_ROWS
    per = tm // d
    nh = HEADS_PER_GROUP

    def body(i_ref, dz_ref, o_ref, nat_ref):
        del dz_ref
        if d == 1:
            o_ref[...] = i_ref[0, 0]
        else:
            for r in range(d):
                nat_ref[pl.ds(r, per, stride=d), :] = i_ref[0, r].astype(F32)
            o_ref[...] = nat_ref[...].astype(BF16)

    return pl.pallas_call(
        body, name=f"undilate_dqkv_{g}", grid=(s // tm, 3, nh),
        in_specs=[pl.BlockSpec((1, d, per, HEAD_DIM), lambda i, w, h: (w, 0, i, h)),
                  pl.BlockSpec(memory_space=pl.ANY)],
        out_specs=pl.BlockSpec((tm, HEAD_DIM), lambda i, w, h: (i, (3 * w + g) * nh + h)),
        out_shape=jax.ShapeDtypeStruct(dz.shape, dz.dtype), input_output_aliases={1: 0},
        scratch_shapes=[pltpu.VMEM((tm, HEAD_DIM), F32)],
        compiler_params=_params(("parallel", "parallel", "parallel")))(dqkv, dz)


def _alibi_slope(head):
    return 2.0 ** (-8.0 * (head + 1) / N_ATTN_HEADS)


def _band_masks(n, nb):
    qi = lax.broadcasted_iota(jnp.int32, (ATTN_BLK, ATTN_BLK), 0)
    ki = lax.broadcasted_iota(jnp.int32, (ATTN_BLK, ATTN_BLK), 1)
    dist_cur = qi - ki
    dist_prev = ATTN_BLK + qi - ki
    return dist_cur.astype(F32), dist_cur >= 0, dist_prev.astype(F32), dist_prev <= ATTN_BLK


def _dot_nt(a, b):
    return lax.dot_general(a, b, _DOT_DIMS["nt"], preferred_element_type=F32)


def _dot_tn(a, b):
    return lax.dot_general(a, b, _DOT_DIMS["tn"], preferred_element_type=F32)


def _dot(a, b):
    return jnp.dot(a, b, preferred_element_type=F32)


def _scores(q, k, slope_d, dist, valid):
    s = _dot_nt(q, k) * (HEAD_DIM ** -0.5) - slope_d * dist
    return jnp.where(valid, s, NEG_BIG)


def _attn_fwd(qkv, g, d):
    _, _, L, _ = qkv.shape
    nb = L // ATTN_BLK

    def body(q_ref, kc_ref, kp_ref, vc_ref, vp_ref, o_ref, lse_ref):
        n = pl.program_id(1)
        dist_c, valid_c, dist_p, valid_p = _band_masks(n, nb)
        valid_p = jnp.logical_and(valid_p, n > 0)
        for hh in range(HEADS_PER_GROUP):
            cols = slice(hh * HEAD_DIM, (hh + 1) * HEAD_DIM)
            slope_d = _alibi_slope(g * HEADS_PER_GROUP + hh) * d
            q = q_ref[0, 0, :, cols]
            sc = _scores(q, kc_ref[0, 0, :, cols], slope_d, dist_c, valid_c)
            sp = _scores(q, kp_ref[0, 0, :, cols], slope_d, dist_p, valid_p)
            m = jnp.maximum(jnp.max(sc, axis=-1, keepdims=True), jnp.max(sp, axis=-1, keepdims=True))
            ec = jnp.exp(sc - m)
            ep = jnp.exp(sp - m)
            l = jnp.sum(ec, axis=-1, keepdims=True) + jnp.sum(ep, axis=-1, keepdims=True)
            inv = 1.0 / l
            o = _dot((ec * inv).astype(BF16), vc_ref[0, 0, :, cols]) + _dot((ep * inv).astype(BF16), vp_ref[0, 0, :, cols])
            o_ref[0, :, cols] = o
            lse_ref[0, :, cols] = jnp.broadcast_to(m + jnp.log(l), (ATTN_BLK, HEAD_DIM))

    def spec(w, shift):
        return pl.BlockSpec((1, 1, ATTN_BLK, GROUP_W), lambda r, n: (w, r, jnp.maximum(n + shift, 0), 0))

    out = pl.BlockSpec((1, ATTN_BLK, GROUP_W), lambda r, n: (r, n, 0))
    return pl.pallas_call(
        body, name=f"attn_fwd_{g}", grid=(d, nb),
        in_specs=[spec(0, 0), spec(1, 0), spec(1, -1), spec(2, 0), spec(2, -1)], out_specs=[out, out],
        out_shape=[jax.ShapeDtypeStruct((d, L, GROUP_W), F32)] * 2,
        compiler_params=_params(("parallel", "parallel")))(qkv, qkv, qkv, qkv, qkv)


def _attn_bwd(qkv, do, lse, cc, g, d):
    _, _, L, _ = qkv.shape
    nb = L // ATTN_BLK
    scale = HEAD_DIM ** -0.5

    def body(q0_ref, q1_ref, k0_ref, kp_ref, v0_ref, vp_ref, do0_ref, do1_ref, l0_ref, l1_ref, c0_ref, c1_ref, o_ref):
        n = pl.program_id(1)
        dist_c, valid_c, dist_p, valid_p = _band_masks(n, nb)
        valid_b = jnp.logical_and(valid_p, n > 0)
        valid_n = jnp.logical_and(valid_p, n < nb - 1)
        for hh in range(HEADS_PER_GROUP):
            cols = slice(hh * HEAD_DIM, (hh + 1) * HEAD_DIM)
            slope_d = _alibi_slope(g * HEADS_PER_GROUP + hh) * d
            q0, q1 = q0_ref[0, 0, :, cols], q1_ref[0, 0, :, cols]
            k0, kp = k0_ref[0, 0, :, cols], kp_ref[0, 0, :, cols]
            v0, vp = v0_ref[0, 0, :, cols], vp_ref[0, 0, :, cols]
            do0, do1 = do0_ref[0, :, cols], do1_ref[0, :, cols]
            l0, l1 = l0_ref[0, :, cols], l1_ref[0, :, cols]
            c0, c1 = c0_ref[0, :, cols], c1_ref[0, :, cols]
            pa = jnp.exp(_scores(q0, k0, slope_d, dist_c, valid_c) - l0)
            dsa = (pa * (_dot_nt(do0, v0) + c0)).astype(BF16)
            pb = jnp.exp(_scores(q0, kp, slope_d, dist_p, valid_b) - l0)
            dsb = (pb * (_dot_nt(do0, vp) + c0)).astype(BF16)
            pc = jnp.exp(_scores(q1, k0, slope_d, dist_p, valid_n) - l1)
            dsc = (pc * (_dot_nt(do1, v0) + c1)).astype(BF16)
            o_ref[0, 0, :, cols] = ((_dot(dsa, k0) + _dot(dsb, kp)) * scale).astype(BF16)
            o_ref[1, 0, :, cols] = ((_dot_tn(dsa, q0) + _dot_tn(dsc, q1)) * scale).astype(BF16)
            o_ref[2, 0, :, cols] = (_dot_tn(pa.astype(BF16), do0) + _dot_tn(pc.astype(BF16), do1)).astype(BF16)

    def spec(w, shift):
        return pl.BlockSpec((1, 1, ATTN_BLK, GROUP_W), lambda r, n: (w, r, jnp.clip(n + shift, 0, nb - 1), 0))

    def spec3(shift):
        return pl.BlockSpec((1, ATTN_BLK, GROUP_W), lambda r, n: (r, jnp.clip(n + shift, 0, nb - 1), 0))

    return pl.pallas_call(
        body, name=f"attn_bwd_{g}", grid=(d, nb),
        in_specs=[spec(0, 0), spec(0, 1), spec(1, 0), spec(1, -1), spec(2, 0), spec(2, -1),
                  spec3(0), spec3(1), spec3(0), spec3(1), spec3(0), spec3(1)],
        out_specs=pl.BlockSpec((3, 1, ATTN_BLK, GROUP_W), lambda r, n: (0, r, n, 0)),
        out_shape=jax.ShapeDtypeStruct((3, d, L, GROUP_W), BF16),
        compiler_params=_params(("parallel", "parallel")))(qkv, qkv, qkv, qkv, qkv, qkv, do, do, lse, lse, cc, cc)


def _load_natural(refs, nat_refs):
    for g, d in enumerate(ATTN_DILATIONS):
        if d == 1:
            nat_refs[g][...] = refs[g][0]
        else:
            per = ATTN_ROWS // d
            for r in range(d):
                nat_refs[g][pl.ds(r, per, stride=d), :] = refs[g][r]


def _mix_weights(lse_nat):
    l0, l1, l2 = lse_nat[0][...], lse_nat[1][...], lse_nat[2][...]
    m = jnp.maximum(jnp.maximum(l0, l1), l2)
    e0, e1, e2 = jnp.exp(l0 - m), jnp.exp(l1 - m), jnp.exp(l2 - m)
    inv = 1.0 / (e0 + e1 + e2)
    return e0 * inv, e1 * inv, e2 * inv


def _dilated_specs(s):
    return [pl.BlockSpec((d, ATTN_ROWS // d, HEAD_DIM), lambda i, h: (0, i, h)) for d in ATTN_DILATIONS]


NATURAL_SCRATCH = [pltpu.VMEM((ATTN_ROWS, HEAD_DIM), F32)] * (2 * len(ATTN_DILATIONS))


def _attn_merge(outs, lses):
    s = outs[0].shape[0] * outs[0].shape[1]

    def body(o0, o1, o2, l0, l1, l2, a_ref, *nat):
        onat, lnat = nat[:3], nat[3:]
        _load_natural((o0, o1, o2), onat)
        _load_natural((l0, l1, l2), lnat)
        w0, w1, w2 = _mix_weights(lnat)
        a_ref[...] = (w0 * onat[0][...] + w1 * onat[1][...] + w2 * onat[2][...]).astype(BF16)

    return pl.pallas_call(
        body, name="attn_merge", grid=(s // ATTN_ROWS, HEADS_PER_GROUP), in_specs=_dilated_specs(s) * 2,
        out_specs=pl.BlockSpec((ATTN_ROWS, HEAD_DIM), lambda i, h: (i, h)),
        out_shape=jax.ShapeDtypeStruct((s, GROUP_W), BF16), scratch_shapes=NATURAL_SCRATCH,
        compiler_params=_params(("parallel", "parallel")))(*outs, *lses)


def _attn_merge_bwd(outs, lses, dattn):
    s = dattn.shape[0]

    def body(o0, o1, o2, l0, l1, l2, da_ref, do0, do1, do2, c0, c1, c2, *nat):
        onat, lnat = nat[:3], nat[3:]
        _load_natural((o0, o1, o2), onat)
        _load_natural((l0, l1, l2), lnat)
        ws = _mix_weights(lnat)
        da = da_ref[...]
        attn = ws[0] * onat[0][...] + ws[1] * onat[1][...] + ws[2] * onat[2][...]
        tot = jnp.broadcast_to(jnp.sum(da * attn, axis=-1, keepdims=True), (ATTN_ROWS, HEAD_DIM))
        for g, (d, do_ref, c_ref) in enumerate(zip(ATTN_DILATIONS, (do0, do1, do2), (c0, c1, c2))):
            if d == 1:
                do_ref[0] = (ws[g] * da).astype(BF16)
                c_ref[0] = -ws[g] * tot
            else:
                onat[g][...] = ws[g] * da
                lnat[g][...] = -ws[g] * tot
                per = ATTN_ROWS // d
                for r in range(d):
                    do_ref[r] = onat[g][pl.ds(r, per, stride=d), :].astype(BF16)
                    c_ref[r] = lnat[g][pl.ds(r, per, stride=d), :]

    dil = _dilated_specs(s)
    shapes = [jax.ShapeDtypeStruct(o.shape, BF16) for o in outs] + [jax.ShapeDtypeStruct(o.shape, F32) for o in outs]
    return pl.pallas_call(
        body, name="attn_merge_bwd", grid=(s // ATTN_ROWS, HEADS_PER_GROUP),
        in_specs=dil * 2 + [pl.BlockSpec((ATTN_ROWS, HEAD_DIM), lambda i, h: (i, h))], out_specs=dil * 2,
        out_shape=shapes, scratch_shapes=NATURAL_SCRATCH,
        compiler_params=_params(("parallel", "parallel")))(*outs, *lses, dattn)


def _ssm_prepare(a_re, a_im, log_dt, b_re, b_im, c_re, c_im):
    n_g = a_re.shape[0]
    nj = n_g * SSM_GROUP // SSM_TILE_CH
    gpt = SSM_TILE_CH // SSM_GROUP
    dt = jnp.exp(log_dt)[:, None]
    mag = jnp.exp(a_re * dt)
    lr, li = mag * jnp.cos(a_im * dt), mag * jnp.sin(a_im * dt)
    den = a_re * a_re + a_im * a_im
    cr = ((lr - 1.0) * a_re + li * a_im) / den
    ci = (li * a_re - (lr - 1.0) * a_im) / den
    bb_re = cr[..., None] * b_re - ci[..., None] * b_im
    bb_im = cr[..., None] * b_im + ci[..., None] * b_re
    eye = jnp.eye(gpt, dtype=F32)

    def b_tiles(t):
        t = t.transpose(0, 2, 1).reshape(nj, gpt, SSM_GROUP, SSM_STATE)
        return jnp.einsum("jgcp,gh->jgchp", t, eye).reshape(nj, SSM_TILE_CH, SSM_TILE_ST)

    def c_tiles(t):
        t = t.reshape(nj, gpt, SSM_GROUP, SSM_STATE)
        return jnp.einsum("jgcp,gh->jhpgc", t, eye).reshape(nj, SSM_TILE_ST, SSM_TILE_CH)

    lam = jnp.stack([lr.reshape(-1), li.reshape(-1)])
    bmat = jnp.concatenate([b_tiles(bb_re), b_tiles(bb_im)], axis=2)
    cmat = jnp.concatenate([c_tiles(c_re), -c_tiles(c_im)], axis=1)
    return lam, bmat, cmat


def _scan_chunk(xr, xi, lr, li, cr, ci, row):
    t_rows = xr.shape[0]
    first = row == 0
    xr = xr + jnp.where(first, lr * cr - li * ci, 0.0)
    xi = xi + jnp.where(first, lr * ci + li * cr, 0.0)
    pr, pi = lr, li
    shift = 1
    while shift < t_rows:
        keep = row >= shift
        sr = jnp.where(keep, pltpu.roll(xr, shift, 0), 0.0)
        si = jnp.where(keep, pltpu.roll(xi, shift, 0), 0.0)
        xr, xi = xr + pr * sr - pi * si, xi + pr * si + pi * sr
        pr, pi = pr * pr - pi * pi, 2.0 * pr * pi
        shift *= 2
    return xr, xi


def _rscan_chunk(gr, gi, lr, li, cr, ci, row):
    t_rows = gr.shape[0]
    last = row == t_rows - 1
    gr = gr + jnp.where(last, lr * cr + li * ci, 0.0)
    gi = gi + jnp.where(last, lr * ci - li * cr, 0.0)
    pr, pi = lr, li
    shift = 1
    while shift < t_rows:
        keep = row < t_rows - shift
        sr = jnp.where(keep, pltpu.roll(gr, t_rows - shift, 0), 0.0)
        si = jnp.where(keep, pltpu.roll(gi, t_rows - shift, 0), 0.0)
        gr, gi = gr + pr * sr + pi * si, gi + pr * si - pi * sr
        pr, pi = pr * pr - pi * pi, 2.0 * pr * pi
        shift *= 2
    return gr, gi


def _ssm_dims(z, bmat, u_off):
    s = z.shape[0]
    nj = bmat.shape[0]
    w = nj * SSM_TILE_CH
    t_rows = _pick(s, (256, 128))
    ub = _pick(math.gcd(w, u_off), (512, 256, 128))
    return s, nj, w, nj * SSM_TILE_ST, t_rows, ub


def _u_tile(u_refs, ub, j):
    col = j * SSM_TILE_CH
    return u_refs[col // ub][:, col % ub:col % ub + SSM_TILE_CH]


def _ssm_fwd(z, bmat, cmat, lam, dskip, u_off, side=None):
    s, nj, w, ns, t_rows, ub = _ssm_dims(z, bmat, u_off)
    nub = w // ub

    def body(*refs):
        u_refs = refs[:nub]
        b_ref, c_ref, lam_ref, d_ref, y_ref, yg_ref, xin_ref, carry_ref = refs[nub:]

        @pl.when(pl.program_id(0) == 0)
        def _():
            carry_ref[...] = jnp.zeros_like(carry_ref)

        xin_ref[0] = carry_ref[...]
        row = lax.broadcasted_iota(jnp.int32, (t_rows, SSM_TILE_ST), 0)
        for j in range(nj):
            st = slice(j * SSM_TILE_ST, (j + 1) * SSM_TILE_ST)
            ch = slice(j * SSM_TILE_CH, (j + 1) * SSM_TILE_CH)
            uj = _u_tile(u_refs, ub, j)
            bu = _dot(uj.astype(BF16), b_ref[j])
            xr, xi = _scan_chunk(bu[:, :SSM_TILE_ST], bu[:, SSM_TILE_ST:], lam_ref[0:1, st], lam_ref[1:2, st],
                                 carry_ref[0:1, st], carry_ref[1:2, st], row)
            carry_ref[0:1, st] = xr[t_rows - 1:t_rows, :]
            carry_ref[1:2, st] = xi[t_rows - 1:t_rows, :]
            xs = jnp.concatenate([xr, xi], axis=1).astype(BF16)
            yj = _dot(xs, c_ref[j]) + d_ref[:, ch] * uj
            y_ref[:, ch] = yj
            yg_ref[:, ch] = _gelu(yj).astype(BF16)

    u_specs = [pl.BlockSpec((t_rows, ub), lambda c, k=k: (c, u_off // ub + k)) for k in range(nub)]
    full3 = lambda shape: pl.BlockSpec(shape, lambda c: (0, 0, 0))
    full2 = lambda shape: pl.BlockSpec(shape, lambda c: (0, 0))
    rows = pl.BlockSpec((t_rows, w), lambda c: (c, 0))
    outs, carried = _call(
        body, name="ssm_fwd", grid=(s // t_rows,),
        in_specs=u_specs + [full3(bmat.shape), full3(cmat.shape), full2(lam.shape), full2(dskip.shape)],
        out_specs=[rows, rows, pl.BlockSpec((1, 2, ns), lambda c: (c, 0, 0))],
        out_shape=[jax.ShapeDtypeStruct((s, w), F32), jax.ShapeDtypeStruct((s, w), BF16),
                   jax.ShapeDtypeStruct((s // t_rows, 2, ns), F32)],
        scratch_shapes=[pltpu.VMEM((2, ns), F32)], semantics=("arbitrary",),
        args=[*([z] * nub), bmat, cmat, lam, dskip], side=side)
    return outs if side is None else (outs, carried)


def _ssm_bwd(z, y, dyg, xin, bmat, cmat, lam, dskip, u_off, side=None):
    s, nj, w, ns, t_rows, ub = _ssm_dims(z, bmat, u_off)
    nub = w // ub
    nc = s // t_rows

    def body(*refs):
        u_refs = refs[:nub]
        (y_ref, dyg_ref, xin_ref, b_ref, c_ref, lam_ref, d_ref,
         du_ref, db_ref, dc_ref, dlam_ref, dd_ref, carry_ref) = refs[nub:]

        @pl.when(pl.program_id(0) == 0)
        def _():
            carry_ref[...] = jnp.zeros_like(carry_ref)
            db_ref[...] = jnp.zeros_like(db_ref)
            dc_ref[...] = jnp.zeros_like(dc_ref)
            dlam_ref[...] = jnp.zeros_like(dlam_ref)
            dd_ref[...] = jnp.zeros_like(dd_ref)

        row = lax.broadcasted_iota(jnp.int32, (t_rows, SSM_TILE_ST), 0)
        for j in range(nj):
            st = slice(j * SSM_TILE_ST, (j + 1) * SSM_TILE_ST)
            ch = slice(j * SSM_TILE_CH, (j + 1) * SSM_TILE_CH)
            uj = _u_tile(u_refs, ub, j)
            ujb = uj.astype(BF16)
            lr, li = lam_ref[0:1, st], lam_ref[1:2, st]
            cr, ci = xin_ref[0, 0:1, st], xin_ref[0, 1:2, st]
            bu = _dot(ujb, b_ref[j])
            xr, xi = _scan_chunk(bu[:, :SSM_TILE_ST], bu[:, SSM_TILE_ST:], lr, li, cr, ci, row)
            dyj = dyg_ref[:, ch] * _gelu_grad(y_ref[:, ch])
            dyb = dyj.astype(BF16)
            gin = _dot_nt(dyb, c_ref[j])
            gr, gi = _rscan_chunk(gin[:, :SSM_TILE_ST], gin[:, SSM_TILE_ST:], lr, li,
                                  carry_ref[0:1, st], carry_ref[1:2, st], row)
            carry_ref[0:1, st] = gr[0:1, :]
            carry_ref[1:2, st] = gi[0:1, :]
            first = row == 0
            pxr = jnp.where(first, cr, pltpu.roll(xr, 1, 0))
            pxi = jnp.where(first, ci, pltpu.roll(xi, 1, 0))
            dlam_ref[0:1, st] += jnp.sum(gr * pxr + gi * pxi, axis=0, keepdims=True)
            dlam_ref[1:2, st] += jnp.sum(gi * pxr - gr * pxi, axis=0, keepdims=True)
            gx = jnp.concatenate([gr, gi], axis=1).astype(BF16)
            xs = jnp.concatenate([xr, xi], axis=1).astype(BF16)
            du_ref[:, ch] = (_dot_nt(gx, b_ref[j]) + d_ref[:, ch] * dyj).astype(BF16)
            db_ref[j] += _dot_tn(ujb, gx)
            dc_ref[j] += _dot_tn(xs, dyb)
            dd_ref[:, ch] += jnp.sum(dyj * uj, axis=0, keepdims=True)

    rev = lambda c: nc - 1 - c
    u_specs = [pl.BlockSpec((t_rows, ub), lambda c, k=k: (rev(c), u_off // ub + k)) for k in range(nub)]
    full3 = lambda shape: pl.BlockSpec(shape, lambda c: (0, 0, 0))
    full2 = lambda shape: pl.BlockSpec(shape, lambda c: (0, 0))
    rows = pl.BlockSpec((t_rows, w), lambda c: (rev(c), 0))
    outs, carried = _call(
        body, name="ssm_bwd", grid=(nc,),
        in_specs=u_specs + [rows, rows, pl.BlockSpec((1, 2, ns), lambda c: (rev(c), 0, 0)),
                            full3(bmat.shape), full3(cmat.shape), full2(lam.shape), full2(dskip.shape)],
        out_specs=[rows, full3(bmat.shape), full3(cmat.shape), full2(lam.shape), full2(dskip.shape)],
        out_shape=[jax.ShapeDtypeStruct((s, w), BF16), jax.ShapeDtypeStruct(bmat.shape, F32),
                   jax.ShapeDtypeStruct(cmat.shape, F32), jax.ShapeDtypeStruct(lam.shape, F32),
                   jax.ShapeDtypeStruct(dskip.shape, F32)],
        scratch_shapes=[pltpu.VMEM((2, ns), F32)], semantics=("arbitrary",),
        args=[*([z] * nub), y, dyg, xin, bmat, cmat, lam, dskip], side=side)
    return outs if side is None else (outs, carried)


def _adam_math(w, g, m, v):
    m = ADAM_B1 * m + (1.0 - ADAM_B1) * g
    v = ADAM_B2 * v + (1.0 - ADAM_B2) * (g * g)
    m_hat = m / (1.0 - ADAM_B1 ** ADAM_STEP)
    v_hat = v / (1.0 - ADAM_B2 ** ADAM_STEP)
    delta = -ADAM_LR * (m_hat / (jnp.sqrt(v_hat) + ADAM_EPS) + ADAM_WD * w)
    return delta, m, v


def _adam_rows(r, c):
    for tr in (512, 256, 128, 64, 32, 16, 8):
        if r % tr == 0 and tr * c * 4 <= (1 << 20):
            return tr
    return r


def _adamw_big(w, p_mine, p_sib, m, v, name):
    r, c = w.shape
    tr = _adam_rows(r, c)

    def body(w_ref, a_ref, b_ref, m_ref, v_ref, g_ref, d_ref, nm_ref, nv_ref):
        g = a_ref[...] + b_ref[...]
        g_ref[...] = g
        d_ref[...], nm_ref[...], nv_ref[...] = _adam_math(w_ref[...], g, m_ref[...], v_ref[...])

    blk = pl.BlockSpec((tr, c), lambda i: (i, 0))
    return pl.pallas_call(body, name=f"adamw_{name}", grid=(r // tr,), in_specs=[blk] * 5, out_specs=[blk] * 4,
                          out_shape=[jax.ShapeDtypeStruct((r, c), F32)] * 4,
                          compiler_params=_params(("parallel",)))(w, p_mine, p_sib, m, v)


def _adamw_small(w, parts, m, v):
    r, c = w.shape
    n_dev = parts.shape[0]

    def body(w_ref, p_ref, m_ref, v_ref, g_ref, d_ref, nm_ref, nv_ref):
        g = p_ref[0]
        for k in range(1, n_dev):
            g = g + p_ref[k]
        g_ref[...] = g
        d_ref[...], nm_ref[...], nv_ref[...] = _adam_math(w_ref[...], g, m_ref[...], v_ref[...])

    blk = pl.BlockSpec((r, c), lambda i: (0, 0))
    return pl.pallas_call(body, name="adamw_small", grid=(1,),
                          in_specs=[blk, pl.BlockSpec((n_dev, r, c), lambda i: (0, 0, 0)), blk, blk],
                          out_specs=[blk] * 4, out_shape=[jax.ShapeDtypeStruct((r, c), F32)] * 4,
                          compiler_params=_params(("arbitrary",)))(w, parts, m, v)


def _cast_bf16(w, name):
    r, c = w.shape
    tr = _adam_rows(r, c)

    def body(w_ref, o_ref):
        o_ref[...] = w_ref[...].astype(BF16)

    blk = pl.BlockSpec((tr, c), lambda i: (i, 0))
    return pl.pallas_call(body, name=f"cast_{name}", grid=(r // tr,), in_specs=[blk], out_specs=blk,
                          out_shape=jax.ShapeDtypeStruct((r, c), BF16), compiler_params=_params(("parallel",)))(w)


def _sum_slots(recv, name):
    _, r, c = recv.shape
    tr = _adam_rows(r, c)

    def body(p_ref, o_ref):
        acc = p_ref[0].astype(F32)
        for k in range(1, N_CHIPS):
            acc = acc + p_ref[k].astype(F32)
        o_ref[...] = acc

    return pl.pallas_call(body, name=f"sum_{name}", grid=(r // tr,),
                          in_specs=[pl.BlockSpec((N_CHIPS, tr, c), lambda i: (0, i, 0))],
                          out_specs=pl.BlockSpec((tr, c), lambda i: (i, 0)),
                          out_shape=jax.ShapeDtypeStruct((r, c), F32), compiler_params=_params(("parallel",)))(recv)


BIG_WEIGHTS = ("w_in", "w_attn_up", "w_glu_v", "w_glu_g", "w_out", "w_ffn_gate", "w_ffn_up", "w_ffn_down")
COL_SHARDED = ("w_in", "w_attn_up", "w_glu_v", "w_glu_g", "w_ffn_gate", "w_ffn_up")


def _shard_of(ref, name, j, shard_shape):
    r, c = shard_shape
    if name in COL_SHARDED:
        return ref.at[:, pl.ds(pl.multiple_of(j * c, 128), c)]
    return ref.at[pl.ds(pl.multiple_of(j * r, 8), r), :]


def _other_chips():
    x, y = lax.axis_index("x"), lax.axis_index("y")
    return [(1 - x, y), (x, 1 - y), (1 - x, 1 - y)]


def _three_sems(n):
    return [pltpu.SemaphoreType.DMA((n, 3)), pltpu.SemaphoreType.DMA((n, 3)), pltpu.SemaphoreType.DMA((n,))]


def _gather_side(shards):
    names = list(shards)
    n = len(names)
    full_shapes = []
    for k in names:
        r, c = shards[k].shape
        full_shapes.append((r, c * N_CHIPS) if k in COL_SHARDED else (r * N_CHIPS, c))

    def build(src, dst, sems):
        send_sems, recv_sems, local_sems = sems
        x, y, c = lax.axis_index("x"), lax.axis_index("y"), lax.axis_index("c")
        me = 2 * x + y
        locals_, sends, arrivals = [], [], []
        for i, k in enumerate(names):
            shape = shards[k].shape
            locals_.append(pltpu.make_async_copy(src[i], _shard_of(dst[i], k, me, shape), local_sems.at[i]))
            for p, (px, py) in enumerate(_other_chips()):
                sends.append(pltpu.make_async_remote_copy(
                    src_ref=src[i], dst_ref=_shard_of(dst[i], k, me, shape), send_sem=send_sems.at[i, p],
                    recv_sem=recv_sems.at[i, p], device_id=(px, py, c), device_id_type=MESH))
                arrivals.append(pltpu.make_async_remote_copy(
                    src_ref=src[i], dst_ref=_shard_of(dst[i], k, 2 * px + py, shape), send_sem=send_sems.at[i, p],
                    recv_sem=recv_sems.at[i, p], device_id=(px, py, c), device_id_type=MESH))
        return locals_, sends, arrivals

    return _Side([shards[k] for k in names], [jax.ShapeDtypeStruct(s, BF16) for s in full_shapes], _three_sems(n), build)


def _scatter_side(grads, shard_shapes):
    names = list(grads)
    n = len(names)

    def build(src, dst, sems):
        send_sems, recv_sems, local_sems = sems
        x, y, c = lax.axis_index("x"), lax.axis_index("y"), lax.axis_index("c")
        me = 2 * x + y
        locals_, sends, arrivals = [], [], []
        for i, k in enumerate(names):
            shape = shard_shapes[k]
            locals_.append(pltpu.make_async_copy(_shard_of(src[i], k, me, shape), dst[i].at[me], local_sems.at[i]))
            for p, (px, py) in enumerate(_other_chips()):
                peer = 2 * px + py
                sends.append(pltpu.make_async_remote_copy(
                    src_ref=_shard_of(src[i], k, peer, shape), dst_ref=dst[i].at[me], send_sem=send_sems.at[i, p],
                    recv_sem=recv_sems.at[i, p], device_id=(px, py, c), device_id_type=MESH))
                arrivals.append(pltpu.make_async_remote_copy(
                    src_ref=_shard_of(src[i], k, peer, shape), dst_ref=dst[i].at[peer], send_sem=send_sems.at[i, p],
                    recv_sem=recv_sems.at[i, p], device_id=(px, py, c), device_id_type=MESH))
        return locals_, sends, arrivals

    return _Side([grads[k] for k in names],
                 [jax.ShapeDtypeStruct((N_CHIPS,) + tuple(shard_shapes[k]), BF16) for k in names], _three_sems(n), build)


def _swap_with_sibling(parts):
    names = list(parts)
    n = len(names)

    def body(*refs):
        src, dst = refs[:n], refs[n:2 * n]
        send_sems, recv_sems = refs[2 * n:]
        sibling = (lax.axis_index("x"), lax.axis_index("y"), 1 - lax.axis_index("c"))
        copies = []
        for i in range(n):
            cp = pltpu.make_async_remote_copy(src_ref=src[i], dst_ref=dst[i], send_sem=send_sems.at[i],
                                              recv_sem=recv_sems.at[i], device_id=sibling, device_id_type=MESH)
            cp.start()
            copies.append(cp)
        for cp in copies:
            cp.wait_recv()
        for cp in copies:
            cp.wait_send()

    outs = pl.pallas_call(
        body, name="swap_with_sibling", in_specs=[HBM] * n, out_specs=[HBM] * n,
        out_shape=[jax.ShapeDtypeStruct(parts[k].shape, F32) for k in names],
        scratch_shapes=[pltpu.SemaphoreType.DMA((n,)), pltpu.SemaphoreType.DMA((n,))],
        compiler_params=pltpu.CompilerParams(has_side_effects=True),
    )(*[parts[k] for k in names])
    return dict(zip(names, outs))


def _share_small(packed):
    r, c = packed.shape
    n_dev = 8

    def body(src, dst, send_sems, recv_sems, local_sem):
        x, y, cc = lax.axis_index("x"), lax.axis_index("y"), lax.axis_index("c")
        me = 4 * x + 2 * y + cc
        own = pltpu.make_async_copy(src, dst.at[me], local_sem)
        own.start()
        sends, arrivals = [], []
        p = 0
        for fx in range(2):
            for fy in range(2):
                for fc in range(2):
                    if fx == fy == fc == 0:
                        continue
                    px, py, pc = x ^ fx, y ^ fy, cc ^ fc
                    out = pltpu.make_async_remote_copy(src_ref=src, dst_ref=dst.at[me], send_sem=send_sems.at[p],
                                                       recv_sem=recv_sems.at[p], device_id=(px, py, pc), device_id_type=MESH)
                    out.start()
                    sends.append(out)
                    arrivals.append(pltpu.make_async_remote_copy(
                        src_ref=src, dst_ref=dst.at[4 * px + 2 * py + pc], send_sem=send_sems.at[p],
                        recv_sem=recv_sems.at[p], device_id=(px, py, pc), device_id_type=MESH))
                    p += 1
        for a in arrivals:
            a.wait_recv()
        for cp in sends:
            cp.wait_send()
        own.wait()

    return pl.pallas_call(
        body, name="share_small", in_specs=[HBM], out_specs=HBM,
        out_shape=jax.ShapeDtypeStruct((n_dev, r, c), F32),
        scratch_shapes=[pltpu.SemaphoreType.DMA((7,)), pltpu.SemaphoreType.DMA((7,)), pltpu.SemaphoreType.DMA],
        compiler_params=pltpu.CompilerParams(has_side_effects=True),
    )(packed)


SMALL_WEIGHTS = ("norm_mix_pre", "ssm_a_re", "ssm_a_im", "ssm_log_dt", "ssm_b_re", "ssm_b_im", "ssm_c_re", "ssm_c_im",
                 "ssm_d", "norm_mix_post", "norm_ffn_pre", "norm_ffn_post")
WEIGHT_ORDER = ("norm_mix_pre", "w_in", "w_attn_up", "ssm_a_re", "ssm_a_im", "ssm_log_dt", "ssm_b_re", "ssm_b_im",
                "ssm_c_re", "ssm_c_im", "ssm_d", "w_glu_v", "w_glu_g", "w_out", "norm_mix_post", "norm_ffn_pre",
                "w_ffn_gate", "w_ffn_up", "w_ffn_down", "norm_ffn_post")
PACK_LANES = 128
PACK_ROWS = 8


def _pack_small(arrs):
    flat = jnp.concatenate([arrs[k].reshape(-1) for k in SMALL_WEIGHTS])
    pad = -flat.shape[0] % (PACK_LANES * PACK_ROWS)
    return jnp.pad(flat, (0, pad)).reshape(-1, PACK_LANES)


def _unpack_small(packed, like):
    flat = packed.reshape(-1)
    out, pos = {}, 0
    for k in SMALL_WEIGHTS:
        n = like[k].size
        out[k] = flat[pos:pos + n].reshape(like[k].shape)
        pos += n
    return out


GATHER_BEHIND_IN_PROJ = ("w_attn_up", "w_glu_v", "w_glu_g", "w_out", "w_ffn_gate")
GATHER_BEHIND_SSM = ("w_ffn_up", "w_ffn_down")
SCATTER_BEHIND_SSM = ("w_ffn_down", "w_ffn_gate", "w_ffn_up", "w_out", "w_glu_v", "w_glu_g")
SCATTER_BEHIND_D_H1 = ("w_attn_up", "w_in")


def _local_step(x, target, big, small, shards=None, shard_shapes=None):
    s, d = x.shape
    big = dict(big)
    carry = shards is not None

    def gather_side(names):
        return _gather_side({k: shards[k] for k in names}) if carry else None

    def scatter_side(names):
        return _scatter_side({k: grads[k] for k in names}, shard_shapes) if carry else None
    u_off = 3 * HQ
    gate_off = u_off + d // 2
    g1, g2, g3, g4 = (small[k][0:1] for k in ("norm_mix_pre", "norm_mix_post", "norm_ffn_pre", "norm_ffn_post"))
    ssm_names = ("ssm_a_re", "ssm_a_im", "ssm_log_dt", "ssm_b_re", "ssm_b_im", "ssm_c_re", "ssm_c_im")
    (lam, bmat, cmat), ssm_vjp = jax.vjp(_ssm_prepare, *[small[k][0] for k in ssm_names])
    bmat, cmat = bmat.astype(BF16), cmat.astype(BF16)
    dskip = small["ssm_d"][0:1]

    h1 = _norm_in(x, g1)
    z = _mm(h1, big["w_in"], "nn", F32, "in_proj", side=gather_side(GATHER_BEHIND_IN_PROJ))
    if carry:
        z, got = z
        big.update(zip(GATHER_BEHIND_IN_PROJ, got))
    ssm_out = _ssm_fwd(z, bmat, cmat, lam, dskip, u_off, side=gather_side(GATHER_BEHIND_SSM))
    if carry:
        ssm_out, got = ssm_out
        big.update(zip(GATHER_BEHIND_SSM, got))
    y, yg, xin = ssm_out
    qkv = [_dilate_qkv(z, g, dil) for g, dil in enumerate(ATTN_DILATIONS)]
    outs, lses = zip(*[_attn_fwd(qkv[g], g, dil) for g, dil in enumerate(ATTN_DILATIONS)])
    attn = _attn_merge(outs, lses)
    merged, ab, gv, gg = _mm_fused(
        [attn, yg], [big["w_attn_up"], big["w_glu_v"], big["w_glu_g"]], [(0, 0), (1, 1), (1, 2)], "nn",
        [BF16, F32, F32, F32], "branches_merge", extras=[(z, gate_off), (z, gate_off + d)], epilogue=_gates_epilogue)
    mo = _mm(merged, big["w_out"], "nn", F32, "mix_out")
    x2, h2 = _norm_mid(x, mo, g2, g3)
    act, fg, fu = _mm_fused([h2], [big["w_ffn_gate"], big["w_ffn_up"]], [(0, 0), (0, 1)], "nn", [BF16, BF16, BF16],
                            "ffn_up_act", epilogue=_swiglu_epilogue)
    f = _mm(act, big["w_ffn_down"], "nn", F32, "ffn_down")
    loss, dout, df, dg4 = _loss_head(x2, f, g4, target)

    grads = {}
    dfg, dfu = _mm_fused([df], [big["w_ffn_down"]], [(0, 0)], "nt", [BF16, BF16], "d_ffn_act",
                         extras=[(fg, 0), (fu, 0)], epilogue=_swiglu_bwd_epilogue)
    grads["w_ffn_down"] = _mm_kloop(act, df, "tn", BF16, "dw_ffn_down")
    dh2 = _mm_kloop(dfg, big["w_ffn_gate"], "nt", F32, "d_h2_gate")
    dh2 = _mm_kloop(dfu, big["w_ffn_up"], "nt", F32, "d_h2_up", add=dh2)
    grads["w_ffn_gate"] = _mm_kloop(h2, dfg, "tn", BF16, "dw_ffn_gate")
    grads["w_ffn_up"] = _mm_kloop(h2, dfu, "tn", BF16, "dw_ffn_up")
    dx2, dmo, dg2, dg3 = _norm_mid_bwd(x2, mo, g2, g3, dout, dh2)
    dmerged = _mm(dmo, big["w_out"], "nt", F32, "d_merged")
    grads["w_out"] = _mm_kloop(merged, dmo, "tn", BF16, "dw_out")
    dz, dab = _merge_gate_a_bwd(z, ab, dmerged, gate_off)
    dz, dgv, dgg = _merge_gate_s_bwd(z, gv, gg, dmerged, dz, gate_off)
    dyg = _mm_fused([dgv, dgg], [big["w_glu_v"], big["w_glu_g"]], [(0, 0), (1, 1)], "nt", [F32], "d_yg",
                    epilogue=_sum_epilogue)[0]
    grads["w_glu_v"] = _mm_kloop(yg, dgv, "tn", BF16, "dw_glu_v")
    grads["w_glu_g"] = _mm_kloop(yg, dgg, "tn", BF16, "dw_glu_g")
    slots = {}
    ssm_grads = _ssm_bwd(z, y, dyg, xin, bmat, cmat, lam, dskip, u_off, side=scatter_side(SCATTER_BEHIND_SSM))
    if carry:
        ssm_grads, got = ssm_grads
        slots.update(zip(SCATTER_BEHIND_SSM, got))
    du, dbmat, dcmat, dlam, dd = ssm_grads
    dz = _put_cols(dz, du, u_off)
    dattn = _mm(dab, big["w_attn_up"], "nt", F32, "d_attn")
    grads["w_attn_up"] = _mm_kloop(attn, dab, "tn", BF16, "dw_attn_up")
    merged_bwd = _attn_merge_bwd(outs, lses, dattn)
    for g, dil in enumerate(ATTN_DILATIONS):
        dqkv = _attn_bwd(qkv[g], merged_bwd[g], lses[g], merged_bwd[3 + g], g, dil)
        dz = _undilate_dqkv(dqkv, dz, g, dil)
    grads["w_in"] = _mm_kloop(h1, dz, "tn", BF16, "dw_in")
    dh1 = _mm_kloop(dz, big["w_in"], "nt", F32, "d_h1", side=scatter_side(SCATTER_BEHIND_D_H1))
    if carry:
        dh1, got = dh1
        slots.update(zip(SCATTER_BEHIND_D_H1, got))
    grad_x, dg1 = _norm_in_bwd(x, g1, dh1, dx2)

    small_grads = dict(zip(ssm_names, (t[None] for t in ssm_vjp((dlam, dbmat, dcmat)))))
    small_grads.update(norm_mix_pre=dg1, norm_mix_post=dg2, norm_ffn_pre=dg3, norm_ffn_post=dg4, ssm_d=dd)
    return loss[0, 0], grad_x, slots if carry else grads, small_grads


def kernel(x, norm_mix_pre, w_in, w_attn_up, ssm_a_re, ssm_a_im, ssm_log_dt, ssm_b_re, ssm_b_im, ssm_c_re, ssm_c_im, ssm_d, w_glu_v, w_glu_g, w_out, norm_mix_post, norm_ffn_pre, w_ffn_gate, w_ffn_up, w_ffn_down, norm_ffn_post, loss_target, m_norm_mix_pre, m_w_in, m_w_attn_up, m_ssm_a_re, m_ssm_a_im, m_ssm_log_dt, m_ssm_b_re, m_ssm_b_im, m_ssm_c_re, m_ssm_c_im, m_ssm_d, m_w_glu_v, m_w_glu_g, m_w_out, m_norm_mix_post, m_norm_ffn_pre, m_w_ffn_gate, m_w_ffn_up, m_w_ffn_down, m_norm_ffn_post, v_norm_mix_pre, v_w_in, v_w_attn_up, v_ssm_a_re, v_ssm_a_im, v_ssm_log_dt, v_ssm_b_re, v_ssm_b_im, v_ssm_c_re, v_ssm_c_im, v_ssm_d, v_w_glu_v, v_w_glu_g, v_w_out, v_norm_mix_post, v_norm_ffn_pre, v_w_ffn_gate, v_w_ffn_up, v_w_ffn_down, v_norm_ffn_post):
    given = dict(locals())
    w = {k: given[k] for k in WEIGHT_ORDER}
    m = {k: given["m_" + k] for k in WEIGHT_ORDER}
    v = {k: given["v_" + k] for k in WEIGHT_ORDER}

    shards = {k: _cast_bf16(w[k][0], k) for k in BIG_WEIGHTS}
    shard_shapes = {k: w[k].shape[1:] for k in BIG_WEIGHTS}
    big = {"w_in": _run_side(_gather_side({"w_in": shards["w_in"]}), "gather_w_in")[0]}

    loss, grad_x, slots, small_grads = _local_step(x[0], loss_target[0], big, {k: w[k] for k in SMALL_WEIGHTS},
                                                   shards, shard_shapes)
    loss = lax.psum(loss, MESH_AXES)

    mine = {k: _sum_slots(slots[k], k) for k in BIG_WEIGHTS}
    theirs = _swap_with_sibling(mine)
    out_g, out_d, out_m, out_v = {}, {}, {}, {}
    for k in BIG_WEIGHTS:
        res = _adamw_big(w[k][0], mine[k], theirs[k], m[k][0], v[k][0], k)
        out_g[k], out_d[k], out_m[k], out_v[k] = (t[None] for t in res)

    pick = lambda tree: {k: tree[k] for k in SMALL_WEIGHTS}
    parts = _share_small(_pack_small(small_grads))
    res = _adamw_small(_pack_small(pick(w)), parts, _pack_small(pick(m)), _pack_small(pick(v)))
    for dst, packed in zip((out_g, out_d, out_m, out_v), res):
        dst.update(_unpack_small(packed, pick(w)))

    return (loss, grad_x[None], *[out_g[k] for k in WEIGHT_ORDER], *[out_d[k] for k in WEIGHT_ORDER],
            *[out_m[k] for k in WEIGHT_ORDER], *[out_v[k] for k in WEIGHT_ORDER])
```

```python
import functools
import math

import jax
import jax.numpy as jnp
from jax import lax
from jax.experimental import pallas as pl
from jax.experimental.pallas import tpu as pltpu

F32 = jnp.float32
BF16 = jnp.bfloat16

EPS = 1e-6
HEAD_DIM = 128
HEADS_PER_GROUP = 4
ATTN_DILATIONS = (1, 4, 16)
ATTN_BLK = 128
N_ATTN_HEADS = HEADS_PER_GROUP * len(ATTN_DILATIONS)
GROUP_W = HEADS_PER_GROUP * HEAD_DIM
HQ = N_ATTN_HEADS * HEAD_DIM
SSM_GROUP = 16
SSM_STATE = 64
SSM_TILE_CH = 128
SSM_TILE_ST = SSM_TILE_CH // SSM_GROUP * SSM_STATE
ADAM_LR = 0.001
ADAM_B1 = 0.9
ADAM_B2 = 0.999
ADAM_EPS = 1e-08
ADAM_WD = 0.01
ADAM_STEP = 10
NEG_BIG = -1e30
V7X_VMEM_LIMIT = 56 * 1024 * 1024
MESH_AXES = ("x", "y", "c")
N_CHIPS = 4


def _pick(n, cands):
    for c in cands:
        if n % c == 0:
            return c
    raise ValueError(f"no tile of {cands} divides {n}")


def _params(sem):
    return pltpu.CompilerParams(dimension_semantics=sem, vmem_limit_bytes=V7X_VMEM_LIMIT)


HBM = pl.BlockSpec(memory_space=pl.ANY)
MESH = pl.DeviceIdType.MESH


class _Side:
    def __init__(self, srcs, out_shapes, sem_shapes, build):
        self.srcs, self.out_shapes, self.sem_shapes, self.build = list(srcs), list(out_shapes), list(sem_shapes), build

    def start(self, src, dst, sems):
        local, sends, _ = self.build(src, dst, sems)
        for cp in local + sends:
            cp.start()

    def wait(self, src, dst, sems):
        local, sends, arrivals = self.build(src, dst, sems)
        for cp in arrivals:
            cp.wait_recv()
        for cp in sends:
            cp.wait_send()
        for cp in local:
            cp.wait()


def _call(body, *, name, grid, in_specs, out_specs, out_shape, semantics, args, scratch_shapes=(), side=None, **kw):
    in_specs, out_specs, out_shape, scratch_shapes = list(in_specs), list(out_specs), list(out_shape), list(scratch_shapes)
    if side is None:
        res = pl.pallas_call(body, name=name, grid=grid, in_specs=in_specs, out_specs=out_specs, out_shape=out_shape,
                             scratch_shapes=scratch_shapes, compiler_params=_params(semantics), **kw)(*args)
        return list(res), []
    n_in, n_out, n_scr = len(in_specs), len(out_specs), len(scratch_shapes)
    ns_in, ns_out = len(side.srcs), len(side.out_shapes)

    def carrying(*refs):
        ins, s_in = refs[:n_in], refs[n_in:n_in + ns_in]
        o0 = n_in + ns_in
        outs, s_out = refs[o0:o0 + n_out], refs[o0 + n_out:o0 + n_out + ns_out]
        c0 = o0 + n_out + ns_out
        scr, sems = refs[c0:c0 + n_scr], refs[c0 + n_scr:]
        ids = [pl.program_id(a) for a in range(len(grid))]
        first = functools.reduce(jnp.logical_and, [i == 0 for i in ids])
        last = functools.reduce(jnp.logical_and, [i == g - 1 for i, g in zip(ids, grid)])

        @pl.when(first)
        def _():
            side.start(s_in, s_out, sems)

        body(*ins, *outs, *scr)

        @pl.when(last)
        def _():
            side.wait(s_in, s_out, sems)

    res = pl.pallas_call(
        carrying, name=name, grid=grid, in_specs=in_specs + [HBM] * ns_in, out_specs=out_specs + [HBM] * ns_out,
        out_shape=out_shape + side.out_shapes, scratch_shapes=scratch_shapes + side.sem_shapes,
        compiler_params=pltpu.CompilerParams(dimension_semantics=("arbitrary",) * len(grid),
                                             vmem_limit_bytes=V7X_VMEM_LIMIT, has_side_effects=True), **kw,
    )(*args, *side.srcs)
    return list(res[:n_out]), list(res[n_out:])


def _run_side(side, name):
    ns, no = len(side.srcs), len(side.out_shapes)

    def body(*refs):
        src, dst, sems = refs[:ns], refs[ns:ns + no], refs[ns + no:]
        side.start(src, dst, sems)
        side.wait(src, dst, sems)

    return list(pl.pallas_call(body, name=name, in_specs=[HBM] * ns, out_specs=[HBM] * no, out_shape=side.out_shapes,
                               scratch_shapes=side.sem_shapes,
                               compiler_params=pltpu.CompilerParams(has_side_effects=True))(*side.srcs))


_DOT_DIMS = {"nn": (((1,), (0,)), ((), ())), "nt": (((1,), (1,)), ((), ())), "tn": (((0,), (0,)), ((), ()))}


MM_VMEM_BUDGET = 44 * 1024 * 1024
MM_STEP_BYTES = 1 << 20


def _size(dtype):
    return jnp.dtype(dtype).itemsize


def _mm_fused(as_, bs, pairs, mode, out_dtypes, name, extras=(), epilogue=None, side=None):
    M = as_[0].shape[0]
    N = bs[0].shape[1] if mode == "nn" else bs[0].shape[0]
    ks_a = [a.shape[1] for a in as_]
    ks_b = [b.shape[0] if mode == "nn" else b.shape[1] for b in bs]
    if epilogue is None:
        epilogue = lambda rs, es: rs
    offs = [off for _, off in extras]
    best = None
    for tm in (2048, 1024, 512, 256, 128):
        for tn in (2048, 1024, 512, 256, 128):
            if M % tm or N % tn or any(off % tn for off in offs):
                continue
            vmem = (sum(2 * tm * k * 2 for k in ks_a) + sum(2 * k * tn * 2 for k in ks_b)
                    + sum(2 * tm * tn * _size(d) for d in out_dtypes) + sum(2 * tm * tn * _size(e.dtype) for e, _ in extras)
                    + len(pairs) * tm * tn * 4)
            cost = sum(k * N * 2 for k in ks_b) * (M // tm) + (M // tm) * (N // tn) * MM_STEP_BYTES
            if vmem <= MM_VMEM_BUDGET and (best is None or cost < best[0]):
                best = (cost, tm, tn)
    _, tm, tn = best
    na, nb, ne, no = len(as_), len(bs), len(extras), len(out_dtypes)
    dims = _DOT_DIMS[mode]

    def body(*refs):
        a_refs, b_refs = refs[:na], refs[na:na + nb]
        e_refs, o_refs = refs[na + nb:na + nb + ne], refs[na + nb + ne:]
        rs = [lax.dot_general(a_refs[ai][...], b_refs[bi][...], dims, preferred_element_type=F32) for ai, bi in pairs]
        outs = epilogue(rs, [e[...] for e in e_refs])
        for o_ref, o in zip(o_refs, outs):
            o_ref[...] = o.astype(o_ref.dtype)

    a_specs = [pl.BlockSpec((tm, k), lambda i, j: (i, 0)) for k in ks_a]
    if mode == "nn":
        b_specs = [pl.BlockSpec((k, tn), lambda i, j: (0, j)) for k in ks_b]
    else:
        b_specs = [pl.BlockSpec((tn, k), lambda i, j: (j, 0)) for k in ks_b]
    e_specs = [pl.BlockSpec((tm, tn), lambda i, j, o=off // tn: (i, o + j)) for off in offs]
    o_spec = pl.BlockSpec((tm, tn), lambda i, j: (i, j))
    outs, carried = _call(
        body, name=name, grid=(M // tm, N // tn), in_specs=a_specs + b_specs + e_specs, out_specs=[o_spec] * no,
        out_shape=[jax.ShapeDtypeStruct((M, N), d) for d in out_dtypes], semantics=("parallel", "arbitrary"),
        args=[*as_, *bs, *[e for e, _ in extras]], side=side)
    return outs if side is None else (outs, carried)


def _mm(a, b, mode, out_dtype, name, side=None):
    res = _mm_fused([a], [b], [(0, 0)], mode, [out_dtype], name, side=side)
    return res[0] if side is None else (res[0][0], res[1])


def _mm_kloop(a, b, mode, out_dtype, name, add=None, side=None):
    if mode == "nn":
        (M, K), (_, N) = a.shape, b.shape
    elif mode == "nt":
        (M, K), (N, _) = a.shape, b.shape
    else:
        (K, M), (_, N) = a.shape, b.shape
    best = None
    for tm in (2816, 2048, 1408, 1024, 512, 256, 128):
        for tn in (2816, 2432, 2048, 1408, 1024, 512, 256, 128):
            for tk in (1024, 512, 256, 128):
                if M % tm or N % tn or K % tk:
                    continue
                vmem = (2 * tm * tn * 4 + 2 * tm * tn * _size(out_dtype) + 2 * tk * (tm + tn) * 2
                        + (2 * tm * tn * 4 if add is not None else 0))
                steps = (M // tm) * (N // tn) * (K // tk)
                cost = K * M * 2 * (N // tn) + K * N * 2 * (M // tm) + steps * MM_STEP_BYTES
                if vmem <= MM_VMEM_BUDGET and (best is None or cost < best[0]):
                    best = (cost, tm, tn, tk)
    _, tm, tn, tk = best
    nk = K // tk
    dims = _DOT_DIMS[mode]

    def body(*refs):
        if add is None:
            a_ref, b_ref, o_ref, acc_ref = refs
        else:
            a_ref, b_ref, add_ref, o_ref, acc_ref = refs
        k = pl.program_id(2)

        @pl.when(k == 0)
        def _():
            acc_ref[...] = jnp.zeros_like(acc_ref)

        acc_ref[...] += lax.dot_general(a_ref[...], b_ref[...], dims, preferred_element_type=F32)

        @pl.when(k == nk - 1)
        def _():
            r = acc_ref[...]
            if add is not None:
                r = r + add_ref[...]
            o_ref[...] = r.astype(o_ref.dtype)

    a_spec = pl.BlockSpec((tk, tm), lambda i, j, k: (k, i)) if mode == "tn" else pl.BlockSpec((tm, tk), lambda i, j, k: (i, k))
    b_spec = pl.BlockSpec((tn, tk), lambda i, j, k: (j, k)) if mode == "nt" else pl.BlockSpec((tk, tn), lambda i, j, k: (k, j))
    o_spec = pl.BlockSpec((tm, tn), lambda i, j, k: (i, j))
    outs, carried = _call(
        body, name=name, grid=(M // tm, N // tn, nk),
        in_specs=[a_spec, b_spec] + ([o_spec] if add is not None else []), out_specs=[o_spec],
        out_shape=[jax.ShapeDtypeStruct((M, N), out_dtype)], scratch_shapes=[pltpu.VMEM((tm, tn), F32)],
        semantics=("parallel", "parallel", "arbitrary"), args=(a, b) + ((add,) if add is not None else ()), side=side)
    return outs[0] if side is None else (outs[0], carried)


def _sigmoid(v):
    return 1.0 / (1.0 + jnp.exp(-v))


_GELU_C = math.sqrt(2.0 / math.pi)


def _gelu(v):
    return 0.5 * v * (1.0 + jnp.tanh(_GELU_C * (v + 0.044715 * v * v * v)))


def _gelu_grad(v):
    t = jnp.tanh(_GELU_C * (v + 0.044715 * v * v * v))
    return 0.5 * (1.0 + t) + 0.5 * v * (1.0 - t * t) * _GELU_C * (1.0 + 3.0 * 0.044715 * v * v)


def _rms(v, gain):
    r = lax.rsqrt(jnp.mean(v * v, axis=-1, keepdims=True) + EPS)
    return v * r * gain


def _rms_bwd(v, gain, dy):
    r = lax.rsqrt(jnp.mean(v * v, axis=-1, keepdims=True) + EPS)
    a = dy * gain
    dv = r * a - v * (r * r * r) * jnp.mean(a * v, axis=-1, keepdims=True)
    return dv, dy * v * r


def _row_tile(s):
    return _pick(s, (256, 128, 64, 8))


def _norm_in(x, gain):
    s, d = x.shape
    tr = _row_tile(s)

    def body(x_ref, g_ref, h_ref):
        h_ref[...] = _rms(x_ref[...], g_ref[...]).astype(BF16)

    row = pl.BlockSpec((tr, d), lambda i: (i, 0))
    vec = pl.BlockSpec((1, d), lambda i: (0, 0))
    return pl.pallas_call(body, name="norm_in", grid=(s // tr,), in_specs=[row, vec], out_specs=row,
                          out_shape=jax.ShapeDtypeStruct((s, d), BF16), compiler_params=_params(("parallel",)))(x, gain)


def _norm_mid(x, mo, g_post, g_pre):
    s, d = x.shape
    tr = _row_tile(s)

    def body(x_ref, mo_ref, g2_ref, g3_ref, x2_ref, h2_ref):
        x2 = x_ref[...] + _rms(mo_ref[...], g2_ref[...])
        x2_ref[...] = x2
        h2_ref[...] = _rms(x2, g3_ref[...]).astype(BF16)

    row = pl.BlockSpec((tr, d), lambda i: (i, 0))
    vec = pl.BlockSpec((1, d), lambda i: (0, 0))
    return pl.pallas_call(
        body, name="norm_mid", grid=(s // tr,), in_specs=[row, row, vec, vec], out_specs=[row, row],
        out_shape=[jax.ShapeDtypeStruct((s, d), F32), jax.ShapeDtypeStruct((s, d), BF16)],
        compiler_params=_params(("parallel",)))(x, mo, g_post, g_pre)


def _loss_head(x2, f, g_post, target):
    s, d = x2.shape
    tr = _row_tile(s)

    def body(x2_ref, f_ref, g_ref, t_ref, loss_ref, dout_ref, df_ref, dg_ref):
        @pl.when(pl.program_id(0) == 0)
        def _():
            loss_ref[...] = jnp.zeros_like(loss_ref)
            dg_ref[...] = jnp.zeros_like(dg_ref)

        fv = f_ref[...]
        g = g_ref[...]
        err = x2_ref[...] + _rms(fv, g) - t_ref[...]
        loss_ref[...] += 0.5 * jnp.sum(jnp.mean(err * err, axis=-1, keepdims=True), axis=0, keepdims=True)
        dout = err * (1.0 / d)
        dout_ref[...] = dout
        df, dg = _rms_bwd(fv, g, dout)
        df_ref[...] = df.astype(BF16)
        dg_ref[...] += jnp.sum(dg, axis=0, keepdims=True)

    row = pl.BlockSpec((tr, d), lambda i: (i, 0))
    vec = pl.BlockSpec((1, d), lambda i: (0, 0))
    one = pl.BlockSpec((1, 1), lambda i: (0, 0))
    return pl.pallas_call(
        body, name="loss_head", grid=(s // tr,), in_specs=[row, row, vec, row], out_specs=[one, row, row, vec],
        out_shape=[jax.ShapeDtypeStruct((1, 1), F32), jax.ShapeDtypeStruct((s, d), F32),
                   jax.ShapeDtypeStruct((s, d), BF16), jax.ShapeDtypeStruct((1, d), F32)],
        compiler_params=_params(("arbitrary",)))(x2, f, g_post, target)


def _norm_mid_bwd(x2, mo, g_post, g_pre, dout, dh2):
    s, d = x2.shape
    tr = _row_tile(s)

    def body(x2_ref, mo_ref, g2_ref, g3_ref, dout_ref, dh2_ref, dx2_ref, dmo_ref, dg2_ref, dg3_ref):
        @pl.when(pl.program_id(0) == 0)
        def _():
            dg2_ref[...] = jnp.zeros_like(dg2_ref)
            dg3_ref[...] = jnp.zeros_like(dg3_ref)

        dv, dg3 = _rms_bwd(x2_ref[...], g3_ref[...], dh2_ref[...])
        dx2 = dout_ref[...] + dv
        dx2_ref[...] = dx2
        dmo, dg2 = _rms_bwd(mo_ref[...], g2_ref[...], dx2)
        dmo_ref[...] = dmo.astype(BF16)
        dg2_ref[...] += jnp.sum(dg2, axis=0, keepdims=True)
        dg3_ref[...] += jnp.sum(dg3, axis=0, keepdims=True)

    row = pl.BlockSpec((tr, d), lambda i: (i, 0))
    vec = pl.BlockSpec((1, d), lambda i: (0, 0))
    return pl.pallas_call(
        body, name="norm_mid_bwd", grid=(s // tr,), in_specs=[row, row, vec, vec, row, row],
        out_specs=[row, row, vec, vec],
        out_shape=[jax.ShapeDtypeStruct((s, d), F32), jax.ShapeDtypeStruct((s, d), BF16),
                   jax.ShapeDtypeStruct((1, d), F32), jax.ShapeDtypeStruct((1, d), F32)],
        compiler_params=_params(("arbitrary",)))(x2, mo, g_post, g_pre, dout, dh2)


def _norm_in_bwd(x, gain, dh, dx2):
    s, d = x.shape
    tr = _row_tile(s)

    def body(x_ref, g_ref, dh_ref, dx2_ref, dx_ref, dg_ref):
        @pl.when(pl.program_id(0) == 0)
        def _():
            dg_ref[...] = jnp.zeros_like(dg_ref)

        dv, dg = _rms_bwd(x_ref[...], g_ref[...], dh_ref[...])
        dx_ref[...] = dx2_ref[...] + dv
        dg_ref[...] += jnp.sum(dg, axis=0, keepdims=True)

    row = pl.BlockSpec((tr, d), lambda i: (i, 0))
    vec = pl.BlockSpec((1, d), lambda i: (0, 0))
    return pl.pallas_call(
        body, name="norm_in_bwd", grid=(s // tr,), in_specs=[row, vec, row, row], out_specs=[row, vec],
        out_shape=[jax.ShapeDtypeStruct((s, d), F32), jax.ShapeDtypeStruct((1, d), F32)],
        compiler_params=_params(("arbitrary",)))(x, gain, dh, dx2)


def _swiglu_epilogue(rs, es):
    g, u = rs
    return [g * _sigmoid(g) * u, g, u]


def _swiglu_bwd_epilogue(rs, es):
    d = rs[0]
    g, u = es[0].astype(F32), es[1].astype(F32)
    sg = _sigmoid(g)
    return [d * u * sg * (1.0 + g * (1.0 - sg)), d * g * sg]


def _sum_epilogue(rs, es):
    return [rs[0] + rs[1]]


def _gates_epilogue(rs, es):
    ab, gv, gg = rs
    ga, gs = es
    return [_sigmoid(ga) * ab + _sigmoid(gs) * gv * _sigmoid(gg), ab, gv, gg]


def _gate_cols(d, gate_off):
    tc = _pick(math.gcd(d, gate_off), (512, 256, 128))
    return tc, gate_off // tc, d // tc


def _merge_gate_a_bwd(z, ab, dmerged, gate_off):
    s, d = ab.shape
    tr = _row_tile(s)
    tc, off, nd = _gate_cols(d, gate_off)

    def body(ga_ref, ab_ref, dm_ref, dga_ref, dab_ref):
        dm = dm_ref[...]
        sa = _sigmoid(ga_ref[...])
        dga_ref[...] = (dm * ab_ref[...] * sa * (1.0 - sa)).astype(BF16)
        dab_ref[...] = (dm * sa).astype(BF16)

    blk = pl.BlockSpec((tr, tc), lambda i, j: (i, j))
    ga = pl.BlockSpec((tr, tc), lambda i, j: (i, off + j))
    return pl.pallas_call(
        body, name="merge_gate_a_bwd", grid=(s // tr, nd), in_specs=[ga, blk, blk], out_specs=[ga, blk],
        out_shape=[jax.ShapeDtypeStruct(z.shape, BF16), jax.ShapeDtypeStruct((s, d), BF16)],
        compiler_params=_params(("parallel", "parallel")))(z, ab, dmerged)


def _merge_gate_s_bwd(z, gv, gg, dmerged, dz, gate_off):
    s, d = gv.shape
    tr = _row_tile(s)
    tc, off, nd = _gate_cols(d, gate_off)

    def body(gs_ref, gv_ref, gg_ref, dm_ref, dz_ref, dgs_ref, dgv_ref, dgg_ref):
        del dz_ref
        ss = _sigmoid(gs_ref[...])
        sg = _sigmoid(gg_ref[...])
        gv_ = gv_ref[...]
        dm = dm_ref[...]
        dgs_ref[...] = (dm * gv_ * sg * ss * (1.0 - ss)).astype(BF16)
        dsb = dm * ss
        dgv_ref[...] = (dsb * sg).astype(BF16)
        dgg_ref[...] = (dsb * gv_ * sg * (1.0 - sg)).astype(BF16)

    blk = pl.BlockSpec((tr, tc), lambda i, j: (i, j))
    gs = pl.BlockSpec((tr, tc), lambda i, j: (i, off + nd + j))
    return pl.pallas_call(
        body, name="merge_gate_s_bwd", grid=(s // tr, nd),
        in_specs=[gs, blk, blk, blk, pl.BlockSpec(memory_space=pl.ANY)], out_specs=[gs, blk, blk],
        out_shape=[jax.ShapeDtypeStruct(z.shape, BF16)] + [jax.ShapeDtypeStruct((s, d), BF16)] * 2,
        input_output_aliases={4: 0},
        compiler_params=_params(("parallel", "parallel")))(z, gv, gg, dmerged, dz)


def _put_cols(dz, src, col_off):
    s, w = src.shape
    tr = _row_tile(s)
    tc = _pick(math.gcd(w, col_off), (512, 256, 128))
    off = col_off // tc

    def body(src_ref, dz_ref, o_ref):
        del dz_ref
        o_ref[...] = src_ref[...].astype(o_ref.dtype)

    return pl.pallas_call(
        body, name="put_cols", grid=(s // tr, w // tc),
        in_specs=[pl.BlockSpec((tr, tc), lambda i, j: (i, j)), pl.BlockSpec(memory_space=pl.ANY)],
        out_specs=pl.BlockSpec((tr, tc), lambda i, j: (i, off + j)),
        out_shape=jax.ShapeDtypeStruct(dz.shape, dz.dtype), input_output_aliases={1: 0},
        compiler_params=_params(("parallel", "parallel")))(src, dz)


ATTN_ROWS = 2048


def _dilate_qkv(z, g, d):
    s = z.shape[0]
    tm = ATTN_ROWS
    per = tm // d
    nh = HEADS_PER_GROUP

    def body(z_ref, o_ref):
        for r in range(d):
            rows = z_ref[...] if d == 1 else z_ref[pl.ds(r, per, stride=d), :]
            o_ref[0, r] = rows.astype(BF16)

    return pl.pallas_call(
        body, name=f"dilate_qkv_{g}", grid=(s // tm, 3, nh),
        in_specs=[pl.BlockSpec((tm, HEAD_DIM), lambda i, w, h: (i, (3 * w + g) * nh + h))],
        out_specs=pl.BlockSpec((1, d, per, HEAD_DIM), lambda i, w, h: (w, 0, i, h)),
        out_shape=jax.ShapeDtypeStruct((3, d, s // d, GROUP_W), BF16),
        compiler_params=_params(("parallel", "parallel", "parallel")))(z)


def _undilate_dqkv(dqkv, dz, g, d):
    s = dz.shape[0]
    tm = ATTN_ROWS
    per = tm // d
    nh = HEADS_PER_GROUP

    def body(i_ref, dz_ref, o_ref, nat_ref):
        del dz_ref
        if d == 1:
            o_ref[...] = i_ref[0, 0]
        else:
            for r in range(d):
                nat_ref[pl.ds(r, per, stride=d), :] = i_ref[0, r].astype(F32)
            o_ref[...] = nat_ref[...].astype(BF16)

    return pl.pallas_call(
        body, name=f"undilate_dqkv_{g}", grid=(s // tm, 3, nh),
        in_specs=[pl.BlockSpec((1, d, per, HEAD_DIM), lambda i, w, h: (w, 0, i, h)),
                  pl.BlockSpec(memory_space=pl.ANY)],
        out_specs=pl.BlockSpec((tm, HEAD_DIM), lambda i, w, h: (i, (3 * w + g) * nh + h)),
        out_shape=jax.ShapeDtypeStruct(dz.shape, dz.dtype), input_output_aliases={1: 0},
        scratch_shapes=[pltpu.VMEM((tm, HEAD_DIM), F32)],
        compiler_params=_params(("parallel", "parallel", "parallel")))(dqkv, dz)


def _alibi_slope(head):
    return 2.0 ** (-8.0 * (head + 1) / N_ATTN_HEADS)


def _band_masks(n, nb):
    qi = lax.broadcasted_iota(jnp.int32, (ATTN_BLK, ATTN_BLK), 0)
    ki = lax.broadcasted_iota(jnp.int32, (ATTN_BLK, ATTN_BLK), 1)
    dist_cur = qi - ki
    dist_prev = ATTN_BLK + qi - ki
    return dist_cur.astype(F32), dist_cur >= 0, dist_prev.astype(F32), dist_prev <= ATTN_BLK


def _dot_nt(a, b):
    return lax.dot_general(a, b, _DOT_DIMS["nt"], preferred_element_type=F32)


def _dot_tn(a, b):
    return lax.dot_general(a, b, _DOT_DIMS["tn"], preferred_element_type=F32)


def _dot(a, b):
    return jnp.dot(a, b, preferred_element_type=F32)


def _scores(q, k, slope_d, dist, valid):
    s = _dot_nt(q, k) * (HEAD_DIM ** -0.5) - slope_d * dist
    return jnp.where(valid, s, NEG_BIG)


def _attn_fwd(qkv, g, d):
    _, _, L, _ = qkv.shape
    nb = L // ATTN_BLK

    def body(q_ref, kc_ref, kp_ref, vc_ref, vp_ref, o_ref, lse_ref):
        n = pl.program_id(1)
        dist_c, valid_c, dist_p, valid_p = _band_masks(n, nb)
        valid_p = jnp.logical_and(valid_p, n > 0)
        for hh in range(HEADS_PER_GROUP):
            cols = slice(hh * HEAD_DIM, (hh + 1) * HEAD_DIM)
            slope_d = _alibi_slope(g * HEADS_PER_GROUP + hh) * d
            q = q_ref[0, 0, :, cols]
            sc = _scores(q, kc_ref[0, 0, :, cols], slope_d, dist_c, valid_c)
            sp = _scores(q, kp_ref[0, 0, :, cols], slope_d, dist_p, valid_p)
            m = jnp.maximum(jnp.max(sc, axis=-1, keepdims=True), jnp.max(sp, axis=-1, keepdims=True))
            ec = jnp.exp(sc - m)
            ep = jnp.exp(sp - m)
            l = jnp.sum(ec, axis=-1, keepdims=True) + jnp.sum(ep, axis=-1, keepdims=True)
            inv = 1.0 / l
            o = _dot((ec * inv).astype(BF16), vc_ref[0, 0, :, cols]) + _dot((ep * inv).astype(BF16), vp_ref[0, 0, :, cols])
            o_ref[0, :, cols] = o
            lse_ref[0, :, cols] = jnp.broadcast_to(m + jnp.log(l), (ATTN_BLK, HEAD_DIM))

    def spec(w, shift):
        return pl.BlockSpec((1, 1, ATTN_BLK, GROUP_W), lambda r, n: (w, r, jnp.maximum(n + shift, 0), 0))

    out = pl.BlockSpec((1, ATTN_BLK, GROUP_W), lambda r, n: (r, n, 0))
    return pl.pallas_call(
        body, name=f"attn_fwd_{g}", grid=(d, nb),
        in_specs=[spec(0, 0), spec(1, 0), spec(1, -1), spec(2, 0), spec(2, -1)], out_specs=[out, out],
        out_shape=[jax.ShapeDtypeStruct((d, L, GROUP_W), F32)] * 2,
        compiler_params=_params(("parallel", "parallel")))(qkv, qkv, qkv, qkv, qkv)


def _attn_bwd(qkv, do, lse, cc, g, d):
    _, _, L, _ = qkv.shape
    nb = L // ATTN_BLK
    scale = HEAD_DIM ** -0.5

    def body(q0_ref, q1_ref, k0_ref, kp_ref, v0_ref, vp_ref, do0_ref, do1_ref, l0_ref, l1_ref, c0_ref, c1_ref, o_ref):
        n = pl.program_id(1)
        dist_c, valid_c, dist_p, valid_p = _band_masks(n, nb)
        valid_b = jnp.logical_and(valid_p, n > 0)
        valid_n = jnp.logical_and(valid_p, n < nb - 1)
        for hh in range(HEADS_PER_GROUP):
            cols = slice(hh * HEAD_DIM, (hh + 1) * HEAD_DIM)
            slope_d = _alibi_slope(g * HEADS_PER_GROUP + hh) * d
            q0, q1 = q0_ref[0, 0, :, cols], q1_ref[0, 0, :, cols]
            k0, kp = k0_ref[0, 0, :, cols], kp_ref[0, 0, :, cols]
            v0, vp = v0_ref[0, 0, :, cols], vp_ref[0, 0, :, cols]
            do0, do1 = do0_ref[0, :, cols], do1_ref[0, :, cols]
            l0, l1 = l0_ref[0, :, cols], l1_ref[0, :, cols]
            c0, c1 = c0_ref[0, :, cols], c1_ref[0, :, cols]
            pa = jnp.exp(_scores(q0, k0, slope_d, dist_c, valid_c) - l0)
            dsa = (pa * (_dot_nt(do0, v0) + c0)).astype(BF16)
            pb = jnp.exp(_scores(q0, kp, slope_d, dist_p, valid_b) - l0)
            dsb = (pb * (_dot_nt(do0, vp) + c0)).astype(BF16)
            pc = jnp.exp(_scores(q1, k0, slope_d, dist_p, valid_n) - l1)
            dsc = (pc * (_dot_nt(do1, v0) + c1)).astype(BF16)
            o_ref[0, 0, :, cols] = ((_dot(dsa, k0) + _dot(dsb, kp)) * scale).astype(BF16)
            o_ref[1, 0, :, cols] = ((_dot_tn(dsa, q0) + _dot_tn(dsc, q1)) * scale).astype(BF16)
            o_ref[2, 0, :, cols] = (_dot_tn(pa.astype(BF16), do0) + _dot_tn(pc.astype(BF16), do1)).astype(BF16)

    def spec(w, shift):
        return pl.BlockSpec((1, 1, ATTN_BLK, GROUP_W), lambda r, n: (w, r, jnp.clip(n + shift, 0, nb - 1), 0))

    def spec3(shift):
        return pl.BlockSpec((1, ATTN_BLK, GROUP_W), lambda r, n: (r, jnp.clip(n + shift, 0, nb - 1), 0))

    return pl.pallas_call(
        body, name=f"attn_bwd_{g}", grid=(d, nb),
        in_specs=[spec(0, 0), spec(0, 1), spec(1, 0), spec(1, -1), spec(2, 0), spec(2, -1),
                  spec3(0), spec3(1), spec3(0), spec3(1), spec3(0), spec3(1)],
        out_specs=pl.BlockSpec((3, 1, ATTN_BLK, GROUP_W), lambda r, n: (0, r, n, 0)),
        out_shape=jax.ShapeDtypeStruct((3, d, L, GROUP_W), BF16),
        compiler_params=_params(("parallel", "parallel")))(qkv, qkv, qkv, qkv, qkv, qkv, do, do, lse, lse, cc, cc)


def _load_natural(refs, nat_refs):
    for g, d in enumerate(ATTN_DILATIONS):
        if d == 1:
            nat_refs[g][...] = refs[g][0]
        else:
            per = ATTN_ROWS // d
            for r in range(d):
                nat_refs[g][pl.ds(r, per, stride=d), :] = refs[g][r]


def _mix_weights(lse_nat):
    l0, l1, l2 = lse_nat[0][...], lse_nat[1][...], lse_nat[2][...]
    m = jnp.maximum(jnp.maximum(l0, l1), l2)
    e0, e1, e2 = jnp.exp(l0 - m), jnp.exp(l1 - m), jnp.exp(l2 - m)
    inv = 1.0 / (e0 + e1 + e2)
    return e0 * inv, e1 * inv, e2 * inv


def _dilated_specs(s):
    return [pl.BlockSpec((d, ATTN_ROWS // d, HEAD_DIM), lambda i, h: (0, i, h)) for d in ATTN_DILATIONS]


NATURAL_SCRATCH = [pltpu.VMEM((ATTN_ROWS, HEAD_DIM), F32)] * (2 * len(ATTN_DILATIONS))


def _attn_merge(outs, lses):
    s = outs[0].shape[0] * outs[0].shape[1]

    def body(o0, o1, o2, l0, l1, l2, a_ref, *nat):
        onat, lnat = nat[:3], nat[3:]
        _load_natural((o0, o1, o2), onat)
        _load_natural((l0, l1, l2), lnat)
        w0, w1, w2 = _mix_weights(lnat)
        a_ref[...] = (w0 * onat[0][...] + w1 * onat[1][...] + w2 * onat[2][...]).astype(BF16)

    return pl.pallas_call(
        body, name="attn_merge", grid=(s // ATTN_ROWS, HEADS_PER_GROUP), in_specs=_dilated_specs(s) * 2,
        out_specs=pl.BlockSpec((ATTN_ROWS, HEAD_DIM), lambda i, h: (i, h)),
        out_shape=jax.ShapeDtypeStruct((s, GROUP_W), BF16), scratch_shapes=NATURAL_SCRATCH,
        compiler_params=_params(("parallel", "parallel")))(*outs, *lses)


def _attn_merge_bwd(outs, lses, dattn):
    s = dattn.shape[0]

    def body(o0, o1, o2, l0, l1, l2, da_ref, do0, do1, do2, c0, c1, c2, *nat):
        onat, lnat = nat[:3], nat[3:]
        _load_natural((o0, o1, o2), onat)
        _load_natural((l0, l1, l2), lnat)
        ws = _mix_weights(lnat)
        da = da_ref[...]
        attn = ws[0] * onat[0][...] + ws[1] * onat[1][...] + ws[2] * onat[2][...]
        tot = jnp.broadcast_to(jnp.sum(da * attn, axis=-1, keepdims=True), (ATTN_ROWS, HEAD_DIM))
        for g, (d, do_ref, c_ref) in enumerate(zip(ATTN_DILATIONS, (do0, do1, do2), (c0, c1, c2))):
            if d == 1:
                do_ref[0] = (ws[g] * da).astype(BF16)
                c_ref[0] = -ws[g] * tot
            else:
                onat[g][...] = ws[g] * da
                lnat[g][...] = -ws[g] * tot
                per = ATTN_ROWS // d
                for r in range(d):
                    do_ref[r] = onat[g][pl.ds(r, per, stride=d), :].astype(BF16)
                    c_ref[r] = lnat[g][pl.ds(r, per, stride=d), :]

    dil = _dilated_specs(s)
    shapes = [jax.ShapeDtypeStruct(o.shape, BF16) for o in outs] + [jax.ShapeDtypeStruct(o.shape, F32) for o in outs]
    return pl.pallas_call(
        body, name="attn_merge_bwd", grid=(s // ATTN_ROWS, HEADS_PER_GROUP),
        in_specs=dil * 2 + [pl.BlockSpec((ATTN_ROWS, HEAD_DIM), lambda i, h: (i, h))], out_specs=dil * 2,
        out_shape=shapes, scratch_shapes=NATURAL_SCRATCH,
        compiler_params=_params(("parallel", "parallel")))(*outs, *lses, dattn)


def _ssm_prepare(a_re, a_im, log_dt, b_re, b_im, c_re, c_im):
    n_g = a_re.shape[0]
    nj = n_g * SSM_GROUP // SSM_TILE_CH
    gpt = SSM_TILE_CH // SSM_GROUP
    dt = jnp.exp(log_dt)[:, None]
    mag = jnp.exp(a_re * dt)
    lr, li = mag * jnp.cos(a_im * dt), mag * jnp.sin(a_im * dt)
    den = a_re * a_re + a_im * a_im
    cr = ((lr - 1.0) * a_re + li * a_im) / den
    ci = (li * a_re - (lr - 1.0) * a_im) / den
    bb_re = cr[..., None] * b_re - ci[..., None] * b_im
    bb_im = cr[..., None] * b_im + ci[..., None] * b_re
    eye = jnp.eye(gpt, dtype=F32)

    def b_tiles(t):
        t = t.transpose(0, 2, 1).reshape(nj, gpt, SSM_GROUP, SSM_STATE)
        return jnp.einsum("jgcp,gh->jgchp", t, eye).reshape(nj, SSM_TILE_CH, SSM_TILE_ST)

    def c_tiles(t):
        t = t.reshape(nj, gpt, SSM_GROUP, SSM_STATE)
        return jnp.einsum("jgcp,gh->jhpgc", t, eye).reshape(nj, SSM_TILE_ST, SSM_TILE_CH)

    lam = jnp.stack([lr.reshape(-1), li.reshape(-1)])
    bmat = jnp.concatenate([b_tiles(bb_re), b_tiles(bb_im)], axis=2)
    cmat = jnp.concatenate([c_tiles(c_re), -c_tiles(c_im)], axis=1)
    return lam, bmat, cmat


SSM_SEGMENTS = 8


def _to_segment_order(nat, perm_ref):
    per = nat.shape[0] // SSM_SEGMENTS
    for i in range(SSM_SEGMENTS):
        perm_ref[pl.ds(i, per, stride=SSM_SEGMENTS), :] = nat[i * per:(i + 1) * per, :]
    return perm_ref[...]


def _to_time_order(val, perm_ref, store):
    per = val.shape[0] // SSM_SEGMENTS
    perm_ref[...] = val
    for i in range(SSM_SEGMENTS):
        store(i, perm_ref[pl.ds(i, per, stride=SSM_SEGMENTS), :])


def _fill_powers(lam_ref, w_ref, nj, tau_n):
    for j in range(nj):
        st = slice(j * SSM_TILE_ST, (j + 1) * SSM_TILE_ST)
        lr = jnp.broadcast_to(lam_ref[0:1, st], (SSM_SEGMENTS, SSM_TILE_ST))
        li = jnp.broadcast_to(lam_ref[1:2, st], (SSM_SEGMENTS, SSM_TILE_ST))
        wr, wi = lr, li
        for tau in range(tau_n):
            rows = slice(tau * SSM_SEGMENTS, (tau + 1) * SSM_SEGMENTS)
            w_ref[j, rows, :SSM_TILE_ST] = wr
            w_ref[j, rows, SSM_TILE_ST:] = wi
            wr, wi = wr * lr - wi * li, wr * li + wi * lr


def _segment_scan(src, xs_ref, w_tile, lr, li, cr, ci, conj, reverse):
    seg, half = SSM_SEGMENTS, SSM_TILE_ST
    tau_n = src.shape[0] // seg
    sgn = -1.0 if conj else 1.0
    lr8 = jnp.broadcast_to(lr, (seg, half))
    li8 = jnp.broadcast_to(li, (seg, half)) * sgn
    xr = jnp.zeros((seg, half), F32)
    xi = jnp.zeros((seg, half), F32)
    order = range(tau_n - 1, -1, -1) if reverse else range(tau_n)
    for tau in order:
        rows = slice(tau * seg, (tau + 1) * seg)
        xr, xi = lr8 * xr - li8 * xi + src[rows, :half], lr8 * xi + li8 * xr + src[rows, half:]
        xs_ref[rows, :half] = xr
        xs_ref[rows, half:] = xi
    pr = w_tile[(tau_n - 1) * seg:(tau_n - 1) * seg + 1, :half]
    pi = w_tile[(tau_n - 1) * seg:(tau_n - 1) * seg + 1, half:] * sgn
    fr, fi = cr, ci
    ins_r, ins_i = [None] * seg, [None] * seg
    runs = range(seg - 1, -1, -1) if reverse else range(seg)
    for i in runs:
        ins_r[i], ins_i[i] = fr, fi
        fr, fi = xr[i:i + 1, :] + pr * fr - pi * fi, xi[i:i + 1, :] + pr * fi + pi * fr
    in_r = jnp.concatenate(ins_r, axis=0)
    in_i = jnp.concatenate(ins_i, axis=0)
    for tau in range(tau_n):
        rows = slice(tau * seg, (tau + 1) * seg)
        wrow = (tau_n - 1 - tau) if reverse else tau
        wr = w_tile[wrow * seg:(wrow + 1) * seg, :half]
        wi = w_tile[wrow * seg:(wrow + 1) * seg, half:] * sgn
        xs_ref[rows, :half] += wr * in_r - wi * in_i
        xs_ref[rows, half:] += wr * in_i + wi * in_r
    return (fr, fi), (in_r, in_i)


def _ssm_dims(z, bmat, u_off):
    s = z.shape[0]
    nj = bmat.shape[0]
    t_rows = _pick(s, (256, 128))
    return s, nj, nj * SSM_TILE_CH, nj * SSM_TILE_ST, t_rows


def _ssm_fwd(z, bmat, cmat, lam, dskip, u_off, side=None):
    s, nj, w, ns, t_rows = _ssm_dims(z, bmat, u_off)
    per = t_rows // SSM_SEGMENTS

    def body(*refs):
        u_refs = refs[:nj]
        b_ref, c_ref, lam_ref, d_ref, y_ref, yg_ref, xin_ref, carry_ref, w_ref, xs_ref, perm_ref = refs[nj:]

        @pl.when(pl.program_id(0) == 0)
        def _():
            carry_ref[...] = jnp.zeros_like(carry_ref)
            _fill_powers(lam_ref, w_ref, nj, per)

        xin_ref[0] = carry_ref[...]
        for j in range(nj):
            st = slice(j * SSM_TILE_ST, (j + 1) * SSM_TILE_ST)
            ch = slice(j * SSM_TILE_CH, (j + 1) * SSM_TILE_CH)
            up = _to_segment_order(u_refs[j], perm_ref)
            bu = _dot(up.astype(BF16), b_ref[j])
            (fr, fi), _ = _segment_scan(bu, xs_ref, w_ref.at[j], lam_ref[0:1, st], lam_ref[1:2, st],
                                        carry_ref[0:1, st], carry_ref[1:2, st], conj=False, reverse=False)
            carry_ref[0:1, st] = fr
            carry_ref[1:2, st] = fi
            yp = _dot(xs_ref[...].astype(BF16), c_ref[j]) + d_ref[:, ch] * up

            def store(i, rows, ch=ch):
                y_ref[i * per:(i + 1) * per, ch] = rows
                yg_ref[i * per:(i + 1) * per, ch] = _gelu(rows).astype(BF16)

            _to_time_order(yp, perm_ref, store)

    u_specs = [pl.BlockSpec((t_rows, SSM_TILE_CH), lambda c, k=k: (c, u_off // SSM_TILE_CH + k)) for k in range(nj)]
    full3 = lambda shape: pl.BlockSpec(shape, lambda c: (0, 0, 0))
    full2 = lambda shape: pl.BlockSpec(shape, lambda c: (0, 0))
    rows = pl.BlockSpec((t_rows, w), lambda c: (c, 0))
    outs, carried = _call(
        body, name="ssm_fwd", grid=(s // t_rows,),
        in_specs=u_specs + [full3(bmat.shape), full3(cmat.shape), full2(lam.shape), full2(dskip.shape)],
        out_specs=[rows, rows, pl.BlockSpec((1, 2, ns), lambda c: (c, 0, 0))],
        out_shape=[jax.ShapeDtypeStruct((s, w), F32), jax.ShapeDtypeStruct((s, w), BF16),
                   jax.ShapeDtypeStruct((s // t_rows, 2, ns), F32)],
        scratch_shapes=[pltpu.VMEM((2, ns), F32), pltpu.VMEM((nj, t_rows, 2 * SSM_TILE_ST), F32),
                        pltpu.VMEM((t_rows, 2 * SSM_TILE_ST), F32), pltpu.VMEM((t_rows, SSM_TILE_CH), F32)],
        semantics=("arbitrary",), args=[*([z] * nj), bmat, cmat, lam, dskip], side=side)
    return outs if side is None else (outs, carried)


def _ssm_bwd(z, y, dyg, xin, bmat, cmat, lam, dskip, u_off, side=None):
    s, nj, w, ns, t_rows = _ssm_dims(z, bmat, u_off)
    nc = s // t_rows
    per = t_rows // SSM_SEGMENTS
    seg, half = SSM_SEGMENTS, SSM_TILE_ST

    def body(*refs):
        u_refs = refs[:nj]
        (y_ref, dyg_ref, xin_ref, b_ref, c_ref, lam_ref, d_ref, du_ref, db_ref, dc_ref, dlam_ref, dd_ref,
         carry_ref, w_ref, xs_ref, gs_ref, perm_ref, acc_ref) = refs[nj:]

        @pl.when(pl.program_id(0) == 0)
        def _():
            carry_ref[...] = jnp.zeros_like(carry_ref)
            db_ref[...] = jnp.zeros_like(db_ref)
            dc_ref[...] = jnp.zeros_like(dc_ref)
            dd_ref[...] = jnp.zeros_like(dd_ref)
            acc_ref[...] = jnp.zeros_like(acc_ref)
            _fill_powers(lam_ref, w_ref, nj, per)

        for j in range(nj):
            st = slice(j * SSM_TILE_ST, (j + 1) * SSM_TILE_ST)
            ch = slice(j * SSM_TILE_CH, (j + 1) * SSM_TILE_CH)
            lr, li = lam_ref[0:1, st], lam_ref[1:2, st]
            up = _to_segment_order(u_refs[j], perm_ref)
            upb = up.astype(BF16)
            dyp = _to_segment_order(dyg_ref[:, ch] * _gelu_grad(y_ref[:, ch]), perm_ref)
            dyb = dyp.astype(BF16)
            _, (in_r, in_i) = _segment_scan(_dot(upb, b_ref[j]), xs_ref, w_ref.at[j], lr, li,
                                            xin_ref[0, 0:1, st], xin_ref[0, 1:2, st], conj=False, reverse=False)
            (gr, gi), _ = _segment_scan(_dot_nt(dyb, c_ref[j]), gs_ref, w_ref.at[j], lr, li,
                                        carry_ref[0:1, st], carry_ref[1:2, st], conj=True, reverse=True)
            carry_ref[0:1, st] = gr
            carry_ref[1:2, st] = gi
            xs, gs = xs_ref[...], gs_ref[...]
            xsr, xsi, gsr, gsi = xs[:, :half], xs[:, half:], gs[:, :half], gs[:, half:]
            pxr = jnp.concatenate([in_r, xsr[:t_rows - seg]], axis=0)
            pxi = jnp.concatenate([in_i, xsi[:t_rows - seg]], axis=0)
            dl_r = gsr * pxr + gsi * pxi
            dl_i = gsi * pxr - gsr * pxi
            acc_ref[0, :, st] += jnp.sum(dl_r.reshape(per, seg, half), axis=0)
            acc_ref[1, :, st] += jnp.sum(dl_i.reshape(per, seg, half), axis=0)
            gx = gs.astype(BF16)
            dup = _dot_nt(gx, b_ref[j]) + d_ref[:, ch] * dyp

            def store(i, rows, ch=ch):
                du_ref[i * per:(i + 1) * per, ch] = rows.astype(BF16)

            _to_time_order(dup, perm_ref, store)
            db_ref[j] += _dot_tn(upb, gx)
            dc_ref[j] += _dot_tn(xs.astype(BF16), dyb)
            dd_ref[:, ch] += jnp.sum(dyp * up, axis=0, keepdims=True)

        @pl.when(pl.program_id(0) == nc - 1)
        def _():
            dlam_ref[...] = jnp.sum(acc_ref[...], axis=1)

    rev = lambda c: nc - 1 - c
    u_specs = [pl.BlockSpec((t_rows, SSM_TILE_CH), lambda c, k=k: (rev(c), u_off // SSM_TILE_CH + k))
               for k in range(nj)]
    full3 = lambda shape: pl.BlockSpec(shape, lambda c: (0, 0, 0))
    full2 = lambda shape: pl.BlockSpec(shape, lambda c: (0, 0))
    rows = pl.BlockSpec((t_rows, w), lambda c: (rev(c), 0))
    outs, carried = _call(
        body, name="ssm_bwd", grid=(nc,),
        in_specs=u_specs + [rows, rows, pl.BlockSpec((1, 2, ns), lambda c: (rev(c), 0, 0)),
                            full3(bmat.shape), full3(cmat.shape), full2(lam.shape), full2(dskip.shape)],
        out_specs=[rows, full3(bmat.shape), full3(cmat.shape), full2(lam.shape), full2(dskip.shape)],
        out_shape=[jax.ShapeDtypeStruct((s, w), BF16), jax.ShapeDtypeStruct(bmat.shape, F32),
                   jax.ShapeDtypeStruct(cmat.shape, F32), jax.ShapeDtypeStruct(lam.shape, F32),
                   jax.ShapeDtypeStruct(dskip.shape, F32)],
        scratch_shapes=[pltpu.VMEM((2, ns), F32), pltpu.VMEM((nj, t_rows, 2 * SSM_TILE_ST), F32),
                        pltpu.VMEM((t_rows, 2 * SSM_TILE_ST), F32), pltpu.VMEM((t_rows, 2 * SSM_TILE_ST), F32),
                        pltpu.VMEM((t_rows, SSM_TILE_CH), F32), pltpu.VMEM((2, SSM_SEGMENTS, ns), F32)],
        semantics=("arbitrary",), args=[*([z] * nj), y, dyg, xin, bmat, cmat, lam, dskip], side=side)
    return outs if side is None else (outs, carried)


def _adam_math(w, g, m, v):
    m = ADAM_B1 * m + (1.0 - ADAM_B1) * g
    v = ADAM_B2 * v + (1.0 - ADAM_B2) * (g * g)
    m_hat = m / (1.0 - ADAM_B1 ** ADAM_STEP)
    v_hat = v / (1.0 - ADAM_B2 ** ADAM_STEP)
    delta = -ADAM_LR * (m_hat / (jnp.sqrt(v_hat) + ADAM_EPS) + ADAM_WD * w)
    return delta, m, v


def _adam_rows(r, c):
    for tr in (512, 256, 128, 64, 32, 16, 8):
        if r % tr == 0 and tr * c * 4 <= (1 << 20):
            return tr
    return r


def _adamw_big(w, p_mine, p_sib, m, v, name):
    r, c = w.shape
    tr = _adam_rows(r, c)

    def body(w_ref, a_ref, b_ref, m_ref, v_ref, g_ref, d_ref, nm_ref, nv_ref):
        g = a_ref[...] + b_ref[...]
        g_ref[...] = g
        d_ref[...], nm_ref[...], nv_ref[...] = _adam_math(w_ref[...], g, m_ref[...], v_ref[...])

    blk = pl.BlockSpec((tr, c), lambda i: (i, 0))
    return pl.pallas_call(body, name=f"adamw_{name}", grid=(r // tr,), in_specs=[blk] * 5, out_specs=[blk] * 4,
                          out_shape=[jax.ShapeDtypeStruct((r, c), F32)] * 4,
                          compiler_params=_params(("parallel",)))(w, p_mine, p_sib, m, v)


def _adamw_small(w, parts, m, v):
    r, c = w.shape
    n_dev = parts.shape[0]

    def body(w_ref, p_ref, m_ref, v_ref, g_ref, d_ref, nm_ref, nv_ref):
        g = p_ref[0]
        for k in range(1, n_dev):
            g = g + p_ref[k]
        g_ref[...] = g
        d_ref[...], nm_ref[...], nv_ref[...] = _adam_math(w_ref[...], g, m_ref[...], v_ref[...])

    blk = pl.BlockSpec((r, c), lambda i: (0, 0))
    return pl.pallas_call(body, name="adamw_small", grid=(1,),
                          in_specs=[blk, pl.BlockSpec((n_dev, r, c), lambda i: (0, 0, 0)), blk, blk],
                          out_specs=[blk] * 4, out_shape=[jax.ShapeDtypeStruct((r, c), F32)] * 4,
                          compiler_params=_params(("arbitrary",)))(w, parts, m, v)


def _cast_bf16(w, name):
    r, c = w.shape
    tr = _adam_rows(r, c)

    def body(w_ref, o_ref):
        o_ref[...] = w_ref[...].astype(BF16)

    blk = pl.BlockSpec((tr, c), lambda i: (i, 0))
    return pl.pallas_call(body, name=f"cast_{name}", grid=(r // tr,), in_specs=[blk], out_specs=blk,
                          out_shape=jax.ShapeDtypeStruct((r, c), BF16), compiler_params=_params(("parallel",)))(w)


def _sum_slots(recv, name):
    _, r, c = recv.shape
    tr = _adam_rows(r, c)

    def body(p_ref, o_ref):
        acc = p_ref[0].astype(F32)
        for k in range(1, N_CHIPS):
            acc = acc + p_ref[k].astype(F32)
        o_ref[...] = acc

    return pl.pallas_call(body, name=f"sum_{name}", grid=(r // tr,),
                          in_specs=[pl.BlockSpec((N_CHIPS, tr, c), lambda i: (0, i, 0))],
                          out_specs=pl.BlockSpec((tr, c), lambda i: (i, 0)),
                          out_shape=jax.ShapeDtypeStruct((r, c), F32), compiler_params=_params(("parallel",)))(recv)


BIG_WEIGHTS = ("w_in", "w_attn_up", "w_glu_v", "w_glu_g", "w_out", "w_ffn_gate", "w_ffn_up", "w_ffn_down")
COL_SHARDED = ("w_in", "w_attn_up", "w_glu_v", "w_glu_g", "w_ffn_gate", "w_ffn_up")


def _shard_of(ref, name, j, shard_shape):
    r, c = shard_shape
    if name in COL_SHARDED:
        return ref.at[:, pl.ds(pl.multiple_of(j * c, 128), c)]
    return ref.at[pl.ds(pl.multiple_of(j * r, 8), r), :]


def _other_chips():
    x, y = lax.axis_index("x"), lax.axis_index("y")
    return [(1 - x, y), (x, 1 - y), (1 - x, 1 - y)]


def _three_sems(n):
    return [pltpu.SemaphoreType.DMA((n, 3)), pltpu.SemaphoreType.DMA((n, 3)), pltpu.SemaphoreType.DMA((n,))]


def _gather_side(shards):
    names = list(shards)
    n = len(names)
    full_shapes = []
    for k in names:
        r, c = shards[k].shape
        full_shapes.append((r, c * N_CHIPS) if k in COL_SHARDED else (r * N_CHIPS, c))

    def build(src, dst, sems):
        send_sems, recv_sems, local_sems = sems
        x, y, c = lax.axis_index("x"), lax.axis_index("y"), lax.axis_index("c")
        me = 2 * x + y
        locals_, sends, arrivals = [], [], []
        for i, k in enumerate(names):
            shape = shards[k].shape
            locals_.append(pltpu.make_async_copy(src[i], _shard_of(dst[i], k, me, shape), local_sems.at[i]))
            for p, (px, py) in enumerate(_other_chips()):
                sends.append(pltpu.make_async_remote_copy(
                    src_ref=src[i], dst_ref=_shard_of(dst[i], k, me, shape), send_sem=send_sems.at[i, p],
                    recv_sem=recv_sems.at[i, p], device_id=(px, py, c), device_id_type=MESH))
                arrivals.append(pltpu.make_async_remote_copy(
                    src_ref=src[i], dst_ref=_shard_of(dst[i], k, 2 * px + py, shape), send_sem=send_sems.at[i, p],
                    recv_sem=recv_sems.at[i, p], device_id=(px, py, c), device_id_type=MESH))
        return locals_, sends, arrivals

    return _Side([shards[k] for k in names], [jax.ShapeDtypeStruct(s, BF16) for s in full_shapes], _three_sems(n), build)


def _scatter_side(grads, shard_shapes):
    names = list(grads)
    n = len(names)

    def build(src, dst, sems):
        send_sems, recv_sems, local_sems = sems
        x, y, c = lax.axis_index("x"), lax.axis_index("y"), lax.axis_index("c")
        me = 2 * x + y
        locals_, sends, arrivals = [], [], []
        for i, k in enumerate(names):
            shape = shard_shapes[k]
            locals_.append(pltpu.make_async_copy(_shard_of(src[i], k, me, shape), dst[i].at[me], local_sems.at[i]))
            for p, (px, py) in enumerate(_other_chips()):
                peer = 2 * px + py
                sends.append(pltpu.make_async_remote_copy(
                    src_ref=_shard_of(src[i], k, peer, shape), dst_ref=dst[i].at[me], send_sem=send_sems.at[i, p],
                    recv_sem=recv_sems.at[i, p], device_id=(px, py, c), device_id_type=MESH))
                arrivals.append(pltpu.make_async_remote_copy(
                    src_ref=_shard_of(src[i], k, peer, shape), dst_ref=dst[i].at[peer], send_sem=send_sems.at[i, p],
                    recv_sem=recv_sems.at[i, p], device_id=(px, py, c), device_id_type=MESH))
        return locals_, sends, arrivals

    return _Side([grads[k] for k in names],
                 [jax.ShapeDtypeStruct((N_CHIPS,) + tuple(shard_shapes[k]), BF16) for k in names], _three_sems(n), build)


def _swap_with_sibling(parts):
    names = list(parts)
    n = len(names)

    def body(*refs):
        src, dst = refs[:n], refs[n:2 * n]
        send_sems, recv_sems = refs[2 * n:]
        sibling = (lax.axis_index("x"), lax.axis_index("y"), 1 - lax.axis_index("c"))
        copies = []
        for i in range(n):
            cp = pltpu.make_async_remote_copy(src_ref=src[i], dst_ref=dst[i], send_sem=send_sems.at[i],
                                              recv_sem=recv_sems.at[i], device_id=sibling, device_id_type=MESH)
            cp.start()
            copies.append(cp)
        for cp in copies:
            cp.wait_recv()
        for cp in copies:
            cp.wait_send()

    outs = pl.pallas_call(
        body, name="swap_with_sibling", in_specs=[HBM] * n, out_specs=[HBM] * n,
        out_shape=[jax.ShapeDtypeStruct(parts[k].shape, F32) for k in names],
        scratch_shapes=[pltpu.SemaphoreType.DMA((n,)), pltpu.SemaphoreType.DMA((n,))],
        compiler_params=pltpu.CompilerParams(has_side_effects=True),
    )(*[parts[k] for k in names])
    return dict(zip(names, outs))


def _share_small(packed):
    r, c = packed.shape
    n_dev = 8

    def body(src, dst, send_sems, recv_sems, local_sem):
        x, y, cc = lax.axis_index("x"), lax.axis_index("y"), lax.axis_index("c")
        me = 4 * x + 2 * y + cc
        own = pltpu.make_async_copy(src, dst.at[me], local_sem)
        own.start()
        sends, arrivals = [], []
        p = 0
        for fx in range(2):
            for fy in range(2):
                for fc in range(2):
                    if fx == fy == fc == 0:
                        continue
                    px, py, pc = x ^ fx, y ^ fy, cc ^ fc
                    out = pltpu.make_async_remote_copy(src_ref=src, dst_ref=dst.at[me], send_sem=send_sems.at[p],
                                                       recv_sem=recv_sems.at[p], device_id=(px, py, pc), device_id_type=MESH)
                    out.start()
                    sends.append(out)
                    arrivals.append(pltpu.make_async_remote_copy(
                        src_ref=src, dst_ref=dst.at[4 * px + 2 * py + pc], send_sem=send_sems.at[p],
                        recv_sem=recv_sems.at[p], device_id=(px, py, pc), device_id_type=MESH))
                    p += 1
        for a in arrivals:
            a.wait_recv()
        for cp in sends:
            cp.wait_send()
        own.wait()

    return pl.pallas_call(
        body, name="share_small", in_specs=[HBM], out_specs=HBM,
        out_shape=jax.ShapeDtypeStruct((n_dev, r, c), F32),
        scratch_shapes=[pltpu.SemaphoreType.DMA((7,)), pltpu.SemaphoreType.DMA((7,)), pltpu.SemaphoreType.DMA],
        compiler_params=pltpu.CompilerParams(has_side_effects=True),
    )(packed)


SMALL_WEIGHTS = ("norm_mix_pre", "ssm_a_re", "ssm_a_im", "ssm_log_dt", "ssm_b_re", "ssm_b_im", "ssm_c_re", "ssm_c_im",
                 "ssm_d", "norm_mix_post", "norm_ffn_pre", "norm_ffn_post")
WEIGHT_ORDER = ("norm_mix_pre", "w_in", "w_attn_up", "ssm_a_re", "ssm_a_im", "ssm_log_dt", "ssm_b_re", "ssm_b_im",
                "ssm_c_re", "ssm_c_im", "ssm_d", "w_glu_v", "w_glu_g", "w_out", "norm_mix_post", "norm_ffn_pre",
                "w_ffn_gate", "w_ffn_up", "w_ffn_down", "norm_ffn_post")
PACK_LANES = 128
PACK_ROWS = 8


def _pack_small(arrs):
    flat = jnp.concatenate([arrs[k].reshape(-1) for k in SMALL_WEIGHTS])
    pad = -flat.shape[0] % (PACK_LANES * PACK_ROWS)
    return jnp.pad(flat, (0, pad)).reshape(-1, PACK_LANES)


def _unpack_small(packed, like):
    flat = packed.reshape(-1)
    out, pos = {}, 0
    for k in SMALL_WEIGHTS:
        n = like[k].size
        out[k] = flat[pos:pos + n].reshape(like[k].shape)
        pos += n
    return out


GATHER_BEHIND_IN_PROJ = ("w_attn_up", "w_glu_v", "w_glu_g", "w_out", "w_ffn_gate")
GATHER_BEHIND_SSM = ("w_ffn_up", "w_ffn_down")
SCATTER_BEHIND_SSM = ("w_ffn_down", "w_ffn_gate", "w_ffn_up", "w_out", "w_glu_v", "w_glu_g")
SCATTER_BEHIND_D_H1 = ("w_attn_up", "w_in")


def _local_step(x, target, big, small, shards=None, shard_shapes=None):
    s, d = x.shape
    big = dict(big)
    carry = shards is not None

    def gather_side(names):
        return _gather_side({k: shards[k] for k in names}) if carry else None

    def scatter_side(names):
        return _scatter_side({k: grads[k] for k in names}, shard_shapes) if carry else None
    u_off = 3 * HQ
    gate_off = u_off + d // 2
    g1, g2, g3, g4 = (small[k][0:1] for k in ("norm_mix_pre", "norm_mix_post", "norm_ffn_pre", "norm_ffn_post"))
    ssm_names = ("ssm_a_re", "ssm_a_im", "ssm_log_dt", "ssm_b_re", "ssm_b_im", "ssm_c_re", "ssm_c_im")
    (lam, bmat, cmat), ssm_vjp = jax.vjp(_ssm_prepare, *[small[k][0] for k in ssm_names])
    bmat, cmat = bmat.astype(BF16), cmat.astype(BF16)
    dskip = small["ssm_d"][0:1]

    h1 = _norm_in(x, g1)
    z = _mm(h1, big["w_in"], "nn", F32, "in_proj", side=gather_side(GATHER_BEHIND_IN_PROJ))
    if carry:
        z, got = z
        big.update(zip(GATHER_BEHIND_IN_PROJ, got))
    ssm_out = _ssm_fwd(z, bmat, cmat, lam, dskip, u_off, side=gather_side(GATHER_BEHIND_SSM))
    if carry:
        ssm_out, got = ssm_out
        big.update(zip(GATHER_BEHIND_SSM, got))
    y, yg, xin = ssm_out
    qkv = [_dilate_qkv(z, g, dil) for g, dil in enumerate(ATTN_DILATIONS)]
    outs, lses = zip(*[_attn_fwd(qkv[g], g, dil) for g, dil in enumerate(ATTN_DILATIONS)])
    attn = _attn_merge(outs, lses)
    merged, ab, gv, gg = _mm_fused(
        [attn, yg], [big["w_attn_up"], big["w_glu_v"], big["w_glu_g"]], [(0, 0), (1, 1), (1, 2)], "nn",
        [BF16, F32, F32, F32], "branches_merge", extras=[(z, gate_off), (z, gate_off + d)], epilogue=_gates_epilogue)
    mo = _mm(merged, big["w_out"], "nn", F32, "mix_out")
    x2, h2 = _norm_mid(x, mo, g2, g3)
    act, fg, fu = _mm_fused([h2], [big["w_ffn_gate"], big["w_ffn_up"]], [(0, 0), (0, 1)], "nn", [BF16, BF16, BF16],
                            "ffn_up_act", epilogue=_swiglu_epilogue)
    f = _mm(act, big["w_ffn_down"], "nn", F32, "ffn_down")
    loss, dout, df, dg4 = _loss_head(x2, f, g4, target)

    grads = {}
    dfg, dfu = _mm_fused([df], [big["w_ffn_down"]], [(0, 0)], "nt", [BF16, BF16], "d_ffn_act",
                         extras=[(fg, 0), (fu, 0)], epilogue=_swiglu_bwd_epilogue)
    grads["w_ffn_down"] = _mm_kloop(act, df, "tn", BF16, "dw_ffn_down")
    dh2 = _mm_kloop(dfg, big["w_ffn_gate"], "nt", F32, "d_h2_gate")
    dh2 = _mm_kloop(dfu, big["w_ffn_up"], "nt", F32, "d_h2_up", add=dh2)
    grads["w_ffn_gate"] = _mm_kloop(h2, dfg, "tn", BF16, "dw_ffn_gate")
    grads["w_ffn_up"] = _mm_kloop(h2, dfu, "tn", BF16, "dw_ffn_up")
    dx2, dmo, dg2, dg3 = _norm_mid_bwd(x2, mo, g2, g3, dout, dh2)
    dmerged = _mm(dmo, big["w_out"], "nt", F32, "d_merged")
    grads["w_out"] = _mm_kloop(merged, dmo, "tn", BF16, "dw_out")
    dz, dab = _merge_gate_a_bwd(z, ab, dmerged, gate_off)
    dz, dgv, dgg = _merge_gate_s_bwd(z, gv, gg, dmerged, dz, gate_off)
    dyg = _mm_fused([dgv, dgg], [big["w_glu_v"], big["w_glu_g"]], [(0, 0), (1, 1)], "nt", [F32], "d_yg",
                    epilogue=_sum_epilogue)[0]
    grads["w_glu_v"] = _mm_kloop(yg, dgv, "tn", BF16, "dw_glu_v")
    grads["w_glu_g"] = _mm_kloop(yg, dgg, "tn", BF16, "dw_glu_g")
    slots = {}
    ssm_grads = _ssm_bwd(z, y, dyg, xin, bmat, cmat, lam, dskip, u_off, side=scatter_side(SCATTER_BEHIND_SSM))
    if carry:
        ssm_grads, got = ssm_grads
        slots.update(zip(SCATTER_BEHIND_SSM, got))
    du, dbmat, dcmat, dlam, dd = ssm_grads
    dz = _put_cols(dz, du, u_off)
    dattn = _mm(dab, big["w_attn_up"], "nt", F32, "d_attn")
    grads["w_attn_up"] = _mm_kloop(attn, dab, "tn", BF16, "dw_attn_up")
    merged_bwd = _attn_merge_bwd(outs, lses, dattn)
    for g, dil in enumerate(ATTN_DILATIONS):
        dqkv = _attn_bwd(qkv[g], merged_bwd[g], lses[g], merged_bwd[3 + g], g, dil)
        dz = _undilate_dqkv(dqkv, dz, g, dil)
    grads["w_in"] = _mm_kloop(h1, dz, "tn", BF16, "dw_in")
    dh1 = _mm_kloop(dz, big["w_in"], "nt", F32, "d_h1", side=scatter_side(SCATTER_BEHIND_D_H1))
    if carry:
        dh1, got = dh1
        slots.update(zip(SCATTER_BEHIND_D_H1, got))
    grad_x, dg1 = _norm_in_bwd(x, g1, dh1, dx2)

    small_grads = dict(zip(ssm_names, (t[None] for t in ssm_vjp((dlam, dbmat, dcmat)))))
    small_grads.update(norm_mix_pre=dg1, norm_mix_post=dg2, norm_ffn_pre=dg3, norm_ffn_post=dg4, ssm_d=dd)
    return loss[0, 0], grad_x, slots if carry else grads, small_grads


def kernel(x, norm_mix_pre, w_in, w_attn_up, ssm_a_re, ssm_a_im, ssm_log_dt, ssm_b_re, ssm_b_im, ssm_c_re, ssm_c_im, ssm_d, w_glu_v, w_glu_g, w_out, norm_mix_post, norm_ffn_pre, w_ffn_gate, w_ffn_up, w_ffn_down, norm_ffn_post, loss_target, m_norm_mix_pre, m_w_in, m_w_attn_up, m_ssm_a_re, m_ssm_a_im, m_ssm_log_dt, m_ssm_b_re, m_ssm_b_im, m_ssm_c_re, m_ssm_c_im, m_ssm_d, m_w_glu_v, m_w_glu_g, m_w_out, m_norm_mix_post, m_norm_ffn_pre, m_w_ffn_gate, m_w_ffn_up, m_w_ffn_down, m_norm_ffn_post, v_norm_mix_pre, v_w_in, v_w_attn_up, v_ssm_a_re, v_ssm_a_im, v_ssm_log_dt, v_ssm_b_re, v_ssm_b_im, v_ssm_c_re, v_ssm_c_im, v_ssm_d, v_w_glu_v, v_w_glu_g, v_w_out, v_norm_mix_post, v_norm_ffn_pre, v_w_ffn_gate, v_w_ffn_up, v_w_ffn_down, v_norm_ffn_post):
    given = dict(locals())
    w = {k: given[k] for k in WEIGHT_ORDER}
    m = {k: given["m_" + k] for k in WEIGHT_ORDER}
    v = {k: given["v_" + k] for k in WEIGHT_ORDER}

    shards = {k: _cast_bf16(w[k][0], k) for k in BIG_WEIGHTS}
    shard_shapes = {k: w[k].shape[1:] for k in BIG_WEIGHTS}
    big = {"w_in": _run_side(_gather_side({"w_in": shards["w_in"]}), "gather_w_in")[0]}

    loss, grad_x, slots, small_grads = _local_step(x[0], loss_target[0], big, {k: w[k] for k in SMALL_WEIGHTS},
                                                   shards, shard_shapes)
    loss = lax.psum(loss, MESH_AXES)

    mine = {k: _sum_slots(slots[k], k) for k in BIG_WEIGHTS}
    theirs = _swap_with_sibling(mine)
    out_g, out_d, out_m, out_v = {}, {}, {}, {}
    for k in BIG_WEIGHTS:
        res = _adamw_big(w[k][0], mine[k], theirs[k], m[k][0], v[k][0], k)
        out_g[k], out_d[k], out_m[k], out_v[k] = (t[None] for t in res)

    pick = lambda tree: {k: tree[k] for k in SMALL_WEIGHTS}
    parts = _share_small(_pack_small(small_grads))
    res = _adamw_small(_pack_small(pick(w)), parts, _pack_small(pick(m)), _pack_small(pick(v)))
    for dst, packed in zip((out_g, out_d, out_m, out_v), res):
        dst.update(_unpack_small(packed, pick(w)))

    return (loss, grad_x[None], *[out_g[k] for k in WEIGHT_ORDER], *[out_d[k] for k in WEIGHT_ORDER],
            *[out_m[k] for k in WEIGHT_ORDER], *[out_v[k] for k in WEIGHT_ORDER])
```

```python
import functools
import math

import jax
import jax.numpy as jnp
from jax import lax
from jax.experimental import pallas as pl
from jax.experimental.pallas import tpu as pltpu

F32 = jnp.float32
BF16 = jnp.bfloat16

EPS = 1e-6
HEAD_DIM = 128
HEADS_PER_GROUP = 4
ATTN_DILATIONS = (1, 4, 16)
ATTN_BLK = 128
N_ATTN_HEADS = HEADS_PER_GROUP * len(ATTN_DILATIONS)
GROUP_W = HEADS_PER_GROUP * HEAD_DIM
HQ = N_ATTN_HEADS * HEAD_DIM
SSM_GROUP = 16
SSM_STATE = 64
SSM_TILE_CH = 128
SSM_TILE_ST = SSM_TILE_CH // SSM_GROUP * SSM_STATE
ADAM_LR = 0.001
ADAM_B1 = 0.9
ADAM_B2 = 0.999
ADAM_EPS = 1e-08
ADAM_WD = 0.01
ADAM_STEP = 10
NEG_BIG = -1e30
V7X_VMEM_LIMIT = 56 * 1024 * 1024
MESH_AXES = ("x", "y", "c")
N_CHIPS = 4


def _pick(n, cands):
    for c in cands:
        if n % c == 0:
            return c
    raise ValueError(f"no tile of {cands} divides {n}")


def _params(sem):
    return pltpu.CompilerParams(dimension_semantics=sem, vmem_limit_bytes=V7X_VMEM_LIMIT)


HBM = pl.BlockSpec(memory_space=pl.ANY)
MESH = pl.DeviceIdType.MESH


class _Side:
    def __init__(self, srcs, out_shapes, sem_shapes, build):
        self.srcs, self.out_shapes, self.sem_shapes, self.build = list(srcs), list(out_shapes), list(sem_shapes), build

    def start(self, src, dst, sems):
        local, sends = self.build(src, dst, sems)[:2]
        for cp in local + sends:
            cp.start()

    def wait(self, src, dst, sems):
        local, sends, arrivals, forwards, passed_on = self.build(src, dst, sems)
        for k, cp in enumerate(arrivals):
            cp.wait_recv()
            if forwards:
                forwards[k].start()
        for cp in passed_on:
            cp.wait_recv()
        for cp in sends + forwards:
            cp.wait_send()
        for cp in local:
            cp.wait()


def _call(body, *, name, grid, in_specs, out_specs, out_shape, semantics, args, scratch_shapes=(), side=None, **kw):
    in_specs, out_specs, out_shape, scratch_shapes = list(in_specs), list(out_specs), list(out_shape), list(scratch_shapes)
    if side is None:
        res = pl.pallas_call(body, name=name, grid=grid, in_specs=in_specs, out_specs=out_specs, out_shape=out_shape,
                             scratch_shapes=scratch_shapes, compiler_params=_params(semantics), **kw)(*args)
        return list(res), []
    n_in, n_out, n_scr = len(in_specs), len(out_specs), len(scratch_shapes)
    ns_in, ns_out = len(side.srcs), len(side.out_shapes)

    def carrying(*refs):
        ins, s_in = refs[:n_in], refs[n_in:n_in + ns_in]
        o0 = n_in + ns_in
        outs, s_out = refs[o0:o0 + n_out], refs[o0 + n_out:o0 + n_out + ns_out]
        c0 = o0 + n_out + ns_out
        scr, sems = refs[c0:c0 + n_scr], refs[c0 + n_scr:]
        ids = [pl.program_id(a) for a in range(len(grid))]
        first = functools.reduce(jnp.logical_and, [i == 0 for i in ids])
        last = functools.reduce(jnp.logical_and, [i == g - 1 for i, g in zip(ids, grid)])

        @pl.when(first)
        def _():
            side.start(s_in, s_out, sems)

        body(*ins, *outs, *scr)

        @pl.when(last)
        def _():
            side.wait(s_in, s_out, sems)

    res = pl.pallas_call(
        carrying, name=name, grid=grid, in_specs=in_specs + [HBM] * ns_in, out_specs=out_specs + [HBM] * ns_out,
        out_shape=out_shape + side.out_shapes, scratch_shapes=scratch_shapes + side.sem_shapes,
        compiler_params=pltpu.CompilerParams(dimension_semantics=("arbitrary",) * len(grid),
                                             vmem_limit_bytes=V7X_VMEM_LIMIT, has_side_effects=True), **kw,
    )(*args, *side.srcs)
    return list(res[:n_out]), list(res[n_out:])


def _run_side(side, name):
    ns, no = len(side.srcs), len(side.out_shapes)

    def body(*refs):
        src, dst, sems = refs[:ns], refs[ns:ns + no], refs[ns + no:]
        side.start(src, dst, sems)
        side.wait(src, dst, sems)

    return list(pl.pallas_call(body, name=name, in_specs=[HBM] * ns, out_specs=[HBM] * no, out_shape=side.out_shapes,
                               scratch_shapes=side.sem_shapes,
                               compiler_params=pltpu.CompilerParams(has_side_effects=True))(*side.srcs))


_DOT_DIMS = {"nn": (((1,), (0,)), ((), ())), "nt": (((1,), (1,)), ((), ())), "tn": (((0,), (0,)), ((), ()))}


MM_VMEM_BUDGET = 44 * 1024 * 1024
MM_STEP_BYTES = 1 << 20


def _size(dtype):
    return jnp.dtype(dtype).itemsize


def _mm_fused(as_, bs, pairs, mode, out_dtypes, name, extras=(), epilogue=None, side=None):
    M = as_[0].shape[0]
    N = bs[0].shape[1] if mode == "nn" else bs[0].shape[0]
    ks_a = [a.shape[1] for a in as_]
    ks_b = [b.shape[0] if mode == "nn" else b.shape[1] for b in bs]
    if epilogue is None:
        epilogue = lambda rs, es: rs
    offs = [off for _, off in extras]
    best = None
    for tm in (2048, 1024, 512, 256, 128):
        for tn in (2048, 1024, 512, 256, 128):
            if M % tm or N % tn or any(off % tn for off in offs):
                continue
            vmem = (sum(2 * tm * k * 2 for k in ks_a) + sum(2 * k * tn * 2 for k in ks_b)
                    + sum(2 * tm * tn * _size(d) for d in out_dtypes) + sum(2 * tm * tn * _size(e.dtype) for e, _ in extras)
                    + len(pairs) * tm * tn * 4)
            cost = sum(k * N * 2 for k in ks_b) * (M // tm) + (M // tm) * (N // tn) * MM_STEP_BYTES
            if vmem <= MM_VMEM_BUDGET and (best is None or cost < best[0]):
                best = (cost, tm, tn)
    _, tm, tn = best
    na, nb, ne, no = len(as_), len(bs), len(extras), len(out_dtypes)
    dims = _DOT_DIMS[mode]

    def body(*refs):
        a_refs, b_refs = refs[:na], refs[na:na + nb]
        e_refs, o_refs = refs[na + nb:na + nb + ne], refs[na + nb + ne:]
        rs = [lax.dot_general(a_refs[ai][...], b_refs[bi][...], dims, preferred_element_type=F32) for ai, bi in pairs]
        outs = epilogue(rs, [e[...] for e in e_refs])
        for o_ref, o in zip(o_refs, outs):
            o_ref[...] = o.astype(o_ref.dtype)

    a_specs = [pl.BlockSpec((tm, k), lambda i, j: (i, 0)) for k in ks_a]
    if mode == "nn":
        b_specs = [pl.BlockSpec((k, tn), lambda i, j: (0, j)) for k in ks_b]
    else:
        b_specs = [pl.BlockSpec((tn, k), lambda i, j: (j, 0)) for k in ks_b]
    e_specs = [pl.BlockSpec((tm, tn), lambda i, j, o=off // tn: (i, o + j)) for off in offs]
    o_spec = pl.BlockSpec((tm, tn), lambda i, j: (i, j))
    outs, carried = _call(
        body, name=name, grid=(M // tm, N // tn), in_specs=a_specs + b_specs + e_specs, out_specs=[o_spec] * no,
        out_shape=[jax.ShapeDtypeStruct((M, N), d) for d in out_dtypes], semantics=("parallel", "arbitrary"),
        args=[*as_, *bs, *[e for e, _ in extras]], side=side)
    return outs if side is None else (outs, carried)


def _mm(a, b, mode, out_dtype, name, side=None):
    res = _mm_fused([a], [b], [(0, 0)], mode, [out_dtype], name, side=side)
    return res[0] if side is None else (res[0][0], res[1])


def _mm_kloop(a, b, mode, out_dtype, name, add=None, side=None):
    if mode == "nn":
        (M, K), (_, N) = a.shape, b.shape
    elif mode == "nt":
        (M, K), (N, _) = a.shape, b.shape
    else:
        (K, M), (_, N) = a.shape, b.shape
    best = None
    for tm in (2816, 2048, 1408, 1024, 512, 256, 128):
        for tn in (2816, 2432, 2048, 1408, 1024, 512, 256, 128):
            for tk in (1024, 512, 256, 128):
                if M % tm or N % tn or K % tk:
                    continue
                vmem = (2 * tm * tn * 4 + 2 * tm * tn * _size(out_dtype) + 2 * tk * (tm + tn) * 2
                        + (2 * tm * tn * 4 if add is not None else 0))
                steps = (M // tm) * (N // tn) * (K // tk)
                cost = K * M * 2 * (N // tn) + K * N * 2 * (M // tm) + steps * MM_STEP_BYTES
                if vmem <= MM_VMEM_BUDGET and (best is None or cost < best[0]):
                    best = (cost, tm, tn, tk)
    _, tm, tn, tk = best
    nk = K // tk
    dims = _DOT_DIMS[mode]

    def body(*refs):
        if add is None:
            a_ref, b_ref, o_ref, acc_ref = refs
        else:
            a_ref, b_ref, add_ref, o_ref, acc_ref = refs
        k = pl.program_id(2)

        @pl.when(k == 0)
        def _():
            acc_ref[...] = jnp.zeros_like(acc_ref)

        acc_ref[...] += lax.dot_general(a_ref[...], b_ref[...], dims, preferred_element_type=F32)

        @pl.when(k == nk - 1)
        def _():
            r = acc_ref[...]
            if add is not None:
                r = r + add_ref[...]
            o_ref[...] = r.astype(o_ref.dtype)

    a_spec = pl.BlockSpec((tk, tm), lambda i, j, k: (k, i)) if mode == "tn" else pl.BlockSpec((tm, tk), lambda i, j, k: (i, k))
    b_spec = pl.BlockSpec((tn, tk), lambda i, j, k: (j, k)) if mode == "nt" else pl.BlockSpec((tk, tn), lambda i, j, k: (k, j))
    o_spec = pl.BlockSpec((tm, tn), lambda i, j, k: (i, j))
    outs, carried = _call(
        body, name=name, grid=(M // tm, N // tn, nk),
        in_specs=[a_spec, b_spec] + ([o_spec] if add is not None else []), out_specs=[o_spec],
        out_shape=[jax.ShapeDtypeStruct((M, N), out_dtype)], scratch_shapes=[pltpu.VMEM((tm, tn), F32)],
        semantics=("parallel", "parallel", "arbitrary"), args=(a, b) + ((add,) if add is not None else ()), side=side)
    return outs[0] if side is None else (outs[0], carried)


def _sigmoid(v):
    return 1.0 / (1.0 + jnp.exp(-v))


_GELU_C = math.sqrt(2.0 / math.pi)


def _gelu(v):
    return 0.5 * v * (1.0 + jnp.tanh(_GELU_C * (v + 0.044715 * v * v * v)))


def _gelu_grad(v):
    t = jnp.tanh(_GELU_C * (v + 0.044715 * v * v * v))
    return 0.5 * (1.0 + t) + 0.5 * v * (1.0 - t * t) * _GELU_C * (1.0 + 3.0 * 0.044715 * v * v)


def _rms(v, gain):
    r = lax.rsqrt(jnp.mean(v * v, axis=-1, keepdims=True) + EPS)
    return v * r * gain


def _rms_bwd(v, gain, dy):
    r = lax.rsqrt(jnp.mean(v * v, axis=-1, keepdims=True) + EPS)
    a = dy * gain
    dv = r * a - v * (r * r * r) * jnp.mean(a * v, axis=-1, keepdims=True)
    return dv, dy * v * r


def _row_tile(s):
    return _pick(s, (256, 128, 64, 8))


def _norm_in(x, gain):
    s, d = x.shape
    tr = _row_tile(s)

    def body(x_ref, g_ref, h_ref):
        h_ref[...] = _rms(x_ref[...], g_ref[...]).astype(BF16)

    row = pl.BlockSpec((tr, d), lambda i: (i, 0))
    vec = pl.BlockSpec((1, d), lambda i: (0, 0))
    return pl.pallas_call(body, name="norm_in", grid=(s // tr,), in_specs=[row, vec], out_specs=row,
                          out_shape=jax.ShapeDtypeStruct((s, d), BF16), compiler_params=_params(("parallel",)))(x, gain)


def _norm_mid(x, mo, g_post, g_pre):
    s, d = x.shape
    tr = _row_tile(s)

    def body(x_ref, mo_ref, g2_ref, g3_ref, x2_ref, h2_ref):
        x2 = x_ref[...] + _rms(mo_ref[...], g2_ref[...])
        x2_ref[...] = x2
        h2_ref[...] = _rms(x2, g3_ref[...]).astype(BF16)

    row = pl.BlockSpec((tr, d), lambda i: (i, 0))
    vec = pl.BlockSpec((1, d), lambda i: (0, 0))
    return pl.pallas_call(
        body, name="norm_mid", grid=(s // tr,), in_specs=[row, row, vec, vec], out_specs=[row, row],
        out_shape=[jax.ShapeDtypeStruct((s, d), F32), jax.ShapeDtypeStruct((s, d), BF16)],
        compiler_params=_params(("parallel",)))(x, mo, g_post, g_pre)


def _loss_head(x2, f, g_post, target):
    s, d = x2.shape
    tr = _row_tile(s)

    def body(x2_ref, f_ref, g_ref, t_ref, loss_ref, dout_ref, df_ref, dg_ref):
        @pl.when(pl.program_id(0) == 0)
        def _():
            loss_ref[...] = jnp.zeros_like(loss_ref)
            dg_ref[...] = jnp.zeros_like(dg_ref)

        fv = f_ref[...]
        g = g_ref[...]
        err = x2_ref[...] + _rms(fv, g) - t_ref[...]
        loss_ref[...] += 0.5 * jnp.sum(jnp.mean(err * err, axis=-1, keepdims=True), axis=0, keepdims=True)
        dout = err * (1.0 / d)
        dout_ref[...] = dout
        df, dg = _rms_bwd(fv, g, dout)
        df_ref[...] = df.astype(BF16)
        dg_ref[...] += jnp.sum(dg, axis=0, keepdims=True)

    row = pl.BlockSpec((tr, d), lambda i: (i, 0))
    vec = pl.BlockSpec((1, d), lambda i: (0, 0))
    one = pl.BlockSpec((1, 1), lambda i: (0, 0))
    return pl.pallas_call(
        body, name="loss_head", grid=(s // tr,), in_specs=[row, row, vec, row], out_specs=[one, row, row, vec],
        out_shape=[jax.ShapeDtypeStruct((1, 1), F32), jax.ShapeDtypeStruct((s, d), F32),
                   jax.ShapeDtypeStruct((s, d), BF16), jax.ShapeDtypeStruct((1, d), F32)],
        compiler_params=_params(("arbitrary",)))(x2, f, g_post, target)


def _norm_mid_bwd(x2, mo, g_post, g_pre, dout, dh2):
    s, d = x2.shape
    tr = _row_tile(s)

    def body(x2_ref, mo_ref, g2_ref, g3_ref, dout_ref, dh2_ref, dx2_ref, dmo_ref, dg2_ref, dg3_ref):
        @pl.when(pl.program_id(0) == 0)
        def _():
            dg2_ref[...] = jnp.zeros_like(dg2_ref)
            dg3_ref[...] = jnp.zeros_like(dg3_ref)

        dv, dg3 = _rms_bwd(x2_ref[...], g3_ref[...], dh2_ref[...])
        dx2 = dout_ref[...] + dv
        dx2_ref[...] = dx2
        dmo, dg2 = _rms_bwd(mo_ref[...], g2_ref[...], dx2)
        dmo_ref[...] = dmo.astype(BF16)
        dg2_ref[...] += jnp.sum(dg2, axis=0, keepdims=True)
        dg3_ref[...] += jnp.sum(dg3, axis=0, keepdims=True)

    row = pl.BlockSpec((tr, d), lambda i: (i, 0))
    vec = pl.BlockSpec((1, d), lambda i: (0, 0))
    return pl.pallas_call(
        body, name="norm_mid_bwd", grid=(s // tr,), in_specs=[row, row, vec, vec, row, row],
        out_specs=[row, row, vec, vec],
        out_shape=[jax.ShapeDtypeStruct((s, d), F32), jax.ShapeDtypeStruct((s, d), BF16),
                   jax.ShapeDtypeStruct((1, d), F32), jax.ShapeDtypeStruct((1, d), F32)],
        compiler_params=_params(("arbitrary",)))(x2, mo, g_post, g_pre, dout, dh2)


def _norm_in_bwd(x, gain, dh, dx2):
    s, d = x.shape
    tr = _row_tile(s)

    def body(x_ref, g_ref, dh_ref, dx2_ref, dx_ref, dg_ref):
        @pl.when(pl.program_id(0) == 0)
        def _():
            dg_ref[...] = jnp.zeros_like(dg_ref)

        dv, dg = _rms_bwd(x_ref[...], g_ref[...], dh_ref[...])
        dx_ref[...] = dx2_ref[...] + dv
        dg_ref[...] += jnp.sum(dg, axis=0, keepdims=True)

    row = pl.BlockSpec((tr, d), lambda i: (i, 0))
    vec = pl.BlockSpec((1, d), lambda i: (0, 0))
    return pl.pallas_call(
        body, name="norm_in_bwd", grid=(s // tr,), in_specs=[row, vec, row, row], out_specs=[row, vec],
        out_shape=[jax.ShapeDtypeStruct((s, d), F32), jax.ShapeDtypeStruct((1, d), F32)],
        compiler_params=_params(("arbitrary",)))(x, gain, dh, dx2)


def _swiglu_epilogue(rs, es):
    g, u = rs
    return [g * _sigmoid(g) * u, g, u]


def _swiglu_bwd_epilogue(rs, es):
    d = rs[0]
    g, u = es[0].astype(F32), es[1].astype(F32)
    sg = _sigmoid(g)
    return [d * u * sg * (1.0 + g * (1.0 - sg)), d * g * sg]


def _sum_epilogue(rs, es):
    return [rs[0] + rs[1]]


def _gates_epilogue(rs, es):
    ab, gv, gg = rs
    ga, gs = es
    return [_sigmoid(ga) * ab + _sigmoid(gs) * gv * _sigmoid(gg), ab, gv, gg]


def _gate_cols(d, gate_off):
    tc = _pick(math.gcd(d, gate_off), (512, 256, 128))
    return tc, gate_off // tc, d // tc


def _merge_gate_a_bwd(z, ab, dmerged, gate_off):
    s, d = ab.shape
    tr = _row_tile(s)
    tc, off, nd = _gate_cols(d, gate_off)

    def body(ga_ref, ab_ref, dm_ref, dga_ref, dab_ref):
        dm = dm_ref[...]
        sa = _sigmoid(ga_ref[...])
        dga_ref[...] = (dm * ab_ref[...] * sa * (1.0 - sa)).astype(BF16)
        dab_ref[...] = (dm * sa).astype(BF16)

    blk = pl.BlockSpec((tr, tc), lambda i, j: (i, j))
    ga = pl.BlockSpec((tr, tc), lambda i, j: (i, off + j))
    return pl.pallas_call(
        body, name="merge_gate_a_bwd", grid=(s // tr, nd), in_specs=[ga, blk, blk], out_specs=[ga, blk],
        out_shape=[jax.ShapeDtypeStruct(z.shape, BF16), jax.ShapeDtypeStruct((s, d), BF16)],
        compiler_params=_params(("parallel", "parallel")))(z, ab, dmerged)


def _merge_gate_s_bwd(z, gv, gg, dmerged, dz, gate_off):
    s, d = gv.shape
    tr = _row_tile(s)
    tc, off, nd = _gate_cols(d, gate_off)

    def body(gs_ref, gv_ref, gg_ref, dm_ref, dz_ref, dgs_ref, dgv_ref, dgg_ref):
        del dz_ref
        ss = _sigmoid(gs_ref[...])
        sg = _sigmoid(gg_ref[...])
        gv_ = gv_ref[...]
        dm = dm_ref[...]
        dgs_ref[...] = (dm * gv_ * sg * ss * (1.0 - ss)).astype(BF16)
        dsb = dm * ss
        dgv_ref[...] = (dsb * sg).astype(BF16)
        dgg_ref[...] = (dsb * gv_ * sg * (1.0 - sg)).astype(BF16)

    blk = pl.BlockSpec((tr, tc), lambda i, j: (i, j))
    gs = pl.BlockSpec((tr, tc), lambda i, j: (i, off + nd + j))
    return pl.pallas_call(
        body, name="merge_gate_s_bwd", grid=(s // tr, nd),
        in_specs=[gs, blk, blk, blk, pl.BlockSpec(memory_space=pl.ANY)], out_specs=[gs, blk, blk],
        out_shape=[jax.ShapeDtypeStruct(z.shape, BF16)] + [jax.ShapeDtypeStruct((s, d), BF16)] * 2,
        input_output_aliases={4: 0},
        compiler_params=_params(("parallel", "parallel")))(z, gv, gg, dmerged, dz)


def _put_cols(dz, src, col_off):
    s, w = src.shape
    tr = _row_tile(s)
    tc = _pick(math.gcd(w, col_off), (512, 256, 128))
    off = col_off // tc

    def body(src_ref, dz_ref, o_ref):
        del dz_ref
        o_ref[...] = src_ref[...].astype(o_ref.dtype)

    return pl.pallas_call(
        body, name="put_cols", grid=(s // tr, w // tc),
        in_specs=[pl.BlockSpec((tr, tc), lambda i, j: (i, j)), pl.BlockSpec(memory_space=pl.ANY)],
        out_specs=pl.BlockSpec((tr, tc), lambda i, j: (i, off + j)),
        out_shape=jax.ShapeDtypeStruct(dz.shape, dz.dtype), input_output_aliases={1: 0},
        compiler_params=_params(("parallel", "parallel")))(src, dz)


ATTN_ROWS = 2048


def _dilate_qkv(z, g, d):
    s = z.shape[0]
    tm = ATTN_ROWS
    per = tm // d
    nh = HEADS_PER_GROUP

    def body(z_ref, o_ref):
        for r in range(d):
            rows = z_ref[...] if d == 1 else z_ref[pl.ds(r, per, stride=d), :]
            o_ref[0, r] = rows.astype(BF16)

    return pl.pallas_call(
        body, name=f"dilate_qkv_{g}", grid=(s // tm, 3, nh),
        in_specs=[pl.BlockSpec((tm, HEAD_DIM), lambda i, w, h: (i, (3 * w + g) * nh + h))],
        out_specs=pl.BlockSpec((1, d, per, HEAD_DIM), lambda i, w, h: (w, 0, i, h)),
        out_shape=jax.ShapeDtypeStruct((3, d, s // d, GROUP_W), BF16),
        compiler_params=_params(("parallel", "parallel", "parallel")))(z)


def _undilate_dqkv(dqkv, dz, g, d):
    s = dz.shape[0]
    tm = ATTN_ROWS
    per = tm // d
    nh = HEADS_PER_GROUP

    def body(i_ref, dz_ref, o_ref, nat_ref):
        del dz_ref
        if d == 1:
            o_ref[...] = i_ref[0, 0]
        else:
            for r in range(d):
                nat_ref[pl.ds(r, per, stride=d), :] = i_ref[0, r].astype(F32)
            o_ref[...] = nat_ref[...].astype(BF16)

    return pl.pallas_call(
        body, name=f"undilate_dqkv_{g}", grid=(s // tm, 3, nh),
        in_specs=[pl.BlockSpec((1, d, per, HEAD_DIM), lambda i, w, h: (w, 0, i, h)),
                  pl.BlockSpec(memory_space=pl.ANY)],
        out_specs=pl.BlockSpec((tm, HEAD_DIM), lambda i, w, h: (i, (3 * w + g) * nh + h)),
        out_shape=jax.ShapeDtypeStruct(dz.shape, dz.dtype), input_output_aliases={1: 0},
        scratch_shapes=[pltpu.VMEM((tm, HEAD_DIM), F32)],
        compiler_params=_params(("parallel", "parallel", "parallel")))(dqkv, dz)


def _alibi_slope(head):
    return 2.0 ** (-8.0 * (head + 1) / N_ATTN_HEADS)


def _band_masks(n, nb):
    qi = lax.broadcasted_iota(jnp.int32, (ATTN_BLK, ATTN_BLK), 0)
    ki = lax.broadcasted_iota(jnp.int32, (ATTN_BLK, ATTN_BLK), 1)
    dist_cur = qi - ki
    dist_prev = ATTN_BLK + qi - ki
    return dist_cur.astype(F32), dist_cur >= 0, dist_prev.astype(F32), dist_prev <= ATTN_BLK


def _dot_nt(a, b):
    return lax.dot_general(a, b, _DOT_DIMS["nt"], preferred_element_type=F32)


def _dot_tn(a, b):
    return lax.dot_general(a, b, _DOT_DIMS["tn"], preferred_element_type=F32)


def _dot(a, b):
    return jnp.dot(a, b, preferred_element_type=F32)


def _scores(q, k, slope_d, dist, valid):
    s = _dot_nt(q, k) * (HEAD_DIM ** -0.5) - slope_d * dist
    return jnp.where(valid, s, NEG_BIG)


def _attn_fwd(qkv, g, d):
    _, _, L, _ = qkv.shape
    nb = L // ATTN_BLK

    def body(q_ref, kc_ref, kp_ref, vc_ref, vp_ref, o_ref, lse_ref):
        n = pl.program_id(1)
        dist_c, valid_c, dist_p, valid_p = _band_masks(n, nb)
        valid_p = jnp.logical_and(valid_p, n > 0)
        for hh in range(HEADS_PER_GROUP):
            cols = slice(hh * HEAD_DIM, (hh + 1) * HEAD_DIM)
            slope_d = _alibi_slope(g * HEADS_PER_GROUP + hh) * d
            q = q_ref[0, 0, :, cols]
            sc = _scores(q, kc_ref[0, 0, :, cols], slope_d, dist_c, valid_c)
            sp = _scores(q, kp_ref[0, 0, :, cols], slope_d, dist_p, valid_p)
            m = jnp.maximum(jnp.max(sc, axis=-1, keepdims=True), jnp.max(sp, axis=-1, keepdims=True))
            ec = jnp.exp(sc - m)
            ep = jnp.exp(sp - m)
            l = jnp.sum(ec, axis=-1, keepdims=True) + jnp.sum(ep, axis=-1, keepdims=True)
            inv = 1.0 / l
            o = _dot((ec * inv).astype(BF16), vc_ref[0, 0, :, cols]) + _dot((ep * inv).astype(BF16), vp_ref[0, 0, :, cols])
            o_ref[0, :, cols] = o
            lse_ref[0, :, cols] = jnp.broadcast_to(m + jnp.log(l), (ATTN_BLK, HEAD_DIM))

    def spec(w, shift):
        return pl.BlockSpec((1, 1, ATTN_BLK, GROUP_W), lambda r, n: (w, r, jnp.maximum(n + shift, 0), 0))

    out = pl.BlockSpec((1, ATTN_BLK, GROUP_W), lambda r, n: (r, n, 0))
    return pl.pallas_call(
        body, name=f"attn_fwd_{g}", grid=(d, nb),
        in_specs=[spec(0, 0), spec(1, 0), spec(1, -1), spec(2, 0), spec(2, -1)], out_specs=[out, out],
        out_shape=[jax.ShapeDtypeStruct((d, L, GROUP_W), F32)] * 2,
        compiler_params=_params(("parallel", "parallel")))(qkv, qkv, qkv, qkv, qkv)


def _attn_bwd(qkv, do, lse, cc, g, d):
    _, _, L, _ = qkv.shape
    nb = L // ATTN_BLK
    scale = HEAD_DIM ** -0.5

    def body(q0_ref, q1_ref, k0_ref, kp_ref, v0_ref, vp_ref, do0_ref, do1_ref, l0_ref, l1_ref, c0_ref, c1_ref, o_ref):
        n = pl.program_id(1)
        dist_c, valid_c, dist_p, valid_p = _band_masks(n, nb)
        valid_b = jnp.logical_and(valid_p, n > 0)
        valid_n = jnp.logical_and(valid_p, n < nb - 1)
        for hh in range(HEADS_PER_GROUP):
            cols = slice(hh * HEAD_DIM, (hh + 1) * HEAD_DIM)
            slope_d = _alibi_slope(g * HEADS_PER_GROUP + hh) * d
            q0, q1 = q0_ref[0, 0, :, cols], q1_ref[0, 0, :, cols]
            k0, kp = k0_ref[0, 0, :, cols], kp_ref[0, 0, :, cols]
            v0, vp = v0_ref[0, 0, :, cols], vp_ref[0, 0, :, cols]
            do0, do1 = do0_ref[0, :, cols], do1_ref[0, :, cols]
            l0, l1 = l0_ref[0, :, cols], l1_ref[0, :, cols]
            c0, c1 = c0_ref[0, :, cols], c1_ref[0, :, cols]
            pa = jnp.exp(_scores(q0, k0, slope_d, dist_c, valid_c) - l0)
            dsa = (pa * (_dot_nt(do0, v0) + c0)).astype(BF16)
            pb = jnp.exp(_scores(q0, kp, slope_d, dist_p, valid_b) - l0)
            dsb = (pb * (_dot_nt(do0, vp) + c0)).astype(BF16)
            pc = jnp.exp(_scores(q1, k0, slope_d, dist_p, valid_n) - l1)
            dsc = (pc * (_dot_nt(do1, v0) + c1)).astype(BF16)
            o_ref[0, 0, :, cols] = ((_dot(dsa, k0) + _dot(dsb, kp)) * scale).astype(BF16)
            o_ref[1, 0, :, cols] = ((_dot_tn(dsa, q0) + _dot_tn(dsc, q1)) * scale).astype(BF16)
            o_ref[2, 0, :, cols] = (_dot_tn(pa.astype(BF16), do0) + _dot_tn(pc.astype(BF16), do1)).astype(BF16)

    def spec(w, shift):
        return pl.BlockSpec((1, 1, ATTN_BLK, GROUP_W), lambda r, n: (w, r, jnp.clip(n + shift, 0, nb - 1), 0))

    def spec3(shift):
        return pl.BlockSpec((1, ATTN_BLK, GROUP_W), lambda r, n: (r, jnp.clip(n + shift, 0, nb - 1), 0))

    return pl.pallas_call(
        body, name=f"attn_bwd_{g}", grid=(d, nb),
        in_specs=[spec(0, 0), spec(0, 1), spec(1, 0), spec(1, -1), spec(2, 0), spec(2, -1),
                  spec3(0), spec3(1), spec3(0), spec3(1), spec3(0), spec3(1)],
        out_specs=pl.BlockSpec((3, 1, ATTN_BLK, GROUP_W), lambda r, n: (0, r, n, 0)),
        out_shape=jax.ShapeDtypeStruct((3, d, L, GROUP_W), BF16),
        compiler_params=_params(("parallel", "parallel")))(qkv, qkv, qkv, qkv, qkv, qkv, do, do, lse, lse, cc, cc)


def _load_natural(refs, nat_refs):
    for g, d in enumerate(ATTN_DILATIONS):
        if d == 1:
            nat_refs[g][...] = refs[g][0]
        else:
            per = ATTN_ROWS // d
            for r in range(d):
                nat_refs[g][pl.ds(r, per, stride=d), :] = refs[g][r]


def _mix_weights(lse_nat):
    l0, l1, l2 = lse_nat[0][...], lse_nat[1][...], lse_nat[2][...]
    m = jnp.maximum(jnp.maximum(l0, l1), l2)
    e0, e1, e2 = jnp.exp(l0 - m), jnp.exp(l1 - m), jnp.exp(l2 - m)
    inv = 1.0 / (e0 + e1 + e2)
    return e0 * inv, e1 * inv, e2 * inv


def _dilated_specs(s):
    return [pl.BlockSpec((d, ATTN_ROWS // d, HEAD_DIM), lambda i, h: (0, i, h)) for d in ATTN_DILATIONS]


NATURAL_SCRATCH = [pltpu.VMEM((ATTN_ROWS, HEAD_DIM), F32)] * (2 * len(ATTN_DILATIONS))


def _attn_merge(outs, lses):
    s = outs[0].shape[0] * outs[0].shape[1]

    def body(o0, o1, o2, l0, l1, l2, a_ref, *nat):
        onat, lnat = nat[:3], nat[3:]
        _load_natural((o0, o1, o2), onat)
        _load_natural((l0, l1, l2), lnat)
        w0, w1, w2 = _mix_weights(lnat)
        a_ref[...] = (w0 * onat[0][...] + w1 * onat[1][...] + w2 * onat[2][...]).astype(BF16)

    return pl.pallas_call(
        body, name="attn_merge", grid=(s // ATTN_ROWS, HEADS_PER_GROUP), in_specs=_dilated_specs(s) * 2,
        out_specs=pl.BlockSpec((ATTN_ROWS, HEAD_DIM), lambda i, h: (i, h)),
        out_shape=jax.ShapeDtypeStruct((s, GROUP_W), BF16), scratch_shapes=NATURAL_SCRATCH,
        compiler_params=_params(("parallel", "parallel")))(*outs, *lses)


def _attn_merge_bwd(outs, lses, dattn):
    s = dattn.shape[0]

    def body(o0, o1, o2, l0, l1, l2, da_ref, do0, do1, do2, c0, c1, c2, *nat):
        onat, lnat = nat[:3], nat[3:]
        _load_natural((o0, o1, o2), onat)
        _load_natural((l0, l1, l2), lnat)
        ws = _mix_weights(lnat)
        da = da_ref[...]
        attn = ws[0] * onat[0][...] + ws[1] * onat[1][...] + ws[2] * onat[2][...]
        tot = jnp.broadcast_to(jnp.sum(da * attn, axis=-1, keepdims=True), (ATTN_ROWS, HEAD_DIM))
        for g, (d, do_ref, c_ref) in enumerate(zip(ATTN_DILATIONS, (do0, do1, do2), (c0, c1, c2))):
            if d == 1:
                do_ref[0] = (ws[g] * da).astype(BF16)
                c_ref[0] = -ws[g] * tot
            else:
                onat[g][...] = ws[g] * da
                lnat[g][...] = -ws[g] * tot
                per = ATTN_ROWS // d
                for r in range(d):
                    do_ref[r] = onat[g][pl.ds(r, per, stride=d), :].astype(BF16)
                    c_ref[r] = lnat[g][pl.ds(r, per, stride=d), :]

    dil = _dilated_specs(s)
    shapes = [jax.ShapeDtypeStruct(o.shape, BF16) for o in outs] + [jax.ShapeDtypeStruct(o.shape, F32) for o in outs]
    return pl.pallas_call(
        body, name="attn_merge_bwd", grid=(s // ATTN_ROWS, HEADS_PER_GROUP),
        in_specs=dil * 2 + [pl.BlockSpec((ATTN_ROWS, HEAD_DIM), lambda i, h: (i, h))], out_specs=dil * 2,
        out_shape=shapes, scratch_shapes=NATURAL_SCRATCH,
        compiler_params=_params(("parallel", "parallel")))(*outs, *lses, dattn)


def _ssm_prepare(a_re, a_im, log_dt, b_re, b_im, c_re, c_im):
    n_g = a_re.shape[0]
    nj = n_g * SSM_GROUP // SSM_TILE_CH
    gpt = SSM_TILE_CH // SSM_GROUP
    dt = jnp.exp(log_dt)[:, None]
    mag = jnp.exp(a_re * dt)
    lr, li = mag * jnp.cos(a_im * dt), mag * jnp.sin(a_im * dt)
    den = a_re * a_re + a_im * a_im
    cr = ((lr - 1.0) * a_re + li * a_im) / den
    ci = (li * a_re - (lr - 1.0) * a_im) / den
    bb_re = cr[..., None] * b_re - ci[..., None] * b_im
    bb_im = cr[..., None] * b_im + ci[..., None] * b_re
    eye = jnp.eye(gpt, dtype=F32)

    def b_tiles(t):
        t = t.transpose(0, 2, 1).reshape(nj, gpt, SSM_GROUP, SSM_STATE)
        return jnp.einsum("jgcp,gh->jgchp", t, eye).reshape(nj, SSM_TILE_CH, SSM_TILE_ST)

    def c_tiles(t):
        t = t.reshape(nj, gpt, SSM_GROUP, SSM_STATE)
        return jnp.einsum("jgcp,gh->jhpgc", t, eye).reshape(nj, SSM_TILE_ST, SSM_TILE_CH)

    lam = jnp.stack([lr.reshape(-1), li.reshape(-1)])
    bmat = jnp.concatenate([b_tiles(bb_re), b_tiles(bb_im)], axis=2)
    cmat = jnp.concatenate([c_tiles(c_re), -c_tiles(c_im)], axis=1)
    return lam, bmat, cmat


SSM_SEGMENTS = 8


def _to_segment_order(nat, perm_ref):
    per = nat.shape[0] // SSM_SEGMENTS
    for i in range(SSM_SEGMENTS):
        perm_ref[pl.ds(i, per, stride=SSM_SEGMENTS), :] = nat[i * per:(i + 1) * per, :]
    return perm_ref[...]


def _to_time_order(val, perm_ref, store):
    per = val.shape[0] // SSM_SEGMENTS
    perm_ref[...] = val
    for i in range(SSM_SEGMENTS):
        store(i, perm_ref[pl.ds(i, per, stride=SSM_SEGMENTS), :])


def _fill_powers(lam_ref, w_ref, nj, tau_n):
    for j in range(nj):
        st = slice(j * SSM_TILE_ST, (j + 1) * SSM_TILE_ST)
        lr = jnp.broadcast_to(lam_ref[0:1, st], (SSM_SEGMENTS, SSM_TILE_ST))
        li = jnp.broadcast_to(lam_ref[1:2, st], (SSM_SEGMENTS, SSM_TILE_ST))
        wr, wi = lr, li
        for tau in range(tau_n):
            rows = slice(tau * SSM_SEGMENTS, (tau + 1) * SSM_SEGMENTS)
            w_ref[j, rows, :SSM_TILE_ST] = wr
            w_ref[j, rows, SSM_TILE_ST:] = wi
            wr, wi = wr * lr - wi * li, wr * li + wi * lr


def _segment_scan(src, xs_ref, w_tile, lr, li, cr, ci, conj, reverse):
    seg, half = SSM_SEGMENTS, SSM_TILE_ST
    tau_n = src.shape[0] // seg
    sgn = -1.0 if conj else 1.0
    lr8 = jnp.broadcast_to(lr, (seg, half))
    li8 = jnp.broadcast_to(li, (seg, half)) * sgn
    xr = jnp.zeros((seg, half), F32)
    xi = jnp.zeros((seg, half), F32)
    order = range(tau_n - 1, -1, -1) if reverse else range(tau_n)
    for tau in order:
        rows = slice(tau * seg, (tau + 1) * seg)
        xr, xi = lr8 * xr - li8 * xi + src[rows, :half], lr8 * xi + li8 * xr + src[rows, half:]
        xs_ref[rows, :half] = xr
        xs_ref[rows, half:] = xi
    pr = w_tile[(tau_n - 1) * seg:(tau_n - 1) * seg + 1, :half]
    pi = w_tile[(tau_n - 1) * seg:(tau_n - 1) * seg + 1, half:] * sgn
    fr, fi = cr, ci
    ins_r, ins_i = [None] * seg, [None] * seg
    runs = range(seg - 1, -1, -1) if reverse else range(seg)
    for i in runs:
        ins_r[i], ins_i[i] = fr, fi
        fr, fi = xr[i:i + 1, :] + pr * fr - pi * fi, xi[i:i + 1, :] + pr * fi + pi * fr
    in_r = jnp.concatenate(ins_r, axis=0)
    in_i = jnp.concatenate(ins_i, axis=0)
    for tau in range(tau_n):
        rows = slice(tau * seg, (tau + 1) * seg)
        wrow = (tau_n - 1 - tau) if reverse else tau
        wr = w_tile[wrow * seg:(wrow + 1) * seg, :half]
        wi = w_tile[wrow * seg:(wrow + 1) * seg, half:] * sgn
        xs_ref[rows, :half] += wr * in_r - wi * in_i
        xs_ref[rows, half:] += wr * in_i + wi * in_r
    return (fr, fi), (in_r, in_i)


def _ssm_dims(z, bmat, u_off):
    s = z.shape[0]
    nj = bmat.shape[0]
    t_rows = _pick(s, (256, 128))
    return s, nj, nj * SSM_TILE_CH, nj * SSM_TILE_ST, t_rows


def _ssm_fwd(z, bmat, cmat, lam, dskip, u_off, side=None):
    s, nj, w, ns, t_rows = _ssm_dims(z, bmat, u_off)
    per = t_rows // SSM_SEGMENTS

    def body(*refs):
        u_refs = refs[:nj]
        b_ref, c_ref, lam_ref, d_ref, y_ref, yg_ref, xin_ref, carry_ref, w_ref, xs_ref, perm_ref = refs[nj:]

        @pl.when(pl.program_id(0) == 0)
        def _():
            carry_ref[...] = jnp.zeros_like(carry_ref)
            _fill_powers(lam_ref, w_ref, nj, per)

        xin_ref[0] = carry_ref[...]
        for j in range(nj):
            st = slice(j * SSM_TILE_ST, (j + 1) * SSM_TILE_ST)
            ch = slice(j * SSM_TILE_CH, (j + 1) * SSM_TILE_CH)
            up = _to_segment_order(u_refs[j], perm_ref)
            bu = _dot(up.astype(BF16), b_ref[j])
            (fr, fi), _ = _segment_scan(bu, xs_ref, w_ref.at[j], lam_ref[0:1, st], lam_ref[1:2, st],
                                        carry_ref[0:1, st], carry_ref[1:2, st], conj=False, reverse=False)
            carry_ref[0:1, st] = fr
            carry_ref[1:2, st] = fi
            yp = _dot(xs_ref[...].astype(BF16), c_ref[j]) + d_ref[:, ch] * up

            def store(i, rows, ch=ch):
                y_ref[i * per:(i + 1) * per, ch] = rows
                yg_ref[i * per:(i + 1) * per, ch] = _gelu(rows).astype(BF16)

            _to_time_order(yp, perm_ref, store)

    u_specs = [pl.BlockSpec((t_rows, SSM_TILE_CH), lambda c, k=k: (c, u_off // SSM_TILE_CH + k)) for k in range(nj)]
    full3 = lambda shape: pl.BlockSpec(shape, lambda c: (0, 0, 0))
    full2 = lambda shape: pl.BlockSpec(shape, lambda c: (0, 0))
    rows = pl.BlockSpec((t_rows, w), lambda c: (c, 0))
    outs, carried = _call(
        body, name="ssm_fwd", grid=(s // t_rows,),
        in_specs=u_specs + [full3(bmat.shape), full3(cmat.shape), full2(lam.shape), full2(dskip.shape)],
        out_specs=[rows, rows, pl.BlockSpec((1, 2, ns), lambda c: (c, 0, 0))],
        out_shape=[jax.ShapeDtypeStruct((s, w), F32), jax.ShapeDtypeStruct((s, w), BF16),
                   jax.ShapeDtypeStruct((s // t_rows, 2, ns), F32)],
        scratch_shapes=[pltpu.VMEM((2, ns), F32), pltpu.VMEM((nj, t_rows, 2 * SSM_TILE_ST), F32),
                        pltpu.VMEM((t_rows, 2 * SSM_TILE_ST), F32), pltpu.VMEM((t_rows, SSM_TILE_CH), F32)],
        semantics=("arbitrary",), args=[*([z] * nj), bmat, cmat, lam, dskip], side=side)
    return outs if side is None else (outs, carried)


def _ssm_bwd(z, y, dyg, xin, bmat, cmat, lam, dskip, u_off, side=None):
    s, nj, w, ns, t_rows = _ssm_dims(z, bmat, u_off)
    nc = s // t_rows
    per = t_rows // SSM_SEGMENTS
    seg, half = SSM_SEGMENTS, SSM_TILE_ST

    def body(*refs):
        u_refs = refs[:nj]
        (y_ref, dyg_ref, xin_ref, b_ref, c_ref, lam_ref, d_ref, du_ref, db_ref, dc_ref, dlam_ref, dd_ref,
         carry_ref, w_ref, xs_ref, gs_ref, perm_ref, acc_ref) = refs[nj:]

        @pl.when(pl.program_id(0) == 0)
        def _():
            carry_ref[...] = jnp.zeros_like(carry_ref)
            db_ref[...] = jnp.zeros_like(db_ref)
            dc_ref[...] = jnp.zeros_like(dc_ref)
            dd_ref[...] = jnp.zeros_like(dd_ref)
            acc_ref[...] = jnp.zeros_like(acc_ref)
            _fill_powers(lam_ref, w_ref, nj, per)

        for j in range(nj):
            st = slice(j * SSM_TILE_ST, (j + 1) * SSM_TILE_ST)
            ch = slice(j * SSM_TILE_CH, (j + 1) * SSM_TILE_CH)
            lr, li = lam_ref[0:1, st], lam_ref[1:2, st]
            up = _to_segment_order(u_refs[j], perm_ref)
            upb = up.astype(BF16)
            dyp = _to_segment_order(dyg_ref[:, ch] * _gelu_grad(y_ref[:, ch]), perm_ref)
            dyb = dyp.astype(BF16)
            _, (in_r, in_i) = _segment_scan(_dot(upb, b_ref[j]), xs_ref, w_ref.at[j], lr, li,
                                            xin_ref[0, 0:1, st], xin_ref[0, 1:2, st], conj=False, reverse=False)
            (gr, gi), _ = _segment_scan(_dot_nt(dyb, c_ref[j]), gs_ref, w_ref.at[j], lr, li,
                                        carry_ref[0:1, st], carry_ref[1:2, st], conj=True, reverse=True)
            carry_ref[0:1, st] = gr
            carry_ref[1:2, st] = gi
            xs, gs = xs_ref[...], gs_ref[...]
            xsr, xsi, gsr, gsi = xs[:, :half], xs[:, half:], gs[:, :half], gs[:, half:]
            pxr = jnp.concatenate([in_r, xsr[:t_rows - seg]], axis=0)
            pxi = jnp.concatenate([in_i, xsi[:t_rows - seg]], axis=0)
            dl_r = gsr * pxr + gsi * pxi
            dl_i = gsi * pxr - gsr * pxi
            acc_ref[0, :, st] += jnp.sum(dl_r.reshape(per, seg, half), axis=0)
            acc_ref[1, :, st] += jnp.sum(dl_i.reshape(per, seg, half), axis=0)
            gx = gs.astype(BF16)
            dup = _dot_nt(gx, b_ref[j]) + d_ref[:, ch] * dyp

            def store(i, rows, ch=ch):
                du_ref[i * per:(i + 1) * per, ch] = rows.astype(BF16)

            _to_time_order(dup, perm_ref, store)
            db_ref[j] += _dot_tn(upb, gx)
            dc_ref[j] += _dot_tn(xs.astype(BF16), dyb)
            dd_ref[:, ch] += jnp.sum(dyp * up, axis=0, keepdims=True)

        @pl.when(pl.program_id(0) == nc - 1)
        def _():
            dlam_ref[...] = jnp.sum(acc_ref[...], axis=1)

    rev = lambda c: nc - 1 - c
    u_specs = [pl.BlockSpec((t_rows, SSM_TILE_CH), lambda c, k=k: (rev(c), u_off // SSM_TILE_CH + k))
               for k in range(nj)]
    full3 = lambda shape: pl.BlockSpec(shape, lambda c: (0, 0, 0))
    full2 = lambda shape: pl.BlockSpec(shape, lambda c: (0, 0))
    rows = pl.BlockSpec((t_rows, w), lambda c: (rev(c), 0))
    outs, carried = _call(
        body, name="ssm_bwd", grid=(nc,),
        in_specs=u_specs + [rows, rows, pl.BlockSpec((1, 2, ns), lambda c: (rev(c), 0, 0)),
                            full3(bmat.shape), full3(cmat.shape), full2(lam.shape), full2(dskip.shape)],
        out_specs=[rows, full3(bmat.shape), full3(cmat.shape), full2(lam.shape), full2(dskip.shape)],
        out_shape=[jax.ShapeDtypeStruct((s, w), BF16), jax.ShapeDtypeStruct(bmat.shape, F32),
                   jax.ShapeDtypeStruct(cmat.shape, F32), jax.ShapeDtypeStruct(lam.shape, F32),
                   jax.ShapeDtypeStruct(dskip.shape, F32)],
        scratch_shapes=[pltpu.VMEM((2, ns), F32), pltpu.VMEM((nj, t_rows, 2 * SSM_TILE_ST), F32),
                        pltpu.VMEM((t_rows, 2 * SSM_TILE_ST), F32), pltpu.VMEM((t_rows, 2 * SSM_TILE_ST), F32),
                        pltpu.VMEM((t_rows, SSM_TILE_CH), F32), pltpu.VMEM((2, SSM_SEGMENTS, ns), F32)],
        semantics=("arbitrary",), args=[*([z] * nj), y, dyg, xin, bmat, cmat, lam, dskip], side=side)
    return outs if side is None else (outs, carried)


def _adam_math(w, g, m, v):
    m = ADAM_B1 * m + (1.0 - ADAM_B1) * g
    v = ADAM_B2 * v + (1.0 - ADAM_B2) * (g * g)
    m_hat = m / (1.0 - ADAM_B1 ** ADAM_STEP)
    v_hat = v / (1.0 - ADAM_B2 ** ADAM_STEP)
    delta = -ADAM_LR * (m_hat / (jnp.sqrt(v_hat) + ADAM_EPS) + ADAM_WD * w)
    return delta, m, v


def _adam_rows(r, c):
    for tr in (512, 256, 128, 64, 32, 16, 8):
        if r % tr == 0 and tr * c * 4 <= (1 << 20):
            return tr
    return r


def _adamw_big(w, p_mine, p_sib, m, v, name):
    r, c = w.shape
    tr = _adam_rows(r, c)

    def body(w_ref, a_ref, b_ref, m_ref, v_ref, g_ref, d_ref, nm_ref, nv_ref):
        g = a_ref[...] + b_ref[...]
        g_ref[...] = g
        d_ref[...], nm_ref[...], nv_ref[...] = _adam_math(w_ref[...], g, m_ref[...], v_ref[...])

    blk = pl.BlockSpec((tr, c), lambda i: (i, 0))
    return pl.pallas_call(body, name=f"adamw_{name}", grid=(r // tr,), in_specs=[blk] * 5, out_specs=[blk] * 4,
                          out_shape=[jax.ShapeDtypeStruct((r, c), F32)] * 4,
                          compiler_params=_params(("parallel",)))(w, p_mine, p_sib, m, v)


def _adamw_small(w, parts, m, v):
    r, c = w.shape
    n_dev = parts.shape[0]

    def body(w_ref, p_ref, m_ref, v_ref, g_ref, d_ref, nm_ref, nv_ref):
        g = p_ref[0]
        for k in range(1, n_dev):
            g = g + p_ref[k]
        g_ref[...] = g
        d_ref[...], nm_ref[...], nv_ref[...] = _adam_math(w_ref[...], g, m_ref[...], v_ref[...])

    blk = pl.BlockSpec((r, c), lambda i: (0, 0))
    return pl.pallas_call(body, name="adamw_small", grid=(1,),
                          in_specs=[blk, pl.BlockSpec((n_dev, r, c), lambda i: (0, 0, 0)), blk, blk],
                          out_specs=[blk] * 4, out_shape=[jax.ShapeDtypeStruct((r, c), F32)] * 4,
                          compiler_params=_params(("arbitrary",)))(w, parts, m, v)


def _cast_bf16(w, name):
    r, c = w.shape
    tr = _adam_rows(r, c)

    def body(w_ref, o_ref):
        o_ref[...] = w_ref[...].astype(BF16)

    blk = pl.BlockSpec((tr, c), lambda i: (i, 0))
    return pl.pallas_call(body, name=f"cast_{name}", grid=(r // tr,), in_specs=[blk], out_specs=blk,
                          out_shape=jax.ShapeDtypeStruct((r, c), BF16), compiler_params=_params(("parallel",)))(w)


def _sum_slots(recv, name):
    _, r, c = recv.shape
    tr = _adam_rows(r, c)

    def body(p_ref, o_ref):
        acc = p_ref[0].astype(F32)
        for k in range(1, N_CHIPS):
            acc = acc + p_ref[k].astype(F32)
        o_ref[...] = acc

    return pl.pallas_call(body, name=f"sum_{name}", grid=(r // tr,),
                          in_specs=[pl.BlockSpec((N_CHIPS, tr, c), lambda i: (0, i, 0))],
                          out_specs=pl.BlockSpec((tr, c), lambda i: (i, 0)),
                          out_shape=jax.ShapeDtypeStruct((r, c), F32), compiler_params=_params(("parallel",)))(recv)


BIG_WEIGHTS = ("w_in", "w_attn_up", "w_glu_v", "w_glu_g", "w_out", "w_ffn_gate", "w_ffn_up", "w_ffn_down")
COL_SHARDED = ("w_in", "w_attn_up", "w_glu_v", "w_glu_g", "w_ffn_gate", "w_ffn_up")


def _aligned(v, m):
    return v if isinstance(v, int) else pl.multiple_of(v, m)


def _shard_of(ref, name, j, shard_shape, half=None):
    r, c = shard_shape
    rows = r if half is None else r // 2
    row0 = 0 if half is None else half * rows
    if name in COL_SHARDED:
        return ref.at[pl.ds(_aligned(row0, 16), rows), pl.ds(_aligned(j * c, 128), c)]
    return ref.at[pl.ds(_aligned(j * r + row0, 16), rows), :]


def _other_chips():
    x, y = lax.axis_index("x"), lax.axis_index("y")
    return [(1 - x, y), (x, 1 - y), (1 - x, 1 - y)]


def _dma_sems(n, arrays):
    return [pltpu.SemaphoreType.DMA((n, 3))] * arrays + [pltpu.SemaphoreType.DMA((n,))]


def _gather_side(shards):
    names = list(shards)
    n = len(names)
    full_shapes = []
    for k in names:
        r, c = shards[k].shape
        full_shapes.append((r, c * N_CHIPS) if k in COL_SHARDED else (r * N_CHIPS, c))

    def build(src, dst, sems):
        send_sems, recv_sems, pass_send_sems, pass_recv_sems, local_sems = sems
        x, y, c = lax.axis_index("x"), lax.axis_index("y"), lax.axis_index("c")
        me = 2 * x + y
        locals_, sends, arrivals, forwards, passed_on = [], [], [], [], []
        for i, k in enumerate(names):
            shape = shards[k].shape
            half_rows = shape[0] // 2
            locals_.append(pltpu.make_async_copy(src[i], _shard_of(dst[i], k, me, shape), local_sems.at[i]))
            my_half = src[i].at[pl.ds(_aligned(c * half_rows, 16), half_rows), :]
            for p, (px, py) in enumerate(_other_chips()):
                peer = 2 * px + py
                landed = _shard_of(dst[i], k, peer, shape, half=c)
                sends.append(pltpu.make_async_remote_copy(
                    src_ref=my_half, dst_ref=_shard_of(dst[i], k, me, shape, half=c), send_sem=send_sems.at[i, p],
                    recv_sem=recv_sems.at[i, p], device_id=(px, py, c), device_id_type=MESH))
                arrivals.append(pltpu.make_async_remote_copy(
                    src_ref=my_half, dst_ref=landed, send_sem=send_sems.at[i, p],
                    recv_sem=recv_sems.at[i, p], device_id=(px, py, c), device_id_type=MESH))
                forwards.append(pltpu.make_async_remote_copy(
                    src_ref=landed, dst_ref=landed, send_sem=pass_send_sems.at[i, p],
                    recv_sem=pass_recv_sems.at[i, p], device_id=(x, y, 1 - c), device_id_type=MESH))
                passed_on.append(pltpu.make_async_remote_copy(
                    src_ref=landed, dst_ref=_shard_of(dst[i], k, peer, shape, half=1 - c),
                    send_sem=pass_send_sems.at[i, p], recv_sem=pass_recv_sems.at[i, p],
                    device_id=(x, y, 1 - c), device_id_type=MESH))
        return locals_, sends, arrivals, forwards, passed_on

    return _Side([shards[k] for k in names], [jax.ShapeDtypeStruct(s, BF16) for s in full_shapes], _dma_sems(n, 4), build)


def _scatter_side(grads, shard_shapes):
    names = list(grads)
    n = len(names)

    def build(src, dst, sems):
        send_sems, recv_sems, local_sems = sems
        x, y, c = lax.axis_index("x"), lax.axis_index("y"), lax.axis_index("c")
        me = 2 * x + y
        locals_, sends, arrivals = [], [], []
        for i, k in enumerate(names):
            shape = shard_shapes[k]
            locals_.append(pltpu.make_async_copy(_shard_of(src[i], k, me, shape), dst[i].at[me], local_sems.at[i]))
            for p, (px, py) in enumerate(_other_chips()):
                peer = 2 * px + py
                sends.append(pltpu.make_async_remote_copy(
                    src_ref=_shard_of(src[i], k, peer, shape), dst_ref=dst[i].at[me], send_sem=send_sems.at[i, p],
                    recv_sem=recv_sems.at[i, p], device_id=(px, py, c), device_id_type=MESH))
                arrivals.append(pltpu.make_async_remote_copy(
                    src_ref=_shard_of(src[i], k, peer, shape), dst_ref=dst[i].at[peer], send_sem=send_sems.at[i, p],
                    recv_sem=recv_sems.at[i, p], device_id=(px, py, c), device_id_type=MESH))
        return locals_, sends, arrivals, [], []

    return _Side([grads[k] for k in names],
                 [jax.ShapeDtypeStruct((N_CHIPS,) + tuple(shard_shapes[k]), BF16) for k in names], _dma_sems(n, 2), build)


def _swap_with_sibling(parts):
    names = list(parts)
    n = len(names)

    def body(*refs):
        src, dst = refs[:n], refs[n:2 * n]
        send_sems, recv_sems = refs[2 * n:]
        sibling = (lax.axis_index("x"), lax.axis_index("y"), 1 - lax.axis_index("c"))
        copies = []
        for i in range(n):
            cp = pltpu.make_async_remote_copy(src_ref=src[i], dst_ref=dst[i], send_sem=send_sems.at[i],
                                              recv_sem=recv_sems.at[i], device_id=sibling, device_id_type=MESH)
            cp.start()
            copies.append(cp)
        for cp in copies:
            cp.wait_recv()
        for cp in copies:
            cp.wait_send()

    outs = pl.pallas_call(
        body, name="swap_with_sibling", in_specs=[HBM] * n, out_specs=[HBM] * n,
        out_shape=[jax.ShapeDtypeStruct(parts[k].shape, F32) for k in names],
        scratch_shapes=[pltpu.SemaphoreType.DMA((n,)), pltpu.SemaphoreType.DMA((n,))],
        compiler_params=pltpu.CompilerParams(has_side_effects=True),
    )(*[parts[k] for k in names])
    return dict(zip(names, outs))


def _share_small(packed):
    r, c = packed.shape
    n_dev = 8

    def body(src, dst, send_sems, recv_sems, local_sem):
        x, y, cc = lax.axis_index("x"), lax.axis_index("y"), lax.axis_index("c")
        me = 4 * x + 2 * y + cc
        own = pltpu.make_async_copy(src, dst.at[me], local_sem)
        own.start()
        sends, arrivals = [], []
        p = 0
        for fx in range(2):
            for fy in range(2):
                for fc in range(2):
                    if fx == fy == fc == 0:
                        continue
                    px, py, pc = x ^ fx, y ^ fy, cc ^ fc
                    out = pltpu.make_async_remote_copy(src_ref=src, dst_ref=dst.at[me], send_sem=send_sems.at[p],
                                                       recv_sem=recv_sems.at[p], device_id=(px, py, pc), device_id_type=MESH)
                    out.start()
                    sends.append(out)
                    arrivals.append(pltpu.make_async_remote_copy(
                        src_ref=src, dst_ref=dst.at[4 * px + 2 * py + pc], send_sem=send_sems.at[p],
                        recv_sem=recv_sems.at[p], device_id=(px, py, pc), device_id_type=MESH))
                    p += 1
        for a in arrivals:
            a.wait_recv()
        for cp in sends:
            cp.wait_send()
        own.wait()

    return pl.pallas_call(
        body, name="share_small", in_specs=[HBM], out_specs=HBM,
        out_shape=jax.ShapeDtypeStruct((n_dev, r, c), F32),
        scratch_shapes=[pltpu.SemaphoreType.DMA((7,)), pltpu.SemaphoreType.DMA((7,)), pltpu.SemaphoreType.DMA],
        compiler_params=pltpu.CompilerParams(has_side_effects=True),
    )(packed)


SMALL_WEIGHTS = ("norm_mix_pre", "ssm_a_re", "ssm_a_im", "ssm_log_dt", "ssm_b_re", "ssm_b_im", "ssm_c_re", "ssm_c_im",
                 "ssm_d", "norm_mix_post", "norm_ffn_pre", "norm_ffn_post")
WEIGHT_ORDER = ("norm_mix_pre", "w_in", "w_attn_up", "ssm_a_re", "ssm_a_im", "ssm_log_dt", "ssm_b_re", "ssm_b_im",
                "ssm_c_re", "ssm_c_im", "ssm_d", "w_glu_v", "w_glu_g", "w_out", "norm_mix_post", "norm_ffn_pre",
                "w_ffn_gate", "w_ffn_up", "w_ffn_down", "norm_ffn_post")
PACK_LANES = 128
PACK_ROWS = 8


def _pack_small(arrs):
    flat = jnp.concatenate([arrs[k].reshape(-1) for k in SMALL_WEIGHTS])
    pad = -flat.shape[0] % (PACK_LANES * PACK_ROWS)
    return jnp.pad(flat, (0, pad)).reshape(-1, PACK_LANES)


def _unpack_small(packed, like):
    flat = packed.reshape(-1)
    out, pos = {}, 0
    for k in SMALL_WEIGHTS:
        n = like[k].size
        out[k] = flat[pos:pos + n].reshape(like[k].shape)
        pos += n
    return out


def _local_step(x, target, big, small, shards=None, shard_shapes=None):
    s, d = x.shape
    big, grads, slots = dict(big), {}, {}
    carry = shards is not None

    def gathering(names, call):
        if not carry:
            return call(None)
        res, got = call(_gather_side({k: shards[k] for k in names}))
        big.update(zip(names, got))
        return res

    def scattering(names, call):
        if not carry:
            return call(None)
        res, got = call(_scatter_side({k: grads[k] for k in names}, shard_shapes))
        slots.update(zip(names, got))
        return res

    u_off = 3 * HQ
    gate_off = u_off + d // 2
    g1, g2, g3, g4 = (small[k][0:1] for k in ("norm_mix_pre", "norm_mix_post", "norm_ffn_pre", "norm_ffn_post"))
    ssm_names = ("ssm_a_re", "ssm_a_im", "ssm_log_dt", "ssm_b_re", "ssm_b_im", "ssm_c_re", "ssm_c_im")
    (lam, bmat, cmat), ssm_vjp = jax.vjp(_ssm_prepare, *[small[k][0] for k in ssm_names])
    bmat, cmat = bmat.astype(BF16), cmat.astype(BF16)
    dskip = small["ssm_d"][0:1]

    h1 = _norm_in(x, g1)
    z = gathering(("w_attn_up", "w_glu_v", "w_glu_g", "w_out", "w_ffn_gate"),
                  lambda side: _mm(h1, big["w_in"], "nn", F32, "in_proj", side=side))
    y, yg, xin = gathering(("w_ffn_up",), lambda side: _ssm_fwd(z, bmat, cmat, lam, dskip, u_off, side=side))
    qkv = [_dilate_qkv(z, g, dil) for g, dil in enumerate(ATTN_DILATIONS)]
    outs, lses = zip(*[_attn_fwd(qkv[g], g, dil) for g, dil in enumerate(ATTN_DILATIONS)])
    attn = _attn_merge(outs, lses)
    merged, ab, gv, gg = _mm_fused(
        [attn, yg], [big["w_attn_up"], big["w_glu_v"], big["w_glu_g"]], [(0, 0), (1, 1), (1, 2)], "nn",
        [BF16, F32, F32, F32], "branches_merge", extras=[(z, gate_off), (z, gate_off + d)], epilogue=_gates_epilogue)
    mo = _mm(merged, big["w_out"], "nn", F32, "mix_out")
    x2, h2 = _norm_mid(x, mo, g2, g3)
    act, fg, fu = gathering(("w_ffn_down",), lambda side: _mm_fused(
        [h2], [big["w_ffn_gate"], big["w_ffn_up"]], [(0, 0), (0, 1)], "nn", [BF16, BF16, BF16], "ffn_up_act",
        epilogue=_swiglu_epilogue, side=side))
    f = _mm(act, big["w_ffn_down"], "nn", F32, "ffn_down")
    loss, dout, df, dg4 = _loss_head(x2, f, g4, target)

    dfg, dfu = _mm_fused([df], [big["w_ffn_down"]], [(0, 0)], "nt", [BF16, BF16], "d_ffn_act",
                         extras=[(fg, 0), (fu, 0)], epilogue=_swiglu_bwd_epilogue)
    grads["w_ffn_down"] = _mm_kloop(act, df, "tn", BF16, "dw_ffn_down")
    grads["w_ffn_gate"] = scattering(("w_ffn_down",), lambda side: _mm_kloop(h2, dfg, "tn", BF16, "dw_ffn_gate", side=side))
    grads["w_ffn_up"] = scattering(("w_ffn_gate",), lambda side: _mm_kloop(h2, dfu, "tn", BF16, "dw_ffn_up", side=side))
    dh2 = scattering(("w_ffn_up",), lambda side: _mm_kloop(dfg, big["w_ffn_gate"], "nt", F32, "d_h2_gate", side=side))
    dh2 = _mm_kloop(dfu, big["w_ffn_up"], "nt", F32, "d_h2_up", add=dh2)
    dx2, dmo, dg2, dg3 = _norm_mid_bwd(x2, mo, g2, g3, dout, dh2)
    dmerged = _mm(dmo, big["w_out"], "nt", F32, "d_merged")
    grads["w_out"] = _mm_kloop(merged, dmo, "tn", BF16, "dw_out")
    dz, dab = _merge_gate_a_bwd(z, ab, dmerged, gate_off)
    dz, dgv, dgg = _merge_gate_s_bwd(z, gv, gg, dmerged, dz, gate_off)
    dyg = _mm_fused([dgv, dgg], [big["w_glu_v"], big["w_glu_g"]], [(0, 0), (1, 1)], "nt", [F32], "d_yg",
                    epilogue=_sum_epilogue)[0]
    grads["w_glu_v"] = _mm_kloop(yg, dgv, "tn", BF16, "dw_glu_v")
    grads["w_glu_g"] = _mm_kloop(yg, dgg, "tn", BF16, "dw_glu_g")
    du, dbmat, dcmat, dlam, dd = scattering(
        ("w_out", "w_glu_v", "w_glu_g"),
        lambda side: _ssm_bwd(z, y, dyg, xin, bmat, cmat, lam, dskip, u_off, side=side))
    dz = _put_cols(dz, du, u_off)
    dattn = _mm(dab, big["w_attn_up"], "nt", F32, "d_attn")
    grads["w_attn_up"] = _mm_kloop(attn, dab, "tn", BF16, "dw_attn_up")
    merged_bwd = _attn_merge_bwd(outs, lses, dattn)
    for g, dil in enumerate(ATTN_DILATIONS):
        dqkv = _attn_bwd(qkv[g], merged_bwd[g], lses[g], merged_bwd[3 + g], g, dil)
        dz = _undilate_dqkv(dqkv, dz, g, dil)
    grads["w_in"] = scattering(("w_attn_up",), lambda side: _mm_kloop(h1, dz, "tn", BF16, "dw_in", side=side))
    dh1 = scattering(("w_in",), lambda side: _mm_kloop(dz, big["w_in"], "nt", F32, "d_h1", side=side))
    grad_x, dg1 = _norm_in_bwd(x, g1, dh1, dx2)

    small_grads = dict(zip(ssm_names, (t[None] for t in ssm_vjp((dlam, dbmat, dcmat)))))
    small_grads.update(norm_mix_pre=dg1, norm_mix_post=dg2, norm_ffn_pre=dg3, norm_ffn_post=dg4, ssm_d=dd)
    return loss[0, 0], grad_x, slots if carry else grads, small_grads


def kernel(x, norm_mix_pre, w_in, w_attn_up, ssm_a_re, ssm_a_im, ssm_log_dt, ssm_b_re, ssm_b_im, ssm_c_re, ssm_c_im, ssm_d, w_glu_v, w_glu_g, w_out, norm_mix_post, norm_ffn_pre, w_ffn_gate, w_ffn_up, w_ffn_down, norm_ffn_post, loss_target, m_norm_mix_pre, m_w_in, m_w_attn_up, m_ssm_a_re, m_ssm_a_im, m_ssm_log_dt, m_ssm_b_re, m_ssm_b_im, m_ssm_c_re, m_ssm_c_im, m_ssm_d, m_w_glu_v, m_w_glu_g, m_w_out, m_norm_mix_post, m_norm_ffn_pre, m_w_ffn_gate, m_w_ffn_up, m_w_ffn_down, m_norm_ffn_post, v_norm_mix_pre, v_w_in, v_w_attn_up, v_ssm_a_re, v_ssm_a_im, v_ssm_log_dt, v_ssm_b_re, v_ssm_b_im, v_ssm_c_re, v_ssm_c_im, v_ssm_d, v_w_glu_v, v_w_glu_g, v_w_out, v_norm_mix_post, v_norm_ffn_pre, v_w_ffn_gate, v_w_ffn_up, v_w_ffn_down, v_norm_ffn_post):
    given = dict(locals())
    w = {k: given[k] for k in WEIGHT_ORDER}
    m = {k: given["m_" + k] for k in WEIGHT_ORDER}
    v = {k: given["v_" + k] for k in WEIGHT_ORDER}

    shards = {k: _cast_bf16(w[k][0], k) for k in BIG_WEIGHTS}
    shard_shapes = {k: w[k].shape[1:] for k in BIG_WEIGHTS}
    big = {"w_in": _run_side(_gather_side({"w_in": shards["w_in"]}), "gather_w_in")[0]}

    loss, grad_x, slots, small_grads = _local_step(x[0], loss_target[0], big, {k: w[k] for k in SMALL_WEIGHTS},
                                                   shards, shard_shapes)
    loss = lax.psum(loss, MESH_AXES)

    mine = {k: _sum_slots(slots[k], k) for k in BIG_WEIGHTS}
    theirs = _swap_with_sibling(mine)
    out_g, out_d, out_m, out_v = {}, {}, {}, {}
    for k in BIG_WEIGHTS:
        res = _adamw_big(w[k][0], mine[k], theirs[k], m[k][0], v[k][0], k)
        out_g[k], out_d[k], out_m[k], out_v[k] = (t[None] for t in res)

    pick = lambda tree: {k: tree[k] for k in SMALL_WEIGHTS}
    parts = _share_small(_pack_small(small_grads))
    res = _adamw_small(_pack_small(pick(w)), parts, _pack_small(pick(m)), _pack_small(pick(v)))
    for dst, packed in zip((out_g, out_d, out_m, out_v), res):
        dst.update(_unpack_small(packed, pick(w)))

    return (loss, grad_x[None], *[out_g[k] for k in WEIGHT_ORDER], *[out_d[k] for k in WEIGHT_ORDER],
            *[out_m[k] for k in WEIGHT_ORDER], *[out_v[k] for k in WEIGHT_ORDER])
```

```python
import functools
import math

import jax
import jax.numpy as jnp
from jax import lax
from jax.experimental import pallas as pl
from jax.experimental.pallas import tpu as pltpu

F32 = jnp.float32
BF16 = jnp.bfloat16

EPS = 1e-6
HEAD_DIM = 128
HEADS_PER_GROUP = 4
ATTN_DILATIONS = (1, 4, 16)
ATTN_BLK = 128
N_ATTN_HEADS = HEADS_PER_GROUP * len(ATTN_DILATIONS)
GROUP_W = HEADS_PER_GROUP * HEAD_DIM
HQ = N_ATTN_HEADS * HEAD_DIM
SSM_GROUP = 16
SSM_STATE = 64
SSM_TILE_CH = 128
SSM_TILE_ST = SSM_TILE_CH // SSM_GROUP * SSM_STATE
ADAM_LR = 0.001
ADAM_B1 = 0.9
ADAM_B2 = 0.999
ADAM_EPS = 1e-08
ADAM_WD = 0.01
ADAM_STEP = 10
NEG_BIG = -1e30
V7X_VMEM_LIMIT = 56 * 1024 * 1024
MESH_AXES = ("x", "y", "c")
N_CHIPS = 4


def _pick(n, cands):
    for c in cands:
        if n % c == 0:
            return c
    raise ValueError(f"no tile of {cands} divides {n}")


def _params(sem):
    return pltpu.CompilerParams(dimension_semantics=sem, vmem_limit_bytes=V7X_VMEM_LIMIT)


HBM = pl.BlockSpec(memory_space=pl.ANY)
MESH = pl.DeviceIdType.MESH


class _Side:
    def __init__(self, srcs, out_shapes, sem_shapes, build):
        self.srcs, self.out_shapes, self.sem_shapes, self.build = list(srcs), list(out_shapes), list(sem_shapes), build

    def start(self, src, dst, sems):
        local, sends = self.build(src, dst, sems)[:2]
        for cp in local + sends:
            cp.start()

    def wait(self, src, dst, sems):
        local, sends, arrivals, forwards, passed_on = self.build(src, dst, sems)
        for cp, forward in zip(arrivals, forwards):
            cp.wait_recv()
            if forward is not None:
                forward.start()
        for cp in passed_on:
            cp.wait_recv()
        for cp in sends + [f for f in forwards if f is not None]:
            cp.wait_send()
        for cp in local:
            cp.wait()


def _join_sides(a, b):
    ns, no, nm = len(a.srcs), len(a.out_shapes), len(a.sem_shapes)

    def build(src, dst, sems):
        ra, rb = a.build(src[:ns], dst[:no], sems[:nm]), b.build(src[ns:], dst[no:], sems[nm:])
        return tuple(p + q for p, q in zip(ra, rb))

    return _Side(a.srcs + b.srcs, a.out_shapes + b.out_shapes, a.sem_shapes + b.sem_shapes, build)


def _call(body, *, name, grid, in_specs, out_specs, out_shape, semantics, args, scratch_shapes=(), side=None, **kw):
    in_specs, out_specs, out_shape, scratch_shapes = list(in_specs), list(out_specs), list(out_shape), list(scratch_shapes)
    if side is None:
        res = pl.pallas_call(body, name=name, grid=grid, in_specs=in_specs, out_specs=out_specs, out_shape=out_shape,
                             scratch_shapes=scratch_shapes, compiler_params=_params(semantics), **kw)(*args)
        return list(res), []
    n_in, n_out, n_scr = len(in_specs), len(out_specs), len(scratch_shapes)
    ns_in, ns_out = len(side.srcs), len(side.out_shapes)

    def carrying(*refs):
        ins, s_in = refs[:n_in], refs[n_in:n_in + ns_in]
        o0 = n_in + ns_in
        outs, s_out = refs[o0:o0 + n_out], refs[o0 + n_out:o0 + n_out + ns_out]
        c0 = o0 + n_out + ns_out
        scr, sems = refs[c0:c0 + n_scr], refs[c0 + n_scr:]
        ids = [pl.program_id(a) for a in range(len(grid))]
        first = functools.reduce(jnp.logical_and, [i == 0 for i in ids])
        last = functools.reduce(jnp.logical_and, [i == g - 1 for i, g in zip(ids, grid)])

        @pl.when(first)
        def _():
            side.start(s_in, s_out, sems)

        body(*ins, *outs, *scr)

        @pl.when(last)
        def _():
            side.wait(s_in, s_out, sems)

    res = pl.pallas_call(
        carrying, name=name, grid=grid, in_specs=in_specs + [HBM] * ns_in, out_specs=out_specs + [HBM] * ns_out,
        out_shape=out_shape + side.out_shapes, scratch_shapes=scratch_shapes + side.sem_shapes,
        compiler_params=pltpu.CompilerParams(dimension_semantics=("arbitrary",) * len(grid),
                                             vmem_limit_bytes=V7X_VMEM_LIMIT, has_side_effects=True), **kw,
    )(*args, *side.srcs)
    return list(res[:n_out]), list(res[n_out:])


def _run_side(side, name):
    ns, no = len(side.srcs), len(side.out_shapes)

    def body(*refs):
        src, dst, sems = refs[:ns], refs[ns:ns + no], refs[ns + no:]
        side.start(src, dst, sems)
        side.wait(src, dst, sems)

    return list(pl.pallas_call(body, name=name, in_specs=[HBM] * ns, out_specs=[HBM] * no, out_shape=side.out_shapes,
                               scratch_shapes=side.sem_shapes,
                               compiler_params=pltpu.CompilerParams(has_side_effects=True))(*side.srcs))


_DOT_DIMS = {"nn": (((1,), (0,)), ((), ())), "nt": (((1,), (1,)), ((), ())), "tn": (((0,), (0,)), ((), ()))}


MM_VMEM_BUDGET = 44 * 1024 * 1024
MM_STEP_BYTES = 1 << 20


def _size(dtype):
    return jnp.dtype(dtype).itemsize


def _mm_fused(as_, bs, pairs, mode, out_dtypes, name, extras=(), epilogue=None, side=None, out_place=None):
    M = as_[0].shape[0]
    N = bs[0].shape[1] if mode == "nn" else bs[0].shape[0]
    ks_a = [a.shape[1] for a in as_]
    ks_b = [b.shape[0] if mode == "nn" else b.shape[1] for b in bs]
    if epilogue is None:
        epilogue = lambda rs, es: rs
    offs = [off for _, off in extras]
    place = list(out_place) if out_place else [None] * len(out_dtypes)
    offs_all = offs + [p[1] for p in place if p is not None]
    best = None
    for tm in (2048, 1024, 512, 256, 128):
        for tn in (2048, 1024, 512, 256, 128):
            if M % tm or N % tn or any(off % tn for off in offs_all):
                continue
            vmem = (sum(2 * tm * k * 2 for k in ks_a) + sum(2 * k * tn * 2 for k in ks_b)
                    + sum(2 * tm * tn * _size(d) for d in out_dtypes) + sum(2 * tm * tn * _size(e.dtype) for e, _ in extras)
                    + len(pairs) * tm * tn * 4)
            cost = sum(k * N * 2 for k in ks_b) * (M // tm) + (M // tm) * (N // tn) * MM_STEP_BYTES
            if vmem <= MM_VMEM_BUDGET and (best is None or cost < best[0]):
                best = (cost, tm, tn)
    _, tm, tn = best
    na, nb, ne, no = len(as_), len(bs), len(extras), len(out_dtypes)
    dims = _DOT_DIMS[mode]

    def body(*refs):
        a_refs, b_refs = refs[:na], refs[na:na + nb]
        e_refs, o_refs = refs[na + nb:na + nb + ne], refs[na + nb + ne:]
        rs = [lax.dot_general(a_refs[ai][...], b_refs[bi][...], dims, preferred_element_type=F32) for ai, bi in pairs]
        outs = epilogue(rs, [e[...] for e in e_refs])
        for o_ref, o in zip(o_refs, outs):
            o_ref[...] = o.astype(o_ref.dtype)

    a_specs = [pl.BlockSpec((tm, k), lambda i, j: (i, 0)) for k in ks_a]
    if mode == "nn":
        b_specs = [pl.BlockSpec((k, tn), lambda i, j: (0, j)) for k in ks_b]
    else:
        b_specs = [pl.BlockSpec((tn, k), lambda i, j: (j, 0)) for k in ks_b]
    e_specs = [pl.BlockSpec((tm, tn), lambda i, j, o=off // tn: (i, o + j)) for off in offs]
    o_specs = [pl.BlockSpec((tm, tn), lambda i, j, o=(p[1] // tn if p else 0): (i, o + j)) for p in place]
    outs, carried = _call(
        body, name=name, grid=(M // tm, N // tn), in_specs=a_specs + b_specs + e_specs, out_specs=o_specs,
        out_shape=[jax.ShapeDtypeStruct((M, p[0] if p else N), d) for d, p in zip(out_dtypes, place)],
        semantics=("parallel", "arbitrary"),
        args=[*as_, *bs, *[e for e, _ in extras]], side=side)
    return outs if side is None else (outs, carried)


def _mm(a, b, mode, out_dtype, name, side=None):
    res = _mm_fused([a], [b], [(0, 0)], mode, [out_dtype], name, side=side)
    return res[0] if side is None else (res[0][0], res[1])


def _mm_kloop(a, b, mode, out_dtype, name, add=None, side=None):
    if mode == "nn":
        (M, K), (_, N) = a.shape, b.shape
    elif mode == "nt":
        (M, K), (N, _) = a.shape, b.shape
    else:
        (K, M), (_, N) = a.shape, b.shape
    best = None
    for tm in (2816, 2048, 1408, 1024, 512, 256, 128):
        for tn in (2816, 2432, 2048, 1408, 1024, 512, 256, 128):
            for tk in (1024, 512, 256, 128):
                if M % tm or N % tn or K % tk:
                    continue
                vmem = (2 * tm * tn * 4 + 2 * tm * tn * _size(out_dtype) + 2 * tk * (tm + tn) * 2
                        + (2 * tm * tn * 4 if add is not None else 0))
                steps = (M // tm) * (N // tn) * (K // tk)
                cost = K * M * 2 * (N // tn) + K * N * 2 * (M // tm) + steps * MM_STEP_BYTES
                if vmem <= MM_VMEM_BUDGET and (best is None or cost < best[0]):
                    best = (cost, tm, tn, tk)
    _, tm, tn, tk = best
    nk = K // tk
    dims = _DOT_DIMS[mode]

    def body(*refs):
        if add is None:
            a_ref, b_ref, o_ref, acc_ref = refs
        else:
            a_ref, b_ref, add_ref, o_ref, acc_ref = refs
        k = pl.program_id(2)

        @pl.when(k == 0)
        def _():
            acc_ref[...] = jnp.zeros_like(acc_ref)

        acc_ref[...] += lax.dot_general(a_ref[...], b_ref[...], dims, preferred_element_type=F32)

        @pl.when(k == nk - 1)
        def _():
            r = acc_ref[...]
            if add is not None:
                r = r + add_ref[...]
            o_ref[...] = r.astype(o_ref.dtype)

    a_spec = pl.BlockSpec((tk, tm), lambda i, j, k: (k, i)) if mode == "tn" else pl.BlockSpec((tm, tk), lambda i, j, k: (i, k))
    b_spec = pl.BlockSpec((tn, tk), lambda i, j, k: (j, k)) if mode == "nt" else pl.BlockSpec((tk, tn), lambda i, j, k: (k, j))
    o_spec = pl.BlockSpec((tm, tn), lambda i, j, k: (i, j))
    outs, carried = _call(
        body, name=name, grid=(M // tm, N // tn, nk),
        in_specs=[a_spec, b_spec] + ([o_spec] if add is not None else []), out_specs=[o_spec],
        out_shape=[jax.ShapeDtypeStruct((M, N), out_dtype)], scratch_shapes=[pltpu.VMEM((tm, tn), F32)],
        semantics=("parallel", "parallel", "arbitrary"), args=(a, b) + ((add,) if add is not None else ()), side=side)
    return outs[0] if side is None else (outs[0], carried)


def _sigmoid(v):
    return 1.0 / (1.0 + jnp.exp(-v))


_GELU_C = math.sqrt(2.0 / math.pi)


def _gelu(v):
    return 0.5 * v * (1.0 + jnp.tanh(_GELU_C * (v + 0.044715 * v * v * v)))


def _gelu_grad(v):
    t = jnp.tanh(_GELU_C * (v + 0.044715 * v * v * v))
    return 0.5 * (1.0 + t) + 0.5 * v * (1.0 - t * t) * _GELU_C * (1.0 + 3.0 * 0.044715 * v * v)


def _rms(v, gain):
    r = lax.rsqrt(jnp.mean(v * v, axis=-1, keepdims=True) + EPS)
    return v * r * gain


def _rms_bwd(v, gain, dy):
    r = lax.rsqrt(jnp.mean(v * v, axis=-1, keepdims=True) + EPS)
    a = dy * gain
    dv = r * a - v * (r * r * r) * jnp.mean(a * v, axis=-1, keepdims=True)
    return dv, dy * v * r


def _row_tile(s):
    return _pick(s, (256, 128, 64, 8))


def _norm_in(x, gain):
    s, d = x.shape
    tr = _row_tile(s)

    def body(x_ref, g_ref, h_ref):
        h_ref[...] = _rms(x_ref[...], g_ref[...]).astype(BF16)

    row = pl.BlockSpec((tr, d), lambda i: (i, 0))
    vec = pl.BlockSpec((1, d), lambda i: (0, 0))
    return pl.pallas_call(body, name="norm_in", grid=(s // tr,), in_specs=[row, vec], out_specs=row,
                          out_shape=jax.ShapeDtypeStruct((s, d), BF16), compiler_params=_params(("parallel",)))(x, gain)


def _norm_mid(x, mo, g_post, g_pre):
    s, d = x.shape
    tr = _row_tile(s)

    def body(x_ref, mo_ref, g2_ref, g3_ref, x2_ref, h2_ref):
        x2 = x_ref[...] + _rms(mo_ref[...], g2_ref[...])
        x2_ref[...] = x2
        h2_ref[...] = _rms(x2, g3_ref[...]).astype(BF16)

    row = pl.BlockSpec((tr, d), lambda i: (i, 0))
    vec = pl.BlockSpec((1, d), lambda i: (0, 0))
    return pl.pallas_call(
        body, name="norm_mid", grid=(s // tr,), in_specs=[row, row, vec, vec], out_specs=[row, row],
        out_shape=[jax.ShapeDtypeStruct((s, d), F32), jax.ShapeDtypeStruct((s, d), BF16)],
        compiler_params=_params(("parallel",)))(x, mo, g_post, g_pre)


def _loss_head(x2, f, g_post, target):
    s, d = x2.shape
    tr = _row_tile(s)

    def body(x2_ref, f_ref, g_ref, t_ref, loss_ref, dout_ref, df_ref, dg_ref):
        @pl.when(pl.program_id(0) == 0)
        def _():
            loss_ref[...] = jnp.zeros_like(loss_ref)
            dg_ref[...] = jnp.zeros_like(dg_ref)

        fv = f_ref[...]
        g = g_ref[...]
        err = x2_ref[...] + _rms(fv, g) - t_ref[...]
        loss_ref[...] += 0.5 * jnp.sum(jnp.mean(err * err, axis=-1, keepdims=True), axis=0, keepdims=True)
        dout = err * (1.0 / d)
        dout_ref[...] = dout
        df, dg = _rms_bwd(fv, g, dout)
        df_ref[...] = df.astype(BF16)
        dg_ref[...] += jnp.sum(dg, axis=0, keepdims=True)

    row = pl.BlockSpec((tr, d), lambda i: (i, 0))
    vec = pl.BlockSpec((1, d), lambda i: (0, 0))
    one = pl.BlockSpec((1, 1), lambda i: (0, 0))
    return pl.pallas_call(
        body, name="loss_head", grid=(s // tr,), in_specs=[row, row, vec, row], out_specs=[one, row, row, vec],
        out_shape=[jax.ShapeDtypeStruct((1, 1), F32), jax.ShapeDtypeStruct((s, d), F32),
                   jax.ShapeDtypeStruct((s, d), BF16), jax.ShapeDtypeStruct((1, d), F32)],
        compiler_params=_params(("arbitrary",)))(x2, f, g_post, target)


def _norm_mid_bwd(x2, mo, g_post, g_pre, dout, dh2):
    s, d = x2.shape
    tr = _row_tile(s)

    def body(x2_ref, mo_ref, g2_ref, g3_ref, dout_ref, dh2_ref, dx2_ref, dmo_ref, dg2_ref, dg3_ref):
        @pl.when(pl.program_id(0) == 0)
        def _():
            dg2_ref[...] = jnp.zeros_like(dg2_ref)
            dg3_ref[...] = jnp.zeros_like(dg3_ref)

        dv, dg3 = _rms_bwd(x2_ref[...], g3_ref[...], dh2_ref[...])
        dx2 = dout_ref[...] + dv
        dx2_ref[...] = dx2
        dmo, dg2 = _rms_bwd(mo_ref[...], g2_ref[...], dx2)
        dmo_ref[...] = dmo.astype(BF16)
        dg2_ref[...] += jnp.sum(dg2, axis=0, keepdims=True)
        dg3_ref[...] += jnp.sum(dg3, axis=0, keepdims=True)

    row = pl.BlockSpec((tr, d), lambda i: (i, 0))
    vec = pl.BlockSpec((1, d), lambda i: (0, 0))
    return pl.pallas_call(
        body, name="norm_mid_bwd", grid=(s // tr,), in_specs=[row, row, vec, vec, row, row],
        out_specs=[row, row, vec, vec],
        out_shape=[jax.ShapeDtypeStruct((s, d), F32), jax.ShapeDtypeStruct((s, d), BF16),
                   jax.ShapeDtypeStruct((1, d), F32), jax.ShapeDtypeStruct((1, d), F32)],
        compiler_params=_params(("arbitrary",)))(x2, mo, g_post, g_pre, dout, dh2)


def _norm_in_bwd(x, gain, dh, dx2):
    s, d = x.shape
    tr = _row_tile(s)

    def body(x_ref, g_ref, dh_ref, dx2_ref, dx_ref, dg_ref):
        @pl.when(pl.program_id(0) == 0)
        def _():
            dg_ref[...] = jnp.zeros_like(dg_ref)

        dv, dg = _rms_bwd(x_ref[...], g_ref[...], dh_ref[...])
        dx_ref[...] = dx2_ref[...] + dv
        dg_ref[...] += jnp.sum(dg, axis=0, keepdims=True)

    row = pl.BlockSpec((tr, d), lambda i: (i, 0))
    vec = pl.BlockSpec((1, d), lambda i: (0, 0))
    return pl.pallas_call(
        body, name="norm_in_bwd", grid=(s // tr,), in_specs=[row, vec, row, row], out_specs=[row, vec],
        out_shape=[jax.ShapeDtypeStruct((s, d), F32), jax.ShapeDtypeStruct((1, d), F32)],
        compiler_params=_params(("arbitrary",)))(x, gain, dh, dx2)


def _swiglu_epilogue(rs, es):
    g, u = rs
    return [g * _sigmoid(g) * u, g, u]


def _swiglu_bwd_epilogue(rs, es):
    d = rs[0]
    g, u = es[0].astype(F32), es[1].astype(F32)
    sg = _sigmoid(g)
    return [d * u * sg * (1.0 + g * (1.0 - sg)), d * g * sg]


def _sum_epilogue(rs, es):
    return [rs[0] + rs[1]]


def _gates_epilogue(rs, es):
    ab, gv, gg = rs
    ga, gs = es
    return [_sigmoid(ga) * ab + _sigmoid(gs) * gv * _sigmoid(gg), ab, gv, gg]


def _gates_bwd_epilogue(rs, es):
    dm = rs[0]
    ga, gs, ab, gv, gg = (e.astype(F32) for e in es)
    sa, ss, sg = _sigmoid(ga), _sigmoid(gs), _sigmoid(gg)
    dsb = dm * ss
    return [dm * ab * sa * (1.0 - sa), dm * gv * sg * ss * (1.0 - ss), dm * sa, dsb * sg, dsb * gv * sg * (1.0 - sg)]


def _put_cols(dz, src, col_off):
    s, w = src.shape
    tr = _row_tile(s)
    tc = _pick(math.gcd(w, col_off), (512, 256, 128))
    off = col_off // tc

    def body(src_ref, dz_ref, o_ref):
        del dz_ref
        o_ref[...] = src_ref[...].astype(o_ref.dtype)

    return pl.pallas_call(
        body, name="put_cols", grid=(s // tr, w // tc),
        in_specs=[pl.BlockSpec((tr, tc), lambda i, j: (i, j)), pl.BlockSpec(memory_space=pl.ANY)],
        out_specs=pl.BlockSpec((tr, tc), lambda i, j: (i, off + j)),
        out_shape=jax.ShapeDtypeStruct(dz.shape, dz.dtype), input_output_aliases={1: 0},
        compiler_params=_params(("parallel", "parallel")))(src, dz)


ATTN_ROWS = 2048


def _dilate_qkv(z, g, d):
    s = z.shape[0]
    tm = ATTN_ROWS
    per = tm // d
    nh = HEADS_PER_GROUP

    def body(z_ref, o_ref):
        for r in range(d):
            rows = z_ref[...] if d == 1 else z_ref[pl.ds(r, per, stride=d), :]
            o_ref[0, r] = rows.astype(BF16)

    return pl.pallas_call(
        body, name=f"dilate_qkv_{g}", grid=(s // tm, 3, nh),
        in_specs=[pl.BlockSpec((tm, HEAD_DIM), lambda i, w, h: (i, (3 * w + g) * nh + h))],
        out_specs=pl.BlockSpec((1, d, per, HEAD_DIM), lambda i, w, h: (w, 0, i, h)),
        out_shape=jax.ShapeDtypeStruct((3, d, s // d, GROUP_W), BF16),
        compiler_params=_params(("parallel", "parallel", "parallel")))(z)


def _undilate_dqkv(dqkv, dz, g, d):
    s = dz.shape[0]
    tm = ATTN_ROWS
    per = tm // d
    nh = HEADS_PER_GROUP

    def body(i_ref, dz_ref, o_ref, nat_ref):
        del dz_ref
        if d == 1:
            o_ref[...] = i_ref[0, 0]
        else:
            for r in range(d):
                nat_ref[pl.ds(r, per, stride=d), :] = i_ref[0, r].astype(F32)
            o_ref[...] = nat_ref[...].astype(BF16)

    return pl.pallas_call(
        body, name=f"undilate_dqkv_{g}", grid=(s // tm, 3, nh),
        in_specs=[pl.BlockSpec((1, d, per, HEAD_DIM), lambda i, w, h: (w, 0, i, h)),
                  pl.BlockSpec(memory_space=pl.ANY)],
        out_specs=pl.BlockSpec((tm, HEAD_DIM), lambda i, w, h: (i, (3 * w + g) * nh + h)),
        out_shape=jax.ShapeDtypeStruct(dz.shape, dz.dtype), input_output_aliases={1: 0},
        scratch_shapes=[pltpu.VMEM((tm, HEAD_DIM), F32)],
        compiler_params=_params(("parallel", "parallel", "parallel")))(dqkv, dz)


def _alibi_slope(head):
    return 2.0 ** (-8.0 * (head + 1) / N_ATTN_HEADS)


def _dot_nt(a, b):
    return lax.dot_general(a, b, _DOT_DIMS["nt"], preferred_element_type=F32)


def _dot_tn(a, b):
    return lax.dot_general(a, b, _DOT_DIMS["tn"], preferred_element_type=F32)


def _dot(a, b):
    return jnp.dot(a, b, preferred_element_type=F32)


GROUP_ROWS = HEADS_PER_GROUP * ATTN_BLK


def _band_bias(g, d, pairs):
    qi = jnp.arange(ATTN_BLK)[:, None]
    ki = jnp.arange(ATTN_BLK)[None, :]
    rows = []
    for hh in range(HEADS_PER_GROUP):
        slope_d = _alibi_slope(g * HEADS_PER_GROUP + hh) * d
        tiles = []
        for kind in pairs:
            dist = qi - ki if kind == "cur" else ATTN_BLK + qi - ki
            ok = dist >= 0 if kind == "cur" else dist <= ATTN_BLK
            tiles.append(jnp.where(ok, -slope_d * dist.astype(F32), NEG_BIG))
        rows.append(jnp.concatenate(tiles, axis=1))
    return jnp.concatenate(rows, axis=0).astype(F32)


def _tile_cols(t):
    return slice(t * ATTN_BLK, (t + 1) * ATTN_BLK)


def _attn_fwd(qkv, g, d):
    _, _, L, _ = qkv.shape
    nb = L // ATTN_BLK
    scale = HEAD_DIM ** -0.5

    def body(q_ref, kc_ref, kp_ref, vc_ref, vp_ref, bias_ref, o_ref, lse_ref, s_ref, p_ref):
        n = pl.program_id(1)
        for hh in range(HEADS_PER_GROUP):
            cols, rows = _tile_cols(hh), _tile_cols(hh)
            q = q_ref[0, 0, :, cols]
            s_ref[rows, _tile_cols(0)] = _dot_nt(q, kp_ref[0, 0, :, cols])
            s_ref[rows, _tile_cols(1)] = _dot_nt(q, kc_ref[0, 0, :, cols])
        col = lax.broadcasted_iota(jnp.int32, (GROUP_ROWS, 2 * ATTN_BLK), 1)
        s = s_ref[...] * scale + bias_ref[...]
        s = jnp.where(jnp.logical_and(col < ATTN_BLK, n == 0), NEG_BIG, s)
        m = jnp.max(s, axis=-1, keepdims=True)
        e = jnp.exp(s - m)
        l = jnp.sum(e, axis=-1, keepdims=True)
        p_ref[...] = (e * (1.0 / l)).astype(BF16)
        lse = m + jnp.log(l)
        for hh in range(HEADS_PER_GROUP):
            cols, rows = _tile_cols(hh), _tile_cols(hh)
            o_ref[0, :, cols] = (_dot(p_ref[rows, _tile_cols(0)], vp_ref[0, 0, :, cols])
                                 + _dot(p_ref[rows, _tile_cols(1)], vc_ref[0, 0, :, cols]))
            lse_ref[0, :, cols] = jnp.broadcast_to(lse[rows], (ATTN_BLK, HEAD_DIM))

    def spec(w, shift):
        return pl.BlockSpec((1, 1, ATTN_BLK, GROUP_W), lambda r, n: (w, r, jnp.maximum(n + shift, 0), 0))

    out = pl.BlockSpec((1, ATTN_BLK, GROUP_W), lambda r, n: (r, n, 0))
    bias = _band_bias(g, d, ("prev", "cur"))
    return pl.pallas_call(
        body, name=f"attn_fwd_{g}", grid=(d, nb),
        in_specs=[spec(0, 0), spec(1, 0), spec(1, -1), spec(2, 0), spec(2, -1),
                  pl.BlockSpec(bias.shape, lambda r, n: (0, 0))],
        out_specs=[out, out], out_shape=[jax.ShapeDtypeStruct((d, L, GROUP_W), F32)] * 2,
        scratch_shapes=[pltpu.VMEM((GROUP_ROWS, 2 * ATTN_BLK), F32), pltpu.VMEM((GROUP_ROWS, 2 * ATTN_BLK), BF16)],
        compiler_params=_params(("parallel", "parallel")))(qkv, qkv, qkv, qkv, qkv, bias)


def _attn_bwd(qkv, do, lse, cc, g, d):
    _, _, L, _ = qkv.shape
    nb = L // ATTN_BLK
    scale = HEAD_DIM ** -0.5
    a_, b_, c_ = _tile_cols(0), _tile_cols(1), _tile_cols(2)

    def body(q0_ref, q1_ref, k0_ref, kp_ref, v0_ref, vp_ref, do0_ref, do1_ref, l0_ref, l1_ref, c0_ref, c1_ref,
             bias_ref, o_ref, s_ref, dp_ref, l_ref, c_ref, p_ref, ds_ref):
        n = pl.program_id(1)
        for hh in range(HEADS_PER_GROUP):
            cols, rows = _tile_cols(hh), _tile_cols(hh)
            q0, q1 = q0_ref[0, 0, :, cols], q1_ref[0, 0, :, cols]
            k0, kp = k0_ref[0, 0, :, cols], kp_ref[0, 0, :, cols]
            v0, vp = v0_ref[0, 0, :, cols], vp_ref[0, 0, :, cols]
            do0, do1 = do0_ref[0, :, cols], do1_ref[0, :, cols]
            s_ref[rows, a_], s_ref[rows, b_], s_ref[rows, c_] = _dot_nt(q0, k0), _dot_nt(q0, kp), _dot_nt(q1, k0)
            dp_ref[rows, a_], dp_ref[rows, b_], dp_ref[rows, c_] = _dot_nt(do0, v0), _dot_nt(do0, vp), _dot_nt(do1, v0)
            l_ref[rows, a_], l_ref[rows, b_], l_ref[rows, c_] = l0_ref[0, :, cols], l0_ref[0, :, cols], l1_ref[0, :, cols]
            c_ref[rows, a_], c_ref[rows, b_], c_ref[rows, c_] = c0_ref[0, :, cols], c0_ref[0, :, cols], c1_ref[0, :, cols]
        col = lax.broadcasted_iota(jnp.int32, (GROUP_ROWS, 3 * ATTN_BLK), 1)
        tile = col // ATTN_BLK
        gone = jnp.logical_or(jnp.logical_and(tile == 1, n == 0), jnp.logical_and(tile == 2, n == nb - 1))
        s = jnp.where(gone, NEG_BIG, s_ref[...] * scale + bias_ref[...])
        p = jnp.exp(s - l_ref[...])
        p_ref[...] = p.astype(BF16)
        ds_ref[...] = (p * (dp_ref[...] + c_ref[...])).astype(BF16)
        for hh in range(HEADS_PER_GROUP):
            cols, rows = _tile_cols(hh), _tile_cols(hh)
            q0, q1 = q0_ref[0, 0, :, cols], q1_ref[0, 0, :, cols]
            k0, kp = k0_ref[0, 0, :, cols], kp_ref[0, 0, :, cols]
            do0, do1 = do0_ref[0, :, cols], do1_ref[0, :, cols]
            o_ref[0, 0, :, cols] = ((_dot(ds_ref[rows, a_], k0) + _dot(ds_ref[rows, b_], kp)) * scale).astype(BF16)
            o_ref[1, 0, :, cols] = ((_dot_tn(ds_ref[rows, a_], q0) + _dot_tn(ds_ref[rows, c_], q1)) * scale).astype(BF16)
            o_ref[2, 0, :, cols] = (_dot_tn(p_ref[rows, a_], do0) + _dot_tn(p_ref[rows, c_], do1)).astype(BF16)

    def spec(w, shift):
        return pl.BlockSpec((1, 1, ATTN_BLK, GROUP_W), lambda r, n: (w, r, jnp.clip(n + shift, 0, nb - 1), 0))

    def spec3(shift):
        return pl.BlockSpec((1, ATTN_BLK, GROUP_W), lambda r, n: (r, jnp.clip(n + shift, 0, nb - 1), 0))

    bias = _band_bias(g, d, ("cur", "prev", "prev"))
    wide = (GROUP_ROWS, 3 * ATTN_BLK)
    return pl.pallas_call(
        body, name=f"attn_bwd_{g}", grid=(d, nb),
        in_specs=[spec(0, 0), spec(0, 1), spec(1, 0), spec(1, -1), spec(2, 0), spec(2, -1),
                  spec3(0), spec3(1), spec3(0), spec3(1), spec3(0), spec3(1), pl.BlockSpec(wide, lambda r, n: (0, 0))],
        out_specs=pl.BlockSpec((3, 1, ATTN_BLK, GROUP_W), lambda r, n: (0, r, n, 0)),
        out_shape=jax.ShapeDtypeStruct((3, d, L, GROUP_W), BF16),
        scratch_shapes=[pltpu.VMEM(wide, F32)] * 4 + [pltpu.VMEM(wide, BF16)] * 2,
        compiler_params=_params(("parallel", "parallel")))(qkv, qkv, qkv, qkv, qkv, qkv, do, do, lse, lse, cc, cc, bias)


def _load_natural(refs, nat_refs):
    for g, d in enumerate(ATTN_DILATIONS):
        if d == 1:
            nat_refs[g][...] = refs[g][0]
        else:
            per = ATTN_ROWS // d
            for r in range(d):
                nat_refs[g][pl.ds(r, per, stride=d), :] = refs[g][r]


def _mix_weights(lse_nat):
    l0, l1, l2 = lse_nat[0][...], lse_nat[1][...], lse_nat[2][...]
    m = jnp.maximum(jnp.maximum(l0, l1), l2)
    e0, e1, e2 = jnp.exp(l0 - m), jnp.exp(l1 - m), jnp.exp(l2 - m)
    inv = 1.0 / (e0 + e1 + e2)
    return e0 * inv, e1 * inv, e2 * inv


def _dilated_specs(s):
    return [pl.BlockSpec((d, ATTN_ROWS // d, HEAD_DIM), lambda i, h: (0, i, h)) for d in ATTN_DILATIONS]


NATURAL_SCRATCH = [pltpu.VMEM((ATTN_ROWS, HEAD_DIM), F32)] * (2 * len(ATTN_DILATIONS))


def _attn_merge(outs, lses):
    s = outs[0].shape[0] * outs[0].shape[1]

    def body(o0, o1, o2, l0, l1, l2, a_ref, *nat):
        onat, lnat = nat[:3], nat[3:]
        _load_natural((o0, o1, o2), onat)
        _load_natural((l0, l1, l2), lnat)
        w0, w1, w2 = _mix_weights(lnat)
        a_ref[...] = (w0 * onat[0][...] + w1 * onat[1][...] + w2 * onat[2][...]).astype(BF16)

    return pl.pallas_call(
        body, name="attn_merge", grid=(s // ATTN_ROWS, HEADS_PER_GROUP), in_specs=_dilated_specs(s) * 2,
        out_specs=pl.BlockSpec((ATTN_ROWS, HEAD_DIM), lambda i, h: (i, h)),
        out_shape=jax.ShapeDtypeStruct((s, GROUP_W), BF16), scratch_shapes=NATURAL_SCRATCH,
        compiler_params=_params(("parallel", "parallel")))(*outs, *lses)


def _attn_merge_bwd(outs, lses, dattn):
    s = dattn.shape[0]

    def body(o0, o1, o2, l0, l1, l2, da_ref, do0, do1, do2, c0, c1, c2, *nat):
        onat, lnat = nat[:3], nat[3:]
        _load_natural((o0, o1, o2), onat)
        _load_natural((l0, l1, l2), lnat)
        ws = _mix_weights(lnat)
        da = da_ref[...]
        attn = ws[0] * onat[0][...] + ws[1] * onat[1][...] + ws[2] * onat[2][...]
        tot = jnp.broadcast_to(jnp.sum(da * attn, axis=-1, keepdims=True), (ATTN_ROWS, HEAD_DIM))
        for g, (d, do_ref, c_ref) in enumerate(zip(ATTN_DILATIONS, (do0, do1, do2), (c0, c1, c2))):
            if d == 1:
                do_ref[0] = (ws[g] * da).astype(BF16)
                c_ref[0] = -ws[g] * tot
            else:
                onat[g][...] = ws[g] * da
                lnat[g][...] = -ws[g] * tot
                per = ATTN_ROWS // d
                for r in range(d):
                    do_ref[r] = onat[g][pl.ds(r, per, stride=d), :].astype(BF16)
                    c_ref[r] = lnat[g][pl.ds(r, per, stride=d), :]

    dil = _dilated_specs(s)
    shapes = [jax.ShapeDtypeStruct(o.shape, BF16) for o in outs] + [jax.ShapeDtypeStruct(o.shape, F32) for o in outs]
    return pl.pallas_call(
        body, name="attn_merge_bwd", grid=(s // ATTN_ROWS, HEADS_PER_GROUP),
        in_specs=dil * 2 + [pl.BlockSpec((ATTN_ROWS, HEAD_DIM), lambda i, h: (i, h))], out_specs=dil * 2,
        out_shape=shapes, scratch_shapes=NATURAL_SCRATCH,
        compiler_params=_params(("parallel", "parallel")))(*outs, *lses, dattn)


def _ssm_prepare(a_re, a_im, log_dt, b_re, b_im, c_re, c_im):
    n_g = a_re.shape[0]
    nj = n_g * SSM_GROUP // SSM_TILE_CH
    gpt = SSM_TILE_CH // SSM_GROUP
    dt = jnp.exp(log_dt)[:, None]
    mag = jnp.exp(a_re * dt)
    lr, li = mag * jnp.cos(a_im * dt), mag * jnp.sin(a_im * dt)
    den = a_re * a_re + a_im * a_im
    cr = ((lr - 1.0) * a_re + li * a_im) / den
    ci = (li * a_re - (lr - 1.0) * a_im) / den
    bb_re = cr[..., None] * b_re - ci[..., None] * b_im
    bb_im = cr[..., None] * b_im + ci[..., None] * b_re
    eye = jnp.eye(gpt, dtype=F32)

    def b_tiles(t):
        t = t.transpose(0, 2, 1).reshape(nj, gpt, SSM_GROUP, SSM_STATE)
        return jnp.einsum("jgcp,gh->jgchp", t, eye).reshape(nj, SSM_TILE_CH, SSM_TILE_ST)

    def c_tiles(t):
        t = t.reshape(nj, gpt, SSM_GROUP, SSM_STATE)
        return jnp.einsum("jgcp,gh->jhpgc", t, eye).reshape(nj, SSM_TILE_ST, SSM_TILE_CH)

    lam = jnp.stack([lr.reshape(-1), li.reshape(-1)])
    bmat = jnp.concatenate([b_tiles(bb_re), b_tiles(bb_im)], axis=2)
    cmat = jnp.concatenate([c_tiles(c_re), -c_tiles(c_im)], axis=1)
    return lam, bmat, cmat


SSM_SEGMENTS = 8


def _to_segment_order(nat, perm_ref):
    per = nat.shape[0] // SSM_SEGMENTS
    for i in range(SSM_SEGMENTS):
        perm_ref[pl.ds(i, per, stride=SSM_SEGMENTS), :] = nat[i * per:(i + 1) * per, :]
    return perm_ref[...]


def _to_time_order(val, perm_ref, store):
    per = val.shape[0] // SSM_SEGMENTS
    perm_ref[...] = val
    for i in range(SSM_SEGMENTS):
        store(i, perm_ref[pl.ds(i, per, stride=SSM_SEGMENTS), :])


def _fill_powers(lam_ref, w_ref, nj, tau_n):
    for j in range(nj):
        st = slice(j * SSM_TILE_ST, (j + 1) * SSM_TILE_ST)
        lr = jnp.broadcast_to(lam_ref[0:1, st], (SSM_SEGMENTS, SSM_TILE_ST))
        li = jnp.broadcast_to(lam_ref[1:2, st], (SSM_SEGMENTS, SSM_TILE_ST))
        wr, wi = lr, li
        for tau in range(tau_n):
            rows = slice(tau * SSM_SEGMENTS, (tau + 1) * SSM_SEGMENTS)
            w_ref[j, rows, :SSM_TILE_ST] = wr
            w_ref[j, rows, SSM_TILE_ST:] = wi
            wr, wi = wr * lr - wi * li, wr * li + wi * lr


def _segment_scan(src, xs_ref, w_tile, lr, li, cr, ci, conj, reverse):
    seg, half = SSM_SEGMENTS, SSM_TILE_ST
    tau_n = src.shape[0] // seg
    sgn = -1.0 if conj else 1.0
    lr8 = jnp.broadcast_to(lr, (seg, half))
    li8 = jnp.broadcast_to(li, (seg, half)) * sgn
    xr = jnp.zeros((seg, half), F32)
    xi = jnp.zeros((seg, half), F32)
    order = range(tau_n - 1, -1, -1) if reverse else range(tau_n)
    for tau in order:
        rows = slice(tau * seg, (tau + 1) * seg)
        xr, xi = lr8 * xr - li8 * xi + src[rows, :half], lr8 * xi + li8 * xr + src[rows, half:]
        xs_ref[rows, :half] = xr
        xs_ref[rows, half:] = xi
    pr = w_tile[(tau_n - 1) * seg:(tau_n - 1) * seg + 1, :half]
    pi = w_tile[(tau_n - 1) * seg:(tau_n - 1) * seg + 1, half:] * sgn
    fr, fi = cr, ci
    ins_r, ins_i = [None] * seg, [None] * seg
    runs = range(seg - 1, -1, -1) if reverse else range(seg)
    for i in runs:
        ins_r[i], ins_i[i] = fr, fi
        fr, fi = xr[i:i + 1, :] + pr * fr - pi * fi, xi[i:i + 1, :] + pr * fi + pi * fr
    in_r = jnp.concatenate(ins_r, axis=0)
    in_i = jnp.concatenate(ins_i, axis=0)
    for tau in range(tau_n):
        rows = slice(tau * seg, (tau + 1) * seg)
        wrow = (tau_n - 1 - tau) if reverse else tau
        wr = w_tile[wrow * seg:(wrow + 1) * seg, :half]
        wi = w_tile[wrow * seg:(wrow + 1) * seg, half:] * sgn
        xs_ref[rows, :half] += wr * in_r - wi * in_i
        xs_ref[rows, half:] += wr * in_i + wi * in_r
    return (fr, fi), (in_r, in_i)


def _ssm_dims(z, bmat, u_off):
    s = z.shape[0]
    nj = bmat.shape[0]
    t_rows = _pick(s, (256, 128))
    return s, nj, nj * SSM_TILE_CH, nj * SSM_TILE_ST, t_rows


def _ssm_fwd(z, bmat, cmat, lam, dskip, u_off, side=None):
    s, nj, w, ns, t_rows = _ssm_dims(z, bmat, u_off)
    per = t_rows // SSM_SEGMENTS

    def body(*refs):
        u_refs = refs[:nj]
        b_ref, c_ref, lam_ref, d_ref, y_ref, yg_ref, xin_ref, carry_ref, w_ref, xs_ref, perm_ref = refs[nj:]

        @pl.when(pl.program_id(0) == 0)
        def _():
            carry_ref[...] = jnp.zeros_like(carry_ref)
            _fill_powers(lam_ref, w_ref, nj, per)

        xin_ref[0] = carry_ref[...]
        for j in range(nj):
            st = slice(j * SSM_TILE_ST, (j + 1) * SSM_TILE_ST)
            ch = slice(j * SSM_TILE_CH, (j + 1) * SSM_TILE_CH)
            up = _to_segment_order(u_refs[j], perm_ref)
            bu = _dot(up.astype(BF16), b_ref[j])
            (fr, fi), _ = _segment_scan(bu, xs_ref, w_ref.at[j], lam_ref[0:1, st], lam_ref[1:2, st],
                                        carry_ref[0:1, st], carry_ref[1:2, st], conj=False, reverse=False)
            carry_ref[0:1, st] = fr
            carry_ref[1:2, st] = fi
            yp = _dot(xs_ref[...].astype(BF16), c_ref[j]) + d_ref[:, ch] * up

            def store(i, rows, ch=ch):
                y_ref[i * per:(i + 1) * per, ch] = rows
                yg_ref[i * per:(i + 1) * per, ch] = _gelu(rows).astype(BF16)

            _to_time_order(yp, perm_ref, store)

    u_specs = [pl.BlockSpec((t_rows, SSM_TILE_CH), lambda c, k=k: (c, u_off // SSM_TILE_CH + k)) for k in range(nj)]
    full3 = lambda shape: pl.BlockSpec(shape, lambda c: (0, 0, 0))
    full2 = lambda shape: pl.BlockSpec(shape, lambda c: (0, 0))
    rows = pl.BlockSpec((t_rows, w), lambda c: (c, 0))
    outs, carried = _call(
        body, name="ssm_fwd", grid=(s // t_rows,),
        in_specs=u_specs + [full3(bmat.shape), full3(cmat.shape), full2(lam.shape), full2(dskip.shape)],
        out_specs=[rows, rows, pl.BlockSpec((1, 2, ns), lambda c: (c, 0, 0))],
        out_shape=[jax.ShapeDtypeStruct((s, w), F32), jax.ShapeDtypeStruct((s, w), BF16),
                   jax.ShapeDtypeStruct((s // t_rows, 2, ns), F32)],
        scratch_shapes=[pltpu.VMEM((2, ns), F32), pltpu.VMEM((nj, t_rows, 2 * SSM_TILE_ST), F32),
                        pltpu.VMEM((t_rows, 2 * SSM_TILE_ST), F32), pltpu.VMEM((t_rows, SSM_TILE_CH), F32)],
        semantics=("arbitrary",), args=[*([z] * nj), bmat, cmat, lam, dskip], side=side)
    return outs if side is None else (outs, carried)


def _ssm_bwd(z, y, dyg, xin, bmat, cmat, lam, dskip, u_off, side=None):
    s, nj, w, ns, t_rows = _ssm_dims(z, bmat, u_off)
    nc = s // t_rows
    per = t_rows // SSM_SEGMENTS
    seg, half = SSM_SEGMENTS, SSM_TILE_ST

    def body(*refs):
        u_refs = refs[:nj]
        (y_ref, dyg_ref, xin_ref, b_ref, c_ref, lam_ref, d_ref, du_ref, db_ref, dc_ref, dlam_ref, dd_ref,
         carry_ref, w_ref, xs_ref, gs_ref, perm_ref, acc_ref) = refs[nj:]

        @pl.when(pl.program_id(0) == 0)
        def _():
            carry_ref[...] = jnp.zeros_like(carry_ref)
            db_ref[...] = jnp.zeros_like(db_ref)
            dc_ref[...] = jnp.zeros_like(dc_ref)
            dd_ref[...] = jnp.zeros_like(dd_ref)
            acc_ref[...] = jnp.zeros_like(acc_ref)
            _fill_powers(lam_ref, w_ref, nj, per)

        for j in range(nj):
            st = slice(j * SSM_TILE_ST, (j + 1) * SSM_TILE_ST)
            ch = slice(j * SSM_TILE_CH, (j + 1) * SSM_TILE_CH)
            lr, li = lam_ref[0:1, st], lam_ref[1:2, st]
            up = _to_segment_order(u_refs[j], perm_ref)
            upb = up.astype(BF16)
            dyp = _to_segment_order(dyg_ref[:, ch] * _gelu_grad(y_ref[:, ch]), perm_ref)
            dyb = dyp.astype(BF16)
            _, (in_r, in_i) = _segment_scan(_dot(upb, b_ref[j]), xs_ref, w_ref.at[j], lr, li,
                                            xin_ref[0, 0:1, st], xin_ref[0, 1:2, st], conj=False, reverse=False)
            (gr, gi), _ = _segment_scan(_dot_nt(dyb, c_ref[j]), gs_ref, w_ref.at[j], lr, li,
                                        carry_ref[0:1, st], carry_ref[1:2, st], conj=True, reverse=True)
            carry_ref[0:1, st] = gr
            carry_ref[1:2, st] = gi
            xs, gs = xs_ref[...], gs_ref[...]
            xsr, xsi, gsr, gsi = xs[:, :half], xs[:, half:], gs[:, :half], gs[:, half:]
            pxr = jnp.concatenate([in_r, xsr[:t_rows - seg]], axis=0)
            pxi = jnp.concatenate([in_i, xsi[:t_rows - seg]], axis=0)
            dl_r = gsr * pxr + gsi * pxi
            dl_i = gsi * pxr - gsr * pxi
            acc_ref[0, :, st] += jnp.sum(dl_r.reshape(per, seg, half), axis=0)
            acc_ref[1, :, st] += jnp.sum(dl_i.reshape(per, seg, half), axis=0)
            gx = gs.astype(BF16)
            dup = _dot_nt(gx, b_ref[j]) + d_ref[:, ch] * dyp

            def store(i, rows, ch=ch):
                du_ref[i * per:(i + 1) * per, ch] = rows.astype(BF16)

            _to_time_order(dup, perm_ref, store)
            db_ref[j] += _dot_tn(upb, gx)
            dc_ref[j] += _dot_tn(xs.astype(BF16), dyb)
            dd_ref[:, ch] += jnp.sum(dyp * up, axis=0, keepdims=True)

        @pl.when(pl.program_id(0) == nc - 1)
        def _():
            dlam_ref[...] = jnp.sum(acc_ref[...], axis=1)

    rev = lambda c: nc - 1 - c
    u_specs = [pl.BlockSpec((t_rows, SSM_TILE_CH), lambda c, k=k: (rev(c), u_off // SSM_TILE_CH + k))
               for k in range(nj)]
    full3 = lambda shape: pl.BlockSpec(shape, lambda c: (0, 0, 0))
    full2 = lambda shape: pl.BlockSpec(shape, lambda c: (0, 0))
    rows = pl.BlockSpec((t_rows, w), lambda c: (rev(c), 0))
    outs, carried = _call(
        body, name="ssm_bwd", grid=(nc,),
        in_specs=u_specs + [rows, rows, pl.BlockSpec((1, 2, ns), lambda c: (rev(c), 0, 0)),
                            full3(bmat.shape), full3(cmat.shape), full2(lam.shape), full2(dskip.shape)],
        out_specs=[rows, full3(bmat.shape), full3(cmat.shape), full2(lam.shape), full2(dskip.shape)],
        out_shape=[jax.ShapeDtypeStruct((s, w), BF16), jax.ShapeDtypeStruct(bmat.shape, F32),
                   jax.ShapeDtypeStruct(cmat.shape, F32), jax.ShapeDtypeStruct(lam.shape, F32),
                   jax.ShapeDtypeStruct(dskip.shape, F32)],
        scratch_shapes=[pltpu.VMEM((2, ns), F32), pltpu.VMEM((nj, t_rows, 2 * SSM_TILE_ST), F32),
                        pltpu.VMEM((t_rows, 2 * SSM_TILE_ST), F32), pltpu.VMEM((t_rows, 2 * SSM_TILE_ST), F32),
                        pltpu.VMEM((t_rows, SSM_TILE_CH), F32), pltpu.VMEM((2, SSM_SEGMENTS, ns), F32)],
        semantics=("arbitrary",), args=[*([z] * nj), y, dyg, xin, bmat, cmat, lam, dskip], side=side)
    return outs if side is None else (outs, carried)


def _adam_math(w, g, m, v):
    m = ADAM_B1 * m + (1.0 - ADAM_B1) * g
    v = ADAM_B2 * v + (1.0 - ADAM_B2) * (g * g)
    m_hat = m / (1.0 - ADAM_B1 ** ADAM_STEP)
    v_hat = v / (1.0 - ADAM_B2 ** ADAM_STEP)
    delta = -ADAM_LR * (m_hat / (jnp.sqrt(v_hat) + ADAM_EPS) + ADAM_WD * w)
    return delta, m, v


def _adam_rows(r, c):
    for tr in (512, 256, 128, 64, 32, 16, 8):
        if r % tr == 0 and tr * c * 4 <= (1 << 20):
            return tr
    return r


def _adamw_big(w, p_mine, p_sib, m, v, name):
    r, c = w.shape
    tr = _adam_rows(r, c)

    def body(w_ref, a_ref, b_ref, m_ref, v_ref, g_ref, d_ref, nm_ref, nv_ref):
        g = a_ref[...] + b_ref[...]
        g_ref[...] = g
        d_ref[...], nm_ref[...], nv_ref[...] = _adam_math(w_ref[...], g, m_ref[...], v_ref[...])

    blk = pl.BlockSpec((tr, c), lambda i: (i, 0))
    return pl.pallas_call(body, name=f"adamw_{name}", grid=(r // tr,), in_specs=[blk] * 5, out_specs=[blk] * 4,
                          out_shape=[jax.ShapeDtypeStruct((r, c), F32)] * 4,
                          compiler_params=_params(("parallel",)))(w, p_mine, p_sib, m, v)


def _adamw_small(w, parts, m, v):
    r, c = w.shape
    n_dev = parts.shape[0]

    def body(w_ref, p_ref, m_ref, v_ref, g_ref, d_ref, nm_ref, nv_ref):
        g = p_ref[0]
        for k in range(1, n_dev):
            g = g + p_ref[k]
        g_ref[...] = g
        d_ref[...], nm_ref[...], nv_ref[...] = _adam_math(w_ref[...], g, m_ref[...], v_ref[...])

    blk = pl.BlockSpec((r, c), lambda i: (0, 0))
    return pl.pallas_call(body, name="adamw_small", grid=(1,),
                          in_specs=[blk, pl.BlockSpec((n_dev, r, c), lambda i: (0, 0, 0)), blk, blk],
                          out_specs=[blk] * 4, out_shape=[jax.ShapeDtypeStruct((r, c), F32)] * 4,
                          compiler_params=_params(("arbitrary",)))(w, parts, m, v)


def _cast_bf16(w, name):
    r, c = w.shape
    tr = _adam_rows(r, c)

    def body(w_ref, o_ref):
        o_ref[...] = w_ref[...].astype(BF16)

    blk = pl.BlockSpec((tr, c), lambda i: (i, 0))
    return pl.pallas_call(body, name=f"cast_{name}", grid=(r // tr,), in_specs=[blk], out_specs=blk,
                          out_shape=jax.ShapeDtypeStruct((r, c), BF16), compiler_params=_params(("parallel",)))(w)


def _sum_slots(recv, name):
    _, r, c = recv.shape
    tr = _adam_rows(r, c)

    def body(p_ref, o_ref):
        acc = p_ref[0].astype(F32)
        for k in range(1, N_CHIPS):
            acc = acc + p_ref[k].astype(F32)
        o_ref[...] = acc

    return pl.pallas_call(body, name=f"sum_{name}", grid=(r // tr,),
                          in_specs=[pl.BlockSpec((N_CHIPS, tr, c), lambda i: (0, i, 0))],
                          out_specs=pl.BlockSpec((tr, c), lambda i: (i, 0)),
                          out_shape=jax.ShapeDtypeStruct((r, c), F32), compiler_params=_params(("parallel",)))(recv)


BIG_WEIGHTS = ("w_in", "w_attn_up", "w_glu_v", "w_glu_g", "w_out", "w_ffn_gate", "w_ffn_up", "w_ffn_down")
COL_SHARDED = ("w_in", "w_attn_up", "w_glu_v", "w_glu_g", "w_ffn_gate", "w_ffn_up")


def _aligned(v, m):
    return v if isinstance(v, int) else pl.multiple_of(v, m)


def _shard_of(ref, name, j, shard_shape, half=None):
    r, c = shard_shape
    rows = r if half is None else r // 2
    row0 = 0 if half is None else half * rows
    if name in COL_SHARDED:
        return ref.at[pl.ds(_aligned(row0, 16), rows), pl.ds(_aligned(j * c, 128), c)]
    return ref.at[pl.ds(_aligned(j * r + row0, 16), rows), :]


def _other_chips():
    x, y = lax.axis_index("x"), lax.axis_index("y")
    return [(1 - x, y), (x, 1 - y), (1 - x, 1 - y)]


def _dma_sems(n, arrays):
    return [pltpu.SemaphoreType.DMA((n, 3))] * arrays + [pltpu.SemaphoreType.DMA((n,))]


def _gather_side(shards):
    names = list(shards)
    n = len(names)
    full_shapes = []
    for k in names:
        r, c = shards[k].shape
        full_shapes.append((r, c * N_CHIPS) if k in COL_SHARDED else (r * N_CHIPS, c))

    def build(src, dst, sems):
        send_sems, recv_sems, pass_send_sems, pass_recv_sems, local_sems = sems
        x, y, c = lax.axis_index("x"), lax.axis_index("y"), lax.axis_index("c")
        me = 2 * x + y
        locals_, sends, arrivals, forwards, passed_on = [], [], [], [], []
        for i, k in enumerate(names):
            shape = shards[k].shape
            half_rows = shape[0] // 2
            locals_.append(pltpu.make_async_copy(src[i], _shard_of(dst[i], k, me, shape), local_sems.at[i]))
            my_half = src[i].at[pl.ds(_aligned(c * half_rows, 16), half_rows), :]
            for p, (px, py) in enumerate(_other_chips()):
                peer = 2 * px + py
                landed = _shard_of(dst[i], k, peer, shape, half=c)
                sends.append(pltpu.make_async_remote_copy(
                    src_ref=my_half, dst_ref=_shard_of(dst[i], k, me, shape, half=c), send_sem=send_sems.at[i, p],
                    recv_sem=recv_sems.at[i, p], device_id=(px, py, c), device_id_type=MESH))
                arrivals.append(pltpu.make_async_remote_copy(
                    src_ref=my_half, dst_ref=landed, send_sem=send_sems.at[i, p],
                    recv_sem=recv_sems.at[i, p], device_id=(px, py, c), device_id_type=MESH))
                forwards.append(pltpu.make_async_remote_copy(
                    src_ref=landed, dst_ref=landed, send_sem=pass_send_sems.at[i, p],
                    recv_sem=pass_recv_sems.at[i, p], device_id=(x, y, 1 - c), device_id_type=MESH))
                passed_on.append(pltpu.make_async_remote_copy(
                    src_ref=landed, dst_ref=_shard_of(dst[i], k, peer, shape, half=1 - c),
                    send_sem=pass_send_sems.at[i, p], recv_sem=pass_recv_sems.at[i, p],
                    device_id=(x, y, 1 - c), device_id_type=MESH))
        return locals_, sends, arrivals, forwards, passed_on

    return _Side([shards[k] for k in names], [jax.ShapeDtypeStruct(s, BF16) for s in full_shapes], _dma_sems(n, 4), build)


def _scatter_side(grads, shard_shapes):
    names = list(grads)
    n = len(names)

    def build(src, dst, sems):
        send_sems, recv_sems, local_sems = sems
        x, y, c = lax.axis_index("x"), lax.axis_index("y"), lax.axis_index("c")
        me = 2 * x + y
        locals_, sends, arrivals = [], [], []
        for i, k in enumerate(names):
            shape = shard_shapes[k]
            locals_.append(pltpu.make_async_copy(_shard_of(src[i], k, me, shape), dst[i].at[me], local_sems.at[i]))
            for p, (px, py) in enumerate(_other_chips()):
                peer = 2 * px + py
                sends.append(pltpu.make_async_remote_copy(
                    src_ref=_shard_of(src[i], k, peer, shape), dst_ref=dst[i].at[me], send_sem=send_sems.at[i, p],
                    recv_sem=recv_sems.at[i, p], device_id=(px, py, c), device_id_type=MESH))
                arrivals.append(pltpu.make_async_remote_copy(
                    src_ref=_shard_of(src[i], k, peer, shape), dst_ref=dst[i].at[peer], send_sem=send_sems.at[i, p],
                    recv_sem=recv_sems.at[i, p], device_id=(px, py, c), device_id_type=MESH))
        return locals_, sends, arrivals, [None] * len(arrivals), []

    return _Side([grads[k] for k in names],
                 [jax.ShapeDtypeStruct((N_CHIPS,) + tuple(shard_shapes[k]), BF16) for k in names], _dma_sems(n, 2), build)


def _swap_with_sibling(parts):
    names = list(parts)
    n = len(names)

    def body(*refs):
        src, dst = refs[:n], refs[n:2 * n]
        send_sems, recv_sems = refs[2 * n:]
        sibling = (lax.axis_index("x"), lax.axis_index("y"), 1 - lax.axis_index("c"))
        copies = []
        for i in range(n):
            cp = pltpu.make_async_remote_copy(src_ref=src[i], dst_ref=dst[i], send_sem=send_sems.at[i],
                                              recv_sem=recv_sems.at[i], device_id=sibling, device_id_type=MESH)
            cp.start()
            copies.append(cp)
        for cp in copies:
            cp.wait_recv()
        for cp in copies:
            cp.wait_send()

    outs = pl.pallas_call(
        body, name="swap_with_sibling", in_specs=[HBM] * n, out_specs=[HBM] * n,
        out_shape=[jax.ShapeDtypeStruct(parts[k].shape, F32) for k in names],
        scratch_shapes=[pltpu.SemaphoreType.DMA((n,)), pltpu.SemaphoreType.DMA((n,))],
        compiler_params=pltpu.CompilerParams(has_side_effects=True),
    )(*[parts[k] for k in names])
    return dict(zip(names, outs))


def _share_side(packed):
    r, c = packed.shape

    def build(src, dst, sems):
        send_sems, recv_sems, local_sem = sems
        x, y, cc = lax.axis_index("x"), lax.axis_index("y"), lax.axis_index("c")
        me = 4 * x + 2 * y + cc
        own = pltpu.make_async_copy(src[0], dst[0].at[me], local_sem)
        sends, arrivals = [], []
        flips = [(fx, fy, fc) for fx in range(2) for fy in range(2) for fc in range(2) if fx or fy or fc]
        for p, (fx, fy, fc) in enumerate(flips):
            px, py, pc = x ^ fx, y ^ fy, cc ^ fc
            sends.append(pltpu.make_async_remote_copy(
                src_ref=src[0], dst_ref=dst[0].at[me], send_sem=send_sems.at[p], recv_sem=recv_sems.at[p],
                device_id=(px, py, pc), device_id_type=MESH))
            arrivals.append(pltpu.make_async_remote_copy(
                src_ref=src[0], dst_ref=dst[0].at[4 * px + 2 * py + pc], send_sem=send_sems.at[p],
                recv_sem=recv_sems.at[p], device_id=(px, py, pc), device_id_type=MESH))
        return [own], sends, arrivals, [None] * len(arrivals), []

    return _Side([packed], [jax.ShapeDtypeStruct((8, r, c), F32)],
                 [pltpu.SemaphoreType.DMA((7,)), pltpu.SemaphoreType.DMA((7,)), pltpu.SemaphoreType.DMA], build)


SMALL_WEIGHTS = ("norm_mix_pre", "ssm_a_re", "ssm_a_im", "ssm_log_dt", "ssm_b_re", "ssm_b_im", "ssm_c_re", "ssm_c_im",
                 "ssm_d", "norm_mix_post", "norm_ffn_pre", "norm_ffn_post")
WEIGHT_ORDER = ("norm_mix_pre", "w_in", "w_attn_up", "ssm_a_re", "ssm_a_im", "ssm_log_dt", "ssm_b_re", "ssm_b_im",
                "ssm_c_re", "ssm_c_im", "ssm_d", "w_glu_v", "w_glu_g", "w_out", "norm_mix_post", "norm_ffn_pre",
                "w_ffn_gate", "w_ffn_up", "w_ffn_down", "norm_ffn_post")
PACK_LANES = 128
PACK_ROWS = 8
PACK_GROUPS = (SMALL_WEIGHTS[:1], SMALL_WEIGHTS[1:])


def _pack_group(arrs, names):
    flat = jnp.concatenate([arrs[k].reshape(-1) for k in names])
    pad = -flat.shape[0] % (PACK_LANES * PACK_ROWS)
    return jnp.pad(flat, (0, pad)).reshape(-1, PACK_LANES)


def _pack_small(arrs):
    return jnp.concatenate([_pack_group(arrs, names) for names in PACK_GROUPS], axis=0)


def _unpack_small(packed, like):
    out, row = {}, 0
    for names in PACK_GROUPS:
        rows = _pack_group(like, names).shape[0]
        flat, pos = packed[row:row + rows].reshape(-1), 0
        for k in names:
            n = like[k].size
            out[k] = flat[pos:pos + n].reshape(like[k].shape)
            pos += n
        row += rows
    return out


def _local_step(x, target, big, small, shards=None, shard_shapes=None):
    s, d = x.shape
    big, grads, slots = dict(big), {}, {}
    carry = shards is not None

    def gathering(names, call):
        if not carry:
            return call(None)
        res, got = call(_gather_side({k: shards[k] for k in names}))
        big.update(zip(names, got))
        return res

    def scattering(names, call):
        if not carry:
            return call(None)
        res, got = call(_scatter_side({k: grads[k] for k in names}, shard_shapes))
        slots.update(zip(names, got))
        return res

    u_off = 3 * HQ
    gate_off = u_off + d // 2
    g1, g2, g3, g4 = (small[k][0:1] for k in ("norm_mix_pre", "norm_mix_post", "norm_ffn_pre", "norm_ffn_post"))
    ssm_names = ("ssm_a_re", "ssm_a_im", "ssm_log_dt", "ssm_b_re", "ssm_b_im", "ssm_c_re", "ssm_c_im")
    (lam, bmat, cmat), ssm_vjp = jax.vjp(_ssm_prepare, *[small[k][0] for k in ssm_names])
    bmat, cmat = bmat.astype(BF16), cmat.astype(BF16)
    dskip = small["ssm_d"][0:1]

    h1 = _norm_in(x, g1)
    z = gathering(("w_attn_up", "w_glu_v", "w_glu_g", "w_out", "w_ffn_gate"),
                  lambda side: _mm(h1, big["w_in"], "nn", F32, "in_proj", side=side))
    y, yg, xin = gathering(("w_ffn_up",), lambda side: _ssm_fwd(z, bmat, cmat, lam, dskip, u_off, side=side))
    qkv = [_dilate_qkv(z, g, dil) for g, dil in enumerate(ATTN_DILATIONS)]
    outs, lses = zip(*[_attn_fwd(qkv[g], g, dil) for g, dil in enumerate(ATTN_DILATIONS)])
    attn = _attn_merge(outs, lses)
    merged, ab, gv, gg = _mm_fused(
        [attn, yg], [big["w_attn_up"], big["w_glu_v"], big["w_glu_g"]], [(0, 0), (1, 1), (1, 2)], "nn",
        [BF16, BF16, BF16, BF16], "branches_merge", extras=[(z, gate_off), (z, gate_off + d)], epilogue=_gates_epilogue)
    mo = _mm(merged, big["w_out"], "nn", F32, "mix_out")
    x2, h2 = _norm_mid(x, mo, g2, g3)
    act, fg, fu = gathering(("w_ffn_down",), lambda side: _mm_fused(
        [h2], [big["w_ffn_gate"], big["w_ffn_up"]], [(0, 0), (0, 1)], "nn", [BF16, BF16, BF16], "ffn_up_act",
        epilogue=_swiglu_epilogue, side=side))
    f = _mm(act, big["w_ffn_down"], "nn", F32, "ffn_down")
    loss, dout, df, dg4 = _loss_head(x2, f, g4, target)

    dfg, dfu = _mm_fused([df], [big["w_ffn_down"]], [(0, 0)], "nt", [BF16, BF16], "d_ffn_act",
                         extras=[(fg, 0), (fu, 0)], epilogue=_swiglu_bwd_epilogue)
    grads["w_ffn_down"] = _mm_kloop(act, df, "tn", BF16, "dw_ffn_down")
    grads["w_ffn_gate"] = scattering(("w_ffn_down",), lambda side: _mm_kloop(h2, dfg, "tn", BF16, "dw_ffn_gate", side=side))
    grads["w_ffn_up"] = scattering(("w_ffn_gate",), lambda side: _mm_kloop(h2, dfu, "tn", BF16, "dw_ffn_up", side=side))
    dh2 = scattering(("w_ffn_up",), lambda side: _mm_kloop(dfg, big["w_ffn_gate"], "nt", F32, "d_h2_gate", side=side))
    dh2 = _mm_kloop(dfu, big["w_ffn_up"], "nt", F32, "d_h2_up", add=dh2)
    dx2, dmo, dg2, dg3 = _norm_mid_bwd(x2, mo, g2, g3, dout, dh2)
    dz, dgs, dab, dgv, dgg = _mm_fused(
        [dmo], [big["w_out"]], [(0, 0)], "nt", [BF16] * 5, "d_merged_gates",
        extras=[(z, gate_off), (z, gate_off + d), (ab, 0), (gv, 0), (gg, 0)], epilogue=_gates_bwd_epilogue,
        out_place=[(z.shape[1], gate_off), None, None, None, None])
    dz = _put_cols(dz, dgs, gate_off + d)
    grads["w_out"] = _mm_kloop(merged, dmo, "tn", BF16, "dw_out")
    dyg = _mm_fused([dgv, dgg], [big["w_glu_v"], big["w_glu_g"]], [(0, 0), (1, 1)], "nt", [F32], "d_yg",
                    epilogue=_sum_epilogue)[0]
    grads["w_glu_v"] = _mm_kloop(yg, dgv, "tn", BF16, "dw_glu_v")
    grads["w_glu_g"] = _mm_kloop(yg, dgg, "tn", BF16, "dw_glu_g")
    du, dbmat, dcmat, dlam, dd = scattering(
        ("w_out", "w_glu_v", "w_glu_g"),
        lambda side: _ssm_bwd(z, y, dyg, xin, bmat, cmat, lam, dskip, u_off, side=side))
    dz = _put_cols(dz, du, u_off)
    dattn = _mm(dab, big["w_attn_up"], "nt", F32, "d_attn")
    grads["w_attn_up"] = _mm_kloop(attn, dab, "tn", BF16, "dw_attn_up")
    merged_bwd = _attn_merge_bwd(outs, lses, dattn)
    for g, dil in enumerate(ATTN_DILATIONS):
        dqkv = _attn_bwd(qkv[g], merged_bwd[g], lses[g], merged_bwd[3 + g], g, dil)
        dz = _undilate_dqkv(dqkv, dz, g, dil)
    small_grads = dict(zip(ssm_names, (t[None] for t in ssm_vjp((dlam, dbmat, dcmat)))))
    small_grads.update(norm_mix_post=dg2, norm_ffn_pre=dg3, norm_ffn_post=dg4, ssm_d=dd)
    if carry:
        side = _join_sides(_scatter_side({"w_attn_up": grads["w_attn_up"]}, shard_shapes),
                           _share_side(_pack_group(small_grads, PACK_GROUPS[1])))
        grads["w_in"], (slots["w_attn_up"], shared) = _mm_kloop(h1, dz, "tn", BF16, "dw_in", side=side)
    else:
        grads["w_in"] = _mm_kloop(h1, dz, "tn", BF16, "dw_in")
    dh1 = scattering(("w_in",), lambda side: _mm_kloop(dz, big["w_in"], "nt", F32, "d_h1", side=side))
    grad_x, dg1 = _norm_in_bwd(x, g1, dh1, dx2)
    small_grads["norm_mix_pre"] = dg1
    if carry:
        return loss[0, 0], grad_x, slots, (dg1, shared)
    return loss[0, 0], grad_x, grads, small_grads


def kernel(x, norm_mix_pre, w_in, w_attn_up, ssm_a_re, ssm_a_im, ssm_log_dt, ssm_b_re, ssm_b_im, ssm_c_re, ssm_c_im, ssm_d, w_glu_v, w_glu_g, w_out, norm_mix_post, norm_ffn_pre, w_ffn_gate, w_ffn_up, w_ffn_down, norm_ffn_post, loss_target, m_norm_mix_pre, m_w_in, m_w_attn_up, m_ssm_a_re, m_ssm_a_im, m_ssm_log_dt, m_ssm_b_re, m_ssm_b_im, m_ssm_c_re, m_ssm_c_im, m_ssm_d, m_w_glu_v, m_w_glu_g, m_w_out, m_norm_mix_post, m_norm_ffn_pre, m_w_ffn_gate, m_w_ffn_up, m_w_ffn_down, m_norm_ffn_post, v_norm_mix_pre, v_w_in, v_w_attn_up, v_ssm_a_re, v_ssm_a_im, v_ssm_log_dt, v_ssm_b_re, v_ssm_b_im, v_ssm_c_re, v_ssm_c_im, v_ssm_d, v_w_glu_v, v_w_glu_g, v_w_out, v_norm_mix_post, v_norm_ffn_pre, v_w_ffn_gate, v_w_ffn_up, v_w_ffn_down, v_norm_ffn_post):
    given = dict(locals())
    w = {k: given[k] for k in WEIGHT_ORDER}
    m = {k: given["m_" + k] for k in WEIGHT_ORDER}
    v = {k: given["v_" + k] for k in WEIGHT_ORDER}

    shards = {k: _cast_bf16(w[k][0], k) for k in BIG_WEIGHTS}
    shard_shapes = {k: w[k].shape[1:] for k in BIG_WEIGHTS}
    big = {"w_in": _run_side(_gather_side({"w_in": shards["w_in"]}), "gather_w_in")[0]}

    loss, grad_x, slots, small_grads = _local_step(x[0], loss_target[0], big, {k: w[k] for k in SMALL_WEIGHTS},
                                                   shards, shard_shapes)
    loss = lax.psum(loss, MESH_AXES)

    mine = {k: _sum_slots(slots[k], k) for k in BIG_WEIGHTS}
    theirs = _swap_with_sibling(mine)
    out_g, out_d, out_m, out_v = {}, {}, {}, {}
    for k in BIG_WEIGHTS:
        res = _adamw_big(w[k][0], mine[k], theirs[k], m[k][0], v[k][0], k)
        out_g[k], out_d[k], out_m[k], out_v[k] = (t[None] for t in res)

    pick = lambda tree: {k: tree[k] for k in SMALL_WEIGHTS}
    dg1, shared = small_grads
    late = _run_side(_share_side(_pack_group({"norm_mix_pre": dg1}, PACK_GROUPS[0])), "share_last_grad")[0]
    parts = jnp.concatenate([late, shared], axis=1)
    res = _adamw_small(_pack_small(pick(w)), parts, _pack_small(pick(m)), _pack_small(pick(v)))
    for dst, packed in zip((out_g, out_d, out_m, out_v), res):
        dst.update(_unpack_small(packed, pick(w)))

    return (loss, grad_x[None], *[out_g[k] for k in WEIGHT_ORDER], *[out_d[k] for k in WEIGHT_ORDER],
            *[out_m[k] for k in WEIGHT_ORDER], *[out_v[k] for k in WEIGHT_ORDER])
```

```python
import functools
import math

import jax
import jax.numpy as jnp
from jax import lax
from jax.experimental import pallas as pl
from jax.experimental.pallas import tpu as pltpu

F32 = jnp.float32
BF16 = jnp.bfloat16

EPS = 1e-6
HEAD_DIM = 128
HEADS_PER_GROUP = 4
ATTN_DILATIONS = (1, 4, 16)
ATTN_BLK = 128
N_ATTN_HEADS = HEADS_PER_GROUP * len(ATTN_DILATIONS)
GROUP_W = HEADS_PER_GROUP * HEAD_DIM
HQ = N_ATTN_HEADS * HEAD_DIM
SSM_GROUP = 16
SSM_STATE = 64
SSM_TILE_CH = 128
SSM_TILE_ST = SSM_TILE_CH // SSM_GROUP * SSM_STATE
ADAM_LR = 0.001
ADAM_B1 = 0.9
ADAM_B2 = 0.999
ADAM_EPS = 1e-08
ADAM_WD = 0.01
ADAM_STEP = 10
NEG_BIG = -1e30
V7X_VMEM_LIMIT = 56 * 1024 * 1024
MESH_AXES = ("x", "y", "c")
N_CHIPS = 4


def _pick(n, cands):
    for c in cands:
        if n % c == 0:
            return c
    raise ValueError(f"no tile of {cands} divides {n}")


def _params(sem):
    return pltpu.CompilerParams(dimension_semantics=sem, vmem_limit_bytes=V7X_VMEM_LIMIT)


HBM = pl.BlockSpec(memory_space=pl.ANY)
MESH = pl.DeviceIdType.MESH


class _Side:
    def __init__(self, srcs, out_shapes, sem_shapes, build, aliases=None):
        self.srcs, self.out_shapes, self.sem_shapes, self.build = list(srcs), list(out_shapes), list(sem_shapes), build
        self.aliases = dict(aliases or {})

    def start(self, src, dst, sems):
        local, sends = self.build(src, dst, sems)[:2]
        for cp in local + sends:
            cp.start()

    def wait(self, src, dst, sems):
        local, sends, arrivals, forwards, passed_on = self.build(src, dst, sems)
        for cp, forward in zip(arrivals, forwards):
            cp.wait_recv()
            if forward is not None:
                forward.start()
        for cp in passed_on:
            cp.wait_recv()
        for cp in sends + [f for f in forwards if f is not None]:
            cp.wait_send()
        for cp in local:
            cp.wait()


def _join_sides(a, b):
    ns, no, nm = len(a.srcs), len(a.out_shapes), len(a.sem_shapes)

    def build(src, dst, sems):
        ra, rb = a.build(src[:ns], dst[:no], sems[:nm]), b.build(src[ns:], dst[no:], sems[nm:])
        return tuple(p + q for p, q in zip(ra, rb))

    aliases = {**a.aliases, **{ns + k: no + v for k, v in b.aliases.items()}}
    return _Side(a.srcs + b.srcs, a.out_shapes + b.out_shapes, a.sem_shapes + b.sem_shapes, build, aliases)


def _call(body, *, name, grid, in_specs, out_specs, out_shape, semantics, args, scratch_shapes=(), side=None, **kw):
    in_specs, out_specs, out_shape, scratch_shapes = list(in_specs), list(out_specs), list(out_shape), list(scratch_shapes)
    if side is None:
        res = pl.pallas_call(body, name=name, grid=grid, in_specs=in_specs, out_specs=out_specs, out_shape=out_shape,
                             scratch_shapes=scratch_shapes, compiler_params=_params(semantics), **kw)(*args)
        return list(res), []
    n_in, n_out, n_scr = len(in_specs), len(out_specs), len(scratch_shapes)
    ns_in, ns_out = len(side.srcs), len(side.out_shapes)

    def carrying(*refs):
        ins, s_in = refs[:n_in], refs[n_in:n_in + ns_in]
        o0 = n_in + ns_in
        outs, s_out = refs[o0:o0 + n_out], refs[o0 + n_out:o0 + n_out + ns_out]
        c0 = o0 + n_out + ns_out
        scr, sems = refs[c0:c0 + n_scr], refs[c0 + n_scr:]
        ids = [pl.program_id(a) for a in range(len(grid))]
        first = functools.reduce(jnp.logical_and, [i == 0 for i in ids])
        last = functools.reduce(jnp.logical_and, [i == g - 1 for i, g in zip(ids, grid)])

        @pl.when(first)
        def _():
            side.start(s_in, s_out, sems)

        body(*ins, *outs, *scr)

        @pl.when(last)
        def _():
            side.wait(s_in, s_out, sems)

    res = pl.pallas_call(
        carrying, name=name, grid=grid, in_specs=in_specs + [HBM] * ns_in, out_specs=out_specs + [HBM] * ns_out,
        out_shape=out_shape + side.out_shapes, scratch_shapes=scratch_shapes + side.sem_shapes,
        input_output_aliases={n_in + k: n_out + v for k, v in side.aliases.items()},
        compiler_params=pltpu.CompilerParams(dimension_semantics=("arbitrary",) * len(grid),
                                             vmem_limit_bytes=V7X_VMEM_LIMIT, has_side_effects=True), **kw,
    )(*args, *side.srcs)
    return list(res[:n_out]), list(res[n_out:])


def _run_side(side, name):
    ns, no = len(side.srcs), len(side.out_shapes)

    def body(*refs):
        src, dst, sems = refs[:ns], refs[ns:ns + no], refs[ns + no:]
        side.start(src, dst, sems)
        side.wait(src, dst, sems)

    return list(pl.pallas_call(body, name=name, in_specs=[HBM] * ns, out_specs=[HBM] * no, out_shape=side.out_shapes,
                               scratch_shapes=side.sem_shapes,
                               compiler_params=pltpu.CompilerParams(has_side_effects=True))(*side.srcs))


_DOT_DIMS = {"nn": (((1,), (0,)), ((), ())), "nt": (((1,), (1,)), ((), ())), "tn": (((0,), (0,)), ((), ()))}


MM_VMEM_BUDGET = 44 * 1024 * 1024
MM_STEP_BYTES = 1 << 20


def _size(dtype):
    return jnp.dtype(dtype).itemsize


def _mm_fused(as_, bs, pairs, mode, out_dtypes, name, extras=(), epilogue=None, side=None, out_place=None):
    M = as_[0].shape[0]
    N = bs[0].shape[1] if mode == "nn" else bs[0].shape[0]
    ks_a = [a.shape[1] for a in as_]
    ks_b = [b.shape[0] if mode == "nn" else b.shape[1] for b in bs]
    if epilogue is None:
        epilogue = lambda rs, es: rs
    offs = [off for _, off in extras]
    place = list(out_place) if out_place else [None] * len(out_dtypes)
    offs_all = offs + [p[1] for p in place if p is not None]
    best = None
    for tm in (2048, 1024, 512, 256, 128):
        for tn in (2048, 1024, 512, 256, 128):
            if M % tm or N % tn or any(off % tn for off in offs_all):
                continue
            vmem = (sum(2 * tm * k * 2 for k in ks_a) + sum(2 * k * tn * 2 for k in ks_b)
                    + sum(2 * tm * tn * _size(d) for d in out_dtypes) + sum(2 * tm * tn * _size(e.dtype) for e, _ in extras)
                    + len(pairs) * tm * tn * 4)
            cost = sum(k * N * 2 for k in ks_b) * (M // tm) + (M // tm) * (N // tn) * MM_STEP_BYTES
            if vmem <= MM_VMEM_BUDGET and (best is None or cost < best[0]):
                best = (cost, tm, tn)
    _, tm, tn = best
    na, nb, ne, no = len(as_), len(bs), len(extras), len(out_dtypes)
    dims = _DOT_DIMS[mode]

    def body(*refs):
        a_refs, b_refs = refs[:na], refs[na:na + nb]
        e_refs, o_refs = refs[na + nb:na + nb + ne], refs[na + nb + ne:]
        rs = [lax.dot_general(a_refs[ai][...], b_refs[bi][...], dims, preferred_element_type=F32) for ai, bi in pairs]
        outs = epilogue(rs, [e[...] for e in e_refs])
        for o_ref, o in zip(o_refs, outs):
            o_ref[...] = o.astype(o_ref.dtype)

    a_specs = [pl.BlockSpec((tm, k), lambda i, j: (i, 0)) for k in ks_a]
    if mode == "nn":
        b_specs = [pl.BlockSpec((k, tn), lambda i, j: (0, j)) for k in ks_b]
    else:
        b_specs = [pl.BlockSpec((tn, k), lambda i, j: (j, 0)) for k in ks_b]
    e_specs = [pl.BlockSpec((tm, tn), lambda i, j, o=off // tn: (i, o + j)) for off in offs]
    o_specs = [pl.BlockSpec((tm, tn), lambda i, j, o=(p[1] // tn if p else 0): (i, o + j)) for p in place]
    outs, carried = _call(
        body, name=name, grid=(M // tm, N // tn), in_specs=a_specs + b_specs + e_specs, out_specs=o_specs,
        out_shape=[jax.ShapeDtypeStruct((M, p[0] if p else N), d) for d, p in zip(out_dtypes, place)],
        semantics=("parallel", "arbitrary"),
        args=[*as_, *bs, *[e for e, _ in extras]], side=side)
    return outs if side is None else (outs, carried)


def _mm(a, b, mode, out_dtype, name, side=None):
    res = _mm_fused([a], [b], [(0, 0)], mode, [out_dtype], name, side=side)
    return res[0] if side is None else (res[0][0], res[1])


def _mm_kloop(a, b, mode, out_dtype, name, add=None, side=None):
    if mode == "nn":
        (M, K), (_, N) = a.shape, b.shape
    elif mode == "nt":
        (M, K), (N, _) = a.shape, b.shape
    else:
        (K, M), (_, N) = a.shape, b.shape
    best = None
    for tm in (2816, 2048, 1408, 1024, 512, 256, 128):
        for tn in (2816, 2432, 2048, 1408, 1024, 512, 256, 128):
            for tk in (1024, 512, 256, 128):
                if M % tm or N % tn or K % tk:
                    continue
                vmem = (2 * tm * tn * 4 + 2 * tm * tn * _size(out_dtype) + 2 * tk * (tm + tn) * 2
                        + (2 * tm * tn * 4 if add is not None else 0))
                steps = (M // tm) * (N // tn) * (K // tk)
                cost = K * M * 2 * (N // tn) + K * N * 2 * (M // tm) + steps * MM_STEP_BYTES
                if vmem <= MM_VMEM_BUDGET and (best is None or cost < best[0]):
                    best = (cost, tm, tn, tk)
    _, tm, tn, tk = best
    nk = K // tk
    dims = _DOT_DIMS[mode]

    def body(*refs):
        if add is None:
            a_ref, b_ref, o_ref, acc_ref = refs
        else:
            a_ref, b_ref, add_ref, o_ref, acc_ref = refs
        k = pl.program_id(2)

        @pl.when(k == 0)
        def _():
            acc_ref[...] = jnp.zeros_like(acc_ref)

        acc_ref[...] += lax.dot_general(a_ref[...], b_ref[...], dims, preferred_element_type=F32)

        @pl.when(k == nk - 1)
        def _():
            r = acc_ref[...]
            if add is not None:
                r = r + add_ref[...]
            o_ref[...] = r.astype(o_ref.dtype)

    a_spec = pl.BlockSpec((tk, tm), lambda i, j, k: (k, i)) if mode == "tn" else pl.BlockSpec((tm, tk), lambda i, j, k: (i, k))
    b_spec = pl.BlockSpec((tn, tk), lambda i, j, k: (j, k)) if mode == "nt" else pl.BlockSpec((tk, tn), lambda i, j, k: (k, j))
    o_spec = pl.BlockSpec((tm, tn), lambda i, j, k: (i, j))
    outs, carried = _call(
        body, name=name, grid=(M // tm, N // tn, nk),
        in_specs=[a_spec, b_spec] + ([o_spec] if add is not None else []), out_specs=[o_spec],
        out_shape=[jax.ShapeDtypeStruct((M, N), out_dtype)], scratch_shapes=[pltpu.VMEM((tm, tn), F32)],
        semantics=("parallel", "parallel", "arbitrary"), args=(a, b) + ((add,) if add is not None else ()), side=side)
    return outs[0] if side is None else (outs[0], carried)


def _sigmoid(v):
    return 0.5 * jnp.tanh(0.5 * v) + 0.5


_GELU_C = math.sqrt(2.0 / math.pi)


def _gelu(v):
    return 0.5 * v * (1.0 + jnp.tanh(_GELU_C * (v + 0.044715 * v * v * v)))


def _gelu_grad(v):
    t = jnp.tanh(_GELU_C * (v + 0.044715 * v * v * v))
    return 0.5 * (1.0 + t) + 0.5 * v * (1.0 - t * t) * _GELU_C * (1.0 + 3.0 * 0.044715 * v * v)


def _rms(v, gain):
    r = lax.rsqrt(jnp.mean(v * v, axis=-1, keepdims=True) + EPS)
    return v * r * gain


def _rms_bwd(v, gain, dy):
    r = lax.rsqrt(jnp.mean(v * v, axis=-1, keepdims=True) + EPS)
    a = dy * gain
    dv = r * a - v * (r * r * r) * jnp.mean(a * v, axis=-1, keepdims=True)
    return dv, dy * v * r


def _row_tile(s):
    return _pick(s, (256, 128, 64, 8))


def _norm_in(x, gain):
    s, d = x.shape
    tr = _row_tile(s)

    def body(x_ref, g_ref, h_ref):
        h_ref[...] = _rms(x_ref[...], g_ref[...]).astype(BF16)

    row = pl.BlockSpec((tr, d), lambda i: (i, 0))
    vec = pl.BlockSpec((1, d), lambda i: (0, 0))
    return pl.pallas_call(body, name="norm_in", grid=(s // tr,), in_specs=[row, vec], out_specs=row,
                          out_shape=jax.ShapeDtypeStruct((s, d), BF16), compiler_params=_params(("parallel",)))(x, gain)


def _norm_mid(x, mo, g_post, g_pre):
    s, d = x.shape
    tr = _row_tile(s)

    def body(x_ref, mo_ref, g2_ref, g3_ref, x2_ref, h2_ref):
        x2 = x_ref[...] + _rms(mo_ref[...], g2_ref[...])
        x2_ref[...] = x2
        h2_ref[...] = _rms(x2, g3_ref[...]).astype(BF16)

    row = pl.BlockSpec((tr, d), lambda i: (i, 0))
    vec = pl.BlockSpec((1, d), lambda i: (0, 0))
    return pl.pallas_call(
        body, name="norm_mid", grid=(s // tr,), in_specs=[row, row, vec, vec], out_specs=[row, row],
        out_shape=[jax.ShapeDtypeStruct((s, d), F32), jax.ShapeDtypeStruct((s, d), BF16)],
        compiler_params=_params(("parallel",)))(x, mo, g_post, g_pre)


def _loss_head(x2, f, g_post, target):
    s, d = x2.shape
    tr = _row_tile(s)

    def body(x2_ref, f_ref, g_ref, t_ref, loss_ref, dout_ref, df_ref, dg_ref):
        @pl.when(pl.program_id(0) == 0)
        def _():
            loss_ref[...] = jnp.zeros_like(loss_ref)
            dg_ref[...] = jnp.zeros_like(dg_ref)

        fv = f_ref[...]
        g = g_ref[...]
        err = x2_ref[...] + _rms(fv, g) - t_ref[...]
        loss_ref[...] += 0.5 * jnp.sum(jnp.mean(err * err, axis=-1, keepdims=True), axis=0, keepdims=True)
        dout = err * (1.0 / d)
        dout_ref[...] = dout
        df, dg = _rms_bwd(fv, g, dout)
        df_ref[...] = df.astype(BF16)
        dg_ref[...] += jnp.sum(dg, axis=0, keepdims=True)

    row = pl.BlockSpec((tr, d), lambda i: (i, 0))
    vec = pl.BlockSpec((1, d), lambda i: (0, 0))
    one = pl.BlockSpec((1, 1), lambda i: (0, 0))
    return pl.pallas_call(
        body, name="loss_head", grid=(s // tr,), in_specs=[row, row, vec, row], out_specs=[one, row, row, vec],
        out_shape=[jax.ShapeDtypeStruct((1, 1), F32), jax.ShapeDtypeStruct((s, d), F32),
                   jax.ShapeDtypeStruct((s, d), BF16), jax.ShapeDtypeStruct((1, d), F32)],
        compiler_params=_params(("arbitrary",)))(x2, f, g_post, target)


def _norm_mid_bwd(x2, mo, g_post, g_pre, dout, dh2):
    s, d = x2.shape
    tr = _row_tile(s)

    def body(x2_ref, mo_ref, g2_ref, g3_ref, dout_ref, dh2_ref, dx2_ref, dmo_ref, dg2_ref, dg3_ref):
        @pl.when(pl.program_id(0) == 0)
        def _():
            dg2_ref[...] = jnp.zeros_like(dg2_ref)
            dg3_ref[...] = jnp.zeros_like(dg3_ref)

        dv, dg3 = _rms_bwd(x2_ref[...], g3_ref[...], dh2_ref[...])
        dx2 = dout_ref[...] + dv
        dx2_ref[...] = dx2
        dmo, dg2 = _rms_bwd(mo_ref[...], g2_ref[...], dx2)
        dmo_ref[...] = dmo.astype(BF16)
        dg2_ref[...] += jnp.sum(dg2, axis=0, keepdims=True)
        dg3_ref[...] += jnp.sum(dg3, axis=0, keepdims=True)

    row = pl.BlockSpec((tr, d), lambda i: (i, 0))
    vec = pl.BlockSpec((1, d), lambda i: (0, 0))
    return pl.pallas_call(
        body, name="norm_mid_bwd", grid=(s // tr,), in_specs=[row, row, vec, vec, row, row],
        out_specs=[row, row, vec, vec],
        out_shape=[jax.ShapeDtypeStruct((s, d), F32), jax.ShapeDtypeStruct((s, d), BF16),
                   jax.ShapeDtypeStruct((1, d), F32), jax.ShapeDtypeStruct((1, d), F32)],
        compiler_params=_params(("arbitrary",)))(x2, mo, g_post, g_pre, dout, dh2)


def _norm_in_bwd(x, gain, dh, dx2):
    s, d = x.shape
    tr = _row_tile(s)

    def body(x_ref, g_ref, dh_ref, dx2_ref, dx_ref, dg_ref):
        @pl.when(pl.program_id(0) == 0)
        def _():
            dg_ref[...] = jnp.zeros_like(dg_ref)

        dv, dg = _rms_bwd(x_ref[...], g_ref[...], dh_ref[...])
        dx_ref[...] = dx2_ref[...] + dv
        dg_ref[...] += jnp.sum(dg, axis=0, keepdims=True)

    row = pl.BlockSpec((tr, d), lambda i: (i, 0))
    vec = pl.BlockSpec((1, d), lambda i: (0, 0))
    return pl.pallas_call(
        body, name="norm_in_bwd", grid=(s // tr,), in_specs=[row, vec, row, row], out_specs=[row, vec],
        out_shape=[jax.ShapeDtypeStruct((s, d), F32), jax.ShapeDtypeStruct((1, d), F32)],
        compiler_params=_params(("arbitrary",)))(x, gain, dh, dx2)


def _swiglu_epilogue(rs, es):
    g, u = rs
    return [g * _sigmoid(g) * u, g, u]


def _swiglu_bwd_epilogue(rs, es):
    d = rs[0]
    g, u = es[0].astype(F32), es[1].astype(F32)
    sg = _sigmoid(g)
    return [d * u * sg * (1.0 + g * (1.0 - sg)), d * g * sg]


def _sum_epilogue(rs, es):
    return [rs[0] + rs[1]]


def _gates_epilogue(rs, es):
    ab, gv, gg = rs
    ga, gs = es
    return [_sigmoid(ga) * ab + _sigmoid(gs) * gv * _sigmoid(gg), ab, gv, gg]


def _gates_bwd_epilogue(rs, es):
    dm = rs[0]
    ga, gs, ab, gv, gg = (e.astype(F32) for e in es)
    sa, ss, sg = _sigmoid(ga), _sigmoid(gs), _sigmoid(gg)
    dsb = dm * ss
    return [dm * ab * sa * (1.0 - sa), dm * gv * sg * ss * (1.0 - ss), dm * sa, dsb * sg, dsb * gv * sg * (1.0 - sg)]


ATTN_ROWS = 2048


def _dilate_qkv(z, g, d):
    s = z.shape[0]
    tm = ATTN_ROWS
    per = tm // d
    nh = HEADS_PER_GROUP

    def body(z_ref, o_ref):
        for r in range(d):
            rows = z_ref[...] if d == 1 else z_ref[pl.ds(r, per, stride=d), :]
            o_ref[0, r] = rows.astype(BF16)

    return pl.pallas_call(
        body, name=f"dilate_qkv_{g}", grid=(s // tm, 3, nh),
        in_specs=[pl.BlockSpec((tm, HEAD_DIM), lambda i, w, h: (i, (3 * w + g) * nh + h))],
        out_specs=pl.BlockSpec((1, d, per, HEAD_DIM), lambda i, w, h: (w, 0, i, h)),
        out_shape=jax.ShapeDtypeStruct((3, d, s // d, GROUP_W), BF16),
        compiler_params=_params(("parallel", "parallel", "parallel")))(z)


def _undilate_dqkv(dqkv, dz, g, d):
    s = dz.shape[0]
    tm = ATTN_ROWS
    per = tm // d
    nh = HEADS_PER_GROUP

    def body(i_ref, dz_ref, o_ref, nat_ref):
        del dz_ref
        if d == 1:
            o_ref[...] = i_ref[0, 0]
        else:
            for r in range(d):
                nat_ref[pl.ds(r, per, stride=d), :] = i_ref[0, r].astype(F32)
            o_ref[...] = nat_ref[...].astype(BF16)

    return pl.pallas_call(
        body, name=f"undilate_dqkv_{g}", grid=(s // tm, 3, nh),
        in_specs=[pl.BlockSpec((1, d, per, HEAD_DIM), lambda i, w, h: (w, 0, i, h)),
                  pl.BlockSpec(memory_space=pl.ANY)],
        out_specs=pl.BlockSpec((tm, HEAD_DIM), lambda i, w, h: (i, (3 * w + g) * nh + h)),
        out_shape=jax.ShapeDtypeStruct(dz.shape, dz.dtype), input_output_aliases={1: 0},
        scratch_shapes=[pltpu.VMEM((tm, HEAD_DIM), F32)],
        compiler_params=_params(("parallel", "parallel", "parallel")))(dqkv, dz)


def _alibi_slope(head):
    return 2.0 ** (-8.0 * (head + 1) / N_ATTN_HEADS)


def _dot_nt(a, b):
    return lax.dot_general(a, b, _DOT_DIMS["nt"], preferred_element_type=F32)


def _dot_tn(a, b):
    return lax.dot_general(a, b, _DOT_DIMS["tn"], preferred_element_type=F32)


def _dot(a, b):
    return jnp.dot(a, b, preferred_element_type=F32)


GROUP_ROWS = HEADS_PER_GROUP * ATTN_BLK


def _band_bias(g, d, pairs):
    qi = jnp.arange(ATTN_BLK)[:, None]
    ki = jnp.arange(ATTN_BLK)[None, :]
    rows = []
    for hh in range(HEADS_PER_GROUP):
        slope_d = _alibi_slope(g * HEADS_PER_GROUP + hh) * d
        tiles = []
        for kind in pairs:
            dist = qi - ki if kind == "cur" else ATTN_BLK + qi - ki
            ok = dist >= 0 if kind == "cur" else dist <= ATTN_BLK
            tiles.append(jnp.where(ok, -slope_d * dist.astype(F32), NEG_BIG))
        rows.append(jnp.concatenate(tiles, axis=1))
    return jnp.concatenate(rows, axis=0).astype(F32)


def _tile_cols(t):
    return slice(t * ATTN_BLK, (t + 1) * ATTN_BLK)


def _attn_fwd(qkv, g, d):
    _, _, L, _ = qkv.shape
    nb = L // ATTN_BLK
    scale = HEAD_DIM ** -0.5

    def body(q_ref, kc_ref, kp_ref, vc_ref, vp_ref, bias_ref, o_ref, lse_ref, s_ref, p_ref):
        n = pl.program_id(1)
        for hh in range(HEADS_PER_GROUP):
            cols, rows = _tile_cols(hh), _tile_cols(hh)
            q = q_ref[0, 0, :, cols]
            s_ref[rows, _tile_cols(0)] = _dot_nt(q, kp_ref[0, 0, :, cols])
            s_ref[rows, _tile_cols(1)] = _dot_nt(q, kc_ref[0, 0, :, cols])
        col = lax.broadcasted_iota(jnp.int32, (GROUP_ROWS, 2 * ATTN_BLK), 1)
        s = s_ref[...] * scale + bias_ref[...]
        s = jnp.where(jnp.logical_and(col < ATTN_BLK, n == 0), NEG_BIG, s)
        m = jnp.max(s, axis=-1, keepdims=True)
        e = jnp.exp(s - m)
        l = jnp.sum(e, axis=-1, keepdims=True)
        p_ref[...] = (e * (1.0 / l)).astype(BF16)
        lse = m + jnp.log(l)
        for hh in range(HEADS_PER_GROUP):
            cols, rows = _tile_cols(hh), _tile_cols(hh)
            o_ref[0, :, cols] = (_dot(p_ref[rows, _tile_cols(0)], vp_ref[0, 0, :, cols])
                                 + _dot(p_ref[rows, _tile_cols(1)], vc_ref[0, 0, :, cols]))
            lse_ref[0, :, cols] = jnp.broadcast_to(lse[rows], (ATTN_BLK, HEAD_DIM))

    def spec(w, shift):
        return pl.BlockSpec((1, 1, ATTN_BLK, GROUP_W), lambda r, n: (w, r, jnp.maximum(n + shift, 0), 0))

    out = pl.BlockSpec((1, ATTN_BLK, GROUP_W), lambda r, n: (r, n, 0))
    bias = _band_bias(g, d, ("prev", "cur"))
    return pl.pallas_call(
        body, name=f"attn_fwd_{g}", grid=(d, nb),
        in_specs=[spec(0, 0), spec(1, 0), spec(1, -1), spec(2, 0), spec(2, -1),
                  pl.BlockSpec(bias.shape, lambda r, n: (0, 0))],
        out_specs=[out, out], out_shape=[jax.ShapeDtypeStruct((d, L, GROUP_W), F32)] * 2,
        scratch_shapes=[pltpu.VMEM((GROUP_ROWS, 2 * ATTN_BLK), F32), pltpu.VMEM((GROUP_ROWS, 2 * ATTN_BLK), BF16)],
        compiler_params=_params(("parallel", "parallel")))(qkv, qkv, qkv, qkv, qkv, bias)


def _attn_bwd(qkv, do, lse, cc, g, d, side=None):
    _, _, L, _ = qkv.shape
    nb = L // ATTN_BLK
    scale = HEAD_DIM ** -0.5
    a_, b_, c_ = _tile_cols(0), _tile_cols(1), _tile_cols(2)

    def body(q0_ref, q1_ref, k0_ref, kp_ref, v0_ref, vp_ref, do0_ref, do1_ref, l0_ref, l1_ref, c0_ref, c1_ref,
             bias_ref, o_ref, s_ref, dp_ref, l_ref, c_ref, p_ref, ds_ref):
        n = pl.program_id(1)
        for hh in range(HEADS_PER_GROUP):
            cols, rows = _tile_cols(hh), _tile_cols(hh)
            q0, q1 = q0_ref[0, 0, :, cols], q1_ref[0, 0, :, cols]
            k0, kp = k0_ref[0, 0, :, cols], kp_ref[0, 0, :, cols]
            v0, vp = v0_ref[0, 0, :, cols], vp_ref[0, 0, :, cols]
            do0, do1 = do0_ref[0, :, cols], do1_ref[0, :, cols]
            s_ref[rows, a_], s_ref[rows, b_], s_ref[rows, c_] = _dot_nt(q0, k0), _dot_nt(q0, kp), _dot_nt(q1, k0)
            dp_ref[rows, a_], dp_ref[rows, b_], dp_ref[rows, c_] = _dot_nt(do0, v0), _dot_nt(do0, vp), _dot_nt(do1, v0)
            l_ref[rows, a_], l_ref[rows, b_], l_ref[rows, c_] = l0_ref[0, :, cols], l0_ref[0, :, cols], l1_ref[0, :, cols]
            c_ref[rows, a_], c_ref[rows, b_], c_ref[rows, c_] = c0_ref[0, :, cols], c0_ref[0, :, cols], c1_ref[0, :, cols]
        col = lax.broadcasted_iota(jnp.int32, (GROUP_ROWS, 3 * ATTN_BLK), 1)
        tile = col // ATTN_BLK
        gone = jnp.logical_or(jnp.logical_and(tile == 1, n == 0), jnp.logical_and(tile == 2, n == nb - 1))
        s = jnp.where(gone, NEG_BIG, s_ref[...] * scale + bias_ref[...])
        p = jnp.exp(s - l_ref[...])
        p_ref[...] = p.astype(BF16)
        ds_ref[...] = (p * (dp_ref[...] + c_ref[...])).astype(BF16)
        for hh in range(HEADS_PER_GROUP):
            cols, rows = _tile_cols(hh), _tile_cols(hh)
            q0, q1 = q0_ref[0, 0, :, cols], q1_ref[0, 0, :, cols]
            k0, kp = k0_ref[0, 0, :, cols], kp_ref[0, 0, :, cols]
            do0, do1 = do0_ref[0, :, cols], do1_ref[0, :, cols]
            o_ref[0, 0, :, cols] = ((_dot(ds_ref[rows, a_], k0) + _dot(ds_ref[rows, b_], kp)) * scale).astype(BF16)
            o_ref[1, 0, :, cols] = ((_dot_tn(ds_ref[rows, a_], q0) + _dot_tn(ds_ref[rows, c_], q1)) * scale).astype(BF16)
            o_ref[2, 0, :, cols] = (_dot_tn(p_ref[rows, a_], do0) + _dot_tn(p_ref[rows, c_], do1)).astype(BF16)

    def spec(w, shift):
        return pl.BlockSpec((1, 1, ATTN_BLK, GROUP_W), lambda r, n: (w, r, jnp.clip(n + shift, 0, nb - 1), 0))

    def spec3(shift):
        return pl.BlockSpec((1, ATTN_BLK, GROUP_W), lambda r, n: (r, jnp.clip(n + shift, 0, nb - 1), 0))

    bias = _band_bias(g, d, ("cur", "prev", "prev"))
    wide = (GROUP_ROWS, 3 * ATTN_BLK)
    outs, carried = _call(
        body, name=f"attn_bwd_{g}", grid=(d, nb),
        in_specs=[spec(0, 0), spec(0, 1), spec(1, 0), spec(1, -1), spec(2, 0), spec(2, -1),
                  spec3(0), spec3(1), spec3(0), spec3(1), spec3(0), spec3(1), pl.BlockSpec(wide, lambda r, n: (0, 0))],
        out_specs=[pl.BlockSpec((3, 1, ATTN_BLK, GROUP_W), lambda r, n: (0, r, n, 0))],
        out_shape=[jax.ShapeDtypeStruct((3, d, L, GROUP_W), BF16)],
        scratch_shapes=[pltpu.VMEM(wide, F32)] * 4 + [pltpu.VMEM(wide, BF16)] * 2, semantics=("parallel", "parallel"),
        args=[qkv, qkv, qkv, qkv, qkv, qkv, do, do, lse, lse, cc, cc, bias], side=side)
    return outs[0] if side is None else (outs[0], carried)


def _load_natural(refs, nat_refs):
    for g, d in enumerate(ATTN_DILATIONS):
        if d == 1:
            nat_refs[g][...] = refs[g][0]
        else:
            per = ATTN_ROWS // d
            for r in range(d):
                nat_refs[g][pl.ds(r, per, stride=d), :] = refs[g][r]


def _mix_weights(lse_nat):
    l0, l1, l2 = lse_nat[0][...], lse_nat[1][...], lse_nat[2][...]
    m = jnp.maximum(jnp.maximum(l0, l1), l2)
    e0, e1, e2 = jnp.exp(l0 - m), jnp.exp(l1 - m), jnp.exp(l2 - m)
    inv = 1.0 / (e0 + e1 + e2)
    return e0 * inv, e1 * inv, e2 * inv


def _dilated_specs(s):
    return [pl.BlockSpec((d, ATTN_ROWS // d, HEAD_DIM), lambda i, h: (0, i, h)) for d in ATTN_DILATIONS]


NATURAL_SCRATCH = [pltpu.VMEM((ATTN_ROWS, HEAD_DIM), F32)] * (2 * len(ATTN_DILATIONS))


def _attn_merge(outs, lses):
    s = outs[0].shape[0] * outs[0].shape[1]

    def body(o0, o1, o2, l0, l1, l2, a_ref, *nat):
        onat, lnat = nat[:3], nat[3:]
        _load_natural((o0, o1, o2), onat)
        _load_natural((l0, l1, l2), lnat)
        w0, w1, w2 = _mix_weights(lnat)
        a_ref[...] = (w0 * onat[0][...] + w1 * onat[1][...] + w2 * onat[2][...]).astype(BF16)

    return pl.pallas_call(
        body, name="attn_merge", grid=(s // ATTN_ROWS, HEADS_PER_GROUP), in_specs=_dilated_specs(s) * 2,
        out_specs=pl.BlockSpec((ATTN_ROWS, HEAD_DIM), lambda i, h: (i, h)),
        out_shape=jax.ShapeDtypeStruct((s, GROUP_W), BF16), scratch_shapes=NATURAL_SCRATCH,
        compiler_params=_params(("parallel", "parallel")))(*outs, *lses)


def _attn_merge_bwd(outs, lses, dattn):
    s = dattn.shape[0]

    def body(o0, o1, o2, l0, l1, l2, da_ref, do0, do1, do2, c0, c1, c2, *nat):
        onat, lnat = nat[:3], nat[3:]
        _load_natural((o0, o1, o2), onat)
        _load_natural((l0, l1, l2), lnat)
        ws = _mix_weights(lnat)
        da = da_ref[...]
        attn = ws[0] * onat[0][...] + ws[1] * onat[1][...] + ws[2] * onat[2][...]
        tot = jnp.broadcast_to(jnp.sum(da * attn, axis=-1, keepdims=True), (ATTN_ROWS, HEAD_DIM))
        for g, (d, do_ref, c_ref) in enumerate(zip(ATTN_DILATIONS, (do0, do1, do2), (c0, c1, c2))):
            if d == 1:
                do_ref[0] = (ws[g] * da).astype(BF16)
                c_ref[0] = -ws[g] * tot
            else:
                onat[g][...] = ws[g] * da
                lnat[g][...] = -ws[g] * tot
                per = ATTN_ROWS // d
                for r in range(d):
                    do_ref[r] = onat[g][pl.ds(r, per, stride=d), :].astype(BF16)
                    c_ref[r] = lnat[g][pl.ds(r, per, stride=d), :]

    dil = _dilated_specs(s)
    shapes = [jax.ShapeDtypeStruct(o.shape, BF16) for o in outs] + [jax.ShapeDtypeStruct(o.shape, F32) for o in outs]
    return pl.pallas_call(
        body, name="attn_merge_bwd", grid=(s // ATTN_ROWS, HEADS_PER_GROUP),
        in_specs=dil * 2 + [pl.BlockSpec((ATTN_ROWS, HEAD_DIM), lambda i, h: (i, h))], out_specs=dil * 2,
        out_shape=shapes, scratch_shapes=NATURAL_SCRATCH,
        compiler_params=_params(("parallel", "parallel")))(*outs, *lses, dattn)


def _ssm_prepare(a_re, a_im, log_dt, b_re, b_im, c_re, c_im):
    n_g = a_re.shape[0]
    nj = n_g * SSM_GROUP // SSM_TILE_CH
    gpt = SSM_TILE_CH // SSM_GROUP
    dt = jnp.exp(log_dt)[:, None]
    mag = jnp.exp(a_re * dt)
    lr, li = mag * jnp.cos(a_im * dt), mag * jnp.sin(a_im * dt)
    den = a_re * a_re + a_im * a_im
    cr = ((lr - 1.0) * a_re + li * a_im) / den
    ci = (li * a_re - (lr - 1.0) * a_im) / den
    bb_re = cr[..., None] * b_re - ci[..., None] * b_im
    bb_im = cr[..., None] * b_im + ci[..., None] * b_re
    eye = jnp.eye(gpt, dtype=F32)

    def b_tiles(t):
        t = t.transpose(0, 2, 1).reshape(nj, gpt, SSM_GROUP, SSM_STATE)
        return jnp.einsum("jgcp,gh->jgchp", t, eye).reshape(nj, SSM_TILE_CH, SSM_TILE_ST)

    def c_tiles(t):
        t = t.reshape(nj, gpt, SSM_GROUP, SSM_STATE)
        return jnp.einsum("jgcp,gh->jhpgc", t, eye).reshape(nj, SSM_TILE_ST, SSM_TILE_CH)

    lam = jnp.stack([lr.reshape(-1), li.reshape(-1)])
    bmat = jnp.concatenate([b_tiles(bb_re), b_tiles(bb_im)], axis=2)
    cmat = jnp.concatenate([c_tiles(c_re), -c_tiles(c_im)], axis=1)
    return lam, bmat, cmat


SSM_SEGMENTS = 8


def _to_segment_order(nat, perm_ref):
    per = nat.shape[0] // SSM_SEGMENTS
    for i in range(SSM_SEGMENTS):
        perm_ref[pl.ds(i, per, stride=SSM_SEGMENTS), :] = nat[i * per:(i + 1) * per, :]
    return perm_ref[...]


def _to_time_order(val, perm_ref, store):
    per = val.shape[0] // SSM_SEGMENTS
    perm_ref[...] = val
    for i in range(SSM_SEGMENTS):
        store(i, perm_ref[pl.ds(i, per, stride=SSM_SEGMENTS), :])


def _fill_powers(lam_ref, w_ref, nj, tau_n):
    for j in range(nj):
        st = slice(j * SSM_TILE_ST, (j + 1) * SSM_TILE_ST)
        lr = jnp.broadcast_to(lam_ref[0:1, st], (SSM_SEGMENTS, SSM_TILE_ST))
        li = jnp.broadcast_to(lam_ref[1:2, st], (SSM_SEGMENTS, SSM_TILE_ST))
        wr, wi = lr, li
        for tau in range(tau_n):
            rows = slice(tau * SSM_SEGMENTS, (tau + 1) * SSM_SEGMENTS)
            w_ref[j, rows, :SSM_TILE_ST] = wr
            w_ref[j, rows, SSM_TILE_ST:] = wi
            wr, wi = wr * lr - wi * li, wr * li + wi * lr


def _segment_scan(src, xs_ref, w_tile, lr, li, cr, ci, conj, reverse):
    seg, half = SSM_SEGMENTS, SSM_TILE_ST
    tau_n = src.shape[0] // seg
    sgn = -1.0 if conj else 1.0
    lr8 = jnp.broadcast_to(lr, (seg, half))
    li8 = jnp.broadcast_to(li, (seg, half)) * sgn
    xr = jnp.zeros((seg, half), F32)
    xi = jnp.zeros((seg, half), F32)
    order = range(tau_n - 1, -1, -1) if reverse else range(tau_n)
    for tau in order:
        rows = slice(tau * seg, (tau + 1) * seg)
        xr, xi = lr8 * xr - li8 * xi + src[rows, :half], lr8 * xi + li8 * xr + src[rows, half:]
        xs_ref[rows, :half] = xr
        xs_ref[rows, half:] = xi
    pr = w_tile[(tau_n - 1) * seg:(tau_n - 1) * seg + 1, :half]
    pi = w_tile[(tau_n - 1) * seg:(tau_n - 1) * seg + 1, half:] * sgn
    fr, fi = cr, ci
    ins_r, ins_i = [None] * seg, [None] * seg
    runs = range(seg - 1, -1, -1) if reverse else range(seg)
    for i in runs:
        ins_r[i], ins_i[i] = fr, fi
        fr, fi = xr[i:i + 1, :] + pr * fr - pi * fi, xi[i:i + 1, :] + pr * fi + pi * fr
    in_r = jnp.concatenate(ins_r, axis=0)
    in_i = jnp.concatenate(ins_i, axis=0)
    for tau in range(tau_n):
        rows = slice(tau * seg, (tau + 1) * seg)
        wrow = (tau_n - 1 - tau) if reverse else tau
        wr = w_tile[wrow * seg:(wrow + 1) * seg, :half]
        wi = w_tile[wrow * seg:(wrow + 1) * seg, half:] * sgn
        xs_ref[rows, :half] += wr * in_r - wi * in_i
        xs_ref[rows, half:] += wr * in_i + wi * in_r
    return (fr, fi), (in_r, in_i)


def _ssm_dims(z, bmat, u_off):
    s = z.shape[0]
    nj = bmat.shape[0]
    t_rows = _pick(s, (256, 128))
    return s, nj, nj * SSM_TILE_CH, nj * SSM_TILE_ST, t_rows


def _ssm_fwd(z, bmat, cmat, lam, dskip, u_off, side=None):
    s, nj, w, ns, t_rows = _ssm_dims(z, bmat, u_off)
    per = t_rows // SSM_SEGMENTS

    def body(*refs):
        u_refs = refs[:nj]
        b_ref, c_ref, lam_ref, d_ref, y_ref, yg_ref, xin_ref, carry_ref, w_ref, xs_ref, perm_ref = refs[nj:]

        @pl.when(pl.program_id(0) == 0)
        def _():
            carry_ref[...] = jnp.zeros_like(carry_ref)
            _fill_powers(lam_ref, w_ref, nj, per)

        xin_ref[0] = carry_ref[...]
        for j in range(nj):
            st = slice(j * SSM_TILE_ST, (j + 1) * SSM_TILE_ST)
            ch = slice(j * SSM_TILE_CH, (j + 1) * SSM_TILE_CH)
            up = _to_segment_order(u_refs[j], perm_ref)
            bu = _dot(up.astype(BF16), b_ref[j])
            (fr, fi), _ = _segment_scan(bu, xs_ref, w_ref.at[j], lam_ref[0:1, st], lam_ref[1:2, st],
                                        carry_ref[0:1, st], carry_ref[1:2, st], conj=False, reverse=False)
            carry_ref[0:1, st] = fr
            carry_ref[1:2, st] = fi
            yp = _dot(xs_ref[...].astype(BF16), c_ref[j]) + d_ref[:, ch] * up

            def store(i, rows, ch=ch):
                y_ref[i * per:(i + 1) * per, ch] = rows
                yg_ref[i * per:(i + 1) * per, ch] = _gelu(rows).astype(BF16)

            _to_time_order(yp, perm_ref, store)

    u_specs = [pl.BlockSpec((t_rows, SSM_TILE_CH), lambda c, k=k: (c, u_off // SSM_TILE_CH + k)) for k in range(nj)]
    full3 = lambda shape: pl.BlockSpec(shape, lambda c: (0, 0, 0))
    full2 = lambda shape: pl.BlockSpec(shape, lambda c: (0, 0))
    rows = pl.BlockSpec((t_rows, w), lambda c: (c, 0))
    outs, carried = _call(
        body, name="ssm_fwd", grid=(s // t_rows,),
        in_specs=u_specs + [full3(bmat.shape), full3(cmat.shape), full2(lam.shape), full2(dskip.shape)],
        out_specs=[rows, rows, pl.BlockSpec((1, 2, ns), lambda c: (c, 0, 0))],
        out_shape=[jax.ShapeDtypeStruct((s, w), F32), jax.ShapeDtypeStruct((s, w), BF16),
                   jax.ShapeDtypeStruct((s // t_rows, 2, ns), F32)],
        scratch_shapes=[pltpu.VMEM((2, ns), F32), pltpu.VMEM((nj, t_rows, 2 * SSM_TILE_ST), F32),
                        pltpu.VMEM((t_rows, 2 * SSM_TILE_ST), F32), pltpu.VMEM((t_rows, SSM_TILE_CH), F32)],
        semantics=("arbitrary",), args=[*([z] * nj), bmat, cmat, lam, dskip], side=side)
    return outs if side is None else (outs, carried)


def _ssm_bwd(z, y, dyg, xin, bmat, cmat, lam, dskip, u_off, side=None):
    s, nj, w, ns, t_rows = _ssm_dims(z, bmat, u_off)
    nc = s // t_rows
    per = t_rows // SSM_SEGMENTS
    seg, half = SSM_SEGMENTS, SSM_TILE_ST

    def body(*refs):
        u_refs = refs[:nj]
        (y_ref, dyg_ref, xin_ref, b_ref, c_ref, lam_ref, d_ref, du_ref, db_ref, dc_ref, dlam_ref, dd_ref,
         carry_ref, w_ref, xs_ref, gs_ref, perm_ref, acc_ref) = refs[nj:]

        @pl.when(pl.program_id(0) == 0)
        def _():
            carry_ref[...] = jnp.zeros_like(carry_ref)
            db_ref[...] = jnp.zeros_like(db_ref)
            dc_ref[...] = jnp.zeros_like(dc_ref)
            dd_ref[...] = jnp.zeros_like(dd_ref)
            acc_ref[...] = jnp.zeros_like(acc_ref)
            _fill_powers(lam_ref, w_ref, nj, per)

        for j in range(nj):
            st = slice(j * SSM_TILE_ST, (j + 1) * SSM_TILE_ST)
            ch = slice(j * SSM_TILE_CH, (j + 1) * SSM_TILE_CH)
            lr, li = lam_ref[0:1, st], lam_ref[1:2, st]
            up = _to_segment_order(u_refs[j], perm_ref)
            upb = up.astype(BF16)
            dyp = _to_segment_order(dyg_ref[:, ch] * _gelu_grad(y_ref[:, ch]), perm_ref)
            dyb = dyp.astype(BF16)
            _, (in_r, in_i) = _segment_scan(_dot(upb, b_ref[j]), xs_ref, w_ref.at[j], lr, li,
                                            xin_ref[0, 0:1, st], xin_ref[0, 1:2, st], conj=False, reverse=False)
            (gr, gi), _ = _segment_scan(_dot_nt(dyb, c_ref[j]), gs_ref, w_ref.at[j], lr, li,
                                        carry_ref[0:1, st], carry_ref[1:2, st], conj=True, reverse=True)
            carry_ref[0:1, st] = gr
            carry_ref[1:2, st] = gi
            xs, gs = xs_ref[...], gs_ref[...]
            xsr, xsi, gsr, gsi = xs[:, :half], xs[:, half:], gs[:, :half], gs[:, half:]
            pxr = jnp.concatenate([in_r, xsr[:t_rows - seg]], axis=0)
            pxi = jnp.concatenate([in_i, xsi[:t_rows - seg]], axis=0)
            dl_r = gsr * pxr + gsi * pxi
            dl_i = gsi * pxr - gsr * pxi
            acc_ref[0, :, st] += jnp.sum(dl_r.reshape(per, seg, half), axis=0)
            acc_ref[1, :, st] += jnp.sum(dl_i.reshape(per, seg, half), axis=0)
            gx = gs.astype(BF16)
            dup = _dot_nt(gx, b_ref[j]) + d_ref[:, ch] * dyp

            def store(i, rows, ch=ch):
                du_ref[i * per:(i + 1) * per, ch] = rows.astype(BF16)

            _to_time_order(dup, perm_ref, store)
            db_ref[j] += _dot_tn(upb, gx)
            dc_ref[j] += _dot_tn(xs.astype(BF16), dyb)
            dd_ref[:, ch] += jnp.sum(dyp * up, axis=0, keepdims=True)

        @pl.when(pl.program_id(0) == nc - 1)
        def _():
            dlam_ref[...] = jnp.sum(acc_ref[...], axis=1)

    rev = lambda c: nc - 1 - c
    u_specs = [pl.BlockSpec((t_rows, SSM_TILE_CH), lambda c, k=k: (rev(c), u_off // SSM_TILE_CH + k))
               for k in range(nj)]
    full3 = lambda shape: pl.BlockSpec(shape, lambda c: (0, 0, 0))
    full2 = lambda shape: pl.BlockSpec(shape, lambda c: (0, 0))
    rows = pl.BlockSpec((t_rows, w), lambda c: (rev(c), 0))
    outs, carried = _call(
        body, name="ssm_bwd", grid=(nc,),
        in_specs=u_specs + [rows, rows, pl.BlockSpec((1, 2, ns), lambda c: (rev(c), 0, 0)),
                            full3(bmat.shape), full3(cmat.shape), full2(lam.shape), full2(dskip.shape)],
        out_specs=[rows, full3(bmat.shape), full3(cmat.shape), full2(lam.shape), full2(dskip.shape)],
        out_shape=[jax.ShapeDtypeStruct((s, w), BF16), jax.ShapeDtypeStruct(bmat.shape, F32),
                   jax.ShapeDtypeStruct(cmat.shape, F32), jax.ShapeDtypeStruct(lam.shape, F32),
                   jax.ShapeDtypeStruct(dskip.shape, F32)],
        scratch_shapes=[pltpu.VMEM((2, ns), F32), pltpu.VMEM((nj, t_rows, 2 * SSM_TILE_ST), F32),
                        pltpu.VMEM((t_rows, 2 * SSM_TILE_ST), F32), pltpu.VMEM((t_rows, 2 * SSM_TILE_ST), F32),
                        pltpu.VMEM((t_rows, SSM_TILE_CH), F32), pltpu.VMEM((2, SSM_SEGMENTS, ns), F32)],
        semantics=("arbitrary",), args=[*([z] * nj), y, dyg, xin, bmat, cmat, lam, dskip], side=side)
    return outs if side is None else (outs, carried)


def _adam_math(w, g, m, v):
    m = ADAM_B1 * m + (1.0 - ADAM_B1) * g
    v = ADAM_B2 * v + (1.0 - ADAM_B2) * (g * g)
    m_hat = m / (1.0 - ADAM_B1 ** ADAM_STEP)
    v_hat = v / (1.0 - ADAM_B2 ** ADAM_STEP)
    delta = -ADAM_LR * (m_hat / (jnp.sqrt(v_hat) + ADAM_EPS) + ADAM_WD * w)
    return delta, m, v


def _adam_rows(r, c):
    for tr in (512, 256, 128, 64, 32, 16, 8):
        if r % tr == 0 and tr * c * 4 <= (1 << 20):
            return tr
    return r


def _adamw_big(w, p_mine, p_sib, m, v, name):
    r, c = w.shape
    tr = _adam_rows(r, c)

    def body(w_ref, a_ref, b_ref, m_ref, v_ref, g_ref, d_ref, nm_ref, nv_ref):
        g = a_ref[...] + b_ref[...]
        g_ref[...] = g
        d_ref[...], nm_ref[...], nv_ref[...] = _adam_math(w_ref[...], g, m_ref[...], v_ref[...])

    blk = pl.BlockSpec((tr, c), lambda i: (i, 0))
    return pl.pallas_call(body, name=f"adamw_{name}", grid=(r // tr,), in_specs=[blk] * 5, out_specs=[blk] * 4,
                          out_shape=[jax.ShapeDtypeStruct((r, c), F32)] * 4,
                          compiler_params=_params(("parallel",)))(w, p_mine, p_sib, m, v)


def _adamw_small(w, parts, m, v):
    r, c = w.shape
    n_dev = parts.shape[0]

    def body(w_ref, p_ref, m_ref, v_ref, g_ref, d_ref, nm_ref, nv_ref):
        g = p_ref[0]
        for k in range(1, n_dev):
            g = g + p_ref[k]
        g_ref[...] = g
        d_ref[...], nm_ref[...], nv_ref[...] = _adam_math(w_ref[...], g, m_ref[...], v_ref[...])

    blk = pl.BlockSpec((r, c), lambda i: (0, 0))
    return pl.pallas_call(body, name="adamw_small", grid=(1,),
                          in_specs=[blk, pl.BlockSpec((n_dev, r, c), lambda i: (0, 0, 0)), blk, blk],
                          out_specs=[blk] * 4, out_shape=[jax.ShapeDtypeStruct((r, c), F32)] * 4,
                          compiler_params=_params(("arbitrary",)))(w, parts, m, v)


def _cast_bf16(w, name):
    r, c = w.shape
    tr = _adam_rows(r, c)

    def body(w_ref, o_ref):
        o_ref[...] = w_ref[...].astype(BF16)

    blk = pl.BlockSpec((tr, c), lambda i: (i, 0))
    return pl.pallas_call(body, name=f"cast_{name}", grid=(r // tr,), in_specs=[blk], out_specs=blk,
                          out_shape=jax.ShapeDtypeStruct((r, c), BF16), compiler_params=_params(("parallel",)))(w)


def _sum_slots(recv, name):
    _, r, c = recv.shape
    tr = _adam_rows(r, c)

    def body(p_ref, o_ref):
        acc = p_ref[0].astype(F32)
        for k in range(1, N_CHIPS):
            acc = acc + p_ref[k].astype(F32)
        o_ref[...] = acc

    return pl.pallas_call(body, name=f"sum_{name}", grid=(r // tr,),
                          in_specs=[pl.BlockSpec((N_CHIPS, tr, c), lambda i: (0, i, 0))],
                          out_specs=pl.BlockSpec((tr, c), lambda i: (i, 0)),
                          out_shape=jax.ShapeDtypeStruct((r, c), F32), compiler_params=_params(("parallel",)))(recv)


BIG_WEIGHTS = ("w_in", "w_attn_up", "w_glu_v", "w_glu_g", "w_out", "w_ffn_gate", "w_ffn_up", "w_ffn_down")
COL_SHARDED = ("w_in", "w_attn_up", "w_glu_v", "w_glu_g", "w_ffn_gate", "w_ffn_up")


def _aligned(v, m):
    return v if isinstance(v, int) else pl.multiple_of(v, m)


def _shard_of(ref, name, j, shard_shape, half=None):
    r, c = shard_shape
    rows = r if half is None else r // 2
    row0 = 0 if half is None else half * rows
    if name in COL_SHARDED:
        return ref.at[pl.ds(_aligned(row0, 16), rows), pl.ds(_aligned(j * c, 128), c)]
    return ref.at[pl.ds(_aligned(j * r + row0, 16), rows), :]


def _other_chips():
    x, y = lax.axis_index("x"), lax.axis_index("y")
    return [(1 - x, y), (x, 1 - y), (1 - x, 1 - y)]


def _dma_sems(n, arrays):
    return [pltpu.SemaphoreType.DMA((n, 3))] * arrays + [pltpu.SemaphoreType.DMA((n,))]


def _gather_side(shards):
    names = list(shards)
    n = len(names)
    full_shapes = []
    for k in names:
        r, c = shards[k].shape
        full_shapes.append((r, c * N_CHIPS) if k in COL_SHARDED else (r * N_CHIPS, c))

    def build(src, dst, sems):
        send_sems, recv_sems, pass_send_sems, pass_recv_sems, local_sems = sems
        x, y, c = lax.axis_index("x"), lax.axis_index("y"), lax.axis_index("c")
        me = 2 * x + y
        locals_, sends, arrivals, forwards, passed_on = [], [], [], [], []
        for i, k in enumerate(names):
            shape = shards[k].shape
            half_rows = shape[0] // 2
            locals_.append(pltpu.make_async_copy(src[i], _shard_of(dst[i], k, me, shape), local_sems.at[i]))
            my_half = src[i].at[pl.ds(_aligned(c * half_rows, 16), half_rows), :]
            for p, (px, py) in enumerate(_other_chips()):
                peer = 2 * px + py
                landed = _shard_of(dst[i], k, peer, shape, half=c)
                sends.append(pltpu.make_async_remote_copy(
                    src_ref=my_half, dst_ref=_shard_of(dst[i], k, me, shape, half=c), send_sem=send_sems.at[i, p],
                    recv_sem=recv_sems.at[i, p], device_id=(px, py, c), device_id_type=MESH))
                arrivals.append(pltpu.make_async_remote_copy(
                    src_ref=my_half, dst_ref=landed, send_sem=send_sems.at[i, p],
                    recv_sem=recv_sems.at[i, p], device_id=(px, py, c), device_id_type=MESH))
                forwards.append(pltpu.make_async_remote_copy(
                    src_ref=landed, dst_ref=landed, send_sem=pass_send_sems.at[i, p],
                    recv_sem=pass_recv_sems.at[i, p], device_id=(x, y, 1 - c), device_id_type=MESH))
                passed_on.append(pltpu.make_async_remote_copy(
                    src_ref=landed, dst_ref=_shard_of(dst[i], k, peer, shape, half=1 - c),
                    send_sem=pass_send_sems.at[i, p], recv_sem=pass_recv_sems.at[i, p],
                    device_id=(x, y, 1 - c), device_id_type=MESH))
        return locals_, sends, arrivals, forwards, passed_on

    return _Side([shards[k] for k in names], [jax.ShapeDtypeStruct(s, BF16) for s in full_shapes], _dma_sems(n, 4), build)


def _scatter_side(grads, shard_shapes):
    names = list(grads)
    n = len(names)

    def build(src, dst, sems):
        send_sems, recv_sems, local_sems = sems
        x, y, c = lax.axis_index("x"), lax.axis_index("y"), lax.axis_index("c")
        me = 2 * x + y
        locals_, sends, arrivals = [], [], []
        for i, k in enumerate(names):
            shape = shard_shapes[k]
            locals_.append(pltpu.make_async_copy(_shard_of(src[i], k, me, shape), dst[i].at[me], local_sems.at[i]))
            for p, (px, py) in enumerate(_other_chips()):
                peer = 2 * px + py
                sends.append(pltpu.make_async_remote_copy(
                    src_ref=_shard_of(src[i], k, peer, shape), dst_ref=dst[i].at[me], send_sem=send_sems.at[i, p],
                    recv_sem=recv_sems.at[i, p], device_id=(px, py, c), device_id_type=MESH))
                arrivals.append(pltpu.make_async_remote_copy(
                    src_ref=_shard_of(src[i], k, peer, shape), dst_ref=dst[i].at[peer], send_sem=send_sems.at[i, p],
                    recv_sem=recv_sems.at[i, p], device_id=(px, py, c), device_id_type=MESH))
        return locals_, sends, arrivals, [None] * len(arrivals), []

    return _Side([grads[k] for k in names],
                 [jax.ShapeDtypeStruct((N_CHIPS,) + tuple(shard_shapes[k]), BF16) for k in names], _dma_sems(n, 2), build)


def _put_side(dz, pieces):
    n = len(pieces)

    def build(src, dst, sems):
        copies = [pltpu.make_async_copy(src[1 + k], dst[0].at[:, pl.ds(off, piece.shape[1])], sems[0].at[k])
                  for k, (piece, off) in enumerate(pieces)]
        return copies, [], [], [], []

    return _Side([dz] + [p for p, _ in pieces], [jax.ShapeDtypeStruct(dz.shape, dz.dtype)],
                 [pltpu.SemaphoreType.DMA((n,))], build, aliases={0: 0})


def _swap_side(parts):
    n = len(parts)

    def build(src, dst, sems):
        send_sems, recv_sems = sems
        sibling = (lax.axis_index("x"), lax.axis_index("y"), 1 - lax.axis_index("c"))
        copies = [pltpu.make_async_remote_copy(src_ref=src[i], dst_ref=dst[i], send_sem=send_sems.at[i],
                                               recv_sem=recv_sems.at[i], device_id=sibling, device_id_type=MESH)
                  for i in range(n)]
        return [], copies, copies, [None] * n, []

    return _Side(parts, [jax.ShapeDtypeStruct(p.shape, F32) for p in parts],
                 [pltpu.SemaphoreType.DMA((n,)), pltpu.SemaphoreType.DMA((n,))], build)


def _share_side(packed):
    r, c = packed.shape

    def build(src, dst, sems):
        send_sems, recv_sems, local_sem = sems
        x, y, cc = lax.axis_index("x"), lax.axis_index("y"), lax.axis_index("c")
        me = 4 * x + 2 * y + cc
        own = pltpu.make_async_copy(src[0], dst[0].at[me], local_sem)
        sends, arrivals = [], []
        flips = [(fx, fy, fc) for fx in range(2) for fy in range(2) for fc in range(2) if fx or fy or fc]
        for p, (fx, fy, fc) in enumerate(flips):
            px, py, pc = x ^ fx, y ^ fy, cc ^ fc
            sends.append(pltpu.make_async_remote_copy(
                src_ref=src[0], dst_ref=dst[0].at[me], send_sem=send_sems.at[p], recv_sem=recv_sems.at[p],
                device_id=(px, py, pc), device_id_type=MESH))
            arrivals.append(pltpu.make_async_remote_copy(
                src_ref=src[0], dst_ref=dst[0].at[4 * px + 2 * py + pc], send_sem=send_sems.at[p],
                recv_sem=recv_sems.at[p], device_id=(px, py, pc), device_id_type=MESH))
        return [own], sends, arrivals, [None] * len(arrivals), []

    return _Side([packed], [jax.ShapeDtypeStruct((8, r, c), F32)],
                 [pltpu.SemaphoreType.DMA((7,)), pltpu.SemaphoreType.DMA((7,)), pltpu.SemaphoreType.DMA], build)


SMALL_WEIGHTS = ("norm_mix_pre", "ssm_a_re", "ssm_a_im", "ssm_log_dt", "ssm_b_re", "ssm_b_im", "ssm_c_re", "ssm_c_im",
                 "ssm_d", "norm_mix_post", "norm_ffn_pre", "norm_ffn_post")
WEIGHT_ORDER = ("norm_mix_pre", "w_in", "w_attn_up", "ssm_a_re", "ssm_a_im", "ssm_log_dt", "ssm_b_re", "ssm_b_im",
                "ssm_c_re", "ssm_c_im", "ssm_d", "w_glu_v", "w_glu_g", "w_out", "norm_mix_post", "norm_ffn_pre",
                "w_ffn_gate", "w_ffn_up", "w_ffn_down", "norm_ffn_post")
PACK_LANES = 128
PACK_ROWS = 8
PACK_GROUPS = (SMALL_WEIGHTS[:1], SMALL_WEIGHTS[1:])


def _pack_group(arrs, names):
    flat = jnp.concatenate([arrs[k].reshape(-1) for k in names])
    pad = -flat.shape[0] % (PACK_LANES * PACK_ROWS)
    return jnp.pad(flat, (0, pad)).reshape(-1, PACK_LANES)


def _pack_small(arrs):
    return jnp.concatenate([_pack_group(arrs, names) for names in PACK_GROUPS], axis=0)


def _unpack_small(packed, like):
    out, row = {}, 0
    for names in PACK_GROUPS:
        rows = _pack_group(like, names).shape[0]
        flat, pos = packed[row:row + rows].reshape(-1), 0
        for k in names:
            n = like[k].size
            out[k] = flat[pos:pos + n].reshape(like[k].shape)
            pos += n
        row += rows
    return out


def _local_step(x, target, big, small, shards=None, shard_shapes=None):
    s, d = x.shape
    big, grads, slots = dict(big), {}, {}
    carry = shards is not None

    def gathering(names, call):
        if not carry:
            return call(None)
        res, got = call(_gather_side({k: shards[k] for k in names}))
        big.update(zip(names, got))
        return res

    def scattering(names, call):
        if not carry:
            return call(None)
        res, got = call(_scatter_side({k: grads[k] for k in names}, shard_shapes))
        slots.update(zip(names, got))
        return res

    u_off = 3 * HQ
    gate_off = u_off + d // 2
    g1, g2, g3, g4 = (small[k][0:1] for k in ("norm_mix_pre", "norm_mix_post", "norm_ffn_pre", "norm_ffn_post"))
    ssm_names = ("ssm_a_re", "ssm_a_im", "ssm_log_dt", "ssm_b_re", "ssm_b_im", "ssm_c_re", "ssm_c_im")
    (lam, bmat, cmat), ssm_vjp = jax.vjp(_ssm_prepare, *[small[k][0] for k in ssm_names])
    bmat, cmat = bmat.astype(BF16), cmat.astype(BF16)
    dskip = small["ssm_d"][0:1]

    h1 = _norm_in(x, g1)
    z = gathering(("w_attn_up", "w_glu_v", "w_glu_g", "w_out", "w_ffn_gate"),
                  lambda side: _mm(h1, big["w_in"], "nn", F32, "in_proj", side=side))
    y, yg, xin = gathering(("w_ffn_up",), lambda side: _ssm_fwd(z, bmat, cmat, lam, dskip, u_off, side=side))
    qkv = [_dilate_qkv(z, g, dil) for g, dil in enumerate(ATTN_DILATIONS)]
    outs, lses = zip(*[_attn_fwd(qkv[g], g, dil) for g, dil in enumerate(ATTN_DILATIONS)])
    attn = _attn_merge(outs, lses)
    merged, ab, gv, gg = _mm_fused(
        [attn, yg], [big["w_attn_up"], big["w_glu_v"], big["w_glu_g"]], [(0, 0), (1, 1), (1, 2)], "nn",
        [BF16, BF16, BF16, BF16], "branches_merge", extras=[(z, gate_off), (z, gate_off + d)], epilogue=_gates_epilogue)
    mo = _mm(merged, big["w_out"], "nn", F32, "mix_out")
    x2, h2 = _norm_mid(x, mo, g2, g3)
    act, fg, fu = gathering(("w_ffn_down",), lambda side: _mm_fused(
        [h2], [big["w_ffn_gate"], big["w_ffn_up"]], [(0, 0), (0, 1)], "nn", [BF16, BF16, BF16], "ffn_up_act",
        epilogue=_swiglu_epilogue, side=side))
    f = _mm(act, big["w_ffn_down"], "nn", F32, "ffn_down")
    loss, dout, df, dg4 = _loss_head(x2, f, g4, target)

    dfg, dfu = _mm_fused([df], [big["w_ffn_down"]], [(0, 0)], "nt", [BF16, BF16], "d_ffn_act",
                         extras=[(fg, 0), (fu, 0)], epilogue=_swiglu_bwd_epilogue)
    grads["w_ffn_down"] = _mm_kloop(act, df, "tn", BF16, "dw_ffn_down")
    grads["w_ffn_gate"] = scattering(("w_ffn_down",), lambda side: _mm_kloop(h2, dfg, "tn", BF16, "dw_ffn_gate", side=side))
    grads["w_ffn_up"] = scattering(("w_ffn_gate",), lambda side: _mm_kloop(h2, dfu, "tn", BF16, "dw_ffn_up", side=side))
    dh2 = scattering(("w_ffn_up",), lambda side: _mm_kloop(dfg, big["w_ffn_gate"], "nt", F32, "d_h2_gate", side=side))
    dh2 = _mm_kloop(dfu, big["w_ffn_up"], "nt", F32, "d_h2_up", add=dh2)
    dx2, dmo, dg2, dg3 = _norm_mid_bwd(x2, mo, g2, g3, dout, dh2)
    dz, dgs, dab, dgv, dgg = _mm_fused(
        [dmo], [big["w_out"]], [(0, 0)], "nt", [BF16] * 5, "d_merged_gates",
        extras=[(z, gate_off), (z, gate_off + d), (ab, 0), (gv, 0), (gg, 0)], epilogue=_gates_bwd_epilogue,
        out_place=[(z.shape[1], gate_off), None, None, None, None])
    grads["w_out"], (dz,) = _mm_kloop(merged, dmo, "tn", BF16, "dw_out", side=_put_side(dz, [(dgs, gate_off + d)]))
    dyg = _mm_fused([dgv, dgg], [big["w_glu_v"], big["w_glu_g"]], [(0, 0), (1, 1)], "nt", [F32], "d_yg",
                    epilogue=_sum_epilogue)[0]
    grads["w_glu_v"] = _mm_kloop(yg, dgv, "tn", BF16, "dw_glu_v")
    grads["w_glu_g"] = _mm_kloop(yg, dgg, "tn", BF16, "dw_glu_g")
    du, dbmat, dcmat, dlam, dd = scattering(
        ("w_out", "w_glu_v", "w_glu_g"),
        lambda side: _ssm_bwd(z, y, dyg, xin, bmat, cmat, lam, dskip, u_off, side=side))
    dattn = _mm(dab, big["w_attn_up"], "nt", F32, "d_attn")
    grads["w_attn_up"], (dz,) = _mm_kloop(attn, dab, "tn", BF16, "dw_attn_up", side=_put_side(dz, [(du, u_off)]))
    merged_bwd = _attn_merge_bwd(outs, lses, dattn)
    mine, theirs = {}, {}
    for g, dil in enumerate(ATTN_DILATIONS):
        side = None
        if carry and g == 0:
            mine = {k: _sum_slots(slots[k], k) for k in slots}
            side = _swap_side(list(mine.values()))
        dqkv = _attn_bwd(qkv[g], merged_bwd[g], lses[g], merged_bwd[3 + g], g, dil, side=side)
        if side is not None:
            dqkv, got = dqkv
            theirs = dict(zip(mine, got))
        dz = _undilate_dqkv(dqkv, dz, g, dil)
    small_grads = dict(zip(ssm_names, (t[None] for t in ssm_vjp((dlam, dbmat, dcmat)))))
    small_grads.update(norm_mix_post=dg2, norm_ffn_pre=dg3, norm_ffn_post=dg4, ssm_d=dd)
    if carry:
        side = _join_sides(_scatter_side({"w_attn_up": grads["w_attn_up"]}, shard_shapes),
                           _share_side(_pack_group(small_grads, PACK_GROUPS[1])))
        grads["w_in"], (slots["w_attn_up"], shared) = _mm_kloop(h1, dz, "tn", BF16, "dw_in", side=side)
    else:
        grads["w_in"] = _mm_kloop(h1, dz, "tn", BF16, "dw_in")
    dh1 = scattering(("w_in",), lambda side: _mm_kloop(dz, big["w_in"], "nt", F32, "d_h1", side=side))
    grad_x, dg1 = _norm_in_bwd(x, g1, dh1, dx2)
    small_grads["norm_mix_pre"] = dg1
    if carry:
        return loss[0, 0], grad_x, (slots, mine, theirs), (dg1, shared)
    return loss[0, 0], grad_x, grads, small_grads


def kernel(x, norm_mix_pre, w_in, w_attn_up, ssm_a_re, ssm_a_im, ssm_log_dt, ssm_b_re, ssm_b_im, ssm_c_re, ssm_c_im, ssm_d, w_glu_v, w_glu_g, w_out, norm_mix_post, norm_ffn_pre, w_ffn_gate, w_ffn_up, w_ffn_down, norm_ffn_post, loss_target, m_norm_mix_pre, m_w_in, m_w_attn_up, m_ssm_a_re, m_ssm_a_im, m_ssm_log_dt, m_ssm_b_re, m_ssm_b_im, m_ssm_c_re, m_ssm_c_im, m_ssm_d, m_w_glu_v, m_w_glu_g, m_w_out, m_norm_mix_post, m_norm_ffn_pre, m_w_ffn_gate, m_w_ffn_up, m_w_ffn_down, m_norm_ffn_post, v_norm_mix_pre, v_w_in, v_w_attn_up, v_ssm_a_re, v_ssm_a_im, v_ssm_log_dt, v_ssm_b_re, v_ssm_b_im, v_ssm_c_re, v_ssm_c_im, v_ssm_d, v_w_glu_v, v_w_glu_g, v_w_out, v_norm_mix_post, v_norm_ffn_pre, v_w_ffn_gate, v_w_ffn_up, v_w_ffn_down, v_norm_ffn_post):
    given = dict(locals())
    w = {k: given[k] for k in WEIGHT_ORDER}
    m = {k: given["m_" + k] for k in WEIGHT_ORDER}
    v = {k: given["v_" + k] for k in WEIGHT_ORDER}

    shards = {k: _cast_bf16(w[k][0], k) for k in BIG_WEIGHTS}
    shard_shapes = {k: w[k].shape[1:] for k in BIG_WEIGHTS}
    big = {"w_in": _run_side(_gather_side({"w_in": shards["w_in"]}), "gather_w_in")[0]}

    loss, grad_x, (slots, mine, theirs), small_grads = _local_step(
        x[0], loss_target[0], big, {k: w[k] for k in SMALL_WEIGHTS}, shards, shard_shapes)
    loss = lax.psum(loss, MESH_AXES)

    last = [k for k in BIG_WEIGHTS if k not in mine]
    mine.update({k: _sum_slots(slots[k], k) for k in last})
    theirs.update(zip(last, _run_side(_swap_side([mine[k] for k in last]), "swap_last_grads")))
    out_g, out_d, out_m, out_v = {}, {}, {}, {}
    for k in BIG_WEIGHTS:
        res = _adamw_big(w[k][0], mine[k], theirs[k], m[k][0], v[k][0], k)
        out_g[k], out_d[k], out_m[k], out_v[k] = (t[None] for t in res)

    pick = lambda tree: {k: tree[k] for k in SMALL_WEIGHTS}
    dg1, shared = small_grads
    late = _run_side(_share_side(_pack_group({"norm_mix_pre": dg1}, PACK_GROUPS[0])), "share_last_grad")[0]
    parts = jnp.concatenate([late, shared], axis=1)
    res = _adamw_small(_pack_small(pick(w)), parts, _pack_small(pick(m)), _pack_small(pick(v)))
    for dst, packed in zip((out_g, out_d, out_m, out_v), res):
        dst.update(_unpack_small(packed, pick(w)))

    return (loss, grad_x[None], *[out_g[k] for k in WEIGHT_ORDER], *[out_d[k] for k in WEIGHT_ORDER],
            *[out_m[k] for k in WEIGHT_ORDER], *[out_v[k] for k in WEIGHT_ORDER])
```

```python
import functools
import math

import jax
import jax.numpy as jnp
from jax import lax
from jax.experimental import pallas as pl
from jax.experimental.pallas import tpu as pltpu

F32 = jnp.float32
BF16 = jnp.bfloat16

EPS = 1e-6
HEAD_DIM = 128
HEADS_PER_GROUP = 4
ATTN_DILATIONS = (1, 4, 16)
ATTN_BLK = 128
N_ATTN_HEADS = HEADS_PER_GROUP * len(ATTN_DILATIONS)
GROUP_W = HEADS_PER_GROUP * HEAD_DIM
HQ = N_ATTN_HEADS * HEAD_DIM
SSM_GROUP = 16
SSM_STATE = 64
SSM_TILE_CH = 128
SSM_TILE_ST = SSM_TILE_CH // SSM_GROUP * SSM_STATE
ADAM_LR = 0.001
ADAM_B1 = 0.9
ADAM_B2 = 0.999
ADAM_EPS = 1e-08
ADAM_WD = 0.01
ADAM_STEP = 10
NEG_BIG = -1e30
V7X_VMEM_LIMIT = 56 * 1024 * 1024
MESH_AXES = ("x", "y", "c")
N_CHIPS = 4


def _pick(n, cands):
    for c in cands:
        if n % c == 0:
            return c
    raise ValueError(f"no tile of {cands} divides {n}")


def _params(sem):
    return pltpu.CompilerParams(dimension_semantics=sem, vmem_limit_bytes=V7X_VMEM_LIMIT)


HBM = pl.BlockSpec(memory_space=pl.ANY)
MESH = pl.DeviceIdType.MESH


class _Side:
    def __init__(self, srcs, out_shapes, sem_shapes, build, aliases=None):
        self.srcs, self.out_shapes, self.sem_shapes, self.build = list(srcs), list(out_shapes), list(sem_shapes), build
        self.aliases = dict(aliases or {})

    def start(self, src, dst, sems):
        local, sends = self.build(src, dst, sems)[:2]
        for cp in local + sends:
            cp.start()

    def wait(self, src, dst, sems):
        local, sends, arrivals, forwards, passed_on = self.build(src, dst, sems)
        for cp, forward in zip(arrivals, forwards):
            cp.wait_recv()
            if forward is not None:
                forward.start()
        for cp in passed_on:
            cp.wait_recv()
        for cp in sends + [f for f in forwards if f is not None]:
            cp.wait_send()
        for cp in local:
            cp.wait()


def _join_sides(a, b):
    ns, no, nm = len(a.srcs), len(a.out_shapes), len(a.sem_shapes)

    def build(src, dst, sems):
        ra, rb = a.build(src[:ns], dst[:no], sems[:nm]), b.build(src[ns:], dst[no:], sems[nm:])
        return tuple(p + q for p, q in zip(ra, rb))

    aliases = {**a.aliases, **{ns + k: no + v for k, v in b.aliases.items()}}
    return _Side(a.srcs + b.srcs, a.out_shapes + b.out_shapes, a.sem_shapes + b.sem_shapes, build, aliases)


def _call(body, *, name, grid, in_specs, out_specs, out_shape, semantics, args, scratch_shapes=(), side=None, **kw):
    in_specs, out_specs, out_shape, scratch_shapes = list(in_specs), list(out_specs), list(out_shape), list(scratch_shapes)
    if side is None:
        res = pl.pallas_call(body, name=name, grid=grid, in_specs=in_specs, out_specs=out_specs, out_shape=out_shape,
                             scratch_shapes=scratch_shapes, compiler_params=_params(semantics), **kw)(*args)
        return list(res), []
    n_in, n_out, n_scr = len(in_specs), len(out_specs), len(scratch_shapes)
    ns_in, ns_out = len(side.srcs), len(side.out_shapes)

    def carrying(*refs):
        ins, s_in = refs[:n_in], refs[n_in:n_in + ns_in]
        o0 = n_in + ns_in
        outs, s_out = refs[o0:o0 + n_out], refs[o0 + n_out:o0 + n_out + ns_out]
        c0 = o0 + n_out + ns_out
        scr, sems = refs[c0:c0 + n_scr], refs[c0 + n_scr:]
        ids = [pl.program_id(a) for a in range(len(grid))]
        first = functools.reduce(jnp.logical_and, [i == 0 for i in ids])
        last = functools.reduce(jnp.logical_and, [i == g - 1 for i, g in zip(ids, grid)])

        @pl.when(first)
        def _():
            side.start(s_in, s_out, sems)

        body(*ins, *outs, *scr)

        @pl.when(last)
        def _():
            side.wait(s_in, s_out, sems)

    res = pl.pallas_call(
        carrying, name=name, grid=grid, in_specs=in_specs + [HBM] * ns_in, out_specs=out_specs + [HBM] * ns_out,
        out_shape=out_shape + side.out_shapes, scratch_shapes=scratch_shapes + side.sem_shapes,
        input_output_aliases={n_in + k: n_out + v for k, v in side.aliases.items()},
        compiler_params=pltpu.CompilerParams(dimension_semantics=("arbitrary",) * len(grid),
                                             vmem_limit_bytes=V7X_VMEM_LIMIT, has_side_effects=True), **kw,
    )(*args, *side.srcs)
    return list(res[:n_out]), list(res[n_out:])


def _run_side(side, name):
    ns, no = len(side.srcs), len(side.out_shapes)

    def body(*refs):
        src, dst, sems = refs[:ns], refs[ns:ns + no], refs[ns + no:]
        side.start(src, dst, sems)
        side.wait(src, dst, sems)

    return list(pl.pallas_call(body, name=name, in_specs=[HBM] * ns, out_specs=[HBM] * no, out_shape=side.out_shapes,
                               scratch_shapes=side.sem_shapes,
                               compiler_params=pltpu.CompilerParams(has_side_effects=True))(*side.srcs))


_DOT_DIMS = {"nn": (((1,), (0,)), ((), ())), "nt": (((1,), (1,)), ((), ())), "tn": (((0,), (0,)), ((), ()))}


MM_VMEM_BUDGET = 44 * 1024 * 1024
MM_STEP_BYTES = 1 << 20


def _size(dtype):
    return jnp.dtype(dtype).itemsize


def _mm_fused(as_, bs, pairs, mode, out_dtypes, name, extras=(), epilogue=None, side=None, out_place=None):
    M = as_[0].shape[0]
    N = bs[0].shape[1] if mode == "nn" else bs[0].shape[0]
    ks_a = [a.shape[1] for a in as_]
    ks_b = [b.shape[0] if mode == "nn" else b.shape[1] for b in bs]
    if epilogue is None:
        epilogue = lambda rs, es: rs
    offs = [off for _, off in extras]
    place = list(out_place) if out_place else [None] * len(out_dtypes)
    offs_all = offs + [p[1] for p in place if p is not None]
    best = None
    for tm in (2048, 1024, 512, 256, 128):
        for tn in (2048, 1024, 512, 256, 128):
            if M % tm or N % tn or any(off % tn for off in offs_all):
                continue
            vmem = (sum(2 * tm * k * 2 for k in ks_a) + sum(2 * k * tn * 2 for k in ks_b)
                    + sum(2 * tm * tn * _size(d) for d in out_dtypes) + sum(2 * tm * tn * _size(e.dtype) for e, _ in extras)
                    + len(pairs) * tm * tn * 4)
            cost = sum(k * N * 2 for k in ks_b) * (M // tm) + (M // tm) * (N // tn) * MM_STEP_BYTES
            if vmem <= MM_VMEM_BUDGET and (best is None or cost < best[0]):
                best = (cost, tm, tn)
    _, tm, tn = best
    na, nb, ne, no = len(as_), len(bs), len(extras), len(out_dtypes)
    dims = _DOT_DIMS[mode]

    def body(*refs):
        a_refs, b_refs = refs[:na], refs[na:na + nb]
        e_refs, o_refs = refs[na + nb:na + nb + ne], refs[na + nb + ne:]
        rs = [lax.dot_general(a_refs[ai][...], b_refs[bi][...], dims, preferred_element_type=F32) for ai, bi in pairs]
        outs = epilogue(rs, [e[...] for e in e_refs])
        for o_ref, o in zip(o_refs, outs):
            o_ref[...] = o.astype(o_ref.dtype)

    a_specs = [pl.BlockSpec((tm, k), lambda i, j: (i, 0)) for k in ks_a]
    if mode == "nn":
        b_specs = [pl.BlockSpec((k, tn), lambda i, j: (0, j)) for k in ks_b]
    else:
        b_specs = [pl.BlockSpec((tn, k), lambda i, j: (j, 0)) for k in ks_b]
    e_specs = [pl.BlockSpec((tm, tn), lambda i, j, o=off // tn: (i, o + j)) for off in offs]
    o_specs = [pl.BlockSpec((tm, tn), lambda i, j, o=(p[1] // tn if p else 0): (i, o + j)) for p in place]
    outs, carried = _call(
        body, name=name, grid=(M // tm, N // tn), in_specs=a_specs + b_specs + e_specs, out_specs=o_specs,
        out_shape=[jax.ShapeDtypeStruct((M, p[0] if p else N), d) for d, p in zip(out_dtypes, place)],
        semantics=("parallel", "arbitrary"),
        args=[*as_, *bs, *[e for e, _ in extras]], side=side)
    return outs if side is None else (outs, carried)


def _mm(a, b, mode, out_dtype, name, side=None):
    res = _mm_fused([a], [b], [(0, 0)], mode, [out_dtype], name, side=side)
    return res[0] if side is None else (res[0][0], res[1])


def _mm_kloop(a, b, mode, out_dtype, name, second=None, side=None):
    if mode == "nn":
        (M, K), (_, N) = a.shape, b.shape
    elif mode == "nt":
        (M, K), (N, _) = a.shape, b.shape
    else:
        (K, M), (_, N) = a.shape, b.shape
    products = 1 if second is None else 2
    best = None
    for tm in (2816, 2048, 1408, 1024, 512, 256, 128):
        for tn in (2816, 2432, 2048, 1408, 1024, 512, 256, 128):
            for tk in (1024, 512, 256, 128):
                if M % tm or N % tn or K % tk:
                    continue
                vmem = 2 * tm * tn * 4 + 2 * tm * tn * _size(out_dtype) + products * 2 * tk * (tm + tn) * 2
                steps = (M // tm) * (N // tn) * (K // tk)
                cost = K * M * 2 * (N // tn) + K * N * 2 * (M // tm) + steps * MM_STEP_BYTES
                if vmem <= MM_VMEM_BUDGET and (best is None or cost < best[0]):
                    best = (cost, tm, tn, tk)
    _, tm, tn, tk = best
    nk = K // tk
    dims = _DOT_DIMS[mode]

    def body(*refs):
        o_ref, acc_ref = refs[-2:]
        k = pl.program_id(2)

        @pl.when(k == 0)
        def _():
            acc_ref[...] = jnp.zeros_like(acc_ref)

        for p in range(products):
            @pl.when(jnp.logical_and(k >= p * nk, k < (p + 1) * nk))
            def _(p=p):
                acc_ref[...] += lax.dot_general(refs[2 * p][...], refs[2 * p + 1][...], dims, preferred_element_type=F32)

        @pl.when(k == products * nk - 1)
        def _():
            o_ref[...] = acc_ref[...].astype(o_ref.dtype)

    def a_spec(p):
        kk = lambda k: jnp.clip(k - p * nk, 0, nk - 1)
        if mode == "tn":
            return pl.BlockSpec((tk, tm), lambda i, j, k: (kk(k), i))
        return pl.BlockSpec((tm, tk), lambda i, j, k: (i, kk(k)))

    def b_spec(p):
        kk = lambda k: jnp.clip(k - p * nk, 0, nk - 1)
        if mode == "nt":
            return pl.BlockSpec((tn, tk), lambda i, j, k: (j, kk(k)))
        return pl.BlockSpec((tk, tn), lambda i, j, k: (kk(k), j))

    o_spec = pl.BlockSpec((tm, tn), lambda i, j, k: (i, j))
    operands = (a, b) + (tuple(second) if second is not None else ())
    outs, carried = _call(
        body, name=name, grid=(M // tm, N // tn, products * nk),
        in_specs=[spec(p) for p in range(products) for spec in (a_spec, b_spec)], out_specs=[o_spec],
        out_shape=[jax.ShapeDtypeStruct((M, N), out_dtype)], scratch_shapes=[pltpu.VMEM((tm, tn), F32)],
        semantics=("parallel", "parallel", "arbitrary"), args=operands, side=side)
    return outs[0] if side is None else (outs[0], carried)


def _sigmoid(v):
    return 0.5 * jnp.tanh(0.5 * v) + 0.5


_GELU_C = math.sqrt(2.0 / math.pi)


def _gelu(v):
    return 0.5 * v * (1.0 + jnp.tanh(_GELU_C * (v + 0.044715 * v * v * v)))


def _gelu_grad(v):
    t = jnp.tanh(_GELU_C * (v + 0.044715 * v * v * v))
    return 0.5 * (1.0 + t) + 0.5 * v * (1.0 - t * t) * _GELU_C * (1.0 + 3.0 * 0.044715 * v * v)


def _rms(v, gain):
    r = lax.rsqrt(jnp.mean(v * v, axis=-1, keepdims=True) + EPS)
    return v * r * gain


def _rms_bwd(v, gain, dy):
    r = lax.rsqrt(jnp.mean(v * v, axis=-1, keepdims=True) + EPS)
    a = dy * gain
    dv = r * a - v * (r * r * r) * jnp.mean(a * v, axis=-1, keepdims=True)
    return dv, dy * v * r


def _row_tile(s):
    return _pick(s, (256, 128, 64, 8))


def _norm_in(x, gain):
    s, d = x.shape
    tr = _row_tile(s)

    def body(x_ref, g_ref, h_ref):
        h_ref[...] = _rms(x_ref[...], g_ref[...]).astype(BF16)

    row = pl.BlockSpec((tr, d), lambda i: (i, 0))
    vec = pl.BlockSpec((1, d), lambda i: (0, 0))
    return pl.pallas_call(body, name="norm_in", grid=(s // tr,), in_specs=[row, vec], out_specs=row,
                          out_shape=jax.ShapeDtypeStruct((s, d), BF16), compiler_params=_params(("parallel",)))(x, gain)


def _norm_mid(x, mo, g_post, g_pre):
    s, d = x.shape
    tr = _row_tile(s)

    def body(x_ref, mo_ref, g2_ref, g3_ref, x2_ref, h2_ref):
        x2 = x_ref[...] + _rms(mo_ref[...], g2_ref[...])
        x2_ref[...] = x2
        h2_ref[...] = _rms(x2, g3_ref[...]).astype(BF16)

    row = pl.BlockSpec((tr, d), lambda i: (i, 0))
    vec = pl.BlockSpec((1, d), lambda i: (0, 0))
    return pl.pallas_call(
        body, name="norm_mid", grid=(s // tr,), in_specs=[row, row, vec, vec], out_specs=[row, row],
        out_shape=[jax.ShapeDtypeStruct((s, d), F32), jax.ShapeDtypeStruct((s, d), BF16)],
        compiler_params=_params(("parallel",)))(x, mo, g_post, g_pre)


def _loss_head(x2, f, g_post, target):
    s, d = x2.shape
    tr = _row_tile(s)

    def body(x2_ref, f_ref, g_ref, t_ref, loss_ref, dout_ref, df_ref, dg_ref):
        @pl.when(pl.program_id(0) == 0)
        def _():
            loss_ref[...] = jnp.zeros_like(loss_ref)
            dg_ref[...] = jnp.zeros_like(dg_ref)

        fv = f_ref[...]
        g = g_ref[...]
        err = x2_ref[...] + _rms(fv, g) - t_ref[...]
        loss_ref[...] += 0.5 * jnp.sum(jnp.mean(err * err, axis=-1, keepdims=True), axis=0, keepdims=True)
        dout = err * (1.0 / d)
        dout_ref[...] = dout
        df, dg = _rms_bwd(fv, g, dout)
        df_ref[...] = df.astype(BF16)
        dg_ref[...] += jnp.sum(dg, axis=0, keepdims=True)

    row = pl.BlockSpec((tr, d), lambda i: (i, 0))
    vec = pl.BlockSpec((1, d), lambda i: (0, 0))
    one = pl.BlockSpec((1, 1), lambda i: (0, 0))
    return pl.pallas_call(
        body, name="loss_head", grid=(s // tr,), in_specs=[row, row, vec, row], out_specs=[one, row, row, vec],
        out_shape=[jax.ShapeDtypeStruct((1, 1), F32), jax.ShapeDtypeStruct((s, d), F32),
                   jax.ShapeDtypeStruct((s, d), BF16), jax.ShapeDtypeStruct((1, d), F32)],
        compiler_params=_params(("arbitrary",)))(x2, f, g_post, target)


def _norm_mid_bwd(x2, mo, g_post, g_pre, dout, dh2):
    s, d = x2.shape
    tr = _row_tile(s)

    def body(x2_ref, mo_ref, g2_ref, g3_ref, dout_ref, dh2_ref, dx2_ref, dmo_ref, dg2_ref, dg3_ref):
        @pl.when(pl.program_id(0) == 0)
        def _():
            dg2_ref[...] = jnp.zeros_like(dg2_ref)
            dg3_ref[...] = jnp.zeros_like(dg3_ref)

        dv, dg3 = _rms_bwd(x2_ref[...], g3_ref[...], dh2_ref[...])
        dx2 = dout_ref[...] + dv
        dx2_ref[...] = dx2
        dmo, dg2 = _rms_bwd(mo_ref[...], g2_ref[...], dx2)
        dmo_ref[...] = dmo.astype(BF16)
        dg2_ref[...] += jnp.sum(dg2, axis=0, keepdims=True)
        dg3_ref[...] += jnp.sum(dg3, axis=0, keepdims=True)

    row = pl.BlockSpec((tr, d), lambda i: (i, 0))
    vec = pl.BlockSpec((1, d), lambda i: (0, 0))
    return pl.pallas_call(
        body, name="norm_mid_bwd", grid=(s // tr,), in_specs=[row, row, vec, vec, row, row],
        out_specs=[row, row, vec, vec],
        out_shape=[jax.ShapeDtypeStruct((s, d), F32), jax.ShapeDtypeStruct((s, d), BF16),
                   jax.ShapeDtypeStruct((1, d), F32), jax.ShapeDtypeStruct((1, d), F32)],
        compiler_params=_params(("arbitrary",)))(x2, mo, g_post, g_pre, dout, dh2)


def _norm_in_bwd(x, gain, dh, dx2):
    s, d = x.shape
    tr = _row_tile(s)

    def body(x_ref, g_ref, dh_ref, dx2_ref, dx_ref, dg_ref):
        @pl.when(pl.program_id(0) == 0)
        def _():
            dg_ref[...] = jnp.zeros_like(dg_ref)

        dv, dg = _rms_bwd(x_ref[...], g_ref[...], dh_ref[...])
        dx_ref[...] = dx2_ref[...] + dv
        dg_ref[...] += jnp.sum(dg, axis=0, keepdims=True)

    row = pl.BlockSpec((tr, d), lambda i: (i, 0))
    vec = pl.BlockSpec((1, d), lambda i: (0, 0))
    return pl.pallas_call(
        body, name="norm_in_bwd", grid=(s // tr,), in_specs=[row, vec, row, row], out_specs=[row, vec],
        out_shape=[jax.ShapeDtypeStruct((s, d), F32), jax.ShapeDtypeStruct((1, d), F32)],
        compiler_params=_params(("arbitrary",)))(x, gain, dh, dx2)


def _swiglu_epilogue(rs, es):
    g, u = rs
    return [g * _sigmoid(g) * u, g, u]


def _swiglu_bwd_epilogue(rs, es):
    d = rs[0]
    g, u = es[0].astype(F32), es[1].astype(F32)
    sg = _sigmoid(g)
    return [d * u * sg * (1.0 + g * (1.0 - sg)), d * g * sg]


def _sum_epilogue(rs, es):
    return [rs[0] + rs[1]]


def _gates_epilogue(rs, es):
    ab, gv, gg = rs
    ga, gs = es
    return [_sigmoid(ga) * ab + _sigmoid(gs) * gv * _sigmoid(gg), ab, gv, gg]


def _gates_bwd_epilogue(rs, es):
    dm = rs[0]
    ga, gs, ab, gv, gg = (e.astype(F32) for e in es)
    sa, ss, sg = _sigmoid(ga), _sigmoid(gs), _sigmoid(gg)
    dsb = dm * ss
    return [dm * ab * sa * (1.0 - sa), dm * gv * sg * ss * (1.0 - ss), dm * sa, dsb * sg, dsb * gv * sg * (1.0 - sg)]


ATTN_ROWS = 2048


def _dilate_qkv(z, g, d):
    s = z.shape[0]
    tm = ATTN_ROWS
    per = tm // d
    nh = HEADS_PER_GROUP

    def body(z_ref, o_ref):
        for r in range(d):
            rows = z_ref[...] if d == 1 else z_ref[pl.ds(r, per, stride=d), :]
            o_ref[0, r] = rows.astype(BF16)

    return pl.pallas_call(
        body, name=f"dilate_qkv_{g}", grid=(s // tm, 3, nh),
        in_specs=[pl.BlockSpec((tm, HEAD_DIM), lambda i, w, h: (i, (3 * w + g) * nh + h))],
        out_specs=pl.BlockSpec((1, d, per, HEAD_DIM), lambda i, w, h: (w, 0, i, h)),
        out_shape=jax.ShapeDtypeStruct((3, d, s // d, GROUP_W), BF16),
        compiler_params=_params(("parallel", "parallel", "parallel")))(z)


def _undilate_dqkv(dqkv, dz, g, d):
    s = dz.shape[0]
    tm = ATTN_ROWS
    per = tm // d
    nh = HEADS_PER_GROUP

    def body(i_ref, dz_ref, o_ref, nat_ref):
        del dz_ref
        if d == 1:
            o_ref[...] = i_ref[0, 0]
        else:
            for r in range(d):
                nat_ref[pl.ds(r, per, stride=d), :] = i_ref[0, r].astype(F32)
            o_ref[...] = nat_ref[...].astype(BF16)

    return pl.pallas_call(
        body, name=f"undilate_dqkv_{g}", grid=(s // tm, 3, nh),
        in_specs=[pl.BlockSpec((1, d, per, HEAD_DIM), lambda i, w, h: (w, 0, i, h)),
                  pl.BlockSpec(memory_space=pl.ANY)],
        out_specs=pl.BlockSpec((tm, HEAD_DIM), lambda i, w, h: (i, (3 * w + g) * nh + h)),
        out_shape=jax.ShapeDtypeStruct(dz.shape, dz.dtype), input_output_aliases={1: 0},
        scratch_shapes=[pltpu.VMEM((tm, HEAD_DIM), F32)],
        compiler_params=_params(("parallel", "parallel", "parallel")))(dqkv, dz)


def _alibi_slope(head):
    return 2.0 ** (-8.0 * (head + 1) / N_ATTN_HEADS)


def _dot_nt(a, b):
    return lax.dot_general(a, b, _DOT_DIMS["nt"], preferred_element_type=F32)


def _dot_tn(a, b):
    return lax.dot_general(a, b, _DOT_DIMS["tn"], preferred_element_type=F32)


def _dot(a, b):
    return jnp.dot(a, b, preferred_element_type=F32)


GROUP_ROWS = HEADS_PER_GROUP * ATTN_BLK


def _band_bias(g, d, pairs):
    qi = jnp.arange(ATTN_BLK)[:, None]
    ki = jnp.arange(ATTN_BLK)[None, :]
    rows = []
    for hh in range(HEADS_PER_GROUP):
        slope_d = _alibi_slope(g * HEADS_PER_GROUP + hh) * d
        tiles = []
        for kind in pairs:
            dist = qi - ki if kind == "cur" else ATTN_BLK + qi - ki
            ok = dist >= 0 if kind == "cur" else dist <= ATTN_BLK
            tiles.append(jnp.where(ok, -slope_d * dist.astype(F32), NEG_BIG))
        rows.append(jnp.concatenate(tiles, axis=1))
    return jnp.concatenate(rows, axis=0).astype(F32)


def _tile_cols(t):
    return slice(t * ATTN_BLK, (t + 1) * ATTN_BLK)


def _attn_fwd(qkv, g, d):
    _, _, L, _ = qkv.shape
    nb = L // ATTN_BLK
    scale = HEAD_DIM ** -0.5

    def body(q_ref, kc_ref, kp_ref, vc_ref, vp_ref, bias_ref, o_ref, lse_ref, s_ref, p_ref):
        n = pl.program_id(1)
        for hh in range(HEADS_PER_GROUP):
            cols, rows = _tile_cols(hh), _tile_cols(hh)
            q = q_ref[0, 0, :, cols]
            s_ref[rows, _tile_cols(0)] = _dot_nt(q, kp_ref[0, 0, :, cols])
            s_ref[rows, _tile_cols(1)] = _dot_nt(q, kc_ref[0, 0, :, cols])
        col = lax.broadcasted_iota(jnp.int32, (GROUP_ROWS, 2 * ATTN_BLK), 1)
        s = s_ref[...] * scale + bias_ref[...]
        s = jnp.where(jnp.logical_and(col < ATTN_BLK, n == 0), NEG_BIG, s)
        m = jnp.max(s, axis=-1, keepdims=True)
        e = jnp.exp(s - m)
        l = jnp.sum(e, axis=-1, keepdims=True)
        p_ref[...] = (e * (1.0 / l)).astype(BF16)
        lse = m + jnp.log(l)
        for hh in range(HEADS_PER_GROUP):
            cols, rows = _tile_cols(hh), _tile_cols(hh)
            o_ref[0, :, cols] = (_dot(p_ref[rows, _tile_cols(0)], vp_ref[0, 0, :, cols])
                                 + _dot(p_ref[rows, _tile_cols(1)], vc_ref[0, 0, :, cols]))
            lse_ref[0, :, cols] = jnp.broadcast_to(lse[rows], (ATTN_BLK, HEAD_DIM))

    def spec(w, shift):
        return pl.BlockSpec((1, 1, ATTN_BLK, GROUP_W), lambda r, n: (w, r, jnp.maximum(n + shift, 0), 0))

    out = pl.BlockSpec((1, ATTN_BLK, GROUP_W), lambda r, n: (r, n, 0))
    bias = _band_bias(g, d, ("prev", "cur"))
    return pl.pallas_call(
        body, name=f"attn_fwd_{g}", grid=(d, nb),
        in_specs=[spec(0, 0), spec(1, 0), spec(1, -1), spec(2, 0), spec(2, -1),
                  pl.BlockSpec(bias.shape, lambda r, n: (0, 0))],
        out_specs=[out, out], out_shape=[jax.ShapeDtypeStruct((d, L, GROUP_W), F32)] * 2,
        scratch_shapes=[pltpu.VMEM((GROUP_ROWS, 2 * ATTN_BLK), F32), pltpu.VMEM((GROUP_ROWS, 2 * ATTN_BLK), BF16)],
        compiler_params=_params(("parallel", "parallel")))(qkv, qkv, qkv, qkv, qkv, bias)


def _attn_bwd(qkv, do, lse, cc, g, d, side=None):
    _, _, L, _ = qkv.shape
    nb = L // ATTN_BLK
    scale = HEAD_DIM ** -0.5
    a_, b_, c_ = _tile_cols(0), _tile_cols(1), _tile_cols(2)

    def body(q0_ref, q1_ref, k0_ref, kp_ref, v0_ref, vp_ref, do0_ref, do1_ref, l0_ref, l1_ref, c0_ref, c1_ref,
             bias_ref, o_ref, s_ref, dp_ref, l_ref, c_ref, p_ref, ds_ref):
        n = pl.program_id(1)
        for hh in range(HEADS_PER_GROUP):
            cols, rows = _tile_cols(hh), _tile_cols(hh)
            q0, q1 = q0_ref[0, 0, :, cols], q1_ref[0, 0, :, cols]
            k0, kp = k0_ref[0, 0, :, cols], kp_ref[0, 0, :, cols]
            v0, vp = v0_ref[0, 0, :, cols], vp_ref[0, 0, :, cols]
            do0, do1 = do0_ref[0, :, cols], do1_ref[0, :, cols]
            s_ref[rows, a_], s_ref[rows, b_], s_ref[rows, c_] = _dot_nt(q0, k0), _dot_nt(q0, kp), _dot_nt(q1, k0)
            dp_ref[rows, a_], dp_ref[rows, b_], dp_ref[rows, c_] = _dot_nt(do0, v0), _dot_nt(do0, vp), _dot_nt(do1, v0)
            l_ref[rows, a_], l_ref[rows, b_], l_ref[rows, c_] = l0_ref[0, :, cols], l0_ref[0, :, cols], l1_ref[0, :, cols]
            c_ref[rows, a_], c_ref[rows, b_], c_ref[rows, c_] = c0_ref[0, :, cols], c0_ref[0, :, cols], c1_ref[0, :, cols]
        col = lax.broadcasted_iota(jnp.int32, (GROUP_ROWS, 3 * ATTN_BLK), 1)
        tile = col // ATTN_BLK
        gone = jnp.logical_or(jnp.logical_and(tile == 1, n == 0), jnp.logical_and(tile == 2, n == nb - 1))
        s = jnp.where(gone, NEG_BIG, s_ref[...] * scale + bias_ref[...])
        p = jnp.exp(s - l_ref[...])
        p_ref[...] = p.astype(BF16)
        ds_ref[...] = (p * (dp_ref[...] + c_ref[...])).astype(BF16)
        for hh in range(HEADS_PER_GROUP):
            cols, rows = _tile_cols(hh), _tile_cols(hh)
            q0, q1 = q0_ref[0, 0, :, cols], q1_ref[0, 0, :, cols]
            k0, kp = k0_ref[0, 0, :, cols], kp_ref[0, 0, :, cols]
            do0, do1 = do0_ref[0, :, cols], do1_ref[0, :, cols]
            o_ref[0, 0, :, cols] = ((_dot(ds_ref[rows, a_], k0) + _dot(ds_ref[rows, b_], kp)) * scale).astype(BF16)
            o_ref[1, 0, :, cols] = ((_dot_tn(ds_ref[rows, a_], q0) + _dot_tn(ds_ref[rows, c_], q1)) * scale).astype(BF16)
            o_ref[2, 0, :, cols] = (_dot_tn(p_ref[rows, a_], do0) + _dot_tn(p_ref[rows, c_], do1)).astype(BF16)

    def spec(w, shift):
        return pl.BlockSpec((1, 1, ATTN_BLK, GROUP_W), lambda r, n: (w, r, jnp.clip(n + shift, 0, nb - 1), 0))

    def spec3(shift):
        return pl.BlockSpec((1, ATTN_BLK, GROUP_W), lambda r, n: (r, jnp.clip(n + shift, 0, nb - 1), 0))

    bias = _band_bias(g, d, ("cur", "prev", "prev"))
    wide = (GROUP_ROWS, 3 * ATTN_BLK)
    outs, carried = _call(
        body, name=f"attn_bwd_{g}", grid=(d, nb),
        in_specs=[spec(0, 0), spec(0, 1), spec(1, 0), spec(1, -1), spec(2, 0), spec(2, -1),
                  spec3(0), spec3(1), spec3(0), spec3(1), spec3(0), spec3(1), pl.BlockSpec(wide, lambda r, n: (0, 0))],
        out_specs=[pl.BlockSpec((3, 1, ATTN_BLK, GROUP_W), lambda r, n: (0, r, n, 0))],
        out_shape=[jax.ShapeDtypeStruct((3, d, L, GROUP_W), BF16)],
        scratch_shapes=[pltpu.VMEM(wide, F32)] * 4 + [pltpu.VMEM(wide, BF16)] * 2, semantics=("parallel", "parallel"),
        args=[qkv, qkv, qkv, qkv, qkv, qkv, do, do, lse, lse, cc, cc, bias], side=side)
    return outs[0] if side is None else (outs[0], carried)


def _load_natural(refs, nat_refs):
    for g, d in enumerate(ATTN_DILATIONS):
        if d == 1:
            nat_refs[g][...] = refs[g][0]
        else:
            per = ATTN_ROWS // d
            for r in range(d):
                nat_refs[g][pl.ds(r, per, stride=d), :] = refs[g][r]


def _mix_weights(lse_nat):
    l0, l1, l2 = lse_nat[0][...], lse_nat[1][...], lse_nat[2][...]
    m = jnp.maximum(jnp.maximum(l0, l1), l2)
    e0, e1, e2 = jnp.exp(l0 - m), jnp.exp(l1 - m), jnp.exp(l2 - m)
    inv = 1.0 / (e0 + e1 + e2)
    return e0 * inv, e1 * inv, e2 * inv


def _dilated_specs(s):
    return [pl.BlockSpec((d, ATTN_ROWS // d, HEAD_DIM), lambda i, h: (0, i, h)) for d in ATTN_DILATIONS]


NATURAL_SCRATCH = [pltpu.VMEM((ATTN_ROWS, HEAD_DIM), F32)] * (2 * len(ATTN_DILATIONS))


def _attn_merge(outs, lses):
    s = outs[0].shape[0] * outs[0].shape[1]

    def body(o0, o1, o2, l0, l1, l2, a_ref, *nat):
        onat, lnat = nat[:3], nat[3:]
        _load_natural((o0, o1, o2), onat)
        _load_natural((l0, l1, l2), lnat)
        w0, w1, w2 = _mix_weights(lnat)
        a_ref[...] = (w0 * onat[0][...] + w1 * onat[1][...] + w2 * onat[2][...]).astype(BF16)

    return pl.pallas_call(
        body, name="attn_merge", grid=(s // ATTN_ROWS, HEADS_PER_GROUP), in_specs=_dilated_specs(s) * 2,
        out_specs=pl.BlockSpec((ATTN_ROWS, HEAD_DIM), lambda i, h: (i, h)),
        out_shape=jax.ShapeDtypeStruct((s, GROUP_W), BF16), scratch_shapes=NATURAL_SCRATCH,
        compiler_params=_params(("parallel", "parallel")))(*outs, *lses)


def _attn_merge_bwd(outs, lses, dattn):
    s = dattn.shape[0]

    def body(o0, o1, o2, l0, l1, l2, da_ref, do0, do1, do2, c0, c1, c2, *nat):
        onat, lnat = nat[:3], nat[3:]
        _load_natural((o0, o1, o2), onat)
        _load_natural((l0, l1, l2), lnat)
        ws = _mix_weights(lnat)
        da = da_ref[...]
        attn = ws[0] * onat[0][...] + ws[1] * onat[1][...] + ws[2] * onat[2][...]
        tot = jnp.broadcast_to(jnp.sum(da * attn, axis=-1, keepdims=True), (ATTN_ROWS, HEAD_DIM))
        for g, (d, do_ref, c_ref) in enumerate(zip(ATTN_DILATIONS, (do0, do1, do2), (c0, c1, c2))):
            if d == 1:
                do_ref[0] = (ws[g] * da).astype(BF16)
                c_ref[0] = -ws[g] * tot
            else:
                onat[g][...] = ws[g] * da
                lnat[g][...] = -ws[g] * tot
                per = ATTN_ROWS // d
                for r in range(d):
                    do_ref[r] = onat[g][pl.ds(r, per, stride=d), :].astype(BF16)
                    c_ref[r] = lnat[g][pl.ds(r, per, stride=d), :]

    dil = _dilated_specs(s)
    shapes = [jax.ShapeDtypeStruct(o.shape, BF16) for o in outs] + [jax.ShapeDtypeStruct(o.shape, F32) for o in outs]
    return pl.pallas_call(
        body, name="attn_merge_bwd", grid=(s // ATTN_ROWS, HEADS_PER_GROUP),
        in_specs=dil * 2 + [pl.BlockSpec((ATTN_ROWS, HEAD_DIM), lambda i, h: (i, h))], out_specs=dil * 2,
        out_shape=shapes, scratch_shapes=NATURAL_SCRATCH,
        compiler_params=_params(("parallel", "parallel")))(*outs, *lses, dattn)


def _ssm_prepare(a_re, a_im, log_dt, b_re, b_im, c_re, c_im):
    n_g = a_re.shape[0]
    nj = n_g * SSM_GROUP // SSM_TILE_CH
    gpt = SSM_TILE_CH // SSM_GROUP
    dt = jnp.exp(log_dt)[:, None]
    mag = jnp.exp(a_re * dt)
    lr, li = mag * jnp.cos(a_im * dt), mag * jnp.sin(a_im * dt)
    den = a_re * a_re + a_im * a_im
    cr = ((lr - 1.0) * a_re + li * a_im) / den
    ci = (li * a_re - (lr - 1.0) * a_im) / den
    bb_re = cr[..., None] * b_re - ci[..., None] * b_im
    bb_im = cr[..., None] * b_im + ci[..., None] * b_re
    eye = jnp.eye(gpt, dtype=F32)

    def b_tiles(t):
        t = t.transpose(0, 2, 1).reshape(nj, gpt, SSM_GROUP, SSM_STATE)
        return jnp.einsum("jgcp,gh->jgchp", t, eye).reshape(nj, SSM_TILE_CH, SSM_TILE_ST)

    def c_tiles(t):
        t = t.reshape(nj, gpt, SSM_GROUP, SSM_STATE)
        return jnp.einsum("jgcp,gh->jhpgc", t, eye).reshape(nj, SSM_TILE_ST, SSM_TILE_CH)

    lam = jnp.stack([lr.reshape(-1), li.reshape(-1)])
    bmat = jnp.concatenate([b_tiles(bb_re), b_tiles(bb_im)], axis=2)
    cmat = jnp.concatenate([c_tiles(c_re), -c_tiles(c_im)], axis=1)
    return lam, bmat, cmat


SSM_SEGMENTS = 8


def _to_segment_order(nat, perm_ref):
    per = nat.shape[0] // SSM_SEGMENTS
    for i in range(SSM_SEGMENTS):
        perm_ref[pl.ds(i, per, stride=SSM_SEGMENTS), :] = nat[i * per:(i + 1) * per, :]
    return perm_ref[...]


def _to_time_order(val, perm_ref, store):
    per = val.shape[0] // SSM_SEGMENTS
    perm_ref[...] = val
    for i in range(SSM_SEGMENTS):
        store(i, perm_ref[pl.ds(i, per, stride=SSM_SEGMENTS), :])


def _fill_powers(lam_ref, w_ref, nj, tau_n):
    for j in range(nj):
        st = slice(j * SSM_TILE_ST, (j + 1) * SSM_TILE_ST)
        lr = jnp.broadcast_to(lam_ref[0:1, st], (SSM_SEGMENTS, SSM_TILE_ST))
        li = jnp.broadcast_to(lam_ref[1:2, st], (SSM_SEGMENTS, SSM_TILE_ST))
        wr, wi = lr, li
        for tau in range(tau_n):
            rows = slice(tau * SSM_SEGMENTS, (tau + 1) * SSM_SEGMENTS)
            w_ref[j, rows, :SSM_TILE_ST] = wr
            w_ref[j, rows, SSM_TILE_ST:] = wi
            wr, wi = wr * lr - wi * li, wr * li + wi * lr


def _segment_scan(src, xs_ref, w_tile, lr, li, cr, ci, conj, reverse):
    seg, half = SSM_SEGMENTS, SSM_TILE_ST
    tau_n = src.shape[0] // seg
    sgn = -1.0 if conj else 1.0
    lr8 = jnp.broadcast_to(lr, (seg, half))
    li8 = jnp.broadcast_to(li, (seg, half)) * sgn
    xr = jnp.zeros((seg, half), F32)
    xi = jnp.zeros((seg, half), F32)
    order = range(tau_n - 1, -1, -1) if reverse else range(tau_n)
    for tau in order:
        rows = slice(tau * seg, (tau + 1) * seg)
        xr, xi = lr8 * xr - li8 * xi + src[rows, :half], lr8 * xi + li8 * xr + src[rows, half:]
        xs_ref[rows, :half] = xr
        xs_ref[rows, half:] = xi
    pr = w_tile[(tau_n - 1) * seg:(tau_n - 1) * seg + 1, :half]
    pi = w_tile[(tau_n - 1) * seg:(tau_n - 1) * seg + 1, half:] * sgn
    fr, fi = cr, ci
    ins_r, ins_i = [None] * seg, [None] * seg
    runs = range(seg - 1, -1, -1) if reverse else range(seg)
    for i in runs:
        ins_r[i], ins_i[i] = fr, fi
        fr, fi = xr[i:i + 1, :] + pr * fr - pi * fi, xi[i:i + 1, :] + pr * fi + pi * fr
    in_r = jnp.concatenate(ins_r, axis=0)
    in_i = jnp.concatenate(ins_i, axis=0)
    for tau in range(tau_n):
        rows = slice(tau * seg, (tau + 1) * seg)
        wrow = (tau_n - 1 - tau) if reverse else tau
        wr = w_tile[wrow * seg:(wrow + 1) * seg, :half]
        wi = w_tile[wrow * seg:(wrow + 1) * seg, half:] * sgn
        xs_ref[rows, :half] += wr * in_r - wi * in_i
        xs_ref[rows, half:] += wr * in_i + wi * in_r
    return (fr, fi), (in_r, in_i)


def _ssm_dims(z, bmat, u_off):
    s = z.shape[0]
    nj = bmat.shape[0]
    t_rows = _pick(s, (256, 128))
    return s, nj, nj * SSM_TILE_CH, nj * SSM_TILE_ST, t_rows


def _ssm_fwd(z, bmat, cmat, lam, dskip, u_off, side=None):
    s, nj, w, ns, t_rows = _ssm_dims(z, bmat, u_off)
    per = t_rows // SSM_SEGMENTS

    def body(*refs):
        u_refs = refs[:nj]
        b_ref, c_ref, lam_ref, d_ref, y_ref, yg_ref, xin_ref, carry_ref, w_ref, xs_ref, perm_ref = refs[nj:]

        @pl.when(pl.program_id(0) == 0)
        def _():
            carry_ref[...] = jnp.zeros_like(carry_ref)
            _fill_powers(lam_ref, w_ref, nj, per)

        xin_ref[0] = carry_ref[...]
        for j in range(nj):
            st = slice(j * SSM_TILE_ST, (j + 1) * SSM_TILE_ST)
            ch = slice(j * SSM_TILE_CH, (j + 1) * SSM_TILE_CH)
            up = _to_segment_order(u_refs[j], perm_ref)
            bu = _dot(up.astype(BF16), b_ref[j])
            (fr, fi), _ = _segment_scan(bu, xs_ref, w_ref.at[j], lam_ref[0:1, st], lam_ref[1:2, st],
                                        carry_ref[0:1, st], carry_ref[1:2, st], conj=False, reverse=False)
            carry_ref[0:1, st] = fr
            carry_ref[1:2, st] = fi
            yp = _dot(xs_ref[...].astype(BF16), c_ref[j]) + d_ref[:, ch] * up

            def store(i, rows, ch=ch):
                y_ref[i * per:(i + 1) * per, ch] = rows
                yg_ref[i * per:(i + 1) * per, ch] = _gelu(rows).astype(BF16)

            _to_time_order(yp, perm_ref, store)

    u_specs = [pl.BlockSpec((t_rows, SSM_TILE_CH), lambda c, k=k: (c, u_off // SSM_TILE_CH + k)) for k in range(nj)]
    full3 = lambda shape: pl.BlockSpec(shape, lambda c: (0, 0, 0))
    full2 = lambda shape: pl.BlockSpec(shape, lambda c: (0, 0))
    rows = pl.BlockSpec((t_rows, w), lambda c: (c, 0))
    outs, carried = _call(
        body, name="ssm_fwd", grid=(s // t_rows,),
        in_specs=u_specs + [full3(bmat.shape), full3(cmat.shape), full2(lam.shape), full2(dskip.shape)],
        out_specs=[rows, rows, pl.BlockSpec((1, 2, ns), lambda c: (c, 0, 0))],
        out_shape=[jax.ShapeDtypeStruct((s, w), F32), jax.ShapeDtypeStruct((s, w), BF16),
                   jax.ShapeDtypeStruct((s // t_rows, 2, ns), F32)],
        scratch_shapes=[pltpu.VMEM((2, ns), F32), pltpu.VMEM((nj, t_rows, 2 * SSM_TILE_ST), F32),
                        pltpu.VMEM((t_rows, 2 * SSM_TILE_ST), F32), pltpu.VMEM((t_rows, SSM_TILE_CH), F32)],
        semantics=("arbitrary",), args=[*([z] * nj), bmat, cmat, lam, dskip], side=side)
    return outs if side is None else (outs, carried)


def _ssm_bwd(z, y, dyg, xin, bmat, cmat, lam, dskip, u_off, side=None):
    s, nj, w, ns, t_rows = _ssm_dims(z, bmat, u_off)
    nc = s // t_rows
    per = t_rows // SSM_SEGMENTS
    seg, half = SSM_SEGMENTS, SSM_TILE_ST

    def body(*refs):
        u_refs = refs[:nj]
        (y_ref, dyg_ref, xin_ref, b_ref, c_ref, lam_ref, d_ref, du_ref, db_ref, dc_ref, dlam_ref, dd_ref,
         carry_ref, w_ref, xs_ref, gs_ref, perm_ref, acc_ref) = refs[nj:]

        @pl.when(pl.program_id(0) == 0)
        def _():
            carry_ref[...] = jnp.zeros_like(carry_ref)
            db_ref[...] = jnp.zeros_like(db_ref)
            dc_ref[...] = jnp.zeros_like(dc_ref)
            dd_ref[...] = jnp.zeros_like(dd_ref)
            acc_ref[...] = jnp.zeros_like(acc_ref)
            _fill_powers(lam_ref, w_ref, nj, per)

        for j in range(nj):
            st = slice(j * SSM_TILE_ST, (j + 1) * SSM_TILE_ST)
            ch = slice(j * SSM_TILE_CH, (j + 1) * SSM_TILE_CH)
            lr, li = lam_ref[0:1, st], lam_ref[1:2, st]
            up = _to_segment_order(u_refs[j], perm_ref)
            upb = up.astype(BF16)
            dyp = _to_segment_order(dyg_ref[:, ch] * _gelu_grad(y_ref[:, ch]), perm_ref)
            dyb = dyp.astype(BF16)
            _, (in_r, in_i) = _segment_scan(_dot(upb, b_ref[j]), xs_ref, w_ref.at[j], lr, li,
                                            xin_ref[0, 0:1, st], xin_ref[0, 1:2, st], conj=False, reverse=False)
            (gr, gi), _ = _segment_scan(_dot_nt(dyb, c_ref[j]), gs_ref, w_ref.at[j], lr, li,
                                        carry_ref[0:1, st], carry_ref[1:2, st], conj=True, reverse=True)
            carry_ref[0:1, st] = gr
            carry_ref[1:2, st] = gi
            xs, gs = xs_ref[...], gs_ref[...]
            xsr, xsi, gsr, gsi = xs[:, :half], xs[:, half:], gs[:, :half], gs[:, half:]
            pxr = jnp.concatenate([in_r, xsr[:t_rows - seg]], axis=0)
            pxi = jnp.concatenate([in_i, xsi[:t_rows - seg]], axis=0)
            dl_r = gsr * pxr + gsi * pxi
            dl_i = gsi * pxr - gsr * pxi
            acc_ref[0, :, st] += jnp.sum(dl_r.reshape(per, seg, half), axis=0)
            acc_ref[1, :, st] += jnp.sum(dl_i.reshape(per, seg, half), axis=0)
            gx = gs.astype(BF16)
            dup = _dot_nt(gx, b_ref[j]) + d_ref[:, ch] * dyp

            def store(i, rows, ch=ch):
                du_ref[i * per:(i + 1) * per, ch] = rows.astype(BF16)

            _to_time_order(dup, perm_ref, store)
            db_ref[j] += _dot_tn(upb, gx)
            dc_ref[j] += _dot_tn(xs.astype(BF16), dyb)
            dd_ref[:, ch] += jnp.sum(dyp * up, axis=0, keepdims=True)

        @pl.when(pl.program_id(0) == nc - 1)
        def _():
            dlam_ref[...] = jnp.sum(acc_ref[...], axis=1)

    rev = lambda c: nc - 1 - c
    u_specs = [pl.BlockSpec((t_rows, SSM_TILE_CH), lambda c, k=k: (rev(c), u_off // SSM_TILE_CH + k))
               for k in range(nj)]
    full3 = lambda shape: pl.BlockSpec(shape, lambda c: (0, 0, 0))
    full2 = lambda shape: pl.BlockSpec(shape, lambda c: (0, 0))
    rows = pl.BlockSpec((t_rows, w), lambda c: (rev(c), 0))
    outs, carried = _call(
        body, name="ssm_bwd", grid=(nc,),
        in_specs=u_specs + [rows, rows, pl.BlockSpec((1, 2, ns), lambda c: (rev(c), 0, 0)),
                            full3(bmat.shape), full3(cmat.shape), full2(lam.shape), full2(dskip.shape)],
        out_specs=[rows, full3(bmat.shape), full3(cmat.shape), full2(lam.shape), full2(dskip.shape)],
        out_shape=[jax.ShapeDtypeStruct((s, w), BF16), jax.ShapeDtypeStruct(bmat.shape, F32),
                   jax.ShapeDtypeStruct(cmat.shape, F32), jax.ShapeDtypeStruct(lam.shape, F32),
                   jax.ShapeDtypeStruct(dskip.shape, F32)],
        scratch_shapes=[pltpu.VMEM((2, ns), F32), pltpu.VMEM((nj, t_rows, 2 * SSM_TILE_ST), F32),
                        pltpu.VMEM((t_rows, 2 * SSM_TILE_ST), F32), pltpu.VMEM((t_rows, 2 * SSM_TILE_ST), F32),
                        pltpu.VMEM((t_rows, SSM_TILE_CH), F32), pltpu.VMEM((2, SSM_SEGMENTS, ns), F32)],
        semantics=("arbitrary",), args=[*([z] * nj), y, dyg, xin, bmat, cmat, lam, dskip], side=side)
    return outs if side is None else (outs, carried)


def _adam_math(w, g, m, v):
    m = ADAM_B1 * m + (1.0 - ADAM_B1) * g
    v = ADAM_B2 * v + (1.0 - ADAM_B2) * (g * g)
    m_hat = m / (1.0 - ADAM_B1 ** ADAM_STEP)
    v_hat = v / (1.0 - ADAM_B2 ** ADAM_STEP)
    delta = -ADAM_LR * (m_hat / (jnp.sqrt(v_hat) + ADAM_EPS) + ADAM_WD * w)
    return delta, m, v


def _adam_rows(r, c):
    for tr in (512, 256, 128, 64, 32, 16, 8):
        if r % tr == 0 and tr * c * 4 <= (1 << 20):
            return tr
    return r


def _adamw_big(w, p_mine, p_sib, m, v, name):
    r, c = w.shape
    tr = _adam_rows(r, c)

    def body(w_ref, a_ref, b_ref, m_ref, v_ref, g_ref, d_ref, nm_ref, nv_ref):
        g = a_ref[...] + b_ref[...]
        g_ref[...] = g
        d_ref[...], nm_ref[...], nv_ref[...] = _adam_math(w_ref[...], g, m_ref[...], v_ref[...])

    blk = pl.BlockSpec((tr, c), lambda i: (i, 0))
    return pl.pallas_call(body, name=f"adamw_{name}", grid=(r // tr,), in_specs=[blk] * 5, out_specs=[blk] * 4,
                          out_shape=[jax.ShapeDtypeStruct((r, c), F32)] * 4,
                          compiler_params=_params(("parallel",)))(w, p_mine, p_sib, m, v)


def _adamw_small(w, parts, m, v):
    r, c = w.shape
    n_dev = parts.shape[0]

    def body(w_ref, p_ref, m_ref, v_ref, g_ref, d_ref, nm_ref, nv_ref):
        g = p_ref[0]
        for k in range(1, n_dev):
            g = g + p_ref[k]
        g_ref[...] = g
        d_ref[...], nm_ref[...], nv_ref[...] = _adam_math(w_ref[...], g, m_ref[...], v_ref[...])

    blk = pl.BlockSpec((r, c), lambda i: (0, 0))
    return pl.pallas_call(body, name="adamw_small", grid=(1,),
                          in_specs=[blk, pl.BlockSpec((n_dev, r, c), lambda i: (0, 0, 0)), blk, blk],
                          out_specs=[blk] * 4, out_shape=[jax.ShapeDtypeStruct((r, c), F32)] * 4,
                          compiler_params=_params(("arbitrary",)))(w, parts, m, v)


def _cast_bf16(w, name):
    r, c = w.shape
    tr = _adam_rows(r, c)

    def body(w_ref, o_ref):
        o_ref[...] = w_ref[...].astype(BF16)

    blk = pl.BlockSpec((tr, c), lambda i: (i, 0))
    return pl.pallas_call(body, name=f"cast_{name}", grid=(r // tr,), in_specs=[blk], out_specs=blk,
                          out_shape=jax.ShapeDtypeStruct((r, c), BF16), compiler_params=_params(("parallel",)))(w)


def _sum_slots(recv, name):
    _, r, c = recv.shape
    tr = _adam_rows(r, c)

    def body(p_ref, o_ref):
        acc = p_ref[0].astype(F32)
        for k in range(1, N_CHIPS):
            acc = acc + p_ref[k].astype(F32)
        o_ref[...] = acc

    return pl.pallas_call(body, name=f"sum_{name}", grid=(r // tr,),
                          in_specs=[pl.BlockSpec((N_CHIPS, tr, c), lambda i: (0, i, 0))],
                          out_specs=pl.BlockSpec((tr, c), lambda i: (i, 0)),
                          out_shape=jax.ShapeDtypeStruct((r, c), F32), compiler_params=_params(("parallel",)))(recv)


BIG_WEIGHTS = ("w_in", "w_attn_up", "w_glu_v", "w_glu_g", "w_out", "w_ffn_gate", "w_ffn_up", "w_ffn_down")
COL_SHARDED = ("w_in", "w_attn_up", "w_glu_v", "w_glu_g", "w_ffn_gate", "w_ffn_up")


def _aligned(v, m):
    return v if isinstance(v, int) else pl.multiple_of(v, m)


def _shard_of(ref, name, j, shard_shape, half=None):
    r, c = shard_shape
    rows = r if half is None else r // 2
    row0 = 0 if half is None else half * rows
    if name in COL_SHARDED:
        return ref.at[pl.ds(_aligned(row0, 16), rows), pl.ds(_aligned(j * c, 128), c)]
    return ref.at[pl.ds(_aligned(j * r + row0, 16), rows), :]


def _other_chips():
    x, y = lax.axis_index("x"), lax.axis_index("y")
    return [(1 - x, y), (x, 1 - y), (1 - x, 1 - y)]


def _dma_sems(n, arrays):
    return [pltpu.SemaphoreType.DMA((n, 3))] * arrays + [pltpu.SemaphoreType.DMA((n,))]


def _gather_side(shards):
    names = list(shards)
    n = len(names)
    full_shapes = []
    for k in names:
        r, c = shards[k].shape
        full_shapes.append((r, c * N_CHIPS) if k in COL_SHARDED else (r * N_CHIPS, c))

    def build(src, dst, sems):
        send_sems, recv_sems, pass_send_sems, pass_recv_sems, local_sems = sems
        x, y, c = lax.axis_index("x"), lax.axis_index("y"), lax.axis_index("c")
        me = 2 * x + y
        locals_, sends, arrivals, forwards, passed_on = [], [], [], [], []
        for i, k in enumerate(names):
            shape = shards[k].shape
            half_rows = shape[0] // 2
            locals_.append(pltpu.make_async_copy(src[i], _shard_of(dst[i], k, me, shape), local_sems.at[i]))
            my_half = src[i].at[pl.ds(_aligned(c * half_rows, 16), half_rows), :]
            for p, (px, py) in enumerate(_other_chips()):
                peer = 2 * px + py
                landed = _shard_of(dst[i], k, peer, shape, half=c)
                sends.append(pltpu.make_async_remote_copy(
                    src_ref=my_half, dst_ref=_shard_of(dst[i], k, me, shape, half=c), send_sem=send_sems.at[i, p],
                    recv_sem=recv_sems.at[i, p], device_id=(px, py, c), device_id_type=MESH))
                arrivals.append(pltpu.make_async_remote_copy(
                    src_ref=my_half, dst_ref=landed, send_sem=send_sems.at[i, p],
                    recv_sem=recv_sems.at[i, p], device_id=(px, py, c), device_id_type=MESH))
                forwards.append(pltpu.make_async_remote_copy(
                    src_ref=landed, dst_ref=landed, send_sem=pass_send_sems.at[i, p],
                    recv_sem=pass_recv_sems.at[i, p], device_id=(x, y, 1 - c), device_id_type=MESH))
                passed_on.append(pltpu.make_async_remote_copy(
                    src_ref=landed, dst_ref=_shard_of(dst[i], k, peer, shape, half=1 - c),
                    send_sem=pass_send_sems.at[i, p], recv_sem=pass_recv_sems.at[i, p],
                    device_id=(x, y, 1 - c), device_id_type=MESH))
        return locals_, sends, arrivals, forwards, passed_on

    return _Side([shards[k] for k in names], [jax.ShapeDtypeStruct(s, BF16) for s in full_shapes], _dma_sems(n, 4), build)


def _scatter_side(grads, shard_shapes):
    names = list(grads)
    n = len(names)

    def build(src, dst, sems):
        send_sems, recv_sems, local_sems = sems
        x, y, c = lax.axis_index("x"), lax.axis_index("y"), lax.axis_index("c")
        me = 2 * x + y
        locals_, sends, arrivals = [], [], []
        for i, k in enumerate(names):
            shape = shard_shapes[k]
            locals_.append(pltpu.make_async_copy(_shard_of(src[i], k, me, shape), dst[i].at[me], local_sems.at[i]))
            for p, (px, py) in enumerate(_other_chips()):
                peer = 2 * px + py
                sends.append(pltpu.make_async_remote_copy(
                    src_ref=_shard_of(src[i], k, peer, shape), dst_ref=dst[i].at[me], send_sem=send_sems.at[i, p],
                    recv_sem=recv_sems.at[i, p], device_id=(px, py, c), device_id_type=MESH))
                arrivals.append(pltpu.make_async_remote_copy(
                    src_ref=_shard_of(src[i], k, peer, shape), dst_ref=dst[i].at[peer], send_sem=send_sems.at[i, p],
                    recv_sem=recv_sems.at[i, p], device_id=(px, py, c), device_id_type=MESH))
        return locals_, sends, arrivals, [None] * len(arrivals), []

    return _Side([grads[k] for k in names],
                 [jax.ShapeDtypeStruct((N_CHIPS,) + tuple(shard_shapes[k]), BF16) for k in names], _dma_sems(n, 2), build)


def _put_cols(dz, src, col_off):
    s, w = src.shape
    tr = _pick(s, (2048, 1024, 512, 256, 128, 64, 8))
    tc = _pick(math.gcd(w, col_off), (1024, 512, 256, 128))
    off = col_off // tc

    def body(src_ref, dz_ref, o_ref):
        del dz_ref
        o_ref[...] = src_ref[...].astype(o_ref.dtype)

    return pl.pallas_call(
        body, name="put_cols", grid=(s // tr, w // tc),
        in_specs=[pl.BlockSpec((tr, tc), lambda i, j: (i, j)), pl.BlockSpec(memory_space=pl.ANY)],
        out_specs=pl.BlockSpec((tr, tc), lambda i, j: (i, off + j)),
        out_shape=jax.ShapeDtypeStruct(dz.shape, dz.dtype), input_output_aliases={1: 0},
        compiler_params=_params(("parallel", "parallel")))(src, dz)


def _swap_side(parts):
    n = len(parts)

    def build(src, dst, sems):
        send_sems, recv_sems = sems
        sibling = (lax.axis_index("x"), lax.axis_index("y"), 1 - lax.axis_index("c"))
        copies = [pltpu.make_async_remote_copy(src_ref=src[i], dst_ref=dst[i], send_sem=send_sems.at[i],
                                               recv_sem=recv_sems.at[i], device_id=sibling, device_id_type=MESH)
                  for i in range(n)]
        return [], copies, copies, [None] * n, []

    return _Side(parts, [jax.ShapeDtypeStruct(p.shape, F32) for p in parts],
                 [pltpu.SemaphoreType.DMA((n,)), pltpu.SemaphoreType.DMA((n,))], build)


def _share_side(packed):
    r, c = packed.shape

    def build(src, dst, sems):
        send_sems, recv_sems, local_sem = sems
        x, y, cc = lax.axis_index("x"), lax.axis_index("y"), lax.axis_index("c")
        me = 4 * x + 2 * y + cc
        own = pltpu.make_async_copy(src[0], dst[0].at[me], local_sem)
        sends, arrivals = [], []
        flips = [(fx, fy, fc) for fx in range(2) for fy in range(2) for fc in range(2) if fx or fy or fc]
        for p, (fx, fy, fc) in enumerate(flips):
            px, py, pc = x ^ fx, y ^ fy, cc ^ fc
            sends.append(pltpu.make_async_remote_copy(
                src_ref=src[0], dst_ref=dst[0].at[me], send_sem=send_sems.at[p], recv_sem=recv_sems.at[p],
                device_id=(px, py, pc), device_id_type=MESH))
            arrivals.append(pltpu.make_async_remote_copy(
                src_ref=src[0], dst_ref=dst[0].at[4 * px + 2 * py + pc], send_sem=send_sems.at[p],
                recv_sem=recv_sems.at[p], device_id=(px, py, pc), device_id_type=MESH))
        return [own], sends, arrivals, [None] * len(arrivals), []

    return _Side([packed], [jax.ShapeDtypeStruct((8, r, c), F32)],
                 [pltpu.SemaphoreType.DMA((7,)), pltpu.SemaphoreType.DMA((7,)), pltpu.SemaphoreType.DMA], build)


SMALL_WEIGHTS = ("norm_mix_pre", "ssm_a_re", "ssm_a_im", "ssm_log_dt", "ssm_b_re", "ssm_b_im", "ssm_c_re", "ssm_c_im",
                 "ssm_d", "norm_mix_post", "norm_ffn_pre", "norm_ffn_post")
WEIGHT_ORDER = ("norm_mix_pre", "w_in", "w_attn_up", "ssm_a_re", "ssm_a_im", "ssm_log_dt", "ssm_b_re", "ssm_b_im",
                "ssm_c_re", "ssm_c_im", "ssm_d", "w_glu_v", "w_glu_g", "w_out", "norm_mix_post", "norm_ffn_pre",
                "w_ffn_gate", "w_ffn_up", "w_ffn_down", "norm_ffn_post")
PACK_LANES = 128
PACK_ROWS = 8
PACK_GROUPS = (SMALL_WEIGHTS[:1], SMALL_WEIGHTS[1:])


def _pack_group(arrs, names):
    flat = jnp.concatenate([arrs[k].reshape(-1) for k in names])
    pad = -flat.shape[0] % (PACK_LANES * PACK_ROWS)
    return jnp.pad(flat, (0, pad)).reshape(-1, PACK_LANES)


def _pack_small(arrs):
    return jnp.concatenate([_pack_group(arrs, names) for names in PACK_GROUPS], axis=0)


def _unpack_small(packed, like):
    out, row = {}, 0
    for names in PACK_GROUPS:
        rows = _pack_group(like, names).shape[0]
        flat, pos = packed[row:row + rows].reshape(-1), 0
        for k in names:
            n = like[k].size
            out[k] = flat[pos:pos + n].reshape(like[k].shape)
            pos += n
        row += rows
    return out


def _local_step(x, target, big, small, shards=None, shard_shapes=None):
    s, d = x.shape
    big, grads, slots = dict(big), {}, {}
    carry = shards is not None

    def gathering(names, call):
        if not carry:
            return call(None)
        res, got = call(_gather_side({k: shards[k] for k in names}))
        big.update(zip(names, got))
        return res

    def scattering(names, call):
        if not carry:
            return call(None)
        res, got = call(_scatter_side({k: grads[k] for k in names}, shard_shapes))
        slots.update(zip(names, got))
        return res

    u_off = 3 * HQ
    gate_off = u_off + d // 2
    g1, g2, g3, g4 = (small[k][0:1] for k in ("norm_mix_pre", "norm_mix_post", "norm_ffn_pre", "norm_ffn_post"))
    ssm_names = ("ssm_a_re", "ssm_a_im", "ssm_log_dt", "ssm_b_re", "ssm_b_im", "ssm_c_re", "ssm_c_im")
    (lam, bmat, cmat), ssm_vjp = jax.vjp(_ssm_prepare, *[small[k][0] for k in ssm_names])
    bmat, cmat = bmat.astype(BF16), cmat.astype(BF16)
    dskip = small["ssm_d"][0:1]

    h1 = _norm_in(x, g1)
    z = gathering(("w_attn_up", "w_glu_v", "w_glu_g", "w_out", "w_ffn_gate"),
                  lambda side: _mm(h1, big["w_in"], "nn", F32, "in_proj", side=side))
    y, yg, xin = gathering(("w_ffn_up",), lambda side: _ssm_fwd(z, bmat, cmat, lam, dskip, u_off, side=side))
    qkv = [_dilate_qkv(z, g, dil) for g, dil in enumerate(ATTN_DILATIONS)]
    outs, lses = zip(*[_attn_fwd(qkv[g], g, dil) for g, dil in enumerate(ATTN_DILATIONS)])
    attn = _attn_merge(outs, lses)
    merged, ab, gv, gg = _mm_fused(
        [attn, yg], [big["w_attn_up"], big["w_glu_v"], big["w_glu_g"]], [(0, 0), (1, 1), (1, 2)], "nn",
        [BF16, BF16, BF16, BF16], "branches_merge", extras=[(z, gate_off), (z, gate_off + d)], epilogue=_gates_epilogue)
    mo = _mm(merged, big["w_out"], "nn", F32, "mix_out")
    x2, h2 = _norm_mid(x, mo, g2, g3)
    act, fg, fu = gathering(("w_ffn_down",), lambda side: _mm_fused(
        [h2], [big["w_ffn_gate"], big["w_ffn_up"]], [(0, 0), (0, 1)], "nn", [BF16, BF16, BF16], "ffn_up_act",
        epilogue=_swiglu_epilogue, side=side))
    f = _mm(act, big["w_ffn_down"], "nn", F32, "ffn_down")
    loss, dout, df, dg4 = _loss_head(x2, f, g4, target)

    dfg, dfu = _mm_fused([df], [big["w_ffn_down"]], [(0, 0)], "nt", [BF16, BF16], "d_ffn_act",
                         extras=[(fg, 0), (fu, 0)], epilogue=_swiglu_bwd_epilogue)
    grads["w_ffn_down"] = _mm_kloop(act, df, "tn", BF16, "dw_ffn_down")
    grads["w_ffn_gate"] = scattering(("w_ffn_down",), lambda side: _mm_kloop(h2, dfg, "tn", BF16, "dw_ffn_gate", side=side))
    grads["w_ffn_up"] = scattering(("w_ffn_gate",), lambda side: _mm_kloop(h2, dfu, "tn", BF16, "dw_ffn_up", side=side))
    dh2 = scattering(("w_ffn_up",), lambda side: _mm_kloop(dfg, big["w_ffn_gate"], "nt", F32, "d_h2", side=side,
                                                          second=(dfu, big["w_ffn_up"])))
    dx2, dmo, dg2, dg3 = _norm_mid_bwd(x2, mo, g2, g3, dout, dh2)
    dz, dgs, dab, dgv, dgg = _mm_fused(
        [dmo], [big["w_out"]], [(0, 0)], "nt", [BF16] * 5, "d_merged_gates",
        extras=[(z, gate_off), (z, gate_off + d), (ab, 0), (gv, 0), (gg, 0)], epilogue=_gates_bwd_epilogue,
        out_place=[(z.shape[1], gate_off), None, None, None, None])
    dz = _put_cols(dz, dgs, gate_off + d)
    grads["w_out"] = _mm_kloop(merged, dmo, "tn", BF16, "dw_out")
    dyg = _mm_fused([dgv, dgg], [big["w_glu_v"], big["w_glu_g"]], [(0, 0), (1, 1)], "nt", [F32], "d_yg",
                    epilogue=_sum_epilogue)[0]
    grads["w_glu_v"] = _mm_kloop(yg, dgv, "tn", BF16, "dw_glu_v")
    grads["w_glu_g"] = _mm_kloop(yg, dgg, "tn", BF16, "dw_glu_g")
    du, dbmat, dcmat, dlam, dd = scattering(
        ("w_out", "w_glu_v", "w_glu_g"),
        lambda side: _ssm_bwd(z, y, dyg, xin, bmat, cmat, lam, dskip, u_off, side=side))
    dz = _put_cols(dz, du, u_off)
    dattn = _mm(dab, big["w_attn_up"], "nt", F32, "d_attn")
    grads["w_attn_up"] = _mm_kloop(attn, dab, "tn", BF16, "dw_attn_up")
    merged_bwd = _attn_merge_bwd(outs, lses, dattn)
    mine, theirs = {}, {}
    for g, dil in enumerate(ATTN_DILATIONS):
        side = None
        if carry and g == 0:
            mine = {k: _sum_slots(slots[k], k) for k in slots}
            side = _swap_side(list(mine.values()))
        dqkv = _attn_bwd(qkv[g], merged_bwd[g], lses[g], merged_bwd[3 + g], g, dil, side=side)
        if side is not None:
            dqkv, got = dqkv
            theirs = dict(zip(mine, got))
        dz = _undilate_dqkv(dqkv, dz, g, dil)
    small_grads = dict(zip(ssm_names, (t[None] for t in ssm_vjp((dlam, dbmat, dcmat)))))
    small_grads.update(norm_mix_post=dg2, norm_ffn_pre=dg3, norm_ffn_post=dg4, ssm_d=dd)
    if carry:
        side = _join_sides(_scatter_side({"w_attn_up": grads["w_attn_up"]}, shard_shapes),
                           _share_side(_pack_group(small_grads, PACK_GROUPS[1])))
        grads["w_in"], (slots["w_attn_up"], shared) = _mm_kloop(h1, dz, "tn", BF16, "dw_in", side=side)
    else:
        grads["w_in"] = _mm_kloop(h1, dz, "tn", BF16, "dw_in")
    dh1 = scattering(("w_in",), lambda side: _mm_kloop(dz, big["w_in"], "nt", F32, "d_h1", side=side))
    grad_x, dg1 = _norm_in_bwd(x, g1, dh1, dx2)
    small_grads["norm_mix_pre"] = dg1
    if carry:
        return loss[0, 0], grad_x, (slots, mine, theirs), (dg1, shared)
    return loss[0, 0], grad_x, grads, small_grads


def kernel(x, norm_mix_pre, w_in, w_attn_up, ssm_a_re, ssm_a_im, ssm_log_dt, ssm_b_re, ssm_b_im, ssm_c_re, ssm_c_im, ssm_d, w_glu_v, w_glu_g, w_out, norm_mix_post, norm_ffn_pre, w_ffn_gate, w_ffn_up, w_ffn_down, norm_ffn_post, loss_target, m_norm_mix_pre, m_w_in, m_w_attn_up, m_ssm_a_re, m_ssm_a_im, m_ssm_log_dt, m_ssm_b_re, m_ssm_b_im, m_ssm_c_re, m_ssm_c_im, m_ssm_d, m_w_glu_v, m_w_glu_g, m_w_out, m_norm_mix_post, m_norm_ffn_pre, m_w_ffn_gate, m_w_ffn_up, m_w_ffn_down, m_norm_ffn_post, v_norm_mix_pre, v_w_in, v_w_attn_up, v_ssm_a_re, v_ssm_a_im, v_ssm_log_dt, v_ssm_b_re, v_ssm_b_im, v_ssm_c_re, v_ssm_c_im, v_ssm_d, v_w_glu_v, v_w_glu_g, v_w_out, v_norm_mix_post, v_norm_ffn_pre, v_w_ffn_gate, v_w_ffn_up, v_w_ffn_down, v_norm_ffn_post):
    given = dict(locals())
    w = {k: given[k] for k in WEIGHT_ORDER}
    m = {k: given["m_" + k] for k in WEIGHT_ORDER}
    v = {k: given["v_" + k] for k in WEIGHT_ORDER}

    shards = {k: _cast_bf16(w[k][0], k) for k in BIG_WEIGHTS}
    shard_shapes = {k: w[k].shape[1:] for k in BIG_WEIGHTS}
    big = {"w_in": _run_side(_gather_side({"w_in": shards["w_in"]}), "gather_w_in")[0]}

    loss, grad_x, (slots, mine, theirs), small_grads = _local_step(
        x[0], loss_target[0], big, {k: w[k] for k in SMALL_WEIGHTS}, shards, shard_shapes)
    loss = lax.psum(loss, MESH_AXES)

    last = [k for k in BIG_WEIGHTS if k not in mine]
    mine.update({k: _sum_slots(slots[k], k) for k in last})
    theirs.update(zip(last, _run_side(_swap_side([mine[k] for k in last]), "swap_last_grads")))
    out_g, out_d, out_m, out_v = {}, {}, {}, {}
    for k in BIG_WEIGHTS:
        res = _adamw_big(w[k][0], mine[k], theirs[k], m[k][0], v[k][0], k)
        out_g[k], out_d[k], out_m[k], out_v[k] = (t[None] for t in res)

    pick = lambda tree: {k: tree[k] for k in SMALL_WEIGHTS}
    dg1, shared = small_grads
    late = _run_side(_share_side(_pack_group({"norm_mix_pre": dg1}, PACK_GROUPS[0])), "share_last_grad")[0]
    parts = jnp.concatenate([late, shared], axis=1)
    res = _adamw_small(_pack_small(pick(w)), parts, _pack_small(pick(m)), _pack_small(pick(v)))
    for dst, packed in zip((out_g, out_d, out_m, out_v), res):
        dst.update(_unpack_small(packed, pick(w)))

    return (loss, grad_x[None], *[out_g[k] for k in WEIGHT_ORDER], *[out_d[k] for k in WEIGHT_ORDER],
            *[out_m[k] for k in WEIGHT_ORDER], *[out_v[k] for k in WEIGHT_ORDER])
```

```python
import functools
import math

import jax
import jax.numpy as jnp
from jax import lax
from jax.experimental import pallas as pl
from jax.experimental.pallas import tpu as pltpu

F32 = jnp.float32
BF16 = jnp.bfloat16

EPS = 1e-6
HEAD_DIM = 128
HEADS_PER_GROUP = 4
ATTN_DILATIONS = (1, 4, 16)
ATTN_BLK = 128
N_ATTN_HEADS = HEADS_PER_GROUP * len(ATTN_DILATIONS)
GROUP_W = HEADS_PER_GROUP * HEAD_DIM
HQ = N_ATTN_HEADS * HEAD_DIM
SSM_GROUP = 16
SSM_STATE = 64
SSM_TILE_CH = 128
SSM_TILE_ST = SSM_TILE_CH // SSM_GROUP * SSM_STATE
ADAM_LR = 0.001
ADAM_B1 = 0.9
ADAM_B2 = 0.999
ADAM_EPS = 1e-08
ADAM_WD = 0.01
ADAM_STEP = 10
NEG_BIG = -1e30
V7X_VMEM_LIMIT = 56 * 1024 * 1024
MESH_AXES = ("x", "y", "c")
N_CHIPS = 4


def _pick(n, cands):
    for c in cands:
        if n % c == 0:
            return c
    raise ValueError(f"no tile of {cands} divides {n}")


def _params(sem):
    return pltpu.CompilerParams(dimension_semantics=sem, vmem_limit_bytes=V7X_VMEM_LIMIT)


HBM = pl.BlockSpec(memory_space=pl.ANY)
MESH = pl.DeviceIdType.MESH


class _Side:
    def __init__(self, srcs, out_shapes, sem_shapes, build, aliases=None):
        self.srcs, self.out_shapes, self.sem_shapes, self.build = list(srcs), list(out_shapes), list(sem_shapes), build
        self.aliases = dict(aliases or {})

    def start(self, src, dst, sems):
        local, sends = self.build(src, dst, sems)[:2]
        for cp in local + sends:
            cp.start()

    def wait(self, src, dst, sems):
        local, sends, arrivals, forwards, passed_on = self.build(src, dst, sems)
        for cp, forward in zip(arrivals, forwards):
            cp.wait_recv()
            if forward is not None:
                forward.start()
        for cp in passed_on:
            cp.wait_recv()
        for cp in sends + [f for f in forwards if f is not None]:
            cp.wait_send()
        for cp in local:
            cp.wait()


def _join_sides(a, b):
    ns, no, nm = len(a.srcs), len(a.out_shapes), len(a.sem_shapes)

    def build(src, dst, sems):
        ra, rb = a.build(src[:ns], dst[:no], sems[:nm]), b.build(src[ns:], dst[no:], sems[nm:])
        return tuple(p + q for p, q in zip(ra, rb))

    aliases = {**a.aliases, **{ns + k: no + v for k, v in b.aliases.items()}}
    return _Side(a.srcs + b.srcs, a.out_shapes + b.out_shapes, a.sem_shapes + b.sem_shapes, build, aliases)


def _call(body, *, name, grid, in_specs, out_specs, out_shape, semantics, args, scratch_shapes=(), side=None, **kw):
    in_specs, out_specs, out_shape, scratch_shapes = list(in_specs), list(out_specs), list(out_shape), list(scratch_shapes)
    if side is None:
        res = pl.pallas_call(body, name=name, grid=grid, in_specs=in_specs, out_specs=out_specs, out_shape=out_shape,
                             scratch_shapes=scratch_shapes, compiler_params=_params(semantics), **kw)(*args)
        return list(res), []
    n_in, n_out, n_scr = len(in_specs), len(out_specs), len(scratch_shapes)
    ns_in, ns_out = len(side.srcs), len(side.out_shapes)

    def carrying(*refs):
        ins, s_in = refs[:n_in], refs[n_in:n_in + ns_in]
        o0 = n_in + ns_in
        outs, s_out = refs[o0:o0 + n_out], refs[o0 + n_out:o0 + n_out + ns_out]
        c0 = o0 + n_out + ns_out
        scr, sems = refs[c0:c0 + n_scr], refs[c0 + n_scr:]
        ids = [pl.program_id(a) for a in range(len(grid))]
        first = functools.reduce(jnp.logical_and, [i == 0 for i in ids])
        last = functools.reduce(jnp.logical_and, [i == g - 1 for i, g in zip(ids, grid)])

        @pl.when(first)
        def _():
            side.start(s_in, s_out, sems)

        body(*ins, *outs, *scr)

        @pl.when(last)
        def _():
            side.wait(s_in, s_out, sems)

    res = pl.pallas_call(
        carrying, name=name, grid=grid, in_specs=in_specs + [HBM] * ns_in, out_specs=out_specs + [HBM] * ns_out,
        out_shape=out_shape + side.out_shapes, scratch_shapes=scratch_shapes + side.sem_shapes,
        input_output_aliases={n_in + k: n_out + v for k, v in side.aliases.items()},
        compiler_params=pltpu.CompilerParams(dimension_semantics=("arbitrary",) * len(grid),
                                             vmem_limit_bytes=V7X_VMEM_LIMIT, has_side_effects=True), **kw,
    )(*args, *side.srcs)
    return list(res[:n_out]), list(res[n_out:])


def _run_side(side, name):
    ns, no = len(side.srcs), len(side.out_shapes)

    def body(*refs):
        src, dst, sems = refs[:ns], refs[ns:ns + no], refs[ns + no:]
        side.start(src, dst, sems)
        side.wait(src, dst, sems)

    return list(pl.pallas_call(body, name=name, in_specs=[HBM] * ns, out_specs=[HBM] * no, out_shape=side.out_shapes,
                               scratch_shapes=side.sem_shapes,
                               compiler_params=pltpu.CompilerParams(has_side_effects=True))(*side.srcs))


_DOT_DIMS = {"nn": (((1,), (0,)), ((), ())), "nt": (((1,), (1,)), ((), ())), "tn": (((0,), (0,)), ((), ()))}


MM_VMEM_BUDGET = 44 * 1024 * 1024
MM_STEP_BYTES = 1 << 20
MM_ACC_BYTES = 4


def _size(dtype):
    return jnp.dtype(dtype).itemsize


def _mm_fused(as_, bs, pairs, mode, out_dtypes, name, extras=(), epilogue=None, side=None, out_place=None):
    M = as_[0].shape[0]
    N = bs[0].shape[1] if mode == "nn" else bs[0].shape[0]
    ks_a = [a.shape[1] for a in as_]
    ks_b = [b.shape[0] if mode == "nn" else b.shape[1] for b in bs]
    if epilogue is None:
        epilogue = lambda rs, es: rs
    offs = [off for _, off in extras]
    place = list(out_place) if out_place else [None] * len(out_dtypes)
    offs_all = offs + [p[1] for p in place if p is not None]
    best = None
    for tm in (2048, 1024, 512, 256, 128):
        for tn in (2048, 1024, 512, 256, 128):
            if M % tm or N % tn or any(off % tn for off in offs_all):
                continue
            vmem = (sum(2 * tm * k * 2 for k in ks_a) + sum(2 * k * tn * 2 for k in ks_b)
                    + sum(2 * tm * tn * _size(d) for d in out_dtypes) + sum(2 * tm * tn * _size(e.dtype) for e, _ in extras)
                    + len(pairs) * tm * tn * 4)
            cost = sum(k * N * 2 for k in ks_b) * (M // tm) + (M // tm) * (N // tn) * MM_STEP_BYTES
            if vmem <= MM_VMEM_BUDGET and (best is None or cost < best[0]):
                best = (cost, tm, tn)
    _, tm, tn = best
    na, nb, ne, no = len(as_), len(bs), len(extras), len(out_dtypes)
    dims = _DOT_DIMS[mode]

    def body(*refs):
        a_refs, b_refs = refs[:na], refs[na:na + nb]
        e_refs, o_refs = refs[na + nb:na + nb + ne], refs[na + nb + ne:]
        rs = [lax.dot_general(a_refs[ai][...], b_refs[bi][...], dims, preferred_element_type=F32) for ai, bi in pairs]
        outs = epilogue(rs, [e[...] for e in e_refs])
        for o_ref, o in zip(o_refs, outs):
            o_ref[...] = o.astype(o_ref.dtype)

    a_specs = [pl.BlockSpec((tm, k), lambda i, j: (i, 0)) for k in ks_a]
    if mode == "nn":
        b_specs = [pl.BlockSpec((k, tn), lambda i, j: (0, j)) for k in ks_b]
    else:
        b_specs = [pl.BlockSpec((tn, k), lambda i, j: (j, 0)) for k in ks_b]
    e_specs = [pl.BlockSpec((tm, tn), lambda i, j, o=off // tn: (i, o + j)) for off in offs]
    o_specs = [pl.BlockSpec((tm, tn), lambda i, j, o=(p[1] // tn if p else 0): (i, o + j)) for p in place]
    outs, carried = _call(
        body, name=name, grid=(M // tm, N // tn), in_specs=a_specs + b_specs + e_specs, out_specs=o_specs,
        out_shape=[jax.ShapeDtypeStruct((M, p[0] if p else N), d) for d, p in zip(out_dtypes, place)],
        semantics=("parallel", "arbitrary"),
        args=[*as_, *bs, *[e for e, _ in extras]], side=side)
    return outs if side is None else (outs, carried)


def _mm(a, b, mode, out_dtype, name, side=None):
    res = _mm_fused([a], [b], [(0, 0)], mode, [out_dtype], name, side=side)
    return res[0] if side is None else (res[0][0], res[1])


def _mm_kloop(a, b, mode, out_dtype, name, second=None, side=None):
    if mode == "nn":
        (M, K), (_, N) = a.shape, b.shape
    elif mode == "nt":
        (M, K), (N, _) = a.shape, b.shape
    else:
        (K, M), (_, N) = a.shape, b.shape
    products = 1 if second is None else 2
    best = None
    for tm in (2816, 2048, 1408, 1024, 512, 256, 128):
        for tn in (2816, 2432, 2048, 1408, 1024, 512, 256, 128):
            for tk in (2816, 2432, 2048, 1408, 1024, 512, 256, 128):
                if M % tm or N % tn or K % tk:
                    continue
                vmem = 2 * tm * tn * 4 + 2 * tm * tn * _size(out_dtype) + products * 2 * tk * (tm + tn) * 2
                steps = (M // tm) * (N // tn) * (K // tk)
                cost = (K * M * 2 * (N // tn) + K * N * 2 * (M // tm) + steps * MM_STEP_BYTES
                        + steps * tm * tn * MM_ACC_BYTES)
                if vmem <= MM_VMEM_BUDGET and (best is None or cost < best[0]):
                    best = (cost, tm, tn, tk)
    _, tm, tn, tk = best
    nk = K // tk
    dims = _DOT_DIMS[mode]

    def body(*refs):
        o_ref, acc_ref = refs[-2:]
        k = pl.program_id(2)

        @pl.when(k == 0)
        def _():
            acc_ref[...] = jnp.zeros_like(acc_ref)

        for p in range(products):
            @pl.when(jnp.logical_and(k >= p * nk, k < (p + 1) * nk))
            def _(p=p):
                acc_ref[...] += lax.dot_general(refs[2 * p][...], refs[2 * p + 1][...], dims, preferred_element_type=F32)

        @pl.when(k == products * nk - 1)
        def _():
            o_ref[...] = acc_ref[...].astype(o_ref.dtype)

    def a_spec(p):
        kk = lambda k: jnp.clip(k - p * nk, 0, nk - 1)
        if mode == "tn":
            return pl.BlockSpec((tk, tm), lambda i, j, k: (kk(k), i))
        return pl.BlockSpec((tm, tk), lambda i, j, k: (i, kk(k)))

    def b_spec(p):
        kk = lambda k: jnp.clip(k - p * nk, 0, nk - 1)
        if mode == "nt":
            return pl.BlockSpec((tn, tk), lambda i, j, k: (j, kk(k)))
        return pl.BlockSpec((tk, tn), lambda i, j, k: (kk(k), j))

    o_spec = pl.BlockSpec((tm, tn), lambda i, j, k: (i, j))
    operands = (a, b) + (tuple(second) if second is not None else ())
    outs, carried = _call(
        body, name=name, grid=(M // tm, N // tn, products * nk),
        in_specs=[spec(p) for p in range(products) for spec in (a_spec, b_spec)], out_specs=[o_spec],
        out_shape=[jax.ShapeDtypeStruct((M, N), out_dtype)], scratch_shapes=[pltpu.VMEM((tm, tn), F32)],
        semantics=("parallel", "parallel", "arbitrary"), args=operands, side=side)
    return outs[0] if side is None else (outs[0], carried)


def _sigmoid(v):
    return 0.5 * jnp.tanh(0.5 * v) + 0.5


_GELU_C = math.sqrt(2.0 / math.pi)


def _gelu(v):
    return 0.5 * v * (1.0 + jnp.tanh(_GELU_C * (v + 0.044715 * v * v * v)))


def _gelu_grad(v):
    t = jnp.tanh(_GELU_C * (v + 0.044715 * v * v * v))
    return 0.5 * (1.0 + t) + 0.5 * v * (1.0 - t * t) * _GELU_C * (1.0 + 3.0 * 0.044715 * v * v)


def _rms(v, gain):
    r = lax.rsqrt(jnp.mean(v * v, axis=-1, keepdims=True) + EPS)
    return v * r * gain


def _rms_bwd(v, gain, dy):
    r = lax.rsqrt(jnp.mean(v * v, axis=-1, keepdims=True) + EPS)
    a = dy * gain
    dv = r * a - v * (r * r * r) * jnp.mean(a * v, axis=-1, keepdims=True)
    return dv, dy * v * r


def _row_tile(s):
    return _pick(s, (256, 128, 64, 8))


def _norm_in(x, gain):
    s, d = x.shape
    tr = _row_tile(s)

    def body(x_ref, g_ref, h_ref):
        h_ref[...] = _rms(x_ref[...], g_ref[...]).astype(BF16)

    row = pl.BlockSpec((tr, d), lambda i: (i, 0))
    vec = pl.BlockSpec((1, d), lambda i: (0, 0))
    return pl.pallas_call(body, name="norm_in", grid=(s // tr,), in_specs=[row, vec], out_specs=row,
                          out_shape=jax.ShapeDtypeStruct((s, d), BF16), compiler_params=_params(("parallel",)))(x, gain)


def _prologue(x, gain, weights, side=None):
    s, d = x.shape
    tr = _row_tile(s)
    steps = s // tr
    names = list(weights)
    tiles = []
    for k in names:
        r, _ = weights[k].shape
        tiles.append(next(t for t in range(16, r + 1, 16) if r % t == 0 and r // t <= steps))

    def body(*refs):
        x_ref, g_ref = refs[:2]
        w_refs, h_ref, o_refs = refs[2:2 + len(names)], refs[2 + len(names)], refs[3 + len(names):]
        h_ref[...] = _rms(x_ref[...], g_ref[...]).astype(BF16)
        for w_ref, o_ref in zip(w_refs, o_refs):
            o_ref[...] = w_ref[...].astype(BF16)

    row = pl.BlockSpec((tr, d), lambda i: (i, 0))
    w_specs = [pl.BlockSpec((t, weights[k].shape[1]), lambda i, last=weights[k].shape[0] // t - 1: (jnp.minimum(i, last), 0))
               for k, t in zip(names, tiles)]
    outs, carried = _call(
        body, name="prologue", grid=(steps,), in_specs=[row, pl.BlockSpec((1, d), lambda i: (0, 0))] + w_specs,
        out_specs=[row] + w_specs,
        out_shape=[jax.ShapeDtypeStruct((s, d), BF16)] + [jax.ShapeDtypeStruct(weights[k].shape, BF16) for k in names],
        semantics=("arbitrary",), args=[x, gain] + [weights[k] for k in names], side=side)
    return outs[0], dict(zip(names, outs[1:])), carried


def _norm_mid(x, mo, g_post, g_pre):
    s, d = x.shape
    tr = _row_tile(s)

    def body(x_ref, mo_ref, g2_ref, g3_ref, x2_ref, h2_ref):
        x2 = x_ref[...] + _rms(mo_ref[...], g2_ref[...])
        x2_ref[...] = x2
        h2_ref[...] = _rms(x2, g3_ref[...]).astype(BF16)

    row = pl.BlockSpec((tr, d), lambda i: (i, 0))
    vec = pl.BlockSpec((1, d), lambda i: (0, 0))
    return pl.pallas_call(
        body, name="norm_mid", grid=(s // tr,), in_specs=[row, row, vec, vec], out_specs=[row, row],
        out_shape=[jax.ShapeDtypeStruct((s, d), F32), jax.ShapeDtypeStruct((s, d), BF16)],
        compiler_params=_params(("parallel",)))(x, mo, g_post, g_pre)


def _loss_head(x2, f, g_post, target):
    s, d = x2.shape
    tr = _row_tile(s)

    def body(x2_ref, f_ref, g_ref, t_ref, loss_ref, dout_ref, df_ref, dg_ref):
        @pl.when(pl.program_id(0) == 0)
        def _():
            loss_ref[...] = jnp.zeros_like(loss_ref)
            dg_ref[...] = jnp.zeros_like(dg_ref)

        fv = f_ref[...]
        g = g_ref[...]
        err = x2_ref[...] + _rms(fv, g) - t_ref[...]
        loss_ref[...] += 0.5 * jnp.sum(jnp.mean(err * err, axis=-1, keepdims=True), axis=0, keepdims=True)
        dout = err * (1.0 / d)
        dout_ref[...] = dout
        df, dg = _rms_bwd(fv, g, dout)
        df_ref[...] = df.astype(BF16)
        dg_ref[...] += jnp.sum(dg, axis=0, keepdims=True)

    row = pl.BlockSpec((tr, d), lambda i: (i, 0))
    vec = pl.BlockSpec((1, d), lambda i: (0, 0))
    one = pl.BlockSpec((1, 1), lambda i: (0, 0))
    return pl.pallas_call(
        body, name="loss_head", grid=(s // tr,), in_specs=[row, row, vec, row], out_specs=[one, row, row, vec],
        out_shape=[jax.ShapeDtypeStruct((1, 1), F32), jax.ShapeDtypeStruct((s, d), F32),
                   jax.ShapeDtypeStruct((s, d), BF16), jax.ShapeDtypeStruct((1, d), F32)],
        compiler_params=_params(("arbitrary",)))(x2, f, g_post, target)


def _norm_mid_bwd(x2, mo, g_post, g_pre, dout, dh2):
    s, d = x2.shape
    tr = _row_tile(s)

    def body(x2_ref, mo_ref, g2_ref, g3_ref, dout_ref, dh2_ref, dx2_ref, dmo_ref, dg2_ref, dg3_ref):
        @pl.when(pl.program_id(0) == 0)
        def _():
            dg2_ref[...] = jnp.zeros_like(dg2_ref)
            dg3_ref[...] = jnp.zeros_like(dg3_ref)

        dv, dg3 = _rms_bwd(x2_ref[...], g3_ref[...], dh2_ref[...])
        dx2 = dout_ref[...] + dv
        dx2_ref[...] = dx2
        dmo, dg2 = _rms_bwd(mo_ref[...], g2_ref[...], dx2)
        dmo_ref[...] = dmo.astype(BF16)
        dg2_ref[...] += jnp.sum(dg2, axis=0, keepdims=True)
        dg3_ref[...] += jnp.sum(dg3, axis=0, keepdims=True)

    row = pl.BlockSpec((tr, d), lambda i: (i, 0))
    vec = pl.BlockSpec((1, d), lambda i: (0, 0))
    return pl.pallas_call(
        body, name="norm_mid_bwd", grid=(s // tr,), in_specs=[row, row, vec, vec, row, row],
        out_specs=[row, row, vec, vec],
        out_shape=[jax.ShapeDtypeStruct((s, d), F32), jax.ShapeDtypeStruct((s, d), BF16),
                   jax.ShapeDtypeStruct((1, d), F32), jax.ShapeDtypeStruct((1, d), F32)],
        compiler_params=_params(("arbitrary",)))(x2, mo, g_post, g_pre, dout, dh2)


def _norm_in_bwd(x, gain, dh, dx2):
    s, d = x.shape
    tr = _row_tile(s)

    def body(x_ref, g_ref, dh_ref, dx2_ref, dx_ref, dg_ref):
        @pl.when(pl.program_id(0) == 0)
        def _():
            dg_ref[...] = jnp.zeros_like(dg_ref)

        dv, dg = _rms_bwd(x_ref[...], g_ref[...], dh_ref[...])
        dx_ref[...] = dx2_ref[...] + dv
        dg_ref[...] += jnp.sum(dg, axis=0, keepdims=True)

    row = pl.BlockSpec((tr, d), lambda i: (i, 0))
    vec = pl.BlockSpec((1, d), lambda i: (0, 0))
    return pl.pallas_call(
        body, name="norm_in_bwd", grid=(s // tr,), in_specs=[row, vec, row, row], out_specs=[row, vec],
        out_shape=[jax.ShapeDtypeStruct((s, d), F32), jax.ShapeDtypeStruct((1, d), F32)],
        compiler_params=_params(("arbitrary",)))(x, gain, dh, dx2)


def _swiglu_epilogue(rs, es):
    g, u = rs
    return [g * _sigmoid(g) * u, g, u]


def _swiglu_bwd_epilogue(rs, es):
    d = rs[0]
    g, u = es[0].astype(F32), es[1].astype(F32)
    sg = _sigmoid(g)
    return [d * u * sg * (1.0 + g * (1.0 - sg)), d * g * sg]


def _sum_epilogue(rs, es):
    return [rs[0] + rs[1]]


def _gates_epilogue(rs, es):
    ab, gv, gg = rs
    ga, gs = es
    return [_sigmoid(ga) * ab + _sigmoid(gs) * gv * _sigmoid(gg), ab, gv, gg]


def _gates_bwd_epilogue(rs, es):
    dm = rs[0]
    ga, gs, ab, gv, gg = (e.astype(F32) for e in es)
    sa, ss, sg = _sigmoid(ga), _sigmoid(gs), _sigmoid(gg)
    dsb = dm * ss
    return [dm * ab * sa * (1.0 - sa), dm * gv * sg * ss * (1.0 - ss), dm * sa, dsb * sg, dsb * gv * sg * (1.0 - sg)]


ATTN_ROWS = 2048


def _dilate_qkv(z, g, d):
    s = z.shape[0]
    tm = ATTN_ROWS
    per = tm // d
    nh = HEADS_PER_GROUP

    def body(z_ref, o_ref):
        for r in range(d):
            rows = z_ref[...] if d == 1 else z_ref[pl.ds(r, per, stride=d), :]
            o_ref[0, r] = rows.astype(BF16)

    return pl.pallas_call(
        body, name=f"dilate_qkv_{g}", grid=(s // tm, 3, nh),
        in_specs=[pl.BlockSpec((tm, HEAD_DIM), lambda i, w, h: (i, (3 * w + g) * nh + h))],
        out_specs=pl.BlockSpec((1, d, per, HEAD_DIM), lambda i, w, h: (w, 0, i, h)),
        out_shape=jax.ShapeDtypeStruct((3, d, s // d, GROUP_W), BF16),
        compiler_params=_params(("parallel", "parallel", "parallel")))(z)


def _undilate_dqkv(dqkv, dz, g, d):
    s = dz.shape[0]
    tm = ATTN_ROWS
    per = tm // d
    nh = HEADS_PER_GROUP

    def body(i_ref, dz_ref, o_ref, nat_ref):
        del dz_ref
        if d == 1:
            o_ref[...] = i_ref[0, 0]
        else:
            for r in range(d):
                nat_ref[pl.ds(r, per, stride=d), :] = i_ref[0, r].astype(F32)
            o_ref[...] = nat_ref[...].astype(BF16)

    return pl.pallas_call(
        body, name=f"undilate_dqkv_{g}", grid=(s // tm, 3, nh),
        in_specs=[pl.BlockSpec((1, d, per, HEAD_DIM), lambda i, w, h: (w, 0, i, h)),
                  pl.BlockSpec(memory_space=pl.ANY)],
        out_specs=pl.BlockSpec((tm, HEAD_DIM), lambda i, w, h: (i, (3 * w + g) * nh + h)),
        out_shape=jax.ShapeDtypeStruct(dz.shape, dz.dtype), input_output_aliases={1: 0},
        scratch_shapes=[pltpu.VMEM((tm, HEAD_DIM), F32)],
        compiler_params=_params(("parallel", "parallel", "parallel")))(dqkv, dz)


def _alibi_slope(head):
    return 2.0 ** (-8.0 * (head + 1) / N_ATTN_HEADS)


def _dot_nt(a, b):
    return lax.dot_general(a, b, _DOT_DIMS["nt"], preferred_element_type=F32)


def _dot_tn(a, b):
    return lax.dot_general(a, b, _DOT_DIMS["tn"], preferred_element_type=F32)


def _dot(a, b):
    return jnp.dot(a, b, preferred_element_type=F32)


GROUP_ROWS = HEADS_PER_GROUP * ATTN_BLK


def _band_bias(g, d, pairs):
    qi = jnp.arange(ATTN_BLK)[:, None]
    ki = jnp.arange(ATTN_BLK)[None, :]
    rows = []
    for hh in range(HEADS_PER_GROUP):
        slope_d = _alibi_slope(g * HEADS_PER_GROUP + hh) * d
        tiles = []
        for kind in pairs:
            dist = qi - ki if kind == "cur" else ATTN_BLK + qi - ki
            ok = dist >= 0 if kind == "cur" else dist <= ATTN_BLK
            tiles.append(jnp.where(ok, -slope_d * dist.astype(F32), NEG_BIG))
        rows.append(jnp.concatenate(tiles, axis=1))
    return jnp.concatenate(rows, axis=0).astype(F32)


def _tile_cols(t):
    return slice(t * ATTN_BLK, (t + 1) * ATTN_BLK)


def _attn_fwd(qkv, g, d):
    _, _, L, _ = qkv.shape
    nb = L // ATTN_BLK
    scale = HEAD_DIM ** -0.5

    def body(q_ref, kc_ref, kp_ref, vc_ref, vp_ref, bias_ref, o_ref, lse_ref, s_ref, p_ref):
        n = pl.program_id(1)
        for hh in range(HEADS_PER_GROUP):
            cols, rows = _tile_cols(hh), _tile_cols(hh)
            q = q_ref[0, 0, :, cols]
            s_ref[rows, _tile_cols(0)] = _dot_nt(q, kp_ref[0, 0, :, cols])
            s_ref[rows, _tile_cols(1)] = _dot_nt(q, kc_ref[0, 0, :, cols])
        col = lax.broadcasted_iota(jnp.int32, (GROUP_ROWS, 2 * ATTN_BLK), 1)
        s = s_ref[...] * scale + bias_ref[...]
        s = jnp.where(jnp.logical_and(col < ATTN_BLK, n == 0), NEG_BIG, s)
        m = jnp.max(s, axis=-1, keepdims=True)
        e = jnp.exp(s - m)
        l = jnp.sum(e, axis=-1, keepdims=True)
        p_ref[...] = (e * (1.0 / l)).astype(BF16)
        lse = m + jnp.log(l)
        for hh in range(HEADS_PER_GROUP):
            cols, rows = _tile_cols(hh), _tile_cols(hh)
            o_ref[0, :, cols] = (_dot(p_ref[rows, _tile_cols(0)], vp_ref[0, 0, :, cols])
                                 + _dot(p_ref[rows, _tile_cols(1)], vc_ref[0, 0, :, cols]))
            lse_ref[0, :, cols] = jnp.broadcast_to(lse[rows], (ATTN_BLK, HEAD_DIM))

    def spec(w, shift):
        return pl.BlockSpec((1, 1, ATTN_BLK, GROUP_W), lambda r, n: (w, r, jnp.maximum(n + shift, 0), 0))

    out = pl.BlockSpec((1, ATTN_BLK, GROUP_W), lambda r, n: (r, n, 0))
    bias = _band_bias(g, d, ("prev", "cur"))
    return pl.pallas_call(
        body, name=f"attn_fwd_{g}", grid=(d, nb),
        in_specs=[spec(0, 0), spec(1, 0), spec(1, -1), spec(2, 0), spec(2, -1),
                  pl.BlockSpec(bias.shape, lambda r, n: (0, 0))],
        out_specs=[out, out], out_shape=[jax.ShapeDtypeStruct((d, L, GROUP_W), F32)] * 2,
        scratch_shapes=[pltpu.VMEM((GROUP_ROWS, 2 * ATTN_BLK), F32), pltpu.VMEM((GROUP_ROWS, 2 * ATTN_BLK), BF16)],
        compiler_params=_params(("parallel", "parallel")))(qkv, qkv, qkv, qkv, qkv, bias)


def _attn_bwd(qkv, do, lse, cc, g, d, side=None):
    _, _, L, _ = qkv.shape
    nb = L // ATTN_BLK
    scale = HEAD_DIM ** -0.5
    a_, b_, c_ = _tile_cols(0), _tile_cols(1), _tile_cols(2)

    def body(q0_ref, q1_ref, k0_ref, kp_ref, v0_ref, vp_ref, do0_ref, do1_ref, l0_ref, l1_ref, c0_ref, c1_ref,
             bias_ref, o_ref, s_ref, dp_ref, l_ref, c_ref, p_ref, ds_ref):
        n = pl.program_id(1)
        for hh in range(HEADS_PER_GROUP):
            cols, rows = _tile_cols(hh), _tile_cols(hh)
            q0, q1 = q0_ref[0, 0, :, cols], q1_ref[0, 0, :, cols]
            k0, kp = k0_ref[0, 0, :, cols], kp_ref[0, 0, :, cols]
            v0, vp = v0_ref[0, 0, :, cols], vp_ref[0, 0, :, cols]
            do0, do1 = do0_ref[0, :, cols], do1_ref[0, :, cols]
            s_ref[rows, a_], s_ref[rows, b_], s_ref[rows, c_] = _dot_nt(q0, k0), _dot_nt(q0, kp), _dot_nt(q1, k0)
            dp_ref[rows, a_], dp_ref[rows, b_], dp_ref[rows, c_] = _dot_nt(do0, v0), _dot_nt(do0, vp), _dot_nt(do1, v0)
            l_ref[rows, a_], l_ref[rows, b_], l_ref[rows, c_] = l0_ref[0, :, cols], l0_ref[0, :, cols], l1_ref[0, :, cols]
            c_ref[rows, a_], c_ref[rows, b_], c_ref[rows, c_] = c0_ref[0, :, cols], c0_ref[0, :, cols], c1_ref[0, :, cols]
        col = lax.broadcasted_iota(jnp.int32, (GROUP_ROWS, 3 * ATTN_BLK), 1)
        tile = col // ATTN_BLK
        gone = jnp.logical_or(jnp.logical_and(tile == 1, n == 0), jnp.logical_and(tile == 2, n == nb - 1))
        s = jnp.where(gone, NEG_BIG, s_ref[...] * scale + bias_ref[...])
        p = jnp.exp(s - l_ref[...])
        p_ref[...] = p.astype(BF16)
        ds_ref[...] = (p * (dp_ref[...] + c_ref[...])).astype(BF16)
        for hh in range(HEADS_PER_GROUP):
            cols, rows = _tile_cols(hh), _tile_cols(hh)
            q0, q1 = q0_ref[0, 0, :, cols], q1_ref[0, 0, :, cols]
            k0, kp = k0_ref[0, 0, :, cols], kp_ref[0, 0, :, cols]
            do0, do1 = do0_ref[0, :, cols], do1_ref[0, :, cols]
            o_ref[0, 0, :, cols] = ((_dot(ds_ref[rows, a_], k0) + _dot(ds_ref[rows, b_], kp)) * scale).astype(BF16)
            o_ref[1, 0, :, cols] = ((_dot_tn(ds_ref[rows, a_], q0) + _dot_tn(ds_ref[rows, c_], q1)) * scale).astype(BF16)
            o_ref[2, 0, :, cols] = (_dot_tn(p_ref[rows, a_], do0) + _dot_tn(p_ref[rows, c_], do1)).astype(BF16)

    def spec(w, shift):
        return pl.BlockSpec((1, 1, ATTN_BLK, GROUP_W), lambda r, n: (w, r, jnp.clip(n + shift, 0, nb - 1), 0))

    def spec3(shift):
        return pl.BlockSpec((1, ATTN_BLK, GROUP_W), lambda r, n: (r, jnp.clip(n + shift, 0, nb - 1), 0))

    bias = _band_bias(g, d, ("cur", "prev", "prev"))
    wide = (GROUP_ROWS, 3 * ATTN_BLK)
    outs, carried = _call(
        body, name=f"attn_bwd_{g}", grid=(d, nb),
        in_specs=[spec(0, 0), spec(0, 1), spec(1, 0), spec(1, -1), spec(2, 0), spec(2, -1),
                  spec3(0), spec3(1), spec3(0), spec3(1), spec3(0), spec3(1), pl.BlockSpec(wide, lambda r, n: (0, 0))],
        out_specs=[pl.BlockSpec((3, 1, ATTN_BLK, GROUP_W), lambda r, n: (0, r, n, 0))],
        out_shape=[jax.ShapeDtypeStruct((3, d, L, GROUP_W), BF16)],
        scratch_shapes=[pltpu.VMEM(wide, F32)] * 4 + [pltpu.VMEM(wide, BF16)] * 2, semantics=("parallel", "parallel"),
        args=[qkv, qkv, qkv, qkv, qkv, qkv, do, do, lse, lse, cc, cc, bias], side=side)
    return outs[0] if side is None else (outs[0], carried)


def _load_natural(refs, nat_refs):
    for g, d in enumerate(ATTN_DILATIONS):
        if d == 1:
            nat_refs[g][...] = refs[g][0]
        else:
            per = ATTN_ROWS // d
            for r in range(d):
                nat_refs[g][pl.ds(r, per, stride=d), :] = refs[g][r]


def _mix_weights(lse_nat):
    l0, l1, l2 = lse_nat[0][...], lse_nat[1][...], lse_nat[2][...]
    m = jnp.maximum(jnp.maximum(l0, l1), l2)
    e0, e1, e2 = jnp.exp(l0 - m), jnp.exp(l1 - m), jnp.exp(l2 - m)
    inv = 1.0 / (e0 + e1 + e2)
    return e0 * inv, e1 * inv, e2 * inv


def _dilated_specs(s):
    return [pl.BlockSpec((d, ATTN_ROWS // d, HEAD_DIM), lambda i, h: (0, i, h)) for d in ATTN_DILATIONS]


NATURAL_SCRATCH = [pltpu.VMEM((ATTN_ROWS, HEAD_DIM), F32)] * (2 * len(ATTN_DILATIONS))


def _attn_merge(outs, lses):
    s = outs[0].shape[0] * outs[0].shape[1]

    def body(o0, o1, o2, l0, l1, l2, a_ref, *nat):
        onat, lnat = nat[:3], nat[3:]
        _load_natural((o0, o1, o2), onat)
        _load_natural((l0, l1, l2), lnat)
        w0, w1, w2 = _mix_weights(lnat)
        a_ref[...] = (w0 * onat[0][...] + w1 * onat[1][...] + w2 * onat[2][...]).astype(BF16)

    return pl.pallas_call(
        body, name="attn_merge", grid=(s // ATTN_ROWS, HEADS_PER_GROUP), in_specs=_dilated_specs(s) * 2,
        out_specs=pl.BlockSpec((ATTN_ROWS, HEAD_DIM), lambda i, h: (i, h)),
        out_shape=jax.ShapeDtypeStruct((s, GROUP_W), BF16), scratch_shapes=NATURAL_SCRATCH,
        compiler_params=_params(("parallel", "parallel")))(*outs, *lses)


def _attn_merge_bwd(outs, lses, dattn):
    s = dattn.shape[0]

    def body(o0, o1, o2, l0, l1, l2, da_ref, do0, do1, do2, c0, c1, c2, *nat):
        onat, lnat = nat[:3], nat[3:]
        _load_natural((o0, o1, o2), onat)
        _load_natural((l0, l1, l2), lnat)
        ws = _mix_weights(lnat)
        da = da_ref[...]
        attn = ws[0] * onat[0][...] + ws[1] * onat[1][...] + ws[2] * onat[2][...]
        tot = jnp.broadcast_to(jnp.sum(da * attn, axis=-1, keepdims=True), (ATTN_ROWS, HEAD_DIM))
        for g, (d, do_ref, c_ref) in enumerate(zip(ATTN_DILATIONS, (do0, do1, do2), (c0, c1, c2))):
            if d == 1:
                do_ref[0] = (ws[g] * da).astype(BF16)
                c_ref[0] = -ws[g] * tot
            else:
                onat[g][...] = ws[g] * da
                lnat[g][...] = -ws[g] * tot
                per = ATTN_ROWS // d
                for r in range(d):
                    do_ref[r] = onat[g][pl.ds(r, per, stride=d), :].astype(BF16)
                    c_ref[r] = lnat[g][pl.ds(r, per, stride=d), :]

    dil = _dilated_specs(s)
    shapes = [jax.ShapeDtypeStruct(o.shape, BF16) for o in outs] + [jax.ShapeDtypeStruct(o.shape, F32) for o in outs]
    return pl.pallas_call(
        body, name="attn_merge_bwd", grid=(s // ATTN_ROWS, HEADS_PER_GROUP),
        in_specs=dil * 2 + [pl.BlockSpec((ATTN_ROWS, HEAD_DIM), lambda i, h: (i, h))], out_specs=dil * 2,
        out_shape=shapes, scratch_shapes=NATURAL_SCRATCH,
        compiler_params=_params(("parallel", "parallel")))(*outs, *lses, dattn)


def _ssm_prepare(a_re, a_im, log_dt, b_re, b_im, c_re, c_im):
    n_g = a_re.shape[0]
    nj = n_g * SSM_GROUP // SSM_TILE_CH
    gpt = SSM_TILE_CH // SSM_GROUP
    dt = jnp.exp(log_dt)[:, None]
    mag = jnp.exp(a_re * dt)
    lr, li = mag * jnp.cos(a_im * dt), mag * jnp.sin(a_im * dt)
    den = a_re * a_re + a_im * a_im
    cr = ((lr - 1.0) * a_re + li * a_im) / den
    ci = (li * a_re - (lr - 1.0) * a_im) / den
    bb_re = cr[..., None] * b_re - ci[..., None] * b_im
    bb_im = cr[..., None] * b_im + ci[..., None] * b_re
    eye = jnp.eye(gpt, dtype=F32)

    def b_tiles(t):
        t = t.transpose(0, 2, 1).reshape(nj, gpt, SSM_GROUP, SSM_STATE)
        return jnp.einsum("jgcp,gh->jgchp", t, eye).reshape(nj, SSM_TILE_CH, SSM_TILE_ST)

    def c_tiles(t):
        t = t.reshape(nj, gpt, SSM_GROUP, SSM_STATE)
        return jnp.einsum("jgcp,gh->jhpgc", t, eye).reshape(nj, SSM_TILE_ST, SSM_TILE_CH)

    lam = jnp.stack([lr.reshape(-1), li.reshape(-1)])
    bmat = jnp.concatenate([b_tiles(bb_re), b_tiles(bb_im)], axis=2)
    cmat = jnp.concatenate([c_tiles(c_re), -c_tiles(c_im)], axis=1)
    return lam, bmat, cmat


SSM_SEGMENTS = 8


def _to_segment_order(nat, perm_ref):
    per = nat.shape[0] // SSM_SEGMENTS
    for i in range(SSM_SEGMENTS):
        perm_ref[pl.ds(i, per, stride=SSM_SEGMENTS), :] = nat[i * per:(i + 1) * per, :]
    return perm_ref[...]


def _to_time_order(val, perm_ref, store):
    per = val.shape[0] // SSM_SEGMENTS
    perm_ref[...] = val
    for i in range(SSM_SEGMENTS):
        store(i, perm_ref[pl.ds(i, per, stride=SSM_SEGMENTS), :])


def _fill_powers(lam_ref, w_ref, nj, tau_n):
    for j in range(nj):
        st = slice(j * SSM_TILE_ST, (j + 1) * SSM_TILE_ST)
        lr = jnp.broadcast_to(lam_ref[0:1, st], (SSM_SEGMENTS, SSM_TILE_ST))
        li = jnp.broadcast_to(lam_ref[1:2, st], (SSM_SEGMENTS, SSM_TILE_ST))
        wr, wi = lr, li
        for tau in range(tau_n):
            rows = slice(tau * SSM_SEGMENTS, (tau + 1) * SSM_SEGMENTS)
            w_ref[j, rows, :SSM_TILE_ST] = wr
            w_ref[j, rows, SSM_TILE_ST:] = wi
            wr, wi = wr * lr - wi * li, wr * li + wi * lr


def _segment_scan(src, xs_ref, w_tile, lr, li, cr, ci, conj, reverse):
    seg, half = SSM_SEGMENTS, SSM_TILE_ST
    tau_n = src.shape[0] // seg
    sgn = -1.0 if conj else 1.0
    lr8 = jnp.broadcast_to(lr, (seg, half))
    li8 = jnp.broadcast_to(li, (seg, half)) * sgn
    xr = jnp.zeros((seg, half), F32)
    xi = jnp.zeros((seg, half), F32)
    order = range(tau_n - 1, -1, -1) if reverse else range(tau_n)
    for tau in order:
        rows = slice(tau * seg, (tau + 1) * seg)
        xr, xi = lr8 * xr - li8 * xi + src[rows, :half], lr8 * xi + li8 * xr + src[rows, half:]
        xs_ref[rows, :half] = xr
        xs_ref[rows, half:] = xi
    pr = w_tile[(tau_n - 1) * seg:(tau_n - 1) * seg + 1, :half]
    pi = w_tile[(tau_n - 1) * seg:(tau_n - 1) * seg + 1, half:] * sgn
    fr, fi = cr, ci
    ins_r, ins_i = [None] * seg, [None] * seg
    runs = range(seg - 1, -1, -1) if reverse else range(seg)
    for i in runs:
        ins_r[i], ins_i[i] = fr, fi
        fr, fi = xr[i:i + 1, :] + pr * fr - pi * fi, xi[i:i + 1, :] + pr * fi + pi * fr
    in_r = jnp.concatenate(ins_r, axis=0)
    in_i = jnp.concatenate(ins_i, axis=0)
    for tau in range(tau_n):
        rows = slice(tau * seg, (tau + 1) * seg)
        wrow = (tau_n - 1 - tau) if reverse else tau
        wr = w_tile[wrow * seg:(wrow + 1) * seg, :half]
        wi = w_tile[wrow * seg:(wrow + 1) * seg, half:] * sgn
        xs_ref[rows, :half] += wr * in_r - wi * in_i
        xs_ref[rows, half:] += wr * in_i + wi * in_r
    return (fr, fi), (in_r, in_i)


def _ssm_dims(z, bmat, u_off):
    s = z.shape[0]
    nj = bmat.shape[0]
    t_rows = _pick(s, (256, 128))
    return s, nj, nj * SSM_TILE_CH, nj * SSM_TILE_ST, t_rows


def _ssm_fwd(z, bmat, cmat, lam, dskip, u_off, side=None):
    s, nj, w, ns, t_rows = _ssm_dims(z, bmat, u_off)
    per = t_rows // SSM_SEGMENTS

    def body(*refs):
        u_refs = refs[:nj]
        b_ref, c_ref, lam_ref, d_ref, y_ref, yg_ref, xin_ref, carry_ref, w_ref, xs_ref, perm_ref = refs[nj:]

        @pl.when(pl.program_id(0) == 0)
        def _():
            carry_ref[...] = jnp.zeros_like(carry_ref)
            _fill_powers(lam_ref, w_ref, nj, per)

        xin_ref[0] = carry_ref[...]
        for j in range(nj):
            st = slice(j * SSM_TILE_ST, (j + 1) * SSM_TILE_ST)
            ch = slice(j * SSM_TILE_CH, (j + 1) * SSM_TILE_CH)
            up = _to_segment_order(u_refs[j], perm_ref)
            bu = _dot(up.astype(BF16), b_ref[j])
            (fr, fi), _ = _segment_scan(bu, xs_ref, w_ref.at[j], lam_ref[0:1, st], lam_ref[1:2, st],
                                        carry_ref[0:1, st], carry_ref[1:2, st], conj=False, reverse=False)
            carry_ref[0:1, st] = fr
            carry_ref[1:2, st] = fi
            yp = _dot(xs_ref[...].astype(BF16), c_ref[j]) + d_ref[:, ch] * up

            def store(i, rows, ch=ch):
                y_ref[i * per:(i + 1) * per, ch] = rows
                yg_ref[i * per:(i + 1) * per, ch] = _gelu(rows).astype(BF16)

            _to_time_order(yp, perm_ref, store)

    u_specs = [pl.BlockSpec((t_rows, SSM_TILE_CH), lambda c, k=k: (c, u_off // SSM_TILE_CH + k)) for k in range(nj)]
    full3 = lambda shape: pl.BlockSpec(shape, lambda c: (0, 0, 0))
    full2 = lambda shape: pl.BlockSpec(shape, lambda c: (0, 0))
    rows = pl.BlockSpec((t_rows, w), lambda c: (c, 0))
    outs, carried = _call(
        body, name="ssm_fwd", grid=(s // t_rows,),
        in_specs=u_specs + [full3(bmat.shape), full3(cmat.shape), full2(lam.shape), full2(dskip.shape)],
        out_specs=[rows, rows, pl.BlockSpec((1, 2, ns), lambda c: (c, 0, 0))],
        out_shape=[jax.ShapeDtypeStruct((s, w), F32), jax.ShapeDtypeStruct((s, w), BF16),
                   jax.ShapeDtypeStruct((s // t_rows, 2, ns), F32)],
        scratch_shapes=[pltpu.VMEM((2, ns), F32), pltpu.VMEM((nj, t_rows, 2 * SSM_TILE_ST), F32),
                        pltpu.VMEM((t_rows, 2 * SSM_TILE_ST), F32), pltpu.VMEM((t_rows, SSM_TILE_CH), F32)],
        semantics=("arbitrary",), args=[*([z] * nj), bmat, cmat, lam, dskip], side=side)
    return outs if side is None else (outs, carried)


def _ssm_bwd(z, y, dyg, xin, bmat, cmat, lam, dskip, u_off, side=None):
    s, nj, w, ns, t_rows = _ssm_dims(z, bmat, u_off)
    nc = s // t_rows
    per = t_rows // SSM_SEGMENTS
    seg, half = SSM_SEGMENTS, SSM_TILE_ST

    def body(*refs):
        u_refs = refs[:nj]
        (y_ref, dyg_ref, xin_ref, b_ref, c_ref, lam_ref, d_ref, du_ref, db_ref, dc_ref, dlam_ref, dd_ref,
         carry_ref, w_ref, xs_ref, gs_ref, perm_ref, acc_ref) = refs[nj:]

        @pl.when(pl.program_id(0) == 0)
        def _():
            carry_ref[...] = jnp.zeros_like(carry_ref)
            db_ref[...] = jnp.zeros_like(db_ref)
            dc_ref[...] = jnp.zeros_like(dc_ref)
            dd_ref[...] = jnp.zeros_like(dd_ref)
            acc_ref[...] = jnp.zeros_like(acc_ref)
            _fill_powers(lam_ref, w_ref, nj, per)

        for j in range(nj):
            st = slice(j * SSM_TILE_ST, (j + 1) * SSM_TILE_ST)
            ch = slice(j * SSM_TILE_CH, (j + 1) * SSM_TILE_CH)
            lr, li = lam_ref[0:1, st], lam_ref[1:2, st]
            up = _to_segment_order(u_refs[j], perm_ref)
            upb = up.astype(BF16)
            dyp = _to_segment_order(dyg_ref[:, ch] * _gelu_grad(y_ref[:, ch]), perm_ref)
            dyb = dyp.astype(BF16)
            _, (in_r, in_i) = _segment_scan(_dot(upb, b_ref[j]), xs_ref, w_ref.at[j], lr, li,
                                            xin_ref[0, 0:1, st], xin_ref[0, 1:2, st], conj=False, reverse=False)
            (gr, gi), _ = _segment_scan(_dot_nt(dyb, c_ref[j]), gs_ref, w_ref.at[j], lr, li,
                                        carry_ref[0:1, st], carry_ref[1:2, st], conj=True, reverse=True)
            carry_ref[0:1, st] = gr
            carry_ref[1:2, st] = gi
            xs, gs = xs_ref[...], gs_ref[...]
            xsr, xsi, gsr, gsi = xs[:, :half], xs[:, half:], gs[:, :half], gs[:, half:]
            pxr = jnp.concatenate([in_r, xsr[:t_rows - seg]], axis=0)
            pxi = jnp.concatenate([in_i, xsi[:t_rows - seg]], axis=0)
            dl_r = gsr * pxr + gsi * pxi
            dl_i = gsi * pxr - gsr * pxi
            acc_ref[0, :, st] += jnp.sum(dl_r.reshape(per, seg, half), axis=0)
            acc_ref[1, :, st] += jnp.sum(dl_i.reshape(per, seg, half), axis=0)
            gx = gs.astype(BF16)
            dup = _dot_nt(gx, b_ref[j]) + d_ref[:, ch] * dyp

            def store(i, rows, ch=ch):
                du_ref[i * per:(i + 1) * per, ch] = rows.astype(BF16)

            _to_time_order(dup, perm_ref, store)
            db_ref[j] += _dot_tn(upb, gx)
            dc_ref[j] += _dot_tn(xs.astype(BF16), dyb)
            dd_ref[:, ch] += jnp.sum(dyp * up, axis=0, keepdims=True)

        @pl.when(pl.program_id(0) == nc - 1)
        def _():
            dlam_ref[...] = jnp.sum(acc_ref[...], axis=1)

    rev = lambda c: nc - 1 - c
    u_specs = [pl.BlockSpec((t_rows, SSM_TILE_CH), lambda c, k=k: (rev(c), u_off // SSM_TILE_CH + k))
               for k in range(nj)]
    full3 = lambda shape: pl.BlockSpec(shape, lambda c: (0, 0, 0))
    full2 = lambda shape: pl.BlockSpec(shape, lambda c: (0, 0))
    rows = pl.BlockSpec((t_rows, w), lambda c: (rev(c), 0))
    outs, carried = _call(
        body, name="ssm_bwd", grid=(nc,),
        in_specs=u_specs + [rows, rows, pl.BlockSpec((1, 2, ns), lambda c: (rev(c), 0, 0)),
                            full3(bmat.shape), full3(cmat.shape), full2(lam.shape), full2(dskip.shape)],
        out_specs=[rows, full3(bmat.shape), full3(cmat.shape), full2(lam.shape), full2(dskip.shape)],
        out_shape=[jax.ShapeDtypeStruct((s, w), BF16), jax.ShapeDtypeStruct(bmat.shape, F32),
                   jax.ShapeDtypeStruct(cmat.shape, F32), jax.ShapeDtypeStruct(lam.shape, F32),
                   jax.ShapeDtypeStruct(dskip.shape, F32)],
        scratch_shapes=[pltpu.VMEM((2, ns), F32), pltpu.VMEM((nj, t_rows, 2 * SSM_TILE_ST), F32),
                        pltpu.VMEM((t_rows, 2 * SSM_TILE_ST), F32), pltpu.VMEM((t_rows, 2 * SSM_TILE_ST), F32),
                        pltpu.VMEM((t_rows, SSM_TILE_CH), F32), pltpu.VMEM((2, SSM_SEGMENTS, ns), F32)],
        semantics=("arbitrary",), args=[*([z] * nj), y, dyg, xin, bmat, cmat, lam, dskip], side=side)
    return outs if side is None else (outs, carried)


def _adam_math(w, g, m, v):
    m = ADAM_B1 * m + (1.0 - ADAM_B1) * g
    v = ADAM_B2 * v + (1.0 - ADAM_B2) * (g * g)
    m_hat = m / (1.0 - ADAM_B1 ** ADAM_STEP)
    v_hat = v / (1.0 - ADAM_B2 ** ADAM_STEP)
    delta = -ADAM_LR * (m_hat / (jnp.sqrt(v_hat) + ADAM_EPS) + ADAM_WD * w)
    return delta, m, v


def _adam_rows(r, c):
    for tr in (512, 256, 128, 64, 32, 16, 8):
        if r % tr == 0 and tr * c * 4 <= (1 << 20):
            return tr
    return r


def _adamw_big(w, p_mine, p_sib, m, v, name):
    r, c = w.shape
    tr = _adam_rows(r, c)

    def body(w_ref, a_ref, b_ref, m_ref, v_ref, g_ref, d_ref, nm_ref, nv_ref):
        g = a_ref[...] + b_ref[...]
        g_ref[...] = g
        d_ref[...], nm_ref[...], nv_ref[...] = _adam_math(w_ref[...], g, m_ref[...], v_ref[...])

    blk = pl.BlockSpec((tr, c), lambda i: (i, 0))
    return pl.pallas_call(body, name=f"adamw_{name}", grid=(r // tr,), in_specs=[blk] * 5, out_specs=[blk] * 4,
                          out_shape=[jax.ShapeDtypeStruct((r, c), F32)] * 4,
                          compiler_params=_params(("parallel",)))(w, p_mine, p_sib, m, v)


def _adamw_small(w, parts, m, v):
    r, c = w.shape
    n_dev = parts.shape[0]

    def body(w_ref, p_ref, m_ref, v_ref, g_ref, d_ref, nm_ref, nv_ref):
        g = p_ref[0]
        for k in range(1, n_dev):
            g = g + p_ref[k]
        g_ref[...] = g
        d_ref[...], nm_ref[...], nv_ref[...] = _adam_math(w_ref[...], g, m_ref[...], v_ref[...])

    blk = pl.BlockSpec((r, c), lambda i: (0, 0))
    return pl.pallas_call(body, name="adamw_small", grid=(1,),
                          in_specs=[blk, pl.BlockSpec((n_dev, r, c), lambda i: (0, 0, 0)), blk, blk],
                          out_specs=[blk] * 4, out_shape=[jax.ShapeDtypeStruct((r, c), F32)] * 4,
                          compiler_params=_params(("arbitrary",)))(w, parts, m, v)


def _cast_bf16(w, name):
    r, c = w.shape
    tr = _adam_rows(r, c)

    def body(w_ref, o_ref):
        o_ref[...] = w_ref[...].astype(BF16)

    blk = pl.BlockSpec((tr, c), lambda i: (i, 0))
    return pl.pallas_call(body, name=f"cast_{name}", grid=(r // tr,), in_specs=[blk], out_specs=blk,
                          out_shape=jax.ShapeDtypeStruct((r, c), BF16), compiler_params=_params(("parallel",)))(w)


def _sum_slots(recv, name):
    _, r, c = recv.shape
    tr = _adam_rows(r, c)

    def body(p_ref, o_ref):
        acc = p_ref[0].astype(F32)
        for k in range(1, N_CHIPS):
            acc = acc + p_ref[k].astype(F32)
        o_ref[...] = acc

    return pl.pallas_call(body, name=f"sum_{name}", grid=(r // tr,),
                          in_specs=[pl.BlockSpec((N_CHIPS, tr, c), lambda i: (0, i, 0))],
                          out_specs=pl.BlockSpec((tr, c), lambda i: (i, 0)),
                          out_shape=jax.ShapeDtypeStruct((r, c), F32), compiler_params=_params(("parallel",)))(recv)


BIG_WEIGHTS = ("w_in", "w_attn_up", "w_glu_v", "w_glu_g", "w_out", "w_ffn_gate", "w_ffn_up", "w_ffn_down")
COL_SHARDED = ("w_in", "w_attn_up", "w_glu_v", "w_glu_g", "w_ffn_gate", "w_ffn_up")


def _aligned(v, m):
    return v if isinstance(v, int) else pl.multiple_of(v, m)


def _shard_of(ref, name, j, shard_shape, half=None):
    r, c = shard_shape
    rows = r if half is None else r // 2
    row0 = 0 if half is None else half * rows
    if name in COL_SHARDED:
        return ref.at[pl.ds(_aligned(row0, 16), rows), pl.ds(_aligned(j * c, 128), c)]
    return ref.at[pl.ds(_aligned(j * r + row0, 16), rows), :]


def _other_chips():
    x, y = lax.axis_index("x"), lax.axis_index("y")
    return [(1 - x, y), (x, 1 - y), (1 - x, 1 - y)]


def _dma_sems(n, arrays):
    return [pltpu.SemaphoreType.DMA((n, 3))] * arrays + [pltpu.SemaphoreType.DMA((n,))]


def _gather_side(shards):
    names = list(shards)
    n = len(names)
    full_shapes = []
    for k in names:
        r, c = shards[k].shape
        full_shapes.append((r, c * N_CHIPS) if k in COL_SHARDED else (r * N_CHIPS, c))

    def build(src, dst, sems):
        send_sems, recv_sems, pass_send_sems, pass_recv_sems, local_sems = sems
        x, y, c = lax.axis_index("x"), lax.axis_index("y"), lax.axis_index("c")
        me = 2 * x + y
        locals_, sends, arrivals, forwards, passed_on = [], [], [], [], []
        for i, k in enumerate(names):
            shape = shards[k].shape
            half_rows = shape[0] // 2
            locals_.append(pltpu.make_async_copy(src[i], _shard_of(dst[i], k, me, shape), local_sems.at[i]))
            my_half = src[i].at[pl.ds(_aligned(c * half_rows, 16), half_rows), :]
            for p, (px, py) in enumerate(_other_chips()):
                peer = 2 * px + py
                landed = _shard_of(dst[i], k, peer, shape, half=c)
                sends.append(pltpu.make_async_remote_copy(
                    src_ref=my_half, dst_ref=_shard_of(dst[i], k, me, shape, half=c), send_sem=send_sems.at[i, p],
                    recv_sem=recv_sems.at[i, p], device_id=(px, py, c), device_id_type=MESH))
                arrivals.append(pltpu.make_async_remote_copy(
                    src_ref=my_half, dst_ref=landed, send_sem=send_sems.at[i, p],
                    recv_sem=recv_sems.at[i, p], device_id=(px, py, c), device_id_type=MESH))
                forwards.append(pltpu.make_async_remote_copy(
                    src_ref=landed, dst_ref=landed, send_sem=pass_send_sems.at[i, p],
                    recv_sem=pass_recv_sems.at[i, p], device_id=(x, y, 1 - c), device_id_type=MESH))
                passed_on.append(pltpu.make_async_remote_copy(
                    src_ref=landed, dst_ref=_shard_of(dst[i], k, peer, shape, half=1 - c),
                    send_sem=pass_send_sems.at[i, p], recv_sem=pass_recv_sems.at[i, p],
                    device_id=(x, y, 1 - c), device_id_type=MESH))
        return locals_, sends, arrivals, forwards, passed_on

    return _Side([shards[k] for k in names], [jax.ShapeDtypeStruct(s, BF16) for s in full_shapes], _dma_sems(n, 4), build)


def _scatter_side(grads, shard_shapes):
    names = list(grads)
    n = len(names)

    def build(src, dst, sems):
        send_sems, recv_sems, local_sems = sems
        x, y, c = lax.axis_index("x"), lax.axis_index("y"), lax.axis_index("c")
        me = 2 * x + y
        locals_, sends, arrivals = [], [], []
        for i, k in enumerate(names):
            shape = shard_shapes[k]
            locals_.append(pltpu.make_async_copy(_shard_of(src[i], k, me, shape), dst[i].at[me], local_sems.at[i]))
            for p, (px, py) in enumerate(_other_chips()):
                peer = 2 * px + py
                sends.append(pltpu.make_async_remote_copy(
                    src_ref=_shard_of(src[i], k, peer, shape), dst_ref=dst[i].at[me], send_sem=send_sems.at[i, p],
                    recv_sem=recv_sems.at[i, p], device_id=(px, py, c), device_id_type=MESH))
                arrivals.append(pltpu.make_async_remote_copy(
                    src_ref=_shard_of(src[i], k, peer, shape), dst_ref=dst[i].at[peer], send_sem=send_sems.at[i, p],
                    recv_sem=recv_sems.at[i, p], device_id=(px, py, c), device_id_type=MESH))
        return locals_, sends, arrivals, [None] * len(arrivals), []

    return _Side([grads[k] for k in names],
                 [jax.ShapeDtypeStruct((N_CHIPS,) + tuple(shard_shapes[k]), BF16) for k in names], _dma_sems(n, 2), build)


def _put_cols(dz, src, col_off):
    s, w = src.shape
    tr = _pick(s, (2048, 1024, 512, 256, 128, 64, 8))
    tc = _pick(math.gcd(w, col_off), (1024, 512, 256, 128))
    off = col_off // tc

    def body(src_ref, dz_ref, o_ref):
        del dz_ref
        o_ref[...] = src_ref[...].astype(o_ref.dtype)

    return pl.pallas_call(
        body, name="put_cols", grid=(s // tr, w // tc),
        in_specs=[pl.BlockSpec((tr, tc), lambda i, j: (i, j)), pl.BlockSpec(memory_space=pl.ANY)],
        out_specs=pl.BlockSpec((tr, tc), lambda i, j: (i, off + j)),
        out_shape=jax.ShapeDtypeStruct(dz.shape, dz.dtype), input_output_aliases={1: 0},
        compiler_params=_params(("parallel", "parallel")))(src, dz)


def _swap_side(parts):
    n = len(parts)

    def build(src, dst, sems):
        send_sems, recv_sems = sems
        sibling = (lax.axis_index("x"), lax.axis_index("y"), 1 - lax.axis_index("c"))
        copies = [pltpu.make_async_remote_copy(src_ref=src[i], dst_ref=dst[i], send_sem=send_sems.at[i],
                                               recv_sem=recv_sems.at[i], device_id=sibling, device_id_type=MESH)
                  for i in range(n)]
        return [], copies, copies, [None] * n, []

    return _Side(parts, [jax.ShapeDtypeStruct(p.shape, F32) for p in parts],
                 [pltpu.SemaphoreType.DMA((n,)), pltpu.SemaphoreType.DMA((n,))], build)


def _share_side(packed):
    r, c = packed.shape

    def build(src, dst, sems):
        send_sems, recv_sems, local_sem = sems
        x, y, cc = lax.axis_index("x"), lax.axis_index("y"), lax.axis_index("c")
        me = 4 * x + 2 * y + cc
        own = pltpu.make_async_copy(src[0], dst[0].at[me], local_sem)
        sends, arrivals = [], []
        flips = [(fx, fy, fc) for fx in range(2) for fy in range(2) for fc in range(2) if fx or fy or fc]
        for p, (fx, fy, fc) in enumerate(flips):
            px, py, pc = x ^ fx, y ^ fy, cc ^ fc
            sends.append(pltpu.make_async_remote_copy(
                src_ref=src[0], dst_ref=dst[0].at[me], send_sem=send_sems.at[p], recv_sem=recv_sems.at[p],
                device_id=(px, py, pc), device_id_type=MESH))
            arrivals.append(pltpu.make_async_remote_copy(
                src_ref=src[0], dst_ref=dst[0].at[4 * px + 2 * py + pc], send_sem=send_sems.at[p],
                recv_sem=recv_sems.at[p], device_id=(px, py, pc), device_id_type=MESH))
        return [own], sends, arrivals, [None] * len(arrivals), []

    return _Side([packed], [jax.ShapeDtypeStruct((8, r, c), F32)],
                 [pltpu.SemaphoreType.DMA((7,)), pltpu.SemaphoreType.DMA((7,)), pltpu.SemaphoreType.DMA], build)


SMALL_WEIGHTS = ("norm_mix_pre", "ssm_a_re", "ssm_a_im", "ssm_log_dt", "ssm_b_re", "ssm_b_im", "ssm_c_re", "ssm_c_im",
                 "ssm_d", "norm_mix_post", "norm_ffn_pre", "norm_ffn_post")
WEIGHT_ORDER = ("norm_mix_pre", "w_in", "w_attn_up", "ssm_a_re", "ssm_a_im", "ssm_log_dt", "ssm_b_re", "ssm_b_im",
                "ssm_c_re", "ssm_c_im", "ssm_d", "w_glu_v", "w_glu_g", "w_out", "norm_mix_post", "norm_ffn_pre",
                "w_ffn_gate", "w_ffn_up", "w_ffn_down", "norm_ffn_post")
PACK_LANES = 128
PACK_ROWS = 8
PACK_GROUPS = (SMALL_WEIGHTS[:1], SMALL_WEIGHTS[1:])


def _pack_group(arrs, names):
    flat = jnp.concatenate([arrs[k].reshape(-1) for k in names])
    pad = -flat.shape[0] % (PACK_LANES * PACK_ROWS)
    return jnp.pad(flat, (0, pad)).reshape(-1, PACK_LANES)


def _pack_small(arrs):
    return jnp.concatenate([_pack_group(arrs, names) for names in PACK_GROUPS], axis=0)


def _unpack_small(packed, like):
    out, row = {}, 0
    for names in PACK_GROUPS:
        rows = _pack_group(like, names).shape[0]
        flat, pos = packed[row:row + rows].reshape(-1), 0
        for k in names:
            n = like[k].size
            out[k] = flat[pos:pos + n].reshape(like[k].shape)
            pos += n
        row += rows
    return out


def _local_step(x, target, big, small, shards=None, shard_shapes=None, h1=None):
    s, d = x.shape
    big, grads, slots = dict(big), {}, {}
    carry = shards is not None

    def gathering(names, call):
        if not carry:
            return call(None)
        res, got = call(_gather_side({k: shards[k] for k in names}))
        big.update(zip(names, got))
        return res

    def scattering(names, call):
        if not carry:
            return call(None)
        res, got = call(_scatter_side({k: grads[k] for k in names}, shard_shapes))
        slots.update(zip(names, got))
        return res

    u_off = 3 * HQ
    gate_off = u_off + d // 2
    g1, g2, g3, g4 = (small[k][0:1] for k in ("norm_mix_pre", "norm_mix_post", "norm_ffn_pre", "norm_ffn_post"))
    ssm_names = ("ssm_a_re", "ssm_a_im", "ssm_log_dt", "ssm_b_re", "ssm_b_im", "ssm_c_re", "ssm_c_im")
    (lam, bmat, cmat), ssm_vjp = jax.vjp(_ssm_prepare, *[small[k][0] for k in ssm_names])
    bmat, cmat = bmat.astype(BF16), cmat.astype(BF16)
    dskip = small["ssm_d"][0:1]

    if h1 is None:
        h1 = _norm_in(x, g1)
    z = gathering(("w_attn_up", "w_glu_v", "w_glu_g", "w_out", "w_ffn_gate"),
                  lambda side: _mm(h1, big["w_in"], "nn", F32, "in_proj", side=side))
    y, yg, xin = gathering(("w_ffn_up",), lambda side: _ssm_fwd(z, bmat, cmat, lam, dskip, u_off, side=side))
    qkv = [_dilate_qkv(z, g, dil) for g, dil in enumerate(ATTN_DILATIONS)]
    outs, lses = zip(*[_attn_fwd(qkv[g], g, dil) for g, dil in enumerate(ATTN_DILATIONS)])
    attn = _attn_merge(outs, lses)
    merged, ab, gv, gg = _mm_fused(
        [attn, yg], [big["w_attn_up"], big["w_glu_v"], big["w_glu_g"]], [(0, 0), (1, 1), (1, 2)], "nn",
        [BF16, BF16, BF16, BF16], "branches_merge", extras=[(z, gate_off), (z, gate_off + d)], epilogue=_gates_epilogue)
    mo = _mm(merged, big["w_out"], "nn", F32, "mix_out")
    x2, h2 = _norm_mid(x, mo, g2, g3)
    act, fg, fu = gathering(("w_ffn_down",), lambda side: _mm_fused(
        [h2], [big["w_ffn_gate"], big["w_ffn_up"]], [(0, 0), (0, 1)], "nn", [BF16, BF16, BF16], "ffn_up_act",
        epilogue=_swiglu_epilogue, side=side))
    f = _mm(act, big["w_ffn_down"], "nn", F32, "ffn_down")
    loss, dout, df, dg4 = _loss_head(x2, f, g4, target)

    dfg, dfu = _mm_fused([df], [big["w_ffn_down"]], [(0, 0)], "nt", [BF16, BF16], "d_ffn_act",
                         extras=[(fg, 0), (fu, 0)], epilogue=_swiglu_bwd_epilogue)
    grads["w_ffn_down"] = _mm_kloop(act, df, "tn", BF16, "dw_ffn_down")
    grads["w_ffn_gate"] = scattering(("w_ffn_down",), lambda side: _mm_kloop(h2, dfg, "tn", BF16, "dw_ffn_gate", side=side))
    grads["w_ffn_up"] = scattering(("w_ffn_gate",), lambda side: _mm_kloop(h2, dfu, "tn", BF16, "dw_ffn_up", side=side))
    dh2 = scattering(("w_ffn_up",), lambda side: _mm_kloop(dfg, big["w_ffn_gate"], "nt", F32, "d_h2", side=side,
                                                          second=(dfu, big["w_ffn_up"])))
    dx2, dmo, dg2, dg3 = _norm_mid_bwd(x2, mo, g2, g3, dout, dh2)
    dz, dgs, dab, dgv, dgg = _mm_fused(
        [dmo], [big["w_out"]], [(0, 0)], "nt", [BF16] * 5, "d_merged_gates",
        extras=[(z, gate_off), (z, gate_off + d), (ab, 0), (gv, 0), (gg, 0)], epilogue=_gates_bwd_epilogue,
        out_place=[(z.shape[1], gate_off), None, None, None, None])
    dz = _put_cols(dz, dgs, gate_off + d)
    grads["w_out"] = _mm_kloop(merged, dmo, "tn", BF16, "dw_out")
    dyg = _mm_fused([dgv, dgg], [big["w_glu_v"], big["w_glu_g"]], [(0, 0), (1, 1)], "nt", [F32], "d_yg",
                    epilogue=_sum_epilogue)[0]
    grads["w_glu_v"] = _mm_kloop(yg, dgv, "tn", BF16, "dw_glu_v")
    grads["w_glu_g"] = _mm_kloop(yg, dgg, "tn", BF16, "dw_glu_g")
    du, dbmat, dcmat, dlam, dd = scattering(
        ("w_out", "w_glu_v", "w_glu_g"),
        lambda side: _ssm_bwd(z, y, dyg, xin, bmat, cmat, lam, dskip, u_off, side=side))
    dz = _put_cols(dz, du, u_off)
    dattn = _mm(dab, big["w_attn_up"], "nt", F32, "d_attn")
    grads["w_attn_up"] = _mm_kloop(attn, dab, "tn", BF16, "dw_attn_up")
    merged_bwd = _attn_merge_bwd(outs, lses, dattn)
    mine, theirs = {}, {}
    for g, dil in enumerate(ATTN_DILATIONS):
        side = None
        if carry and g == 0:
            mine = {k: _sum_slots(slots[k], k) for k in slots}
            side = _swap_side(list(mine.values()))
        dqkv = _attn_bwd(qkv[g], merged_bwd[g], lses[g], merged_bwd[3 + g], g, dil, side=side)
        if side is not None:
            dqkv, got = dqkv
            theirs = dict(zip(mine, got))
        dz = _undilate_dqkv(dqkv, dz, g, dil)
    small_grads = dict(zip(ssm_names, (t[None] for t in ssm_vjp((dlam, dbmat, dcmat)))))
    small_grads.update(norm_mix_post=dg2, norm_ffn_pre=dg3, norm_ffn_post=dg4, ssm_d=dd)
    if carry:
        side = _join_sides(_scatter_side({"w_attn_up": grads["w_attn_up"]}, shard_shapes),
                           _share_side(_pack_group(small_grads, PACK_GROUPS[1])))
        grads["w_in"], (slots["w_attn_up"], shared) = _mm_kloop(h1, dz, "tn", BF16, "dw_in", side=side)
    else:
        grads["w_in"] = _mm_kloop(h1, dz, "tn", BF16, "dw_in")
    dh1 = scattering(("w_in",), lambda side: _mm_kloop(dz, big["w_in"], "nt", F32, "d_h1", side=side))
    grad_x, dg1 = _norm_in_bwd(x, g1, dh1, dx2)
    small_grads["norm_mix_pre"] = dg1
    if carry:
        return loss[0, 0], grad_x, (slots, mine, theirs), (dg1, shared)
    return loss[0, 0], grad_x, grads, small_grads


def kernel(x, norm_mix_pre, w_in, w_attn_up, ssm_a_re, ssm_a_im, ssm_log_dt, ssm_b_re, ssm_b_im, ssm_c_re, ssm_c_im, ssm_d, w_glu_v, w_glu_g, w_out, norm_mix_post, norm_ffn_pre, w_ffn_gate, w_ffn_up, w_ffn_down, norm_ffn_post, loss_target, m_norm_mix_pre, m_w_in, m_w_attn_up, m_ssm_a_re, m_ssm_a_im, m_ssm_log_dt, m_ssm_b_re, m_ssm_b_im, m_ssm_c_re, m_ssm_c_im, m_ssm_d, m_w_glu_v, m_w_glu_g, m_w_out, m_norm_mix_post, m_norm_ffn_pre, m_w_ffn_gate, m_w_ffn_up, m_w_ffn_down, m_norm_ffn_post, v_norm_mix_pre, v_w_in, v_w_attn_up, v_ssm_a_re, v_ssm_a_im, v_ssm_log_dt, v_ssm_b_re, v_ssm_b_im, v_ssm_c_re, v_ssm_c_im, v_ssm_d, v_w_glu_v, v_w_glu_g, v_w_out, v_norm_mix_post, v_norm_ffn_pre, v_w_ffn_gate, v_w_ffn_up, v_w_ffn_down, v_norm_ffn_post):
    given = dict(locals())
    w = {k: given[k] for k in WEIGHT_ORDER}
    m = {k: given["m_" + k] for k in WEIGHT_ORDER}
    v = {k: given["v_" + k] for k in WEIGHT_ORDER}

    shard_shapes = {k: w[k].shape[1:] for k in BIG_WEIGHTS}
    shards = {"w_in": _cast_bf16(w["w_in"][0], "w_in")}
    h1, casts, got = _prologue(x[0], norm_mix_pre[0:1], {k: w[k][0] for k in BIG_WEIGHTS if k != "w_in"},
                               side=_gather_side({"w_in": shards["w_in"]}))
    shards.update(casts)
    big = {"w_in": got[0]}

    loss, grad_x, (slots, mine, theirs), small_grads = _local_step(
        x[0], loss_target[0], big, {k: w[k] for k in SMALL_WEIGHTS}, shards, shard_shapes, h1)
    loss = lax.psum(loss, MESH_AXES)

    last = [k for k in BIG_WEIGHTS if k not in mine]
    mine.update({k: _sum_slots(slots[k], k) for k in last})
    theirs.update(zip(last, _run_side(_swap_side([mine[k] for k in last]), "swap_last_grads")))
    out_g, out_d, out_m, out_v = {}, {}, {}, {}
    for k in BIG_WEIGHTS:
        res = _adamw_big(w[k][0], mine[k], theirs[k], m[k][0], v[k][0], k)
        out_g[k], out_d[k], out_m[k], out_v[k] = (t[None] for t in res)

    pick = lambda tree: {k: tree[k] for k in SMALL_WEIGHTS}
    dg1, shared = small_grads
    late = _run_side(_share_side(_pack_group({"norm_mix_pre": dg1}, PACK_GROUPS[0])), "share_last_grad")[0]
    parts = jnp.concatenate([late, shared], axis=1)
    res = _adamw_small(_pack_small(pick(w)), parts, _pack_small(pick(m)), _pack_small(pick(v)))
    for dst, packed in zip((out_g, out_d, out_m, out_v), res):
        dst.update(_unpack_small(packed, pick(w)))

    return (loss, grad_x[None], *[out_g[k] for k in WEIGHT_ORDER], *[out_d[k] for k in WEIGHT_ORDER],
            *[out_m[k] for k in WEIGHT_ORDER], *[out_v[k] for k in WEIGHT_ORDER])
```

```python
import functools
import math

import jax
import jax.numpy as jnp
from jax import lax
from jax.experimental import pallas as pl
from jax.experimental.pallas import tpu as pltpu

F32 = jnp.float32
BF16 = jnp.bfloat16

EPS = 1e-6
HEAD_DIM = 128
HEADS_PER_GROUP = 4
ATTN_DILATIONS = (1, 4, 16)
ATTN_BLK = 128
N_ATTN_HEADS = HEADS_PER_GROUP * len(ATTN_DILATIONS)
GROUP_W = HEADS_PER_GROUP * HEAD_DIM
HQ = N_ATTN_HEADS * HEAD_DIM
SSM_GROUP = 16
SSM_STATE = 64
SSM_TILE_CH = 128
SSM_TILE_ST = SSM_TILE_CH // SSM_GROUP * SSM_STATE
ADAM_LR = 0.001
ADAM_B1 = 0.9
ADAM_B2 = 0.999
ADAM_EPS = 1e-08
ADAM_WD = 0.01
ADAM_STEP = 10
NEG_BIG = -1e30
V7X_VMEM_LIMIT = 56 * 1024 * 1024
MESH_AXES = ("x", "y", "c")
N_CHIPS = 4


def _pick(n, cands):
    for c in cands:
        if n % c == 0:
            return c
    raise ValueError(f"no tile of {cands} divides {n}")


def _params(sem):
    return pltpu.CompilerParams(dimension_semantics=sem, vmem_limit_bytes=V7X_VMEM_LIMIT)


HBM = pl.BlockSpec(memory_space=pl.ANY)
MESH = pl.DeviceIdType.MESH


class _Side:
    def __init__(self, srcs, out_shapes, sem_shapes, build, aliases=None):
        self.srcs, self.out_shapes, self.sem_shapes, self.build = list(srcs), list(out_shapes), list(sem_shapes), build
        self.aliases = dict(aliases or {})

    def start(self, src, dst, sems):
        local, sends = self.build(src, dst, sems)[:2]
        for cp in local + sends:
            cp.start()

    def wait(self, src, dst, sems):
        local, sends, arrivals, forwards, passed_on = self.build(src, dst, sems)
        for cp, forward in zip(arrivals, forwards):
            cp.wait_recv()
            if forward is not None:
                forward.start()
        for cp in passed_on:
            cp.wait_recv()
        for cp in sends + [f for f in forwards if f is not None]:
            cp.wait_send()
        for cp in local:
            cp.wait()


def _join_sides(a, b):
    ns, no, nm = len(a.srcs), len(a.out_shapes), len(a.sem_shapes)

    def build(src, dst, sems):
        ra, rb = a.build(src[:ns], dst[:no], sems[:nm]), b.build(src[ns:], dst[no:], sems[nm:])
        return tuple(p + q for p, q in zip(ra, rb))

    aliases = {**a.aliases, **{ns + k: no + v for k, v in b.aliases.items()}}
    return _Side(a.srcs + b.srcs, a.out_shapes + b.out_shapes, a.sem_shapes + b.sem_shapes, build, aliases)


def _call(body, *, name, grid, in_specs, out_specs, out_shape, semantics, args, scratch_shapes=(), side=None, **kw):
    in_specs, out_specs, out_shape, scratch_shapes = list(in_specs), list(out_specs), list(out_shape), list(scratch_shapes)
    if side is None:
        res = pl.pallas_call(body, name=name, grid=grid, in_specs=in_specs, out_specs=out_specs, out_shape=out_shape,
                             scratch_shapes=scratch_shapes, compiler_params=_params(semantics), **kw)(*args)
        return list(res), []
    n_in, n_out, n_scr = len(in_specs), len(out_specs), len(scratch_shapes)
    ns_in, ns_out = len(side.srcs), len(side.out_shapes)

    def carrying(*refs):
        ins, s_in = refs[:n_in], refs[n_in:n_in + ns_in]
        o0 = n_in + ns_in
        outs, s_out = refs[o0:o0 + n_out], refs[o0 + n_out:o0 + n_out + ns_out]
        c0 = o0 + n_out + ns_out
        scr, sems = refs[c0:c0 + n_scr], refs[c0 + n_scr:]
        ids = [pl.program_id(a) for a in range(len(grid))]
        first = functools.reduce(jnp.logical_and, [i == 0 for i in ids])
        last = functools.reduce(jnp.logical_and, [i == g - 1 for i, g in zip(ids, grid)])

        @pl.when(first)
        def _():
            side.start(s_in, s_out, sems)

        body(*ins, *outs, *scr)

        @pl.when(last)
        def _():
            side.wait(s_in, s_out, sems)

    res = pl.pallas_call(
        carrying, name=name, grid=grid, in_specs=in_specs + [HBM] * ns_in, out_specs=out_specs + [HBM] * ns_out,
        out_shape=out_shape + side.out_shapes, scratch_shapes=scratch_shapes + side.sem_shapes,
        input_output_aliases={n_in + k: n_out + v for k, v in side.aliases.items()},
        compiler_params=pltpu.CompilerParams(dimension_semantics=("arbitrary",) * len(grid),
                                             vmem_limit_bytes=V7X_VMEM_LIMIT, has_side_effects=True), **kw,
    )(*args, *side.srcs)
    return list(res[:n_out]), list(res[n_out:])


def _run_side(side, name):
    ns, no = len(side.srcs), len(side.out_shapes)

    def body(*refs):
        src, dst, sems = refs[:ns], refs[ns:ns + no], refs[ns + no:]
        side.start(src, dst, sems)
        side.wait(src, dst, sems)

    return list(pl.pallas_call(body, name=name, in_specs=[HBM] * ns, out_specs=[HBM] * no, out_shape=side.out_shapes,
                               scratch_shapes=side.sem_shapes,
                               compiler_params=pltpu.CompilerParams(has_side_effects=True))(*side.srcs))


_DOT_DIMS = {"nn": (((1,), (0,)), ((), ())), "nt": (((1,), (1,)), ((), ())), "tn": (((0,), (0,)), ((), ()))}


MM_VMEM_BUDGET = 44 * 1024 * 1024
MM_STEP_BYTES = 1 << 20
MM_ACC_BYTES = 4
MM_EPILOGUE_COLS = 256


def _size(dtype):
    return jnp.dtype(dtype).itemsize


def _mm_fused(as_, bs, pairs, mode, out_dtypes, name, extras=(), epilogue=None, side=None, out_place=None):
    M = as_[0].shape[0]
    N = bs[0].shape[1] if mode == "nn" else bs[0].shape[0]
    ks_a = [a.shape[1] for a in as_]
    ks_b = [b.shape[0] if mode == "nn" else b.shape[1] for b in bs]
    chunked = epilogue is not None
    if epilogue is None:
        epilogue = lambda rs, es: rs
    offs = [off for _, off in extras]
    place = list(out_place) if out_place else [None] * len(out_dtypes)
    offs_all = offs + [p[1] for p in place if p is not None]
    best = None
    for tm in (2048, 1024, 512, 256, 128):
        for tn in (2048, 1024, 512, 256, 128):
            if M % tm or N % tn or any(off % tn for off in offs_all):
                continue
            vmem = (sum(2 * tm * k * 2 for k in ks_a) + sum(2 * k * tn * 2 for k in ks_b)
                    + sum(2 * tm * tn * _size(d) for d in out_dtypes) + sum(2 * tm * tn * _size(e.dtype) for e, _ in extras)
                    + len(pairs) * tm * tn * 4)
            cost = sum(k * N * 2 for k in ks_b) * (M // tm) + (M // tm) * (N // tn) * MM_STEP_BYTES
            if vmem <= MM_VMEM_BUDGET and (best is None or cost < best[0]):
                best = (cost, tm, tn)
    _, tm, tn = best
    na, nb, ne, no = len(as_), len(bs), len(extras), len(out_dtypes)
    dims = _DOT_DIMS[mode]

    sub = MM_EPILOGUE_COLS if chunked and tn % MM_EPILOGUE_COLS == 0 else tn

    def body(*refs):
        a_refs, b_refs = refs[:na], refs[na:na + nb]
        e_refs, o_refs = refs[na + nb:na + nb + ne], refs[na + nb + ne:]
        for c0 in range(0, tn, sub):
            cs = slice(c0, c0 + sub)
            rs = [lax.dot_general(a_refs[ai][...], b_refs[bi][:, cs] if mode == "nn" else b_refs[bi][cs, :], dims,
                                  preferred_element_type=F32) for ai, bi in pairs]
            outs = epilogue(rs, [e[:, cs] for e in e_refs])
            for o_ref, o in zip(o_refs, outs):
                o_ref[:, cs] = o.astype(o_ref.dtype)

    a_specs = [pl.BlockSpec((tm, k), lambda i, j: (i, 0)) for k in ks_a]
    if mode == "nn":
        b_specs = [pl.BlockSpec((k, tn), lambda i, j: (0, j)) for k in ks_b]
    else:
        b_specs = [pl.BlockSpec((tn, k), lambda i, j: (j, 0)) for k in ks_b]
    e_specs = [pl.BlockSpec((tm, tn), lambda i, j, o=off // tn: (i, o + j)) for off in offs]
    o_specs = [pl.BlockSpec((tm, tn), lambda i, j, o=(p[1] // tn if p else 0): (i, o + j)) for p in place]
    outs, carried = _call(
        body, name=name, grid=(M // tm, N // tn), in_specs=a_specs + b_specs + e_specs, out_specs=o_specs,
        out_shape=[jax.ShapeDtypeStruct((M, p[0] if p else N), d) for d, p in zip(out_dtypes, place)],
        semantics=("parallel", "arbitrary"),
        args=[*as_, *bs, *[e for e, _ in extras]], side=side)
    return outs if side is None else (outs, carried)


def _mm(a, b, mode, out_dtype, name, side=None):
    res = _mm_fused([a], [b], [(0, 0)], mode, [out_dtype], name, side=side)
    return res[0] if side is None else (res[0][0], res[1])


def _mm_kloop(a, b, mode, out_dtype, name, second=None, side=None):
    if mode == "nn":
        (M, K), (_, N) = a.shape, b.shape
    elif mode == "nt":
        (M, K), (N, _) = a.shape, b.shape
    else:
        (K, M), (_, N) = a.shape, b.shape
    products = 1 if second is None else 2
    best = None
    for tm in (2816, 2048, 1408, 1024, 512, 256, 128):
        for tn in (2816, 2432, 2048, 1408, 1024, 512, 256, 128):
            for tk in (2816, 2432, 2048, 1408, 1024, 512, 256, 128):
                if M % tm or N % tn or K % tk:
                    continue
                vmem = 2 * tm * tn * 4 + 2 * tm * tn * _size(out_dtype) + products * 2 * tk * (tm + tn) * 2
                steps = (M // tm) * (N // tn) * (K // tk)
                cost = (K * M * 2 * (N // tn) + K * N * 2 * (M // tm) + steps * MM_STEP_BYTES
                        + steps * tm * tn * MM_ACC_BYTES)
                if vmem <= MM_VMEM_BUDGET and (best is None or cost < best[0]):
                    best = (cost, tm, tn, tk)
    _, tm, tn, tk = best
    nk = K // tk
    dims = _DOT_DIMS[mode]

    def body(*refs):
        o_ref, acc_ref = refs[-2:]
        k = pl.program_id(2)

        @pl.when(k == 0)
        def _():
            acc_ref[...] = jnp.zeros_like(acc_ref)

        for p in range(products):
            @pl.when(jnp.logical_and(k >= p * nk, k < (p + 1) * nk))
            def _(p=p):
                acc_ref[...] += lax.dot_general(refs[2 * p][...], refs[2 * p + 1][...], dims, preferred_element_type=F32)

        @pl.when(k == products * nk - 1)
        def _():
            o_ref[...] = acc_ref[...].astype(o_ref.dtype)

    def a_spec(p):
        kk = lambda k: jnp.clip(k - p * nk, 0, nk - 1)
        if mode == "tn":
            return pl.BlockSpec((tk, tm), lambda i, j, k: (kk(k), i))
        return pl.BlockSpec((tm, tk), lambda i, j, k: (i, kk(k)))

    def b_spec(p):
        kk = lambda k: jnp.clip(k - p * nk, 0, nk - 1)
        if mode == "nt":
            return pl.BlockSpec((tn, tk), lambda i, j, k: (j, kk(k)))
        return pl.BlockSpec((tk, tn), lambda i, j, k: (kk(k), j))

    o_spec = pl.BlockSpec((tm, tn), lambda i, j, k: (i, j))
    operands = (a, b) + (tuple(second) if second is not None else ())
    outs, carried = _call(
        body, name=name, grid=(M // tm, N // tn, products * nk),
        in_specs=[spec(p) for p in range(products) for spec in (a_spec, b_spec)], out_specs=[o_spec],
        out_shape=[jax.ShapeDtypeStruct((M, N), out_dtype)], scratch_shapes=[pltpu.VMEM((tm, tn), F32)],
        semantics=("parallel", "parallel", "arbitrary"), args=operands, side=side)
    return outs[0] if side is None else (outs[0], carried)


def _sigmoid(v):
    return 0.5 * jnp.tanh(0.5 * v) + 0.5


_GELU_C = math.sqrt(2.0 / math.pi)


def _gelu(v):
    return 0.5 * v * (1.0 + jnp.tanh(_GELU_C * (v + 0.044715 * v * v * v)))


def _gelu_grad(v):
    t = jnp.tanh(_GELU_C * (v + 0.044715 * v * v * v))
    return 0.5 * (1.0 + t) + 0.5 * v * (1.0 - t * t) * _GELU_C * (1.0 + 3.0 * 0.044715 * v * v)


def _rms(v, gain):
    r = lax.rsqrt(jnp.mean(v * v, axis=-1, keepdims=True) + EPS)
    return v * r * gain


def _rms_bwd(v, gain, dy):
    r = lax.rsqrt(jnp.mean(v * v, axis=-1, keepdims=True) + EPS)
    a = dy * gain
    dv = r * a - v * (r * r * r) * jnp.mean(a * v, axis=-1, keepdims=True)
    return dv, dy * v * r


def _row_tile(s):
    return _pick(s, (256, 128, 64, 8))


def _norm_in(x, gain):
    s, d = x.shape
    tr = _row_tile(s)

    def body(x_ref, g_ref, h_ref):
        h_ref[...] = _rms(x_ref[...], g_ref[...]).astype(BF16)

    row = pl.BlockSpec((tr, d), lambda i: (i, 0))
    vec = pl.BlockSpec((1, d), lambda i: (0, 0))
    return pl.pallas_call(body, name="norm_in", grid=(s // tr,), in_specs=[row, vec], out_specs=row,
                          out_shape=jax.ShapeDtypeStruct((s, d), BF16), compiler_params=_params(("parallel",)))(x, gain)


def _prologue(x, gain, weights, side=None):
    s, d = x.shape
    tr = _row_tile(s)
    steps = s // tr
    names = list(weights)
    tiles = []
    for k in names:
        r, _ = weights[k].shape
        tiles.append(next(t for t in range(16, r + 1, 16) if r % t == 0 and r // t <= steps))

    def body(*refs):
        x_ref, g_ref = refs[:2]
        w_refs, h_ref, o_refs = refs[2:2 + len(names)], refs[2 + len(names)], refs[3 + len(names):]
        h_ref[...] = _rms(x_ref[...], g_ref[...]).astype(BF16)
        for w_ref, o_ref in zip(w_refs, o_refs):
            o_ref[...] = w_ref[...].astype(BF16)

    row = pl.BlockSpec((tr, d), lambda i: (i, 0))
    w_specs = [pl.BlockSpec((t, weights[k].shape[1]), lambda i, last=weights[k].shape[0] // t - 1: (jnp.minimum(i, last), 0))
               for k, t in zip(names, tiles)]
    outs, carried = _call(
        body, name="prologue", grid=(steps,), in_specs=[row, pl.BlockSpec((1, d), lambda i: (0, 0))] + w_specs,
        out_specs=[row] + w_specs,
        out_shape=[jax.ShapeDtypeStruct((s, d), BF16)] + [jax.ShapeDtypeStruct(weights[k].shape, BF16) for k in names],
        semantics=("arbitrary",), args=[x, gain] + [weights[k] for k in names], side=side)
    return outs[0], dict(zip(names, outs[1:])), carried


def _norm_mid(x, mo, g_post, g_pre):
    s, d = x.shape
    tr = _row_tile(s)

    def body(x_ref, mo_ref, g2_ref, g3_ref, x2_ref, h2_ref):
        x2 = x_ref[...] + _rms(mo_ref[...], g2_ref[...])
        x2_ref[...] = x2
        h2_ref[...] = _rms(x2, g3_ref[...]).astype(BF16)

    row = pl.BlockSpec((tr, d), lambda i: (i, 0))
    vec = pl.BlockSpec((1, d), lambda i: (0, 0))
    return pl.pallas_call(
        body, name="norm_mid", grid=(s // tr,), in_specs=[row, row, vec, vec], out_specs=[row, row],
        out_shape=[jax.ShapeDtypeStruct((s, d), F32), jax.ShapeDtypeStruct((s, d), BF16)],
        compiler_params=_params(("parallel",)))(x, mo, g_post, g_pre)


def _loss_head(x2, f, g_post, target):
    s, d = x2.shape
    tr = _row_tile(s)

    def body(x2_ref, f_ref, g_ref, t_ref, loss_ref, dout_ref, df_ref, dg_ref):
        @pl.when(pl.program_id(0) == 0)
        def _():
            loss_ref[...] = jnp.zeros_like(loss_ref)
            dg_ref[...] = jnp.zeros_like(dg_ref)

        fv = f_ref[...]
        g = g_ref[...]
        err = x2_ref[...] + _rms(fv, g) - t_ref[...]
        loss_ref[...] += 0.5 * jnp.sum(jnp.mean(err * err, axis=-1, keepdims=True), axis=0, keepdims=True)
        dout = err * (1.0 / d)
        dout_ref[...] = dout
        df, dg = _rms_bwd(fv, g, dout)
        df_ref[...] = df.astype(BF16)
        dg_ref[...] += jnp.sum(dg, axis=0, keepdims=True)

    row = pl.BlockSpec((tr, d), lambda i: (i, 0))
    vec = pl.BlockSpec((1, d), lambda i: (0, 0))
    one = pl.BlockSpec((1, 1), lambda i: (0, 0))
    return pl.pallas_call(
        body, name="loss_head", grid=(s // tr,), in_specs=[row, row, vec, row], out_specs=[one, row, row, vec],
        out_shape=[jax.ShapeDtypeStruct((1, 1), F32), jax.ShapeDtypeStruct((s, d), F32),
                   jax.ShapeDtypeStruct((s, d), BF16), jax.ShapeDtypeStruct((1, d), F32)],
        compiler_params=_params(("arbitrary",)))(x2, f, g_post, target)


def _norm_mid_bwd(x2, mo, g_post, g_pre, dout, dh2):
    s, d = x2.shape
    tr = _row_tile(s)

    def body(x2_ref, mo_ref, g2_ref, g3_ref, dout_ref, dh2_ref, dx2_ref, dmo_ref, dg2_ref, dg3_ref):
        @pl.when(pl.program_id(0) == 0)
        def _():
            dg2_ref[...] = jnp.zeros_like(dg2_ref)
            dg3_ref[...] = jnp.zeros_like(dg3_ref)

        dv, dg3 = _rms_bwd(x2_ref[...], g3_ref[...], dh2_ref[...])
        dx2 = dout_ref[...] + dv
        dx2_ref[...] = dx2
        dmo, dg2 = _rms_bwd(mo_ref[...], g2_ref[...], dx2)
        dmo_ref[...] = dmo.astype(BF16)
        dg2_ref[...] += jnp.sum(dg2, axis=0, keepdims=True)
        dg3_ref[...] += jnp.sum(dg3, axis=0, keepdims=True)

    row = pl.BlockSpec((tr, d), lambda i: (i, 0))
    vec = pl.BlockSpec((1, d), lambda i: (0, 0))
    return pl.pallas_call(
        body, name="norm_mid_bwd", grid=(s // tr,), in_specs=[row, row, vec, vec, row, row],
        out_specs=[row, row, vec, vec],
        out_shape=[jax.ShapeDtypeStruct((s, d), F32), jax.ShapeDtypeStruct((s, d), BF16),
                   jax.ShapeDtypeStruct((1, d), F32), jax.ShapeDtypeStruct((1, d), F32)],
        compiler_params=_params(("arbitrary",)))(x2, mo, g_post, g_pre, dout, dh2)


def _norm_in_bwd(x, gain, dh, dx2):
    s, d = x.shape
    tr = _row_tile(s)

    def body(x_ref, g_ref, dh_ref, dx2_ref, dx_ref, dg_ref):
        @pl.when(pl.program_id(0) == 0)
        def _():
            dg_ref[...] = jnp.zeros_like(dg_ref)

        dv, dg = _rms_bwd(x_ref[...], g_ref[...], dh_ref[...])
        dx_ref[...] = dx2_ref[...] + dv
        dg_ref[...] += jnp.sum(dg, axis=0, keepdims=True)

    row = pl.BlockSpec((tr, d), lambda i: (i, 0))
    vec = pl.BlockSpec((1, d), lambda i: (0, 0))
    return pl.pallas_call(
        body, name="norm_in_bwd", grid=(s // tr,), in_specs=[row, vec, row, row], out_specs=[row, vec],
        out_shape=[jax.ShapeDtypeStruct((s, d), F32), jax.ShapeDtypeStruct((1, d), F32)],
        compiler_params=_params(("arbitrary",)))(x, gain, dh, dx2)


def _swiglu_epilogue(rs, es):
    g, u = rs
    return [g * _sigmoid(g) * u, g, u]


def _swiglu_bwd_epilogue(rs, es):
    d = rs[0]
    g, u = es[0].astype(F32), es[1].astype(F32)
    sg = _sigmoid(g)
    return [d * u * sg * (1.0 + g * (1.0 - sg)), d * g * sg]


def _sum_epilogue(rs, es):
    return [rs[0] + rs[1]]


def _gates_epilogue(rs, es):
    ab, gv, gg = rs
    ga, gs = es
    return [_sigmoid(ga) * ab + _sigmoid(gs) * gv * _sigmoid(gg), ab, gv, gg]


def _gates_bwd_epilogue(rs, es):
    dm = rs[0]
    ga, gs, ab, gv, gg = (e.astype(F32) for e in es)
    sa, ss, sg = _sigmoid(ga), _sigmoid(gs), _sigmoid(gg)
    dsb = dm * ss
    return [dm * ab * sa * (1.0 - sa), dm * gv * sg * ss * (1.0 - ss), dm * sa, dsb * sg, dsb * gv * sg * (1.0 - sg)]


ATTN_ROWS = 2048


def _dilate_qkv(z, g, d):
    s = z.shape[0]
    tm = ATTN_ROWS
    per = tm // d
    nh = HEADS_PER_GROUP

    def body(z_ref, o_ref):
        for r in range(d):
            rows = z_ref[...] if d == 1 else z_ref[pl.ds(r, per, stride=d), :]
            o_ref[0, r] = rows.astype(BF16)

    return pl.pallas_call(
        body, name=f"dilate_qkv_{g}", grid=(s // tm, 3, nh),
        in_specs=[pl.BlockSpec((tm, HEAD_DIM), lambda i, w, h: (i, (3 * w + g) * nh + h))],
        out_specs=pl.BlockSpec((1, d, per, HEAD_DIM), lambda i, w, h: (w, 0, i, h)),
        out_shape=jax.ShapeDtypeStruct((3, d, s // d, GROUP_W), BF16),
        compiler_params=_params(("parallel", "parallel", "parallel")))(z)


def _undilate_dqkv(dqkv, dz, g, d):
    s = dz.shape[0]
    tm = ATTN_ROWS
    per = tm // d
    nh = HEADS_PER_GROUP

    def body(i_ref, dz_ref, o_ref, nat_ref):
        del dz_ref
        if d == 1:
            o_ref[...] = i_ref[0, 0]
        else:
            for r in range(d):
                nat_ref[pl.ds(r, per, stride=d), :] = i_ref[0, r].astype(F32)
            o_ref[...] = nat_ref[...].astype(BF16)

    return pl.pallas_call(
        body, name=f"undilate_dqkv_{g}", grid=(s // tm, 3, nh),
        in_specs=[pl.BlockSpec((1, d, per, HEAD_DIM), lambda i, w, h: (w, 0, i, h)),
                  pl.BlockSpec(memory_space=pl.ANY)],
        out_specs=pl.BlockSpec((tm, HEAD_DIM), lambda i, w, h: (i, (3 * w + g) * nh + h)),
        out_shape=jax.ShapeDtypeStruct(dz.shape, dz.dtype), input_output_aliases={1: 0},
        scratch_shapes=[pltpu.VMEM((tm, HEAD_DIM), F32)],
        compiler_params=_params(("parallel", "parallel", "parallel")))(dqkv, dz)


def _alibi_slope(head):
    return 2.0 ** (-8.0 * (head + 1) / N_ATTN_HEADS)


def _dot_nt(a, b):
    return lax.dot_general(a, b, _DOT_DIMS["nt"], preferred_element_type=F32)


def _dot_tn(a, b):
    return lax.dot_general(a, b, _DOT_DIMS["tn"], preferred_element_type=F32)


def _dot(a, b):
    return jnp.dot(a, b, preferred_element_type=F32)


GROUP_ROWS = HEADS_PER_GROUP * ATTN_BLK


def _band_bias(g, d, pairs):
    qi = jnp.arange(ATTN_BLK)[:, None]
    ki = jnp.arange(ATTN_BLK)[None, :]
    rows = []
    for hh in range(HEADS_PER_GROUP):
        slope_d = _alibi_slope(g * HEADS_PER_GROUP + hh) * d
        tiles = []
        for kind in pairs:
            dist = qi - ki if kind == "cur" else ATTN_BLK + qi - ki
            ok = dist >= 0 if kind == "cur" else dist <= ATTN_BLK
            tiles.append(jnp.where(ok, -slope_d * dist.astype(F32), NEG_BIG))
        rows.append(jnp.concatenate(tiles, axis=1))
    return jnp.concatenate(rows, axis=0).astype(F32)


def _tile_cols(t):
    return slice(t * ATTN_BLK, (t + 1) * ATTN_BLK)


def _attn_fwd(qkv, g, d):
    _, _, L, _ = qkv.shape
    nb = L // ATTN_BLK
    scale = HEAD_DIM ** -0.5

    def body(q_ref, kc_ref, kp_ref, vc_ref, vp_ref, bias_ref, o_ref, lse_ref, s_ref, p_ref):
        n = pl.program_id(1)
        for hh in range(HEADS_PER_GROUP):
            cols, rows = _tile_cols(hh), _tile_cols(hh)
            q = q_ref[0, 0, :, cols]
            s_ref[rows, _tile_cols(0)] = _dot_nt(q, kp_ref[0, 0, :, cols])
            s_ref[rows, _tile_cols(1)] = _dot_nt(q, kc_ref[0, 0, :, cols])
        col = lax.broadcasted_iota(jnp.int32, (GROUP_ROWS, 2 * ATTN_BLK), 1)
        s = s_ref[...] * scale + bias_ref[...]
        s = jnp.where(jnp.logical_and(col < ATTN_BLK, n == 0), NEG_BIG, s)
        m = jnp.max(s, axis=-1, keepdims=True)
        e = jnp.exp(s - m)
        l = jnp.sum(e, axis=-1, keepdims=True)
        p_ref[...] = (e * (1.0 / l)).astype(BF16)
        lse = m + jnp.log(l)
        for hh in range(HEADS_PER_GROUP):
            cols, rows = _tile_cols(hh), _tile_cols(hh)
            o_ref[0, :, cols] = (_dot(p_ref[rows, _tile_cols(0)], vp_ref[0, 0, :, cols])
                                 + _dot(p_ref[rows, _tile_cols(1)], vc_ref[0, 0, :, cols]))
            lse_ref[0, :, cols] = jnp.broadcast_to(lse[rows], (ATTN_BLK, HEAD_DIM))

    def spec(w, shift):
        return pl.BlockSpec((1, 1, ATTN_BLK, GROUP_W), lambda r, n: (w, r, jnp.maximum(n + shift, 0), 0))

    out = pl.BlockSpec((1, ATTN_BLK, GROUP_W), lambda r, n: (r, n, 0))
    bias = _band_bias(g, d, ("prev", "cur"))
    return pl.pallas_call(
        body, name=f"attn_fwd_{g}", grid=(d, nb),
        in_specs=[spec(0, 0), spec(1, 0), spec(1, -1), spec(2, 0), spec(2, -1),
                  pl.BlockSpec(bias.shape, lambda r, n: (0, 0))],
        out_specs=[out, out], out_shape=[jax.ShapeDtypeStruct((d, L, GROUP_W), F32)] * 2,
        scratch_shapes=[pltpu.VMEM((GROUP_ROWS, 2 * ATTN_BLK), F32), pltpu.VMEM((GROUP_ROWS, 2 * ATTN_BLK), BF16)],
        compiler_params=_params(("parallel", "parallel")))(qkv, qkv, qkv, qkv, qkv, bias)


def _attn_bwd(qkv, do, lse, cc, g, d, side=None):
    _, _, L, _ = qkv.shape
    nb = L // ATTN_BLK
    scale = HEAD_DIM ** -0.5
    a_, b_, c_ = _tile_cols(0), _tile_cols(1), _tile_cols(2)

    def body(q0_ref, q1_ref, k0_ref, kp_ref, v0_ref, vp_ref, do0_ref, do1_ref, l0_ref, l1_ref, c0_ref, c1_ref,
             bias_ref, o_ref, s_ref, dp_ref, l_ref, c_ref, p_ref, ds_ref):
        n = pl.program_id(1)
        for hh in range(HEADS_PER_GROUP):
            cols, rows = _tile_cols(hh), _tile_cols(hh)
            q0, q1 = q0_ref[0, 0, :, cols], q1_ref[0, 0, :, cols]
            k0, kp = k0_ref[0, 0, :, cols], kp_ref[0, 0, :, cols]
            v0, vp = v0_ref[0, 0, :, cols], vp_ref[0, 0, :, cols]
            do0, do1 = do0_ref[0, :, cols], do1_ref[0, :, cols]
            s_ref[rows, a_], s_ref[rows, b_], s_ref[rows, c_] = _dot_nt(q0, k0), _dot_nt(q0, kp), _dot_nt(q1, k0)
            dp_ref[rows, a_], dp_ref[rows, b_], dp_ref[rows, c_] = _dot_nt(do0, v0), _dot_nt(do0, vp), _dot_nt(do1, v0)
            l_ref[rows, a_], l_ref[rows, b_], l_ref[rows, c_] = l0_ref[0, :, cols], l0_ref[0, :, cols], l1_ref[0, :, cols]
            c_ref[rows, a_], c_ref[rows, b_], c_ref[rows, c_] = c0_ref[0, :, cols], c0_ref[0, :, cols], c1_ref[0, :, cols]
        col = lax.broadcasted_iota(jnp.int32, (GROUP_ROWS, 3 * ATTN_BLK), 1)
        tile = col // ATTN_BLK
        gone = jnp.logical_or(jnp.logical_and(tile == 1, n == 0), jnp.logical_and(tile == 2, n == nb - 1))
        s = jnp.where(gone, NEG_BIG, s_ref[...] * scale + bias_ref[...])
        p = jnp.exp(s - l_ref[...])
        p_ref[...] = p.astype(BF16)
        ds_ref[...] = (p * (dp_ref[...] + c_ref[...])).astype(BF16)
        for hh in range(HEADS_PER_GROUP):
            cols, rows = _tile_cols(hh), _tile_cols(hh)
            q0, q1 = q0_ref[0, 0, :, cols], q1_ref[0, 0, :, cols]
            k0, kp = k0_ref[0, 0, :, cols], kp_ref[0, 0, :, cols]
            do0, do1 = do0_ref[0, :, cols], do1_ref[0, :, cols]
            o_ref[0, 0, :, cols] = ((_dot(ds_ref[rows, a_], k0) + _dot(ds_ref[rows, b_], kp)) * scale).astype(BF16)
            o_ref[1, 0, :, cols] = ((_dot_tn(ds_ref[rows, a_], q0) + _dot_tn(ds_ref[rows, c_], q1)) * scale).astype(BF16)
            o_ref[2, 0, :, cols] = (_dot_tn(p_ref[rows, a_], do0) + _dot_tn(p_ref[rows, c_], do1)).astype(BF16)

    def spec(w, shift):
        return pl.BlockSpec((1, 1, ATTN_BLK, GROUP_W), lambda r, n: (w, r, jnp.clip(n + shift, 0, nb - 1), 0))

    def spec3(shift):
        return pl.BlockSpec((1, ATTN_BLK, GROUP_W), lambda r, n: (r, jnp.clip(n + shift, 0, nb - 1), 0))

    bias = _band_bias(g, d, ("cur", "prev", "prev"))
    wide = (GROUP_ROWS, 3 * ATTN_BLK)
    outs, carried = _call(
        body, name=f"attn_bwd_{g}", grid=(d, nb),
        in_specs=[spec(0, 0), spec(0, 1), spec(1, 0), spec(1, -1), spec(2, 0), spec(2, -1),
                  spec3(0), spec3(1), spec3(0), spec3(1), spec3(0), spec3(1), pl.BlockSpec(wide, lambda r, n: (0, 0))],
        out_specs=[pl.BlockSpec((3, 1, ATTN_BLK, GROUP_W), lambda r, n: (0, r, n, 0))],
        out_shape=[jax.ShapeDtypeStruct((3, d, L, GROUP_W), BF16)],
        scratch_shapes=[pltpu.VMEM(wide, F32)] * 4 + [pltpu.VMEM(wide, BF16)] * 2, semantics=("parallel", "parallel"),
        args=[qkv, qkv, qkv, qkv, qkv, qkv, do, do, lse, lse, cc, cc, bias], side=side)
    return outs[0] if side is None else (outs[0], carried)


def _load_natural(refs, nat_refs):
    for g, d in enumerate(ATTN_DILATIONS):
        if d == 1:
            nat_refs[g][...] = refs[g][0]
        else:
            per = ATTN_ROWS // d
            for r in range(d):
                nat_refs[g][pl.ds(r, per, stride=d), :] = refs[g][r]


def _mix_weights(lse_nat):
    l0, l1, l2 = lse_nat[0][...], lse_nat[1][...], lse_nat[2][...]
    m = jnp.maximum(jnp.maximum(l0, l1), l2)
    e0, e1, e2 = jnp.exp(l0 - m), jnp.exp(l1 - m), jnp.exp(l2 - m)
    inv = 1.0 / (e0 + e1 + e2)
    return e0 * inv, e1 * inv, e2 * inv


def _dilated_specs(s):
    return [pl.BlockSpec((d, ATTN_ROWS // d, HEAD_DIM), lambda i, h: (0, i, h)) for d in ATTN_DILATIONS]


NATURAL_SCRATCH = [pltpu.VMEM((ATTN_ROWS, HEAD_DIM), F32)] * (2 * len(ATTN_DILATIONS))


def _attn_merge(outs, lses):
    s = outs[0].shape[0] * outs[0].shape[1]

    def body(o0, o1, o2, l0, l1, l2, a_ref, *nat):
        onat, lnat = nat[:3], nat[3:]
        _load_natural((o0, o1, o2), onat)
        _load_natural((l0, l1, l2), lnat)
        w0, w1, w2 = _mix_weights(lnat)
        a_ref[...] = (w0 * onat[0][...] + w1 * onat[1][...] + w2 * onat[2][...]).astype(BF16)

    return pl.pallas_call(
        body, name="attn_merge", grid=(s // ATTN_ROWS, HEADS_PER_GROUP), in_specs=_dilated_specs(s) * 2,
        out_specs=pl.BlockSpec((ATTN_ROWS, HEAD_DIM), lambda i, h: (i, h)),
        out_shape=jax.ShapeDtypeStruct((s, GROUP_W), BF16), scratch_shapes=NATURAL_SCRATCH,
        compiler_params=_params(("parallel", "parallel")))(*outs, *lses)


def _attn_merge_bwd(outs, lses, dattn):
    s = dattn.shape[0]

    def body(o0, o1, o2, l0, l1, l2, da_ref, do0, do1, do2, c0, c1, c2, *nat):
        onat, lnat = nat[:3], nat[3:]
        _load_natural((o0, o1, o2), onat)
        _load_natural((l0, l1, l2), lnat)
        ws = _mix_weights(lnat)
        da = da_ref[...]
        attn = ws[0] * onat[0][...] + ws[1] * onat[1][...] + ws[2] * onat[2][...]
        tot = jnp.broadcast_to(jnp.sum(da * attn, axis=-1, keepdims=True), (ATTN_ROWS, HEAD_DIM))
        for g, (d, do_ref, c_ref) in enumerate(zip(ATTN_DILATIONS, (do0, do1, do2), (c0, c1, c2))):
            if d == 1:
                do_ref[0] = (ws[g] * da).astype(BF16)
                c_ref[0] = -ws[g] * tot
            else:
                onat[g][...] = ws[g] * da
                lnat[g][...] = -ws[g] * tot
                per = ATTN_ROWS // d
                for r in range(d):
                    do_ref[r] = onat[g][pl.ds(r, per, stride=d), :].astype(BF16)
                    c_ref[r] = lnat[g][pl.ds(r, per, stride=d), :]

    dil = _dilated_specs(s)
    shapes = [jax.ShapeDtypeStruct(o.shape, BF16) for o in outs] + [jax.ShapeDtypeStruct(o.shape, F32) for o in outs]
    return pl.pallas_call(
        body, name="attn_merge_bwd", grid=(s // ATTN_ROWS, HEADS_PER_GROUP),
        in_specs=dil * 2 + [pl.BlockSpec((ATTN_ROWS, HEAD_DIM), lambda i, h: (i, h))], out_specs=dil * 2,
        out_shape=shapes, scratch_shapes=NATURAL_SCRATCH,
        compiler_params=_params(("parallel", "parallel")))(*outs, *lses, dattn)


def _ssm_prepare(a_re, a_im, log_dt, b_re, b_im, c_re, c_im):
    n_g = a_re.shape[0]
    nj = n_g * SSM_GROUP // SSM_TILE_CH
    gpt = SSM_TILE_CH // SSM_GROUP
    dt = jnp.exp(log_dt)[:, None]
    mag = jnp.exp(a_re * dt)
    lr, li = mag * jnp.cos(a_im * dt), mag * jnp.sin(a_im * dt)
    den = a_re * a_re + a_im * a_im
    cr = ((lr - 1.0) * a_re + li * a_im) / den
    ci = (li * a_re - (lr - 1.0) * a_im) / den
    bb_re = cr[..., None] * b_re - ci[..., None] * b_im
    bb_im = cr[..., None] * b_im + ci[..., None] * b_re
    eye = jnp.eye(gpt, dtype=F32)

    def b_tiles(t):
        t = t.transpose(0, 2, 1).reshape(nj, gpt, SSM_GROUP, SSM_STATE)
        return jnp.einsum("jgcp,gh->jgchp", t, eye).reshape(nj, SSM_TILE_CH, SSM_TILE_ST)

    def c_tiles(t):
        t = t.reshape(nj, gpt, SSM_GROUP, SSM_STATE)
        return jnp.einsum("jgcp,gh->jhpgc", t, eye).reshape(nj, SSM_TILE_ST, SSM_TILE_CH)

    lam = jnp.stack([lr.reshape(-1), li.reshape(-1)])
    bmat = jnp.concatenate([b_tiles(bb_re), b_tiles(bb_im)], axis=2)
    cmat = jnp.concatenate([c_tiles(c_re), -c_tiles(c_im)], axis=1)
    return lam, bmat, cmat


SSM_SEGMENTS = 8


def _to_segment_order(nat, perm_ref):
    per = nat.shape[0] // SSM_SEGMENTS
    for i in range(SSM_SEGMENTS):
        perm_ref[pl.ds(i, per, stride=SSM_SEGMENTS), :] = nat[i * per:(i + 1) * per, :]
    return perm_ref[...]


def _to_time_order(val, perm_ref, store):
    per = val.shape[0] // SSM_SEGMENTS
    perm_ref[...] = val
    for i in range(SSM_SEGMENTS):
        store(i, perm_ref[pl.ds(i, per, stride=SSM_SEGMENTS), :])


def _fill_powers(lam_ref, w_ref, nj, tau_n):
    for j in range(nj):
        st = slice(j * SSM_TILE_ST, (j + 1) * SSM_TILE_ST)
        lr = jnp.broadcast_to(lam_ref[0:1, st], (SSM_SEGMENTS, SSM_TILE_ST))
        li = jnp.broadcast_to(lam_ref[1:2, st], (SSM_SEGMENTS, SSM_TILE_ST))
        wr, wi = lr, li
        for tau in range(tau_n):
            rows = slice(tau * SSM_SEGMENTS, (tau + 1) * SSM_SEGMENTS)
            w_ref[j, rows, :SSM_TILE_ST] = wr
            w_ref[j, rows, SSM_TILE_ST:] = wi
            wr, wi = wr * lr - wi * li, wr * li + wi * lr


def _segment_scan(src, xs_ref, w_tile, lr, li, cr, ci, conj, reverse):
    seg, half = SSM_SEGMENTS, SSM_TILE_ST
    tau_n = src.shape[0] // seg
    sgn = -1.0 if conj else 1.0
    lr8 = jnp.broadcast_to(lr, (seg, half))
    li8 = jnp.broadcast_to(li, (seg, half)) * sgn
    xr = jnp.zeros((seg, half), F32)
    xi = jnp.zeros((seg, half), F32)
    order = range(tau_n - 1, -1, -1) if reverse else range(tau_n)
    for tau in order:
        rows = slice(tau * seg, (tau + 1) * seg)
        xr, xi = lr8 * xr - li8 * xi + src[rows, :half], lr8 * xi + li8 * xr + src[rows, half:]
        xs_ref[rows, :half] = xr
        xs_ref[rows, half:] = xi
    pr = w_tile[(tau_n - 1) * seg:(tau_n - 1) * seg + 1, :half]
    pi = w_tile[(tau_n - 1) * seg:(tau_n - 1) * seg + 1, half:] * sgn
    fr, fi = cr, ci
    ins_r, ins_i = [None] * seg, [None] * seg
    runs = range(seg - 1, -1, -1) if reverse else range(seg)
    for i in runs:
        ins_r[i], ins_i[i] = fr, fi
        fr, fi = xr[i:i + 1, :] + pr * fr - pi * fi, xi[i:i + 1, :] + pr * fi + pi * fr
    in_r = jnp.concatenate(ins_r, axis=0)
    in_i = jnp.concatenate(ins_i, axis=0)
    for tau in range(tau_n):
        rows = slice(tau * seg, (tau + 1) * seg)
        wrow = (tau_n - 1 - tau) if reverse else tau
        wr = w_tile[wrow * seg:(wrow + 1) * seg, :half]
        wi = w_tile[wrow * seg:(wrow + 1) * seg, half:] * sgn
        xs_ref[rows, :half] += wr * in_r - wi * in_i
        xs_ref[rows, half:] += wr * in_i + wi * in_r
    return (fr, fi), (in_r, in_i)


def _ssm_dims(z, bmat, u_off):
    s = z.shape[0]
    nj = bmat.shape[0]
    t_rows = _pick(s, (256, 128))
    return s, nj, nj * SSM_TILE_CH, nj * SSM_TILE_ST, t_rows


def _ssm_fwd(z, bmat, cmat, lam, dskip, u_off, side=None):
    s, nj, w, ns, t_rows = _ssm_dims(z, bmat, u_off)
    per = t_rows // SSM_SEGMENTS

    def body(*refs):
        u_refs = refs[:nj]
        b_ref, c_ref, lam_ref, d_ref, y_ref, yg_ref, xin_ref, carry_ref, w_ref, xs_ref, perm_ref = refs[nj:]

        @pl.when(pl.program_id(0) == 0)
        def _():
            carry_ref[...] = jnp.zeros_like(carry_ref)
            _fill_powers(lam_ref, w_ref, nj, per)

        xin_ref[0] = carry_ref[...]
        for j in range(nj):
            st = slice(j * SSM_TILE_ST, (j + 1) * SSM_TILE_ST)
            ch = slice(j * SSM_TILE_CH, (j + 1) * SSM_TILE_CH)
            up = _to_segment_order(u_refs[j], perm_ref)
            bu = _dot(up.astype(BF16), b_ref[j])
            (fr, fi), _ = _segment_scan(bu, xs_ref, w_ref.at[j], lam_ref[0:1, st], lam_ref[1:2, st],
                                        carry_ref[0:1, st], carry_ref[1:2, st], conj=False, reverse=False)
            carry_ref[0:1, st] = fr
            carry_ref[1:2, st] = fi
            yp = _dot(xs_ref[...].astype(BF16), c_ref[j]) + d_ref[:, ch] * up

            def store(i, rows, ch=ch):
                y_ref[i * per:(i + 1) * per, ch] = rows
                yg_ref[i * per:(i + 1) * per, ch] = _gelu(rows).astype(BF16)

            _to_time_order(yp, perm_ref, store)

    u_specs = [pl.BlockSpec((t_rows, SSM_TILE_CH), lambda c, k=k: (c, u_off // SSM_TILE_CH + k)) for k in range(nj)]
    full3 = lambda shape: pl.BlockSpec(shape, lambda c: (0, 0, 0))
    full2 = lambda shape: pl.BlockSpec(shape, lambda c: (0, 0))
    rows = pl.BlockSpec((t_rows, w), lambda c: (c, 0))
    outs, carried = _call(
        body, name="ssm_fwd", grid=(s // t_rows,),
        in_specs=u_specs + [full3(bmat.shape), full3(cmat.shape), full2(lam.shape), full2(dskip.shape)],
        out_specs=[rows, rows, pl.BlockSpec((1, 2, ns), lambda c: (c, 0, 0))],
        out_shape=[jax.ShapeDtypeStruct((s, w), F32), jax.ShapeDtypeStruct((s, w), BF16),
                   jax.ShapeDtypeStruct((s // t_rows, 2, ns), F32)],
        scratch_shapes=[pltpu.VMEM((2, ns), F32), pltpu.VMEM((nj, t_rows, 2 * SSM_TILE_ST), F32),
                        pltpu.VMEM((t_rows, 2 * SSM_TILE_ST), F32), pltpu.VMEM((t_rows, SSM_TILE_CH), F32)],
        semantics=("arbitrary",), args=[*([z] * nj), bmat, cmat, lam, dskip], side=side)
    return outs if side is None else (outs, carried)


def _ssm_bwd(z, y, dyg, xin, bmat, cmat, lam, dskip, u_off, side=None):
    s, nj, w, ns, t_rows = _ssm_dims(z, bmat, u_off)
    nc = s // t_rows
    per = t_rows // SSM_SEGMENTS
    seg, half = SSM_SEGMENTS, SSM_TILE_ST

    def body(*refs):
        u_refs = refs[:nj]
        (y_ref, dyg_ref, xin_ref, b_ref, c_ref, lam_ref, d_ref, du_ref, db_ref, dc_ref, dlam_ref, dd_ref,
         carry_ref, w_ref, xs_ref, gs_ref, perm_ref, acc_ref) = refs[nj:]

        @pl.when(pl.program_id(0) == 0)
        def _():
            carry_ref[...] = jnp.zeros_like(carry_ref)
            db_ref[...] = jnp.zeros_like(db_ref)
            dc_ref[...] = jnp.zeros_like(dc_ref)
            dd_ref[...] = jnp.zeros_like(dd_ref)
            acc_ref[...] = jnp.zeros_like(acc_ref)
            _fill_powers(lam_ref, w_ref, nj, per)

        for j in range(nj):
            st = slice(j * SSM_TILE_ST, (j + 1) * SSM_TILE_ST)
            ch = slice(j * SSM_TILE_CH, (j + 1) * SSM_TILE_CH)
            lr, li = lam_ref[0:1, st], lam_ref[1:2, st]
            up = _to_segment_order(u_refs[j], perm_ref)
            upb = up.astype(BF16)
            dyp = _to_segment_order(dyg_ref[:, ch] * _gelu_grad(y_ref[:, ch]), perm_ref)
            dyb = dyp.astype(BF16)
            _, (in_r, in_i) = _segment_scan(_dot(upb, b_ref[j]), xs_ref, w_ref.at[j], lr, li,
                                            xin_ref[0, 0:1, st], xin_ref[0, 1:2, st], conj=False, reverse=False)
            (gr, gi), _ = _segment_scan(_dot_nt(dyb, c_ref[j]), gs_ref, w_ref.at[j], lr, li,
                                        carry_ref[0:1, st], carry_ref[1:2, st], conj=True, reverse=True)
            carry_ref[0:1, st] = gr
            carry_ref[1:2, st] = gi
            xs, gs = xs_ref[...], gs_ref[...]
            xsr, xsi, gsr, gsi = xs[:, :half], xs[:, half:], gs[:, :half], gs[:, half:]
            pxr = jnp.concatenate([in_r, xsr[:t_rows - seg]], axis=0)
            pxi = jnp.concatenate([in_i, xsi[:t_rows - seg]], axis=0)
            dl_r = gsr * pxr + gsi * pxi
            dl_i = gsi * pxr - gsr * pxi
            acc_ref[0, :, st] += jnp.sum(dl_r.reshape(per, seg, half), axis=0)
            acc_ref[1, :, st] += jnp.sum(dl_i.reshape(per, seg, half), axis=0)
            gx = gs.astype(BF16)
            dup = _dot_nt(gx, b_ref[j]) + d_ref[:, ch] * dyp

            def store(i, rows, ch=ch):
                du_ref[i * per:(i + 1) * per, ch] = rows.astype(BF16)

            _to_time_order(dup, perm_ref, store)
            db_ref[j] += _dot_tn(upb, gx)
            dc_ref[j] += _dot_tn(xs.astype(BF16), dyb)
            dd_ref[:, ch] += jnp.sum(dyp * up, axis=0, keepdims=True)

        @pl.when(pl.program_id(0) == nc - 1)
        def _():
            dlam_ref[...] = jnp.sum(acc_ref[...], axis=1)

    rev = lambda c: nc - 1 - c
    u_specs = [pl.BlockSpec((t_rows, SSM_TILE_CH), lambda c, k=k: (rev(c), u_off // SSM_TILE_CH + k))
               for k in range(nj)]
    full3 = lambda shape: pl.BlockSpec(shape, lambda c: (0, 0, 0))
    full2 = lambda shape: pl.BlockSpec(shape, lambda c: (0, 0))
    rows = pl.BlockSpec((t_rows, w), lambda c: (rev(c), 0))
    outs, carried = _call(
        body, name="ssm_bwd", grid=(nc,),
        in_specs=u_specs + [rows, rows, pl.BlockSpec((1, 2, ns), lambda c: (rev(c), 0, 0)),
                            full3(bmat.shape), full3(cmat.shape), full2(lam.shape), full2(dskip.shape)],
        out_specs=[rows, full3(bmat.shape), full3(cmat.shape), full2(lam.shape), full2(dskip.shape)],
        out_shape=[jax.ShapeDtypeStruct((s, w), BF16), jax.ShapeDtypeStruct(bmat.shape, F32),
                   jax.ShapeDtypeStruct(cmat.shape, F32), jax.ShapeDtypeStruct(lam.shape, F32),
                   jax.ShapeDtypeStruct(dskip.shape, F32)],
        scratch_shapes=[pltpu.VMEM((2, ns), F32), pltpu.VMEM((nj, t_rows, 2 * SSM_TILE_ST), F32),
                        pltpu.VMEM((t_rows, 2 * SSM_TILE_ST), F32), pltpu.VMEM((t_rows, 2 * SSM_TILE_ST), F32),
                        pltpu.VMEM((t_rows, SSM_TILE_CH), F32), pltpu.VMEM((2, SSM_SEGMENTS, ns), F32)],
        semantics=("arbitrary",), args=[*([z] * nj), y, dyg, xin, bmat, cmat, lam, dskip], side=side)
    return outs if side is None else (outs, carried)


def _adam_math(w, g, m, v):
    m = ADAM_B1 * m + (1.0 - ADAM_B1) * g
    v = ADAM_B2 * v + (1.0 - ADAM_B2) * (g * g)
    m_hat = m / (1.0 - ADAM_B1 ** ADAM_STEP)
    v_hat = v / (1.0 - ADAM_B2 ** ADAM_STEP)
    delta = -ADAM_LR * (m_hat / (jnp.sqrt(v_hat) + ADAM_EPS) + ADAM_WD * w)
    return delta, m, v


def _adam_rows(r, c):
    for tr in (512, 256, 128, 64, 32, 16, 8):
        if r % tr == 0 and tr * c * 4 <= (1 << 20):
            return tr
    return r


def _adamw_big(w, p_mine, p_sib, m, v, name):
    r, c = w.shape
    tr = _adam_rows(r, c)

    def body(w_ref, a_ref, b_ref, m_ref, v_ref, g_ref, d_ref, nm_ref, nv_ref):
        g = a_ref[...] + b_ref[...]
        g_ref[...] = g
        d_ref[...], nm_ref[...], nv_ref[...] = _adam_math(w_ref[...], g, m_ref[...], v_ref[...])

    blk = pl.BlockSpec((tr, c), lambda i: (i, 0))
    return pl.pallas_call(body, name=f"adamw_{name}", grid=(r // tr,), in_specs=[blk] * 5, out_specs=[blk] * 4,
                          out_shape=[jax.ShapeDtypeStruct((r, c), F32)] * 4,
                          compiler_params=_params(("parallel",)))(w, p_mine, p_sib, m, v)


def _adamw_small(w, parts, m, v):
    r, c = w.shape
    n_dev = parts.shape[0]

    def body(w_ref, p_ref, m_ref, v_ref, g_ref, d_ref, nm_ref, nv_ref):
        g = p_ref[0]
        for k in range(1, n_dev):
            g = g + p_ref[k]
        g_ref[...] = g
        d_ref[...], nm_ref[...], nv_ref[...] = _adam_math(w_ref[...], g, m_ref[...], v_ref[...])

    blk = pl.BlockSpec((r, c), lambda i: (0, 0))
    return pl.pallas_call(body, name="adamw_small", grid=(1,),
                          in_specs=[blk, pl.BlockSpec((n_dev, r, c), lambda i: (0, 0, 0)), blk, blk],
                          out_specs=[blk] * 4, out_shape=[jax.ShapeDtypeStruct((r, c), F32)] * 4,
                          compiler_params=_params(("arbitrary",)))(w, parts, m, v)


def _cast_bf16(w, name):
    r, c = w.shape
    tr = _adam_rows(r, c)

    def body(w_ref, o_ref):
        o_ref[...] = w_ref[...].astype(BF16)

    blk = pl.BlockSpec((tr, c), lambda i: (i, 0))
    return pl.pallas_call(body, name=f"cast_{name}", grid=(r // tr,), in_specs=[blk], out_specs=blk,
                          out_shape=jax.ShapeDtypeStruct((r, c), BF16), compiler_params=_params(("parallel",)))(w)


def _sum_slots(recv, name):
    _, r, c = recv.shape
    tr = _adam_rows(r, c)

    def body(p_ref, o_ref):
        acc = p_ref[0].astype(F32)
        for k in range(1, N_CHIPS):
            acc = acc + p_ref[k].astype(F32)
        o_ref[...] = acc

    return pl.pallas_call(body, name=f"sum_{name}", grid=(r // tr,),
                          in_specs=[pl.BlockSpec((N_CHIPS, tr, c), lambda i: (0, i, 0))],
                          out_specs=pl.BlockSpec((tr, c), lambda i: (i, 0)),
                          out_shape=jax.ShapeDtypeStruct((r, c), F32), compiler_params=_params(("parallel",)))(recv)


BIG_WEIGHTS = ("w_in", "w_attn_up", "w_glu_v", "w_glu_g", "w_out", "w_ffn_gate", "w_ffn_up", "w_ffn_down")
COL_SHARDED = ("w_in", "w_attn_up", "w_glu_v", "w_glu_g", "w_ffn_gate", "w_ffn_up")


def _aligned(v, m):
    return v if isinstance(v, int) else pl.multiple_of(v, m)


def _shard_of(ref, name, j, shard_shape, half=None):
    r, c = shard_shape
    rows = r if half is None else r // 2
    row0 = 0 if half is None else half * rows
    if name in COL_SHARDED:
        return ref.at[pl.ds(_aligned(row0, 16), rows), pl.ds(_aligned(j * c, 128), c)]
    return ref.at[pl.ds(_aligned(j * r + row0, 16), rows), :]


def _other_chips():
    x, y = lax.axis_index("x"), lax.axis_index("y")
    return [(1 - x, y), (x, 1 - y), (1 - x, 1 - y)]


def _dma_sems(n, arrays):
    return [pltpu.SemaphoreType.DMA((n, 3))] * arrays + [pltpu.SemaphoreType.DMA((n,))]


def _gather_side(shards):
    names = list(shards)
    n = len(names)
    full_shapes = []
    for k in names:
        r, c = shards[k].shape
        full_shapes.append((r, c * N_CHIPS) if k in COL_SHARDED else (r * N_CHIPS, c))

    def build(src, dst, sems):
        send_sems, recv_sems, pass_send_sems, pass_recv_sems, local_sems = sems
        x, y, c = lax.axis_index("x"), lax.axis_index("y"), lax.axis_index("c")
        me = 2 * x + y
        locals_, sends, arrivals, forwards, passed_on = [], [], [], [], []
        for i, k in enumerate(names):
            shape = shards[k].shape
            half_rows = shape[0] // 2
            locals_.append(pltpu.make_async_copy(src[i], _shard_of(dst[i], k, me, shape), local_sems.at[i]))
            my_half = src[i].at[pl.ds(_aligned(c * half_rows, 16), half_rows), :]
            for p, (px, py) in enumerate(_other_chips()):
                peer = 2 * px + py
                landed = _shard_of(dst[i], k, peer, shape, half=c)
                sends.append(pltpu.make_async_remote_copy(
                    src_ref=my_half, dst_ref=_shard_of(dst[i], k, me, shape, half=c), send_sem=send_sems.at[i, p],
                    recv_sem=recv_sems.at[i, p], device_id=(px, py, c), device_id_type=MESH))
                arrivals.append(pltpu.make_async_remote_copy(
                    src_ref=my_half, dst_ref=landed, send_sem=send_sems.at[i, p],
                    recv_sem=recv_sems.at[i, p], device_id=(px, py, c), device_id_type=MESH))
                forwards.append(pltpu.make_async_remote_copy(
                    src_ref=landed, dst_ref=landed, send_sem=pass_send_sems.at[i, p],
                    recv_sem=pass_recv_sems.at[i, p], device_id=(x, y, 1 - c), device_id_type=MESH))
                passed_on.append(pltpu.make_async_remote_copy(
                    src_ref=landed, dst_ref=_shard_of(dst[i], k, peer, shape, half=1 - c),
                    send_sem=pass_send_sems.at[i, p], recv_sem=pass_recv_sems.at[i, p],
                    device_id=(x, y, 1 - c), device_id_type=MESH))
        return locals_, sends, arrivals, forwards, passed_on

    return _Side([shards[k] for k in names], [jax.ShapeDtypeStruct(s, BF16) for s in full_shapes], _dma_sems(n, 4), build)


def _scatter_side(grads, shard_shapes):
    names = list(grads)
    n = len(names)

    def build(src, dst, sems):
        send_sems, recv_sems, local_sems = sems
        x, y, c = lax.axis_index("x"), lax.axis_index("y"), lax.axis_index("c")
        me = 2 * x + y
        locals_, sends, arrivals = [], [], []
        for i, k in enumerate(names):
            shape = shard_shapes[k]
            locals_.append(pltpu.make_async_copy(_shard_of(src[i], k, me, shape), dst[i].at[me], local_sems.at[i]))
            for p, (px, py) in enumerate(_other_chips()):
                peer = 2 * px + py
                sends.append(pltpu.make_async_remote_copy(
                    src_ref=_shard_of(src[i], k, peer, shape), dst_ref=dst[i].at[me], send_sem=send_sems.at[i, p],
                    recv_sem=recv_sems.at[i, p], device_id=(px, py, c), device_id_type=MESH))
                arrivals.append(pltpu.make_async_remote_copy(
                    src_ref=_shard_of(src[i], k, peer, shape), dst_ref=dst[i].at[peer], send_sem=send_sems.at[i, p],
                    recv_sem=recv_sems.at[i, p], device_id=(px, py, c), device_id_type=MESH))
        return locals_, sends, arrivals, [None] * len(arrivals), []

    return _Side([grads[k] for k in names],
                 [jax.ShapeDtypeStruct((N_CHIPS,) + tuple(shard_shapes[k]), BF16) for k in names], _dma_sems(n, 2), build)


def _put_cols(dz, src, col_off):
    s, w = src.shape
    tr = _pick(s, (2048, 1024, 512, 256, 128, 64, 8))
    tc = _pick(math.gcd(w, col_off), (1024, 512, 256, 128))
    off = col_off // tc

    def body(src_ref, dz_ref, o_ref):
        del dz_ref
        o_ref[...] = src_ref[...].astype(o_ref.dtype)

    return pl.pallas_call(
        body, name="put_cols", grid=(s // tr, w // tc),
        in_specs=[pl.BlockSpec((tr, tc), lambda i, j: (i, j)), pl.BlockSpec(memory_space=pl.ANY)],
        out_specs=pl.BlockSpec((tr, tc), lambda i, j: (i, off + j)),
        out_shape=jax.ShapeDtypeStruct(dz.shape, dz.dtype), input_output_aliases={1: 0},
        compiler_params=_params(("parallel", "parallel")))(src, dz)


def _swap_side(parts):
    n = len(parts)

    def build(src, dst, sems):
        send_sems, recv_sems = sems
        sibling = (lax.axis_index("x"), lax.axis_index("y"), 1 - lax.axis_index("c"))
        copies = [pltpu.make_async_remote_copy(src_ref=src[i], dst_ref=dst[i], send_sem=send_sems.at[i],
                                               recv_sem=recv_sems.at[i], device_id=sibling, device_id_type=MESH)
                  for i in range(n)]
        return [], copies, copies, [None] * n, []

    return _Side(parts, [jax.ShapeDtypeStruct(p.shape, F32) for p in parts],
                 [pltpu.SemaphoreType.DMA((n,)), pltpu.SemaphoreType.DMA((n,))], build)


def _share_side(packed):
    r, c = packed.shape

    def build(src, dst, sems):
        send_sems, recv_sems, local_sem = sems
        x, y, cc = lax.axis_index("x"), lax.axis_index("y"), lax.axis_index("c")
        me = 4 * x + 2 * y + cc
        own = pltpu.make_async_copy(src[0], dst[0].at[me], local_sem)
        sends, arrivals = [], []
        flips = [(fx, fy, fc) for fx in range(2) for fy in range(2) for fc in range(2) if fx or fy or fc]
        for p, (fx, fy, fc) in enumerate(flips):
            px, py, pc = x ^ fx, y ^ fy, cc ^ fc
            sends.append(pltpu.make_async_remote_copy(
                src_ref=src[0], dst_ref=dst[0].at[me], send_sem=send_sems.at[p], recv_sem=recv_sems.at[p],
                device_id=(px, py, pc), device_id_type=MESH))
            arrivals.append(pltpu.make_async_remote_copy(
                src_ref=src[0], dst_ref=dst[0].at[4 * px + 2 * py + pc], send_sem=send_sems.at[p],
                recv_sem=recv_sems.at[p], device_id=(px, py, pc), device_id_type=MESH))
        return [own], sends, arrivals, [None] * len(arrivals), []

    return _Side([packed], [jax.ShapeDtypeStruct((8, r, c), F32)],
                 [pltpu.SemaphoreType.DMA((7,)), pltpu.SemaphoreType.DMA((7,)), pltpu.SemaphoreType.DMA], build)


SMALL_WEIGHTS = ("norm_mix_pre", "ssm_a_re", "ssm_a_im", "ssm_log_dt", "ssm_b_re", "ssm_b_im", "ssm_c_re", "ssm_c_im",
                 "ssm_d", "norm_mix_post", "norm_ffn_pre", "norm_ffn_post")
WEIGHT_ORDER = ("norm_mix_pre", "w_in", "w_attn_up", "ssm_a_re", "ssm_a_im", "ssm_log_dt", "ssm_b_re", "ssm_b_im",
                "ssm_c_re", "ssm_c_im", "ssm_d", "w_glu_v", "w_glu_g", "w_out", "norm_mix_post", "norm_ffn_pre",
                "w_ffn_gate", "w_ffn_up", "w_ffn_down", "norm_ffn_post")
PACK_LANES = 128
PACK_ROWS = 8
PACK_GROUPS = (SMALL_WEIGHTS[:1], SMALL_WEIGHTS[1:])


def _pack_group(arrs, names):
    flat = jnp.concatenate([arrs[k].reshape(-1) for k in names])
    pad = -flat.shape[0] % (PACK_LANES * PACK_ROWS)
    return jnp.pad(flat, (0, pad)).reshape(-1, PACK_LANES)


def _pack_small(arrs):
    return jnp.concatenate([_pack_group(arrs, names) for names in PACK_GROUPS], axis=0)


def _unpack_small(packed, like):
    out, row = {}, 0
    for names in PACK_GROUPS:
        rows = _pack_group(like, names).shape[0]
        flat, pos = packed[row:row + rows].reshape(-1), 0
        for k in names:
            n = like[k].size
            out[k] = flat[pos:pos + n].reshape(like[k].shape)
            pos += n
        row += rows
    return out


def _local_step(x, target, big, small, shards=None, shard_shapes=None, h1=None):
    s, d = x.shape
    big, grads, slots = dict(big), {}, {}
    carry = shards is not None

    def gathering(names, call):
        if not carry:
            return call(None)
        res, got = call(_gather_side({k: shards[k] for k in names}))
        big.update(zip(names, got))
        return res

    def scattering(names, call):
        if not carry:
            return call(None)
        res, got = call(_scatter_side({k: grads[k] for k in names}, shard_shapes))
        slots.update(zip(names, got))
        return res

    u_off = 3 * HQ
    gate_off = u_off + d // 2
    g1, g2, g3, g4 = (small[k][0:1] for k in ("norm_mix_pre", "norm_mix_post", "norm_ffn_pre", "norm_ffn_post"))
    ssm_names = ("ssm_a_re", "ssm_a_im", "ssm_log_dt", "ssm_b_re", "ssm_b_im", "ssm_c_re", "ssm_c_im")
    (lam, bmat, cmat), ssm_vjp = jax.vjp(_ssm_prepare, *[small[k][0] for k in ssm_names])
    bmat, cmat = bmat.astype(BF16), cmat.astype(BF16)
    dskip = small["ssm_d"][0:1]

    if h1 is None:
        h1 = _norm_in(x, g1)
    z = gathering(("w_attn_up", "w_glu_v", "w_glu_g", "w_out", "w_ffn_gate"),
                  lambda side: _mm(h1, big["w_in"], "nn", F32, "in_proj", side=side))
    y, yg, xin = gathering(("w_ffn_up",), lambda side: _ssm_fwd(z, bmat, cmat, lam, dskip, u_off, side=side))
    qkv = [_dilate_qkv(z, g, dil) for g, dil in enumerate(ATTN_DILATIONS)]
    outs, lses = zip(*[_attn_fwd(qkv[g], g, dil) for g, dil in enumerate(ATTN_DILATIONS)])
    attn = _attn_merge(outs, lses)
    merged, ab, gv, gg = _mm_fused(
        [attn, yg], [big["w_attn_up"], big["w_glu_v"], big["w_glu_g"]], [(0, 0), (1, 1), (1, 2)], "nn",
        [BF16, BF16, BF16, BF16], "branches_merge", extras=[(z, gate_off), (z, gate_off + d)], epilogue=_gates_epilogue)
    mo = _mm(merged, big["w_out"], "nn", F32, "mix_out")
    x2, h2 = _norm_mid(x, mo, g2, g3)
    act, fg, fu = gathering(("w_ffn_down",), lambda side: _mm_fused(
        [h2], [big["w_ffn_gate"], big["w_ffn_up"]], [(0, 0), (0, 1)], "nn", [BF16, BF16, BF16], "ffn_up_act",
        epilogue=_swiglu_epilogue, side=side))
    f = _mm(act, big["w_ffn_down"], "nn", F32, "ffn_down")
    loss, dout, df, dg4 = _loss_head(x2, f, g4, target)

    grads["w_ffn_down"] = _mm_kloop(act, df, "tn", BF16, "dw_ffn_down")
    dfg, dfu = scattering(("w_ffn_down",), lambda side: _mm_fused(
        [df], [big["w_ffn_down"]], [(0, 0)], "nt", [BF16, BF16], "d_ffn_act", extras=[(fg, 0), (fu, 0)],
        epilogue=_swiglu_bwd_epilogue, side=side))
    grads["w_ffn_gate"] = _mm_kloop(h2, dfg, "tn", BF16, "dw_ffn_gate")
    dh2 = scattering(("w_ffn_gate",), lambda side: _mm_fused(
        [dfg, dfu], [big["w_ffn_gate"], big["w_ffn_up"]], [(0, 0), (1, 1)], "nt", [F32], "d_h2",
        epilogue=_sum_epilogue, side=side))[0]
    grads["w_ffn_up"] = _mm_kloop(h2, dfu, "tn", BF16, "dw_ffn_up")
    dx2, dmo, dg2, dg3 = _norm_mid_bwd(x2, mo, g2, g3, dout, dh2)
    dz, dgs, dab, dgv, dgg = _mm_fused(
        [dmo], [big["w_out"]], [(0, 0)], "nt", [BF16] * 5, "d_merged_gates",
        extras=[(z, gate_off), (z, gate_off + d), (ab, 0), (gv, 0), (gg, 0)], epilogue=_gates_bwd_epilogue,
        out_place=[(z.shape[1], gate_off), None, None, None, None])
    dz = _put_cols(dz, dgs, gate_off + d)
    grads["w_out"] = _mm_kloop(merged, dmo, "tn", BF16, "dw_out")
    dyg = _mm_fused([dgv, dgg], [big["w_glu_v"], big["w_glu_g"]], [(0, 0), (1, 1)], "nt", [F32], "d_yg",
                    epilogue=_sum_epilogue)[0]
    grads["w_glu_v"] = _mm_kloop(yg, dgv, "tn", BF16, "dw_glu_v")
    grads["w_glu_g"] = _mm_kloop(yg, dgg, "tn", BF16, "dw_glu_g")
    du, dbmat, dcmat, dlam, dd = scattering(
        ("w_ffn_up", "w_out", "w_glu_v", "w_glu_g"),
        lambda side: _ssm_bwd(z, y, dyg, xin, bmat, cmat, lam, dskip, u_off, side=side))
    dz = _put_cols(dz, du, u_off)
    dattn = _mm(dab, big["w_attn_up"], "nt", F32, "d_attn")
    grads["w_attn_up"] = _mm_kloop(attn, dab, "tn", BF16, "dw_attn_up")
    merged_bwd = _attn_merge_bwd(outs, lses, dattn)
    mine, theirs = {}, {}
    for g, dil in enumerate(ATTN_DILATIONS):
        side = None
        if carry and g == 0:
            mine = {k: _sum_slots(slots[k], k) for k in slots}
            side = _swap_side(list(mine.values()))
        dqkv = _attn_bwd(qkv[g], merged_bwd[g], lses[g], merged_bwd[3 + g], g, dil, side=side)
        if side is not None:
            dqkv, got = dqkv
            theirs = dict(zip(mine, got))
        dz = _undilate_dqkv(dqkv, dz, g, dil)
    small_grads = dict(zip(ssm_names, (t[None] for t in ssm_vjp((dlam, dbmat, dcmat)))))
    small_grads.update(norm_mix_post=dg2, norm_ffn_pre=dg3, norm_ffn_post=dg4, ssm_d=dd)
    if carry:
        side = _join_sides(_scatter_side({"w_attn_up": grads["w_attn_up"]}, shard_shapes),
                           _share_side(_pack_group(small_grads, PACK_GROUPS[1])))
        grads["w_in"], (slots["w_attn_up"], shared) = _mm_kloop(h1, dz, "tn", BF16, "dw_in", side=side)
    else:
        grads["w_in"] = _mm_kloop(h1, dz, "tn", BF16, "dw_in")
    dh1 = scattering(("w_in",), lambda side: _mm_kloop(dz, big["w_in"], "nt", F32, "d_h1", side=side))
    grad_x, dg1 = _norm_in_bwd(x, g1, dh1, dx2)
    small_grads["norm_mix_pre"] = dg1
    if carry:
        return loss[0, 0], grad_x, (slots, mine, theirs), (dg1, shared)
    return loss[0, 0], grad_x, grads, small_grads


def kernel(x, norm_mix_pre, w_in, w_attn_up, ssm_a_re, ssm_a_im, ssm_log_dt, ssm_b_re, ssm_b_im, ssm_c_re, ssm_c_im, ssm_d, w_glu_v, w_glu_g, w_out, norm_mix_post, norm_ffn_pre, w_ffn_gate, w_ffn_up, w_ffn_down, norm_ffn_post, loss_target, m_norm_mix_pre, m_w_in, m_w_attn_up, m_ssm_a_re, m_ssm_a_im, m_ssm_log_dt, m_ssm_b_re, m_ssm_b_im, m_ssm_c_re, m_ssm_c_im, m_ssm_d, m_w_glu_v, m_w_glu_g, m_w_out, m_norm_mix_post, m_norm_ffn_pre, m_w_ffn_gate, m_w_ffn_up, m_w_ffn_down, m_norm_ffn_post, v_norm_mix_pre, v_w_in, v_w_attn_up, v_ssm_a_re, v_ssm_a_im, v_ssm_log_dt, v_ssm_b_re, v_ssm_b_im, v_ssm_c_re, v_ssm_c_im, v_ssm_d, v_w_glu_v, v_w_glu_g, v_w_out, v_norm_mix_post, v_norm_ffn_pre, v_w_ffn_gate, v_w_ffn_up, v_w_ffn_down, v_norm_ffn_post):
    given = dict(locals())
    w = {k: given[k] for k in WEIGHT_ORDER}
    m = {k: given["m_" + k] for k in WEIGHT_ORDER}
    v = {k: given["v_" + k] for k in WEIGHT_ORDER}

    shard_shapes = {k: w[k].shape[1:] for k in BIG_WEIGHTS}
    shards = {"w_in": _cast_bf16(w["w_in"][0], "w_in")}
    h1, casts, got = _prologue(x[0], norm_mix_pre[0:1], {k: w[k][0] for k in BIG_WEIGHTS if k != "w_in"},
                               side=_gather_side({"w_in": shards["w_in"]}))
    shards.update(casts)
    big = {"w_in": got[0]}

    loss, grad_x, (slots, mine, theirs), small_grads = _local_step(
        x[0], loss_target[0], big, {k: w[k] for k in SMALL_WEIGHTS}, shards, shard_shapes, h1)
    loss = lax.psum(loss, MESH_AXES)

    last = [k for k in BIG_WEIGHTS if k not in mine]
    mine.update({k: _sum_slots(slots[k], k) for k in last})
    theirs.update(zip(last, _run_side(_swap_side([mine[k] for k in last]), "swap_last_grads")))
    out_g, out_d, out_m, out_v = {}, {}, {}, {}
    for k in BIG_WEIGHTS:
        res = _adamw_big(w[k][0], mine[k], theirs[k], m[k][0], v[k][0], k)
        out_g[k], out_d[k], out_m[k], out_v[k] = (t[None] for t in res)

    pick = lambda tree: {k: tree[k] for k in SMALL_WEIGHTS}
    dg1, shared = small_grads
    late = _run_side(_share_side(_pack_group({"norm_mix_pre": dg1}, PACK_GROUPS[0])), "share_last_grad")[0]
    parts = jnp.concatenate([late, shared], axis=1)
    res = _adamw_small(_pack_small(pick(w)), parts, _pack_small(pick(m)), _pack_small(pick(v)))
    for dst, packed in zip((out_g, out_d, out_m, out_v), res):
        dst.update(_unpack_small(packed, pick(w)))

    return (loss, grad_x[None], *[out_g[k] for k in WEIGHT_ORDER], *[out_d[k] for k in WEIGHT_ORDER],
            *[out_m[k] for k in WEIGHT_ORDER], *[out_v[k] for k in WEIGHT_ORDER])
```

```python
import functools
import math

import jax
import jax.numpy as jnp
from jax import lax
from jax.experimental import pallas as pl
from jax.experimental.pallas import tpu as pltpu

F32 = jnp.float32
BF16 = jnp.bfloat16

EPS = 1e-6
HEAD_DIM = 128
HEADS_PER_GROUP = 4
ATTN_DILATIONS = (1, 4, 16)
ATTN_BLK = 128
N_ATTN_HEADS = HEADS_PER_GROUP * len(ATTN_DILATIONS)
GROUP_W = HEADS_PER_GROUP * HEAD_DIM
HQ = N_ATTN_HEADS * HEAD_DIM
SSM_GROUP = 16
SSM_STATE = 64
SSM_TILE_CH = 128
SSM_TILE_ST = SSM_TILE_CH // SSM_GROUP * SSM_STATE
ADAM_LR = 0.001
ADAM_B1 = 0.9
ADAM_B2 = 0.999
ADAM_EPS = 1e-08
ADAM_WD = 0.01
ADAM_STEP = 10
NEG_BIG = -1e30
V7X_VMEM_LIMIT = 56 * 1024 * 1024
MESH_AXES = ("x", "y", "c")
N_CHIPS = 4


def _pick(n, cands):
    for c in cands:
        if n % c == 0:
            return c
    raise ValueError(f"no tile of {cands} divides {n}")


def _params(sem):
    return pltpu.CompilerParams(dimension_semantics=sem, vmem_limit_bytes=V7X_VMEM_LIMIT)


HBM = pl.BlockSpec(memory_space=pl.ANY)
MESH = pl.DeviceIdType.MESH


class _Side:
    def __init__(self, srcs, out_shapes, sem_shapes, build, aliases=None, relays=False):
        self.srcs, self.out_shapes, self.sem_shapes, self.build = list(srcs), list(out_shapes), list(sem_shapes), build
        self.aliases = dict(aliases or {})
        self.relays = relays

    def start(self, src, dst, sems):
        local, sends = self.build(src, dst, sems)[:2]
        for cp in local + sends:
            cp.start()

    def relay(self, src, dst, sems):
        _, _, arrivals, forwards, _ = self.build(src, dst, sems)
        for cp, forward in zip(arrivals, forwards):
            if forward is not None:
                cp.wait_recv()
                forward.start()

    def wait(self, src, dst, sems, relayed=False):
        local, sends, arrivals, forwards, passed_on = self.build(src, dst, sems)
        for cp, forward in zip(arrivals, forwards):
            if forward is None:
                cp.wait_recv()
            elif not relayed:
                cp.wait_recv()
                forward.start()
        for cp in passed_on:
            cp.wait_recv()
        for cp in sends + [f for f in forwards if f is not None]:
            cp.wait_send()
        for cp in local:
            cp.wait()


def _join_sides(a, b):
    ns, no, nm = len(a.srcs), len(a.out_shapes), len(a.sem_shapes)

    def build(src, dst, sems):
        ra, rb = a.build(src[:ns], dst[:no], sems[:nm]), b.build(src[ns:], dst[no:], sems[nm:])
        return tuple(p + q for p, q in zip(ra, rb))

    aliases = {**a.aliases, **{ns + k: no + v for k, v in b.aliases.items()}}
    return _Side(a.srcs + b.srcs, a.out_shapes + b.out_shapes, a.sem_shapes + b.sem_shapes, build, aliases,
                 a.relays or b.relays)


def _call(body, *, name, grid, in_specs, out_specs, out_shape, semantics, args, scratch_shapes=(), side=None, **kw):
    in_specs, out_specs, out_shape, scratch_shapes = list(in_specs), list(out_specs), list(out_shape), list(scratch_shapes)
    if side is None:
        res = pl.pallas_call(body, name=name, grid=grid, in_specs=in_specs, out_specs=out_specs, out_shape=out_shape,
                             scratch_shapes=scratch_shapes, compiler_params=_params(semantics), **kw)(*args)
        return list(res), []
    n_in, n_out, n_scr = len(in_specs), len(out_specs), len(scratch_shapes)
    ns_in, ns_out = len(side.srcs), len(side.out_shapes)
    n_steps = math.prod(grid)
    relay_at = (3 * n_steps) // 4 if side.relays and n_steps >= 4 else None

    def carrying(*refs):
        ins, s_in = refs[:n_in], refs[n_in:n_in + ns_in]
        o0 = n_in + ns_in
        outs, s_out = refs[o0:o0 + n_out], refs[o0 + n_out:o0 + n_out + ns_out]
        c0 = o0 + n_out + ns_out
        scr, sems = refs[c0:c0 + n_scr], refs[c0 + n_scr:]
        step = functools.reduce(lambda acc, ig: acc * ig[1] + pl.program_id(ig[0]), enumerate(grid), 0)

        @pl.when(step == 0)
        def _():
            side.start(s_in, s_out, sems)

        if relay_at is not None:
            @pl.when(step == relay_at)
            def _():
                side.relay(s_in, s_out, sems)

        body(*ins, *outs, *scr)

        @pl.when(step == n_steps - 1)
        def _():
            side.wait(s_in, s_out, sems, relayed=relay_at is not None)

    res = pl.pallas_call(
        carrying, name=name, grid=grid, in_specs=in_specs + [HBM] * ns_in, out_specs=out_specs + [HBM] * ns_out,
        out_shape=out_shape + side.out_shapes, scratch_shapes=scratch_shapes + side.sem_shapes,
        input_output_aliases={n_in + k: n_out + v for k, v in side.aliases.items()},
        compiler_params=pltpu.CompilerParams(dimension_semantics=("arbitrary",) * len(grid),
                                             vmem_limit_bytes=V7X_VMEM_LIMIT, has_side_effects=True), **kw,
    )(*args, *side.srcs)
    return list(res[:n_out]), list(res[n_out:])


def _run_side(side, name):
    ns, no = len(side.srcs), len(side.out_shapes)

    def body(*refs):
        src, dst, sems = refs[:ns], refs[ns:ns + no], refs[ns + no:]
        side.start(src, dst, sems)
        side.wait(src, dst, sems)

    return list(pl.pallas_call(body, name=name, in_specs=[HBM] * ns, out_specs=[HBM] * no, out_shape=side.out_shapes,
                               scratch_shapes=side.sem_shapes,
                               compiler_params=pltpu.CompilerParams(has_side_effects=True))(*side.srcs))


_DOT_DIMS = {"nn": (((1,), (0,)), ((), ())), "nt": (((1,), (1,)), ((), ())), "tn": (((0,), (0,)), ((), ()))}


MM_VMEM_BUDGET = 44 * 1024 * 1024
MM_STEP_BYTES = 1 << 20
MM_ACC_BYTES = 4
MM_EPILOGUE_COLS = 256


def _size(dtype):
    return jnp.dtype(dtype).itemsize


def _mm_fused(as_, bs, pairs, mode, out_dtypes, name, extras=(), epilogue=None, side=None, out_place=None):
    M = as_[0].shape[0]
    N = bs[0].shape[1] if mode == "nn" else bs[0].shape[0]
    ks_a = [a.shape[1] for a in as_]
    ks_b = [b.shape[0] if mode == "nn" else b.shape[1] for b in bs]
    chunked = epilogue is not None
    if epilogue is None:
        epilogue = lambda rs, es: rs
    offs = [off for _, off in extras]
    place = list(out_place) if out_place else [None] * len(out_dtypes)
    offs_all = offs + [p[1] for p in place if p is not None]
    best = None
    for tm in (2048, 1024, 512, 256, 128):
        for tn in (2048, 1024, 512, 256, 128):
            if M % tm or N % tn or any(off % tn for off in offs_all):
                continue
            vmem = (sum(2 * tm * k * 2 for k in ks_a) + sum(2 * k * tn * 2 for k in ks_b)
                    + sum(2 * tm * tn * _size(d) for d in out_dtypes) + sum(2 * tm * tn * _size(e.dtype) for e, _ in extras)
                    + len(pairs) * tm * tn * 4)
            cost = sum(k * N * 2 for k in ks_b) * (M // tm) + (M // tm) * (N // tn) * MM_STEP_BYTES
            if vmem <= MM_VMEM_BUDGET and (best is None or cost < best[0]):
                best = (cost, tm, tn)
    _, tm, tn = best
    na, nb, ne, no = len(as_), len(bs), len(extras), len(out_dtypes)
    dims = _DOT_DIMS[mode]

    sub = MM_EPILOGUE_COLS if chunked and tn % MM_EPILOGUE_COLS == 0 else tn

    def body(*refs):
        a_refs, b_refs = refs[:na], refs[na:na + nb]
        e_refs, o_refs = refs[na + nb:na + nb + ne], refs[na + nb + ne:]
        for c0 in range(0, tn, sub):
            cs = slice(c0, c0 + sub)
            rs = [lax.dot_general(a_refs[ai][...], b_refs[bi][:, cs] if mode == "nn" else b_refs[bi][cs, :], dims,
                                  preferred_element_type=F32) for ai, bi in pairs]
            outs = epilogue(rs, [e[:, cs] for e in e_refs])
            for o_ref, o in zip(o_refs, outs):
                o_ref[:, cs] = o.astype(o_ref.dtype)

    a_specs = [pl.BlockSpec((tm, k), lambda i, j: (i, 0)) for k in ks_a]
    if mode == "nn":
        b_specs = [pl.BlockSpec((k, tn), lambda i, j: (0, j)) for k in ks_b]
    else:
        b_specs = [pl.BlockSpec((tn, k), lambda i, j: (j, 0)) for k in ks_b]
    e_specs = [pl.BlockSpec((tm, tn), lambda i, j, o=off // tn: (i, o + j)) for off in offs]
    o_specs = [pl.BlockSpec((tm, tn), lambda i, j, o=(p[1] // tn if p else 0): (i, o + j)) for p in place]
    outs, carried = _call(
        body, name=name, grid=(M // tm, N // tn), in_specs=a_specs + b_specs + e_specs, out_specs=o_specs,
        out_shape=[jax.ShapeDtypeStruct((M, p[0] if p else N), d) for d, p in zip(out_dtypes, place)],
        semantics=("parallel", "arbitrary"),
        args=[*as_, *bs, *[e for e, _ in extras]], side=side)
    return outs if side is None else (outs, carried)


def _mm(a, b, mode, out_dtype, name, side=None):
    res = _mm_fused([a], [b], [(0, 0)], mode, [out_dtype], name, side=side)
    return res[0] if side is None else (res[0][0], res[1])


def _mm_kloop(a, b, mode, out_dtype, name, second=None, side=None):
    if mode == "nn":
        (M, K), (_, N) = a.shape, b.shape
    elif mode == "nt":
        (M, K), (N, _) = a.shape, b.shape
    else:
        (K, M), (_, N) = a.shape, b.shape
    products = 1 if second is None else 2
    best = None
    for tm in (2816, 2048, 1408, 1024, 512, 256, 128):
        for tn in (2816, 2432, 2048, 1408, 1024, 512, 256, 128):
            for tk in (2816, 2432, 2048, 1408, 1024, 512, 256, 128):
                if M % tm or N % tn or K % tk:
                    continue
                vmem = 2 * tm * tn * 4 + 2 * tm * tn * _size(out_dtype) + products * 2 * tk * (tm + tn) * 2
                steps = (M // tm) * (N // tn) * (K // tk)
                cost = (K * M * 2 * (N // tn) + K * N * 2 * (M // tm) + steps * MM_STEP_BYTES
                        + steps * tm * tn * MM_ACC_BYTES)
                if vmem <= MM_VMEM_BUDGET and (best is None or cost < best[0]):
                    best = (cost, tm, tn, tk)
    _, tm, tn, tk = best
    nk = K // tk
    dims = _DOT_DIMS[mode]

    def body(*refs):
        o_ref, acc_ref = refs[-2:]
        k = pl.program_id(2)

        @pl.when(k == 0)
        def _():
            acc_ref[...] = jnp.zeros_like(acc_ref)

        for p in range(products):
            @pl.when(jnp.logical_and(k >= p * nk, k < (p + 1) * nk))
            def _(p=p):
                acc_ref[...] += lax.dot_general(refs[2 * p][...], refs[2 * p + 1][...], dims, preferred_element_type=F32)

        @pl.when(k == products * nk - 1)
        def _():
            o_ref[...] = acc_ref[...].astype(o_ref.dtype)

    def a_spec(p):
        kk = lambda k: jnp.clip(k - p * nk, 0, nk - 1)
        if mode == "tn":
            return pl.BlockSpec((tk, tm), lambda i, j, k: (kk(k), i))
        return pl.BlockSpec((tm, tk), lambda i, j, k: (i, kk(k)))

    def b_spec(p):
        kk = lambda k: jnp.clip(k - p * nk, 0, nk - 1)
        if mode == "nt":
            return pl.BlockSpec((tn, tk), lambda i, j, k: (j, kk(k)))
        return pl.BlockSpec((tk, tn), lambda i, j, k: (kk(k), j))

    o_spec = pl.BlockSpec((tm, tn), lambda i, j, k: (i, j))
    operands = (a, b) + (tuple(second) if second is not None else ())
    outs, carried = _call(
        body, name=name, grid=(M // tm, N // tn, products * nk),
        in_specs=[spec(p) for p in range(products) for spec in (a_spec, b_spec)], out_specs=[o_spec],
        out_shape=[jax.ShapeDtypeStruct((M, N), out_dtype)], scratch_shapes=[pltpu.VMEM((tm, tn), F32)],
        semantics=("parallel", "parallel", "arbitrary"), args=operands, side=side)
    return outs[0] if side is None else (outs[0], carried)


def _sigmoid(v):
    return 0.5 * jnp.tanh(0.5 * v) + 0.5


_GELU_C = math.sqrt(2.0 / math.pi)


def _gelu(v):
    return 0.5 * v * (1.0 + jnp.tanh(_GELU_C * (v + 0.044715 * v * v * v)))


def _gelu_grad(v):
    t = jnp.tanh(_GELU_C * (v + 0.044715 * v * v * v))
    return 0.5 * (1.0 + t) + 0.5 * v * (1.0 - t * t) * _GELU_C * (1.0 + 3.0 * 0.044715 * v * v)


def _rms(v, gain):
    r = lax.rsqrt(jnp.mean(v * v, axis=-1, keepdims=True) + EPS)
    return v * r * gain


def _rms_bwd(v, gain, dy):
    r = lax.rsqrt(jnp.mean(v * v, axis=-1, keepdims=True) + EPS)
    a = dy * gain
    dv = r * a - v * (r * r * r) * jnp.mean(a * v, axis=-1, keepdims=True)
    return dv, dy * v * r


def _row_tile(s):
    return _pick(s, (256, 128, 64, 8))


def _norm_in(x, gain):
    s, d = x.shape
    tr = _row_tile(s)

    def body(x_ref, g_ref, h_ref):
        h_ref[...] = _rms(x_ref[...], g_ref[...]).astype(BF16)

    row = pl.BlockSpec((tr, d), lambda i: (i, 0))
    vec = pl.BlockSpec((1, d), lambda i: (0, 0))
    return pl.pallas_call(body, name="norm_in", grid=(s // tr,), in_specs=[row, vec], out_specs=row,
                          out_shape=jax.ShapeDtypeStruct((s, d), BF16), compiler_params=_params(("parallel",)))(x, gain)


def _prologue(x, gain, weights, side=None):
    s, d = x.shape
    tr = _row_tile(s)
    steps = s // tr
    names = list(weights)
    tiles = []
    for k in names:
        r, _ = weights[k].shape
        tiles.append(next(t for t in range(16, r + 1, 16) if r % t == 0 and r // t <= steps))

    def body(*refs):
        x_ref, g_ref = refs[:2]
        w_refs, h_ref, o_refs = refs[2:2 + len(names)], refs[2 + len(names)], refs[3 + len(names):]
        h_ref[...] = _rms(x_ref[...], g_ref[...]).astype(BF16)
        for w_ref, o_ref in zip(w_refs, o_refs):
            o_ref[...] = w_ref[...].astype(BF16)

    row = pl.BlockSpec((tr, d), lambda i: (i, 0))
    w_specs = [pl.BlockSpec((t, weights[k].shape[1]), lambda i, last=weights[k].shape[0] // t - 1: (jnp.minimum(i, last), 0))
               for k, t in zip(names, tiles)]
    outs, carried = _call(
        body, name="prologue", grid=(steps,), in_specs=[row, pl.BlockSpec((1, d), lambda i: (0, 0))] + w_specs,
        out_specs=[row] + w_specs,
        out_shape=[jax.ShapeDtypeStruct((s, d), BF16)] + [jax.ShapeDtypeStruct(weights[k].shape, BF16) for k in names],
        semantics=("arbitrary",), args=[x, gain] + [weights[k] for k in names], side=side)
    return outs[0], dict(zip(names, outs[1:])), carried


def _norm_mid(x, mo, g_post, g_pre):
    s, d = x.shape
    tr = _row_tile(s)

    def body(x_ref, mo_ref, g2_ref, g3_ref, x2_ref, h2_ref):
        x2 = x_ref[...] + _rms(mo_ref[...], g2_ref[...])
        x2_ref[...] = x2
        h2_ref[...] = _rms(x2, g3_ref[...]).astype(BF16)

    row = pl.BlockSpec((tr, d), lambda i: (i, 0))
    vec = pl.BlockSpec((1, d), lambda i: (0, 0))
    return pl.pallas_call(
        body, name="norm_mid", grid=(s // tr,), in_specs=[row, row, vec, vec], out_specs=[row, row],
        out_shape=[jax.ShapeDtypeStruct((s, d), F32), jax.ShapeDtypeStruct((s, d), BF16)],
        compiler_params=_params(("parallel",)))(x, mo, g_post, g_pre)


def _loss_head(x2, f, g_post, target):
    s, d = x2.shape
    tr = _row_tile(s)

    def body(x2_ref, f_ref, g_ref, t_ref, loss_ref, dout_ref, df_ref, dg_ref):
        @pl.when(pl.program_id(0) == 0)
        def _():
            loss_ref[...] = jnp.zeros_like(loss_ref)
            dg_ref[...] = jnp.zeros_like(dg_ref)

        fv = f_ref[...]
        g = g_ref[...]
        err = x2_ref[...] + _rms(fv, g) - t_ref[...]
        loss_ref[...] += 0.5 * jnp.sum(jnp.mean(err * err, axis=-1, keepdims=True), axis=0, keepdims=True)
        dout = err * (1.0 / d)
        dout_ref[...] = dout
        df, dg = _rms_bwd(fv, g, dout)
        df_ref[...] = df.astype(BF16)
        dg_ref[...] += jnp.sum(dg, axis=0, keepdims=True)

    row = pl.BlockSpec((tr, d), lambda i: (i, 0))
    vec = pl.BlockSpec((1, d), lambda i: (0, 0))
    one = pl.BlockSpec((1, 1), lambda i: (0, 0))
    return pl.pallas_call(
        body, name="loss_head", grid=(s // tr,), in_specs=[row, row, vec, row], out_specs=[one, row, row, vec],
        out_shape=[jax.ShapeDtypeStruct((1, 1), F32), jax.ShapeDtypeStruct((s, d), F32),
                   jax.ShapeDtypeStruct((s, d), BF16), jax.ShapeDtypeStruct((1, d), F32)],
        compiler_params=_params(("arbitrary",)))(x2, f, g_post, target)


def _norm_mid_bwd(x2, mo, g_post, g_pre, dout, dh2):
    s, d = x2.shape
    tr = _row_tile(s)

    def body(x2_ref, mo_ref, g2_ref, g3_ref, dout_ref, dh2_ref, dx2_ref, dmo_ref, dg2_ref, dg3_ref):
        @pl.when(pl.program_id(0) == 0)
        def _():
            dg2_ref[...] = jnp.zeros_like(dg2_ref)
            dg3_ref[...] = jnp.zeros_like(dg3_ref)

        dv, dg3 = _rms_bwd(x2_ref[...], g3_ref[...], dh2_ref[...])
        dx2 = dout_ref[...] + dv
        dx2_ref[...] = dx2
        dmo, dg2 = _rms_bwd(mo_ref[...], g2_ref[...], dx2)
        dmo_ref[...] = dmo.astype(BF16)
        dg2_ref[...] += jnp.sum(dg2, axis=0, keepdims=True)
        dg3_ref[...] += jnp.sum(dg3, axis=0, keepdims=True)

    row = pl.BlockSpec((tr, d), lambda i: (i, 0))
    vec = pl.BlockSpec((1, d), lambda i: (0, 0))
    return pl.pallas_call(
        body, name="norm_mid_bwd", grid=(s // tr,), in_specs=[row, row, vec, vec, row, row],
        out_specs=[row, row, vec, vec],
        out_shape=[jax.ShapeDtypeStruct((s, d), F32), jax.ShapeDtypeStruct((s, d), BF16),
                   jax.ShapeDtypeStruct((1, d), F32), jax.ShapeDtypeStruct((1, d), F32)],
        compiler_params=_params(("arbitrary",)))(x2, mo, g_post, g_pre, dout, dh2)


def _norm_in_bwd(x, gain, dh, dx2):
    s, d = x.shape
    tr = _row_tile(s)

    def body(x_ref, g_ref, dh_ref, dx2_ref, dx_ref, dg_ref):
        @pl.when(pl.program_id(0) == 0)
        def _():
            dg_ref[...] = jnp.zeros_like(dg_ref)

        dv, dg = _rms_bwd(x_ref[...], g_ref[...], dh_ref[...])
        dx_ref[...] = dx2_ref[...] + dv
        dg_ref[...] += jnp.sum(dg, axis=0, keepdims=True)

    row = pl.BlockSpec((tr, d), lambda i: (i, 0))
    vec = pl.BlockSpec((1, d), lambda i: (0, 0))
    return pl.pallas_call(
        body, name="norm_in_bwd", grid=(s // tr,), in_specs=[row, vec, row, row], out_specs=[row, vec],
        out_shape=[jax.ShapeDtypeStruct((s, d), F32), jax.ShapeDtypeStruct((1, d), F32)],
        compiler_params=_params(("arbitrary",)))(x, gain, dh, dx2)


def _swiglu_epilogue(rs, es):
    g, u = rs
    return [g * _sigmoid(g) * u, g, u]


def _swiglu_bwd_epilogue(rs, es):
    d = rs[0]
    g, u = es[0].astype(F32), es[1].astype(F32)
    sg = _sigmoid(g)
    return [d * u * sg * (1.0 + g * (1.0 - sg)), d * g * sg]


def _sum_epilogue(rs, es):
    return [rs[0] + rs[1]]


def _gates_epilogue(rs, es):
    ab, gv, gg = rs
    ga, gs = es
    return [_sigmoid(ga) * ab + _sigmoid(gs) * gv * _sigmoid(gg), ab, gv, gg]


def _gates_bwd_epilogue(rs, es):
    dm = rs[0]
    ga, gs, ab, gv, gg = (e.astype(F32) for e in es)
    sa, ss, sg = _sigmoid(ga), _sigmoid(gs), _sigmoid(gg)
    dsb = dm * ss
    return [dm * ab * sa * (1.0 - sa), dm * gv * sg * ss * (1.0 - ss), dm * sa, dsb * sg, dsb * gv * sg * (1.0 - sg)]


ATTN_ROWS = 2048


def _dilate_qkv(z, g, d):
    s = z.shape[0]
    tm = ATTN_ROWS
    per = tm // d
    nh = HEADS_PER_GROUP

    def body(z_ref, o_ref):
        for r in range(d):
            rows = z_ref[...] if d == 1 else z_ref[pl.ds(r, per, stride=d), :]
            o_ref[0, r] = rows.astype(BF16)

    return pl.pallas_call(
        body, name=f"dilate_qkv_{g}", grid=(s // tm, 3, nh),
        in_specs=[pl.BlockSpec((tm, HEAD_DIM), lambda i, w, h: (i, (3 * w + g) * nh + h))],
        out_specs=pl.BlockSpec((1, d, per, HEAD_DIM), lambda i, w, h: (w, 0, i, h)),
        out_shape=jax.ShapeDtypeStruct((3, d, s // d, GROUP_W), BF16),
        compiler_params=_params(("parallel", "parallel", "parallel")))(z)


def _undilate_dqkv(dqkv, dz, g, d):
    s = dz.shape[0]
    tm = ATTN_ROWS
    per = tm // d
    nh = HEADS_PER_GROUP

    def body(i_ref, dz_ref, o_ref, nat_ref):
        del dz_ref
        if d == 1:
            o_ref[...] = i_ref[0, 0]
        else:
            for r in range(d):
                nat_ref[pl.ds(r, per, stride=d), :] = i_ref[0, r].astype(F32)
            o_ref[...] = nat_ref[...].astype(BF16)

    return pl.pallas_call(
        body, name=f"undilate_dqkv_{g}", grid=(s // tm, 3, nh),
        in_specs=[pl.BlockSpec((1, d, per, HEAD_DIM), lambda i, w, h: (w, 0, i, h)),
                  pl.BlockSpec(memory_space=pl.ANY)],
        out_specs=pl.BlockSpec((tm, HEAD_DIM), lambda i, w, h: (i, (3 * w + g) * nh + h)),
        out_shape=jax.ShapeDtypeStruct(dz.shape, dz.dtype), input_output_aliases={1: 0},
        scratch_shapes=[pltpu.VMEM((tm, HEAD_DIM), F32)],
        compiler_params=_params(("parallel", "parallel", "parallel")))(dqkv, dz)


def _alibi_slope(head):
    return 2.0 ** (-8.0 * (head + 1) / N_ATTN_HEADS)


def _dot_nt(a, b):
    return lax.dot_general(a, b, _DOT_DIMS["nt"], preferred_element_type=F32)


def _dot_tn(a, b):
    return lax.dot_general(a, b, _DOT_DIMS["tn"], preferred_element_type=F32)


def _dot(a, b):
    return jnp.dot(a, b, preferred_element_type=F32)


GROUP_ROWS = HEADS_PER_GROUP * ATTN_BLK


def _band_bias(g, d, pairs):
    qi = jnp.arange(ATTN_BLK)[:, None]
    ki = jnp.arange(ATTN_BLK)[None, :]
    rows = []
    for hh in range(HEADS_PER_GROUP):
        slope_d = _alibi_slope(g * HEADS_PER_GROUP + hh) * d
        tiles = []
        for kind in pairs:
            dist = qi - ki if kind == "cur" else ATTN_BLK + qi - ki
            ok = dist >= 0 if kind == "cur" else dist <= ATTN_BLK
            tiles.append(jnp.where(ok, -slope_d * dist.astype(F32), NEG_BIG))
        rows.append(jnp.concatenate(tiles, axis=1))
    return jnp.concatenate(rows, axis=0).astype(F32)


def _tile_cols(t):
    return slice(t * ATTN_BLK, (t + 1) * ATTN_BLK)


def _attn_fwd(qkv, g, d):
    _, _, L, _ = qkv.shape
    nb = L // ATTN_BLK
    scale = HEAD_DIM ** -0.5

    def body(q_ref, kc_ref, kp_ref, vc_ref, vp_ref, bias_ref, o_ref, lse_ref, s_ref, p_ref):
        n = pl.program_id(1)
        for hh in range(HEADS_PER_GROUP):
            cols, rows = _tile_cols(hh), _tile_cols(hh)
            q = q_ref[0, 0, :, cols]
            s_ref[rows, _tile_cols(0)] = _dot_nt(q, kp_ref[0, 0, :, cols])
            s_ref[rows, _tile_cols(1)] = _dot_nt(q, kc_ref[0, 0, :, cols])
        col = lax.broadcasted_iota(jnp.int32, (GROUP_ROWS, 2 * ATTN_BLK), 1)
        s = s_ref[...] * scale + bias_ref[...]
        s = jnp.where(jnp.logical_and(col < ATTN_BLK, n == 0), NEG_BIG, s)
        m = jnp.max(s, axis=-1, keepdims=True)
        e = jnp.exp(s - m)
        l = jnp.sum(e, axis=-1, keepdims=True)
        p_ref[...] = (e * (1.0 / l)).astype(BF16)
        lse = m + jnp.log(l)
        for hh in range(HEADS_PER_GROUP):
            cols, rows = _tile_cols(hh), _tile_cols(hh)
            o_ref[0, :, cols] = (_dot(p_ref[rows, _tile_cols(0)], vp_ref[0, 0, :, cols])
                                 + _dot(p_ref[rows, _tile_cols(1)], vc_ref[0, 0, :, cols]))
            lse_ref[0, :, cols] = jnp.broadcast_to(lse[rows], (ATTN_BLK, HEAD_DIM))

    def spec(w, shift):
        return pl.BlockSpec((1, 1, ATTN_BLK, GROUP_W), lambda r, n: (w, r, jnp.maximum(n + shift, 0), 0))

    out = pl.BlockSpec((1, ATTN_BLK, GROUP_W), lambda r, n: (r, n, 0))
    bias = _band_bias(g, d, ("prev", "cur"))
    return pl.pallas_call(
        body, name=f"attn_fwd_{g}", grid=(d, nb),
        in_specs=[spec(0, 0), spec(1, 0), spec(1, -1), spec(2, 0), spec(2, -1),
                  pl.BlockSpec(bias.shape, lambda r, n: (0, 0))],
        out_specs=[out, out], out_shape=[jax.ShapeDtypeStruct((d, L, GROUP_W), F32)] * 2,
        scratch_shapes=[pltpu.VMEM((GROUP_ROWS, 2 * ATTN_BLK), F32), pltpu.VMEM((GROUP_ROWS, 2 * ATTN_BLK), BF16)],
        compiler_params=_params(("parallel", "parallel")))(qkv, qkv, qkv, qkv, qkv, bias)


def _attn_bwd(qkv, do, lse, cc, g, d, side=None):
    _, _, L, _ = qkv.shape
    nb = L // ATTN_BLK
    scale = HEAD_DIM ** -0.5
    a_, b_, c_ = _tile_cols(0), _tile_cols(1), _tile_cols(2)

    def body(q0_ref, q1_ref, k0_ref, kp_ref, v0_ref, vp_ref, do0_ref, do1_ref, l0_ref, l1_ref, c0_ref, c1_ref,
             bias_ref, o_ref, s_ref, dp_ref, l_ref, c_ref, p_ref, ds_ref):
        n = pl.program_id(1)
        for hh in range(HEADS_PER_GROUP):
            cols, rows = _tile_cols(hh), _tile_cols(hh)
            q0, q1 = q0_ref[0, 0, :, cols], q1_ref[0, 0, :, cols]
            k0, kp = k0_ref[0, 0, :, cols], kp_ref[0, 0, :, cols]
            v0, vp = v0_ref[0, 0, :, cols], vp_ref[0, 0, :, cols]
            do0, do1 = do0_ref[0, :, cols], do1_ref[0, :, cols]
            s_ref[rows, a_], s_ref[rows, b_], s_ref[rows, c_] = _dot_nt(q0, k0), _dot_nt(q0, kp), _dot_nt(q1, k0)
            dp_ref[rows, a_], dp_ref[rows, b_], dp_ref[rows, c_] = _dot_nt(do0, v0), _dot_nt(do0, vp), _dot_nt(do1, v0)
            l_ref[rows, a_], l_ref[rows, b_], l_ref[rows, c_] = l0_ref[0, :, cols], l0_ref[0, :, cols], l1_ref[0, :, cols]
            c_ref[rows, a_], c_ref[rows, b_], c_ref[rows, c_] = c0_ref[0, :, cols], c0_ref[0, :, cols], c1_ref[0, :, cols]
        col = lax.broadcasted_iota(jnp.int32, (GROUP_ROWS, 3 * ATTN_BLK), 1)
        tile = col // ATTN_BLK
        gone = jnp.logical_or(jnp.logical_and(tile == 1, n == 0), jnp.logical_and(tile == 2, n == nb - 1))
        s = jnp.where(gone, NEG_BIG, s_ref[...] * scale + bias_ref[...])
        p = jnp.exp(s - l_ref[...])
        p_ref[...] = p.astype(BF16)
        ds_ref[...] = (p * (dp_ref[...] + c_ref[...])).astype(BF16)
        for hh in range(HEADS_PER_GROUP):
            cols, rows = _tile_cols(hh), _tile_cols(hh)
            q0, q1 = q0_ref[0, 0, :, cols], q1_ref[0, 0, :, cols]
            k0, kp = k0_ref[0, 0, :, cols], kp_ref[0, 0, :, cols]
            do0, do1 = do0_ref[0, :, cols], do1_ref[0, :, cols]
            o_ref[0, 0, :, cols] = ((_dot(ds_ref[rows, a_], k0) + _dot(ds_ref[rows, b_], kp)) * scale).astype(BF16)
            o_ref[1, 0, :, cols] = ((_dot_tn(ds_ref[rows, a_], q0) + _dot_tn(ds_ref[rows, c_], q1)) * scale).astype(BF16)
            o_ref[2, 0, :, cols] = (_dot_tn(p_ref[rows, a_], do0) + _dot_tn(p_ref[rows, c_], do1)).astype(BF16)

    def spec(w, shift):
        return pl.BlockSpec((1, 1, ATTN_BLK, GROUP_W), lambda r, n: (w, r, jnp.clip(n + shift, 0, nb - 1), 0))

    def spec3(shift):
        return pl.BlockSpec((1, ATTN_BLK, GROUP_W), lambda r, n: (r, jnp.clip(n + shift, 0, nb - 1), 0))

    bias = _band_bias(g, d, ("cur", "prev", "prev"))
    wide = (GROUP_ROWS, 3 * ATTN_BLK)
    outs, carried = _call(
        body, name=f"attn_bwd_{g}", grid=(d, nb),
        in_specs=[spec(0, 0), spec(0, 1), spec(1, 0), spec(1, -1), spec(2, 0), spec(2, -1),
                  spec3(0), spec3(1), spec3(0), spec3(1), spec3(0), spec3(1), pl.BlockSpec(wide, lambda r, n: (0, 0))],
        out_specs=[pl.BlockSpec((3, 1, ATTN_BLK, GROUP_W), lambda r, n: (0, r, n, 0))],
        out_shape=[jax.ShapeDtypeStruct((3, d, L, GROUP_W), BF16)],
        scratch_shapes=[pltpu.VMEM(wide, F32)] * 4 + [pltpu.VMEM(wide, BF16)] * 2, semantics=("parallel", "parallel"),
        args=[qkv, qkv, qkv, qkv, qkv, qkv, do, do, lse, lse, cc, cc, bias], side=side)
    return outs[0] if side is None else (outs[0], carried)


def _load_natural(refs, nat_refs):
    for g, d in enumerate(ATTN_DILATIONS):
        if d == 1:
            nat_refs[g][...] = refs[g][0]
        else:
            per = ATTN_ROWS // d
            for r in range(d):
                nat_refs[g][pl.ds(r, per, stride=d), :] = refs[g][r]


def _mix_weights(lse_nat):
    l0, l1, l2 = lse_nat[0][...], lse_nat[1][...], lse_nat[2][...]
    m = jnp.maximum(jnp.maximum(l0, l1), l2)
    e0, e1, e2 = jnp.exp(l0 - m), jnp.exp(l1 - m), jnp.exp(l2 - m)
    inv = 1.0 / (e0 + e1 + e2)
    return e0 * inv, e1 * inv, e2 * inv


def _dilated_specs(s):
    return [pl.BlockSpec((d, ATTN_ROWS // d, HEAD_DIM), lambda i, h: (0, i, h)) for d in ATTN_DILATIONS]


NATURAL_SCRATCH = [pltpu.VMEM((ATTN_ROWS, HEAD_DIM), F32)] * (2 * len(ATTN_DILATIONS))


def _attn_merge(outs, lses):
    s = outs[0].shape[0] * outs[0].shape[1]

    def body(o0, o1, o2, l0, l1, l2, a_ref, *nat):
        onat, lnat = nat[:3], nat[3:]
        _load_natural((o0, o1, o2), onat)
        _load_natural((l0, l1, l2), lnat)
        w0, w1, w2 = _mix_weights(lnat)
        a_ref[...] = (w0 * onat[0][...] + w1 * onat[1][...] + w2 * onat[2][...]).astype(BF16)

    return pl.pallas_call(
        body, name="attn_merge", grid=(s // ATTN_ROWS, HEADS_PER_GROUP), in_specs=_dilated_specs(s) * 2,
        out_specs=pl.BlockSpec((ATTN_ROWS, HEAD_DIM), lambda i, h: (i, h)),
        out_shape=jax.ShapeDtypeStruct((s, GROUP_W), BF16), scratch_shapes=NATURAL_SCRATCH,
        compiler_params=_params(("parallel", "parallel")))(*outs, *lses)


def _attn_merge_bwd(outs, lses, dattn):
    s = dattn.shape[0]

    def body(o0, o1, o2, l0, l1, l2, da_ref, do0, do1, do2, c0, c1, c2, *nat):
        onat, lnat = nat[:3], nat[3:]
        _load_natural((o0, o1, o2), onat)
        _load_natural((l0, l1, l2), lnat)
        ws = _mix_weights(lnat)
        da = da_ref[...]
        attn = ws[0] * onat[0][...] + ws[1] * onat[1][...] + ws[2] * onat[2][...]
        tot = jnp.broadcast_to(jnp.sum(da * attn, axis=-1, keepdims=True), (ATTN_ROWS, HEAD_DIM))
        for g, (d, do_ref, c_ref) in enumerate(zip(ATTN_DILATIONS, (do0, do1, do2), (c0, c1, c2))):
            if d == 1:
                do_ref[0] = (ws[g] * da).astype(BF16)
                c_ref[0] = -ws[g] * tot
            else:
                onat[g][...] = ws[g] * da
                lnat[g][...] = -ws[g] * tot
                per = ATTN_ROWS // d
                for r in range(d):
                    do_ref[r] = onat[g][pl.ds(r, per, stride=d), :].astype(BF16)
                    c_ref[r] = lnat[g][pl.ds(r, per, stride=d), :]

    dil = _dilated_specs(s)
    shapes = [jax.ShapeDtypeStruct(o.shape, BF16) for o in outs] + [jax.ShapeDtypeStruct(o.shape, F32) for o in outs]
    return pl.pallas_call(
        body, name="attn_merge_bwd", grid=(s // ATTN_ROWS, HEADS_PER_GROUP),
        in_specs=dil * 2 + [pl.BlockSpec((ATTN_ROWS, HEAD_DIM), lambda i, h: (i, h))], out_specs=dil * 2,
        out_shape=shapes, scratch_shapes=NATURAL_SCRATCH,
        compiler_params=_params(("parallel", "parallel")))(*outs, *lses, dattn)


def _ssm_prepare(a_re, a_im, log_dt, b_re, b_im, c_re, c_im):
    n_g = a_re.shape[0]
    nj = n_g * SSM_GROUP // SSM_TILE_CH
    gpt = SSM_TILE_CH // SSM_GROUP
    dt = jnp.exp(log_dt)[:, None]
    mag = jnp.exp(a_re * dt)
    lr, li = mag * jnp.cos(a_im * dt), mag * jnp.sin(a_im * dt)
    den = a_re * a_re + a_im * a_im
    cr = ((lr - 1.0) * a_re + li * a_im) / den
    ci = (li * a_re - (lr - 1.0) * a_im) / den
    bb_re = cr[..., None] * b_re - ci[..., None] * b_im
    bb_im = cr[..., None] * b_im + ci[..., None] * b_re
    eye = jnp.eye(gpt, dtype=F32)

    def b_tiles(t):
        t = t.transpose(0, 2, 1).reshape(nj, gpt, SSM_GROUP, SSM_STATE)
        return jnp.einsum("jgcp,gh->jgchp", t, eye).reshape(nj, SSM_TILE_CH, SSM_TILE_ST)

    def c_tiles(t):
        t = t.reshape(nj, gpt, SSM_GROUP, SSM_STATE)
        return jnp.einsum("jgcp,gh->jhpgc", t, eye).reshape(nj, SSM_TILE_ST, SSM_TILE_CH)

    lam = jnp.stack([lr.reshape(-1), li.reshape(-1)])
    bmat = jnp.concatenate([b_tiles(bb_re), b_tiles(bb_im)], axis=2)
    cmat = jnp.concatenate([c_tiles(c_re), -c_tiles(c_im)], axis=1)
    return lam, bmat, cmat


SSM_SEGMENTS = 8


def _to_segment_order(nat, perm_ref):
    per = nat.shape[0] // SSM_SEGMENTS
    for i in range(SSM_SEGMENTS):
        perm_ref[pl.ds(i, per, stride=SSM_SEGMENTS), :] = nat[i * per:(i + 1) * per, :]
    return perm_ref[...]


def _to_time_order(val, perm_ref, store):
    per = val.shape[0] // SSM_SEGMENTS
    perm_ref[...] = val
    for i in range(SSM_SEGMENTS):
        store(i, perm_ref[pl.ds(i, per, stride=SSM_SEGMENTS), :])


def _fill_powers(lam_ref, w_ref, nj, tau_n):
    for j in range(nj):
        st = slice(j * SSM_TILE_ST, (j + 1) * SSM_TILE_ST)
        lr = jnp.broadcast_to(lam_ref[0:1, st], (SSM_SEGMENTS, SSM_TILE_ST))
        li = jnp.broadcast_to(lam_ref[1:2, st], (SSM_SEGMENTS, SSM_TILE_ST))
        wr, wi = lr, li
        for tau in range(tau_n):
            rows = slice(tau * SSM_SEGMENTS, (tau + 1) * SSM_SEGMENTS)
            w_ref[j, rows, :SSM_TILE_ST] = wr
            w_ref[j, rows, SSM_TILE_ST:] = wi
            wr, wi = wr * lr - wi * li, wr * li + wi * lr


def _segment_scan(src, xs_ref, w_tile, lr, li, cr, ci, conj, reverse):
    seg, half = SSM_SEGMENTS, SSM_TILE_ST
    tau_n = src.shape[0] // seg
    sgn = -1.0 if conj else 1.0
    lr8 = jnp.broadcast_to(lr, (seg, half))
    li8 = jnp.broadcast_to(li, (seg, half)) * sgn
    xr = jnp.zeros((seg, half), F32)
    xi = jnp.zeros((seg, half), F32)
    order = range(tau_n - 1, -1, -1) if reverse else range(tau_n)
    for tau in order:
        rows = slice(tau * seg, (tau + 1) * seg)
        xr, xi = lr8 * xr - li8 * xi + src[rows, :half], lr8 * xi + li8 * xr + src[rows, half:]
        xs_ref[rows, :half] = xr
        xs_ref[rows, half:] = xi
    pr = w_tile[(tau_n - 1) * seg:(tau_n - 1) * seg + 1, :half]
    pi = w_tile[(tau_n - 1) * seg:(tau_n - 1) * seg + 1, half:] * sgn
    fr, fi = cr, ci
    ins_r, ins_i = [None] * seg, [None] * seg
    runs = range(seg - 1, -1, -1) if reverse else range(seg)
    for i in runs:
        ins_r[i], ins_i[i] = fr, fi
        fr, fi = xr[i:i + 1, :] + pr * fr - pi * fi, xi[i:i + 1, :] + pr * fi + pi * fr
    in_r = jnp.concatenate(ins_r, axis=0)
    in_i = jnp.concatenate(ins_i, axis=0)
    for tau in range(tau_n):
        rows = slice(tau * seg, (tau + 1) * seg)
        wrow = (tau_n - 1 - tau) if reverse else tau
        wr = w_tile[wrow * seg:(wrow + 1) * seg, :half]
        wi = w_tile[wrow * seg:(wrow + 1) * seg, half:] * sgn
        xs_ref[rows, :half] += wr * in_r - wi * in_i
        xs_ref[rows, half:] += wr * in_i + wi * in_r
    return (fr, fi), (in_r, in_i)


def _ssm_dims(z, bmat, u_off):
    s = z.shape[0]
    nj = bmat.shape[0]
    t_rows = _pick(s, (256, 128))
    return s, nj, nj * SSM_TILE_CH, nj * SSM_TILE_ST, t_rows


def _ssm_fwd(z, bmat, cmat, lam, dskip, u_off, side=None):
    s, nj, w, ns, t_rows = _ssm_dims(z, bmat, u_off)
    per = t_rows // SSM_SEGMENTS

    def body(*refs):
        u_refs = refs[:nj]
        b_ref, c_ref, lam_ref, d_ref, y_ref, yg_ref, xin_ref, carry_ref, w_ref, xs_ref, perm_ref = refs[nj:]

        @pl.when(pl.program_id(0) == 0)
        def _():
            carry_ref[...] = jnp.zeros_like(carry_ref)
            _fill_powers(lam_ref, w_ref, nj, per)

        xin_ref[0] = carry_ref[...]
        for j in range(nj):
            st = slice(j * SSM_TILE_ST, (j + 1) * SSM_TILE_ST)
            ch = slice(j * SSM_TILE_CH, (j + 1) * SSM_TILE_CH)
            up = _to_segment_order(u_refs[j], perm_ref)
            bu = _dot(up.astype(BF16), b_ref[j])
            (fr, fi), _ = _segment_scan(bu, xs_ref, w_ref.at[j], lam_ref[0:1, st], lam_ref[1:2, st],
                                        carry_ref[0:1, st], carry_ref[1:2, st], conj=False, reverse=False)
            carry_ref[0:1, st] = fr
            carry_ref[1:2, st] = fi
            yp = _dot(xs_ref[...].astype(BF16), c_ref[j]) + d_ref[:, ch] * up

            def store(i, rows, ch=ch):
                y_ref[i * per:(i + 1) * per, ch] = rows
                yg_ref[i * per:(i + 1) * per, ch] = _gelu(rows).astype(BF16)

            _to_time_order(yp, perm_ref, store)

    u_specs = [pl.BlockSpec((t_rows, SSM_TILE_CH), lambda c, k=k: (c, u_off // SSM_TILE_CH + k)) for k in range(nj)]
    full3 = lambda shape: pl.BlockSpec(shape, lambda c: (0, 0, 0))
    full2 = lambda shape: pl.BlockSpec(shape, lambda c: (0, 0))
    rows = pl.BlockSpec((t_rows, w), lambda c: (c, 0))
    outs, carried = _call(
        body, name="ssm_fwd", grid=(s // t_rows,),
        in_specs=u_specs + [full3(bmat.shape), full3(cmat.shape), full2(lam.shape), full2(dskip.shape)],
        out_specs=[rows, rows, pl.BlockSpec((1, 2, ns), lambda c: (c, 0, 0))],
        out_shape=[jax.ShapeDtypeStruct((s, w), F32), jax.ShapeDtypeStruct((s, w), BF16),
                   jax.ShapeDtypeStruct((s // t_rows, 2, ns), F32)],
        scratch_shapes=[pltpu.VMEM((2, ns), F32), pltpu.VMEM((nj, t_rows, 2 * SSM_TILE_ST), F32),
                        pltpu.VMEM((t_rows, 2 * SSM_TILE_ST), F32), pltpu.VMEM((t_rows, SSM_TILE_CH), F32)],
        semantics=("arbitrary",), args=[*([z] * nj), bmat, cmat, lam, dskip], side=side)
    return outs if side is None else (outs, carried)


def _ssm_bwd(z, y, dyg, xin, bmat, cmat, lam, dskip, u_off, side=None):
    s, nj, w, ns, t_rows = _ssm_dims(z, bmat, u_off)
    nc = s // t_rows
    per = t_rows // SSM_SEGMENTS
    seg, half = SSM_SEGMENTS, SSM_TILE_ST

    def body(*refs):
        u_refs = refs[:nj]
        (y_ref, dyg_ref, xin_ref, b_ref, c_ref, lam_ref, d_ref, du_ref, db_ref, dc_ref, dlam_ref, dd_ref,
         carry_ref, w_ref, xs_ref, gs_ref, perm_ref, acc_ref) = refs[nj:]

        @pl.when(pl.program_id(0) == 0)
        def _():
            carry_ref[...] = jnp.zeros_like(carry_ref)
            db_ref[...] = jnp.zeros_like(db_ref)
            dc_ref[...] = jnp.zeros_like(dc_ref)
            dd_ref[...] = jnp.zeros_like(dd_ref)
            acc_ref[...] = jnp.zeros_like(acc_ref)
            _fill_powers(lam_ref, w_ref, nj, per)

        for j in range(nj):
            st = slice(j * SSM_TILE_ST, (j + 1) * SSM_TILE_ST)
            ch = slice(j * SSM_TILE_CH, (j + 1) * SSM_TILE_CH)
            lr, li = lam_ref[0:1, st], lam_ref[1:2, st]
            up = _to_segment_order(u_refs[j], perm_ref)
            upb = up.astype(BF16)
            dyp = _to_segment_order(dyg_ref[:, ch] * _gelu_grad(y_ref[:, ch]), perm_ref)
            dyb = dyp.astype(BF16)
            _, (in_r, in_i) = _segment_scan(_dot(upb, b_ref[j]), xs_ref, w_ref.at[j], lr, li,
                                            xin_ref[0, 0:1, st], xin_ref[0, 1:2, st], conj=False, reverse=False)
            (gr, gi), _ = _segment_scan(_dot_nt(dyb, c_ref[j]), gs_ref, w_ref.at[j], lr, li,
                                        carry_ref[0:1, st], carry_ref[1:2, st], conj=True, reverse=True)
            carry_ref[0:1, st] = gr
            carry_ref[1:2, st] = gi
            xs, gs = xs_ref[...], gs_ref[...]
            xsr, xsi, gsr, gsi = xs[:, :half], xs[:, half:], gs[:, :half], gs[:, half:]
            pxr = jnp.concatenate([in_r, xsr[:t_rows - seg]], axis=0)
            pxi = jnp.concatenate([in_i, xsi[:t_rows - seg]], axis=0)
            dl_r = gsr * pxr + gsi * pxi
            dl_i = gsi * pxr - gsr * pxi
            acc_ref[0, :, st] += jnp.sum(dl_r.reshape(per, seg, half), axis=0)
            acc_ref[1, :, st] += jnp.sum(dl_i.reshape(per, seg, half), axis=0)
            gx = gs.astype(BF16)
            dup = _dot_nt(gx, b_ref[j]) + d_ref[:, ch] * dyp

            def store(i, rows, ch=ch):
                du_ref[i * per:(i + 1) * per, ch] = rows.astype(BF16)

            _to_time_order(dup, perm_ref, store)
            db_ref[j] += _dot_tn(upb, gx)
            dc_ref[j] += _dot_tn(xs.astype(BF16), dyb)
            dd_ref[:, ch] += jnp.sum(dyp * up, axis=0, keepdims=True)

        @pl.when(pl.program_id(0) == nc - 1)
        def _():
            dlam_ref[...] = jnp.sum(acc_ref[...], axis=1)

    rev = lambda c: nc - 1 - c
    u_specs = [pl.BlockSpec((t_rows, SSM_TILE_CH), lambda c, k=k: (rev(c), u_off // SSM_TILE_CH + k))
               for k in range(nj)]
    full3 = lambda shape: pl.BlockSpec(shape, lambda c: (0, 0, 0))
    full2 = lambda shape: pl.BlockSpec(shape, lambda c: (0, 0))
    rows = pl.BlockSpec((t_rows, w), lambda c: (rev(c), 0))
    outs, carried = _call(
        body, name="ssm_bwd", grid=(nc,),
        in_specs=u_specs + [rows, rows, pl.BlockSpec((1, 2, ns), lambda c: (rev(c), 0, 0)),
                            full3(bmat.shape), full3(cmat.shape), full2(lam.shape), full2(dskip.shape)],
        out_specs=[rows, full3(bmat.shape), full3(cmat.shape), full2(lam.shape), full2(dskip.shape)],
        out_shape=[jax.ShapeDtypeStruct((s, w), BF16), jax.ShapeDtypeStruct(bmat.shape, F32),
                   jax.ShapeDtypeStruct(cmat.shape, F32), jax.ShapeDtypeStruct(lam.shape, F32),
                   jax.ShapeDtypeStruct(dskip.shape, F32)],
        scratch_shapes=[pltpu.VMEM((2, ns), F32), pltpu.VMEM((nj, t_rows, 2 * SSM_TILE_ST), F32),
                        pltpu.VMEM((t_rows, 2 * SSM_TILE_ST), F32), pltpu.VMEM((t_rows, 2 * SSM_TILE_ST), F32),
                        pltpu.VMEM((t_rows, SSM_TILE_CH), F32), pltpu.VMEM((2, SSM_SEGMENTS, ns), F32)],
        semantics=("arbitrary",), args=[*([z] * nj), y, dyg, xin, bmat, cmat, lam, dskip], side=side)
    return outs if side is None else (outs, carried)


def _adam_math(w, g, m, v):
    m = ADAM_B1 * m + (1.0 - ADAM_B1) * g
    v = ADAM_B2 * v + (1.0 - ADAM_B2) * (g * g)
    m_hat = m / (1.0 - ADAM_B1 ** ADAM_STEP)
    v_hat = v / (1.0 - ADAM_B2 ** ADAM_STEP)
    delta = -ADAM_LR * (m_hat / (jnp.sqrt(v_hat) + ADAM_EPS) + ADAM_WD * w)
    return delta, m, v


def _adam_rows(r, c):
    for tr in (512, 256, 128, 64, 32, 16, 8):
        if r % tr == 0 and tr * c * 4 <= (1 << 20):
            return tr
    return r


def _adamw_big(w, p_mine, p_sib, m, v, name):
    r, c = w.shape
    tr = _adam_rows(r, c)

    def body(w_ref, a_ref, b_ref, m_ref, v_ref, g_ref, d_ref, nm_ref, nv_ref):
        g = a_ref[...] + b_ref[...]
        g_ref[...] = g
        d_ref[...], nm_ref[...], nv_ref[...] = _adam_math(w_ref[...], g, m_ref[...], v_ref[...])

    blk = pl.BlockSpec((tr, c), lambda i: (i, 0))
    return pl.pallas_call(body, name=f"adamw_{name}", grid=(r // tr,), in_specs=[blk] * 5, out_specs=[blk] * 4,
                          out_shape=[jax.ShapeDtypeStruct((r, c), F32)] * 4,
                          compiler_params=_params(("parallel",)))(w, p_mine, p_sib, m, v)


def _adamw_small(w, parts, m, v):
    r, c = w.shape
    n_dev = parts.shape[0]

    def body(w_ref, p_ref, m_ref, v_ref, g_ref, d_ref, nm_ref, nv_ref):
        g = p_ref[0]
        for k in range(1, n_dev):
            g = g + p_ref[k]
        g_ref[...] = g
        d_ref[...], nm_ref[...], nv_ref[...] = _adam_math(w_ref[...], g, m_ref[...], v_ref[...])

    blk = pl.BlockSpec((r, c), lambda i: (0, 0))
    return pl.pallas_call(body, name="adamw_small", grid=(1,),
                          in_specs=[blk, pl.BlockSpec((n_dev, r, c), lambda i: (0, 0, 0)), blk, blk],
                          out_specs=[blk] * 4, out_shape=[jax.ShapeDtypeStruct((r, c), F32)] * 4,
                          compiler_params=_params(("arbitrary",)))(w, parts, m, v)


def _cast_bf16(w, name):
    r, c = w.shape
    tr = _adam_rows(r, c)

    def body(w_ref, o_ref):
        o_ref[...] = w_ref[...].astype(BF16)

    blk = pl.BlockSpec((tr, c), lambda i: (i, 0))
    return pl.pallas_call(body, name=f"cast_{name}", grid=(r // tr,), in_specs=[blk], out_specs=blk,
                          out_shape=jax.ShapeDtypeStruct((r, c), BF16), compiler_params=_params(("parallel",)))(w)


def _sum_slots(recv, name):
    _, r, c = recv.shape
    tr = _adam_rows(r, c)

    def body(p_ref, o_ref):
        acc = p_ref[0].astype(F32)
        for k in range(1, N_CHIPS):
            acc = acc + p_ref[k].astype(F32)
        o_ref[...] = acc

    return pl.pallas_call(body, name=f"sum_{name}", grid=(r // tr,),
                          in_specs=[pl.BlockSpec((N_CHIPS, tr, c), lambda i: (0, i, 0))],
                          out_specs=pl.BlockSpec((tr, c), lambda i: (i, 0)),
                          out_shape=jax.ShapeDtypeStruct((r, c), F32), compiler_params=_params(("parallel",)))(recv)


BIG_WEIGHTS = ("w_in", "w_attn_up", "w_glu_v", "w_glu_g", "w_out", "w_ffn_gate", "w_ffn_up", "w_ffn_down")
COL_SHARDED = ("w_in", "w_attn_up", "w_glu_v", "w_glu_g", "w_ffn_gate", "w_ffn_up")


def _aligned(v, m):
    return v if isinstance(v, int) else pl.multiple_of(v, m)


def _shard_of(ref, name, j, shard_shape, half=None):
    r, c = shard_shape
    rows = r if half is None else r // 2
    row0 = 0 if half is None else half * rows
    if name in COL_SHARDED:
        return ref.at[pl.ds(_aligned(row0, 16), rows), pl.ds(_aligned(j * c, 128), c)]
    return ref.at[pl.ds(_aligned(j * r + row0, 16), rows), :]


def _other_chips():
    x, y = lax.axis_index("x"), lax.axis_index("y")
    return [(1 - x, y), (x, 1 - y), (1 - x, 1 - y)]


def _dma_sems(n, arrays):
    return [pltpu.SemaphoreType.DMA((n, 3))] * arrays + [pltpu.SemaphoreType.DMA((n,))]


def _gather_side(shards):
    names = list(shards)
    n = len(names)
    full_shapes = []
    for k in names:
        r, c = shards[k].shape
        full_shapes.append((r, c * N_CHIPS) if k in COL_SHARDED else (r * N_CHIPS, c))

    def build(src, dst, sems):
        send_sems, recv_sems, pass_send_sems, pass_recv_sems, local_sems = sems
        x, y, c = lax.axis_index("x"), lax.axis_index("y"), lax.axis_index("c")
        me = 2 * x + y
        locals_, sends, arrivals, forwards, passed_on = [], [], [], [], []
        for i, k in enumerate(names):
            shape = shards[k].shape
            half_rows = shape[0] // 2
            locals_.append(pltpu.make_async_copy(src[i], _shard_of(dst[i], k, me, shape), local_sems.at[i]))
            my_half = src[i].at[pl.ds(_aligned(c * half_rows, 16), half_rows), :]
            for p, (px, py) in enumerate(_other_chips()):
                peer = 2 * px + py
                landed = _shard_of(dst[i], k, peer, shape, half=c)
                sends.append(pltpu.make_async_remote_copy(
                    src_ref=my_half, dst_ref=_shard_of(dst[i], k, me, shape, half=c), send_sem=send_sems.at[i, p],
                    recv_sem=recv_sems.at[i, p], device_id=(px, py, c), device_id_type=MESH))
                arrivals.append(pltpu.make_async_remote_copy(
                    src_ref=my_half, dst_ref=landed, send_sem=send_sems.at[i, p],
                    recv_sem=recv_sems.at[i, p], device_id=(px, py, c), device_id_type=MESH))
                forwards.append(pltpu.make_async_remote_copy(
                    src_ref=landed, dst_ref=landed, send_sem=pass_send_sems.at[i, p],
                    recv_sem=pass_recv_sems.at[i, p], device_id=(x, y, 1 - c), device_id_type=MESH))
                passed_on.append(pltpu.make_async_remote_copy(
                    src_ref=landed, dst_ref=_shard_of(dst[i], k, peer, shape, half=1 - c),
                    send_sem=pass_send_sems.at[i, p], recv_sem=pass_recv_sems.at[i, p],
                    device_id=(x, y, 1 - c), device_id_type=MESH))
        return locals_, sends, arrivals, forwards, passed_on

    return _Side([shards[k] for k in names], [jax.ShapeDtypeStruct(s, BF16) for s in full_shapes], _dma_sems(n, 4), build,
                 relays=True)


def _scatter_side(grads, shard_shapes):
    names = list(grads)
    n = len(names)

    def build(src, dst, sems):
        send_sems, recv_sems, local_sems = sems
        x, y, c = lax.axis_index("x"), lax.axis_index("y"), lax.axis_index("c")
        me = 2 * x + y
        locals_, sends, arrivals = [], [], []
        for i, k in enumerate(names):
            shape = shard_shapes[k]
            locals_.append(pltpu.make_async_copy(_shard_of(src[i], k, me, shape), dst[i].at[me], local_sems.at[i]))
            for p, (px, py) in enumerate(_other_chips()):
                peer = 2 * px + py
                sends.append(pltpu.make_async_remote_copy(
                    src_ref=_shard_of(src[i], k, peer, shape), dst_ref=dst[i].at[me], send_sem=send_sems.at[i, p],
                    recv_sem=recv_sems.at[i, p], device_id=(px, py, c), device_id_type=MESH))
                arrivals.append(pltpu.make_async_remote_copy(
                    src_ref=_shard_of(src[i], k, peer, shape), dst_ref=dst[i].at[peer], send_sem=send_sems.at[i, p],
                    recv_sem=recv_sems.at[i, p], device_id=(px, py, c), device_id_type=MESH))
        return locals_, sends, arrivals, [None] * len(arrivals), []

    return _Side([grads[k] for k in names],
                 [jax.ShapeDtypeStruct((N_CHIPS,) + tuple(shard_shapes[k]), BF16) for k in names], _dma_sems(n, 2), build)


def _put_cols(dz, src, col_off):
    s, w = src.shape
    tr = _pick(s, (2048, 1024, 512, 256, 128, 64, 8))
    tc = _pick(math.gcd(w, col_off), (1024, 512, 256, 128))
    off = col_off // tc

    def body(src_ref, dz_ref, o_ref):
        del dz_ref
        o_ref[...] = src_ref[...].astype(o_ref.dtype)

    return pl.pallas_call(
        body, name="put_cols", grid=(s // tr, w // tc),
        in_specs=[pl.BlockSpec((tr, tc), lambda i, j: (i, j)), pl.BlockSpec(memory_space=pl.ANY)],
        out_specs=pl.BlockSpec((tr, tc), lambda i, j: (i, off + j)),
        out_shape=jax.ShapeDtypeStruct(dz.shape, dz.dtype), input_output_aliases={1: 0},
        compiler_params=_params(("parallel", "parallel")))(src, dz)


def _swap_side(parts):
    n = len(parts)

    def build(src, dst, sems):
        send_sems, recv_sems = sems
        sibling = (lax.axis_index("x"), lax.axis_index("y"), 1 - lax.axis_index("c"))
        copies = [pltpu.make_async_remote_copy(src_ref=src[i], dst_ref=dst[i], send_sem=send_sems.at[i],
                                               recv_sem=recv_sems.at[i], device_id=sibling, device_id_type=MESH)
                  for i in range(n)]
        return [], copies, copies, [None] * n, []

    return _Side(parts, [jax.ShapeDtypeStruct(p.shape, F32) for p in parts],
                 [pltpu.SemaphoreType.DMA((n,)), pltpu.SemaphoreType.DMA((n,))], build)


def _share_side(packed):
    r, c = packed.shape

    def build(src, dst, sems):
        send_sems, recv_sems, local_sem = sems
        x, y, cc = lax.axis_index("x"), lax.axis_index("y"), lax.axis_index("c")
        me = 4 * x + 2 * y + cc
        own = pltpu.make_async_copy(src[0], dst[0].at[me], local_sem)
        sends, arrivals = [], []
        flips = [(fx, fy, fc) for fx in range(2) for fy in range(2) for fc in range(2) if fx or fy or fc]
        for p, (fx, fy, fc) in enumerate(flips):
            px, py, pc = x ^ fx, y ^ fy, cc ^ fc
            sends.append(pltpu.make_async_remote_copy(
                src_ref=src[0], dst_ref=dst[0].at[me], send_sem=send_sems.at[p], recv_sem=recv_sems.at[p],
                device_id=(px, py, pc), device_id_type=MESH))
            arrivals.append(pltpu.make_async_remote_copy(
                src_ref=src[0], dst_ref=dst[0].at[4 * px + 2 * py + pc], send_sem=send_sems.at[p],
                recv_sem=recv_sems.at[p], device_id=(px, py, pc), device_id_type=MESH))
        return [own], sends, arrivals, [None] * len(arrivals), []

    return _Side([packed], [jax.ShapeDtypeStruct((8, r, c), F32)],
                 [pltpu.SemaphoreType.DMA((7,)), pltpu.SemaphoreType.DMA((7,)), pltpu.SemaphoreType.DMA], build)


SMALL_WEIGHTS = ("norm_mix_pre", "ssm_a_re", "ssm_a_im", "ssm_log_dt", "ssm_b_re", "ssm_b_im", "ssm_c_re", "ssm_c_im",
                 "ssm_d", "norm_mix_post", "norm_ffn_pre", "norm_ffn_post")
WEIGHT_ORDER = ("norm_mix_pre", "w_in", "w_attn_up", "ssm_a_re", "ssm_a_im", "ssm_log_dt", "ssm_b_re", "ssm_b_im",
                "ssm_c_re", "ssm_c_im", "ssm_d", "w_glu_v", "w_glu_g", "w_out", "norm_mix_post", "norm_ffn_pre",
                "w_ffn_gate", "w_ffn_up", "w_ffn_down", "norm_ffn_post")
PACK_LANES = 128
PACK_ROWS = 8
PACK_GROUPS = (SMALL_WEIGHTS[:1], SMALL_WEIGHTS[1:])


def _pack_group(arrs, names):
    flat = jnp.concatenate([arrs[k].reshape(-1) for k in names])
    pad = -flat.shape[0] % (PACK_LANES * PACK_ROWS)
    return jnp.pad(flat, (0, pad)).reshape(-1, PACK_LANES)


def _pack_small(arrs):
    return jnp.concatenate([_pack_group(arrs, names) for names in PACK_GROUPS], axis=0)


def _unpack_small(packed, like):
    out, row = {}, 0
    for names in PACK_GROUPS:
        rows = _pack_group(like, names).shape[0]
        flat, pos = packed[row:row + rows].reshape(-1), 0
        for k in names:
            n = like[k].size
            out[k] = flat[pos:pos + n].reshape(like[k].shape)
            pos += n
        row += rows
    return out


def _local_step(x, target, big, small, shards=None, shard_shapes=None, h1=None):
    s, d = x.shape
    big, grads, slots = dict(big), {}, {}
    carry = shards is not None

    def gathering(names, call):
        if not carry:
            return call(None)
        res, got = call(_gather_side({k: shards[k] for k in names}))
        big.update(zip(names, got))
        return res

    def scattering(names, call):
        if not carry:
            return call(None)
        res, got = call(_scatter_side({k: grads[k] for k in names}, shard_shapes))
        slots.update(zip(names, got))
        return res

    u_off = 3 * HQ
    gate_off = u_off + d // 2
    g1, g2, g3, g4 = (small[k][0:1] for k in ("norm_mix_pre", "norm_mix_post", "norm_ffn_pre", "norm_ffn_post"))
    ssm_names = ("ssm_a_re", "ssm_a_im", "ssm_log_dt", "ssm_b_re", "ssm_b_im", "ssm_c_re", "ssm_c_im")
    (lam, bmat, cmat), ssm_vjp = jax.vjp(_ssm_prepare, *[small[k][0] for k in ssm_names])
    bmat, cmat = bmat.astype(BF16), cmat.astype(BF16)
    dskip = small["ssm_d"][0:1]

    if h1 is None:
        h1 = _norm_in(x, g1)
    z = gathering(("w_attn_up", "w_glu_v", "w_glu_g", "w_out", "w_ffn_gate"),
                  lambda side: _mm(h1, big["w_in"], "nn", F32, "in_proj", side=side))
    y, yg, xin = gathering(("w_ffn_up",), lambda side: _ssm_fwd(z, bmat, cmat, lam, dskip, u_off, side=side))
    qkv = [_dilate_qkv(z, g, dil) for g, dil in enumerate(ATTN_DILATIONS)]
    outs, lses = zip(*[_attn_fwd(qkv[g], g, dil) for g, dil in enumerate(ATTN_DILATIONS)])
    attn = _attn_merge(outs, lses)
    merged, ab, gv, gg = _mm_fused(
        [attn, yg], [big["w_attn_up"], big["w_glu_v"], big["w_glu_g"]], [(0, 0), (1, 1), (1, 2)], "nn",
        [BF16, BF16, BF16, BF16], "branches_merge", extras=[(z, gate_off), (z, gate_off + d)], epilogue=_gates_epilogue)
    mo = _mm(merged, big["w_out"], "nn", F32, "mix_out")
    x2, h2 = _norm_mid(x, mo, g2, g3)
    act, fg, fu = gathering(("w_ffn_down",), lambda side: _mm_fused(
        [h2], [big["w_ffn_gate"], big["w_ffn_up"]], [(0, 0), (0, 1)], "nn", [BF16, BF16, BF16], "ffn_up_act",
        epilogue=_swiglu_epilogue, side=side))
    f = _mm(act, big["w_ffn_down"], "nn", F32, "ffn_down")
    loss, dout, df, dg4 = _loss_head(x2, f, g4, target)

    grads["w_ffn_down"] = _mm_kloop(act, df, "tn", BF16, "dw_ffn_down")
    dfg, dfu = scattering(("w_ffn_down",), lambda side: _mm_fused(
        [df], [big["w_ffn_down"]], [(0, 0)], "nt", [BF16, BF16], "d_ffn_act", extras=[(fg, 0), (fu, 0)],
        epilogue=_swiglu_bwd_epilogue, side=side))
    grads["w_ffn_gate"] = _mm_kloop(h2, dfg, "tn", BF16, "dw_ffn_gate")
    dh2 = scattering(("w_ffn_gate",), lambda side: _mm_fused(
        [dfg, dfu], [big["w_ffn_gate"], big["w_ffn_up"]], [(0, 0), (1, 1)], "nt", [F32], "d_h2",
        epilogue=_sum_epilogue, side=side))[0]
    grads["w_ffn_up"] = _mm_kloop(h2, dfu, "tn", BF16, "dw_ffn_up")
    dx2, dmo, dg2, dg3 = _norm_mid_bwd(x2, mo, g2, g3, dout, dh2)
    dz, dgs, dab, dgv, dgg = _mm_fused(
        [dmo], [big["w_out"]], [(0, 0)], "nt", [BF16] * 5, "d_merged_gates",
        extras=[(z, gate_off), (z, gate_off + d), (ab, 0), (gv, 0), (gg, 0)], epilogue=_gates_bwd_epilogue,
        out_place=[(z.shape[1], gate_off), None, None, None, None])
    dz = _put_cols(dz, dgs, gate_off + d)
    grads["w_out"] = _mm_kloop(merged, dmo, "tn", BF16, "dw_out")
    dyg = _mm_fused([dgv, dgg], [big["w_glu_v"], big["w_glu_g"]], [(0, 0), (1, 1)], "nt", [F32], "d_yg",
                    epilogue=_sum_epilogue)[0]
    grads["w_glu_v"] = _mm_kloop(yg, dgv, "tn", BF16, "dw_glu_v")
    grads["w_glu_g"] = _mm_kloop(yg, dgg, "tn", BF16, "dw_glu_g")
    du, dbmat, dcmat, dlam, dd = scattering(
        ("w_ffn_up", "w_out", "w_glu_v", "w_glu_g"),
        lambda side: _ssm_bwd(z, y, dyg, xin, bmat, cmat, lam, dskip, u_off, side=side))
    dz = _put_cols(dz, du, u_off)
    dattn = _mm(dab, big["w_attn_up"], "nt", F32, "d_attn")
    grads["w_attn_up"] = _mm_kloop(attn, dab, "tn", BF16, "dw_attn_up")
    merged_bwd = _attn_merge_bwd(outs, lses, dattn)
    mine, theirs = {}, {}
    for g, dil in enumerate(ATTN_DILATIONS):
        side = None
        if carry and g == 0:
            mine = {k: _sum_slots(slots[k], k) for k in slots}
            side = _swap_side(list(mine.values()))
        dqkv = _attn_bwd(qkv[g], merged_bwd[g], lses[g], merged_bwd[3 + g], g, dil, side=side)
        if side is not None:
            dqkv, got = dqkv
            theirs = dict(zip(mine, got))
        dz = _undilate_dqkv(dqkv, dz, g, dil)
    small_grads = dict(zip(ssm_names, (t[None] for t in ssm_vjp((dlam, dbmat, dcmat)))))
    small_grads.update(norm_mix_post=dg2, norm_ffn_pre=dg3, norm_ffn_post=dg4, ssm_d=dd)
    if carry:
        side = _join_sides(_scatter_side({"w_attn_up": grads["w_attn_up"]}, shard_shapes),
                           _share_side(_pack_group(small_grads, PACK_GROUPS[1])))
        grads["w_in"], (slots["w_attn_up"], shared) = _mm_kloop(h1, dz, "tn", BF16, "dw_in", side=side)
    else:
        grads["w_in"] = _mm_kloop(h1, dz, "tn", BF16, "dw_in")
    dh1 = scattering(("w_in",), lambda side: _mm_kloop(dz, big["w_in"], "nt", F32, "d_h1", side=side))
    grad_x, dg1 = _norm_in_bwd(x, g1, dh1, dx2)
    small_grads["norm_mix_pre"] = dg1
    if carry:
        return loss[0, 0], grad_x, (slots, mine, theirs), (dg1, shared)
    return loss[0, 0], grad_x, grads, small_grads


def kernel(x, norm_mix_pre, w_in, w_attn_up, ssm_a_re, ssm_a_im, ssm_log_dt, ssm_b_re, ssm_b_im, ssm_c_re, ssm_c_im, ssm_d, w_glu_v, w_glu_g, w_out, norm_mix_post, norm_ffn_pre, w_ffn_gate, w_ffn_up, w_ffn_down, norm_ffn_post, loss_target, m_norm_mix_pre, m_w_in, m_w_attn_up, m_ssm_a_re, m_ssm_a_im, m_ssm_log_dt, m_ssm_b_re, m_ssm_b_im, m_ssm_c_re, m_ssm_c_im, m_ssm_d, m_w_glu_v, m_w_glu_g, m_w_out, m_norm_mix_post, m_norm_ffn_pre, m_w_ffn_gate, m_w_ffn_up, m_w_ffn_down, m_norm_ffn_post, v_norm_mix_pre, v_w_in, v_w_attn_up, v_ssm_a_re, v_ssm_a_im, v_ssm_log_dt, v_ssm_b_re, v_ssm_b_im, v_ssm_c_re, v_ssm_c_im, v_ssm_d, v_w_glu_v, v_w_glu_g, v_w_out, v_norm_mix_post, v_norm_ffn_pre, v_w_ffn_gate, v_w_ffn_up, v_w_ffn_down, v_norm_ffn_post):
    given = dict(locals())
    w = {k: given[k] for k in WEIGHT_ORDER}
    m = {k: given["m_" + k] for k in WEIGHT_ORDER}
    v = {k: given["v_" + k] for k in WEIGHT_ORDER}

    shard_shapes = {k: w[k].shape[1:] for k in BIG_WEIGHTS}
    shards = {"w_in": _cast_bf16(w["w_in"][0], "w_in")}
    h1, casts, got = _prologue(x[0], norm_mix_pre[0:1], {k: w[k][0] for k in BIG_WEIGHTS if k != "w_in"},
                               side=_gather_side({"w_in": shards["w_in"]}))
    shards.update(casts)
    big = {"w_in": got[0]}

    loss, grad_x, (slots, mine, theirs), small_grads = _local_step(
        x[0], loss_target[0], big, {k: w[k] for k in SMALL_WEIGHTS}, shards, shard_shapes, h1)
    loss = lax.psum(loss, MESH_AXES)

    last = [k for k in BIG_WEIGHTS if k not in mine]
    mine.update({k: _sum_slots(slots[k], k) for k in last})
    theirs.update(zip(last, _run_side(_swap_side([mine[k] for k in last]), "swap_last_grads")))
    out_g, out_d, out_m, out_v = {}, {}, {}, {}
    for k in BIG_WEIGHTS:
        res = _adamw_big(w[k][0], mine[k], theirs[k], m[k][0], v[k][0], k)
        out_g[k], out_d[k], out_m[k], out_v[k] = (t[None] for t in res)

    pick = lambda tree: {k: tree[k] for k in SMALL_WEIGHTS}
    dg1, shared = small_grads
    late = _run_side(_share_side(_pack_group({"norm_mix_pre": dg1}, PACK_GROUPS[0])), "share_last_grad")[0]
    parts = jnp.concatenate([late, shared], axis=1)
    res = _adamw_small(_pack_small(pick(w)), parts, _pack_small(pick(m)), _pack_small(pick(v)))
    for dst, packed in zip((out_g, out_d, out_m, out_v), res):
        dst.update(_unpack_small(packed, pick(w)))

    return (loss, grad_x[None], *[out_g[k] for k in WEIGHT_ORDER], *[out_d[k] for k in WEIGHT_ORDER],
            *[out_m[k] for k in WEIGHT_ORDER], *[out_v[k] for k in WEIGHT_ORDER])
```

```python
import functools
import math

import jax
import jax.numpy as jnp
from jax import lax
from jax.experimental import pallas as pl
from jax.experimental.pallas import tpu as pltpu

F32 = jnp.float32
BF16 = jnp.bfloat16

EPS = 1e-6
HEAD_DIM = 128
HEADS_PER_GROUP = 4
ATTN_DILATIONS = (1, 4, 16)
ATTN_BLK = 128
N_ATTN_HEADS = HEADS_PER_GROUP * len(ATTN_DILATIONS)
GROUP_W = HEADS_PER_GROUP * HEAD_DIM
HQ = N_ATTN_HEADS * HEAD_DIM
SSM_GROUP = 16
SSM_STATE = 64
SSM_TILE_CH = 128
SSM_TILE_ST = SSM_TILE_CH // SSM_GROUP * SSM_STATE
ADAM_LR = 0.001
ADAM_B1 = 0.9
ADAM_B2 = 0.999
ADAM_EPS = 1e-08
ADAM_WD = 0.01
ADAM_STEP = 10
NEG_BIG = -1e30
V7X_VMEM_LIMIT = 56 * 1024 * 1024
MESH_AXES = ("x", "y", "c")
N_CHIPS = 4


def _pick(n, cands):
    for c in cands:
        if n % c == 0:
            return c
    raise ValueError(f"no tile of {cands} divides {n}")


def _params(sem):
    return pltpu.CompilerParams(dimension_semantics=sem, vmem_limit_bytes=V7X_VMEM_LIMIT)


HBM = pl.BlockSpec(memory_space=pl.ANY)
MESH = pl.DeviceIdType.MESH


class _Side:
    def __init__(self, srcs, out_shapes, sem_shapes, build, aliases=None, relays=False):
        self.srcs, self.out_shapes, self.sem_shapes, self.build = list(srcs), list(out_shapes), list(sem_shapes), build
        self.aliases = dict(aliases or {})
        self.relays = relays

    def start(self, src, dst, sems):
        local, sends = self.build(src, dst, sems)[:2]
        for cp in local + sends:
            cp.start()

    def relay(self, src, dst, sems):
        _, _, arrivals, forwards, _ = self.build(src, dst, sems)
        for cp, forward in zip(arrivals, forwards):
            if forward is not None:
                cp.wait_recv()
                forward.start()

    def wait(self, src, dst, sems, relayed=False):
        local, sends, arrivals, forwards, passed_on = self.build(src, dst, sems)
        for cp, forward in zip(arrivals, forwards):
            if forward is None:
                cp.wait_recv()
            elif not relayed:
                cp.wait_recv()
                forward.start()
        for cp in passed_on:
            cp.wait_recv()
        for cp in sends + [f for f in forwards if f is not None]:
            cp.wait_send()
        for cp in local:
            cp.wait()


def _join_sides(a, b):
    ns, no, nm = len(a.srcs), len(a.out_shapes), len(a.sem_shapes)

    def build(src, dst, sems):
        ra, rb = a.build(src[:ns], dst[:no], sems[:nm]), b.build(src[ns:], dst[no:], sems[nm:])
        return tuple(p + q for p, q in zip(ra, rb))

    aliases = {**a.aliases, **{ns + k: no + v for k, v in b.aliases.items()}}
    return _Side(a.srcs + b.srcs, a.out_shapes + b.out_shapes, a.sem_shapes + b.sem_shapes, build, aliases,
                 a.relays or b.relays)


def _call(body, *, name, grid, in_specs, out_specs, out_shape, semantics, args, scratch_shapes=(), side=None, **kw):
    in_specs, out_specs, out_shape, scratch_shapes = list(in_specs), list(out_specs), list(out_shape), list(scratch_shapes)
    if side is None:
        res = pl.pallas_call(body, name=name, grid=grid, in_specs=in_specs, out_specs=out_specs, out_shape=out_shape,
                             scratch_shapes=scratch_shapes, compiler_params=_params(semantics), **kw)(*args)
        return list(res), []
    n_in, n_out, n_scr = len(in_specs), len(out_specs), len(scratch_shapes)
    ns_in, ns_out = len(side.srcs), len(side.out_shapes)
    n_steps = math.prod(grid)
    relay_at = (3 * n_steps) // 4 if side.relays and n_steps >= 4 else None

    def carrying(*refs):
        ins, s_in = refs[:n_in], refs[n_in:n_in + ns_in]
        o0 = n_in + ns_in
        outs, s_out = refs[o0:o0 + n_out], refs[o0 + n_out:o0 + n_out + ns_out]
        c0 = o0 + n_out + ns_out
        scr, sems = refs[c0:c0 + n_scr], refs[c0 + n_scr:]
        step = functools.reduce(lambda acc, ig: acc * ig[1] + pl.program_id(ig[0]), enumerate(grid), 0)

        @pl.when(step == 0)
        def _():
            side.start(s_in, s_out, sems)

        if relay_at is not None:
            @pl.when(step == relay_at)
            def _():
                side.relay(s_in, s_out, sems)

        body(*ins, *outs, *scr)

        @pl.when(step == n_steps - 1)
        def _():
            side.wait(s_in, s_out, sems, relayed=relay_at is not None)

    res = pl.pallas_call(
        carrying, name=name, grid=grid, in_specs=in_specs + [HBM] * ns_in, out_specs=out_specs + [HBM] * ns_out,
        out_shape=out_shape + side.out_shapes, scratch_shapes=scratch_shapes + side.sem_shapes,
        input_output_aliases={n_in + k: n_out + v for k, v in side.aliases.items()},
        compiler_params=pltpu.CompilerParams(dimension_semantics=("arbitrary",) * len(grid),
                                             vmem_limit_bytes=V7X_VMEM_LIMIT, has_side_effects=True), **kw,
    )(*args, *side.srcs)
    return list(res[:n_out]), list(res[n_out:])


def _run_side(side, name):
    ns, no = len(side.srcs), len(side.out_shapes)

    def body(*refs):
        src, dst, sems = refs[:ns], refs[ns:ns + no], refs[ns + no:]
        side.start(src, dst, sems)
        side.wait(src, dst, sems)

    return list(pl.pallas_call(body, name=name, in_specs=[HBM] * ns, out_specs=[HBM] * no, out_shape=side.out_shapes,
                               scratch_shapes=side.sem_shapes,
                               compiler_params=pltpu.CompilerParams(has_side_effects=True))(*side.srcs))


_DOT_DIMS = {"nn": (((1,), (0,)), ((), ())), "nt": (((1,), (1,)), ((), ())), "tn": (((0,), (0,)), ((), ()))}


MM_VMEM_BUDGET = 44 * 1024 * 1024
MM_STEP_BYTES = 1 << 20
MM_ACC_BYTES = 4
MM_EPILOGUE_COLS = 256


def _size(dtype):
    return jnp.dtype(dtype).itemsize


def _mm_fused(as_, bs, pairs, mode, out_dtypes, name, extras=(), epilogue=None, side=None, out_place=None):
    M = as_[0].shape[0]
    N = bs[0].shape[1] if mode == "nn" else bs[0].shape[0]
    ks_a = [a.shape[1] for a in as_]
    ks_b = [b.shape[0] if mode == "nn" else b.shape[1] for b in bs]
    chunked = epilogue is not None
    if epilogue is None:
        epilogue = lambda rs, es: rs
    offs = [off for _, off in extras]
    place = list(out_place) if out_place else [None] * len(out_dtypes)
    offs_all = offs + [p[1] for p in place if p is not None]
    best = None
    for tm in (2048, 1024, 512, 256, 128):
        for tn in (2048, 1024, 512, 256, 128):
            if M % tm or N % tn or any(off % tn for off in offs_all):
                continue
            vmem = (sum(2 * tm * k * 2 for k in ks_a) + sum(2 * k * tn * 2 for k in ks_b)
                    + sum(2 * tm * tn * _size(d) for d in out_dtypes) + sum(2 * tm * tn * _size(e.dtype) for e, _ in extras)
                    + len(pairs) * tm * tn * 4)
            cost = sum(k * N * 2 for k in ks_b) * (M // tm) + (M // tm) * (N // tn) * MM_STEP_BYTES
            if vmem <= MM_VMEM_BUDGET and (best is None or cost < best[0]):
                best = (cost, tm, tn)
    _, tm, tn = best
    na, nb, ne, no = len(as_), len(bs), len(extras), len(out_dtypes)
    dims = _DOT_DIMS[mode]

    sub = MM_EPILOGUE_COLS if chunked and tn % MM_EPILOGUE_COLS == 0 else tn

    def body(*refs):
        a_refs, b_refs = refs[:na], refs[na:na + nb]
        e_refs, o_refs = refs[na + nb:na + nb + ne], refs[na + nb + ne:]
        for c0 in range(0, tn, sub):
            cs = slice(c0, c0 + sub)
            rs = [lax.dot_general(a_refs[ai][...], b_refs[bi][:, cs] if mode == "nn" else b_refs[bi][cs, :], dims,
                                  preferred_element_type=F32) for ai, bi in pairs]
            outs = epilogue(rs, [e[:, cs] for e in e_refs])
            for o_ref, o in zip(o_refs, outs):
                o_ref[:, cs] = o.astype(o_ref.dtype)

    a_specs = [pl.BlockSpec((tm, k), lambda i, j: (i, 0)) for k in ks_a]
    if mode == "nn":
        b_specs = [pl.BlockSpec((k, tn), lambda i, j: (0, j)) for k in ks_b]
    else:
        b_specs = [pl.BlockSpec((tn, k), lambda i, j: (j, 0)) for k in ks_b]
    e_specs = [pl.BlockSpec((tm, tn), lambda i, j, o=off // tn: (i, o + j)) for off in offs]
    o_specs = [pl.BlockSpec((tm, tn), lambda i, j, o=(p[1] // tn if p else 0): (i, o + j)) for p in place]
    outs, carried = _call(
        body, name=name, grid=(M // tm, N // tn), in_specs=a_specs + b_specs + e_specs, out_specs=o_specs,
        out_shape=[jax.ShapeDtypeStruct((M, p[0] if p else N), d) for d, p in zip(out_dtypes, place)],
        semantics=("parallel", "arbitrary"),
        args=[*as_, *bs, *[e for e, _ in extras]], side=side)
    return outs if side is None else (outs, carried)


def _mm(a, b, mode, out_dtype, name, side=None):
    res = _mm_fused([a], [b], [(0, 0)], mode, [out_dtype], name, side=side)
    return res[0] if side is None else (res[0][0], res[1])


def _mm_kloop(a, b, mode, out_dtype, name, second=None, side=None):
    if mode == "nn":
        (M, K), (_, N) = a.shape, b.shape
    elif mode == "nt":
        (M, K), (N, _) = a.shape, b.shape
    else:
        (K, M), (_, N) = a.shape, b.shape
    products = 1 if second is None else 2
    best = None
    for tm in (2816, 2048, 1408, 1024, 512, 256, 128):
        for tn in (2816, 2432, 2048, 1408, 1024, 512, 256, 128):
            for tk in (2816, 2432, 2048, 1408, 1024, 512, 256, 128):
                if M % tm or N % tn or K % tk:
                    continue
                vmem = 2 * tm * tn * 4 + 2 * tm * tn * _size(out_dtype) + products * 2 * tk * (tm + tn) * 2
                steps = (M // tm) * (N // tn) * (K // tk)
                cost = (K * M * 2 * (N // tn) + K * N * 2 * (M // tm) + steps * MM_STEP_BYTES
                        + steps * tm * tn * MM_ACC_BYTES)
                if vmem <= MM_VMEM_BUDGET and (best is None or cost < best[0]):
                    best = (cost, tm, tn, tk)
    _, tm, tn, tk = best
    nk = K // tk
    dims = _DOT_DIMS[mode]

    def body(*refs):
        o_ref, acc_ref = refs[-2:]
        k = pl.program_id(2)

        @pl.when(k == 0)
        def _():
            acc_ref[...] = jnp.zeros_like(acc_ref)

        for p in range(products):
            @pl.when(jnp.logical_and(k >= p * nk, k < (p + 1) * nk))
            def _(p=p):
                acc_ref[...] += lax.dot_general(refs[2 * p][...], refs[2 * p + 1][...], dims, preferred_element_type=F32)

        @pl.when(k == products * nk - 1)
        def _():
            o_ref[...] = acc_ref[...].astype(o_ref.dtype)

    def a_spec(p):
        kk = lambda k: jnp.clip(k - p * nk, 0, nk - 1)
        if mode == "tn":
            return pl.BlockSpec((tk, tm), lambda i, j, k: (kk(k), i))
        return pl.BlockSpec((tm, tk), lambda i, j, k: (i, kk(k)))

    def b_spec(p):
        kk = lambda k: jnp.clip(k - p * nk, 0, nk - 1)
        if mode == "nt":
            return pl.BlockSpec((tn, tk), lambda i, j, k: (j, kk(k)))
        return pl.BlockSpec((tk, tn), lambda i, j, k: (kk(k), j))

    o_spec = pl.BlockSpec((tm, tn), lambda i, j, k: (i, j))
    operands = (a, b) + (tuple(second) if second is not None else ())
    outs, carried = _call(
        body, name=name, grid=(M // tm, N // tn, products * nk),
        in_specs=[spec(p) for p in range(products) for spec in (a_spec, b_spec)], out_specs=[o_spec],
        out_shape=[jax.ShapeDtypeStruct((M, N), out_dtype)], scratch_shapes=[pltpu.VMEM((tm, tn), F32)],
        semantics=("parallel", "parallel", "arbitrary"), args=operands, side=side)
    return outs[0] if side is None else (outs[0], carried)


def _sigmoid(v):
    return 0.5 * jnp.tanh(0.5 * v) + 0.5


_GELU_C = math.sqrt(2.0 / math.pi)


def _gelu(v):
    return 0.5 * v * (1.0 + jnp.tanh(_GELU_C * (v + 0.044715 * v * v * v)))


def _gelu_grad(v):
    t = jnp.tanh(_GELU_C * (v + 0.044715 * v * v * v))
    return 0.5 * (1.0 + t) + 0.5 * v * (1.0 - t * t) * _GELU_C * (1.0 + 3.0 * 0.044715 * v * v)


def _rms(v, gain):
    r = lax.rsqrt(jnp.mean(v * v, axis=-1, keepdims=True) + EPS)
    return v * r * gain


def _rms_bwd(v, gain, dy):
    r = lax.rsqrt(jnp.mean(v * v, axis=-1, keepdims=True) + EPS)
    a = dy * gain
    dv = r * a - v * (r * r * r) * jnp.mean(a * v, axis=-1, keepdims=True)
    return dv, dy * v * r


def _row_tile(s):
    return _pick(s, (256, 128, 64, 8))


def _norm_in(x, gain):
    s, d = x.shape
    tr = _row_tile(s)

    def body(x_ref, g_ref, h_ref):
        h_ref[...] = _rms(x_ref[...], g_ref[...]).astype(BF16)

    row = pl.BlockSpec((tr, d), lambda i: (i, 0))
    vec = pl.BlockSpec((1, d), lambda i: (0, 0))
    return pl.pallas_call(body, name="norm_in", grid=(s // tr,), in_specs=[row, vec], out_specs=row,
                          out_shape=jax.ShapeDtypeStruct((s, d), BF16), compiler_params=_params(("parallel",)))(x, gain)


def _prologue(x, gain, weights, side=None):
    s, d = x.shape
    tr = _row_tile(s)
    steps = s // tr
    names = list(weights)
    tiles = []
    for k in names:
        r, _ = weights[k].shape
        tiles.append(next(t for t in range(16, r + 1, 16) if r % t == 0 and r // t <= steps))

    def body(*refs):
        x_ref, g_ref = refs[:2]
        w_refs, h_ref, o_refs = refs[2:2 + len(names)], refs[2 + len(names)], refs[3 + len(names):]
        h_ref[...] = _rms(x_ref[...], g_ref[...]).astype(BF16)
        for w_ref, o_ref in zip(w_refs, o_refs):
            o_ref[...] = w_ref[...].astype(BF16)

    row = pl.BlockSpec((tr, d), lambda i: (i, 0))
    w_specs = [pl.BlockSpec((t, weights[k].shape[1]), lambda i, last=weights[k].shape[0] // t - 1: (jnp.minimum(i, last), 0))
               for k, t in zip(names, tiles)]
    outs, carried = _call(
        body, name="prologue", grid=(steps,), in_specs=[row, pl.BlockSpec((1, d), lambda i: (0, 0))] + w_specs,
        out_specs=[row] + w_specs,
        out_shape=[jax.ShapeDtypeStruct((s, d), BF16)] + [jax.ShapeDtypeStruct(weights[k].shape, BF16) for k in names],
        semantics=("arbitrary",), args=[x, gain] + [weights[k] for k in names], side=side)
    return outs[0], dict(zip(names, outs[1:])), carried


def _norm_mid(x, mo, g_post, g_pre):
    s, d = x.shape
    tr = _row_tile(s)

    def body(x_ref, mo_ref, g2_ref, g3_ref, x2_ref, h2_ref):
        x2 = x_ref[...] + _rms(mo_ref[...], g2_ref[...])
        x2_ref[...] = x2
        h2_ref[...] = _rms(x2, g3_ref[...]).astype(BF16)

    row = pl.BlockSpec((tr, d), lambda i: (i, 0))
    vec = pl.BlockSpec((1, d), lambda i: (0, 0))
    return pl.pallas_call(
        body, name="norm_mid", grid=(s // tr,), in_specs=[row, row, vec, vec], out_specs=[row, row],
        out_shape=[jax.ShapeDtypeStruct((s, d), F32), jax.ShapeDtypeStruct((s, d), BF16)],
        compiler_params=_params(("parallel",)))(x, mo, g_post, g_pre)


def _loss_head(x2, f, g_post, target):
    s, d = x2.shape
    tr = _row_tile(s)

    def body(x2_ref, f_ref, g_ref, t_ref, loss_ref, dout_ref, df_ref, dg_ref):
        @pl.when(pl.program_id(0) == 0)
        def _():
            loss_ref[...] = jnp.zeros_like(loss_ref)
            dg_ref[...] = jnp.zeros_like(dg_ref)

        fv = f_ref[...]
        g = g_ref[...]
        err = x2_ref[...] + _rms(fv, g) - t_ref[...]
        loss_ref[...] += 0.5 * jnp.sum(jnp.mean(err * err, axis=-1, keepdims=True), axis=0, keepdims=True)
        dout = err * (1.0 / d)
        dout_ref[...] = dout
        df, dg = _rms_bwd(fv, g, dout)
        df_ref[...] = df.astype(BF16)
        dg_ref[...] += jnp.sum(dg, axis=0, keepdims=True)

    row = pl.BlockSpec((tr, d), lambda i: (i, 0))
    vec = pl.BlockSpec((1, d), lambda i: (0, 0))
    one = pl.BlockSpec((1, 1), lambda i: (0, 0))
    return pl.pallas_call(
        body, name="loss_head", grid=(s // tr,), in_specs=[row, row, vec, row], out_specs=[one, row, row, vec],
        out_shape=[jax.ShapeDtypeStruct((1, 1), F32), jax.ShapeDtypeStruct((s, d), F32),
                   jax.ShapeDtypeStruct((s, d), BF16), jax.ShapeDtypeStruct((1, d), F32)],
        compiler_params=_params(("arbitrary",)))(x2, f, g_post, target)


def _norm_mid_bwd(x2, mo, g_post, g_pre, dout, dh2):
    s, d = x2.shape
    tr = _row_tile(s)

    def body(x2_ref, mo_ref, g2_ref, g3_ref, dout_ref, dh2_ref, dx2_ref, dmo_ref, dg2_ref, dg3_ref):
        @pl.when(pl.program_id(0) == 0)
        def _():
            dg2_ref[...] = jnp.zeros_like(dg2_ref)
            dg3_ref[...] = jnp.zeros_like(dg3_ref)

        dv, dg3 = _rms_bwd(x2_ref[...], g3_ref[...], dh2_ref[...])
        dx2 = dout_ref[...] + dv
        dx2_ref[...] = dx2
        dmo, dg2 = _rms_bwd(mo_ref[...], g2_ref[...], dx2)
        dmo_ref[...] = dmo.astype(BF16)
        dg2_ref[...] += jnp.sum(dg2, axis=0, keepdims=True)
        dg3_ref[...] += jnp.sum(dg3, axis=0, keepdims=True)

    row = pl.BlockSpec((tr, d), lambda i: (i, 0))
    vec = pl.BlockSpec((1, d), lambda i: (0, 0))
    return pl.pallas_call(
        body, name="norm_mid_bwd", grid=(s // tr,), in_specs=[row, row, vec, vec, row, row],
        out_specs=[row, row, vec, vec],
        out_shape=[jax.ShapeDtypeStruct((s, d), F32), jax.ShapeDtypeStruct((s, d), BF16),
                   jax.ShapeDtypeStruct((1, d), F32), jax.ShapeDtypeStruct((1, d), F32)],
        compiler_params=_params(("arbitrary",)))(x2, mo, g_post, g_pre, dout, dh2)


def _norm_in_bwd(x, gain, dh, dx2):
    s, d = x.shape
    tr = _row_tile(s)

    def body(x_ref, g_ref, dh_ref, dx2_ref, dx_ref, dg_ref):
        @pl.when(pl.program_id(0) == 0)
        def _():
            dg_ref[...] = jnp.zeros_like(dg_ref)

        dv, dg = _rms_bwd(x_ref[...], g_ref[...], dh_ref[...])
        dx_ref[...] = dx2_ref[...] + dv
        dg_ref[...] += jnp.sum(dg, axis=0, keepdims=True)

    row = pl.BlockSpec((tr, d), lambda i: (i, 0))
    vec = pl.BlockSpec((1, d), lambda i: (0, 0))
    return pl.pallas_call(
        body, name="norm_in_bwd", grid=(s // tr,), in_specs=[row, vec, row, row], out_specs=[row, vec],
        out_shape=[jax.ShapeDtypeStruct((s, d), F32), jax.ShapeDtypeStruct((1, d), F32)],
        compiler_params=_params(("arbitrary",)))(x, gain, dh, dx2)


def _swiglu_epilogue(rs, es):
    g, u = rs
    return [g * _sigmoid(g) * u, g, u]


def _swiglu_bwd_epilogue(rs, es):
    d = rs[0]
    g, u = es[0].astype(F32), es[1].astype(F32)
    sg = _sigmoid(g)
    return [d * u * sg * (1.0 + g * (1.0 - sg)), d * g * sg]


def _sum_epilogue(rs, es):
    return [rs[0] + rs[1]]


def _gates_epilogue(rs, es):
    ab, gv, gg = rs
    ga, gs = es
    return [_sigmoid(ga) * ab + _sigmoid(gs) * gv * _sigmoid(gg), ab, gv, gg]


def _gates_bwd_epilogue(rs, es):
    dm = rs[0]
    ga, gs, ab, gv, gg = (e.astype(F32) for e in es)
    sa, ss, sg = _sigmoid(ga), _sigmoid(gs), _sigmoid(gg)
    dsb = dm * ss
    return [dm * ab * sa * (1.0 - sa), dm * gv * sg * ss * (1.0 - ss), dm * sa, dsb * sg, dsb * gv * sg * (1.0 - sg)]


ATTN_ROWS = 2048


def _dilate_qkv(z, g, d):
    s = z.shape[0]
    tm = ATTN_ROWS
    per = tm // d
    nh = HEADS_PER_GROUP

    def body(z_ref, o_ref):
        for r in range(d):
            rows = z_ref[...] if d == 1 else z_ref[pl.ds(r, per, stride=d), :]
            o_ref[0, r] = rows.astype(BF16)

    return pl.pallas_call(
        body, name=f"dilate_qkv_{g}", grid=(s // tm, 3, nh),
        in_specs=[pl.BlockSpec((tm, HEAD_DIM), lambda i, w, h: (i, (3 * w + g) * nh + h))],
        out_specs=pl.BlockSpec((1, d, per, HEAD_DIM), lambda i, w, h: (w, 0, i, h)),
        out_shape=jax.ShapeDtypeStruct((3, d, s // d, GROUP_W), BF16),
        compiler_params=_params(("parallel", "parallel", "parallel")))(z)


def _undilate_dqkv(dqkv, dz, g, d):
    s = dz.shape[0]
    tm = ATTN_ROWS
    per = tm // d
    nh = HEADS_PER_GROUP

    def body(i_ref, dz_ref, o_ref, nat_ref):
        del dz_ref
        if d == 1:
            o_ref[...] = i_ref[0, 0]
        else:
            for r in range(d):
                nat_ref[pl.ds(r, per, stride=d), :] = i_ref[0, r].astype(F32)
            o_ref[...] = nat_ref[...].astype(BF16)

    return pl.pallas_call(
        body, name=f"undilate_dqkv_{g}", grid=(s // tm, 3, nh),
        in_specs=[pl.BlockSpec((1, d, per, HEAD_DIM), lambda i, w, h: (w, 0, i, h)),
                  pl.BlockSpec(memory_space=pl.ANY)],
        out_specs=pl.BlockSpec((tm, HEAD_DIM), lambda i, w, h: (i, (3 * w + g) * nh + h)),
        out_shape=jax.ShapeDtypeStruct(dz.shape, dz.dtype), input_output_aliases={1: 0},
        scratch_shapes=[pltpu.VMEM((tm, HEAD_DIM), F32)],
        compiler_params=_params(("parallel", "parallel", "parallel")))(dqkv, dz)


def _alibi_slope(head):
    return 2.0 ** (-8.0 * (head + 1) / N_ATTN_HEADS)


def _dot_nt(a, b):
    return lax.dot_general(a, b, _DOT_DIMS["nt"], preferred_element_type=F32)


def _dot_tn(a, b):
    return lax.dot_general(a, b, _DOT_DIMS["tn"], preferred_element_type=F32)


def _dot(a, b):
    return jnp.dot(a, b, preferred_element_type=F32)


GROUP_ROWS = HEADS_PER_GROUP * ATTN_BLK


def _band_bias(g, d, pairs):
    qi = jnp.arange(ATTN_BLK)[:, None]
    ki = jnp.arange(ATTN_BLK)[None, :]
    rows = []
    for hh in range(HEADS_PER_GROUP):
        slope_d = _alibi_slope(g * HEADS_PER_GROUP + hh) * d
        tiles = []
        for kind in pairs:
            dist = qi - ki if kind == "cur" else ATTN_BLK + qi - ki
            ok = dist >= 0 if kind == "cur" else dist <= ATTN_BLK
            tiles.append(jnp.where(ok, -slope_d * dist.astype(F32), NEG_BIG))
        rows.append(jnp.concatenate(tiles, axis=1))
    return jnp.concatenate(rows, axis=0).astype(F32)


def _tile_cols(t):
    return slice(t * ATTN_BLK, (t + 1) * ATTN_BLK)


def _attn_fwd(qkv, g, d):
    _, _, L, _ = qkv.shape
    nb = L // ATTN_BLK
    scale = HEAD_DIM ** -0.5

    def body(q_ref, kc_ref, kp_ref, vc_ref, vp_ref, bias_ref, o_ref, lse_ref, s_ref, p_ref):
        n = pl.program_id(1)
        for hh in range(HEADS_PER_GROUP):
            cols, rows = _tile_cols(hh), _tile_cols(hh)
            q = q_ref[0, 0, :, cols]
            s_ref[rows, _tile_cols(0)] = _dot_nt(q, kp_ref[0, 0, :, cols])
            s_ref[rows, _tile_cols(1)] = _dot_nt(q, kc_ref[0, 0, :, cols])
        col = lax.broadcasted_iota(jnp.int32, (GROUP_ROWS, 2 * ATTN_BLK), 1)
        s = s_ref[...] * scale + bias_ref[...]
        s = jnp.where(jnp.logical_and(col < ATTN_BLK, n == 0), NEG_BIG, s)
        m = jnp.max(s, axis=-1, keepdims=True)
        e = jnp.exp(s - m)
        l = jnp.sum(e, axis=-1, keepdims=True)
        p_ref[...] = (e * (1.0 / l)).astype(BF16)
        lse = m + jnp.log(l)
        for hh in range(HEADS_PER_GROUP):
            cols, rows = _tile_cols(hh), _tile_cols(hh)
            o_ref[0, :, cols] = (_dot(p_ref[rows, _tile_cols(0)], vp_ref[0, 0, :, cols])
                                 + _dot(p_ref[rows, _tile_cols(1)], vc_ref[0, 0, :, cols]))
            lse_ref[0, :, cols] = jnp.broadcast_to(lse[rows], (ATTN_BLK, HEAD_DIM))

    def spec(w, shift):
        return pl.BlockSpec((1, 1, ATTN_BLK, GROUP_W), lambda r, n: (w, r, jnp.maximum(n + shift, 0), 0))

    out = pl.BlockSpec((1, ATTN_BLK, GROUP_W), lambda r, n: (r, n, 0))
    bias = _band_bias(g, d, ("prev", "cur"))
    return pl.pallas_call(
        body, name=f"attn_fwd_{g}", grid=(d, nb),
        in_specs=[spec(0, 0), spec(1, 0), spec(1, -1), spec(2, 0), spec(2, -1),
                  pl.BlockSpec(bias.shape, lambda r, n: (0, 0))],
        out_specs=[out, out], out_shape=[jax.ShapeDtypeStruct((d, L, GROUP_W), F32)] * 2,
        scratch_shapes=[pltpu.VMEM((GROUP_ROWS, 2 * ATTN_BLK), F32), pltpu.VMEM((GROUP_ROWS, 2 * ATTN_BLK), BF16)],
        compiler_params=_params(("parallel", "parallel")))(qkv, qkv, qkv, qkv, qkv, bias)


def _attn_bwd(qkv, do, lse, cc, g, d, side=None):
    _, _, L, _ = qkv.shape
    nb = L // ATTN_BLK
    scale = HEAD_DIM ** -0.5
    a_, b_, c_ = _tile_cols(0), _tile_cols(1), _tile_cols(2)

    def body(q0_ref, q1_ref, k0_ref, kp_ref, v0_ref, vp_ref, do0_ref, do1_ref, l0_ref, l1_ref, c0_ref, c1_ref,
             bias_ref, o_ref, s_ref, dp_ref, l_ref, c_ref, p_ref, ds_ref):
        n = pl.program_id(1)
        for hh in range(HEADS_PER_GROUP):
            cols, rows = _tile_cols(hh), _tile_cols(hh)
            q0, q1 = q0_ref[0, 0, :, cols], q1_ref[0, 0, :, cols]
            k0, kp = k0_ref[0, 0, :, cols], kp_ref[0, 0, :, cols]
            v0, vp = v0_ref[0, 0, :, cols], vp_ref[0, 0, :, cols]
            do0, do1 = do0_ref[0, :, cols], do1_ref[0, :, cols]
            s_ref[rows, a_], s_ref[rows, b_], s_ref[rows, c_] = _dot_nt(q0, k0), _dot_nt(q0, kp), _dot_nt(q1, k0)
            dp_ref[rows, a_], dp_ref[rows, b_], dp_ref[rows, c_] = _dot_nt(do0, v0), _dot_nt(do0, vp), _dot_nt(do1, v0)
            l_ref[rows, a_], l_ref[rows, b_], l_ref[rows, c_] = l0_ref[0, :, cols], l0_ref[0, :, cols], l1_ref[0, :, cols]
            c_ref[rows, a_], c_ref[rows, b_], c_ref[rows, c_] = c0_ref[0, :, cols], c0_ref[0, :, cols], c1_ref[0, :, cols]
        col = lax.broadcasted_iota(jnp.int32, (GROUP_ROWS, 3 * ATTN_BLK), 1)
        tile = col // ATTN_BLK
        gone = jnp.logical_or(jnp.logical_and(tile == 1, n == 0), jnp.logical_and(tile == 2, n == nb - 1))
        s = jnp.where(gone, NEG_BIG, s_ref[...] * scale + bias_ref[...])
        p = jnp.exp(s - l_ref[...])
        p_ref[...] = p.astype(BF16)
        ds_ref[...] = (p * (dp_ref[...] + c_ref[...])).astype(BF16)
        for hh in range(HEADS_PER_GROUP):
            cols, rows = _tile_cols(hh), _tile_cols(hh)
            q0, q1 = q0_ref[0, 0, :, cols], q1_ref[0, 0, :, cols]
            k0, kp = k0_ref[0, 0, :, cols], kp_ref[0, 0, :, cols]
            do0, do1 = do0_ref[0, :, cols], do1_ref[0, :, cols]
            o_ref[0, 0, :, cols] = ((_dot(ds_ref[rows, a_], k0) + _dot(ds_ref[rows, b_], kp)) * scale).astype(BF16)
            o_ref[1, 0, :, cols] = ((_dot_tn(ds_ref[rows, a_], q0) + _dot_tn(ds_ref[rows, c_], q1)) * scale).astype(BF16)
            o_ref[2, 0, :, cols] = (_dot_tn(p_ref[rows, a_], do0) + _dot_tn(p_ref[rows, c_], do1)).astype(BF16)

    def spec(w, shift):
        return pl.BlockSpec((1, 1, ATTN_BLK, GROUP_W), lambda r, n: (w, r, jnp.clip(n + shift, 0, nb - 1), 0))

    def spec3(shift):
        return pl.BlockSpec((1, ATTN_BLK, GROUP_W), lambda r, n: (r, jnp.clip(n + shift, 0, nb - 1), 0))

    bias = _band_bias(g, d, ("cur", "prev", "prev"))
    wide = (GROUP_ROWS, 3 * ATTN_BLK)
    outs, carried = _call(
        body, name=f"attn_bwd_{g}", grid=(d, nb),
        in_specs=[spec(0, 0), spec(0, 1), spec(1, 0), spec(1, -1), spec(2, 0), spec(2, -1),
                  spec3(0), spec3(1), spec3(0), spec3(1), spec3(0), spec3(1), pl.BlockSpec(wide, lambda r, n: (0, 0))],
        out_specs=[pl.BlockSpec((3, 1, ATTN_BLK, GROUP_W), lambda r, n: (0, r, n, 0))],
        out_shape=[jax.ShapeDtypeStruct((3, d, L, GROUP_W), BF16)],
        scratch_shapes=[pltpu.VMEM(wide, F32)] * 4 + [pltpu.VMEM(wide, BF16)] * 2, semantics=("parallel", "parallel"),
        args=[qkv, qkv, qkv, qkv, qkv, qkv, do, do, lse, lse, cc, cc, bias], side=side)
    return outs[0] if side is None else (outs[0], carried)


def _load_natural(refs, nat_refs):
    for g, d in enumerate(ATTN_DILATIONS):
        if d == 1:
            nat_refs[g][...] = refs[g][0]
        else:
            per = ATTN_ROWS // d
            for r in range(d):
                nat_refs[g][pl.ds(r, per, stride=d), :] = refs[g][r]


def _mix_weights(lse_nat):
    l0, l1, l2 = lse_nat[0][...], lse_nat[1][...], lse_nat[2][...]
    m = jnp.maximum(jnp.maximum(l0, l1), l2)
    e0, e1, e2 = jnp.exp(l0 - m), jnp.exp(l1 - m), jnp.exp(l2 - m)
    inv = 1.0 / (e0 + e1 + e2)
    return e0 * inv, e1 * inv, e2 * inv


def _dilated_specs(s):
    return [pl.BlockSpec((d, ATTN_ROWS // d, HEAD_DIM), lambda i, h: (0, i, h)) for d in ATTN_DILATIONS]


NATURAL_SCRATCH = [pltpu.VMEM((ATTN_ROWS, HEAD_DIM), F32)] * (2 * len(ATTN_DILATIONS))


def _attn_merge(outs, lses):
    s = outs[0].shape[0] * outs[0].shape[1]

    def body(o0, o1, o2, l0, l1, l2, a_ref, *nat):
        onat, lnat = nat[:3], nat[3:]
        _load_natural((o0, o1, o2), onat)
        _load_natural((l0, l1, l2), lnat)
        w0, w1, w2 = _mix_weights(lnat)
        a_ref[...] = (w0 * onat[0][...] + w1 * onat[1][...] + w2 * onat[2][...]).astype(BF16)

    return pl.pallas_call(
        body, name="attn_merge", grid=(s // ATTN_ROWS, HEADS_PER_GROUP), in_specs=_dilated_specs(s) * 2,
        out_specs=pl.BlockSpec((ATTN_ROWS, HEAD_DIM), lambda i, h: (i, h)),
        out_shape=jax.ShapeDtypeStruct((s, GROUP_W), BF16), scratch_shapes=NATURAL_SCRATCH,
        compiler_params=_params(("parallel", "parallel")))(*outs, *lses)


def _attn_merge_bwd(outs, lses, dattn):
    s = dattn.shape[0]

    def body(o0, o1, o2, l0, l1, l2, da_ref, do0, do1, do2, c0, c1, c2, *nat):
        onat, lnat = nat[:3], nat[3:]
        _load_natural((o0, o1, o2), onat)
        _load_natural((l0, l1, l2), lnat)
        ws = _mix_weights(lnat)
        da = da_ref[...]
        attn = ws[0] * onat[0][...] + ws[1] * onat[1][...] + ws[2] * onat[2][...]
        tot = jnp.broadcast_to(jnp.sum(da * attn, axis=-1, keepdims=True), (ATTN_ROWS, HEAD_DIM))
        for g, (d, do_ref, c_ref) in enumerate(zip(ATTN_DILATIONS, (do0, do1, do2), (c0, c1, c2))):
            if d == 1:
                do_ref[0] = (ws[g] * da).astype(BF16)
                c_ref[0] = -ws[g] * tot
            else:
                onat[g][...] = ws[g] * da
                lnat[g][...] = -ws[g] * tot
                per = ATTN_ROWS // d
                for r in range(d):
                    do_ref[r] = onat[g][pl.ds(r, per, stride=d), :].astype(BF16)
                    c_ref[r] = lnat[g][pl.ds(r, per, stride=d), :]

    dil = _dilated_specs(s)
    shapes = [jax.ShapeDtypeStruct(o.shape, BF16) for o in outs] + [jax.ShapeDtypeStruct(o.shape, F32) for o in outs]
    return pl.pallas_call(
        body, name="attn_merge_bwd", grid=(s // ATTN_ROWS, HEADS_PER_GROUP),
        in_specs=dil * 2 + [pl.BlockSpec((ATTN_ROWS, HEAD_DIM), lambda i, h: (i, h))], out_specs=dil * 2,
        out_shape=shapes, scratch_shapes=NATURAL_SCRATCH,
        compiler_params=_params(("parallel", "parallel")))(*outs, *lses, dattn)


def _ssm_prepare(a_re, a_im, log_dt, b_re, b_im, c_re, c_im):
    n_g = a_re.shape[0]
    nj = n_g * SSM_GROUP // SSM_TILE_CH
    gpt = SSM_TILE_CH // SSM_GROUP
    dt = jnp.exp(log_dt)[:, None]
    mag = jnp.exp(a_re * dt)
    lr, li = mag * jnp.cos(a_im * dt), mag * jnp.sin(a_im * dt)
    den = a_re * a_re + a_im * a_im
    cr = ((lr - 1.0) * a_re + li * a_im) / den
    ci = (li * a_re - (lr - 1.0) * a_im) / den
    bb_re = cr[..., None] * b_re - ci[..., None] * b_im
    bb_im = cr[..., None] * b_im + ci[..., None] * b_re
    eye = jnp.eye(gpt, dtype=F32)

    def b_tiles(t):
        t = t.transpose(0, 2, 1).reshape(nj, gpt, SSM_GROUP, SSM_STATE)
        return jnp.einsum("jgcp,gh->jgchp", t, eye).reshape(nj, SSM_TILE_CH, SSM_TILE_ST)

    def c_tiles(t):
        t = t.reshape(nj, gpt, SSM_GROUP, SSM_STATE)
        return jnp.einsum("jgcp,gh->jhpgc", t, eye).reshape(nj, SSM_TILE_ST, SSM_TILE_CH)

    lam = jnp.stack([lr.reshape(-1), li.reshape(-1)])
    bmat = jnp.concatenate([b_tiles(bb_re), b_tiles(bb_im)], axis=2)
    cmat = jnp.concatenate([c_tiles(c_re), -c_tiles(c_im)], axis=1)
    return lam, bmat, cmat


SSM_SEGMENTS = 8


def _to_segment_order(nat, perm_ref):
    per = nat.shape[0] // SSM_SEGMENTS
    for i in range(SSM_SEGMENTS):
        perm_ref[pl.ds(i, per, stride=SSM_SEGMENTS), :] = nat[i * per:(i + 1) * per, :]
    return perm_ref[...]


def _to_time_order(val, perm_ref, store):
    per = val.shape[0] // SSM_SEGMENTS
    perm_ref[...] = val
    for i in range(SSM_SEGMENTS):
        store(i, perm_ref[pl.ds(i, per, stride=SSM_SEGMENTS), :])


def _fill_powers(lam_ref, w_ref, nj, tau_n):
    for j in range(nj):
        st = slice(j * SSM_TILE_ST, (j + 1) * SSM_TILE_ST)
        lr = jnp.broadcast_to(lam_ref[0:1, st], (SSM_SEGMENTS, SSM_TILE_ST))
        li = jnp.broadcast_to(lam_ref[1:2, st], (SSM_SEGMENTS, SSM_TILE_ST))
        wr, wi = lr, li
        for tau in range(tau_n):
            rows = slice(tau * SSM_SEGMENTS, (tau + 1) * SSM_SEGMENTS)
            w_ref[j, rows, :SSM_TILE_ST] = wr
            w_ref[j, rows, SSM_TILE_ST:] = wi
            wr, wi = wr * lr - wi * li, wr * li + wi * lr


def _segment_scan(src, xs_ref, w_tile, lr, li, cr, ci, conj, reverse):
    seg, half = SSM_SEGMENTS, SSM_TILE_ST
    tau_n = src.shape[0] // seg
    sgn = -1.0 if conj else 1.0
    lr8 = jnp.broadcast_to(lr, (seg, half))
    li8 = jnp.broadcast_to(li, (seg, half)) * sgn
    xr = jnp.zeros((seg, half), F32)
    xi = jnp.zeros((seg, half), F32)
    order = range(tau_n - 1, -1, -1) if reverse else range(tau_n)
    for tau in order:
        rows = slice(tau * seg, (tau + 1) * seg)
        xr, xi = lr8 * xr - li8 * xi + src[rows, :half], lr8 * xi + li8 * xr + src[rows, half:]
        xs_ref[rows, :half] = xr
        xs_ref[rows, half:] = xi
    pr = w_tile[(tau_n - 1) * seg:(tau_n - 1) * seg + 1, :half]
    pi = w_tile[(tau_n - 1) * seg:(tau_n - 1) * seg + 1, half:] * sgn
    fr, fi = cr, ci
    ins_r, ins_i = [None] * seg, [None] * seg
    runs = range(seg - 1, -1, -1) if reverse else range(seg)
    for i in runs:
        ins_r[i], ins_i[i] = fr, fi
        fr, fi = xr[i:i + 1, :] + pr * fr - pi * fi, xi[i:i + 1, :] + pr * fi + pi * fr
    in_r = jnp.concatenate(ins_r, axis=0)
    in_i = jnp.concatenate(ins_i, axis=0)
    for tau in range(tau_n):
        rows = slice(tau * seg, (tau + 1) * seg)
        wrow = (tau_n - 1 - tau) if reverse else tau
        wr = w_tile[wrow * seg:(wrow + 1) * seg, :half]
        wi = w_tile[wrow * seg:(wrow + 1) * seg, half:] * sgn
        xs_ref[rows, :half] += wr * in_r - wi * in_i
        xs_ref[rows, half:] += wr * in_i + wi * in_r
    return (fr, fi), (in_r, in_i)


def _ssm_dims(z, bmat, u_off):
    s = z.shape[0]
    nj = bmat.shape[0]
    t_rows = _pick(s, (256, 128))
    return s, nj, nj * SSM_TILE_CH, nj * SSM_TILE_ST, t_rows


def _ssm_fwd(z, bmat, cmat, lam, dskip, u_off, side=None):
    s, nj, w, ns, t_rows = _ssm_dims(z, bmat, u_off)
    per = t_rows // SSM_SEGMENTS

    def body(*refs):
        u_refs = refs[:nj]
        b_ref, c_ref, lam_ref, d_ref, y_ref, yg_ref, xin_ref, xall_ref, carry_ref, w_ref, xs_ref, perm_ref = refs[nj:]

        @pl.when(pl.program_id(0) == 0)
        def _():
            carry_ref[...] = jnp.zeros_like(carry_ref)
            _fill_powers(lam_ref, w_ref, nj, per)

        xin_ref[0] = carry_ref[...]
        for j in range(nj):
            st = slice(j * SSM_TILE_ST, (j + 1) * SSM_TILE_ST)
            ch = slice(j * SSM_TILE_CH, (j + 1) * SSM_TILE_CH)
            up = _to_segment_order(u_refs[j], perm_ref)
            bu = _dot(up.astype(BF16), b_ref[j])
            (fr, fi), _ = _segment_scan(bu, xs_ref, w_ref.at[j], lam_ref[0:1, st], lam_ref[1:2, st],
                                        carry_ref[0:1, st], carry_ref[1:2, st], conj=False, reverse=False)
            carry_ref[0:1, st] = fr
            carry_ref[1:2, st] = fi
            xs = xs_ref[...].astype(BF16)
            xall_ref[:, j * 2 * SSM_TILE_ST:(j + 1) * 2 * SSM_TILE_ST] = xs
            yp = _dot(xs, c_ref[j]) + d_ref[:, ch] * up

            def store(i, rows, ch=ch):
                y_ref[i * per:(i + 1) * per, ch] = rows
                yg_ref[i * per:(i + 1) * per, ch] = _gelu(rows).astype(BF16)

            _to_time_order(yp, perm_ref, store)

    u_specs = [pl.BlockSpec((t_rows, SSM_TILE_CH), lambda c, k=k: (c, u_off // SSM_TILE_CH + k)) for k in range(nj)]
    full3 = lambda shape: pl.BlockSpec(shape, lambda c: (0, 0, 0))
    full2 = lambda shape: pl.BlockSpec(shape, lambda c: (0, 0))
    rows = pl.BlockSpec((t_rows, w), lambda c: (c, 0))
    outs, carried = _call(
        body, name="ssm_fwd", grid=(s // t_rows,),
        in_specs=u_specs + [full3(bmat.shape), full3(cmat.shape), full2(lam.shape), full2(dskip.shape)],
        out_specs=[rows, rows, pl.BlockSpec((1, 2, ns), lambda c: (c, 0, 0)), pl.BlockSpec((t_rows, 2 * ns), lambda c: (c, 0))],
        out_shape=[jax.ShapeDtypeStruct((s, w), F32), jax.ShapeDtypeStruct((s, w), BF16),
                   jax.ShapeDtypeStruct((s // t_rows, 2, ns), F32), jax.ShapeDtypeStruct((s, 2 * ns), BF16)],
        scratch_shapes=[pltpu.VMEM((2, ns), F32), pltpu.VMEM((nj, t_rows, 2 * SSM_TILE_ST), F32),
                        pltpu.VMEM((t_rows, 2 * SSM_TILE_ST), F32), pltpu.VMEM((t_rows, SSM_TILE_CH), F32)],
        semantics=("arbitrary",), args=[*([z] * nj), bmat, cmat, lam, dskip], side=side)
    return outs if side is None else (outs, carried)


def _ssm_bwd(z, y, dyg, xin, xall, bmat, cmat, lam, dskip, u_off, side=None):
    s, nj, w, ns, t_rows = _ssm_dims(z, bmat, u_off)
    nc = s // t_rows
    per = t_rows // SSM_SEGMENTS
    seg, half = SSM_SEGMENTS, SSM_TILE_ST

    def body(*refs):
        u_refs = refs[:nj]
        (y_ref, dyg_ref, xin_ref, xall_ref, b_ref, c_ref, lam_ref, d_ref, du_ref, db_ref, dc_ref, dlam_ref, dd_ref,
         carry_ref, w_ref, gs_ref, perm_ref, acc_ref) = refs[nj:]

        @pl.when(pl.program_id(0) == 0)
        def _():
            carry_ref[...] = jnp.zeros_like(carry_ref)
            db_ref[...] = jnp.zeros_like(db_ref)
            dc_ref[...] = jnp.zeros_like(dc_ref)
            dd_ref[...] = jnp.zeros_like(dd_ref)
            acc_ref[...] = jnp.zeros_like(acc_ref)
            _fill_powers(lam_ref, w_ref, nj, per)

        for j in range(nj):
            st = slice(j * SSM_TILE_ST, (j + 1) * SSM_TILE_ST)
            ch = slice(j * SSM_TILE_CH, (j + 1) * SSM_TILE_CH)
            lr, li = lam_ref[0:1, st], lam_ref[1:2, st]
            up = _to_segment_order(u_refs[j], perm_ref)
            upb = up.astype(BF16)
            dyp = _to_segment_order(dyg_ref[:, ch] * _gelu_grad(y_ref[:, ch]), perm_ref)
            dyb = dyp.astype(BF16)
            xs = xall_ref[:, j * 2 * half:(j + 1) * 2 * half]
            xf = xs.astype(F32)
            ends = xf[t_rows - seg:t_rows - 1, :]
            in_r = jnp.concatenate([xin_ref[0, 0:1, st], ends[:, :half]], axis=0)
            in_i = jnp.concatenate([xin_ref[0, 1:2, st], ends[:, half:]], axis=0)
            (gr, gi), _ = _segment_scan(_dot_nt(dyb, c_ref[j]), gs_ref, w_ref.at[j], lr, li,
                                        carry_ref[0:1, st], carry_ref[1:2, st], conj=True, reverse=True)
            carry_ref[0:1, st] = gr
            carry_ref[1:2, st] = gi
            gs = gs_ref[...]
            xsr, xsi, gsr, gsi = xf[:, :half], xf[:, half:], gs[:, :half], gs[:, half:]
            pxr = jnp.concatenate([in_r, xsr[:t_rows - seg]], axis=0)
            pxi = jnp.concatenate([in_i, xsi[:t_rows - seg]], axis=0)
            dl_r = gsr * pxr + gsi * pxi
            dl_i = gsi * pxr - gsr * pxi
            acc_ref[0, :, st] += jnp.sum(dl_r.reshape(per, seg, half), axis=0)
            acc_ref[1, :, st] += jnp.sum(dl_i.reshape(per, seg, half), axis=0)
            gx = gs.astype(BF16)
            dup = _dot_nt(gx, b_ref[j]) + d_ref[:, ch] * dyp

            def store(i, rows, ch=ch):
                du_ref[i * per:(i + 1) * per, ch] = rows.astype(BF16)

            _to_time_order(dup, perm_ref, store)
            db_ref[j] += _dot_tn(upb, gx)
            dc_ref[j] += _dot_tn(xs, dyb)
            dd_ref[:, ch] += jnp.sum(dyp * up, axis=0, keepdims=True)

        @pl.when(pl.program_id(0) == nc - 1)
        def _():
            dlam_ref[...] = jnp.sum(acc_ref[...], axis=1)

    rev = lambda c: nc - 1 - c
    u_specs = [pl.BlockSpec((t_rows, SSM_TILE_CH), lambda c, k=k: (rev(c), u_off // SSM_TILE_CH + k))
               for k in range(nj)]
    full3 = lambda shape: pl.BlockSpec(shape, lambda c: (0, 0, 0))
    full2 = lambda shape: pl.BlockSpec(shape, lambda c: (0, 0))
    rows = pl.BlockSpec((t_rows, w), lambda c: (rev(c), 0))
    outs, carried = _call(
        body, name="ssm_bwd", grid=(nc,),
        in_specs=u_specs + [rows, rows, pl.BlockSpec((1, 2, ns), lambda c: (rev(c), 0, 0)),
                            pl.BlockSpec((t_rows, 2 * ns), lambda c: (rev(c), 0)),
                            full3(bmat.shape), full3(cmat.shape), full2(lam.shape), full2(dskip.shape)],
        out_specs=[rows, full3(bmat.shape), full3(cmat.shape), full2(lam.shape), full2(dskip.shape)],
        out_shape=[jax.ShapeDtypeStruct((s, w), BF16), jax.ShapeDtypeStruct(bmat.shape, F32),
                   jax.ShapeDtypeStruct(cmat.shape, F32), jax.ShapeDtypeStruct(lam.shape, F32),
                   jax.ShapeDtypeStruct(dskip.shape, F32)],
        scratch_shapes=[pltpu.VMEM((2, ns), F32), pltpu.VMEM((nj, t_rows, 2 * SSM_TILE_ST), F32),
                        pltpu.VMEM((t_rows, 2 * SSM_TILE_ST), F32),
                        pltpu.VMEM((t_rows, SSM_TILE_CH), F32), pltpu.VMEM((2, SSM_SEGMENTS, ns), F32)],
        semantics=("arbitrary",), args=[*([z] * nj), y, dyg, xin, xall, bmat, cmat, lam, dskip], side=side)
    return outs if side is None else (outs, carried)


def _adam_math(w, g, m, v):
    m = ADAM_B1 * m + (1.0 - ADAM_B1) * g
    v = ADAM_B2 * v + (1.0 - ADAM_B2) * (g * g)
    m_hat = m / (1.0 - ADAM_B1 ** ADAM_STEP)
    v_hat = v / (1.0 - ADAM_B2 ** ADAM_STEP)
    delta = -ADAM_LR * (m_hat / (jnp.sqrt(v_hat) + ADAM_EPS) + ADAM_WD * w)
    return delta, m, v


def _adam_rows(r, c):
    for tr in (512, 256, 128, 64, 32, 16, 8):
        if r % tr == 0 and tr * c * 4 <= (1 << 20):
            return tr
    return r


def _adamw_big(w, p_mine, p_sib, m, v, name):
    r, c = w.shape
    tr = _adam_rows(r, c)

    def body(w_ref, a_ref, b_ref, m_ref, v_ref, g_ref, d_ref, nm_ref, nv_ref):
        g = a_ref[...] + b_ref[...]
        g_ref[...] = g
        d_ref[...], nm_ref[...], nv_ref[...] = _adam_math(w_ref[...], g, m_ref[...], v_ref[...])

    blk = pl.BlockSpec((tr, c), lambda i: (i, 0))
    return pl.pallas_call(body, name=f"adamw_{name}", grid=(r // tr,), in_specs=[blk] * 5, out_specs=[blk] * 4,
                          out_shape=[jax.ShapeDtypeStruct((r, c), F32)] * 4,
                          compiler_params=_params(("parallel",)))(w, p_mine, p_sib, m, v)


def _adamw_small(w, parts, m, v):
    r, c = w.shape
    n_dev = parts.shape[0]

    def body(w_ref, p_ref, m_ref, v_ref, g_ref, d_ref, nm_ref, nv_ref):
        g = p_ref[0]
        for k in range(1, n_dev):
            g = g + p_ref[k]
        g_ref[...] = g
        d_ref[...], nm_ref[...], nv_ref[...] = _adam_math(w_ref[...], g, m_ref[...], v_ref[...])

    blk = pl.BlockSpec((r, c), lambda i: (0, 0))
    return pl.pallas_call(body, name="adamw_small", grid=(1,),
                          in_specs=[blk, pl.BlockSpec((n_dev, r, c), lambda i: (0, 0, 0)), blk, blk],
                          out_specs=[blk] * 4, out_shape=[jax.ShapeDtypeStruct((r, c), F32)] * 4,
                          compiler_params=_params(("arbitrary",)))(w, parts, m, v)


def _cast_bf16(w, name):
    r, c = w.shape
    tr = _adam_rows(r, c)

    def body(w_ref, o_ref):
        o_ref[...] = w_ref[...].astype(BF16)

    blk = pl.BlockSpec((tr, c), lambda i: (i, 0))
    return pl.pallas_call(body, name=f"cast_{name}", grid=(r // tr,), in_specs=[blk], out_specs=blk,
                          out_shape=jax.ShapeDtypeStruct((r, c), BF16), compiler_params=_params(("parallel",)))(w)


def _sum_slots(recv, name):
    _, r, c = recv.shape
    tr = _adam_rows(r, c)

    def body(p_ref, o_ref):
        acc = p_ref[0].astype(F32)
        for k in range(1, N_CHIPS):
            acc = acc + p_ref[k].astype(F32)
        o_ref[...] = acc

    return pl.pallas_call(body, name=f"sum_{name}", grid=(r // tr,),
                          in_specs=[pl.BlockSpec((N_CHIPS, tr, c), lambda i: (0, i, 0))],
                          out_specs=pl.BlockSpec((tr, c), lambda i: (i, 0)),
                          out_shape=jax.ShapeDtypeStruct((r, c), F32), compiler_params=_params(("parallel",)))(recv)


BIG_WEIGHTS = ("w_in", "w_attn_up", "w_glu_v", "w_glu_g", "w_out", "w_ffn_gate", "w_ffn_up", "w_ffn_down")
COL_SHARDED = ("w_in", "w_attn_up", "w_glu_v", "w_glu_g", "w_ffn_gate", "w_ffn_up")


def _aligned(v, m):
    return v if isinstance(v, int) else pl.multiple_of(v, m)


def _shard_of(ref, name, j, shard_shape, half=None):
    r, c = shard_shape
    rows = r if half is None else r // 2
    row0 = 0 if half is None else half * rows
    if name in COL_SHARDED:
        return ref.at[pl.ds(_aligned(row0, 16), rows), pl.ds(_aligned(j * c, 128), c)]
    return ref.at[pl.ds(_aligned(j * r + row0, 16), rows), :]


def _other_chips():
    x, y = lax.axis_index("x"), lax.axis_index("y")
    return [(1 - x, y), (x, 1 - y), (1 - x, 1 - y)]


def _dma_sems(n, arrays):
    return [pltpu.SemaphoreType.DMA((n, 3))] * arrays + [pltpu.SemaphoreType.DMA((n,))]


def _gather_side(shards):
    names = list(shards)
    n = len(names)
    full_shapes = []
    for k in names:
        r, c = shards[k].shape
        full_shapes.append((r, c * N_CHIPS) if k in COL_SHARDED else (r * N_CHIPS, c))

    def build(src, dst, sems):
        send_sems, recv_sems, pass_send_sems, pass_recv_sems, local_sems = sems
        x, y, c = lax.axis_index("x"), lax.axis_index("y"), lax.axis_index("c")
        me = 2 * x + y
        locals_, sends, arrivals, forwards, passed_on = [], [], [], [], []
        for i, k in enumerate(names):
            shape = shards[k].shape
            half_rows = shape[0] // 2
            locals_.append(pltpu.make_async_copy(src[i], _shard_of(dst[i], k, me, shape), local_sems.at[i]))
            my_half = src[i].at[pl.ds(_aligned(c * half_rows, 16), half_rows), :]
            for p, (px, py) in enumerate(_other_chips()):
                peer = 2 * px + py
                landed = _shard_of(dst[i], k, peer, shape, half=c)
                sends.append(pltpu.make_async_remote_copy(
                    src_ref=my_half, dst_ref=_shard_of(dst[i], k, me, shape, half=c), send_sem=send_sems.at[i, p],
                    recv_sem=recv_sems.at[i, p], device_id=(px, py, c), device_id_type=MESH))
                arrivals.append(pltpu.make_async_remote_copy(
                    src_ref=my_half, dst_ref=landed, send_sem=send_sems.at[i, p],
                    recv_sem=recv_sems.at[i, p], device_id=(px, py, c), device_id_type=MESH))
                forwards.append(pltpu.make_async_remote_copy(
                    src_ref=landed, dst_ref=landed, send_sem=pass_send_sems.at[i, p],
                    recv_sem=pass_recv_sems.at[i, p], device_id=(x, y, 1 - c), device_id_type=MESH))
                passed_on.append(pltpu.make_async_remote_copy(
                    src_ref=landed, dst_ref=_shard_of(dst[i], k, peer, shape, half=1 - c),
                    send_sem=pass_send_sems.at[i, p], recv_sem=pass_recv_sems.at[i, p],
                    device_id=(x, y, 1 - c), device_id_type=MESH))
        return locals_, sends, arrivals, forwards, passed_on

    return _Side([shards[k] for k in names], [jax.ShapeDtypeStruct(s, BF16) for s in full_shapes], _dma_sems(n, 4), build,
                 relays=True)


def _scatter_side(grads, shard_shapes):
    names = list(grads)
    n = len(names)

    def build(src, dst, sems):
        send_sems, recv_sems, local_sems = sems
        x, y, c = lax.axis_index("x"), lax.axis_index("y"), lax.axis_index("c")
        me = 2 * x + y
        locals_, sends, arrivals = [], [], []
        for i, k in enumerate(names):
            shape = shard_shapes[k]
            locals_.append(pltpu.make_async_copy(_shard_of(src[i], k, me, shape), dst[i].at[me], local_sems.at[i]))
            for p, (px, py) in enumerate(_other_chips()):
                peer = 2 * px + py
                sends.append(pltpu.make_async_remote_copy(
                    src_ref=_shard_of(src[i], k, peer, shape), dst_ref=dst[i].at[me], send_sem=send_sems.at[i, p],
                    recv_sem=recv_sems.at[i, p], device_id=(px, py, c), device_id_type=MESH))
                arrivals.append(pltpu.make_async_remote_copy(
                    src_ref=_shard_of(src[i], k, peer, shape), dst_ref=dst[i].at[peer], send_sem=send_sems.at[i, p],
                    recv_sem=recv_sems.at[i, p], device_id=(px, py, c), device_id_type=MESH))
        return locals_, sends, arrivals, [None] * len(arrivals), []

    return _Side([grads[k] for k in names],
                 [jax.ShapeDtypeStruct((N_CHIPS,) + tuple(shard_shapes[k]), BF16) for k in names], _dma_sems(n, 2), build)


def _put_cols(dz, src, col_off):
    s, w = src.shape
    tr = _pick(s, (2048, 1024, 512, 256, 128, 64, 8))
    tc = _pick(math.gcd(w, col_off), (1024, 512, 256, 128))
    off = col_off // tc

    def body(src_ref, dz_ref, o_ref):
        del dz_ref
        o_ref[...] = src_ref[...].astype(o_ref.dtype)

    return pl.pallas_call(
        body, name="put_cols", grid=(s // tr, w // tc),
        in_specs=[pl.BlockSpec((tr, tc), lambda i, j: (i, j)), pl.BlockSpec(memory_space=pl.ANY)],
        out_specs=pl.BlockSpec((tr, tc), lambda i, j: (i, off + j)),
        out_shape=jax.ShapeDtypeStruct(dz.shape, dz.dtype), input_output_aliases={1: 0},
        compiler_params=_params(("parallel", "parallel")))(src, dz)


def _swap_side(parts):
    n = len(parts)

    def build(src, dst, sems):
        send_sems, recv_sems = sems
        sibling = (lax.axis_index("x"), lax.axis_index("y"), 1 - lax.axis_index("c"))
        copies = [pltpu.make_async_remote_copy(src_ref=src[i], dst_ref=dst[i], send_sem=send_sems.at[i],
                                               recv_sem=recv_sems.at[i], device_id=sibling, device_id_type=MESH)
                  for i in range(n)]
        return [], copies, copies, [None] * n, []

    return _Side(parts, [jax.ShapeDtypeStruct(p.shape, F32) for p in parts],
                 [pltpu.SemaphoreType.DMA((n,)), pltpu.SemaphoreType.DMA((n,))], build)


def _share_side(packed):
    r, c = packed.shape

    def build(src, dst, sems):
        send_sems, recv_sems, local_sem = sems
        x, y, cc = lax.axis_index("x"), lax.axis_index("y"), lax.axis_index("c")
        me = 4 * x + 2 * y + cc
        own = pltpu.make_async_copy(src[0], dst[0].at[me], local_sem)
        sends, arrivals = [], []
        flips = [(fx, fy, fc) for fx in range(2) for fy in range(2) for fc in range(2) if fx or fy or fc]
        for p, (fx, fy, fc) in enumerate(flips):
            px, py, pc = x ^ fx, y ^ fy, cc ^ fc
            sends.append(pltpu.make_async_remote_copy(
                src_ref=src[0], dst_ref=dst[0].at[me], send_sem=send_sems.at[p], recv_sem=recv_sems.at[p],
                device_id=(px, py, pc), device_id_type=MESH))
            arrivals.append(pltpu.make_async_remote_copy(
                src_ref=src[0], dst_ref=dst[0].at[4 * px + 2 * py + pc], send_sem=send_sems.at[p],
                recv_sem=recv_sems.at[p], device_id=(px, py, pc), device_id_type=MESH))
        return [own], sends, arrivals, [None] * len(arrivals), []

    return _Side([packed], [jax.ShapeDtypeStruct((8, r, c), F32)],
                 [pltpu.SemaphoreType.DMA((7,)), pltpu.SemaphoreType.DMA((7,)), pltpu.SemaphoreType.DMA], build)


SMALL_WEIGHTS = ("norm_mix_pre", "ssm_a_re", "ssm_a_im", "ssm_log_dt", "ssm_b_re", "ssm_b_im", "ssm_c_re", "ssm_c_im",
                 "ssm_d", "norm_mix_post", "norm_ffn_pre", "norm_ffn_post")
WEIGHT_ORDER = ("norm_mix_pre", "w_in", "w_attn_up", "ssm_a_re", "ssm_a_im", "ssm_log_dt", "ssm_b_re", "ssm_b_im",
                "ssm_c_re", "ssm_c_im", "ssm_d", "w_glu_v", "w_glu_g", "w_out", "norm_mix_post", "norm_ffn_pre",
                "w_ffn_gate", "w_ffn_up", "w_ffn_down", "norm_ffn_post")
PACK_LANES = 128
PACK_ROWS = 8
PACK_GROUPS = (SMALL_WEIGHTS[:1], SMALL_WEIGHTS[1:])


def _pack_group(arrs, names):
    flat = jnp.concatenate([arrs[k].reshape(-1) for k in names])
    pad = -flat.shape[0] % (PACK_LANES * PACK_ROWS)
    return jnp.pad(flat, (0, pad)).reshape(-1, PACK_LANES)


def _pack_small(arrs):
    return jnp.concatenate([_pack_group(arrs, names) for names in PACK_GROUPS], axis=0)


def _unpack_small(packed, like):
    out, row = {}, 0
    for names in PACK_GROUPS:
        rows = _pack_group(like, names).shape[0]
        flat, pos = packed[row:row + rows].reshape(-1), 0
        for k in names:
            n = like[k].size
            out[k] = flat[pos:pos + n].reshape(like[k].shape)
            pos += n
        row += rows
    return out


def _local_step(x, target, big, small, shards=None, shard_shapes=None, h1=None):
    s, d = x.shape
    big, grads, slots = dict(big), {}, {}
    carry = shards is not None

    def gathering(names, call):
        if not carry:
            return call(None)
        res, got = call(_gather_side({k: shards[k] for k in names}))
        big.update(zip(names, got))
        return res

    def scattering(names, call):
        if not carry:
            return call(None)
        res, got = call(_scatter_side({k: grads[k] for k in names}, shard_shapes))
        slots.update(zip(names, got))
        return res

    u_off = 3 * HQ
    gate_off = u_off + d // 2
    g1, g2, g3, g4 = (small[k][0:1] for k in ("norm_mix_pre", "norm_mix_post", "norm_ffn_pre", "norm_ffn_post"))
    ssm_names = ("ssm_a_re", "ssm_a_im", "ssm_log_dt", "ssm_b_re", "ssm_b_im", "ssm_c_re", "ssm_c_im")
    (lam, bmat, cmat), ssm_vjp = jax.vjp(_ssm_prepare, *[small[k][0] for k in ssm_names])
    bmat, cmat = bmat.astype(BF16), cmat.astype(BF16)
    dskip = small["ssm_d"][0:1]

    if h1 is None:
        h1 = _norm_in(x, g1)
    z = gathering(("w_attn_up", "w_glu_v", "w_glu_g", "w_out", "w_ffn_gate"),
                  lambda side: _mm(h1, big["w_in"], "nn", F32, "in_proj", side=side))
    y, yg, xin, xall = gathering(("w_ffn_up",), lambda side: _ssm_fwd(z, bmat, cmat, lam, dskip, u_off, side=side))
    qkv = [_dilate_qkv(z, g, dil) for g, dil in enumerate(ATTN_DILATIONS)]
    outs, lses = zip(*[_attn_fwd(qkv[g], g, dil) for g, dil in enumerate(ATTN_DILATIONS)])
    attn = _attn_merge(outs, lses)
    merged, ab, gv, gg = _mm_fused(
        [attn, yg], [big["w_attn_up"], big["w_glu_v"], big["w_glu_g"]], [(0, 0), (1, 1), (1, 2)], "nn",
        [BF16, BF16, BF16, BF16], "branches_merge", extras=[(z, gate_off), (z, gate_off + d)], epilogue=_gates_epilogue)
    mo = _mm(merged, big["w_out"], "nn", F32, "mix_out")
    x2, h2 = _norm_mid(x, mo, g2, g3)
    act, fg, fu = gathering(("w_ffn_down",), lambda side: _mm_fused(
        [h2], [big["w_ffn_gate"], big["w_ffn_up"]], [(0, 0), (0, 1)], "nn", [BF16, BF16, BF16], "ffn_up_act",
        epilogue=_swiglu_epilogue, side=side))
    f = _mm(act, big["w_ffn_down"], "nn", F32, "ffn_down")
    loss, dout, df, dg4 = _loss_head(x2, f, g4, target)

    grads["w_ffn_down"] = _mm_kloop(act, df, "tn", BF16, "dw_ffn_down")
    dfg, dfu = scattering(("w_ffn_down",), lambda side: _mm_fused(
        [df], [big["w_ffn_down"]], [(0, 0)], "nt", [BF16, BF16], "d_ffn_act", extras=[(fg, 0), (fu, 0)],
        epilogue=_swiglu_bwd_epilogue, side=side))
    grads["w_ffn_gate"] = _mm_kloop(h2, dfg, "tn", BF16, "dw_ffn_gate")
    dh2 = scattering(("w_ffn_gate",), lambda side: _mm_fused(
        [dfg, dfu], [big["w_ffn_gate"], big["w_ffn_up"]], [(0, 0), (1, 1)], "nt", [F32], "d_h2",
        epilogue=_sum_epilogue, side=side))[0]
    grads["w_ffn_up"] = _mm_kloop(h2, dfu, "tn", BF16, "dw_ffn_up")
    dx2, dmo, dg2, dg3 = _norm_mid_bwd(x2, mo, g2, g3, dout, dh2)
    dz, dgs, dab, dgv, dgg = _mm_fused(
        [dmo], [big["w_out"]], [(0, 0)], "nt", [BF16] * 5, "d_merged_gates",
        extras=[(z, gate_off), (z, gate_off + d), (ab, 0), (gv, 0), (gg, 0)], epilogue=_gates_bwd_epilogue,
        out_place=[(z.shape[1], gate_off), None, None, None, None])
    dz = _put_cols(dz, dgs, gate_off + d)
    grads["w_out"] = _mm_kloop(merged, dmo, "tn", BF16, "dw_out")
    dyg = _mm_fused([dgv, dgg], [big["w_glu_v"], big["w_glu_g"]], [(0, 0), (1, 1)], "nt", [F32], "d_yg",
                    epilogue=_sum_epilogue)[0]
    grads["w_glu_v"] = _mm_kloop(yg, dgv, "tn", BF16, "dw_glu_v")
    grads["w_glu_g"] = _mm_kloop(yg, dgg, "tn", BF16, "dw_glu_g")
    du, dbmat, dcmat, dlam, dd = scattering(
        ("w_ffn_up", "w_out", "w_glu_v", "w_glu_g"),
        lambda side: _ssm_bwd(z, y, dyg, xin, xall, bmat, cmat, lam, dskip, u_off, side=side))
    dz = _put_cols(dz, du, u_off)
    dattn = _mm(dab, big["w_attn_up"], "nt", F32, "d_attn")
    grads["w_attn_up"] = _mm_kloop(attn, dab, "tn", BF16, "dw_attn_up")
    merged_bwd = _attn_merge_bwd(outs, lses, dattn)
    mine, theirs = {}, {}
    for g, dil in enumerate(ATTN_DILATIONS):
        side = None
        if carry and g == 0:
            mine = {k: _sum_slots(slots[k], k) for k in slots}
            side = _swap_side(list(mine.values()))
        dqkv = _attn_bwd(qkv[g], merged_bwd[g], lses[g], merged_bwd[3 + g], g, dil, side=side)
        if side is not None:
            dqkv, got = dqkv
            theirs = dict(zip(mine, got))
        dz = _undilate_dqkv(dqkv, dz, g, dil)
    small_grads = dict(zip(ssm_names, (t[None] for t in ssm_vjp((dlam, dbmat, dcmat)))))
    small_grads.update(norm_mix_post=dg2, norm_ffn_pre=dg3, norm_ffn_post=dg4, ssm_d=dd)
    if carry:
        side = _join_sides(_scatter_side({"w_attn_up": grads["w_attn_up"]}, shard_shapes),
                           _share_side(_pack_group(small_grads, PACK_GROUPS[1])))
        grads["w_in"], (slots["w_attn_up"], shared) = _mm_kloop(h1, dz, "tn", BF16, "dw_in", side=side)
    else:
        grads["w_in"] = _mm_kloop(h1, dz, "tn", BF16, "dw_in")
    dh1 = scattering(("w_in",), lambda side: _mm_kloop(dz, big["w_in"], "nt", F32, "d_h1", side=side))
    grad_x, dg1 = _norm_in_bwd(x, g1, dh1, dx2)
    small_grads["norm_mix_pre"] = dg1
    if carry:
        return loss[0, 0], grad_x, (slots, mine, theirs), (dg1, shared)
    return loss[0, 0], grad_x, grads, small_grads


def kernel(x, norm_mix_pre, w_in, w_attn_up, ssm_a_re, ssm_a_im, ssm_log_dt, ssm_b_re, ssm_b_im, ssm_c_re, ssm_c_im, ssm_d, w_glu_v, w_glu_g, w_out, norm_mix_post, norm_ffn_pre, w_ffn_gate, w_ffn_up, w_ffn_down, norm_ffn_post, loss_target, m_norm_mix_pre, m_w_in, m_w_attn_up, m_ssm_a_re, m_ssm_a_im, m_ssm_log_dt, m_ssm_b_re, m_ssm_b_im, m_ssm_c_re, m_ssm_c_im, m_ssm_d, m_w_glu_v, m_w_glu_g, m_w_out, m_norm_mix_post, m_norm_ffn_pre, m_w_ffn_gate, m_w_ffn_up, m_w_ffn_down, m_norm_ffn_post, v_norm_mix_pre, v_w_in, v_w_attn_up, v_ssm_a_re, v_ssm_a_im, v_ssm_log_dt, v_ssm_b_re, v_ssm_b_im, v_ssm_c_re, v_ssm_c_im, v_ssm_d, v_w_glu_v, v_w_glu_g, v_w_out, v_norm_mix_post, v_norm_ffn_pre, v_w_ffn_gate, v_w_ffn_up, v_w_ffn_down, v_norm_ffn_post):
    given = dict(locals())
    w = {k: given[k] for k in WEIGHT_ORDER}
    m = {k: given["m_" + k] for k in WEIGHT_ORDER}
    v = {k: given["v_" + k] for k in WEIGHT_ORDER}

    shard_shapes = {k: w[k].shape[1:] for k in BIG_WEIGHTS}
    shards = {"w_in": _cast_bf16(w["w_in"][0], "w_in")}
    h1, casts, got = _prologue(x[0], norm_mix_pre[0:1], {k: w[k][0] for k in BIG_WEIGHTS if k != "w_in"},
                               side=_gather_side({"w_in": shards["w_in"]}))
    shards.update(casts)
    big = {"w_in": got[0]}

    loss, grad_x, (slots, mine, theirs), small_grads = _local_step(
        x[0], loss_target[0], big, {k: w[k] for k in SMALL_WEIGHTS}, shards, shard_shapes, h1)
    loss = lax.psum(loss, MESH_AXES)

    last = [k for k in BIG_WEIGHTS if k not in mine]
    mine.update({k: _sum_slots(slots[k], k) for k in last})
    theirs.update(zip(last, _run_side(_swap_side([mine[k] for k in last]), "swap_last_grads")))
    out_g, out_d, out_m, out_v = {}, {}, {}, {}
    for k in BIG_WEIGHTS:
        res = _adamw_big(w[k][0], mine[k], theirs[k], m[k][0], v[k][0], k)
        out_g[k], out_d[k], out_m[k], out_v[k] = (t[None] for t in res)

    pick = lambda tree: {k: tree[k] for k in SMALL_WEIGHTS}
    dg1, shared = small_grads
    late = _run_side(_share_side(_pack_group({"norm_mix_pre": dg1}, PACK_GROUPS[0])), "share_last_grad")[0]
    parts = jnp.concatenate([late, shared], axis=1)
    res = _adamw_small(_pack_small(pick(w)), parts, _pack_small(pick(m)), _pack_small(pick(v)))
    for dst, packed in zip((out_g, out_d, out_m, out_v), res):
        dst.update(_unpack_small(packed, pick(w)))

    return (loss, grad_x[None], *[out_g[k] for k in WEIGHT_ORDER], *[out_d[k] for k in WEIGHT_ORDER],
            *[out_m[k] for k in WEIGHT_ORDER], *[out_v[k] for k in WEIGHT_ORDER])
```

```python
import functools
import math

import jax
import jax.numpy as jnp
from jax import lax
from jax.experimental import pallas as pl
from jax.experimental.pallas import tpu as pltpu

F32 = jnp.float32
BF16 = jnp.bfloat16

EPS = 1e-6
HEAD_DIM = 128
HEADS_PER_GROUP = 4
ATTN_DILATIONS = (1, 4, 16)
ATTN_BLK = 128
N_ATTN_HEADS = HEADS_PER_GROUP * len(ATTN_DILATIONS)
GROUP_W = HEADS_PER_GROUP * HEAD_DIM
HQ = N_ATTN_HEADS * HEAD_DIM
SSM_GROUP = 16
SSM_STATE = 64
SSM_TILE_CH = 128
SSM_TILE_ST = SSM_TILE_CH // SSM_GROUP * SSM_STATE
ADAM_LR = 0.001
ADAM_B1 = 0.9
ADAM_B2 = 0.999
ADAM_EPS = 1e-08
ADAM_WD = 0.01
ADAM_STEP = 10
NEG_BIG = -1e30
V7X_VMEM_LIMIT = 56 * 1024 * 1024
MESH_AXES = ("x", "y", "c")
N_CHIPS = 4


def _pick(n, cands):
    for c in cands:
        if n % c == 0:
            return c
    raise ValueError(f"no tile of {cands} divides {n}")


def _params(sem):
    return pltpu.CompilerParams(dimension_semantics=sem, vmem_limit_bytes=V7X_VMEM_LIMIT)


HBM = pl.BlockSpec(memory_space=pl.ANY)
MESH = pl.DeviceIdType.MESH


class _Side:
    def __init__(self, srcs, out_shapes, sem_shapes, build, aliases=None, relays=False):
        self.srcs, self.out_shapes, self.sem_shapes, self.build = list(srcs), list(out_shapes), list(sem_shapes), build
        self.aliases = dict(aliases or {})
        self.relays = relays

    def start(self, src, dst, sems):
        local, sends = self.build(src, dst, sems)[:2]
        for cp in local + sends:
            cp.start()

    def relay(self, src, dst, sems):
        _, _, arrivals, forwards, _ = self.build(src, dst, sems)
        for cp, forward in zip(arrivals, forwards):
            if forward is not None:
                cp.wait_recv()
                forward.start()

    def wait(self, src, dst, sems, relayed=False):
        local, sends, arrivals, forwards, passed_on = self.build(src, dst, sems)
        for cp, forward in zip(arrivals, forwards):
            if forward is None:
                cp.wait_recv()
            elif not relayed:
                cp.wait_recv()
                forward.start()
        for cp in passed_on:
            cp.wait_recv()
        for cp in sends + [f for f in forwards if f is not None]:
            cp.wait_send()
        for cp in local:
            cp.wait()


def _join_sides(a, b):
    ns, no, nm = len(a.srcs), len(a.out_shapes), len(a.sem_shapes)

    def build(src, dst, sems):
        ra, rb = a.build(src[:ns], dst[:no], sems[:nm]), b.build(src[ns:], dst[no:], sems[nm:])
        return tuple(p + q for p, q in zip(ra, rb))

    aliases = {**a.aliases, **{ns + k: no + v for k, v in b.aliases.items()}}
    return _Side(a.srcs + b.srcs, a.out_shapes + b.out_shapes, a.sem_shapes + b.sem_shapes, build, aliases,
                 a.relays or b.relays)


def _call(body, *, name, grid, in_specs, out_specs, out_shape, semantics, args, scratch_shapes=(), side=None, **kw):
    in_specs, out_specs, out_shape, scratch_shapes = list(in_specs), list(out_specs), list(out_shape), list(scratch_shapes)
    if side is None:
        res = pl.pallas_call(body, name=name, grid=grid, in_specs=in_specs, out_specs=out_specs, out_shape=out_shape,
                             scratch_shapes=scratch_shapes, compiler_params=_params(semantics), **kw)(*args)
        return list(res), []
    n_in, n_out, n_scr = len(in_specs), len(out_specs), len(scratch_shapes)
    ns_in, ns_out = len(side.srcs), len(side.out_shapes)
    n_steps = math.prod(grid)
    relay_at = (3 * n_steps) // 4 if side.relays and n_steps >= 4 else None

    def carrying(*refs):
        ins, s_in = refs[:n_in], refs[n_in:n_in + ns_in]
        o0 = n_in + ns_in
        outs, s_out = refs[o0:o0 + n_out], refs[o0 + n_out:o0 + n_out + ns_out]
        c0 = o0 + n_out + ns_out
        scr, sems = refs[c0:c0 + n_scr], refs[c0 + n_scr:]
        step = functools.reduce(lambda acc, ig: acc * ig[1] + pl.program_id(ig[0]), enumerate(grid), 0)

        @pl.when(step == 0)
        def _():
            side.start(s_in, s_out, sems)

        if relay_at is not None:
            @pl.when(step == relay_at)
            def _():
                side.relay(s_in, s_out, sems)

        body(*ins, *outs, *scr)

        @pl.when(step == n_steps - 1)
        def _():
            side.wait(s_in, s_out, sems, relayed=relay_at is not None)

    res = pl.pallas_call(
        carrying, name=name, grid=grid, in_specs=in_specs + [HBM] * ns_in, out_specs=out_specs + [HBM] * ns_out,
        out_shape=out_shape + side.out_shapes, scratch_shapes=scratch_shapes + side.sem_shapes,
        input_output_aliases={n_in + k: n_out + v for k, v in side.aliases.items()},
        compiler_params=pltpu.CompilerParams(dimension_semantics=("arbitrary",) * len(grid),
                                             vmem_limit_bytes=V7X_VMEM_LIMIT, has_side_effects=True), **kw,
    )(*args, *side.srcs)
    return list(res[:n_out]), list(res[n_out:])


def _run_side(side, name):
    ns, no = len(side.srcs), len(side.out_shapes)

    def body(*refs):
        src, dst, sems = refs[:ns], refs[ns:ns + no], refs[ns + no:]
        side.start(src, dst, sems)
        side.wait(src, dst, sems)

    return list(pl.pallas_call(body, name=name, in_specs=[HBM] * ns, out_specs=[HBM] * no, out_shape=side.out_shapes,
                               scratch_shapes=side.sem_shapes,
                               compiler_params=pltpu.CompilerParams(has_side_effects=True))(*side.srcs))


_DOT_DIMS = {"nn": (((1,), (0,)), ((), ())), "nt": (((1,), (1,)), ((), ())), "tn": (((0,), (0,)), ((), ()))}


MM_VMEM_BUDGET = 44 * 1024 * 1024
MM_STEP_BYTES = 1 << 20
MM_ACC_BYTES = 4
MM_EPILOGUE_COLS = 256


def _size(dtype):
    return jnp.dtype(dtype).itemsize


def _mm_fused(as_, bs, pairs, mode, out_dtypes, name, extras=(), epilogue=None, side=None, out_place=None):
    M = as_[0].shape[0]
    N = bs[0].shape[1] if mode == "nn" else bs[0].shape[0]
    ks_a = [a.shape[1] for a in as_]
    ks_b = [b.shape[0] if mode == "nn" else b.shape[1] for b in bs]
    chunked = epilogue is not None
    if epilogue is None:
        epilogue = lambda rs, es: rs
    offs = [off for _, off in extras]
    place = list(out_place) if out_place else [None] * len(out_dtypes)
    offs_all = offs + [p[1] for p in place if p is not None]
    best = None
    for tm in (2048, 1024, 512, 256, 128):
        for tn in (2048, 1024, 512, 256, 128):
            if M % tm or N % tn or any(off % tn for off in offs_all):
                continue
            vmem = (sum(2 * tm * k * 2 for k in ks_a) + sum(2 * k * tn * 2 for k in ks_b)
                    + sum(2 * tm * tn * _size(d) for d in out_dtypes) + sum(2 * tm * tn * _size(e.dtype) for e, _ in extras)
                    + len(pairs) * tm * tn * 4)
            cost = sum(k * N * 2 for k in ks_b) * (M // tm) + (M // tm) * (N // tn) * MM_STEP_BYTES
            if vmem <= MM_VMEM_BUDGET and (best is None or cost < best[0]):
                best = (cost, tm, tn)
    _, tm, tn = best
    na, nb, ne, no = len(as_), len(bs), len(extras), len(out_dtypes)
    dims = _DOT_DIMS[mode]

    sub = MM_EPILOGUE_COLS if chunked and tn % MM_EPILOGUE_COLS == 0 else tn

    def body(*refs):
        a_refs, b_refs = refs[:na], refs[na:na + nb]
        e_refs, o_refs = refs[na + nb:na + nb + ne], refs[na + nb + ne:]
        for c0 in range(0, tn, sub):
            cs = slice(c0, c0 + sub)
            rs = [lax.dot_general(a_refs[ai][...], b_refs[bi][:, cs] if mode == "nn" else b_refs[bi][cs, :], dims,
                                  preferred_element_type=F32) for ai, bi in pairs]
            outs = epilogue(rs, [e[:, cs] for e in e_refs])
            for o_ref, o in zip(o_refs, outs):
                o_ref[:, cs] = o.astype(o_ref.dtype)

    a_specs = [pl.BlockSpec((tm, k), lambda i, j: (i, 0)) for k in ks_a]
    if mode == "nn":
        b_specs = [pl.BlockSpec((k, tn), lambda i, j: (0, j)) for k in ks_b]
    else:
        b_specs = [pl.BlockSpec((tn, k), lambda i, j: (j, 0)) for k in ks_b]
    e_specs = [pl.BlockSpec((tm, tn), lambda i, j, o=off // tn: (i, o + j)) for off in offs]
    o_specs = [pl.BlockSpec((tm, tn), lambda i, j, o=(p[1] // tn if p else 0): (i, o + j)) for p in place]
    outs, carried = _call(
        body, name=name, grid=(M // tm, N // tn), in_specs=a_specs + b_specs + e_specs, out_specs=o_specs,
        out_shape=[jax.ShapeDtypeStruct((M, p[0] if p else N), d) for d, p in zip(out_dtypes, place)],
        semantics=("parallel", "arbitrary"),
        args=[*as_, *bs, *[e for e, _ in extras]], side=side)
    return outs if side is None else (outs, carried)


def _mm(a, b, mode, out_dtype, name, side=None):
    res = _mm_fused([a], [b], [(0, 0)], mode, [out_dtype], name, side=side)
    return res[0] if side is None else (res[0][0], res[1])


def _mm_kloop(a, b, mode, out_dtype, name, second=None, side=None):
    if mode == "nn":
        (M, K), (_, N) = a.shape, b.shape
    elif mode == "nt":
        (M, K), (N, _) = a.shape, b.shape
    else:
        (K, M), (_, N) = a.shape, b.shape
    products = 1 if second is None else 2
    best = None
    for tm in (2816, 2048, 1408, 1024, 512, 256, 128):
        for tn in (2816, 2432, 2048, 1408, 1024, 512, 256, 128):
            for tk in (2816, 2432, 2048, 1408, 1024, 512, 256, 128):
                if M % tm or N % tn or K % tk:
                    continue
                vmem = 2 * tm * tn * 4 + 2 * tm * tn * _size(out_dtype) + products * 2 * tk * (tm + tn) * 2
                steps = (M // tm) * (N // tn) * (K // tk)
                cost = (K * M * 2 * (N // tn) + K * N * 2 * (M // tm) + steps * MM_STEP_BYTES
                        + steps * tm * tn * MM_ACC_BYTES)
                if vmem <= MM_VMEM_BUDGET and (best is None or cost < best[0]):
                    best = (cost, tm, tn, tk)
    _, tm, tn, tk = best
    nk = K // tk
    dims = _DOT_DIMS[mode]

    def body(*refs):
        o_ref, acc_ref = refs[-2:]
        k = pl.program_id(2)

        @pl.when(k == 0)
        def _():
            acc_ref[...] = jnp.zeros_like(acc_ref)

        for p in range(products):
            @pl.when(jnp.logical_and(k >= p * nk, k < (p + 1) * nk))
            def _(p=p):
                acc_ref[...] += lax.dot_general(refs[2 * p][...], refs[2 * p + 1][...], dims, preferred_element_type=F32)

        @pl.when(k == products * nk - 1)
        def _():
            o_ref[...] = acc_ref[...].astype(o_ref.dtype)

    def a_spec(p):
        kk = lambda k: jnp.clip(k - p * nk, 0, nk - 1)
        if mode == "tn":
            return pl.BlockSpec((tk, tm), lambda i, j, k: (kk(k), i))
        return pl.BlockSpec((tm, tk), lambda i, j, k: (i, kk(k)))

    def b_spec(p):
        kk = lambda k: jnp.clip(k - p * nk, 0, nk - 1)
        if mode == "nt":
            return pl.BlockSpec((tn, tk), lambda i, j, k: (j, kk(k)))
        return pl.BlockSpec((tk, tn), lambda i, j, k: (kk(k), j))

    o_spec = pl.BlockSpec((tm, tn), lambda i, j, k: (i, j))
    operands = (a, b) + (tuple(second) if second is not None else ())
    outs, carried = _call(
        body, name=name, grid=(M // tm, N // tn, products * nk),
        in_specs=[spec(p) for p in range(products) for spec in (a_spec, b_spec)], out_specs=[o_spec],
        out_shape=[jax.ShapeDtypeStruct((M, N), out_dtype)], scratch_shapes=[pltpu.VMEM((tm, tn), F32)],
        semantics=("parallel", "parallel", "arbitrary"), args=operands, side=side)
    return outs[0] if side is None else (outs[0], carried)


def _sigmoid(v):
    return 0.5 * jnp.tanh(0.5 * v) + 0.5


_GELU_C = math.sqrt(2.0 / math.pi)


def _gelu(v):
    return 0.5 * v * (1.0 + jnp.tanh(_GELU_C * (v + 0.044715 * v * v * v)))


def _gelu_grad(v):
    t = jnp.tanh(_GELU_C * (v + 0.044715 * v * v * v))
    return 0.5 * (1.0 + t) + 0.5 * v * (1.0 - t * t) * _GELU_C * (1.0 + 3.0 * 0.044715 * v * v)


def _rms(v, gain):
    r = lax.rsqrt(jnp.mean(v * v, axis=-1, keepdims=True) + EPS)
    return v * r * gain


def _rms_bwd(v, gain, dy):
    r = lax.rsqrt(jnp.mean(v * v, axis=-1, keepdims=True) + EPS)
    a = dy * gain
    dv = r * a - v * (r * r * r) * jnp.mean(a * v, axis=-1, keepdims=True)
    return dv, dy * v * r


def _row_tile(s):
    return _pick(s, (256, 128, 64, 8))


def _norm_in(x, gain):
    s, d = x.shape
    tr = _row_tile(s)

    def body(x_ref, g_ref, h_ref):
        h_ref[...] = _rms(x_ref[...], g_ref[...]).astype(BF16)

    row = pl.BlockSpec((tr, d), lambda i: (i, 0))
    vec = pl.BlockSpec((1, d), lambda i: (0, 0))
    return pl.pallas_call(body, name="norm_in", grid=(s // tr,), in_specs=[row, vec], out_specs=row,
                          out_shape=jax.ShapeDtypeStruct((s, d), BF16), compiler_params=_params(("parallel",)))(x, gain)


def _prologue(x, gain, weights, side=None):
    s, d = x.shape
    tr = _row_tile(s)
    steps = s // tr
    names = list(weights)
    tiles = []
    for k in names:
        r, _ = weights[k].shape
        tiles.append(next(t for t in range(16, r + 1, 16) if r % t == 0 and r // t <= steps))

    def body(*refs):
        x_ref, g_ref = refs[:2]
        w_refs, h_ref, o_refs = refs[2:2 + len(names)], refs[2 + len(names)], refs[3 + len(names):]
        h_ref[...] = _rms(x_ref[...], g_ref[...]).astype(BF16)
        for w_ref, o_ref in zip(w_refs, o_refs):
            o_ref[...] = w_ref[...].astype(BF16)

    row = pl.BlockSpec((tr, d), lambda i: (i, 0))
    w_specs = [pl.BlockSpec((t, weights[k].shape[1]), lambda i, last=weights[k].shape[0] // t - 1: (jnp.minimum(i, last), 0))
               for k, t in zip(names, tiles)]
    outs, carried = _call(
        body, name="prologue", grid=(steps,), in_specs=[row, pl.BlockSpec((1, d), lambda i: (0, 0))] + w_specs,
        out_specs=[row] + w_specs,
        out_shape=[jax.ShapeDtypeStruct((s, d), BF16)] + [jax.ShapeDtypeStruct(weights[k].shape, BF16) for k in names],
        semantics=("arbitrary",), args=[x, gain] + [weights[k] for k in names], side=side)
    return outs[0], dict(zip(names, outs[1:])), carried


def _norm_mid(x, mo, g_post, g_pre):
    s, d = x.shape
    tr = _row_tile(s)

    def body(x_ref, mo_ref, g2_ref, g3_ref, x2_ref, h2_ref):
        x2 = x_ref[...] + _rms(mo_ref[...], g2_ref[...])
        x2_ref[...] = x2
        h2_ref[...] = _rms(x2, g3_ref[...]).astype(BF16)

    row = pl.BlockSpec((tr, d), lambda i: (i, 0))
    vec = pl.BlockSpec((1, d), lambda i: (0, 0))
    return pl.pallas_call(
        body, name="norm_mid", grid=(s // tr,), in_specs=[row, row, vec, vec], out_specs=[row, row],
        out_shape=[jax.ShapeDtypeStruct((s, d), F32), jax.ShapeDtypeStruct((s, d), BF16)],
        compiler_params=_params(("parallel",)))(x, mo, g_post, g_pre)


def _loss_head(x2, f, g_post, target):
    s, d = x2.shape
    tr = _row_tile(s)

    def body(x2_ref, f_ref, g_ref, t_ref, loss_ref, dout_ref, df_ref, dg_ref):
        @pl.when(pl.program_id(0) == 0)
        def _():
            loss_ref[...] = jnp.zeros_like(loss_ref)
            dg_ref[...] = jnp.zeros_like(dg_ref)

        fv = f_ref[...]
        g = g_ref[...]
        err = x2_ref[...] + _rms(fv, g) - t_ref[...]
        loss_ref[...] += 0.5 * jnp.sum(jnp.mean(err * err, axis=-1, keepdims=True), axis=0, keepdims=True)
        dout = err * (1.0 / d)
        dout_ref[...] = dout
        df, dg = _rms_bwd(fv, g, dout)
        df_ref[...] = df.astype(BF16)
        dg_ref[...] += jnp.sum(dg, axis=0, keepdims=True)

    row = pl.BlockSpec((tr, d), lambda i: (i, 0))
    vec = pl.BlockSpec((1, d), lambda i: (0, 0))
    one = pl.BlockSpec((1, 1), lambda i: (0, 0))
    return pl.pallas_call(
        body, name="loss_head", grid=(s // tr,), in_specs=[row, row, vec, row], out_specs=[one, row, row, vec],
        out_shape=[jax.ShapeDtypeStruct((1, 1), F32), jax.ShapeDtypeStruct((s, d), F32),
                   jax.ShapeDtypeStruct((s, d), BF16), jax.ShapeDtypeStruct((1, d), F32)],
        compiler_params=_params(("arbitrary",)))(x2, f, g_post, target)


def _norm_mid_bwd(x2, mo, g_post, g_pre, dout, dh2):
    s, d = x2.shape
    tr = _row_tile(s)

    def body(x2_ref, mo_ref, g2_ref, g3_ref, dout_ref, dh2_ref, dx2_ref, dmo_ref, dg2_ref, dg3_ref):
        @pl.when(pl.program_id(0) == 0)
        def _():
            dg2_ref[...] = jnp.zeros_like(dg2_ref)
            dg3_ref[...] = jnp.zeros_like(dg3_ref)

        dv, dg3 = _rms_bwd(x2_ref[...], g3_ref[...], dh2_ref[...])
        dx2 = dout_ref[...] + dv
        dx2_ref[...] = dx2
        dmo, dg2 = _rms_bwd(mo_ref[...], g2_ref[...], dx2)
        dmo_ref[...] = dmo.astype(BF16)
        dg2_ref[...] += jnp.sum(dg2, axis=0, keepdims=True)
        dg3_ref[...] += jnp.sum(dg3, axis=0, keepdims=True)

    row = pl.BlockSpec((tr, d), lambda i: (i, 0))
    vec = pl.BlockSpec((1, d), lambda i: (0, 0))
    return pl.pallas_call(
        body, name="norm_mid_bwd", grid=(s // tr,), in_specs=[row, row, vec, vec, row, row],
        out_specs=[row, row, vec, vec],
        out_shape=[jax.ShapeDtypeStruct((s, d), F32), jax.ShapeDtypeStruct((s, d), BF16),
                   jax.ShapeDtypeStruct((1, d), F32), jax.ShapeDtypeStruct((1, d), F32)],
        compiler_params=_params(("arbitrary",)))(x2, mo, g_post, g_pre, dout, dh2)


def _norm_in_bwd(x, gain, dh, dx2):
    s, d = x.shape
    tr = _row_tile(s)

    def body(x_ref, g_ref, dh_ref, dx2_ref, dx_ref, dg_ref):
        @pl.when(pl.program_id(0) == 0)
        def _():
            dg_ref[...] = jnp.zeros_like(dg_ref)

        dv, dg = _rms_bwd(x_ref[...], g_ref[...], dh_ref[...])
        dx_ref[...] = dx2_ref[...] + dv
        dg_ref[...] += jnp.sum(dg, axis=0, keepdims=True)

    row = pl.BlockSpec((tr, d), lambda i: (i, 0))
    vec = pl.BlockSpec((1, d), lambda i: (0, 0))
    return pl.pallas_call(
        body, name="norm_in_bwd", grid=(s // tr,), in_specs=[row, vec, row, row], out_specs=[row, vec],
        out_shape=[jax.ShapeDtypeStruct((s, d), F32), jax.ShapeDtypeStruct((1, d), F32)],
        compiler_params=_params(("arbitrary",)))(x, gain, dh, dx2)


def _swiglu_epilogue(rs, es):
    g, u = rs
    return [g * _sigmoid(g) * u, g, u]


def _swiglu_bwd_epilogue(rs, es):
    d = rs[0]
    g, u = es[0].astype(F32), es[1].astype(F32)
    sg = _sigmoid(g)
    return [d * u * sg * (1.0 + g * (1.0 - sg)), d * g * sg]


def _sum_epilogue(rs, es):
    return [rs[0] + rs[1]]


def _gates_epilogue(rs, es):
    ab, gv, gg = rs
    ga, gs = es
    return [_sigmoid(ga) * ab + _sigmoid(gs) * gv * _sigmoid(gg), ab, gv, gg]


def _gates_bwd_epilogue(rs, es):
    dm = rs[0]
    ga, gs, ab, gv, gg = (e.astype(F32) for e in es)
    sa, ss, sg = _sigmoid(ga), _sigmoid(gs), _sigmoid(gg)
    dsb = dm * ss
    return [dm * ab * sa * (1.0 - sa), dm * gv * sg * ss * (1.0 - ss), dm * sa, dsb * sg, dsb * gv * sg * (1.0 - sg)]


ATTN_ROWS = 2048


def _dilate_qkv(z, g, d):
    s = z.shape[0]
    tm = ATTN_ROWS
    per = tm // d
    nh = HEADS_PER_GROUP

    def body(z_ref, o_ref):
        for r in range(d):
            rows = z_ref[...] if d == 1 else z_ref[pl.ds(r, per, stride=d), :]
            o_ref[0, r] = rows.astype(BF16)

    return pl.pallas_call(
        body, name=f"dilate_qkv_{g}", grid=(s // tm, 3, nh),
        in_specs=[pl.BlockSpec((tm, HEAD_DIM), lambda i, w, h: (i, (3 * w + g) * nh + h))],
        out_specs=pl.BlockSpec((1, d, per, HEAD_DIM), lambda i, w, h: (w, 0, i, h)),
        out_shape=jax.ShapeDtypeStruct((3, d, s // d, GROUP_W), BF16),
        compiler_params=_params(("parallel", "parallel", "parallel")))(z)


def _undilate_dqkv(dqkv, dz, g, d):
    s = dz.shape[0]
    tm = ATTN_ROWS
    per = tm // d
    nh = HEADS_PER_GROUP

    def body(i_ref, dz_ref, o_ref, nat_ref):
        del dz_ref
        if d == 1:
            o_ref[...] = i_ref[0, 0]
        else:
            for r in range(d):
                nat_ref[pl.ds(r, per, stride=d), :] = i_ref[0, r].astype(F32)
            o_ref[...] = nat_ref[...].astype(BF16)

    return pl.pallas_call(
        body, name=f"undilate_dqkv_{g}", grid=(s // tm, 3, nh),
        in_specs=[pl.BlockSpec((1, d, per, HEAD_DIM), lambda i, w, h: (w, 0, i, h)),
                  pl.BlockSpec(memory_space=pl.ANY)],
        out_specs=pl.BlockSpec((tm, HEAD_DIM), lambda i, w, h: (i, (3 * w + g) * nh + h)),
        out_shape=jax.ShapeDtypeStruct(dz.shape, dz.dtype), input_output_aliases={1: 0},
        scratch_shapes=[pltpu.VMEM((tm, HEAD_DIM), F32)],
        compiler_params=_params(("parallel", "parallel", "parallel")))(dqkv, dz)


def _alibi_slope(head):
    return 2.0 ** (-8.0 * (head + 1) / N_ATTN_HEADS)


def _dot_nt(a, b):
    return lax.dot_general(a, b, _DOT_DIMS["nt"], preferred_element_type=F32)


def _dot_tn(a, b):
    return lax.dot_general(a, b, _DOT_DIMS["tn"], preferred_element_type=F32)


def _dot(a, b):
    return jnp.dot(a, b, preferred_element_type=F32)


GROUP_ROWS = HEADS_PER_GROUP * ATTN_BLK


def _band_bias(g, d, pairs):
    qi = jnp.arange(ATTN_BLK)[:, None]
    ki = jnp.arange(ATTN_BLK)[None, :]
    rows = []
    for hh in range(HEADS_PER_GROUP):
        slope_d = _alibi_slope(g * HEADS_PER_GROUP + hh) * d
        tiles = []
        for kind in pairs:
            dist = qi - ki if kind == "cur" else ATTN_BLK + qi - ki
            ok = dist >= 0 if kind == "cur" else dist <= ATTN_BLK
            tiles.append(jnp.where(ok, -slope_d * dist.astype(F32), NEG_BIG))
        rows.append(jnp.concatenate(tiles, axis=1))
    return jnp.concatenate(rows, axis=0).astype(F32)


def _tile_cols(t):
    return slice(t * ATTN_BLK, (t + 1) * ATTN_BLK)


def _attn_fwd(qkv, g, d):
    _, _, L, _ = qkv.shape
    nb = L // ATTN_BLK
    scale = HEAD_DIM ** -0.5

    def body(q_ref, kc_ref, kp_ref, vc_ref, vp_ref, bias_ref, o_ref, lse_ref, s_ref, p_ref):
        n = pl.program_id(1)
        for hh in range(HEADS_PER_GROUP):
            cols, rows = _tile_cols(hh), _tile_cols(hh)
            q = q_ref[0, 0, :, cols]
            s_ref[rows, _tile_cols(0)] = _dot_nt(q, kp_ref[0, 0, :, cols])
            s_ref[rows, _tile_cols(1)] = _dot_nt(q, kc_ref[0, 0, :, cols])
        col = lax.broadcasted_iota(jnp.int32, (GROUP_ROWS, 2 * ATTN_BLK), 1)
        s = s_ref[...] * scale + bias_ref[...]
        s = jnp.where(jnp.logical_and(col < ATTN_BLK, n == 0), NEG_BIG, s)
        m = jnp.max(s, axis=-1, keepdims=True)
        e = jnp.exp(s - m)
        l = jnp.sum(e, axis=-1, keepdims=True)
        p_ref[...] = (e * (1.0 / l)).astype(BF16)
        lse = m + jnp.log(l)
        for hh in range(HEADS_PER_GROUP):
            cols, rows = _tile_cols(hh), _tile_cols(hh)
            o_ref[0, :, cols] = (_dot(p_ref[rows, _tile_cols(0)], vp_ref[0, 0, :, cols])
                                 + _dot(p_ref[rows, _tile_cols(1)], vc_ref[0, 0, :, cols]))
            lse_ref[0, :, cols] = jnp.broadcast_to(lse[rows], (ATTN_BLK, HEAD_DIM))

    def spec(w, shift):
        return pl.BlockSpec((1, 1, ATTN_BLK, GROUP_W), lambda r, n: (w, r, jnp.maximum(n + shift, 0), 0))

    out = pl.BlockSpec((1, ATTN_BLK, GROUP_W), lambda r, n: (r, n, 0))
    bias = _band_bias(g, d, ("prev", "cur"))
    return pl.pallas_call(
        body, name=f"attn_fwd_{g}", grid=(d, nb),
        in_specs=[spec(0, 0), spec(1, 0), spec(1, -1), spec(2, 0), spec(2, -1),
                  pl.BlockSpec(bias.shape, lambda r, n: (0, 0))],
        out_specs=[out, out], out_shape=[jax.ShapeDtypeStruct((d, L, GROUP_W), F32)] * 2,
        scratch_shapes=[pltpu.VMEM((GROUP_ROWS, 2 * ATTN_BLK), F32), pltpu.VMEM((GROUP_ROWS, 2 * ATTN_BLK), BF16)],
        compiler_params=_params(("parallel", "parallel")))(qkv, qkv, qkv, qkv, qkv, bias)


def _attn_bwd(qkv, do, lse, cc, g, d, side=None):
    _, _, L, _ = qkv.shape
    nb = L // ATTN_BLK
    scale = HEAD_DIM ** -0.5
    a_, b_, c_ = _tile_cols(0), _tile_cols(1), _tile_cols(2)

    def body(q0_ref, q1_ref, k0_ref, kp_ref, v0_ref, vp_ref, do0_ref, do1_ref, l0_ref, l1_ref, c0_ref, c1_ref,
             bias_ref, o_ref, s_ref, dp_ref, l_ref, c_ref, p_ref, ds_ref):
        n = pl.program_id(1)
        for hh in range(HEADS_PER_GROUP):
            cols, rows = _tile_cols(hh), _tile_cols(hh)
            q0, q1 = q0_ref[0, 0, :, cols], q1_ref[0, 0, :, cols]
            k0, kp = k0_ref[0, 0, :, cols], kp_ref[0, 0, :, cols]
            v0, vp = v0_ref[0, 0, :, cols], vp_ref[0, 0, :, cols]
            do0, do1 = do0_ref[0, :, cols], do1_ref[0, :, cols]
            s_ref[rows, a_], s_ref[rows, b_], s_ref[rows, c_] = _dot_nt(q0, k0), _dot_nt(q0, kp), _dot_nt(q1, k0)
            dp_ref[rows, a_], dp_ref[rows, b_], dp_ref[rows, c_] = _dot_nt(do0, v0), _dot_nt(do0, vp), _dot_nt(do1, v0)
            l_ref[rows, a_], l_ref[rows, b_], l_ref[rows, c_] = l0_ref[0, :, cols], l0_ref[0, :, cols], l1_ref[0, :, cols]
            c_ref[rows, a_], c_ref[rows, b_], c_ref[rows, c_] = c0_ref[0, :, cols], c0_ref[0, :, cols], c1_ref[0, :, cols]
        col = lax.broadcasted_iota(jnp.int32, (GROUP_ROWS, 3 * ATTN_BLK), 1)
        tile = col // ATTN_BLK
        gone = jnp.logical_or(jnp.logical_and(tile == 1, n == 0), jnp.logical_and(tile == 2, n == nb - 1))
        s = jnp.where(gone, NEG_BIG, s_ref[...] * scale + bias_ref[...])
        p = jnp.exp(s - l_ref[...])
        p_ref[...] = p.astype(BF16)
        ds_ref[...] = (p * (dp_ref[...] + c_ref[...])).astype(BF16)
        for hh in range(HEADS_PER_GROUP):
            cols, rows = _tile_cols(hh), _tile_cols(hh)
            q0, q1 = q0_ref[0, 0, :, cols], q1_ref[0, 0, :, cols]
            k0, kp = k0_ref[0, 0, :, cols], kp_ref[0, 0, :, cols]
            do0, do1 = do0_ref[0, :, cols], do1_ref[0, :, cols]
            o_ref[0, 0, :, cols] = ((_dot(ds_ref[rows, a_], k0) + _dot(ds_ref[rows, b_], kp)) * scale).astype(BF16)
            o_ref[1, 0, :, cols] = ((_dot_tn(ds_ref[rows, a_], q0) + _dot_tn(ds_ref[rows, c_], q1)) * scale).astype(BF16)
            o_ref[2, 0, :, cols] = (_dot_tn(p_ref[rows, a_], do0) + _dot_tn(p_ref[rows, c_], do1)).astype(BF16)

    def spec(w, shift):
        return pl.BlockSpec((1, 1, ATTN_BLK, GROUP_W), lambda r, n: (w, r, jnp.clip(n + shift, 0, nb - 1), 0))

    def spec3(shift):
        return pl.BlockSpec((1, ATTN_BLK, GROUP_W), lambda r, n: (r, jnp.clip(n + shift, 0, nb - 1), 0))

    bias = _band_bias(g, d, ("cur", "prev", "prev"))
    wide = (GROUP_ROWS, 3 * ATTN_BLK)
    outs, carried = _call(
        body, name=f"attn_bwd_{g}", grid=(d, nb),
        in_specs=[spec(0, 0), spec(0, 1), spec(1, 0), spec(1, -1), spec(2, 0), spec(2, -1),
                  spec3(0), spec3(1), spec3(0), spec3(1), spec3(0), spec3(1), pl.BlockSpec(wide, lambda r, n: (0, 0))],
        out_specs=[pl.BlockSpec((3, 1, ATTN_BLK, GROUP_W), lambda r, n: (0, r, n, 0))],
        out_shape=[jax.ShapeDtypeStruct((3, d, L, GROUP_W), BF16)],
        scratch_shapes=[pltpu.VMEM(wide, F32)] * 4 + [pltpu.VMEM(wide, BF16)] * 2, semantics=("parallel", "parallel"),
        args=[qkv, qkv, qkv, qkv, qkv, qkv, do, do, lse, lse, cc, cc, bias], side=side)
    return outs[0] if side is None else (outs[0], carried)


def _load_natural(refs, nat_refs):
    for g, d in enumerate(ATTN_DILATIONS):
        if d == 1:
            nat_refs[g][...] = refs[g][0]
        else:
            per = ATTN_ROWS // d
            for r in range(d):
                nat_refs[g][pl.ds(r, per, stride=d), :] = refs[g][r]


def _mix_weights(lse_nat):
    l0, l1, l2 = lse_nat[0][...], lse_nat[1][...], lse_nat[2][...]
    m = jnp.maximum(jnp.maximum(l0, l1), l2)
    e0, e1, e2 = jnp.exp(l0 - m), jnp.exp(l1 - m), jnp.exp(l2 - m)
    inv = 1.0 / (e0 + e1 + e2)
    return e0 * inv, e1 * inv, e2 * inv


def _dilated_specs(s):
    return [pl.BlockSpec((d, ATTN_ROWS // d, HEAD_DIM), lambda i, h: (0, i, h)) for d in ATTN_DILATIONS]


NATURAL_SCRATCH = [pltpu.VMEM((ATTN_ROWS, HEAD_DIM), F32)] * (2 * len(ATTN_DILATIONS))


def _attn_merge(outs, lses):
    s = outs[0].shape[0] * outs[0].shape[1]

    def body(o0, o1, o2, l0, l1, l2, a_ref, *nat):
        onat, lnat = nat[:3], nat[3:]
        _load_natural((o0, o1, o2), onat)
        _load_natural((l0, l1, l2), lnat)
        w0, w1, w2 = _mix_weights(lnat)
        a_ref[...] = (w0 * onat[0][...] + w1 * onat[1][...] + w2 * onat[2][...]).astype(BF16)

    return pl.pallas_call(
        body, name="attn_merge", grid=(s // ATTN_ROWS, HEADS_PER_GROUP), in_specs=_dilated_specs(s) * 2,
        out_specs=pl.BlockSpec((ATTN_ROWS, HEAD_DIM), lambda i, h: (i, h)),
        out_shape=jax.ShapeDtypeStruct((s, GROUP_W), BF16), scratch_shapes=NATURAL_SCRATCH,
        compiler_params=_params(("parallel", "parallel")))(*outs, *lses)


def _attn_merge_bwd(outs, lses, dattn):
    s = dattn.shape[0]

    def body(o0, o1, o2, l0, l1, l2, da_ref, do0, do1, do2, c0, c1, c2, *nat):
        onat, lnat = nat[:3], nat[3:]
        _load_natural((o0, o1, o2), onat)
        _load_natural((l0, l1, l2), lnat)
        ws = _mix_weights(lnat)
        da = da_ref[...]
        attn = ws[0] * onat[0][...] + ws[1] * onat[1][...] + ws[2] * onat[2][...]
        tot = jnp.broadcast_to(jnp.sum(da * attn, axis=-1, keepdims=True), (ATTN_ROWS, HEAD_DIM))
        for g, (d, do_ref, c_ref) in enumerate(zip(ATTN_DILATIONS, (do0, do1, do2), (c0, c1, c2))):
            if d == 1:
                do_ref[0] = (ws[g] * da).astype(BF16)
                c_ref[0] = -ws[g] * tot
            else:
                onat[g][...] = ws[g] * da
                lnat[g][...] = -ws[g] * tot
                per = ATTN_ROWS // d
                for r in range(d):
                    do_ref[r] = onat[g][pl.ds(r, per, stride=d), :].astype(BF16)
                    c_ref[r] = lnat[g][pl.ds(r, per, stride=d), :]

    dil = _dilated_specs(s)
    shapes = [jax.ShapeDtypeStruct(o.shape, BF16) for o in outs] + [jax.ShapeDtypeStruct(o.shape, F32) for o in outs]
    return pl.pallas_call(
        body, name="attn_merge_bwd", grid=(s // ATTN_ROWS, HEADS_PER_GROUP),
        in_specs=dil * 2 + [pl.BlockSpec((ATTN_ROWS, HEAD_DIM), lambda i, h: (i, h))], out_specs=dil * 2,
        out_shape=shapes, scratch_shapes=NATURAL_SCRATCH,
        compiler_params=_params(("parallel", "parallel")))(*outs, *lses, dattn)


def _ssm_prepare(a_re, a_im, log_dt, b_re, b_im, c_re, c_im):
    n_g = a_re.shape[0]
    nj = n_g * SSM_GROUP // SSM_TILE_CH
    gpt = SSM_TILE_CH // SSM_GROUP
    dt = jnp.exp(log_dt)[:, None]
    mag = jnp.exp(a_re * dt)
    lr, li = mag * jnp.cos(a_im * dt), mag * jnp.sin(a_im * dt)
    den = a_re * a_re + a_im * a_im
    cr = ((lr - 1.0) * a_re + li * a_im) / den
    ci = (li * a_re - (lr - 1.0) * a_im) / den
    bb_re = cr[..., None] * b_re - ci[..., None] * b_im
    bb_im = cr[..., None] * b_im + ci[..., None] * b_re
    eye = jnp.eye(gpt, dtype=F32)

    def b_tiles(t):
        t = t.transpose(0, 2, 1).reshape(nj, gpt, SSM_GROUP, SSM_STATE)
        return jnp.einsum("jgcp,gh->jgchp", t, eye).reshape(nj, SSM_TILE_CH, SSM_TILE_ST)

    def c_tiles(t):
        t = t.reshape(nj, gpt, SSM_GROUP, SSM_STATE)
        return jnp.einsum("jgcp,gh->jhpgc", t, eye).reshape(nj, SSM_TILE_ST, SSM_TILE_CH)

    lam = jnp.stack([lr.reshape(-1), li.reshape(-1)])
    bmat = jnp.concatenate([b_tiles(bb_re), b_tiles(bb_im)], axis=2)
    cmat = jnp.concatenate([c_tiles(c_re), -c_tiles(c_im)], axis=1)
    return lam, bmat, cmat


SSM_SEGMENTS = 8


def _to_segment_order(nat, perm_ref):
    per = nat.shape[0] // SSM_SEGMENTS
    for i in range(SSM_SEGMENTS):
        perm_ref[pl.ds(i, per, stride=SSM_SEGMENTS), :] = nat[i * per:(i + 1) * per, :]
    return perm_ref[...]


def _to_time_order(val, perm_ref, store):
    per = val.shape[0] // SSM_SEGMENTS
    perm_ref[...] = val
    for i in range(SSM_SEGMENTS):
        store(i, perm_ref[pl.ds(i, per, stride=SSM_SEGMENTS), :])


def _fill_powers(lam_ref, w_ref, nj, tau_n):
    for j in range(nj):
        st = slice(j * SSM_TILE_ST, (j + 1) * SSM_TILE_ST)
        lr = jnp.broadcast_to(lam_ref[0:1, st], (SSM_SEGMENTS, SSM_TILE_ST))
        li = jnp.broadcast_to(lam_ref[1:2, st], (SSM_SEGMENTS, SSM_TILE_ST))
        wr, wi = lr, li
        for tau in range(tau_n):
            rows = slice(tau * SSM_SEGMENTS, (tau + 1) * SSM_SEGMENTS)
            w_ref[j, rows, :SSM_TILE_ST] = wr
            w_ref[j, rows, SSM_TILE_ST:] = wi
            wr, wi = wr * lr - wi * li, wr * li + wi * lr


def _segment_scan(src, xs_ref, w_tile, lr, li, cr, ci, conj, reverse):
    seg, half = SSM_SEGMENTS, SSM_TILE_ST
    tau_n = src.shape[0] // seg
    sgn = -1.0 if conj else 1.0
    lr8 = jnp.broadcast_to(lr, (seg, half))
    li8 = jnp.broadcast_to(li, (seg, half)) * sgn
    xr = jnp.zeros((seg, half), F32)
    xi = jnp.zeros((seg, half), F32)
    order = range(tau_n - 1, -1, -1) if reverse else range(tau_n)
    for tau in order:
        rows = slice(tau * seg, (tau + 1) * seg)
        xr, xi = lr8 * xr - li8 * xi + src[rows, :half], lr8 * xi + li8 * xr + src[rows, half:]
        xs_ref[rows, :half] = xr
        xs_ref[rows, half:] = xi
    pr = w_tile[(tau_n - 1) * seg:(tau_n - 1) * seg + 1, :half]
    pi = w_tile[(tau_n - 1) * seg:(tau_n - 1) * seg + 1, half:] * sgn
    fr, fi = cr, ci
    ins_r, ins_i = [None] * seg, [None] * seg
    runs = range(seg - 1, -1, -1) if reverse else range(seg)
    for i in runs:
        ins_r[i], ins_i[i] = fr, fi
        fr, fi = xr[i:i + 1, :] + pr * fr - pi * fi, xi[i:i + 1, :] + pr * fi + pi * fr
    in_r = jnp.concatenate(ins_r, axis=0)
    in_i = jnp.concatenate(ins_i, axis=0)
    for tau in range(tau_n):
        rows = slice(tau * seg, (tau + 1) * seg)
        wrow = (tau_n - 1 - tau) if reverse else tau
        wr = w_tile[wrow * seg:(wrow + 1) * seg, :half]
        wi = w_tile[wrow * seg:(wrow + 1) * seg, half:] * sgn
        xs_ref[rows, :half] += wr * in_r - wi * in_i
        xs_ref[rows, half:] += wr * in_i + wi * in_r
    return (fr, fi), (in_r, in_i)


def _ssm_dims(z, bmat, u_off):
    s = z.shape[0]
    nj = bmat.shape[0]
    t_rows = _pick(s, (256, 128))
    return s, nj, nj * SSM_TILE_CH, nj * SSM_TILE_ST, t_rows


def _ssm_fwd(z, bmat, cmat, lam, dskip, u_off, side=None):
    s, nj, w, ns, t_rows = _ssm_dims(z, bmat, u_off)
    per = t_rows // SSM_SEGMENTS

    def body(*refs):
        u_refs = refs[:nj]
        b_ref, c_ref, lam_ref, d_ref, y_ref, yg_ref, xin_ref, xall_ref, carry_ref, w_ref, xs_ref, perm_ref = refs[nj:]

        @pl.when(pl.program_id(0) == 0)
        def _():
            carry_ref[...] = jnp.zeros_like(carry_ref)
            _fill_powers(lam_ref, w_ref, nj, per)

        xin_ref[0] = carry_ref[...]
        for j in range(nj):
            st = slice(j * SSM_TILE_ST, (j + 1) * SSM_TILE_ST)
            ch = slice(j * SSM_TILE_CH, (j + 1) * SSM_TILE_CH)
            up = _to_segment_order(u_refs[j], perm_ref)
            bu = _dot(up.astype(BF16), b_ref[j])
            (fr, fi), _ = _segment_scan(bu, xs_ref, w_ref.at[j], lam_ref[0:1, st], lam_ref[1:2, st],
                                        carry_ref[0:1, st], carry_ref[1:2, st], conj=False, reverse=False)
            carry_ref[0:1, st] = fr
            carry_ref[1:2, st] = fi
            xs = xs_ref[...].astype(BF16)
            xall_ref[:, j * 2 * SSM_TILE_ST:(j + 1) * 2 * SSM_TILE_ST] = xs
            yp = _dot(xs, c_ref[j]) + d_ref[:, ch] * up

            def store(i, rows, ch=ch):
                y_ref[i * per:(i + 1) * per, ch] = rows
                yg_ref[i * per:(i + 1) * per, ch] = _gelu(rows).astype(BF16)

            _to_time_order(yp, perm_ref, store)

    u_specs = [pl.BlockSpec((t_rows, SSM_TILE_CH), lambda c, k=k: (c, u_off // SSM_TILE_CH + k)) for k in range(nj)]
    full3 = lambda shape: pl.BlockSpec(shape, lambda c: (0, 0, 0))
    full2 = lambda shape: pl.BlockSpec(shape, lambda c: (0, 0))
    rows = pl.BlockSpec((t_rows, w), lambda c: (c, 0))
    outs, carried = _call(
        body, name="ssm_fwd", grid=(s // t_rows,),
        in_specs=u_specs + [full3(bmat.shape), full3(cmat.shape), full2(lam.shape), full2(dskip.shape)],
        out_specs=[rows, rows, pl.BlockSpec((1, 2, ns), lambda c: (c, 0, 0)), pl.BlockSpec((t_rows, 2 * ns), lambda c: (c, 0))],
        out_shape=[jax.ShapeDtypeStruct((s, w), F32), jax.ShapeDtypeStruct((s, w), BF16),
                   jax.ShapeDtypeStruct((s // t_rows, 2, ns), F32), jax.ShapeDtypeStruct((s, 2 * ns), BF16)],
        scratch_shapes=[pltpu.VMEM((2, ns), F32), pltpu.VMEM((nj, t_rows, 2 * SSM_TILE_ST), F32),
                        pltpu.VMEM((t_rows, 2 * SSM_TILE_ST), F32), pltpu.VMEM((t_rows, SSM_TILE_CH), F32)],
        semantics=("arbitrary",), args=[*([z] * nj), bmat, cmat, lam, dskip], side=side)
    return outs if side is None else (outs, carried)


def _ssm_bwd(z, y, dyg, xin, xall, bmat, cmat, lam, dskip, u_off, side=None):
    s, nj, w, ns, t_rows = _ssm_dims(z, bmat, u_off)
    nc = s // t_rows
    per = t_rows // SSM_SEGMENTS
    seg, half = SSM_SEGMENTS, SSM_TILE_ST

    def body(*refs):
        u_refs = refs[:nj]
        (y_ref, dyg_ref, xin_ref, xall_ref, b_ref, c_ref, lam_ref, d_ref, du_ref, db_ref, dc_ref, dlam_ref, dd_ref,
         carry_ref, w_ref, gs_ref, perm_ref, acc_ref) = refs[nj:]

        @pl.when(pl.program_id(0) == 0)
        def _():
            carry_ref[...] = jnp.zeros_like(carry_ref)
            db_ref[...] = jnp.zeros_like(db_ref)
            dc_ref[...] = jnp.zeros_like(dc_ref)
            dd_ref[...] = jnp.zeros_like(dd_ref)
            acc_ref[...] = jnp.zeros_like(acc_ref)
            _fill_powers(lam_ref, w_ref, nj, per)

        for j in range(nj):
            st = slice(j * SSM_TILE_ST, (j + 1) * SSM_TILE_ST)
            ch = slice(j * SSM_TILE_CH, (j + 1) * SSM_TILE_CH)
            lr, li = lam_ref[0:1, st], lam_ref[1:2, st]
            up = _to_segment_order(u_refs[j], perm_ref)
            upb = up.astype(BF16)
            dyp = _to_segment_order(dyg_ref[:, ch] * _gelu_grad(y_ref[:, ch]), perm_ref)
            dyb = dyp.astype(BF16)
            xs = xall_ref[:, j * 2 * half:(j + 1) * 2 * half]
            xf = xs.astype(F32)
            ends = xf[t_rows - seg:t_rows - 1, :]
            in_r = jnp.concatenate([xin_ref[0, 0:1, st], ends[:, :half]], axis=0)
            in_i = jnp.concatenate([xin_ref[0, 1:2, st], ends[:, half:]], axis=0)
            (gr, gi), _ = _segment_scan(_dot_nt(dyb, c_ref[j]), gs_ref, w_ref.at[j], lr, li,
                                        carry_ref[0:1, st], carry_ref[1:2, st], conj=True, reverse=True)
            carry_ref[0:1, st] = gr
            carry_ref[1:2, st] = gi
            gs = gs_ref[...]
            xsr, xsi, gsr, gsi = xf[:, :half], xf[:, half:], gs[:, :half], gs[:, half:]
            pxr = jnp.concatenate([in_r, xsr[:t_rows - seg]], axis=0)
            pxi = jnp.concatenate([in_i, xsi[:t_rows - seg]], axis=0)
            dl_r = gsr * pxr + gsi * pxi
            dl_i = gsi * pxr - gsr * pxi
            acc_ref[0, :, st] += jnp.sum(dl_r.reshape(per, seg, half), axis=0)
            acc_ref[1, :, st] += jnp.sum(dl_i.reshape(per, seg, half), axis=0)
            gx = gs.astype(BF16)
            dup = _dot_nt(gx, b_ref[j]) + d_ref[:, ch] * dyp

            def store(i, rows, ch=ch):
                du_ref[i * per:(i + 1) * per, ch] = rows.astype(BF16)

            _to_time_order(dup, perm_ref, store)
            db_ref[j] += _dot_tn(upb, gx)
            dc_ref[j] += _dot_tn(xs, dyb)
            dd_ref[:, ch] += jnp.sum(dyp * up, axis=0, keepdims=True)

        @pl.when(pl.program_id(0) == nc - 1)
        def _():
            dlam_ref[...] = jnp.sum(acc_ref[...], axis=1)

    rev = lambda c: nc - 1 - c
    u_specs = [pl.BlockSpec((t_rows, SSM_TILE_CH), lambda c, k=k: (rev(c), u_off // SSM_TILE_CH + k))
               for k in range(nj)]
    full3 = lambda shape: pl.BlockSpec(shape, lambda c: (0, 0, 0))
    full2 = lambda shape: pl.BlockSpec(shape, lambda c: (0, 0))
    rows = pl.BlockSpec((t_rows, w), lambda c: (rev(c), 0))
    outs, carried = _call(
        body, name="ssm_bwd", grid=(nc,),
        in_specs=u_specs + [rows, rows, pl.BlockSpec((1, 2, ns), lambda c: (rev(c), 0, 0)),
                            pl.BlockSpec((t_rows, 2 * ns), lambda c: (rev(c), 0)),
                            full3(bmat.shape), full3(cmat.shape), full2(lam.shape), full2(dskip.shape)],
        out_specs=[rows, full3(bmat.shape), full3(cmat.shape), full2(lam.shape), full2(dskip.shape)],
        out_shape=[jax.ShapeDtypeStruct((s, w), BF16), jax.ShapeDtypeStruct(bmat.shape, F32),
                   jax.ShapeDtypeStruct(cmat.shape, F32), jax.ShapeDtypeStruct(lam.shape, F32),
                   jax.ShapeDtypeStruct(dskip.shape, F32)],
        scratch_shapes=[pltpu.VMEM((2, ns), F32), pltpu.VMEM((nj, t_rows, 2 * SSM_TILE_ST), F32),
                        pltpu.VMEM((t_rows, 2 * SSM_TILE_ST), F32),
                        pltpu.VMEM((t_rows, SSM_TILE_CH), F32), pltpu.VMEM((2, SSM_SEGMENTS, ns), F32)],
        semantics=("arbitrary",), args=[*([z] * nj), y, dyg, xin, xall, bmat, cmat, lam, dskip], side=side)
    return outs if side is None else (outs, carried)


def _adam_math(w, g, m, v):
    m = ADAM_B1 * m + (1.0 - ADAM_B1) * g
    v = ADAM_B2 * v + (1.0 - ADAM_B2) * (g * g)
    m_hat = m / (1.0 - ADAM_B1 ** ADAM_STEP)
    v_hat = v / (1.0 - ADAM_B2 ** ADAM_STEP)
    delta = -ADAM_LR * (m_hat / (jnp.sqrt(v_hat) + ADAM_EPS) + ADAM_WD * w)
    return delta, m, v


def _adam_rows(r, c):
    for tr in (512, 256, 128, 64, 32, 16, 8):
        if r % tr == 0 and tr * c * 4 <= (1 << 20):
            return tr
    return r


def _adamw_big(w, p_mine, p_sib, m, v, name):
    r, c = w.shape
    tr = _adam_rows(r, c)

    def body(w_ref, a_ref, b_ref, m_ref, v_ref, g_ref, d_ref, nm_ref, nv_ref):
        g = a_ref[...] + b_ref[...]
        g_ref[...] = g
        d_ref[...], nm_ref[...], nv_ref[...] = _adam_math(w_ref[...], g, m_ref[...], v_ref[...])

    blk = pl.BlockSpec((tr, c), lambda i: (i, 0))
    return pl.pallas_call(body, name=f"adamw_{name}", grid=(r // tr,), in_specs=[blk] * 5, out_specs=[blk] * 4,
                          out_shape=[jax.ShapeDtypeStruct((r, c), F32)] * 4,
                          compiler_params=_params(("parallel",)))(w, p_mine, p_sib, m, v)


def _adamw_small(w, parts, m, v):
    r, c = w.shape
    n_dev = parts.shape[0]

    def body(w_ref, p_ref, m_ref, v_ref, g_ref, d_ref, nm_ref, nv_ref):
        g = p_ref[0]
        for k in range(1, n_dev):
            g = g + p_ref[k]
        g_ref[...] = g
        d_ref[...], nm_ref[...], nv_ref[...] = _adam_math(w_ref[...], g, m_ref[...], v_ref[...])

    blk = pl.BlockSpec((r, c), lambda i: (0, 0))
    return pl.pallas_call(body, name="adamw_small", grid=(1,),
                          in_specs=[blk, pl.BlockSpec((n_dev, r, c), lambda i: (0, 0, 0)), blk, blk],
                          out_specs=[blk] * 4, out_shape=[jax.ShapeDtypeStruct((r, c), F32)] * 4,
                          compiler_params=_params(("arbitrary",)))(w, parts, m, v)


def _cast_bf16(w, name):
    r, c = w.shape
    tr = _adam_rows(r, c)

    def body(w_ref, o_ref):
        o_ref[...] = w_ref[...].astype(BF16)

    blk = pl.BlockSpec((tr, c), lambda i: (i, 0))
    return pl.pallas_call(body, name=f"cast_{name}", grid=(r // tr,), in_specs=[blk], out_specs=blk,
                          out_shape=jax.ShapeDtypeStruct((r, c), BF16), compiler_params=_params(("parallel",)))(w)


def _sum_slots(recv, name):
    _, r, c = recv.shape
    tr = _adam_rows(r, c)

    def body(p_ref, o_ref):
        acc = p_ref[0].astype(F32)
        for k in range(1, N_CHIPS):
            acc = acc + p_ref[k].astype(F32)
        o_ref[...] = acc

    return pl.pallas_call(body, name=f"sum_{name}", grid=(r // tr,),
                          in_specs=[pl.BlockSpec((N_CHIPS, tr, c), lambda i: (0, i, 0))],
                          out_specs=pl.BlockSpec((tr, c), lambda i: (i, 0)),
                          out_shape=jax.ShapeDtypeStruct((r, c), F32), compiler_params=_params(("parallel",)))(recv)


BIG_WEIGHTS = ("w_in", "w_attn_up", "w_glu_v", "w_glu_g", "w_out", "w_ffn_gate", "w_ffn_up", "w_ffn_down")
COL_SHARDED = ("w_in", "w_attn_up", "w_glu_v", "w_glu_g", "w_ffn_gate", "w_ffn_up")


def _aligned(v, m):
    return v if isinstance(v, int) else pl.multiple_of(v, m)


def _shard_of(ref, name, j, shard_shape, half=None):
    r, c = shard_shape
    rows = r if half is None else r // 2
    row0 = 0 if half is None else half * rows
    if name in COL_SHARDED:
        return ref.at[pl.ds(_aligned(row0, 16), rows), pl.ds(_aligned(j * c, 128), c)]
    return ref.at[pl.ds(_aligned(j * r + row0, 16), rows), :]


def _other_chips():
    x, y = lax.axis_index("x"), lax.axis_index("y")
    return [(1 - x, y), (x, 1 - y), (1 - x, 1 - y)]


def _dma_sems(n, arrays):
    return [pltpu.SemaphoreType.DMA((n, 3))] * arrays + [pltpu.SemaphoreType.DMA((n,))]


def _gather_side(shards):
    names = list(shards)
    n = len(names)
    full_shapes = []
    for k in names:
        r, c = shards[k].shape
        full_shapes.append((r, c * N_CHIPS) if k in COL_SHARDED else (r * N_CHIPS, c))

    def build(src, dst, sems):
        send_sems, recv_sems, pass_send_sems, pass_recv_sems, local_sems = sems
        x, y, c = lax.axis_index("x"), lax.axis_index("y"), lax.axis_index("c")
        me = 2 * x + y
        locals_, sends, arrivals, forwards, passed_on = [], [], [], [], []
        for i, k in enumerate(names):
            shape = shards[k].shape
            half_rows = shape[0] // 2
            locals_.append(pltpu.make_async_copy(src[i], _shard_of(dst[i], k, me, shape), local_sems.at[i]))
            my_half = src[i].at[pl.ds(_aligned(c * half_rows, 16), half_rows), :]
            for p, (px, py) in enumerate(_other_chips()):
                peer = 2 * px + py
                landed = _shard_of(dst[i], k, peer, shape, half=c)
                sends.append(pltpu.make_async_remote_copy(
                    src_ref=my_half, dst_ref=_shard_of(dst[i], k, me, shape, half=c), send_sem=send_sems.at[i, p],
                    recv_sem=recv_sems.at[i, p], device_id=(px, py, c), device_id_type=MESH))
                arrivals.append(pltpu.make_async_remote_copy(
                    src_ref=my_half, dst_ref=landed, send_sem=send_sems.at[i, p],
                    recv_sem=recv_sems.at[i, p], device_id=(px, py, c), device_id_type=MESH))
                forwards.append(pltpu.make_async_remote_copy(
                    src_ref=landed, dst_ref=landed, send_sem=pass_send_sems.at[i, p],
                    recv_sem=pass_recv_sems.at[i, p], device_id=(x, y, 1 - c), device_id_type=MESH))
                passed_on.append(pltpu.make_async_remote_copy(
                    src_ref=landed, dst_ref=_shard_of(dst[i], k, peer, shape, half=1 - c),
                    send_sem=pass_send_sems.at[i, p], recv_sem=pass_recv_sems.at[i, p],
                    device_id=(x, y, 1 - c), device_id_type=MESH))
        return locals_, sends, arrivals, forwards, passed_on

    return _Side([shards[k] for k in names], [jax.ShapeDtypeStruct(s, BF16) for s in full_shapes], _dma_sems(n, 4), build,
                 relays=True)


def _scatter_side(grads, shard_shapes):
    names = list(grads)
    n = len(names)

    def build(src, dst, sems):
        send_sems, recv_sems, local_sems = sems
        x, y, c = lax.axis_index("x"), lax.axis_index("y"), lax.axis_index("c")
        me = 2 * x + y
        locals_, sends, arrivals = [], [], []
        for i, k in enumerate(names):
            shape = shard_shapes[k]
            locals_.append(pltpu.make_async_copy(_shard_of(src[i], k, me, shape), dst[i].at[me], local_sems.at[i]))
            for p, (px, py) in enumerate(_other_chips()):
                peer = 2 * px + py
                sends.append(pltpu.make_async_remote_copy(
                    src_ref=_shard_of(src[i], k, peer, shape), dst_ref=dst[i].at[me], send_sem=send_sems.at[i, p],
                    recv_sem=recv_sems.at[i, p], device_id=(px, py, c), device_id_type=MESH))
                arrivals.append(pltpu.make_async_remote_copy(
                    src_ref=_shard_of(src[i], k, peer, shape), dst_ref=dst[i].at[peer], send_sem=send_sems.at[i, p],
                    recv_sem=recv_sems.at[i, p], device_id=(px, py, c), device_id_type=MESH))
        return locals_, sends, arrivals, [None] * len(arrivals), []

    return _Side([grads[k] for k in names],
                 [jax.ShapeDtypeStruct((N_CHIPS,) + tuple(shard_shapes[k]), BF16) for k in names], _dma_sems(n, 2), build)


def _put_cols(dz, src, col_off):
    s, w = src.shape
    tr = _pick(s, (2048, 1024, 512, 256, 128, 64, 8))
    tc = _pick(math.gcd(w, col_off), (1024, 512, 256, 128))
    off = col_off // tc

    def body(src_ref, dz_ref, o_ref):
        del dz_ref
        o_ref[...] = src_ref[...].astype(o_ref.dtype)

    return pl.pallas_call(
        body, name="put_cols", grid=(s // tr, w // tc),
        in_specs=[pl.BlockSpec((tr, tc), lambda i, j: (i, j)), pl.BlockSpec(memory_space=pl.ANY)],
        out_specs=pl.BlockSpec((tr, tc), lambda i, j: (i, off + j)),
        out_shape=jax.ShapeDtypeStruct(dz.shape, dz.dtype), input_output_aliases={1: 0},
        compiler_params=_params(("parallel", "parallel")))(src, dz)


def _swap_side(parts):
    n = len(parts)

    def build(src, dst, sems):
        send_sems, recv_sems = sems
        sibling = (lax.axis_index("x"), lax.axis_index("y"), 1 - lax.axis_index("c"))
        copies = [pltpu.make_async_remote_copy(src_ref=src[i], dst_ref=dst[i], send_sem=send_sems.at[i],
                                               recv_sem=recv_sems.at[i], device_id=sibling, device_id_type=MESH)
                  for i in range(n)]
        return [], copies, copies, [None] * n, []

    return _Side(parts, [jax.ShapeDtypeStruct(p.shape, F32) for p in parts],
                 [pltpu.SemaphoreType.DMA((n,)), pltpu.SemaphoreType.DMA((n,))], build)


def _share_side(packed):
    r, c = packed.shape

    def build(src, dst, sems):
        send_sems, recv_sems, local_sem = sems
        x, y, cc = lax.axis_index("x"), lax.axis_index("y"), lax.axis_index("c")
        me = 4 * x + 2 * y + cc
        own = pltpu.make_async_copy(src[0], dst[0].at[me], local_sem)
        sends, arrivals = [], []
        flips = [(fx, fy, fc) for fx in range(2) for fy in range(2) for fc in range(2) if fx or fy or fc]
        for p, (fx, fy, fc) in enumerate(flips):
            px, py, pc = x ^ fx, y ^ fy, cc ^ fc
            sends.append(pltpu.make_async_remote_copy(
                src_ref=src[0], dst_ref=dst[0].at[me], send_sem=send_sems.at[p], recv_sem=recv_sems.at[p],
                device_id=(px, py, pc), device_id_type=MESH))
            arrivals.append(pltpu.make_async_remote_copy(
                src_ref=src[0], dst_ref=dst[0].at[4 * px + 2 * py + pc], send_sem=send_sems.at[p],
                recv_sem=recv_sems.at[p], device_id=(px, py, pc), device_id_type=MESH))
        return [own], sends, arrivals, [None] * len(arrivals), []

    return _Side([packed], [jax.ShapeDtypeStruct((8, r, c), F32)],
                 [pltpu.SemaphoreType.DMA((7,)), pltpu.SemaphoreType.DMA((7,)), pltpu.SemaphoreType.DMA], build)


SMALL_WEIGHTS = ("norm_mix_pre", "ssm_a_re", "ssm_a_im", "ssm_log_dt", "ssm_b_re", "ssm_b_im", "ssm_c_re", "ssm_c_im",
                 "ssm_d", "norm_mix_post", "norm_ffn_pre", "norm_ffn_post")
WEIGHT_ORDER = ("norm_mix_pre", "w_in", "w_attn_up", "ssm_a_re", "ssm_a_im", "ssm_log_dt", "ssm_b_re", "ssm_b_im",
                "ssm_c_re", "ssm_c_im", "ssm_d", "w_glu_v", "w_glu_g", "w_out", "norm_mix_post", "norm_ffn_pre",
                "w_ffn_gate", "w_ffn_up", "w_ffn_down", "norm_ffn_post")
PACK_LANES = 128
PACK_ROWS = 8
PACK_GROUPS = (SMALL_WEIGHTS[:1], SMALL_WEIGHTS[1:])


def _pack_group(arrs, names):
    flat = jnp.concatenate([arrs[k].reshape(-1) for k in names])
    pad = -flat.shape[0] % (PACK_LANES * PACK_ROWS)
    return jnp.pad(flat, (0, pad)).reshape(-1, PACK_LANES)


def _pack_small(arrs):
    return jnp.concatenate([_pack_group(arrs, names) for names in PACK_GROUPS], axis=0)


def _unpack_small(packed, like):
    out, row = {}, 0
    for names in PACK_GROUPS:
        rows = _pack_group(like, names).shape[0]
        flat, pos = packed[row:row + rows].reshape(-1), 0
        for k in names:
            n = like[k].size
            out[k] = flat[pos:pos + n].reshape(like[k].shape)
            pos += n
        row += rows
    return out


def _local_step(x, target, big, small, shards=None, shard_shapes=None, h1=None):
    s, d = x.shape
    big, grads, slots = dict(big), {}, {}
    carry = shards is not None

    def gathering(names, call):
        if not carry:
            return call(None)
        res, got = call(_gather_side({k: shards[k] for k in names}))
        big.update(zip(names, got))
        return res

    def scattering(names, call):
        if not carry:
            return call(None)
        res, got = call(_scatter_side({k: grads[k] for k in names}, shard_shapes))
        slots.update(zip(names, got))
        return res

    u_off = 3 * HQ
    gate_off = u_off + d // 2
    g1, g2, g3, g4 = (small[k][0:1] for k in ("norm_mix_pre", "norm_mix_post", "norm_ffn_pre", "norm_ffn_post"))
    ssm_names = ("ssm_a_re", "ssm_a_im", "ssm_log_dt", "ssm_b_re", "ssm_b_im", "ssm_c_re", "ssm_c_im")
    (lam, bmat, cmat), ssm_vjp = jax.vjp(_ssm_prepare, *[small[k][0] for k in ssm_names])
    bmat, cmat = bmat.astype(BF16), cmat.astype(BF16)
    dskip = small["ssm_d"][0:1]

    if h1 is None:
        h1 = _norm_in(x, g1)
    z = gathering(("w_attn_up", "w_glu_v", "w_glu_g", "w_out", "w_ffn_gate"),
                  lambda side: _mm(h1, big["w_in"], "nn", F32, "in_proj", side=side))
    y, yg, xin, xall = gathering(("w_ffn_up",), lambda side: _ssm_fwd(z, bmat, cmat, lam, dskip, u_off, side=side))
    qkv = [_dilate_qkv(z, g, dil) for g, dil in enumerate(ATTN_DILATIONS)]
    outs, lses = zip(*[_attn_fwd(qkv[g], g, dil) for g, dil in enumerate(ATTN_DILATIONS)])
    attn = _attn_merge(outs, lses)
    merged, ab, gv, gg = _mm_fused(
        [attn, yg], [big["w_attn_up"], big["w_glu_v"], big["w_glu_g"]], [(0, 0), (1, 1), (1, 2)], "nn",
        [BF16, BF16, BF16, BF16], "branches_merge", extras=[(z, gate_off), (z, gate_off + d)], epilogue=_gates_epilogue)
    mo = _mm(merged, big["w_out"], "nn", F32, "mix_out")
    x2, h2 = _norm_mid(x, mo, g2, g3)
    act, fg, fu = gathering(("w_ffn_down",), lambda side: _mm_fused(
        [h2], [big["w_ffn_gate"], big["w_ffn_up"]], [(0, 0), (0, 1)], "nn", [BF16, BF16, BF16], "ffn_up_act",
        epilogue=_swiglu_epilogue, side=side))
    f = _mm(act, big["w_ffn_down"], "nn", F32, "ffn_down")
    loss, dout, df, dg4 = _loss_head(x2, f, g4, target)

    grads["w_ffn_down"] = _mm_kloop(act, df, "tn", BF16, "dw_ffn_down")
    dfg, dfu = scattering(("w_ffn_down",), lambda side: _mm_fused(
        [df], [big["w_ffn_down"]], [(0, 0)], "nt", [BF16, BF16], "d_ffn_act", extras=[(fg, 0), (fu, 0)],
        epilogue=_swiglu_bwd_epilogue, side=side))
    grads["w_ffn_gate"] = _mm_kloop(h2, dfg, "tn", BF16, "dw_ffn_gate")
    dh2 = scattering(("w_ffn_gate",), lambda side: _mm_fused(
        [dfg, dfu], [big["w_ffn_gate"], big["w_ffn_up"]], [(0, 0), (1, 1)], "nt", [F32], "d_h2",
        epilogue=_sum_epilogue, side=side))[0]
    grads["w_ffn_up"] = _mm_kloop(h2, dfu, "tn", BF16, "dw_ffn_up")
    dx2, dmo, dg2, dg3 = _norm_mid_bwd(x2, mo, g2, g3, dout, dh2)
    dz, dgs, dab, dgv, dgg = _mm_fused(
        [dmo], [big["w_out"]], [(0, 0)], "nt", [BF16] * 5, "d_merged_gates",
        extras=[(z, gate_off), (z, gate_off + d), (ab, 0), (gv, 0), (gg, 0)], epilogue=_gates_bwd_epilogue,
        out_place=[(z.shape[1], gate_off), None, None, None, None])
    dz = _put_cols(dz, dgs, gate_off + d)
    grads["w_out"] = _mm_kloop(merged, dmo, "tn", BF16, "dw_out")
    dyg = _mm_fused([dgv, dgg], [big["w_glu_v"], big["w_glu_g"]], [(0, 0), (1, 1)], "nt", [F32], "d_yg",
                    epilogue=_sum_epilogue)[0]
    grads["w_glu_v"] = _mm_kloop(yg, dgv, "tn", BF16, "dw_glu_v")
    grads["w_glu_g"] = _mm_kloop(yg, dgg, "tn", BF16, "dw_glu_g")
    du, dbmat, dcmat, dlam, dd = scattering(
        ("w_ffn_up",),
        lambda side: _ssm_bwd(z, y, dyg, xin, xall, bmat, cmat, lam, dskip, u_off, side=side))
    dz = _put_cols(dz, du, u_off)
    dattn = _mm(dab, big["w_attn_up"], "nt", F32, "d_attn")
    grads["w_attn_up"] = _mm_kloop(attn, dab, "tn", BF16, "dw_attn_up")
    merged_bwd = _attn_merge_bwd(outs, lses, dattn)
    mine, theirs = {}, {}
    for g, dil in enumerate(ATTN_DILATIONS):
        side = None
        if carry and g == 0:
            mine = {k: _sum_slots(slots[k], k) for k in slots}
            side = _swap_side(list(mine.values()))
        dqkv = _attn_bwd(qkv[g], merged_bwd[g], lses[g], merged_bwd[3 + g], g, dil, side=side)
        if side is not None:
            dqkv, got = dqkv
            theirs = dict(zip(mine, got))
        dz = _undilate_dqkv(dqkv, dz, g, dil)
    small_grads = dict(zip(ssm_names, (t[None] for t in ssm_vjp((dlam, dbmat, dcmat)))))
    small_grads.update(norm_mix_post=dg2, norm_ffn_pre=dg3, norm_ffn_post=dg4, ssm_d=dd)
    if carry:
        late = ("w_attn_up", "w_out", "w_glu_v", "w_glu_g")
        side = _join_sides(_scatter_side({k: grads[k] for k in late}, shard_shapes),
                           _share_side(_pack_group(small_grads, PACK_GROUPS[1])))
        grads["w_in"], got = _mm_kloop(h1, dz, "tn", BF16, "dw_in", side=side)
        slots.update(zip(late, got[:-1]))
        shared = got[-1]
    else:
        grads["w_in"] = _mm_kloop(h1, dz, "tn", BF16, "dw_in")
    dh1 = scattering(("w_in",), lambda side: _mm_kloop(dz, big["w_in"], "nt", F32, "d_h1", side=side))
    grad_x, dg1 = _norm_in_bwd(x, g1, dh1, dx2)
    small_grads["norm_mix_pre"] = dg1
    if carry:
        return loss[0, 0], grad_x, (slots, mine, theirs), (dg1, shared)
    return loss[0, 0], grad_x, grads, small_grads


def kernel(x, norm_mix_pre, w_in, w_attn_up, ssm_a_re, ssm_a_im, ssm_log_dt, ssm_b_re, ssm_b_im, ssm_c_re, ssm_c_im, ssm_d, w_glu_v, w_glu_g, w_out, norm_mix_post, norm_ffn_pre, w_ffn_gate, w_ffn_up, w_ffn_down, norm_ffn_post, loss_target, m_norm_mix_pre, m_w_in, m_w_attn_up, m_ssm_a_re, m_ssm_a_im, m_ssm_log_dt, m_ssm_b_re, m_ssm_b_im, m_ssm_c_re, m_ssm_c_im, m_ssm_d, m_w_glu_v, m_w_glu_g, m_w_out, m_norm_mix_post, m_norm_ffn_pre, m_w_ffn_gate, m_w_ffn_up, m_w_ffn_down, m_norm_ffn_post, v_norm_mix_pre, v_w_in, v_w_attn_up, v_ssm_a_re, v_ssm_a_im, v_ssm_log_dt, v_ssm_b_re, v_ssm_b_im, v_ssm_c_re, v_ssm_c_im, v_ssm_d, v_w_glu_v, v_w_glu_g, v_w_out, v_norm_mix_post, v_norm_ffn_pre, v_w_ffn_gate, v_w_ffn_up, v_w_ffn_down, v_norm_ffn_post):
    given = dict(locals())
    w = {k: given[k] for k in WEIGHT_ORDER}
    m = {k: given["m_" + k] for k in WEIGHT_ORDER}
    v = {k: given["v_" + k] for k in WEIGHT_ORDER}

    shard_shapes = {k: w[k].shape[1:] for k in BIG_WEIGHTS}
    shards = {"w_in": _cast_bf16(w["w_in"][0], "w_in")}
    h1, casts, got = _prologue(x[0], norm_mix_pre[0:1], {k: w[k][0] for k in BIG_WEIGHTS if k != "w_in"},
                               side=_gather_side({"w_in": shards["w_in"]}))
    shards.update(casts)
    big = {"w_in": got[0]}

    loss, grad_x, (slots, mine, theirs), small_grads = _local_step(
        x[0], loss_target[0], big, {k: w[k] for k in SMALL_WEIGHTS}, shards, shard_shapes, h1)
    loss = lax.psum(loss, MESH_AXES)

    last = [k for k in BIG_WEIGHTS if k not in mine]
    mine.update({k: _sum_slots(slots[k], k) for k in last})
    theirs.update(zip(last, _run_side(_swap_side([mine[k] for k in last]), "swap_last_grads")))
    out_g, out_d, out_m, out_v = {}, {}, {}, {}
    for k in BIG_WEIGHTS:
        res = _adamw_big(w[k][0], mine[k], theirs[k], m[k][0], v[k][0], k)
        out_g[k], out_d[k], out_m[k], out_v[k] = (t[None] for t in res)

    pick = lambda tree: {k: tree[k] for k in SMALL_WEIGHTS}
    dg1, shared = small_grads
    late = _run_side(_share_side(_pack_group({"norm_mix_pre": dg1}, PACK_GROUPS[0])), "share_last_grad")[0]
    parts = jnp.concatenate([late, shared], axis=1)
    res = _adamw_small(_pack_small(pick(w)), parts, _pack_small(pick(m)), _pack_small(pick(v)))
    for dst, packed in zip((out_g, out_d, out_m, out_v), res):
        dst.update(_unpack_small(packed, pick(w)))

    return (loss, grad_x[None], *[out_g[k] for k in WEIGHT_ORDER], *[out_d[k] for k in WEIGHT_ORDER],
            *[out_m[k] for k in WEIGHT_ORDER], *[out_v[k] for k in WEIGHT_ORDER])
```

```python
import functools
import math

import jax
import jax.numpy as jnp
from jax import lax
from jax.experimental import pallas as pl
from jax.experimental.pallas import tpu as pltpu

F32 = jnp.float32
BF16 = jnp.bfloat16

EPS = 1e-6
HEAD_DIM = 128
HEADS_PER_GROUP = 4
ATTN_DILATIONS = (1, 4, 16)
ATTN_BLK = 128
N_ATTN_HEADS = HEADS_PER_GROUP * len(ATTN_DILATIONS)
GROUP_W = HEADS_PER_GROUP * HEAD_DIM
HQ = N_ATTN_HEADS * HEAD_DIM
SSM_GROUP = 16
SSM_STATE = 64
SSM_TILE_CH = 128
SSM_TILE_ST = SSM_TILE_CH // SSM_GROUP * SSM_STATE
ADAM_LR = 0.001
ADAM_B1 = 0.9
ADAM_B2 = 0.999
ADAM_EPS = 1e-08
ADAM_WD = 0.01
ADAM_STEP = 10
NEG_BIG = -1e30
V7X_VMEM_LIMIT = 56 * 1024 * 1024
MESH_AXES = ("x", "y", "c")
N_CHIPS = 4


def _pick(n, cands):
    for c in cands:
        if n % c == 0:
            return c
    raise ValueError(f"no tile of {cands} divides {n}")


def _params(sem):
    return pltpu.CompilerParams(dimension_semantics=sem, vmem_limit_bytes=V7X_VMEM_LIMIT)


HBM = pl.BlockSpec(memory_space=pl.ANY)
MESH = pl.DeviceIdType.MESH


class _Side:
    def __init__(self, srcs, out_shapes, sem_shapes, build, aliases=None, relays=False):
        self.srcs, self.out_shapes, self.sem_shapes, self.build = list(srcs), list(out_shapes), list(sem_shapes), build
        self.aliases = dict(aliases or {})
        self.relays = relays

    def start(self, src, dst, sems):
        local, sends = self.build(src, dst, sems)[:2]
        for cp in local + sends:
            cp.start()

    def relay(self, src, dst, sems):
        _, _, arrivals, forwards, _ = self.build(src, dst, sems)
        for cp, forward in zip(arrivals, forwards):
            if forward is not None:
                cp.wait_recv()
                forward.start()

    def wait(self, src, dst, sems, relayed=False):
        local, sends, arrivals, forwards, passed_on = self.build(src, dst, sems)
        for cp, forward in zip(arrivals, forwards):
            if forward is None:
                cp.wait_recv()
            elif not relayed:
                cp.wait_recv()
                forward.start()
        for cp in passed_on:
            cp.wait_recv()
        for cp in sends + [f for f in forwards if f is not None]:
            cp.wait_send()
        for cp in local:
            cp.wait()


def _join_sides(a, b):
    ns, no, nm = len(a.srcs), len(a.out_shapes), len(a.sem_shapes)

    def build(src, dst, sems):
        ra, rb = a.build(src[:ns], dst[:no], sems[:nm]), b.build(src[ns:], dst[no:], sems[nm:])
        return tuple(p + q for p, q in zip(ra, rb))

    aliases = {**a.aliases, **{ns + k: no + v for k, v in b.aliases.items()}}
    return _Side(a.srcs + b.srcs, a.out_shapes + b.out_shapes, a.sem_shapes + b.sem_shapes, build, aliases,
                 a.relays or b.relays)


def _call(body, *, name, grid, in_specs, out_specs, out_shape, semantics, args, scratch_shapes=(), side=None, **kw):
    in_specs, out_specs, out_shape, scratch_shapes = list(in_specs), list(out_specs), list(out_shape), list(scratch_shapes)
    if side is None:
        res = pl.pallas_call(body, name=name, grid=grid, in_specs=in_specs, out_specs=out_specs, out_shape=out_shape,
                             scratch_shapes=scratch_shapes, compiler_params=_params(semantics), **kw)(*args)
        return list(res), []
    n_in, n_out, n_scr = len(in_specs), len(out_specs), len(scratch_shapes)
    ns_in, ns_out = len(side.srcs), len(side.out_shapes)
    n_steps = math.prod(grid)
    relay_at = (3 * n_steps) // 4 if side.relays and n_steps >= 4 else None

    def carrying(*refs):
        ins, s_in = refs[:n_in], refs[n_in:n_in + ns_in]
        o0 = n_in + ns_in
        outs, s_out = refs[o0:o0 + n_out], refs[o0 + n_out:o0 + n_out + ns_out]
        c0 = o0 + n_out + ns_out
        scr, sems = refs[c0:c0 + n_scr], refs[c0 + n_scr:]
        step = functools.reduce(lambda acc, ig: acc * ig[1] + pl.program_id(ig[0]), enumerate(grid), 0)

        @pl.when(step == 0)
        def _():
            side.start(s_in, s_out, sems)

        if relay_at is not None:
            @pl.when(step == relay_at)
            def _():
                side.relay(s_in, s_out, sems)

        body(*ins, *outs, *scr)

        @pl.when(step == n_steps - 1)
        def _():
            side.wait(s_in, s_out, sems, relayed=relay_at is not None)

    res = pl.pallas_call(
        carrying, name=name, grid=grid, in_specs=in_specs + [HBM] * ns_in, out_specs=out_specs + [HBM] * ns_out,
        out_shape=out_shape + side.out_shapes, scratch_shapes=scratch_shapes + side.sem_shapes,
        input_output_aliases={n_in + k: n_out + v for k, v in side.aliases.items()},
        compiler_params=pltpu.CompilerParams(dimension_semantics=("arbitrary",) * len(grid),
                                             vmem_limit_bytes=V7X_VMEM_LIMIT, has_side_effects=True), **kw,
    )(*args, *side.srcs)
    return list(res[:n_out]), list(res[n_out:])


_DOT_DIMS = {"nn": (((1,), (0,)), ((), ())), "nt": (((1,), (1,)), ((), ())), "tn": (((0,), (0,)), ((), ()))}


MM_VMEM_BUDGET = 44 * 1024 * 1024
MM_STEP_BYTES = 1 << 20
MM_ACC_BYTES = 4
MM_EPILOGUE_COLS = 256


def _size(dtype):
    return jnp.dtype(dtype).itemsize


def _mm_fused(as_, bs, pairs, mode, out_dtypes, name, extras=(), epilogue=None, side=None, out_place=None):
    M = as_[0].shape[0]
    N = bs[0].shape[1] if mode == "nn" else bs[0].shape[0]
    ks_a = [a.shape[1] for a in as_]
    ks_b = [b.shape[0] if mode == "nn" else b.shape[1] for b in bs]
    chunked = epilogue is not None
    if epilogue is None:
        epilogue = lambda rs, es: rs
    offs = [off for _, off in extras]
    place = list(out_place) if out_place else [None] * len(out_dtypes)
    offs_all = offs + [p[1] for p in place if p is not None]
    best = None
    for tm in (2048, 1024, 512, 256, 128):
        for tn in (2048, 1024, 512, 256, 128):
            if M % tm or N % tn or any(off % tn for off in offs_all):
                continue
            vmem = (sum(2 * tm * k * 2 for k in ks_a) + sum(2 * k * tn * 2 for k in ks_b)
                    + sum(2 * tm * tn * _size(d) for d in out_dtypes) + sum(2 * tm * tn * _size(e.dtype) for e, _ in extras)
                    + len(pairs) * tm * tn * 4)
            cost = sum(k * N * 2 for k in ks_b) * (M // tm) + (M // tm) * (N // tn) * MM_STEP_BYTES
            if vmem <= MM_VMEM_BUDGET and (best is None or cost < best[0]):
                best = (cost, tm, tn)
    _, tm, tn = best
    na, nb, ne, no = len(as_), len(bs), len(extras), len(out_dtypes)
    dims = _DOT_DIMS[mode]

    sub = MM_EPILOGUE_COLS if chunked and tn % MM_EPILOGUE_COLS == 0 else tn

    def body(*refs):
        a_refs, b_refs = refs[:na], refs[na:na + nb]
        e_refs, o_refs = refs[na + nb:na + nb + ne], refs[na + nb + ne:]
        for c0 in range(0, tn, sub):
            cs = slice(c0, c0 + sub)
            rs = [lax.dot_general(a_refs[ai][...], b_refs[bi][:, cs] if mode == "nn" else b_refs[bi][cs, :], dims,
                                  preferred_element_type=F32) for ai, bi in pairs]
            outs = epilogue(rs, [e[:, cs] for e in e_refs])
            for o_ref, o in zip(o_refs, outs):
                o_ref[:, cs] = o.astype(o_ref.dtype)

    a_specs = [pl.BlockSpec((tm, k), lambda i, j: (i, 0)) for k in ks_a]
    if mode == "nn":
        b_specs = [pl.BlockSpec((k, tn), lambda i, j: (0, j)) for k in ks_b]
    else:
        b_specs = [pl.BlockSpec((tn, k), lambda i, j: (j, 0)) for k in ks_b]
    e_specs = [pl.BlockSpec((tm, tn), lambda i, j, o=off // tn: (i, o + j)) for off in offs]
    o_specs = [pl.BlockSpec((tm, tn), lambda i, j, o=(p[1] // tn if p else 0): (i, o + j)) for p in place]
    outs, carried = _call(
        body, name=name, grid=(M // tm, N // tn), in_specs=a_specs + b_specs + e_specs, out_specs=o_specs,
        out_shape=[jax.ShapeDtypeStruct((M, p[0] if p else N), d) for d, p in zip(out_dtypes, place)],
        semantics=("parallel", "arbitrary"),
        args=[*as_, *bs, *[e for e, _ in extras]], side=side)
    return outs if side is None else (outs, carried)


def _mm(a, b, mode, out_dtype, name, side=None):
    res = _mm_fused([a], [b], [(0, 0)], mode, [out_dtype], name, side=side)
    return res[0] if side is None else (res[0][0], res[1])


def _mm_kloop(a, b, mode, out_dtype, name, second=None, side=None):
    if mode == "nn":
        (M, K), (_, N) = a.shape, b.shape
    elif mode == "nt":
        (M, K), (N, _) = a.shape, b.shape
    else:
        (K, M), (_, N) = a.shape, b.shape
    products = 1 if second is None else 2
    best = None
    for tm in (2816, 2048, 1408, 1024, 512, 256, 128):
        for tn in (2816, 2432, 2048, 1408, 1024, 512, 256, 128):
            for tk in (2816, 2432, 2048, 1408, 1024, 512, 256, 128):
                if M % tm or N % tn or K % tk:
                    continue
                vmem = 2 * tm * tn * 4 + 2 * tm * tn * _size(out_dtype) + products * 2 * tk * (tm + tn) * 2
                steps = (M // tm) * (N // tn) * (K // tk)
                cost = (K * M * 2 * (N // tn) + K * N * 2 * (M // tm) + steps * MM_STEP_BYTES
                        + steps * tm * tn * MM_ACC_BYTES)
                if vmem <= MM_VMEM_BUDGET and (best is None or cost < best[0]):
                    best = (cost, tm, tn, tk)
    _, tm, tn, tk = best
    nk = K // tk
    dims = _DOT_DIMS[mode]

    def body(*refs):
        o_ref, acc_ref = refs[-2:]
        k = pl.program_id(2)

        @pl.when(k == 0)
        def _():
            acc_ref[...] = jnp.zeros_like(acc_ref)

        for p in range(products):
            @pl.when(jnp.logical_and(k >= p * nk, k < (p + 1) * nk))
            def _(p=p):
                acc_ref[...] += lax.dot_general(refs[2 * p][...], refs[2 * p + 1][...], dims, preferred_element_type=F32)

        @pl.when(k == products * nk - 1)
        def _():
            o_ref[...] = acc_ref[...].astype(o_ref.dtype)

    def a_spec(p):
        kk = lambda k: jnp.clip(k - p * nk, 0, nk - 1)
        if mode == "tn":
            return pl.BlockSpec((tk, tm), lambda i, j, k: (kk(k), i))
        return pl.BlockSpec((tm, tk), lambda i, j, k: (i, kk(k)))

    def b_spec(p):
        kk = lambda k: jnp.clip(k - p * nk, 0, nk - 1)
        if mode == "nt":
            return pl.BlockSpec((tn, tk), lambda i, j, k: (j, kk(k)))
        return pl.BlockSpec((tk, tn), lambda i, j, k: (kk(k), j))

    o_spec = pl.BlockSpec((tm, tn), lambda i, j, k: (i, j))
    operands = (a, b) + (tuple(second) if second is not None else ())
    outs, carried = _call(
        body, name=name, grid=(M // tm, N // tn, products * nk),
        in_specs=[spec(p) for p in range(products) for spec in (a_spec, b_spec)], out_specs=[o_spec],
        out_shape=[jax.ShapeDtypeStruct((M, N), out_dtype)], scratch_shapes=[pltpu.VMEM((tm, tn), F32)],
        semantics=("parallel", "parallel", "arbitrary"), args=operands, side=side)
    return outs[0] if side is None else (outs[0], carried)


def _sigmoid(v):
    return 0.5 * jnp.tanh(0.5 * v) + 0.5


_GELU_C = math.sqrt(2.0 / math.pi)


def _gelu(v):
    return 0.5 * v * (1.0 + jnp.tanh(_GELU_C * (v + 0.044715 * v * v * v)))


def _gelu_grad(v):
    t = jnp.tanh(_GELU_C * (v + 0.044715 * v * v * v))
    return 0.5 * (1.0 + t) + 0.5 * v * (1.0 - t * t) * _GELU_C * (1.0 + 3.0 * 0.044715 * v * v)


def _rms(v, gain):
    r = lax.rsqrt(jnp.mean(v * v, axis=-1, keepdims=True) + EPS)
    return v * r * gain


def _rms_bwd(v, gain, dy):
    r = lax.rsqrt(jnp.mean(v * v, axis=-1, keepdims=True) + EPS)
    a = dy * gain
    dv = r * a - v * (r * r * r) * jnp.mean(a * v, axis=-1, keepdims=True)
    return dv, dy * v * r


def _row_tile(s):
    return _pick(s, (256, 128, 64, 8))


def _norm_in(x, gain):
    s, d = x.shape
    tr = _row_tile(s)

    def body(x_ref, g_ref, h_ref):
        h_ref[...] = _rms(x_ref[...], g_ref[...]).astype(BF16)

    row = pl.BlockSpec((tr, d), lambda i: (i, 0))
    vec = pl.BlockSpec((1, d), lambda i: (0, 0))
    return pl.pallas_call(body, name="norm_in", grid=(s // tr,), in_specs=[row, vec], out_specs=row,
                          out_shape=jax.ShapeDtypeStruct((s, d), BF16), compiler_params=_params(("parallel",)))(x, gain)


def _prologue(x, gain, weights, side=None):
    s, d = x.shape
    tr = _row_tile(s)
    steps = s // tr
    names = list(weights)
    tiles = []
    for k in names:
        r, _ = weights[k].shape
        tiles.append(next(t for t in range(16, r + 1, 16) if r % t == 0 and r // t <= steps))

    def body(*refs):
        x_ref, g_ref = refs[:2]
        w_refs, h_ref, o_refs = refs[2:2 + len(names)], refs[2 + len(names)], refs[3 + len(names):]
        h_ref[...] = _rms(x_ref[...], g_ref[...]).astype(BF16)
        for w_ref, o_ref in zip(w_refs, o_refs):
            o_ref[...] = w_ref[...].astype(BF16)

    row = pl.BlockSpec((tr, d), lambda i: (i, 0))
    w_specs = [pl.BlockSpec((t, weights[k].shape[1]), lambda i, last=weights[k].shape[0] // t - 1: (jnp.minimum(i, last), 0))
               for k, t in zip(names, tiles)]
    outs, carried = _call(
        body, name="prologue", grid=(steps,), in_specs=[row, pl.BlockSpec((1, d), lambda i: (0, 0))] + w_specs,
        out_specs=[row] + w_specs,
        out_shape=[jax.ShapeDtypeStruct((s, d), BF16)] + [jax.ShapeDtypeStruct(weights[k].shape, BF16) for k in names],
        semantics=("arbitrary",), args=[x, gain] + [weights[k] for k in names], side=side)
    return outs[0], dict(zip(names, outs[1:])), carried


def _norm_mid(x, mo, g_post, g_pre):
    s, d = x.shape
    tr = _row_tile(s)

    def body(x_ref, mo_ref, g2_ref, g3_ref, x2_ref, h2_ref):
        x2 = x_ref[...] + _rms(mo_ref[...], g2_ref[...])
        x2_ref[...] = x2
        h2_ref[...] = _rms(x2, g3_ref[...]).astype(BF16)

    row = pl.BlockSpec((tr, d), lambda i: (i, 0))
    vec = pl.BlockSpec((1, d), lambda i: (0, 0))
    return pl.pallas_call(
        body, name="norm_mid", grid=(s // tr,), in_specs=[row, row, vec, vec], out_specs=[row, row],
        out_shape=[jax.ShapeDtypeStruct((s, d), F32), jax.ShapeDtypeStruct((s, d), BF16)],
        compiler_params=_params(("parallel",)))(x, mo, g_post, g_pre)


def _loss_head(x2, f, g_post, target):
    s, d = x2.shape
    tr = _row_tile(s)

    def body(x2_ref, f_ref, g_ref, t_ref, loss_ref, dout_ref, df_ref, dg_ref):
        @pl.when(pl.program_id(0) == 0)
        def _():
            loss_ref[...] = jnp.zeros_like(loss_ref)
            dg_ref[...] = jnp.zeros_like(dg_ref)

        fv = f_ref[...]
        g = g_ref[...]
        err = x2_ref[...] + _rms(fv, g) - t_ref[...]
        loss_ref[...] += 0.5 * jnp.sum(jnp.mean(err * err, axis=-1, keepdims=True), axis=0, keepdims=True)
        dout = err * (1.0 / d)
        dout_ref[...] = dout
        df, dg = _rms_bwd(fv, g, dout)
        df_ref[...] = df.astype(BF16)
        dg_ref[...] += jnp.sum(dg, axis=0, keepdims=True)

    row = pl.BlockSpec((tr, d), lambda i: (i, 0))
    vec = pl.BlockSpec((1, d), lambda i: (0, 0))
    one = pl.BlockSpec((1, 1), lambda i: (0, 0))
    return pl.pallas_call(
        body, name="loss_head", grid=(s // tr,), in_specs=[row, row, vec, row], out_specs=[one, row, row, vec],
        out_shape=[jax.ShapeDtypeStruct((1, 1), F32), jax.ShapeDtypeStruct((s, d), F32),
                   jax.ShapeDtypeStruct((s, d), BF16), jax.ShapeDtypeStruct((1, d), F32)],
        compiler_params=_params(("arbitrary",)))(x2, f, g_post, target)


def _norm_mid_bwd(x2, mo, g_post, g_pre, dout, dh2):
    s, d = x2.shape
    tr = _row_tile(s)

    def body(x2_ref, mo_ref, g2_ref, g3_ref, dout_ref, dh2_ref, dx2_ref, dmo_ref, dg2_ref, dg3_ref):
        @pl.when(pl.program_id(0) == 0)
        def _():
            dg2_ref[...] = jnp.zeros_like(dg2_ref)
            dg3_ref[...] = jnp.zeros_like(dg3_ref)

        dv, dg3 = _rms_bwd(x2_ref[...], g3_ref[...], dh2_ref[...])
        dx2 = dout_ref[...] + dv
        dx2_ref[...] = dx2
        dmo, dg2 = _rms_bwd(mo_ref[...], g2_ref[...], dx2)
        dmo_ref[...] = dmo.astype(BF16)
        dg2_ref[...] += jnp.sum(dg2, axis=0, keepdims=True)
        dg3_ref[...] += jnp.sum(dg3, axis=0, keepdims=True)

    row = pl.BlockSpec((tr, d), lambda i: (i, 0))
    vec = pl.BlockSpec((1, d), lambda i: (0, 0))
    return pl.pallas_call(
        body, name="norm_mid_bwd", grid=(s // tr,), in_specs=[row, row, vec, vec, row, row],
        out_specs=[row, row, vec, vec],
        out_shape=[jax.ShapeDtypeStruct((s, d), F32), jax.ShapeDtypeStruct((s, d), BF16),
                   jax.ShapeDtypeStruct((1, d), F32), jax.ShapeDtypeStruct((1, d), F32)],
        compiler_params=_params(("arbitrary",)))(x2, mo, g_post, g_pre, dout, dh2)


def _norm_in_bwd(x, gain, dh, dx2):
    s, d = x.shape
    tr = _row_tile(s)

    def body(x_ref, g_ref, dh_ref, dx2_ref, dx_ref, dg_ref):
        @pl.when(pl.program_id(0) == 0)
        def _():
            dg_ref[...] = jnp.zeros_like(dg_ref)

        dv, dg = _rms_bwd(x_ref[...], g_ref[...], dh_ref[...])
        dx_ref[...] = dx2_ref[...] + dv
        dg_ref[...] += jnp.sum(dg, axis=0, keepdims=True)

    row = pl.BlockSpec((tr, d), lambda i: (i, 0))
    vec = pl.BlockSpec((1, d), lambda i: (0, 0))
    return pl.pallas_call(
        body, name="norm_in_bwd", grid=(s // tr,), in_specs=[row, vec, row, row], out_specs=[row, vec],
        out_shape=[jax.ShapeDtypeStruct((s, d), F32), jax.ShapeDtypeStruct((1, d), F32)],
        compiler_params=_params(("arbitrary",)))(x, gain, dh, dx2)


def _swiglu_epilogue(rs, es):
    g, u = rs
    return [g * _sigmoid(g) * u, g, u]


def _swiglu_bwd_epilogue(rs, es):
    d = rs[0]
    g, u = es[0].astype(F32), es[1].astype(F32)
    sg = _sigmoid(g)
    return [d * u * sg * (1.0 + g * (1.0 - sg)), d * g * sg]


def _sum_epilogue(rs, es):
    return [rs[0] + rs[1]]


def _gates_epilogue(rs, es):
    ab, gv, gg = rs
    ga, gs = es
    return [_sigmoid(ga) * ab + _sigmoid(gs) * gv * _sigmoid(gg), ab, gv, gg]


def _gates_bwd_epilogue(rs, es):
    dm = rs[0]
    ga, gs, ab, gv, gg = (e.astype(F32) for e in es)
    sa, ss, sg = _sigmoid(ga), _sigmoid(gs), _sigmoid(gg)
    dsb = dm * ss
    return [dm * ab * sa * (1.0 - sa), dm * gv * sg * ss * (1.0 - ss), dm * sa, dsb * sg, dsb * gv * sg * (1.0 - sg)]


ATTN_ROWS = 2048


def _dilate_qkv(z, g, d):
    s = z.shape[0]
    tm = _pick(s, (2 * ATTN_ROWS, ATTN_ROWS))
    per = tm // d
    nh = HEADS_PER_GROUP

    def body(z_ref, o_ref):
        for r in range(d):
            rows = z_ref[...] if d == 1 else z_ref[pl.ds(r, per, stride=d), :]
            o_ref[0, r] = rows.astype(BF16)

    return pl.pallas_call(
        body, name=f"dilate_qkv_{g}", grid=(s // tm, 3, nh),
        in_specs=[pl.BlockSpec((tm, HEAD_DIM), lambda i, w, h: (i, (3 * w + g) * nh + h))],
        out_specs=pl.BlockSpec((1, d, per, HEAD_DIM), lambda i, w, h: (w, 0, i, h)),
        out_shape=jax.ShapeDtypeStruct((3, d, s // d, GROUP_W), BF16),
        compiler_params=_params(("parallel", "parallel", "parallel")))(z)


def _undilate_dqkv(dqkv, dz, g, d):
    s = dz.shape[0]
    tm = _pick(s, (2 * ATTN_ROWS, ATTN_ROWS))
    per = tm // d
    nh = HEADS_PER_GROUP

    def body(i_ref, dz_ref, o_ref, nat_ref):
        del dz_ref
        if d == 1:
            o_ref[...] = i_ref[0, 0]
        else:
            for r in range(d):
                nat_ref[pl.ds(r, per, stride=d), :] = i_ref[0, r].astype(F32)
            o_ref[...] = nat_ref[...].astype(BF16)

    return pl.pallas_call(
        body, name=f"undilate_dqkv_{g}", grid=(s // tm, 3, nh),
        in_specs=[pl.BlockSpec((1, d, per, HEAD_DIM), lambda i, w, h: (w, 0, i, h)),
                  pl.BlockSpec(memory_space=pl.ANY)],
        out_specs=pl.BlockSpec((tm, HEAD_DIM), lambda i, w, h: (i, (3 * w + g) * nh + h)),
        out_shape=jax.ShapeDtypeStruct(dz.shape, dz.dtype), input_output_aliases={1: 0},
        scratch_shapes=[pltpu.VMEM((tm, HEAD_DIM), F32)],
        compiler_params=_params(("parallel", "parallel", "parallel")))(dqkv, dz)


def _alibi_slope(head):
    return 2.0 ** (-8.0 * (head + 1) / N_ATTN_HEADS)


def _dot_nt(a, b):
    return lax.dot_general(a, b, _DOT_DIMS["nt"], preferred_element_type=F32)


def _dot_tn(a, b):
    return lax.dot_general(a, b, _DOT_DIMS["tn"], preferred_element_type=F32)


def _dot(a, b):
    return jnp.dot(a, b, preferred_element_type=F32)


GROUP_ROWS = HEADS_PER_GROUP * ATTN_BLK


def _band_bias(g, d, pairs):
    qi = jnp.arange(ATTN_BLK)[:, None]
    ki = jnp.arange(ATTN_BLK)[None, :]
    rows = []
    for hh in range(HEADS_PER_GROUP):
        slope_d = _alibi_slope(g * HEADS_PER_GROUP + hh) * d
        tiles = []
        for kind in pairs:
            dist = qi - ki if kind == "cur" else ATTN_BLK + qi - ki
            ok = dist >= 0 if kind == "cur" else dist <= ATTN_BLK
            tiles.append(jnp.where(ok, -slope_d * dist.astype(F32), NEG_BIG))
        rows.append(jnp.concatenate(tiles, axis=1))
    return jnp.concatenate(rows, axis=0).astype(F32)


def _tile_cols(t):
    return slice(t * ATTN_BLK, (t + 1) * ATTN_BLK)


def _attn_fwd(qkv, g, d):
    _, _, L, _ = qkv.shape
    nb = L // ATTN_BLK
    scale = HEAD_DIM ** -0.5

    def body(q_ref, kc_ref, kp_ref, vc_ref, vp_ref, bias_ref, o_ref, lse_ref, s_ref, p_ref):
        n = pl.program_id(1)
        for hh in range(HEADS_PER_GROUP):
            cols, rows = _tile_cols(hh), _tile_cols(hh)
            q = q_ref[0, 0, :, cols]
            s_ref[rows, _tile_cols(0)] = _dot_nt(q, kp_ref[0, 0, :, cols])
            s_ref[rows, _tile_cols(1)] = _dot_nt(q, kc_ref[0, 0, :, cols])
        col = lax.broadcasted_iota(jnp.int32, (GROUP_ROWS, 2 * ATTN_BLK), 1)
        s = s_ref[...] * scale + bias_ref[...]
        s = jnp.where(jnp.logical_and(col < ATTN_BLK, n == 0), NEG_BIG, s)
        m = jnp.max(s, axis=-1, keepdims=True)
        e = jnp.exp(s - m)
        l = jnp.sum(e, axis=-1, keepdims=True)
        p_ref[...] = (e * (1.0 / l)).astype(BF16)
        lse = m + jnp.log(l)
        for hh in range(HEADS_PER_GROUP):
            cols, rows = _tile_cols(hh), _tile_cols(hh)
            o_ref[0, :, cols] = (_dot(p_ref[rows, _tile_cols(0)], vp_ref[0, 0, :, cols])
                                 + _dot(p_ref[rows, _tile_cols(1)], vc_ref[0, 0, :, cols]))
            lse_ref[0, :, cols] = jnp.broadcast_to(lse[rows], (ATTN_BLK, HEAD_DIM))

    def spec(w, shift):
        return pl.BlockSpec((1, 1, ATTN_BLK, GROUP_W), lambda r, n: (w, r, jnp.maximum(n + shift, 0), 0))

    out = pl.BlockSpec((1, ATTN_BLK, GROUP_W), lambda r, n: (r, n, 0))
    bias = _band_bias(g, d, ("prev", "cur"))
    return pl.pallas_call(
        body, name=f"attn_fwd_{g}", grid=(d, nb),
        in_specs=[spec(0, 0), spec(1, 0), spec(1, -1), spec(2, 0), spec(2, -1),
                  pl.BlockSpec(bias.shape, lambda r, n: (0, 0))],
        out_specs=[out, out], out_shape=[jax.ShapeDtypeStruct((d, L, GROUP_W), F32)] * 2,
        scratch_shapes=[pltpu.VMEM((GROUP_ROWS, 2 * ATTN_BLK), F32), pltpu.VMEM((GROUP_ROWS, 2 * ATTN_BLK), BF16)],
        compiler_params=_params(("parallel", "parallel")))(qkv, qkv, qkv, qkv, qkv, bias)


def _attn_bwd(qkv, do, lse, cc, g, d, side=None):
    _, _, L, _ = qkv.shape
    nb = L // ATTN_BLK
    scale = HEAD_DIM ** -0.5
    a_, b_, c_ = _tile_cols(0), _tile_cols(1), _tile_cols(2)

    def body(q0_ref, q1_ref, k0_ref, kp_ref, v0_ref, vp_ref, do0_ref, do1_ref, l0_ref, l1_ref, c0_ref, c1_ref,
             bias_ref, o_ref, s_ref, dp_ref, l_ref, c_ref, p_ref, ds_ref):
        n = pl.program_id(1)
        for hh in range(HEADS_PER_GROUP):
            cols, rows = _tile_cols(hh), _tile_cols(hh)
            q0, q1 = q0_ref[0, 0, :, cols], q1_ref[0, 0, :, cols]
            k0, kp = k0_ref[0, 0, :, cols], kp_ref[0, 0, :, cols]
            v0, vp = v0_ref[0, 0, :, cols], vp_ref[0, 0, :, cols]
            do0, do1 = do0_ref[0, :, cols], do1_ref[0, :, cols]
            s_ref[rows, a_], s_ref[rows, b_], s_ref[rows, c_] = _dot_nt(q0, k0), _dot_nt(q0, kp), _dot_nt(q1, k0)
            dp_ref[rows, a_], dp_ref[rows, b_], dp_ref[rows, c_] = _dot_nt(do0, v0), _dot_nt(do0, vp), _dot_nt(do1, v0)
            l_ref[rows, a_], l_ref[rows, b_], l_ref[rows, c_] = l0_ref[0, :, cols], l0_ref[0, :, cols], l1_ref[0, :, cols]
            c_ref[rows, a_], c_ref[rows, b_], c_ref[rows, c_] = c0_ref[0, :, cols], c0_ref[0, :, cols], c1_ref[0, :, cols]
        col = lax.broadcasted_iota(jnp.int32, (GROUP_ROWS, 3 * ATTN_BLK), 1)
        tile = col // ATTN_BLK
        gone = jnp.logical_or(jnp.logical_and(tile == 1, n == 0), jnp.logical_and(tile == 2, n == nb - 1))
        s = jnp.where(gone, NEG_BIG, s_ref[...] * scale + bias_ref[...])
        p = jnp.exp(s - l_ref[...])
        p_ref[...] = p.astype(BF16)
        ds_ref[...] = (p * (dp_ref[...] + c_ref[...])).astype(BF16)
        for hh in range(HEADS_PER_GROUP):
            cols, rows = _tile_cols(hh), _tile_cols(hh)
            q0, q1 = q0_ref[0, 0, :, cols], q1_ref[0, 0, :, cols]
            k0, kp = k0_ref[0, 0, :, cols], kp_ref[0, 0, :, cols]
            do0, do1 = do0_ref[0, :, cols], do1_ref[0, :, cols]
            o_ref[0, 0, :, cols] = ((_dot(ds_ref[rows, a_], k0) + _dot(ds_ref[rows, b_], kp)) * scale).astype(BF16)
            o_ref[1, 0, :, cols] = ((_dot_tn(ds_ref[rows, a_], q0) + _dot_tn(ds_ref[rows, c_], q1)) * scale).astype(BF16)
            o_ref[2, 0, :, cols] = (_dot_tn(p_ref[rows, a_], do0) + _dot_tn(p_ref[rows, c_], do1)).astype(BF16)

    def spec(w, shift):
        return pl.BlockSpec((1, 1, ATTN_BLK, GROUP_W), lambda r, n: (w, r, jnp.clip(n + shift, 0, nb - 1), 0))

    def spec3(shift):
        return pl.BlockSpec((1, ATTN_BLK, GROUP_W), lambda r, n: (r, jnp.clip(n + shift, 0, nb - 1), 0))

    bias = _band_bias(g, d, ("cur", "prev", "prev"))
    wide = (GROUP_ROWS, 3 * ATTN_BLK)
    outs, carried = _call(
        body, name=f"attn_bwd_{g}", grid=(d, nb),
        in_specs=[spec(0, 0), spec(0, 1), spec(1, 0), spec(1, -1), spec(2, 0), spec(2, -1),
                  spec3(0), spec3(1), spec3(0), spec3(1), spec3(0), spec3(1), pl.BlockSpec(wide, lambda r, n: (0, 0))],
        out_specs=[pl.BlockSpec((3, 1, ATTN_BLK, GROUP_W), lambda r, n: (0, r, n, 0))],
        out_shape=[jax.ShapeDtypeStruct((3, d, L, GROUP_W), BF16)],
        scratch_shapes=[pltpu.VMEM(wide, F32)] * 4 + [pltpu.VMEM(wide, BF16)] * 2, semantics=("parallel", "parallel"),
        args=[qkv, qkv, qkv, qkv, qkv, qkv, do, do, lse, lse, cc, cc, bias], side=side)
    return outs[0] if side is None else (outs[0], carried)


def _load_natural(refs, nat_refs):
    for g, d in enumerate(ATTN_DILATIONS):
        if d == 1:
            nat_refs[g][...] = refs[g][0]
        else:
            per = ATTN_ROWS // d
            for r in range(d):
                nat_refs[g][pl.ds(r, per, stride=d), :] = refs[g][r]


def _mix_weights(lse_nat):
    l0, l1, l2 = lse_nat[0][...], lse_nat[1][...], lse_nat[2][...]
    m = jnp.maximum(jnp.maximum(l0, l1), l2)
    e0, e1, e2 = jnp.exp(l0 - m), jnp.exp(l1 - m), jnp.exp(l2 - m)
    inv = 1.0 / (e0 + e1 + e2)
    return e0 * inv, e1 * inv, e2 * inv


def _dilated_specs(s):
    return [pl.BlockSpec((d, ATTN_ROWS // d, HEAD_DIM), lambda i, h: (0, i, h)) for d in ATTN_DILATIONS]


NATURAL_SCRATCH = [pltpu.VMEM((ATTN_ROWS, HEAD_DIM), F32)] * (2 * len(ATTN_DILATIONS))


def _attn_merge(outs, lses):
    s = outs[0].shape[0] * outs[0].shape[1]

    def body(o0, o1, o2, l0, l1, l2, a_ref, *nat):
        onat, lnat = nat[:3], nat[3:]
        _load_natural((o0, o1, o2), onat)
        _load_natural((l0, l1, l2), lnat)
        w0, w1, w2 = _mix_weights(lnat)
        a_ref[...] = (w0 * onat[0][...] + w1 * onat[1][...] + w2 * onat[2][...]).astype(BF16)

    return pl.pallas_call(
        body, name="attn_merge", grid=(s // ATTN_ROWS, HEADS_PER_GROUP), in_specs=_dilated_specs(s) * 2,
        out_specs=pl.BlockSpec((ATTN_ROWS, HEAD_DIM), lambda i, h: (i, h)),
        out_shape=jax.ShapeDtypeStruct((s, GROUP_W), BF16), scratch_shapes=NATURAL_SCRATCH,
        compiler_params=_params(("parallel", "parallel")))(*outs, *lses)


def _attn_merge_bwd(outs, lses, dattn):
    s = dattn.shape[0]

    def body(o0, o1, o2, l0, l1, l2, da_ref, do0, do1, do2, c0, c1, c2, *nat):
        onat, lnat = nat[:3], nat[3:]
        _load_natural((o0, o1, o2), onat)
        _load_natural((l0, l1, l2), lnat)
        ws = _mix_weights(lnat)
        da = da_ref[...]
        attn = ws[0] * onat[0][...] + ws[1] * onat[1][...] + ws[2] * onat[2][...]
        tot = jnp.broadcast_to(jnp.sum(da * attn, axis=-1, keepdims=True), (ATTN_ROWS, HEAD_DIM))
        for g, (d, do_ref, c_ref) in enumerate(zip(ATTN_DILATIONS, (do0, do1, do2), (c0, c1, c2))):
            if d == 1:
                do_ref[0] = (ws[g] * da).astype(BF16)
                c_ref[0] = -ws[g] * tot
            else:
                onat[g][...] = ws[g] * da
                lnat[g][...] = -ws[g] * tot
                per = ATTN_ROWS // d
                for r in range(d):
                    do_ref[r] = onat[g][pl.ds(r, per, stride=d), :].astype(BF16)
                    c_ref[r] = lnat[g][pl.ds(r, per, stride=d), :]

    dil = _dilated_specs(s)
    shapes = [jax.ShapeDtypeStruct(o.shape, BF16) for o in outs] + [jax.ShapeDtypeStruct(o.shape, F32) for o in outs]
    return pl.pallas_call(
        body, name="attn_merge_bwd", grid=(s // ATTN_ROWS, HEADS_PER_GROUP),
        in_specs=dil * 2 + [pl.BlockSpec((ATTN_ROWS, HEAD_DIM), lambda i, h: (i, h))], out_specs=dil * 2,
        out_shape=shapes, scratch_shapes=NATURAL_SCRATCH,
        compiler_params=_params(("parallel", "parallel")))(*outs, *lses, dattn)


def _ssm_prepare(a_re, a_im, log_dt, b_re, b_im, c_re, c_im):
    n_g = a_re.shape[0]
    nj = n_g * SSM_GROUP // SSM_TILE_CH
    gpt = SSM_TILE_CH // SSM_GROUP
    dt = jnp.exp(log_dt)[:, None]
    mag = jnp.exp(a_re * dt)
    lr, li = mag * jnp.cos(a_im * dt), mag * jnp.sin(a_im * dt)
    den = a_re * a_re + a_im * a_im
    cr = ((lr - 1.0) * a_re + li * a_im) / den
    ci = (li * a_re - (lr - 1.0) * a_im) / den
    bb_re = cr[..., None] * b_re - ci[..., None] * b_im
    bb_im = cr[..., None] * b_im + ci[..., None] * b_re
    eye = jnp.eye(gpt, dtype=F32)

    def b_tiles(t):
        t = t.transpose(0, 2, 1).reshape(nj, gpt, SSM_GROUP, SSM_STATE)
        return jnp.einsum("jgcp,gh->jgchp", t, eye).reshape(nj, SSM_TILE_CH, SSM_TILE_ST)

    def c_tiles(t):
        t = t.reshape(nj, gpt, SSM_GROUP, SSM_STATE)
        return jnp.einsum("jgcp,gh->jhpgc", t, eye).reshape(nj, SSM_TILE_ST, SSM_TILE_CH)

    lam = jnp.stack([lr.reshape(-1), li.reshape(-1)])
    bmat = jnp.concatenate([b_tiles(bb_re), b_tiles(bb_im)], axis=2)
    cmat = jnp.concatenate([c_tiles(c_re), -c_tiles(c_im)], axis=1)
    return lam, bmat, cmat


SSM_SEGMENTS = 8


def _to_segment_order(nat, perm_ref):
    per = nat.shape[0] // SSM_SEGMENTS
    for i in range(SSM_SEGMENTS):
        perm_ref[pl.ds(i, per, stride=SSM_SEGMENTS), :] = nat[i * per:(i + 1) * per, :]
    return perm_ref[...]


def _to_time_order(val, perm_ref, store):
    per = val.shape[0] // SSM_SEGMENTS
    perm_ref[...] = val
    for i in range(SSM_SEGMENTS):
        store(i, perm_ref[pl.ds(i, per, stride=SSM_SEGMENTS), :])


def _fill_powers(lam_ref, w_ref, nj, tau_n):
    for j in range(nj):
        st = slice(j * SSM_TILE_ST, (j + 1) * SSM_TILE_ST)
        lr = jnp.broadcast_to(lam_ref[0:1, st], (SSM_SEGMENTS, SSM_TILE_ST))
        li = jnp.broadcast_to(lam_ref[1:2, st], (SSM_SEGMENTS, SSM_TILE_ST))
        wr, wi = lr, li
        for tau in range(tau_n):
            rows = slice(tau * SSM_SEGMENTS, (tau + 1) * SSM_SEGMENTS)
            w_ref[j, rows, :SSM_TILE_ST] = wr
            w_ref[j, rows, SSM_TILE_ST:] = wi
            wr, wi = wr * lr - wi * li, wr * li + wi * lr


def _segment_scan(src, xs_ref, w_tile, lr, li, cr, ci, conj, reverse):
    seg, half = SSM_SEGMENTS, SSM_TILE_ST
    tau_n = src.shape[0] // seg
    sgn = -1.0 if conj else 1.0
    lr8 = jnp.broadcast_to(lr, (seg, half))
    li8 = jnp.broadcast_to(li, (seg, half)) * sgn
    xr = jnp.zeros((seg, half), F32)
    xi = jnp.zeros((seg, half), F32)
    order = range(tau_n - 1, -1, -1) if reverse else range(tau_n)
    for tau in order:
        rows = slice(tau * seg, (tau + 1) * seg)
        xr, xi = lr8 * xr - li8 * xi + src[rows, :half], lr8 * xi + li8 * xr + src[rows, half:]
        xs_ref[rows, :half] = xr
        xs_ref[rows, half:] = xi
    pr = w_tile[(tau_n - 1) * seg:(tau_n - 1) * seg + 1, :half]
    pi = w_tile[(tau_n - 1) * seg:(tau_n - 1) * seg + 1, half:] * sgn
    fr, fi = cr, ci
    ins_r, ins_i = [None] * seg, [None] * seg
    runs = range(seg - 1, -1, -1) if reverse else range(seg)
    for i in runs:
        ins_r[i], ins_i[i] = fr, fi
        fr, fi = xr[i:i + 1, :] + pr * fr - pi * fi, xi[i:i + 1, :] + pr * fi + pi * fr
    in_r = jnp.concatenate(ins_r, axis=0)
    in_i = jnp.concatenate(ins_i, axis=0)
    for tau in range(tau_n):
        rows = slice(tau * seg, (tau + 1) * seg)
        wrow = (tau_n - 1 - tau) if reverse else tau
        wr = w_tile[wrow * seg:(wrow + 1) * seg, :half]
        wi = w_tile[wrow * seg:(wrow + 1) * seg, half:] * sgn
        xs_ref[rows, :half] += wr * in_r - wi * in_i
        xs_ref[rows, half:] += wr * in_i + wi * in_r
    return (fr, fi), (in_r, in_i)


def _ssm_dims(z, bmat, u_off):
    s = z.shape[0]
    nj = bmat.shape[0]
    t_rows = _pick(s, (256, 128))
    return s, nj, nj * SSM_TILE_CH, nj * SSM_TILE_ST, t_rows


def _ssm_fwd(z, bmat, cmat, lam, dskip, u_off, side=None):
    s, nj, w, ns, t_rows = _ssm_dims(z, bmat, u_off)
    per = t_rows // SSM_SEGMENTS

    def body(*refs):
        u_refs = refs[:nj]
        b_ref, c_ref, lam_ref, d_ref, y_ref, yg_ref, xin_ref, xall_ref, carry_ref, w_ref, xs_ref, perm_ref = refs[nj:]

        @pl.when(pl.program_id(0) == 0)
        def _():
            carry_ref[...] = jnp.zeros_like(carry_ref)
            _fill_powers(lam_ref, w_ref, nj, per)

        xin_ref[0] = carry_ref[...]
        for j in range(nj):
            st = slice(j * SSM_TILE_ST, (j + 1) * SSM_TILE_ST)
            ch = slice(j * SSM_TILE_CH, (j + 1) * SSM_TILE_CH)
            up = _to_segment_order(u_refs[j], perm_ref)
            bu = _dot(up.astype(BF16), b_ref[j])
            (fr, fi), _ = _segment_scan(bu, xs_ref, w_ref.at[j], lam_ref[0:1, st], lam_ref[1:2, st],
                                        carry_ref[0:1, st], carry_ref[1:2, st], conj=False, reverse=False)
            carry_ref[0:1, st] = fr
            carry_ref[1:2, st] = fi
            xs = xs_ref[...].astype(BF16)
            xall_ref[:, j * 2 * SSM_TILE_ST:(j + 1) * 2 * SSM_TILE_ST] = xs
            yp = _dot(xs, c_ref[j]) + d_ref[:, ch] * up

            def store(i, rows, ch=ch):
                y_ref[i * per:(i + 1) * per, ch] = rows
                yg_ref[i * per:(i + 1) * per, ch] = _gelu(rows).astype(BF16)

            _to_time_order(yp, perm_ref, store)

    u_specs = [pl.BlockSpec((t_rows, SSM_TILE_CH), lambda c, k=k: (c, u_off // SSM_TILE_CH + k)) for k in range(nj)]
    full3 = lambda shape: pl.BlockSpec(shape, lambda c: (0, 0, 0))
    full2 = lambda shape: pl.BlockSpec(shape, lambda c: (0, 0))
    rows = pl.BlockSpec((t_rows, w), lambda c: (c, 0))
    outs, carried = _call(
        body, name="ssm_fwd", grid=(s // t_rows,),
        in_specs=u_specs + [full3(bmat.shape), full3(cmat.shape), full2(lam.shape), full2(dskip.shape)],
        out_specs=[rows, rows, pl.BlockSpec((1, 2, ns), lambda c: (c, 0, 0)), pl.BlockSpec((t_rows, 2 * ns), lambda c: (c, 0))],
        out_shape=[jax.ShapeDtypeStruct((s, w), F32), jax.ShapeDtypeStruct((s, w), BF16),
                   jax.ShapeDtypeStruct((s // t_rows, 2, ns), F32), jax.ShapeDtypeStruct((s, 2 * ns), BF16)],
        scratch_shapes=[pltpu.VMEM((2, ns), F32), pltpu.VMEM((nj, t_rows, 2 * SSM_TILE_ST), F32),
                        pltpu.VMEM((t_rows, 2 * SSM_TILE_ST), F32), pltpu.VMEM((t_rows, SSM_TILE_CH), F32)],
        semantics=("arbitrary",), args=[*([z] * nj), bmat, cmat, lam, dskip], side=side)
    return outs if side is None else (outs, carried)


def _ssm_bwd(z, y, dyg, xin, xall, bmat, cmat, lam, dskip, u_off, side=None):
    s, nj, w, ns, t_rows = _ssm_dims(z, bmat, u_off)
    nc = s // t_rows
    per = t_rows // SSM_SEGMENTS
    seg, half = SSM_SEGMENTS, SSM_TILE_ST

    def body(*refs):
        u_refs = refs[:nj]
        (y_ref, dyg_ref, xin_ref, xall_ref, b_ref, c_ref, lam_ref, d_ref, du_ref, db_ref, dc_ref, dlam_ref, dd_ref,
         carry_ref, w_ref, gs_ref, perm_ref, acc_ref) = refs[nj:]

        @pl.when(pl.program_id(0) == 0)
        def _():
            carry_ref[...] = jnp.zeros_like(carry_ref)
            db_ref[...] = jnp.zeros_like(db_ref)
            dc_ref[...] = jnp.zeros_like(dc_ref)
            dd_ref[...] = jnp.zeros_like(dd_ref)
            acc_ref[...] = jnp.zeros_like(acc_ref)
            _fill_powers(lam_ref, w_ref, nj, per)

        for j in range(nj):
            st = slice(j * SSM_TILE_ST, (j + 1) * SSM_TILE_ST)
            ch = slice(j * SSM_TILE_CH, (j + 1) * SSM_TILE_CH)
            lr, li = lam_ref[0:1, st], lam_ref[1:2, st]
            up = _to_segment_order(u_refs[j], perm_ref)
            upb = up.astype(BF16)
            dyp = _to_segment_order(dyg_ref[:, ch] * _gelu_grad(y_ref[:, ch]), perm_ref)
            dyb = dyp.astype(BF16)
            xs = xall_ref[:, j * 2 * half:(j + 1) * 2 * half]
            xf = xs.astype(F32)
            ends = xf[t_rows - seg:t_rows - 1, :]
            in_r = jnp.concatenate([xin_ref[0, 0:1, st], ends[:, :half]], axis=0)
            in_i = jnp.concatenate([xin_ref[0, 1:2, st], ends[:, half:]], axis=0)
            (gr, gi), _ = _segment_scan(_dot_nt(dyb, c_ref[j]), gs_ref, w_ref.at[j], lr, li,
                                        carry_ref[0:1, st], carry_ref[1:2, st], conj=True, reverse=True)
            carry_ref[0:1, st] = gr
            carry_ref[1:2, st] = gi
            gs = gs_ref[...]
            xsr, xsi, gsr, gsi = xf[:, :half], xf[:, half:], gs[:, :half], gs[:, half:]
            pxr = jnp.concatenate([in_r, xsr[:t_rows - seg]], axis=0)
            pxi = jnp.concatenate([in_i, xsi[:t_rows - seg]], axis=0)
            dl_r = gsr * pxr + gsi * pxi
            dl_i = gsi * pxr - gsr * pxi
            acc_ref[0, :, st] += jnp.sum(dl_r.reshape(per, seg, half), axis=0)
            acc_ref[1, :, st] += jnp.sum(dl_i.reshape(per, seg, half), axis=0)
            gx = gs.astype(BF16)
            dup = _dot_nt(gx, b_ref[j]) + d_ref[:, ch] * dyp

            def store(i, rows, ch=ch):
                du_ref[i * per:(i + 1) * per, ch] = rows.astype(BF16)

            _to_time_order(dup, perm_ref, store)
            db_ref[j] += _dot_tn(upb, gx)
            dc_ref[j] += _dot_tn(xs, dyb)
            dd_ref[:, ch] += jnp.sum(dyp * up, axis=0, keepdims=True)

        @pl.when(pl.program_id(0) == nc - 1)
        def _():
            dlam_ref[...] = jnp.sum(acc_ref[...], axis=1)

    rev = lambda c: nc - 1 - c
    u_specs = [pl.BlockSpec((t_rows, SSM_TILE_CH), lambda c, k=k: (rev(c), u_off // SSM_TILE_CH + k))
               for k in range(nj)]
    full3 = lambda shape: pl.BlockSpec(shape, lambda c: (0, 0, 0))
    full2 = lambda shape: pl.BlockSpec(shape, lambda c: (0, 0))
    rows = pl.BlockSpec((t_rows, w), lambda c: (rev(c), 0))
    outs, carried = _call(
        body, name="ssm_bwd", grid=(nc,),
        in_specs=u_specs + [rows, rows, pl.BlockSpec((1, 2, ns), lambda c: (rev(c), 0, 0)),
                            pl.BlockSpec((t_rows, 2 * ns), lambda c: (rev(c), 0)),
                            full3(bmat.shape), full3(cmat.shape), full2(lam.shape), full2(dskip.shape)],
        out_specs=[rows, full3(bmat.shape), full3(cmat.shape), full2(lam.shape), full2(dskip.shape)],
        out_shape=[jax.ShapeDtypeStruct((s, w), BF16), jax.ShapeDtypeStruct(bmat.shape, F32),
                   jax.ShapeDtypeStruct(cmat.shape, F32), jax.ShapeDtypeStruct(lam.shape, F32),
                   jax.ShapeDtypeStruct(dskip.shape, F32)],
        scratch_shapes=[pltpu.VMEM((2, ns), F32), pltpu.VMEM((nj, t_rows, 2 * SSM_TILE_ST), F32),
                        pltpu.VMEM((t_rows, 2 * SSM_TILE_ST), F32),
                        pltpu.VMEM((t_rows, SSM_TILE_CH), F32), pltpu.VMEM((2, SSM_SEGMENTS, ns), F32)],
        semantics=("arbitrary",), args=[*([z] * nj), y, dyg, xin, xall, bmat, cmat, lam, dskip], side=side)
    return outs if side is None else (outs, carried)


def _adam_math(w, g, m, v):
    m = ADAM_B1 * m + (1.0 - ADAM_B1) * g
    v = ADAM_B2 * v + (1.0 - ADAM_B2) * (g * g)
    m_hat = m / (1.0 - ADAM_B1 ** ADAM_STEP)
    v_hat = v / (1.0 - ADAM_B2 ** ADAM_STEP)
    delta = -ADAM_LR * (m_hat / (jnp.sqrt(v_hat) + ADAM_EPS) + ADAM_WD * w)
    return delta, m, v


def _adam_rows(r, c):
    for tr in (512, 256, 128, 64, 32, 16, 8):
        if r % tr == 0 and tr * c * 4 <= (1 << 20):
            return tr
    return r


def _adamw_big(w, p_mine, p_sib, m, v, name, side=None):
    r, c = w.shape
    tr = _adam_rows(r, c)

    def body(w_ref, a_ref, b_ref, m_ref, v_ref, g_ref, d_ref, nm_ref, nv_ref):
        g = a_ref[...] + b_ref[...]
        g_ref[...] = g
        d_ref[...], nm_ref[...], nv_ref[...] = _adam_math(w_ref[...], g, m_ref[...], v_ref[...])

    blk = pl.BlockSpec((tr, c), lambda i: (i, 0))
    outs, carried = _call(body, name=f"adamw_{name}", grid=(r // tr,), in_specs=[blk] * 5, out_specs=[blk] * 4,
                          out_shape=[jax.ShapeDtypeStruct((r, c), F32)] * 4, semantics=("parallel",),
                          args=[w, p_mine, p_sib, m, v], side=side)
    return outs if side is None else (outs, carried)


def _adamw_small(w, parts, m, v):
    r, c = w.shape
    n_dev = parts.shape[0]

    def body(w_ref, p_ref, m_ref, v_ref, g_ref, d_ref, nm_ref, nv_ref):
        g = p_ref[0]
        for k in range(1, n_dev):
            g = g + p_ref[k]
        g_ref[...] = g
        d_ref[...], nm_ref[...], nv_ref[...] = _adam_math(w_ref[...], g, m_ref[...], v_ref[...])

    blk = pl.BlockSpec((r, c), lambda i: (0, 0))
    return pl.pallas_call(body, name="adamw_small", grid=(1,),
                          in_specs=[blk, pl.BlockSpec((n_dev, r, c), lambda i: (0, 0, 0)), blk, blk],
                          out_specs=[blk] * 4, out_shape=[jax.ShapeDtypeStruct((r, c), F32)] * 4,
                          compiler_params=_params(("arbitrary",)))(w, parts, m, v)


def _cast_bf16(w, name):
    r, c = w.shape
    tr = _adam_rows(r, c)

    def body(w_ref, o_ref):
        o_ref[...] = w_ref[...].astype(BF16)

    blk = pl.BlockSpec((tr, c), lambda i: (i, 0))
    return pl.pallas_call(body, name=f"cast_{name}", grid=(r // tr,), in_specs=[blk], out_specs=blk,
                          out_shape=jax.ShapeDtypeStruct((r, c), BF16), compiler_params=_params(("parallel",)))(w)


def _sum_slots(recv, name):
    _, r, c = recv.shape
    tr = _adam_rows(r, c)

    def body(p_ref, o_ref):
        acc = p_ref[0].astype(F32)
        for k in range(1, N_CHIPS):
            acc = acc + p_ref[k].astype(F32)
        o_ref[...] = acc

    return pl.pallas_call(body, name=f"sum_{name}", grid=(r // tr,),
                          in_specs=[pl.BlockSpec((N_CHIPS, tr, c), lambda i: (0, i, 0))],
                          out_specs=pl.BlockSpec((tr, c), lambda i: (i, 0)),
                          out_shape=jax.ShapeDtypeStruct((r, c), F32), compiler_params=_params(("parallel",)))(recv)


BIG_WEIGHTS = ("w_in", "w_attn_up", "w_glu_v", "w_glu_g", "w_out", "w_ffn_gate", "w_ffn_up", "w_ffn_down")
COL_SHARDED = ("w_in", "w_attn_up", "w_glu_v", "w_glu_g", "w_ffn_gate", "w_ffn_up")


def _aligned(v, m):
    return v if isinstance(v, int) else pl.multiple_of(v, m)


def _shard_of(ref, name, j, shard_shape, half=None):
    r, c = shard_shape
    rows = r if half is None else r // 2
    row0 = 0 if half is None else half * rows
    if name in COL_SHARDED:
        return ref.at[pl.ds(_aligned(row0, 16), rows), pl.ds(_aligned(j * c, 128), c)]
    return ref.at[pl.ds(_aligned(j * r + row0, 16), rows), :]


def _other_chips():
    x, y = lax.axis_index("x"), lax.axis_index("y")
    return [(1 - x, y), (x, 1 - y), (1 - x, 1 - y)]


def _dma_sems(n, arrays):
    return [pltpu.SemaphoreType.DMA((n, 3))] * arrays + [pltpu.SemaphoreType.DMA((n,))]


def _gather_side(shards):
    names = list(shards)
    n = len(names)
    full_shapes = []
    for k in names:
        r, c = shards[k].shape
        full_shapes.append((r, c * N_CHIPS) if k in COL_SHARDED else (r * N_CHIPS, c))

    def build(src, dst, sems):
        send_sems, recv_sems, pass_send_sems, pass_recv_sems, local_sems = sems
        x, y, c = lax.axis_index("x"), lax.axis_index("y"), lax.axis_index("c")
        me = 2 * x + y
        locals_, sends, arrivals, forwards, passed_on = [], [], [], [], []
        for i, k in enumerate(names):
            shape = shards[k].shape
            half_rows = shape[0] // 2
            locals_.append(pltpu.make_async_copy(src[i], _shard_of(dst[i], k, me, shape), local_sems.at[i]))
            my_half = src[i].at[pl.ds(_aligned(c * half_rows, 16), half_rows), :]
            for p, (px, py) in enumerate(_other_chips()):
                peer = 2 * px + py
                landed = _shard_of(dst[i], k, peer, shape, half=c)
                sends.append(pltpu.make_async_remote_copy(
                    src_ref=my_half, dst_ref=_shard_of(dst[i], k, me, shape, half=c), send_sem=send_sems.at[i, p],
                    recv_sem=recv_sems.at[i, p], device_id=(px, py, c), device_id_type=MESH))
                arrivals.append(pltpu.make_async_remote_copy(
                    src_ref=my_half, dst_ref=landed, send_sem=send_sems.at[i, p],
                    recv_sem=recv_sems.at[i, p], device_id=(px, py, c), device_id_type=MESH))
                forwards.append(pltpu.make_async_remote_copy(
                    src_ref=landed, dst_ref=landed, send_sem=pass_send_sems.at[i, p],
                    recv_sem=pass_recv_sems.at[i, p], device_id=(x, y, 1 - c), device_id_type=MESH))
                passed_on.append(pltpu.make_async_remote_copy(
                    src_ref=landed, dst_ref=_shard_of(dst[i], k, peer, shape, half=1 - c),
                    send_sem=pass_send_sems.at[i, p], recv_sem=pass_recv_sems.at[i, p],
                    device_id=(x, y, 1 - c), device_id_type=MESH))
        return locals_, sends, arrivals, forwards, passed_on

    return _Side([shards[k] for k in names], [jax.ShapeDtypeStruct(s, BF16) for s in full_shapes], _dma_sems(n, 4), build,
                 relays=True)


def _scatter_side(grads, shard_shapes):
    names = list(grads)
    n = len(names)

    def build(src, dst, sems):
        send_sems, recv_sems, local_sems = sems
        x, y, c = lax.axis_index("x"), lax.axis_index("y"), lax.axis_index("c")
        me = 2 * x + y
        locals_, sends, arrivals = [], [], []
        for i, k in enumerate(names):
            shape = shard_shapes[k]
            locals_.append(pltpu.make_async_copy(_shard_of(src[i], k, me, shape), dst[i].at[me], local_sems.at[i]))
            for p, (px, py) in enumerate(_other_chips()):
                peer = 2 * px + py
                sends.append(pltpu.make_async_remote_copy(
                    src_ref=_shard_of(src[i], k, peer, shape), dst_ref=dst[i].at[me], send_sem=send_sems.at[i, p],
                    recv_sem=recv_sems.at[i, p], device_id=(px, py, c), device_id_type=MESH))
                arrivals.append(pltpu.make_async_remote_copy(
                    src_ref=_shard_of(src[i], k, peer, shape), dst_ref=dst[i].at[peer], send_sem=send_sems.at[i, p],
                    recv_sem=recv_sems.at[i, p], device_id=(px, py, c), device_id_type=MESH))
        return locals_, sends, arrivals, [None] * len(arrivals), []

    return _Side([grads[k] for k in names],
                 [jax.ShapeDtypeStruct((N_CHIPS,) + tuple(shard_shapes[k]), BF16) for k in names], _dma_sems(n, 2), build)


def _put_cols(dz, src, col_off):
    s, w = src.shape
    tr = _pick(s, (2048, 1024, 512, 256, 128, 64, 8))
    tc = _pick(math.gcd(w, col_off), (1024, 512, 256, 128))
    off = col_off // tc

    def body(src_ref, dz_ref, o_ref):
        del dz_ref
        o_ref[...] = src_ref[...].astype(o_ref.dtype)

    return pl.pallas_call(
        body, name="put_cols", grid=(s // tr, w // tc),
        in_specs=[pl.BlockSpec((tr, tc), lambda i, j: (i, j)), pl.BlockSpec(memory_space=pl.ANY)],
        out_specs=pl.BlockSpec((tr, tc), lambda i, j: (i, off + j)),
        out_shape=jax.ShapeDtypeStruct(dz.shape, dz.dtype), input_output_aliases={1: 0},
        compiler_params=_params(("parallel", "parallel")))(src, dz)


def _swap_side(parts):
    n = len(parts)

    def build(src, dst, sems):
        send_sems, recv_sems = sems
        sibling = (lax.axis_index("x"), lax.axis_index("y"), 1 - lax.axis_index("c"))
        copies = [pltpu.make_async_remote_copy(src_ref=src[i], dst_ref=dst[i], send_sem=send_sems.at[i],
                                               recv_sem=recv_sems.at[i], device_id=sibling, device_id_type=MESH)
                  for i in range(n)]
        return [], copies, copies, [None] * n, []

    return _Side(parts, [jax.ShapeDtypeStruct(p.shape, F32) for p in parts],
                 [pltpu.SemaphoreType.DMA((n,)), pltpu.SemaphoreType.DMA((n,))], build)


def _share_side(packed):
    r, c = packed.shape

    def build(src, dst, sems):
        send_sems, recv_sems, local_sem = sems
        x, y, cc = lax.axis_index("x"), lax.axis_index("y"), lax.axis_index("c")
        me = 4 * x + 2 * y + cc
        own = pltpu.make_async_copy(src[0], dst[0].at[me], local_sem)
        sends, arrivals = [], []
        flips = [(fx, fy, fc) for fx in range(2) for fy in range(2) for fc in range(2) if fx or fy or fc]
        for p, (fx, fy, fc) in enumerate(flips):
            px, py, pc = x ^ fx, y ^ fy, cc ^ fc
            sends.append(pltpu.make_async_remote_copy(
                src_ref=src[0], dst_ref=dst[0].at[me], send_sem=send_sems.at[p], recv_sem=recv_sems.at[p],
                device_id=(px, py, pc), device_id_type=MESH))
            arrivals.append(pltpu.make_async_remote_copy(
                src_ref=src[0], dst_ref=dst[0].at[4 * px + 2 * py + pc], send_sem=send_sems.at[p],
                recv_sem=recv_sems.at[p], device_id=(px, py, pc), device_id_type=MESH))
        return [own], sends, arrivals, [None] * len(arrivals), []

    return _Side([packed], [jax.ShapeDtypeStruct((8, r, c), F32)],
                 [pltpu.SemaphoreType.DMA((7,)), pltpu.SemaphoreType.DMA((7,)), pltpu.SemaphoreType.DMA], build)


SMALL_WEIGHTS = ("norm_mix_pre", "ssm_a_re", "ssm_a_im", "ssm_log_dt", "ssm_b_re", "ssm_b_im", "ssm_c_re", "ssm_c_im",
                 "ssm_d", "norm_mix_post", "norm_ffn_pre", "norm_ffn_post")
WEIGHT_ORDER = ("norm_mix_pre", "w_in", "w_attn_up", "ssm_a_re", "ssm_a_im", "ssm_log_dt", "ssm_b_re", "ssm_b_im",
                "ssm_c_re", "ssm_c_im", "ssm_d", "w_glu_v", "w_glu_g", "w_out", "norm_mix_post", "norm_ffn_pre",
                "w_ffn_gate", "w_ffn_up", "w_ffn_down", "norm_ffn_post")
PACK_LANES = 128
PACK_ROWS = 8
PACK_GROUPS = (SMALL_WEIGHTS[:1], SMALL_WEIGHTS[1:])


def _pack_group(arrs, names):
    flat = jnp.concatenate([arrs[k].reshape(-1) for k in names])
    pad = -flat.shape[0] % (PACK_LANES * PACK_ROWS)
    return jnp.pad(flat, (0, pad)).reshape(-1, PACK_LANES)


def _pack_small(arrs):
    return jnp.concatenate([_pack_group(arrs, names) for names in PACK_GROUPS], axis=0)


def _unpack_small(packed, like):
    out, row = {}, 0
    for names in PACK_GROUPS:
        rows = _pack_group(like, names).shape[0]
        flat, pos = packed[row:row + rows].reshape(-1), 0
        for k in names:
            n = like[k].size
            out[k] = flat[pos:pos + n].reshape(like[k].shape)
            pos += n
        row += rows
    return out


def _local_step(x, target, big, small, shards=None, shard_shapes=None, h1=None):
    s, d = x.shape
    big, grads, slots = dict(big), {}, {}
    carry = shards is not None

    def gathering(names, call):
        if not carry:
            return call(None)
        res, got = call(_gather_side({k: shards[k] for k in names}))
        big.update(zip(names, got))
        return res

    def scattering(names, call):
        if not carry:
            return call(None)
        res, got = call(_scatter_side({k: grads[k] for k in names}, shard_shapes))
        slots.update(zip(names, got))
        return res

    u_off = 3 * HQ
    gate_off = u_off + d // 2
    g1, g2, g3, g4 = (small[k][0:1] for k in ("norm_mix_pre", "norm_mix_post", "norm_ffn_pre", "norm_ffn_post"))
    ssm_names = ("ssm_a_re", "ssm_a_im", "ssm_log_dt", "ssm_b_re", "ssm_b_im", "ssm_c_re", "ssm_c_im")
    (lam, bmat, cmat), ssm_vjp = jax.vjp(_ssm_prepare, *[small[k][0] for k in ssm_names])
    bmat, cmat = bmat.astype(BF16), cmat.astype(BF16)
    dskip = small["ssm_d"][0:1]

    if h1 is None:
        h1 = _norm_in(x, g1)
    z = gathering(("w_attn_up", "w_glu_v", "w_glu_g", "w_out", "w_ffn_gate"),
                  lambda side: _mm(h1, big["w_in"], "nn", F32, "in_proj", side=side))
    y, yg, xin, xall = gathering(("w_ffn_up",), lambda side: _ssm_fwd(z, bmat, cmat, lam, dskip, u_off, side=side))
    qkv = [_dilate_qkv(z, g, dil) for g, dil in enumerate(ATTN_DILATIONS)]
    outs, lses = zip(*[_attn_fwd(qkv[g], g, dil) for g, dil in enumerate(ATTN_DILATIONS)])
    attn = _attn_merge(outs, lses)
    merged, ab, gv, gg = _mm_fused(
        [attn, yg], [big["w_attn_up"], big["w_glu_v"], big["w_glu_g"]], [(0, 0), (1, 1), (1, 2)], "nn",
        [BF16, BF16, BF16, BF16], "branches_merge", extras=[(z, gate_off), (z, gate_off + d)], epilogue=_gates_epilogue)
    mo = _mm(merged, big["w_out"], "nn", F32, "mix_out")
    x2, h2 = _norm_mid(x, mo, g2, g3)
    act, fg, fu = gathering(("w_ffn_down",), lambda side: _mm_fused(
        [h2], [big["w_ffn_gate"], big["w_ffn_up"]], [(0, 0), (0, 1)], "nn", [BF16, BF16, BF16], "ffn_up_act",
        epilogue=_swiglu_epilogue, side=side))
    f = _mm(act, big["w_ffn_down"], "nn", F32, "ffn_down")
    loss, dout, df, dg4 = _loss_head(x2, f, g4, target)

    grads["w_ffn_down"] = _mm_kloop(act, df, "tn", BF16, "dw_ffn_down")
    dfg, dfu = scattering(("w_ffn_down",), lambda side: _mm_fused(
        [df], [big["w_ffn_down"]], [(0, 0)], "nt", [BF16, BF16], "d_ffn_act", extras=[(fg, 0), (fu, 0)],
        epilogue=_swiglu_bwd_epilogue, side=side))
    grads["w_ffn_gate"] = _mm_kloop(h2, dfg, "tn", BF16, "dw_ffn_gate")
    dh2 = scattering(("w_ffn_gate",), lambda side: _mm_fused(
        [dfg, dfu], [big["w_ffn_gate"], big["w_ffn_up"]], [(0, 0), (1, 1)], "nt", [F32], "d_h2",
        epilogue=_sum_epilogue, side=side))[0]
    grads["w_ffn_up"] = _mm_kloop(h2, dfu, "tn", BF16, "dw_ffn_up")
    dx2, dmo, dg2, dg3 = _norm_mid_bwd(x2, mo, g2, g3, dout, dh2)
    dz, dgs, dab, dgv, dgg = _mm_fused(
        [dmo], [big["w_out"]], [(0, 0)], "nt", [BF16] * 5, "d_merged_gates",
        extras=[(z, gate_off), (z, gate_off + d), (ab, 0), (gv, 0), (gg, 0)], epilogue=_gates_bwd_epilogue,
        out_place=[(z.shape[1], gate_off), None, None, None, None])
    dz = _put_cols(dz, dgs, gate_off + d)
    grads["w_out"] = _mm_kloop(merged, dmo, "tn", BF16, "dw_out")
    dyg = _mm_fused([dgv, dgg], [big["w_glu_v"], big["w_glu_g"]], [(0, 0), (1, 1)], "nt", [F32], "d_yg",
                    epilogue=_sum_epilogue)[0]
    grads["w_glu_v"] = _mm_kloop(yg, dgv, "tn", BF16, "dw_glu_v")
    grads["w_glu_g"] = _mm_kloop(yg, dgg, "tn", BF16, "dw_glu_g")
    du, dbmat, dcmat, dlam, dd = scattering(
        ("w_ffn_up",),
        lambda side: _ssm_bwd(z, y, dyg, xin, xall, bmat, cmat, lam, dskip, u_off, side=side))
    dz = _put_cols(dz, du, u_off)
    dattn = _mm(dab, big["w_attn_up"], "nt", F32, "d_attn")
    grads["w_attn_up"] = _mm_kloop(attn, dab, "tn", BF16, "dw_attn_up")
    merged_bwd = _attn_merge_bwd(outs, lses, dattn)
    mine, theirs = {}, {}
    for g, dil in enumerate(ATTN_DILATIONS):
        side = None
        if carry and g == 0:
            mine = {k: _sum_slots(slots[k], k) for k in slots}
            side = _swap_side(list(mine.values()))
        dqkv = _attn_bwd(qkv[g], merged_bwd[g], lses[g], merged_bwd[3 + g], g, dil, side=side)
        if side is not None:
            dqkv, got = dqkv
            theirs = dict(zip(mine, got))
        dz = _undilate_dqkv(dqkv, dz, g, dil)
    small_grads = dict(zip(ssm_names, (t[None] for t in ssm_vjp((dlam, dbmat, dcmat)))))
    small_grads.update(norm_mix_post=dg2, norm_ffn_pre=dg3, norm_ffn_post=dg4, ssm_d=dd)
    if carry:
        late = ("w_attn_up", "w_out", "w_glu_v", "w_glu_g")
        side = _join_sides(_scatter_side({k: grads[k] for k in late}, shard_shapes),
                           _share_side(_pack_group(small_grads, PACK_GROUPS[1])))
        grads["w_in"], got = _mm_kloop(h1, dz, "tn", BF16, "dw_in", side=side)
        slots.update(zip(late, got[:-1]))
        shared = got[-1]
    else:
        grads["w_in"] = _mm_kloop(h1, dz, "tn", BF16, "dw_in")
    dh1 = scattering(("w_in",), lambda side: _mm_kloop(dz, big["w_in"], "nt", F32, "d_h1", side=side))
    grad_x, dg1 = _norm_in_bwd(x, g1, dh1, dx2)
    small_grads["norm_mix_pre"] = dg1
    if carry:
        return loss[0, 0], grad_x, (slots, mine, theirs), (dg1, shared)
    return loss[0, 0], grad_x, grads, small_grads


def kernel(x, norm_mix_pre, w_in, w_attn_up, ssm_a_re, ssm_a_im, ssm_log_dt, ssm_b_re, ssm_b_im, ssm_c_re, ssm_c_im, ssm_d, w_glu_v, w_glu_g, w_out, norm_mix_post, norm_ffn_pre, w_ffn_gate, w_ffn_up, w_ffn_down, norm_ffn_post, loss_target, m_norm_mix_pre, m_w_in, m_w_attn_up, m_ssm_a_re, m_ssm_a_im, m_ssm_log_dt, m_ssm_b_re, m_ssm_b_im, m_ssm_c_re, m_ssm_c_im, m_ssm_d, m_w_glu_v, m_w_glu_g, m_w_out, m_norm_mix_post, m_norm_ffn_pre, m_w_ffn_gate, m_w_ffn_up, m_w_ffn_down, m_norm_ffn_post, v_norm_mix_pre, v_w_in, v_w_attn_up, v_ssm_a_re, v_ssm_a_im, v_ssm_log_dt, v_ssm_b_re, v_ssm_b_im, v_ssm_c_re, v_ssm_c_im, v_ssm_d, v_w_glu_v, v_w_glu_g, v_w_out, v_norm_mix_post, v_norm_ffn_pre, v_w_ffn_gate, v_w_ffn_up, v_w_ffn_down, v_norm_ffn_post):
    given = dict(locals())
    w = {k: given[k] for k in WEIGHT_ORDER}
    m = {k: given["m_" + k] for k in WEIGHT_ORDER}
    v = {k: given["v_" + k] for k in WEIGHT_ORDER}

    shard_shapes = {k: w[k].shape[1:] for k in BIG_WEIGHTS}
    shards = {"w_in": _cast_bf16(w["w_in"][0], "w_in")}
    h1, casts, got = _prologue(x[0], norm_mix_pre[0:1], {k: w[k][0] for k in BIG_WEIGHTS if k != "w_in"},
                               side=_gather_side({"w_in": shards["w_in"]}))
    shards.update(casts)
    big = {"w_in": got[0]}

    loss, grad_x, (slots, mine, theirs), small_grads = _local_step(
        x[0], loss_target[0], big, {k: w[k] for k in SMALL_WEIGHTS}, shards, shard_shapes, h1)
    loss = lax.psum(loss, MESH_AXES)

    early = [k for k in BIG_WEIGHTS if k in mine]
    last = [k for k in BIG_WEIGHTS if k not in mine]
    mine.update({k: _sum_slots(slots[k], k) for k in last})
    dg1, shared = small_grads
    last_exchanges = _join_sides(_swap_side([mine[k] for k in last]),
                                 _share_side(_pack_group({"norm_mix_pre": dg1}, PACK_GROUPS[0])))
    out_g, out_d, out_m, out_v = {}, {}, {}, {}
    for k in early + last:
        if k == early[0]:
            res, got = _adamw_big(w[k][0], mine[k], theirs[k], m[k][0], v[k][0], k, side=last_exchanges)
            theirs.update(zip(last, got[:-1]))
            late = got[-1]
        else:
            res = _adamw_big(w[k][0], mine[k], theirs[k], m[k][0], v[k][0], k)
        out_g[k], out_d[k], out_m[k], out_v[k] = (t[None] for t in res)

    pick = lambda tree: {k: tree[k] for k in SMALL_WEIGHTS}
    parts = jnp.concatenate([late, shared], axis=1)
    res = _adamw_small(_pack_small(pick(w)), parts, _pack_small(pick(m)), _pack_small(pick(v)))
    for dst, packed in zip((out_g, out_d, out_m, out_v), res):
        dst.update(_unpack_small(packed, pick(w)))

    return (loss, grad_x[None], *[out_g[k] for k in WEIGHT_ORDER], *[out_d[k] for k in WEIGHT_ORDER],
            *[out_m[k] for k in WEIGHT_ORDER], *[out_v[k] for k in WEIGHT_ORDER])
```

```python
import functools
import math

import jax
import jax.numpy as jnp
from jax import lax
from jax.experimental import pallas as pl
from jax.experimental.pallas import tpu as pltpu

F32 = jnp.float32
BF16 = jnp.bfloat16

EPS = 1e-6
HEAD_DIM = 128
HEADS_PER_GROUP = 4
ATTN_DILATIONS = (1, 4, 16)
ATTN_BLK = 128
N_ATTN_HEADS = HEADS_PER_GROUP * len(ATTN_DILATIONS)
GROUP_W = HEADS_PER_GROUP * HEAD_DIM
HQ = N_ATTN_HEADS * HEAD_DIM
SSM_GROUP = 16
SSM_STATE = 64
SSM_TILE_CH = 128
SSM_TILE_ST = SSM_TILE_CH // SSM_GROUP * SSM_STATE
ADAM_LR = 0.001
ADAM_B1 = 0.9
ADAM_B2 = 0.999
ADAM_EPS = 1e-08
ADAM_WD = 0.01
ADAM_STEP = 10
NEG_BIG = -1e30
V7X_VMEM_LIMIT = 56 * 1024 * 1024
MESH_AXES = ("x", "y", "c")
N_CHIPS = 4


def _pick(n, cands):
    for c in cands:
        if n % c == 0:
            return c
    raise ValueError(f"no tile of {cands} divides {n}")


def _params(sem):
    return pltpu.CompilerParams(dimension_semantics=sem, vmem_limit_bytes=V7X_VMEM_LIMIT)


HBM = pl.BlockSpec(memory_space=pl.ANY)
MESH = pl.DeviceIdType.MESH


class _Side:
    def __init__(self, srcs, out_shapes, sem_shapes, build, aliases=None, relays=False):
        self.srcs, self.out_shapes, self.sem_shapes, self.build = list(srcs), list(out_shapes), list(sem_shapes), build
        self.aliases = dict(aliases or {})
        self.relays = relays

    def start(self, src, dst, sems):
        local, sends = self.build(src, dst, sems)[:2]
        for cp in local + sends:
            cp.start()

    def relay(self, src, dst, sems):
        _, _, arrivals, forwards, _ = self.build(src, dst, sems)
        for cp, forward in zip(arrivals, forwards):
            if forward is not None:
                cp.wait_recv()
                forward.start()

    def wait(self, src, dst, sems, relayed=False):
        local, sends, arrivals, forwards, passed_on = self.build(src, dst, sems)
        for cp, forward in zip(arrivals, forwards):
            if forward is None:
                cp.wait_recv()
            elif not relayed:
                cp.wait_recv()
                forward.start()
        for cp in passed_on:
            cp.wait_recv()
        for cp in sends + [f for f in forwards if f is not None]:
            cp.wait_send()
        for cp in local:
            cp.wait()


def _join_sides(a, b):
    ns, no, nm = len(a.srcs), len(a.out_shapes), len(a.sem_shapes)

    def build(src, dst, sems):
        ra, rb = a.build(src[:ns], dst[:no], sems[:nm]), b.build(src[ns:], dst[no:], sems[nm:])
        return tuple(p + q for p, q in zip(ra, rb))

    aliases = {**a.aliases, **{ns + k: no + v for k, v in b.aliases.items()}}
    return _Side(a.srcs + b.srcs, a.out_shapes + b.out_shapes, a.sem_shapes + b.sem_shapes, build, aliases,
                 a.relays or b.relays)


def _call(body, *, name, grid, in_specs, out_specs, out_shape, semantics, args, scratch_shapes=(), side=None, **kw):
    in_specs, out_specs, out_shape, scratch_shapes = list(in_specs), list(out_specs), list(out_shape), list(scratch_shapes)
    if side is None:
        res = pl.pallas_call(body, name=name, grid=grid, in_specs=in_specs, out_specs=out_specs, out_shape=out_shape,
                             scratch_shapes=scratch_shapes, compiler_params=_params(semantics), **kw)(*args)
        return list(res), []
    n_in, n_out, n_scr = len(in_specs), len(out_specs), len(scratch_shapes)
    ns_in, ns_out = len(side.srcs), len(side.out_shapes)
    n_steps = math.prod(grid)
    relay_at = (3 * n_steps) // 4 if side.relays and n_steps >= 4 else None

    def carrying(*refs):
        ins, s_in = refs[:n_in], refs[n_in:n_in + ns_in]
        o0 = n_in + ns_in
        outs, s_out = refs[o0:o0 + n_out], refs[o0 + n_out:o0 + n_out + ns_out]
        c0 = o0 + n_out + ns_out
        scr, sems = refs[c0:c0 + n_scr], refs[c0 + n_scr:]
        step = functools.reduce(lambda acc, ig: acc * ig[1] + pl.program_id(ig[0]), enumerate(grid), 0)

        @pl.when(step == 0)
        def _():
            side.start(s_in, s_out, sems)

        if relay_at is not None:
            @pl.when(step == relay_at)
            def _():
                side.relay(s_in, s_out, sems)

        body(*ins, *outs, *scr)

        @pl.when(step == n_steps - 1)
        def _():
            side.wait(s_in, s_out, sems, relayed=relay_at is not None)

    res = pl.pallas_call(
        carrying, name=name, grid=grid, in_specs=in_specs + [HBM] * ns_in, out_specs=out_specs + [HBM] * ns_out,
        out_shape=out_shape + side.out_shapes, scratch_shapes=scratch_shapes + side.sem_shapes,
        input_output_aliases={n_in + k: n_out + v for k, v in side.aliases.items()},
        compiler_params=pltpu.CompilerParams(dimension_semantics=("arbitrary",) * len(grid),
                                             vmem_limit_bytes=V7X_VMEM_LIMIT, has_side_effects=True), **kw,
    )(*args, *side.srcs)
    return list(res[:n_out]), list(res[n_out:])


_DOT_DIMS = {"nn": (((1,), (0,)), ((), ())), "nt": (((1,), (1,)), ((), ())), "tn": (((0,), (0,)), ((), ()))}


MM_VMEM_BUDGET = 44 * 1024 * 1024
MM_STEP_BYTES = 1 << 20
MM_ACC_BYTES = 4
MM_EPILOGUE_COLS = 256


def _size(dtype):
    return jnp.dtype(dtype).itemsize


def _mm_fused(as_, bs, pairs, mode, out_dtypes, name, extras=(), epilogue=None, side=None, out_place=None):
    M = as_[0].shape[0]
    N = bs[0].shape[1] if mode == "nn" else bs[0].shape[0]
    ks_a = [a.shape[1] for a in as_]
    ks_b = [b.shape[0] if mode == "nn" else b.shape[1] for b in bs]
    chunked = epilogue is not None
    if epilogue is None:
        epilogue = lambda rs, es: rs
    offs = [off for _, off in extras]
    place = list(out_place) if out_place else [None] * len(out_dtypes)
    offs_all = offs + [p[1] for p in place if p is not None]
    best = None
    for tm in (2048, 1024, 512, 256, 128):
        for tn in (2048, 1024, 512, 256, 128):
            if M % tm or N % tn or any(off % tn for off in offs_all):
                continue
            vmem = (sum(2 * tm * k * 2 for k in ks_a) + sum(2 * k * tn * 2 for k in ks_b)
                    + sum(2 * tm * tn * _size(d) for d in out_dtypes) + sum(2 * tm * tn * _size(e.dtype) for e, _ in extras)
                    + len(pairs) * tm * tn * 4)
            cost = sum(k * N * 2 for k in ks_b) * (M // tm) + (M // tm) * (N // tn) * MM_STEP_BYTES
            if vmem <= MM_VMEM_BUDGET and (best is None or cost < best[0]):
                best = (cost, tm, tn)
    _, tm, tn = best
    na, nb, ne, no = len(as_), len(bs), len(extras), len(out_dtypes)
    dims = _DOT_DIMS[mode]

    sub = MM_EPILOGUE_COLS if chunked and tn % MM_EPILOGUE_COLS == 0 else tn

    def body(*refs):
        a_refs, b_refs = refs[:na], refs[na:na + nb]
        e_refs, o_refs = refs[na + nb:na + nb + ne], refs[na + nb + ne:]
        for c0 in range(0, tn, sub):
            cs = slice(c0, c0 + sub)
            rs = [lax.dot_general(a_refs[ai][...], b_refs[bi][:, cs] if mode == "nn" else b_refs[bi][cs, :], dims,
                                  preferred_element_type=F32) for ai, bi in pairs]
            outs = epilogue(rs, [e[:, cs] for e in e_refs])
            for o_ref, o in zip(o_refs, outs):
                o_ref[:, cs] = o.astype(o_ref.dtype)

    a_specs = [pl.BlockSpec((tm, k), lambda i, j: (i, 0)) for k in ks_a]
    if mode == "nn":
        b_specs = [pl.BlockSpec((k, tn), lambda i, j: (0, j)) for k in ks_b]
    else:
        b_specs = [pl.BlockSpec((tn, k), lambda i, j: (j, 0)) for k in ks_b]
    e_specs = [pl.BlockSpec((tm, tn), lambda i, j, o=off // tn: (i, o + j)) for off in offs]
    o_specs = [pl.BlockSpec((tm, tn), lambda i, j, o=(p[1] // tn if p else 0): (i, o + j)) for p in place]
    outs, carried = _call(
        body, name=name, grid=(M // tm, N // tn), in_specs=a_specs + b_specs + e_specs, out_specs=o_specs,
        out_shape=[jax.ShapeDtypeStruct((M, p[0] if p else N), d) for d, p in zip(out_dtypes, place)],
        semantics=("parallel", "arbitrary"),
        args=[*as_, *bs, *[e for e, _ in extras]], side=side)
    return outs if side is None else (outs, carried)


def _mm(a, b, mode, out_dtype, name, side=None):
    res = _mm_fused([a], [b], [(0, 0)], mode, [out_dtype], name, side=side)
    return res[0] if side is None else (res[0][0], res[1])


def _mm_kloop(a, b, mode, out_dtype, name, second=None, side=None):
    if mode == "nn":
        (M, K), (_, N) = a.shape, b.shape
    elif mode == "nt":
        (M, K), (N, _) = a.shape, b.shape
    else:
        (K, M), (_, N) = a.shape, b.shape
    products = 1 if second is None else 2
    best = None
    for tm in (2816, 2048, 1408, 1024, 512, 256, 128):
        for tn in (2816, 2432, 2048, 1408, 1024, 512, 256, 128):
            for tk in (2816, 2432, 2048, 1408, 1024, 512, 256, 128):
                if M % tm or N % tn or K % tk:
                    continue
                vmem = 2 * tm * tn * 4 + 2 * tm * tn * _size(out_dtype) + products * 2 * tk * (tm + tn) * 2
                steps = (M // tm) * (N // tn) * (K // tk)
                cost = (K * M * 2 * (N // tn) + K * N * 2 * (M // tm) + steps * MM_STEP_BYTES
                        + steps * tm * tn * MM_ACC_BYTES)
                if vmem <= MM_VMEM_BUDGET and (best is None or cost < best[0]):
                    best = (cost, tm, tn, tk)
    _, tm, tn, tk = best
    nk = K // tk
    dims = _DOT_DIMS[mode]

    def body(*refs):
        o_ref, acc_ref = refs[-2:]
        k = pl.program_id(2)

        @pl.when(k == 0)
        def _():
            acc_ref[...] = jnp.zeros_like(acc_ref)

        for p in range(products):
            @pl.when(jnp.logical_and(k >= p * nk, k < (p + 1) * nk))
            def _(p=p):
                acc_ref[...] += lax.dot_general(refs[2 * p][...], refs[2 * p + 1][...], dims, preferred_element_type=F32)

        @pl.when(k == products * nk - 1)
        def _():
            o_ref[...] = acc_ref[...].astype(o_ref.dtype)

    def a_spec(p):
        kk = lambda k: jnp.clip(k - p * nk, 0, nk - 1)
        if mode == "tn":
            return pl.BlockSpec((tk, tm), lambda i, j, k: (kk(k), i))
        return pl.BlockSpec((tm, tk), lambda i, j, k: (i, kk(k)))

    def b_spec(p):
        kk = lambda k: jnp.clip(k - p * nk, 0, nk - 1)
        if mode == "nt":
            return pl.BlockSpec((tn, tk), lambda i, j, k: (j, kk(k)))
        return pl.BlockSpec((tk, tn), lambda i, j, k: (kk(k), j))

    o_spec = pl.BlockSpec((tm, tn), lambda i, j, k: (i, j))
    operands = (a, b) + (tuple(second) if second is not None else ())
    outs, carried = _call(
        body, name=name, grid=(M // tm, N // tn, products * nk),
        in_specs=[spec(p) for p in range(products) for spec in (a_spec, b_spec)], out_specs=[o_spec],
        out_shape=[jax.ShapeDtypeStruct((M, N), out_dtype)], scratch_shapes=[pltpu.VMEM((tm, tn), F32)],
        semantics=("parallel", "parallel", "arbitrary"), args=operands, side=side)
    return outs[0] if side is None else (outs[0], carried)


def _sigmoid(v):
    return 0.5 * jnp.tanh(0.5 * v) + 0.5


_GELU_C = math.sqrt(2.0 / math.pi)


def _gelu(v):
    return 0.5 * v * (1.0 + jnp.tanh(_GELU_C * (v + 0.044715 * v * v * v)))


def _gelu_grad(v):
    t = jnp.tanh(_GELU_C * (v + 0.044715 * v * v * v))
    return 0.5 * (1.0 + t) + 0.5 * v * (1.0 - t * t) * _GELU_C * (1.0 + 3.0 * 0.044715 * v * v)


def _rms(v, gain):
    r = lax.rsqrt(jnp.mean(v * v, axis=-1, keepdims=True) + EPS)
    return v * r * gain


def _rms_bwd(v, gain, dy):
    r = lax.rsqrt(jnp.mean(v * v, axis=-1, keepdims=True) + EPS)
    a = dy * gain
    dv = r * a - v * (r * r * r) * jnp.mean(a * v, axis=-1, keepdims=True)
    return dv, dy * v * r


def _row_tile(s, few_operands=False):
    return _pick(s, ((512,) if few_operands else ()) + (256, 128, 64, 8))


def _norm_in(x, gain):
    s, d = x.shape
    tr = _row_tile(s)

    def body(x_ref, g_ref, h_ref):
        h_ref[...] = _rms(x_ref[...], g_ref[...]).astype(BF16)

    row = pl.BlockSpec((tr, d), lambda i: (i, 0))
    vec = pl.BlockSpec((1, d), lambda i: (0, 0))
    return pl.pallas_call(body, name="norm_in", grid=(s // tr,), in_specs=[row, vec], out_specs=row,
                          out_shape=jax.ShapeDtypeStruct((s, d), BF16), compiler_params=_params(("parallel",)))(x, gain)


def _prologue(x, gain, weights, side=None):
    s, d = x.shape
    tr = _row_tile(s)
    steps = s // tr
    names = list(weights)
    tiles = []
    for k in names:
        r, _ = weights[k].shape
        tiles.append(next(t for t in range(16, r + 1, 16) if r % t == 0 and r // t <= steps))

    def body(*refs):
        x_ref, g_ref = refs[:2]
        w_refs, h_ref, o_refs = refs[2:2 + len(names)], refs[2 + len(names)], refs[3 + len(names):]
        h_ref[...] = _rms(x_ref[...], g_ref[...]).astype(BF16)
        for w_ref, o_ref in zip(w_refs, o_refs):
            o_ref[...] = w_ref[...].astype(BF16)

    row = pl.BlockSpec((tr, d), lambda i: (i, 0))
    w_specs = [pl.BlockSpec((t, weights[k].shape[1]), lambda i, last=weights[k].shape[0] // t - 1: (jnp.minimum(i, last), 0))
               for k, t in zip(names, tiles)]
    outs, carried = _call(
        body, name="prologue", grid=(steps,), in_specs=[row, pl.BlockSpec((1, d), lambda i: (0, 0))] + w_specs,
        out_specs=[row] + w_specs,
        out_shape=[jax.ShapeDtypeStruct((s, d), BF16)] + [jax.ShapeDtypeStruct(weights[k].shape, BF16) for k in names],
        semantics=("arbitrary",), args=[x, gain] + [weights[k] for k in names], side=side)
    return outs[0], dict(zip(names, outs[1:])), carried


def _norm_mid(x, mo, g_post, g_pre):
    s, d = x.shape
    tr = _row_tile(s, few_operands=True)

    def body(x_ref, mo_ref, g2_ref, g3_ref, x2_ref, h2_ref):
        x2 = x_ref[...] + _rms(mo_ref[...], g2_ref[...])
        x2_ref[...] = x2
        h2_ref[...] = _rms(x2, g3_ref[...]).astype(BF16)

    row = pl.BlockSpec((tr, d), lambda i: (i, 0))
    vec = pl.BlockSpec((1, d), lambda i: (0, 0))
    return pl.pallas_call(
        body, name="norm_mid", grid=(s // tr,), in_specs=[row, row, vec, vec], out_specs=[row, row],
        out_shape=[jax.ShapeDtypeStruct((s, d), F32), jax.ShapeDtypeStruct((s, d), BF16)],
        compiler_params=_params(("parallel",)))(x, mo, g_post, g_pre)


def _loss_head(x2, f, g_post, target):
    s, d = x2.shape
    tr = _row_tile(s, few_operands=True)

    def body(x2_ref, f_ref, g_ref, t_ref, loss_ref, dout_ref, df_ref, dg_ref):
        @pl.when(pl.program_id(0) == 0)
        def _():
            loss_ref[...] = jnp.zeros_like(loss_ref)
            dg_ref[...] = jnp.zeros_like(dg_ref)

        fv = f_ref[...]
        g = g_ref[...]
        err = x2_ref[...] + _rms(fv, g) - t_ref[...]
        loss_ref[...] += 0.5 * jnp.sum(jnp.mean(err * err, axis=-1, keepdims=True), axis=0, keepdims=True)
        dout = err * (1.0 / d)
        dout_ref[...] = dout
        df, dg = _rms_bwd(fv, g, dout)
        df_ref[...] = df.astype(BF16)
        dg_ref[...] += jnp.sum(dg, axis=0, keepdims=True)

    row = pl.BlockSpec((tr, d), lambda i: (i, 0))
    vec = pl.BlockSpec((1, d), lambda i: (0, 0))
    one = pl.BlockSpec((1, 1), lambda i: (0, 0))
    return pl.pallas_call(
        body, name="loss_head", grid=(s // tr,), in_specs=[row, row, vec, row], out_specs=[one, row, row, vec],
        out_shape=[jax.ShapeDtypeStruct((1, 1), F32), jax.ShapeDtypeStruct((s, d), F32),
                   jax.ShapeDtypeStruct((s, d), BF16), jax.ShapeDtypeStruct((1, d), F32)],
        compiler_params=_params(("arbitrary",)))(x2, f, g_post, target)


def _norm_mid_bwd(x2, mo, g_post, g_pre, dout, dh2):
    s, d = x2.shape
    tr = _row_tile(s)

    def body(x2_ref, mo_ref, g2_ref, g3_ref, dout_ref, dh2_ref, dx2_ref, dmo_ref, dg2_ref, dg3_ref):
        @pl.when(pl.program_id(0) == 0)
        def _():
            dg2_ref[...] = jnp.zeros_like(dg2_ref)
            dg3_ref[...] = jnp.zeros_like(dg3_ref)

        dv, dg3 = _rms_bwd(x2_ref[...], g3_ref[...], dh2_ref[...])
        dx2 = dout_ref[...] + dv
        dx2_ref[...] = dx2
        dmo, dg2 = _rms_bwd(mo_ref[...], g2_ref[...], dx2)
        dmo_ref[...] = dmo.astype(BF16)
        dg2_ref[...] += jnp.sum(dg2, axis=0, keepdims=True)
        dg3_ref[...] += jnp.sum(dg3, axis=0, keepdims=True)

    row = pl.BlockSpec((tr, d), lambda i: (i, 0))
    vec = pl.BlockSpec((1, d), lambda i: (0, 0))
    return pl.pallas_call(
        body, name="norm_mid_bwd", grid=(s // tr,), in_specs=[row, row, vec, vec, row, row],
        out_specs=[row, row, vec, vec],
        out_shape=[jax.ShapeDtypeStruct((s, d), F32), jax.ShapeDtypeStruct((s, d), BF16),
                   jax.ShapeDtypeStruct((1, d), F32), jax.ShapeDtypeStruct((1, d), F32)],
        compiler_params=_params(("arbitrary",)))(x2, mo, g_post, g_pre, dout, dh2)


def _norm_in_bwd(x, gain, dh, dx2):
    s, d = x.shape
    tr = _row_tile(s, few_operands=True)

    def body(x_ref, g_ref, dh_ref, dx2_ref, dx_ref, dg_ref):
        @pl.when(pl.program_id(0) == 0)
        def _():
            dg_ref[...] = jnp.zeros_like(dg_ref)

        dv, dg = _rms_bwd(x_ref[...], g_ref[...], dh_ref[...])
        dx_ref[...] = dx2_ref[...] + dv
        dg_ref[...] += jnp.sum(dg, axis=0, keepdims=True)

    row = pl.BlockSpec((tr, d), lambda i: (i, 0))
    vec = pl.BlockSpec((1, d), lambda i: (0, 0))
    return pl.pallas_call(
        body, name="norm_in_bwd", grid=(s // tr,), in_specs=[row, vec, row, row], out_specs=[row, vec],
        out_shape=[jax.ShapeDtypeStruct((s, d), F32), jax.ShapeDtypeStruct((1, d), F32)],
        compiler_params=_params(("arbitrary",)))(x, gain, dh, dx2)


def _swiglu_epilogue(rs, es):
    g, u = rs
    return [g * _sigmoid(g) * u, g, u]


def _swiglu_bwd_epilogue(rs, es):
    d = rs[0]
    g, u = es[0].astype(F32), es[1].astype(F32)
    sg = _sigmoid(g)
    return [d * u * sg * (1.0 + g * (1.0 - sg)), d * g * sg]


def _sum_epilogue(rs, es):
    return [rs[0] + rs[1]]


def _gates_epilogue(rs, es):
    ab, gv, gg = rs
    ga, gs = es
    return [_sigmoid(ga) * ab + _sigmoid(gs) * gv * _sigmoid(gg), ab, gv, gg]


def _gates_bwd_epilogue(rs, es):
    dm = rs[0]
    ga, gs, ab, gv, gg = (e.astype(F32) for e in es)
    sa, ss, sg = _sigmoid(ga), _sigmoid(gs), _sigmoid(gg)
    dsb = dm * ss
    return [dm * ab * sa * (1.0 - sa), dm * gv * sg * ss * (1.0 - ss), dm * sa, dsb * sg, dsb * gv * sg * (1.0 - sg)]


ATTN_ROWS = 2048


def _dilate_qkv(z, g, d):
    s = z.shape[0]
    tm = _pick(s, (2 * ATTN_ROWS, ATTN_ROWS))
    per = tm // d
    nh = HEADS_PER_GROUP

    def body(z_ref, o_ref):
        for r in range(d):
            rows = z_ref[...] if d == 1 else z_ref[pl.ds(r, per, stride=d), :]
            o_ref[0, r] = rows.astype(BF16)

    return pl.pallas_call(
        body, name=f"dilate_qkv_{g}", grid=(s // tm, 3, nh),
        in_specs=[pl.BlockSpec((tm, HEAD_DIM), lambda i, w, h: (i, (3 * w + g) * nh + h))],
        out_specs=pl.BlockSpec((1, d, per, HEAD_DIM), lambda i, w, h: (w, 0, i, h)),
        out_shape=jax.ShapeDtypeStruct((3, d, s // d, GROUP_W), BF16),
        compiler_params=_params(("parallel", "parallel", "parallel")))(z)


def _undilate_dqkv(dqkv, dz, g, d):
    s = dz.shape[0]
    tm = _pick(s, (2 * ATTN_ROWS, ATTN_ROWS))
    per = tm // d
    nh = HEADS_PER_GROUP

    def body(i_ref, dz_ref, o_ref, nat_ref):
        del dz_ref
        if d == 1:
            o_ref[...] = i_ref[0, 0]
        else:
            for r in range(d):
                nat_ref[pl.ds(r, per, stride=d), :] = i_ref[0, r].astype(F32)
            o_ref[...] = nat_ref[...].astype(BF16)

    return pl.pallas_call(
        body, name=f"undilate_dqkv_{g}", grid=(s // tm, 3, nh),
        in_specs=[pl.BlockSpec((1, d, per, HEAD_DIM), lambda i, w, h: (w, 0, i, h)),
                  pl.BlockSpec(memory_space=pl.ANY)],
        out_specs=pl.BlockSpec((tm, HEAD_DIM), lambda i, w, h: (i, (3 * w + g) * nh + h)),
        out_shape=jax.ShapeDtypeStruct(dz.shape, dz.dtype), input_output_aliases={1: 0},
        scratch_shapes=[pltpu.VMEM((tm, HEAD_DIM), F32)],
        compiler_params=_params(("parallel", "parallel", "parallel")))(dqkv, dz)


def _alibi_slope(head):
    return 2.0 ** (-8.0 * (head + 1) / N_ATTN_HEADS)


def _dot_nt(a, b):
    return lax.dot_general(a, b, _DOT_DIMS["nt"], preferred_element_type=F32)


def _dot_tn(a, b):
    return lax.dot_general(a, b, _DOT_DIMS["tn"], preferred_element_type=F32)


def _dot(a, b):
    return jnp.dot(a, b, preferred_element_type=F32)


GROUP_ROWS = HEADS_PER_GROUP * ATTN_BLK


def _band_bias(g, d, pairs):
    qi = jnp.arange(ATTN_BLK)[:, None]
    ki = jnp.arange(ATTN_BLK)[None, :]
    rows = []
    for hh in range(HEADS_PER_GROUP):
        slope_d = _alibi_slope(g * HEADS_PER_GROUP + hh) * d
        tiles = []
        for kind in pairs:
            dist = qi - ki if kind == "cur" else ATTN_BLK + qi - ki
            ok = dist >= 0 if kind == "cur" else dist <= ATTN_BLK
            tiles.append(jnp.where(ok, -slope_d * dist.astype(F32), NEG_BIG))
        rows.append(jnp.concatenate(tiles, axis=1))
    return jnp.concatenate(rows, axis=0).astype(F32)


def _tile_cols(t):
    return slice(t * ATTN_BLK, (t + 1) * ATTN_BLK)


def _attn_fwd(qkv, g, d):
    _, _, L, _ = qkv.shape
    nb = L // ATTN_BLK
    scale = HEAD_DIM ** -0.5

    def body(q_ref, kc_ref, kp_ref, vc_ref, vp_ref, bias_ref, o_ref, lse_ref, s_ref, p_ref):
        n = pl.program_id(1)
        for hh in range(HEADS_PER_GROUP):
            cols, rows = _tile_cols(hh), _tile_cols(hh)
            q = q_ref[0, 0, :, cols]
            s_ref[rows, _tile_cols(0)] = _dot_nt(q, kp_ref[0, 0, :, cols])
            s_ref[rows, _tile_cols(1)] = _dot_nt(q, kc_ref[0, 0, :, cols])
        col = lax.broadcasted_iota(jnp.int32, (GROUP_ROWS, 2 * ATTN_BLK), 1)
        s = s_ref[...] * scale + bias_ref[...]
        s = jnp.where(jnp.logical_and(col < ATTN_BLK, n == 0), NEG_BIG, s)
        m = jnp.max(s, axis=-1, keepdims=True)
        e = jnp.exp(s - m)
        l = jnp.sum(e, axis=-1, keepdims=True)
        p_ref[...] = (e * (1.0 / l)).astype(BF16)
        lse = m + jnp.log(l)
        for hh in range(HEADS_PER_GROUP):
            cols, rows = _tile_cols(hh), _tile_cols(hh)
            o_ref[0, :, cols] = (_dot(p_ref[rows, _tile_cols(0)], vp_ref[0, 0, :, cols])
                                 + _dot(p_ref[rows, _tile_cols(1)], vc_ref[0, 0, :, cols]))
            lse_ref[0, :, cols] = jnp.broadcast_to(lse[rows], (ATTN_BLK, HEAD_DIM))

    def spec(w, shift):
        return pl.BlockSpec((1, 1, ATTN_BLK, GROUP_W), lambda r, n: (w, r, jnp.maximum(n + shift, 0), 0))

    out = pl.BlockSpec((1, ATTN_BLK, GROUP_W), lambda r, n: (r, n, 0))
    bias = _band_bias(g, d, ("prev", "cur"))
    return pl.pallas_call(
        body, name=f"attn_fwd_{g}", grid=(d, nb),
        in_specs=[spec(0, 0), spec(1, 0), spec(1, -1), spec(2, 0), spec(2, -1),
                  pl.BlockSpec(bias.shape, lambda r, n: (0, 0))],
        out_specs=[out, out], out_shape=[jax.ShapeDtypeStruct((d, L, GROUP_W), F32)] * 2,
        scratch_shapes=[pltpu.VMEM((GROUP_ROWS, 2 * ATTN_BLK), F32), pltpu.VMEM((GROUP_ROWS, 2 * ATTN_BLK), BF16)],
        compiler_params=_params(("parallel", "parallel")))(qkv, qkv, qkv, qkv, qkv, bias)


def _attn_bwd(qkv, do, lse, cc, g, d, side=None):
    _, _, L, _ = qkv.shape
    nb = L // ATTN_BLK
    scale = HEAD_DIM ** -0.5
    a_, b_, c_ = _tile_cols(0), _tile_cols(1), _tile_cols(2)

    def body(q0_ref, q1_ref, k0_ref, kp_ref, v0_ref, vp_ref, do0_ref, do1_ref, l0_ref, l1_ref, c0_ref, c1_ref,
             bias_ref, o_ref, s_ref, dp_ref, l_ref, c_ref, p_ref, ds_ref):
        n = pl.program_id(1)
        for hh in range(HEADS_PER_GROUP):
            cols, rows = _tile_cols(hh), _tile_cols(hh)
            q0, q1 = q0_ref[0, 0, :, cols], q1_ref[0, 0, :, cols]
            k0, kp = k0_ref[0, 0, :, cols], kp_ref[0, 0, :, cols]
            v0, vp = v0_ref[0, 0, :, cols], vp_ref[0, 0, :, cols]
            do0, do1 = do0_ref[0, :, cols], do1_ref[0, :, cols]
            s_ref[rows, a_], s_ref[rows, b_], s_ref[rows, c_] = _dot_nt(q0, k0), _dot_nt(q0, kp), _dot_nt(q1, k0)
            dp_ref[rows, a_], dp_ref[rows, b_], dp_ref[rows, c_] = _dot_nt(do0, v0), _dot_nt(do0, vp), _dot_nt(do1, v0)
            l_ref[rows, a_], l_ref[rows, b_], l_ref[rows, c_] = l0_ref[0, :, cols], l0_ref[0, :, cols], l1_ref[0, :, cols]
            c_ref[rows, a_], c_ref[rows, b_], c_ref[rows, c_] = c0_ref[0, :, cols], c0_ref[0, :, cols], c1_ref[0, :, cols]
        col = lax.broadcasted_iota(jnp.int32, (GROUP_ROWS, 3 * ATTN_BLK), 1)
        tile = col // ATTN_BLK
        gone = jnp.logical_or(jnp.logical_and(tile == 1, n == 0), jnp.logical_and(tile == 2, n == nb - 1))
        s = jnp.where(gone, NEG_BIG, s_ref[...] * scale + bias_ref[...])
        p = jnp.exp(s - l_ref[...])
        p_ref[...] = p.astype(BF16)
        ds_ref[...] = (p * (dp_ref[...] + c_ref[...])).astype(BF16)
        for hh in range(HEADS_PER_GROUP):
            cols, rows = _tile_cols(hh), _tile_cols(hh)
            q0, q1 = q0_ref[0, 0, :, cols], q1_ref[0, 0, :, cols]
            k0, kp = k0_ref[0, 0, :, cols], kp_ref[0, 0, :, cols]
            do0, do1 = do0_ref[0, :, cols], do1_ref[0, :, cols]
            o_ref[0, 0, :, cols] = ((_dot(ds_ref[rows, a_], k0) + _dot(ds_ref[rows, b_], kp)) * scale).astype(BF16)
            o_ref[1, 0, :, cols] = ((_dot_tn(ds_ref[rows, a_], q0) + _dot_tn(ds_ref[rows, c_], q1)) * scale).astype(BF16)
            o_ref[2, 0, :, cols] = (_dot_tn(p_ref[rows, a_], do0) + _dot_tn(p_ref[rows, c_], do1)).astype(BF16)

    def spec(w, shift):
        return pl.BlockSpec((1, 1, ATTN_BLK, GROUP_W), lambda r, n: (w, r, jnp.clip(n + shift, 0, nb - 1), 0))

    def spec3(shift):
        return pl.BlockSpec((1, ATTN_BLK, GROUP_W), lambda r, n: (r, jnp.clip(n + shift, 0, nb - 1), 0))

    bias = _band_bias(g, d, ("cur", "prev", "prev"))
    wide = (GROUP_ROWS, 3 * ATTN_BLK)
    outs, carried = _call(
        body, name=f"attn_bwd_{g}", grid=(d, nb),
        in_specs=[spec(0, 0), spec(0, 1), spec(1, 0), spec(1, -1), spec(2, 0), spec(2, -1),
                  spec3(0), spec3(1), spec3(0), spec3(1), spec3(0), spec3(1), pl.BlockSpec(wide, lambda r, n: (0, 0))],
        out_specs=[pl.BlockSpec((3, 1, ATTN_BLK, GROUP_W), lambda r, n: (0, r, n, 0))],
        out_shape=[jax.ShapeDtypeStruct((3, d, L, GROUP_W), BF16)],
        scratch_shapes=[pltpu.VMEM(wide, F32)] * 4 + [pltpu.VMEM(wide, BF16)] * 2, semantics=("parallel", "parallel"),
        args=[qkv, qkv, qkv, qkv, qkv, qkv, do, do, lse, lse, cc, cc, bias], side=side)
    return outs[0] if side is None else (outs[0], carried)


def _load_natural(refs, nat_refs):
    for g, d in enumerate(ATTN_DILATIONS):
        if d == 1:
            nat_refs[g][...] = refs[g][0]
        else:
            per = ATTN_ROWS // d
            for r in range(d):
                nat_refs[g][pl.ds(r, per, stride=d), :] = refs[g][r]


def _mix_weights(lse_nat):
    l0, l1, l2 = lse_nat[0][...], lse_nat[1][...], lse_nat[2][...]
    m = jnp.maximum(jnp.maximum(l0, l1), l2)
    e0, e1, e2 = jnp.exp(l0 - m), jnp.exp(l1 - m), jnp.exp(l2 - m)
    inv = 1.0 / (e0 + e1 + e2)
    return e0 * inv, e1 * inv, e2 * inv


def _dilated_specs(s):
    return [pl.BlockSpec((d, ATTN_ROWS // d, HEAD_DIM), lambda i, h: (0, i, h)) for d in ATTN_DILATIONS]


NATURAL_SCRATCH = [pltpu.VMEM((ATTN_ROWS, HEAD_DIM), F32)] * (2 * len(ATTN_DILATIONS))


def _attn_merge(outs, lses):
    s = outs[0].shape[0] * outs[0].shape[1]

    def body(o0, o1, o2, l0, l1, l2, a_ref, *nat):
        onat, lnat = nat[:3], nat[3:]
        _load_natural((o0, o1, o2), onat)
        _load_natural((l0, l1, l2), lnat)
        w0, w1, w2 = _mix_weights(lnat)
        a_ref[...] = (w0 * onat[0][...] + w1 * onat[1][...] + w2 * onat[2][...]).astype(BF16)

    return pl.pallas_call(
        body, name="attn_merge", grid=(s // ATTN_ROWS, HEADS_PER_GROUP), in_specs=_dilated_specs(s) * 2,
        out_specs=pl.BlockSpec((ATTN_ROWS, HEAD_DIM), lambda i, h: (i, h)),
        out_shape=jax.ShapeDtypeStruct((s, GROUP_W), BF16), scratch_shapes=NATURAL_SCRATCH,
        compiler_params=_params(("parallel", "parallel")))(*outs, *lses)


def _attn_merge_bwd(outs, lses, dattn):
    s = dattn.shape[0]

    def body(o0, o1, o2, l0, l1, l2, da_ref, do0, do1, do2, c0, c1, c2, *nat):
        onat, lnat = nat[:3], nat[3:]
        _load_natural((o0, o1, o2), onat)
        _load_natural((l0, l1, l2), lnat)
        ws = _mix_weights(lnat)
        da = da_ref[...]
        attn = ws[0] * onat[0][...] + ws[1] * onat[1][...] + ws[2] * onat[2][...]
        tot = jnp.broadcast_to(jnp.sum(da * attn, axis=-1, keepdims=True), (ATTN_ROWS, HEAD_DIM))
        for g, (d, do_ref, c_ref) in enumerate(zip(ATTN_DILATIONS, (do0, do1, do2), (c0, c1, c2))):
            if d == 1:
                do_ref[0] = (ws[g] * da).astype(BF16)
                c_ref[0] = -ws[g] * tot
            else:
                onat[g][...] = ws[g] * da
                lnat[g][...] = -ws[g] * tot
                per = ATTN_ROWS // d
                for r in range(d):
                    do_ref[r] = onat[g][pl.ds(r, per, stride=d), :].astype(BF16)
                    c_ref[r] = lnat[g][pl.ds(r, per, stride=d), :]

    dil = _dilated_specs(s)
    shapes = [jax.ShapeDtypeStruct(o.shape, BF16) for o in outs] + [jax.ShapeDtypeStruct(o.shape, F32) for o in outs]
    return pl.pallas_call(
        body, name="attn_merge_bwd", grid=(s // ATTN_ROWS, HEADS_PER_GROUP),
        in_specs=dil * 2 + [pl.BlockSpec((ATTN_ROWS, HEAD_DIM), lambda i, h: (i, h))], out_specs=dil * 2,
        out_shape=shapes, scratch_shapes=NATURAL_SCRATCH,
        compiler_params=_params(("parallel", "parallel")))(*outs, *lses, dattn)


def _ssm_prepare(a_re, a_im, log_dt, b_re, b_im, c_re, c_im):
    n_g = a_re.shape[0]
    nj = n_g * SSM_GROUP // SSM_TILE_CH
    gpt = SSM_TILE_CH // SSM_GROUP
    dt = jnp.exp(log_dt)[:, None]
    mag = jnp.exp(a_re * dt)
    lr, li = mag * jnp.cos(a_im * dt), mag * jnp.sin(a_im * dt)
    den = a_re * a_re + a_im * a_im
    cr = ((lr - 1.0) * a_re + li * a_im) / den
    ci = (li * a_re - (lr - 1.0) * a_im) / den
    bb_re = cr[..., None] * b_re - ci[..., None] * b_im
    bb_im = cr[..., None] * b_im + ci[..., None] * b_re
    eye = jnp.eye(gpt, dtype=F32)

    def b_tiles(t):
        t = t.transpose(0, 2, 1).reshape(nj, gpt, SSM_GROUP, SSM_STATE)
        return jnp.einsum("jgcp,gh->jgchp", t, eye).reshape(nj, SSM_TILE_CH, SSM_TILE_ST)

    def c_tiles(t):
        t = t.reshape(nj, gpt, SSM_GROUP, SSM_STATE)
        return jnp.einsum("jgcp,gh->jhpgc", t, eye).reshape(nj, SSM_TILE_ST, SSM_TILE_CH)

    lam = jnp.stack([lr.reshape(-1), li.reshape(-1)])
    bmat = jnp.concatenate([b_tiles(bb_re), b_tiles(bb_im)], axis=2)
    cmat = jnp.concatenate([c_tiles(c_re), -c_tiles(c_im)], axis=1)
    return lam, bmat, cmat


SSM_SEGMENTS = 8


def _to_segment_order(nat, perm_ref):
    per = nat.shape[0] // SSM_SEGMENTS
    for i in range(SSM_SEGMENTS):
        perm_ref[pl.ds(i, per, stride=SSM_SEGMENTS), :] = nat[i * per:(i + 1) * per, :]
    return perm_ref[...]


def _to_time_order(val, perm_ref, store):
    per = val.shape[0] // SSM_SEGMENTS
    perm_ref[...] = val
    for i in range(SSM_SEGMENTS):
        store(i, perm_ref[pl.ds(i, per, stride=SSM_SEGMENTS), :])


def _fill_powers(lam_ref, w_ref, nj, tau_n):
    for j in range(nj):
        st = slice(j * SSM_TILE_ST, (j + 1) * SSM_TILE_ST)
        lr = jnp.broadcast_to(lam_ref[0:1, st], (SSM_SEGMENTS, SSM_TILE_ST))
        li = jnp.broadcast_to(lam_ref[1:2, st], (SSM_SEGMENTS, SSM_TILE_ST))
        wr, wi = lr, li
        for tau in range(tau_n):
            rows = slice(tau * SSM_SEGMENTS, (tau + 1) * SSM_SEGMENTS)
            w_ref[j, rows, :SSM_TILE_ST] = wr
            w_ref[j, rows, SSM_TILE_ST:] = wi
            wr, wi = wr * lr - wi * li, wr * li + wi * lr


def _segment_scan(src, xs_ref, w_tile, lr, li, cr, ci, conj, reverse):
    seg, half = SSM_SEGMENTS, SSM_TILE_ST
    tau_n = src.shape[0] // seg
    sgn = -1.0 if conj else 1.0
    lr8 = jnp.broadcast_to(lr, (seg, half))
    li8 = jnp.broadcast_to(li, (seg, half)) * sgn
    xr = jnp.zeros((seg, half), F32)
    xi = jnp.zeros((seg, half), F32)
    order = range(tau_n - 1, -1, -1) if reverse else range(tau_n)
    for tau in order:
        rows = slice(tau * seg, (tau + 1) * seg)
        xr, xi = lr8 * xr - li8 * xi + src[rows, :half], lr8 * xi + li8 * xr + src[rows, half:]
        xs_ref[rows, :half] = xr
        xs_ref[rows, half:] = xi
    pr = w_tile[(tau_n - 1) * seg:(tau_n - 1) * seg + 1, :half]
    pi = w_tile[(tau_n - 1) * seg:(tau_n - 1) * seg + 1, half:] * sgn
    fr, fi = cr, ci
    ins_r, ins_i = [None] * seg, [None] * seg
    runs = range(seg - 1, -1, -1) if reverse else range(seg)
    for i in runs:
        ins_r[i], ins_i[i] = fr, fi
        fr, fi = xr[i:i + 1, :] + pr * fr - pi * fi, xi[i:i + 1, :] + pr * fi + pi * fr
    in_r = jnp.concatenate(ins_r, axis=0)
    in_i = jnp.concatenate(ins_i, axis=0)
    for tau in range(tau_n):
        rows = slice(tau * seg, (tau + 1) * seg)
        wrow = (tau_n - 1 - tau) if reverse else tau
        wr = w_tile[wrow * seg:(wrow + 1) * seg, :half]
        wi = w_tile[wrow * seg:(wrow + 1) * seg, half:] * sgn
        xs_ref[rows, :half] += wr * in_r - wi * in_i
        xs_ref[rows, half:] += wr * in_i + wi * in_r
    return (fr, fi), (in_r, in_i)


def _ssm_dims(z, bmat, u_off):
    s = z.shape[0]
    nj = bmat.shape[0]
    t_rows = _pick(s, (256, 128))
    return s, nj, nj * SSM_TILE_CH, nj * SSM_TILE_ST, t_rows


def _ssm_fwd(z, bmat, cmat, lam, dskip, u_off, side=None):
    s, nj, w, ns, t_rows = _ssm_dims(z, bmat, u_off)
    per = t_rows // SSM_SEGMENTS

    def body(*refs):
        u_refs = refs[:nj]
        b_ref, c_ref, lam_ref, d_ref, y_ref, yg_ref, xin_ref, xall_ref, carry_ref, w_ref, xs_ref, perm_ref = refs[nj:]

        @pl.when(pl.program_id(0) == 0)
        def _():
            carry_ref[...] = jnp.zeros_like(carry_ref)
            _fill_powers(lam_ref, w_ref, nj, per)

        xin_ref[0] = carry_ref[...]
        for j in range(nj):
            st = slice(j * SSM_TILE_ST, (j + 1) * SSM_TILE_ST)
            ch = slice(j * SSM_TILE_CH, (j + 1) * SSM_TILE_CH)
            up = _to_segment_order(u_refs[j], perm_ref)
            bu = _dot(up.astype(BF16), b_ref[j])
            (fr, fi), _ = _segment_scan(bu, xs_ref, w_ref.at[j], lam_ref[0:1, st], lam_ref[1:2, st],
                                        carry_ref[0:1, st], carry_ref[1:2, st], conj=False, reverse=False)
            carry_ref[0:1, st] = fr
            carry_ref[1:2, st] = fi
            xs = xs_ref[...].astype(BF16)
            xall_ref[:, j * 2 * SSM_TILE_ST:(j + 1) * 2 * SSM_TILE_ST] = xs
            yp = _dot(xs, c_ref[j]) + d_ref[:, ch] * up

            def store(i, rows, ch=ch):
                y_ref[i * per:(i + 1) * per, ch] = rows
                yg_ref[i * per:(i + 1) * per, ch] = _gelu(rows).astype(BF16)

            _to_time_order(yp, perm_ref, store)

    u_specs = [pl.BlockSpec((t_rows, SSM_TILE_CH), lambda c, k=k: (c, u_off // SSM_TILE_CH + k)) for k in range(nj)]
    full3 = lambda shape: pl.BlockSpec(shape, lambda c: (0, 0, 0))
    full2 = lambda shape: pl.BlockSpec(shape, lambda c: (0, 0))
    rows = pl.BlockSpec((t_rows, w), lambda c: (c, 0))
    outs, carried = _call(
        body, name="ssm_fwd", grid=(s // t_rows,),
        in_specs=u_specs + [full3(bmat.shape), full3(cmat.shape), full2(lam.shape), full2(dskip.shape)],
        out_specs=[rows, rows, pl.BlockSpec((1, 2, ns), lambda c: (c, 0, 0)), pl.BlockSpec((t_rows, 2 * ns), lambda c: (c, 0))],
        out_shape=[jax.ShapeDtypeStruct((s, w), F32), jax.ShapeDtypeStruct((s, w), BF16),
                   jax.ShapeDtypeStruct((s // t_rows, 2, ns), F32), jax.ShapeDtypeStruct((s, 2 * ns), BF16)],
        scratch_shapes=[pltpu.VMEM((2, ns), F32), pltpu.VMEM((nj, t_rows, 2 * SSM_TILE_ST), F32),
                        pltpu.VMEM((t_rows, 2 * SSM_TILE_ST), F32), pltpu.VMEM((t_rows, SSM_TILE_CH), F32)],
        semantics=("arbitrary",), args=[*([z] * nj), bmat, cmat, lam, dskip], side=side)
    return outs if side is None else (outs, carried)


def _ssm_bwd(z, y, dyg, xin, xall, bmat, cmat, lam, dskip, u_off, side=None):
    s, nj, w, ns, t_rows = _ssm_dims(z, bmat, u_off)
    nc = s // t_rows
    per = t_rows // SSM_SEGMENTS
    seg, half = SSM_SEGMENTS, SSM_TILE_ST

    def body(*refs):
        u_refs = refs[:nj]
        (y_ref, dyg_ref, xin_ref, xall_ref, b_ref, c_ref, lam_ref, d_ref, du_ref, db_ref, dc_ref, dlam_ref, dd_ref,
         carry_ref, w_ref, gs_ref, perm_ref, acc_ref) = refs[nj:]

        @pl.when(pl.program_id(0) == 0)
        def _():
            carry_ref[...] = jnp.zeros_like(carry_ref)
            db_ref[...] = jnp.zeros_like(db_ref)
            dc_ref[...] = jnp.zeros_like(dc_ref)
            dd_ref[...] = jnp.zeros_like(dd_ref)
            acc_ref[...] = jnp.zeros_like(acc_ref)
            _fill_powers(lam_ref, w_ref, nj, per)

        for j in range(nj):
            st = slice(j * SSM_TILE_ST, (j + 1) * SSM_TILE_ST)
            ch = slice(j * SSM_TILE_CH, (j + 1) * SSM_TILE_CH)
            lr, li = lam_ref[0:1, st], lam_ref[1:2, st]
            up = _to_segment_order(u_refs[j], perm_ref)
            upb = up.astype(BF16)
            dyp = _to_segment_order(dyg_ref[:, ch] * _gelu_grad(y_ref[:, ch]), perm_ref)
            dyb = dyp.astype(BF16)
            xs = xall_ref[:, j * 2 * half:(j + 1) * 2 * half]
            xf = xs.astype(F32)
            ends = xf[t_rows - seg:t_rows - 1, :]
            in_r = jnp.concatenate([xin_ref[0, 0:1, st], ends[:, :half]], axis=0)
            in_i = jnp.concatenate([xin_ref[0, 1:2, st], ends[:, half:]], axis=0)
            (gr, gi), _ = _segment_scan(_dot_nt(dyb, c_ref[j]), gs_ref, w_ref.at[j], lr, li,
                                        carry_ref[0:1, st], carry_ref[1:2, st], conj=True, reverse=True)
            carry_ref[0:1, st] = gr
            carry_ref[1:2, st] = gi
            gs = gs_ref[...]
            xsr, xsi, gsr, gsi = xf[:, :half], xf[:, half:], gs[:, :half], gs[:, half:]
            pxr = jnp.concatenate([in_r, xsr[:t_rows - seg]], axis=0)
            pxi = jnp.concatenate([in_i, xsi[:t_rows - seg]], axis=0)
            dl_r = gsr * pxr + gsi * pxi
            dl_i = gsi * pxr - gsr * pxi
            acc_ref[0, :, st] += jnp.sum(dl_r.reshape(per, seg, half), axis=0)
            acc_ref[1, :, st] += jnp.sum(dl_i.reshape(per, seg, half), axis=0)
            gx = gs.astype(BF16)
            dup = _dot_nt(gx, b_ref[j]) + d_ref[:, ch] * dyp

            def store(i, rows, ch=ch):
                du_ref[i * per:(i + 1) * per, ch] = rows.astype(BF16)

            _to_time_order(dup, perm_ref, store)
            db_ref[j] += _dot_tn(upb, gx)
            dc_ref[j] += _dot_tn(xs, dyb)
            dd_ref[:, ch] += jnp.sum(dyp * up, axis=0, keepdims=True)

        @pl.when(pl.program_id(0) == nc - 1)
        def _():
            dlam_ref[...] = jnp.sum(acc_ref[...], axis=1)

    rev = lambda c: nc - 1 - c
    u_specs = [pl.BlockSpec((t_rows, SSM_TILE_CH), lambda c, k=k: (rev(c), u_off // SSM_TILE_CH + k))
               for k in range(nj)]
    full3 = lambda shape: pl.BlockSpec(shape, lambda c: (0, 0, 0))
    full2 = lambda shape: pl.BlockSpec(shape, lambda c: (0, 0))
    rows = pl.BlockSpec((t_rows, w), lambda c: (rev(c), 0))
    outs, carried = _call(
        body, name="ssm_bwd", grid=(nc,),
        in_specs=u_specs + [rows, rows, pl.BlockSpec((1, 2, ns), lambda c: (rev(c), 0, 0)),
                            pl.BlockSpec((t_rows, 2 * ns), lambda c: (rev(c), 0)),
                            full3(bmat.shape), full3(cmat.shape), full2(lam.shape), full2(dskip.shape)],
        out_specs=[rows, full3(bmat.shape), full3(cmat.shape), full2(lam.shape), full2(dskip.shape)],
        out_shape=[jax.ShapeDtypeStruct((s, w), BF16), jax.ShapeDtypeStruct(bmat.shape, F32),
                   jax.ShapeDtypeStruct(cmat.shape, F32), jax.ShapeDtypeStruct(lam.shape, F32),
                   jax.ShapeDtypeStruct(dskip.shape, F32)],
        scratch_shapes=[pltpu.VMEM((2, ns), F32), pltpu.VMEM((nj, t_rows, 2 * SSM_TILE_ST), F32),
                        pltpu.VMEM((t_rows, 2 * SSM_TILE_ST), F32),
                        pltpu.VMEM((t_rows, SSM_TILE_CH), F32), pltpu.VMEM((2, SSM_SEGMENTS, ns), F32)],
        semantics=("arbitrary",), args=[*([z] * nj), y, dyg, xin, xall, bmat, cmat, lam, dskip], side=side)
    return outs if side is None else (outs, carried)


def _adam_math(w, g, m, v):
    m = ADAM_B1 * m + (1.0 - ADAM_B1) * g
    v = ADAM_B2 * v + (1.0 - ADAM_B2) * (g * g)
    m_hat = m / (1.0 - ADAM_B1 ** ADAM_STEP)
    v_hat = v / (1.0 - ADAM_B2 ** ADAM_STEP)
    delta = -ADAM_LR * (m_hat / (jnp.sqrt(v_hat) + ADAM_EPS) + ADAM_WD * w)
    return delta, m, v


def _adam_rows(r, c):
    for tr in (512, 256, 128, 64, 32, 16, 8):
        if r % tr == 0 and tr * c * 4 <= (2 << 20):
            return tr
    return r


def _adamw_big(w, p_mine, p_sib, m, v, name, side=None):
    r, c = w.shape
    tr = _adam_rows(r, c)

    def body(w_ref, a_ref, b_ref, m_ref, v_ref, g_ref, d_ref, nm_ref, nv_ref):
        g = a_ref[...] + b_ref[...]
        g_ref[...] = g
        d_ref[...], nm_ref[...], nv_ref[...] = _adam_math(w_ref[...], g, m_ref[...], v_ref[...])

    blk = pl.BlockSpec((tr, c), lambda i: (i, 0))
    outs, carried = _call(body, name=f"adamw_{name}", grid=(r // tr,), in_specs=[blk] * 5, out_specs=[blk] * 4,
                          out_shape=[jax.ShapeDtypeStruct((r, c), F32)] * 4, semantics=("parallel",),
                          args=[w, p_mine, p_sib, m, v], side=side)
    return outs if side is None else (outs, carried)


def _adamw_small(w, parts, m, v):
    r, c = w.shape
    n_dev = parts.shape[0]

    def body(w_ref, p_ref, m_ref, v_ref, g_ref, d_ref, nm_ref, nv_ref):
        g = p_ref[0]
        for k in range(1, n_dev):
            g = g + p_ref[k]
        g_ref[...] = g
        d_ref[...], nm_ref[...], nv_ref[...] = _adam_math(w_ref[...], g, m_ref[...], v_ref[...])

    blk = pl.BlockSpec((r, c), lambda i: (0, 0))
    return pl.pallas_call(body, name="adamw_small", grid=(1,),
                          in_specs=[blk, pl.BlockSpec((n_dev, r, c), lambda i: (0, 0, 0)), blk, blk],
                          out_specs=[blk] * 4, out_shape=[jax.ShapeDtypeStruct((r, c), F32)] * 4,
                          compiler_params=_params(("arbitrary",)))(w, parts, m, v)


def _cast_bf16(w, name):
    r, c = w.shape
    tr = _adam_rows(r, c)

    def body(w_ref, o_ref):
        o_ref[...] = w_ref[...].astype(BF16)

    blk = pl.BlockSpec((tr, c), lambda i: (i, 0))
    return pl.pallas_call(body, name=f"cast_{name}", grid=(r // tr,), in_specs=[blk], out_specs=blk,
                          out_shape=jax.ShapeDtypeStruct((r, c), BF16), compiler_params=_params(("parallel",)))(w)


def _sum_slots(recv, name):
    _, r, c = recv.shape
    tr = _adam_rows(r, c)

    def body(p_ref, o_ref):
        acc = p_ref[0].astype(F32)
        for k in range(1, N_CHIPS):
            acc = acc + p_ref[k].astype(F32)
        o_ref[...] = acc

    return pl.pallas_call(body, name=f"sum_{name}", grid=(r // tr,),
                          in_specs=[pl.BlockSpec((N_CHIPS, tr, c), lambda i: (0, i, 0))],
                          out_specs=pl.BlockSpec((tr, c), lambda i: (i, 0)),
                          out_shape=jax.ShapeDtypeStruct((r, c), F32), compiler_params=_params(("parallel",)))(recv)


BIG_WEIGHTS = ("w_in", "w_attn_up", "w_glu_v", "w_glu_g", "w_out", "w_ffn_gate", "w_ffn_up", "w_ffn_down")
COL_SHARDED = ("w_in", "w_attn_up", "w_glu_v", "w_glu_g", "w_ffn_gate", "w_ffn_up")


def _aligned(v, m):
    return v if isinstance(v, int) else pl.multiple_of(v, m)


def _shard_of(ref, name, j, shard_shape, half=None):
    r, c = shard_shape
    rows = r if half is None else r // 2
    row0 = 0 if half is None else half * rows
    if name in COL_SHARDED:
        return ref.at[pl.ds(_aligned(row0, 16), rows), pl.ds(_aligned(j * c, 128), c)]
    return ref.at[pl.ds(_aligned(j * r + row0, 16), rows), :]


def _other_chips():
    x, y = lax.axis_index("x"), lax.axis_index("y")
    return [(1 - x, y), (x, 1 - y), (1 - x, 1 - y)]


def _dma_sems(n, arrays):
    return [pltpu.SemaphoreType.DMA((n, 3))] * arrays + [pltpu.SemaphoreType.DMA((n,))]


def _gather_side(shards):
    names = list(shards)
    n = len(names)
    full_shapes = []
    for k in names:
        r, c = shards[k].shape
        full_shapes.append((r, c * N_CHIPS) if k in COL_SHARDED else (r * N_CHIPS, c))

    def build(src, dst, sems):
        send_sems, recv_sems, pass_send_sems, pass_recv_sems, local_sems = sems
        x, y, c = lax.axis_index("x"), lax.axis_index("y"), lax.axis_index("c")
        me = 2 * x + y
        locals_, sends, arrivals, forwards, passed_on = [], [], [], [], []
        for i, k in enumerate(names):
            shape = shards[k].shape
            half_rows = shape[0] // 2
            locals_.append(pltpu.make_async_copy(src[i], _shard_of(dst[i], k, me, shape), local_sems.at[i]))
            my_half = src[i].at[pl.ds(_aligned(c * half_rows, 16), half_rows), :]
            for p, (px, py) in enumerate(_other_chips()):
                peer = 2 * px + py
                landed = _shard_of(dst[i], k, peer, shape, half=c)
                sends.append(pltpu.make_async_remote_copy(
                    src_ref=my_half, dst_ref=_shard_of(dst[i], k, me, shape, half=c), send_sem=send_sems.at[i, p],
                    recv_sem=recv_sems.at[i, p], device_id=(px, py, c), device_id_type=MESH))
                arrivals.append(pltpu.make_async_remote_copy(
                    src_ref=my_half, dst_ref=landed, send_sem=send_sems.at[i, p],
                    recv_sem=recv_sems.at[i, p], device_id=(px, py, c), device_id_type=MESH))
                forwards.append(pltpu.make_async_remote_copy(
                    src_ref=landed, dst_ref=landed, send_sem=pass_send_sems.at[i, p],
                    recv_sem=pass_recv_sems.at[i, p], device_id=(x, y, 1 - c), device_id_type=MESH))
                passed_on.append(pltpu.make_async_remote_copy(
                    src_ref=landed, dst_ref=_shard_of(dst[i], k, peer, shape, half=1 - c),
                    send_sem=pass_send_sems.at[i, p], recv_sem=pass_recv_sems.at[i, p],
                    device_id=(x, y, 1 - c), device_id_type=MESH))
        return locals_, sends, arrivals, forwards, passed_on

    return _Side([shards[k] for k in names], [jax.ShapeDtypeStruct(s, BF16) for s in full_shapes], _dma_sems(n, 4), build,
                 relays=True)


def _scatter_side(grads, shard_shapes):
    names = list(grads)
    n = len(names)

    def build(src, dst, sems):
        send_sems, recv_sems, local_sems = sems
        x, y, c = lax.axis_index("x"), lax.axis_index("y"), lax.axis_index("c")
        me = 2 * x + y
        locals_, sends, arrivals = [], [], []
        for i, k in enumerate(names):
            shape = shard_shapes[k]
            locals_.append(pltpu.make_async_copy(_shard_of(src[i], k, me, shape), dst[i].at[me], local_sems.at[i]))
            for p, (px, py) in enumerate(_other_chips()):
                peer = 2 * px + py
                sends.append(pltpu.make_async_remote_copy(
                    src_ref=_shard_of(src[i], k, peer, shape), dst_ref=dst[i].at[me], send_sem=send_sems.at[i, p],
                    recv_sem=recv_sems.at[i, p], device_id=(px, py, c), device_id_type=MESH))
                arrivals.append(pltpu.make_async_remote_copy(
                    src_ref=_shard_of(src[i], k, peer, shape), dst_ref=dst[i].at[peer], send_sem=send_sems.at[i, p],
                    recv_sem=recv_sems.at[i, p], device_id=(px, py, c), device_id_type=MESH))
        return locals_, sends, arrivals, [None] * len(arrivals), []

    return _Side([grads[k] for k in names],
                 [jax.ShapeDtypeStruct((N_CHIPS,) + tuple(shard_shapes[k]), BF16) for k in names], _dma_sems(n, 2), build)


def _put_cols(dz, src, col_off):
    s, w = src.shape
    tr = _pick(s, (2048, 1024, 512, 256, 128, 64, 8))
    tc = _pick(math.gcd(w, col_off), (1024, 512, 256, 128))
    off = col_off // tc

    def body(src_ref, dz_ref, o_ref):
        del dz_ref
        o_ref[...] = src_ref[...].astype(o_ref.dtype)

    return pl.pallas_call(
        body, name="put_cols", grid=(s // tr, w // tc),
        in_specs=[pl.BlockSpec((tr, tc), lambda i, j: (i, j)), pl.BlockSpec(memory_space=pl.ANY)],
        out_specs=pl.BlockSpec((tr, tc), lambda i, j: (i, off + j)),
        out_shape=jax.ShapeDtypeStruct(dz.shape, dz.dtype), input_output_aliases={1: 0},
        compiler_params=_params(("parallel", "parallel")))(src, dz)


def _swap_side(parts):
    n = len(parts)

    def build(src, dst, sems):
        send_sems, recv_sems = sems
        sibling = (lax.axis_index("x"), lax.axis_index("y"), 1 - lax.axis_index("c"))
        copies = [pltpu.make_async_remote_copy(src_ref=src[i], dst_ref=dst[i], send_sem=send_sems.at[i],
                                               recv_sem=recv_sems.at[i], device_id=sibling, device_id_type=MESH)
                  for i in range(n)]
        return [], copies, copies, [None] * n, []

    return _Side(parts, [jax.ShapeDtypeStruct(p.shape, F32) for p in parts],
                 [pltpu.SemaphoreType.DMA((n,)), pltpu.SemaphoreType.DMA((n,))], build)


def _share_side(packed):
    r, c = packed.shape

    def build(src, dst, sems):
        send_sems, recv_sems, local_sem = sems
        x, y, cc = lax.axis_index("x"), lax.axis_index("y"), lax.axis_index("c")
        me = 4 * x + 2 * y + cc
        own = pltpu.make_async_copy(src[0], dst[0].at[me], local_sem)
        sends, arrivals = [], []
        flips = [(fx, fy, fc) for fx in range(2) for fy in range(2) for fc in range(2) if fx or fy or fc]
        for p, (fx, fy, fc) in enumerate(flips):
            px, py, pc = x ^ fx, y ^ fy, cc ^ fc
            sends.append(pltpu.make_async_remote_copy(
                src_ref=src[0], dst_ref=dst[0].at[me], send_sem=send_sems.at[p], recv_sem=recv_sems.at[p],
                device_id=(px, py, pc), device_id_type=MESH))
            arrivals.append(pltpu.make_async_remote_copy(
                src_ref=src[0], dst_ref=dst[0].at[4 * px + 2 * py + pc], send_sem=send_sems.at[p],
                recv_sem=recv_sems.at[p], device_id=(px, py, pc), device_id_type=MESH))
        return [own], sends, arrivals, [None] * len(arrivals), []

    return _Side([packed], [jax.ShapeDtypeStruct((8, r, c), F32)],
                 [pltpu.SemaphoreType.DMA((7,)), pltpu.SemaphoreType.DMA((7,)), pltpu.SemaphoreType.DMA], build)


SMALL_WEIGHTS = ("norm_mix_pre", "ssm_a_re", "ssm_a_im", "ssm_log_dt", "ssm_b_re", "ssm_b_im", "ssm_c_re", "ssm_c_im",
                 "ssm_d", "norm_mix_post", "norm_ffn_pre", "norm_ffn_post")
WEIGHT_ORDER = ("norm_mix_pre", "w_in", "w_attn_up", "ssm_a_re", "ssm_a_im", "ssm_log_dt", "ssm_b_re", "ssm_b_im",
                "ssm_c_re", "ssm_c_im", "ssm_d", "w_glu_v", "w_glu_g", "w_out", "norm_mix_post", "norm_ffn_pre",
                "w_ffn_gate", "w_ffn_up", "w_ffn_down", "norm_ffn_post")
PACK_LANES = 128
PACK_ROWS = 8
PACK_GROUPS = (SMALL_WEIGHTS[:1], SMALL_WEIGHTS[1:])


def _pack_group(arrs, names):
    flat = jnp.concatenate([arrs[k].reshape(-1) for k in names])
    pad = -flat.shape[0] % (PACK_LANES * PACK_ROWS)
    return jnp.pad(flat, (0, pad)).reshape(-1, PACK_LANES)


def _pack_small(arrs):
    return jnp.concatenate([_pack_group(arrs, names) for names in PACK_GROUPS], axis=0)


def _unpack_small(packed, like):
    out, row = {}, 0
    for names in PACK_GROUPS:
        rows = _pack_group(like, names).shape[0]
        flat, pos = packed[row:row + rows].reshape(-1), 0
        for k in names:
            n = like[k].size
            out[k] = flat[pos:pos + n].reshape(like[k].shape)
            pos += n
        row += rows
    return out


def _local_step(x, target, big, small, shards=None, shard_shapes=None, h1=None):
    s, d = x.shape
    big, grads, slots = dict(big), {}, {}
    carry = shards is not None

    def gathering(names, call):
        if not carry:
            return call(None)
        res, got = call(_gather_side({k: shards[k] for k in names}))
        big.update(zip(names, got))
        return res

    def scattering(names, call):
        if not carry:
            return call(None)
        res, got = call(_scatter_side({k: grads[k] for k in names}, shard_shapes))
        slots.update(zip(names, got))
        return res

    u_off = 3 * HQ
    gate_off = u_off + d // 2
    g1, g2, g3, g4 = (small[k][0:1] for k in ("norm_mix_pre", "norm_mix_post", "norm_ffn_pre", "norm_ffn_post"))
    ssm_names = ("ssm_a_re", "ssm_a_im", "ssm_log_dt", "ssm_b_re", "ssm_b_im", "ssm_c_re", "ssm_c_im")
    (lam, bmat, cmat), ssm_vjp = jax.vjp(_ssm_prepare, *[small[k][0] for k in ssm_names])
    bmat, cmat = bmat.astype(BF16), cmat.astype(BF16)
    dskip = small["ssm_d"][0:1]

    if h1 is None:
        h1 = _norm_in(x, g1)
    z = gathering(("w_attn_up", "w_glu_v", "w_glu_g", "w_out", "w_ffn_gate"),
                  lambda side: _mm(h1, big["w_in"], "nn", F32, "in_proj", side=side))
    y, yg, xin, xall = gathering(("w_ffn_up",), lambda side: _ssm_fwd(z, bmat, cmat, lam, dskip, u_off, side=side))
    qkv = [_dilate_qkv(z, g, dil) for g, dil in enumerate(ATTN_DILATIONS)]
    outs, lses = zip(*[_attn_fwd(qkv[g], g, dil) for g, dil in enumerate(ATTN_DILATIONS)])
    attn = _attn_merge(outs, lses)
    merged, ab, gv, gg = _mm_fused(
        [attn, yg], [big["w_attn_up"], big["w_glu_v"], big["w_glu_g"]], [(0, 0), (1, 1), (1, 2)], "nn",
        [BF16, BF16, BF16, BF16], "branches_merge", extras=[(z, gate_off), (z, gate_off + d)], epilogue=_gates_epilogue)
    mo = _mm(merged, big["w_out"], "nn", F32, "mix_out")
    x2, h2 = _norm_mid(x, mo, g2, g3)
    act, fg, fu = gathering(("w_ffn_down",), lambda side: _mm_fused(
        [h2], [big["w_ffn_gate"], big["w_ffn_up"]], [(0, 0), (0, 1)], "nn", [BF16, BF16, BF16], "ffn_up_act",
        epilogue=_swiglu_epilogue, side=side))
    f = _mm(act, big["w_ffn_down"], "nn", F32, "ffn_down")
    loss, dout, df, dg4 = _loss_head(x2, f, g4, target)

    grads["w_ffn_down"] = _mm_kloop(act, df, "tn", BF16, "dw_ffn_down")
    dfg, dfu = scattering(("w_ffn_down",), lambda side: _mm_fused(
        [df], [big["w_ffn_down"]], [(0, 0)], "nt", [BF16, BF16], "d_ffn_act", extras=[(fg, 0), (fu, 0)],
        epilogue=_swiglu_bwd_epilogue, side=side))
    grads["w_ffn_gate"] = _mm_kloop(h2, dfg, "tn", BF16, "dw_ffn_gate")
    dh2 = scattering(("w_ffn_gate",), lambda side: _mm_fused(
        [dfg, dfu], [big["w_ffn_gate"], big["w_ffn_up"]], [(0, 0), (1, 1)], "nt", [F32], "d_h2",
        epilogue=_sum_epilogue, side=side))[0]
    grads["w_ffn_up"] = _mm_kloop(h2, dfu, "tn", BF16, "dw_ffn_up")
    dx2, dmo, dg2, dg3 = _norm_mid_bwd(x2, mo, g2, g3, dout, dh2)
    dz, dgs, dab, dgv, dgg = _mm_fused(
        [dmo], [big["w_out"]], [(0, 0)], "nt", [BF16] * 5, "d_merged_gates",
        extras=[(z, gate_off), (z, gate_off + d), (ab, 0), (gv, 0), (gg, 0)], epilogue=_gates_bwd_epilogue,
        out_place=[(z.shape[1], gate_off), None, None, None, None])
    dz = _put_cols(dz, dgs, gate_off + d)
    grads["w_out"] = _mm_kloop(merged, dmo, "tn", BF16, "dw_out")
    dyg = _mm_fused([dgv, dgg], [big["w_glu_v"], big["w_glu_g"]], [(0, 0), (1, 1)], "nt", [F32], "d_yg",
                    epilogue=_sum_epilogue)[0]
    grads["w_glu_v"] = _mm_kloop(yg, dgv, "tn", BF16, "dw_glu_v")
    grads["w_glu_g"] = _mm_kloop(yg, dgg, "tn", BF16, "dw_glu_g")
    du, dbmat, dcmat, dlam, dd = scattering(
        ("w_ffn_up",),
        lambda side: _ssm_bwd(z, y, dyg, xin, xall, bmat, cmat, lam, dskip, u_off, side=side))
    dz = _put_cols(dz, du, u_off)
    dattn = _mm(dab, big["w_attn_up"], "nt", F32, "d_attn")
    grads["w_attn_up"] = _mm_kloop(attn, dab, "tn", BF16, "dw_attn_up")
    merged_bwd = _attn_merge_bwd(outs, lses, dattn)
    mine, theirs = {}, {}
    for g, dil in enumerate(ATTN_DILATIONS):
        side = None
        if carry and g == 0:
            mine = {k: _sum_slots(slots[k], k) for k in slots}
            side = _swap_side(list(mine.values()))
        dqkv = _attn_bwd(qkv[g], merged_bwd[g], lses[g], merged_bwd[3 + g], g, dil, side=side)
        if side is not None:
            dqkv, got = dqkv
            theirs = dict(zip(mine, got))
        dz = _undilate_dqkv(dqkv, dz, g, dil)
    small_grads = dict(zip(ssm_names, (t[None] for t in ssm_vjp((dlam, dbmat, dcmat)))))
    small_grads.update(norm_mix_post=dg2, norm_ffn_pre=dg3, norm_ffn_post=dg4, ssm_d=dd)
    if carry:
        late = ("w_attn_up", "w_out", "w_glu_v", "w_glu_g")
        side = _join_sides(_scatter_side({k: grads[k] for k in late}, shard_shapes),
                           _share_side(_pack_group(small_grads, PACK_GROUPS[1])))
        grads["w_in"], got = _mm_kloop(h1, dz, "tn", BF16, "dw_in", side=side)
        slots.update(zip(late, got[:-1]))
        shared = got[-1]
    else:
        grads["w_in"] = _mm_kloop(h1, dz, "tn", BF16, "dw_in")
    dh1 = scattering(("w_in",), lambda side: _mm_kloop(dz, big["w_in"], "nt", F32, "d_h1", side=side))
    grad_x, dg1 = _norm_in_bwd(x, g1, dh1, dx2)
    small_grads["norm_mix_pre"] = dg1
    if carry:
        return loss[0, 0], grad_x, (slots, mine, theirs), (dg1, shared)
    return loss[0, 0], grad_x, grads, small_grads


def kernel(x, norm_mix_pre, w_in, w_attn_up, ssm_a_re, ssm_a_im, ssm_log_dt, ssm_b_re, ssm_b_im, ssm_c_re, ssm_c_im, ssm_d, w_glu_v, w_glu_g, w_out, norm_mix_post, norm_ffn_pre, w_ffn_gate, w_ffn_up, w_ffn_down, norm_ffn_post, loss_target, m_norm_mix_pre, m_w_in, m_w_attn_up, m_ssm_a_re, m_ssm_a_im, m_ssm_log_dt, m_ssm_b_re, m_ssm_b_im, m_ssm_c_re, m_ssm_c_im, m_ssm_d, m_w_glu_v, m_w_glu_g, m_w_out, m_norm_mix_post, m_norm_ffn_pre, m_w_ffn_gate, m_w_ffn_up, m_w_ffn_down, m_norm_ffn_post, v_norm_mix_pre, v_w_in, v_w_attn_up, v_ssm_a_re, v_ssm_a_im, v_ssm_log_dt, v_ssm_b_re, v_ssm_b_im, v_ssm_c_re, v_ssm_c_im, v_ssm_d, v_w_glu_v, v_w_glu_g, v_w_out, v_norm_mix_post, v_norm_ffn_pre, v_w_ffn_gate, v_w_ffn_up, v_w_ffn_down, v_norm_ffn_post):
    given = dict(locals())
    w = {k: given[k] for k in WEIGHT_ORDER}
    m = {k: given["m_" + k] for k in WEIGHT_ORDER}
    v = {k: given["v_" + k] for k in WEIGHT_ORDER}

    shard_shapes = {k: w[k].shape[1:] for k in BIG_WEIGHTS}
    shards = {"w_in": _cast_bf16(w["w_in"][0], "w_in")}
    h1, casts, got = _prologue(x[0], norm_mix_pre[0:1], {k: w[k][0] for k in BIG_WEIGHTS if k != "w_in"},
                               side=_gather_side({"w_in": shards["w_in"]}))
    shards.update(casts)
    big = {"w_in": got[0]}

    loss, grad_x, (slots, mine, theirs), small_grads = _local_step(
        x[0], loss_target[0], big, {k: w[k] for k in SMALL_WEIGHTS}, shards, shard_shapes, h1)
    loss = lax.psum(loss, MESH_AXES)

    early = [k for k in BIG_WEIGHTS if k in mine]
    last = [k for k in BIG_WEIGHTS if k not in mine]
    mine.update({k: _sum_slots(slots[k], k) for k in last})
    dg1, shared = small_grads
    last_exchanges = _join_sides(_swap_side([mine[k] for k in last]),
                                 _share_side(_pack_group({"norm_mix_pre": dg1}, PACK_GROUPS[0])))
    out_g, out_d, out_m, out_v = {}, {}, {}, {}
    for k in early + last:
        if k == early[0]:
            res, got = _adamw_big(w[k][0], mine[k], theirs[k], m[k][0], v[k][0], k, side=last_exchanges)
            theirs.update(zip(last, got[:-1]))
            late = got[-1]
        else:
            res = _adamw_big(w[k][0], mine[k], theirs[k], m[k][0], v[k][0], k)
        out_g[k], out_d[k], out_m[k], out_v[k] = (t[None] for t in res)

    pick = lambda tree: {k: tree[k] for k in SMALL_WEIGHTS}
    parts = jnp.concatenate([late, shared], axis=1)
    res = _adamw_small(_pack_small(pick(w)), parts, _pack_small(pick(m)), _pack_small(pick(v)))
    for dst, packed in zip((out_g, out_d, out_m, out_v), res):
        dst.update(_unpack_small(packed, pick(w)))

    return (loss, grad_x[None], *[out_g[k] for k in WEIGHT_ORDER], *[out_d[k] for k in WEIGHT_ORDER],
            *[out_m[k] for k in WEIGHT_ORDER], *[out_v[k] for k in WEIGHT_ORDER])
```

```python
import functools
import math

import jax
import jax.numpy as jnp
from jax import lax
from jax.experimental import pallas as pl
from jax.experimental.pallas import tpu as pltpu

F32 = jnp.float32
BF16 = jnp.bfloat16

EPS = 1e-6
HEAD_DIM = 128
HEADS_PER_GROUP = 4
ATTN_DILATIONS = (1, 4, 16)
ATTN_BLK = 128
N_ATTN_HEADS = HEADS_PER_GROUP * len(ATTN_DILATIONS)
GROUP_W = HEADS_PER_GROUP * HEAD_DIM
HQ = N_ATTN_HEADS * HEAD_DIM
SSM_GROUP = 16
SSM_STATE = 64
SSM_TILE_CH = 128
SSM_TILE_ST = SSM_TILE_CH // SSM_GROUP * SSM_STATE
ADAM_LR = 0.001
ADAM_B1 = 0.9
ADAM_B2 = 0.999
ADAM_EPS = 1e-08
ADAM_WD = 0.01
ADAM_STEP = 10
NEG_BIG = -1e30
V7X_VMEM_LIMIT = 56 * 1024 * 1024
MESH_AXES = ("x", "y", "c")
N_CHIPS = 4


def _pick(n, cands):
    for c in cands:
        if n % c == 0:
            return c
    raise ValueError(f"no tile of {cands} divides {n}")


def _params(sem):
    return pltpu.CompilerParams(dimension_semantics=sem, vmem_limit_bytes=V7X_VMEM_LIMIT)


HBM = pl.BlockSpec(memory_space=pl.ANY)
MESH = pl.DeviceIdType.MESH


class _Side:
    def __init__(self, srcs, out_shapes, sem_shapes, build, aliases=None, relays=False):
        self.srcs, self.out_shapes, self.sem_shapes, self.build = list(srcs), list(out_shapes), list(sem_shapes), build
        self.aliases = dict(aliases or {})
        self.relays = relays

    def start(self, src, dst, sems):
        local, sends = self.build(src, dst, sems)[:2]
        for cp in local + sends:
            cp.start()

    def relay(self, src, dst, sems):
        _, _, arrivals, forwards, _ = self.build(src, dst, sems)
        for cp, forward in zip(arrivals, forwards):
            if forward is not None:
                cp.wait_recv()
                forward.start()

    def wait(self, src, dst, sems, relayed=False):
        local, sends, arrivals, forwards, passed_on = self.build(src, dst, sems)
        for cp, forward in zip(arrivals, forwards):
            if forward is None:
                cp.wait_recv()
            elif not relayed:
                cp.wait_recv()
                forward.start()
        for cp in passed_on:
            cp.wait_recv()
        for cp in sends + [f for f in forwards if f is not None]:
            cp.wait_send()
        for cp in local:
            cp.wait()


def _join_sides(a, b):
    ns, no, nm = len(a.srcs), len(a.out_shapes), len(a.sem_shapes)

    def build(src, dst, sems):
        ra, rb = a.build(src[:ns], dst[:no], sems[:nm]), b.build(src[ns:], dst[no:], sems[nm:])
        return tuple(p + q for p, q in zip(ra, rb))

    aliases = {**a.aliases, **{ns + k: no + v for k, v in b.aliases.items()}}
    return _Side(a.srcs + b.srcs, a.out_shapes + b.out_shapes, a.sem_shapes + b.sem_shapes, build, aliases,
                 a.relays or b.relays)


def _call(body, *, name, grid, in_specs, out_specs, out_shape, semantics, args, scratch_shapes=(), side=None, **kw):
    in_specs, out_specs, out_shape, scratch_shapes = list(in_specs), list(out_specs), list(out_shape), list(scratch_shapes)
    if side is None:
        res = pl.pallas_call(body, name=name, grid=grid, in_specs=in_specs, out_specs=out_specs, out_shape=out_shape,
                             scratch_shapes=scratch_shapes, compiler_params=_params(semantics), **kw)(*args)
        return list(res), []
    n_in, n_out, n_scr = len(in_specs), len(out_specs), len(scratch_shapes)
    ns_in, ns_out = len(side.srcs), len(side.out_shapes)
    n_steps = math.prod(grid)
    relay_at = (3 * n_steps) // 4 if side.relays and n_steps >= 4 else None

    def carrying(*refs):
        ins, s_in = refs[:n_in], refs[n_in:n_in + ns_in]
        o0 = n_in + ns_in
        outs, s_out = refs[o0:o0 + n_out], refs[o0 + n_out:o0 + n_out + ns_out]
        c0 = o0 + n_out + ns_out
        scr, sems = refs[c0:c0 + n_scr], refs[c0 + n_scr:]
        step = functools.reduce(lambda acc, ig: acc * ig[1] + pl.program_id(ig[0]), enumerate(grid), 0)

        @pl.when(step == 0)
        def _():
            side.start(s_in, s_out, sems)

        if relay_at is not None:
            @pl.when(step == relay_at)
            def _():
                side.relay(s_in, s_out, sems)

        body(*ins, *outs, *scr)

        @pl.when(step == n_steps - 1)
        def _():
            side.wait(s_in, s_out, sems, relayed=relay_at is not None)

    res = pl.pallas_call(
        carrying, name=name, grid=grid, in_specs=in_specs + [HBM] * ns_in, out_specs=out_specs + [HBM] * ns_out,
        out_shape=out_shape + side.out_shapes, scratch_shapes=scratch_shapes + side.sem_shapes,
        input_output_aliases={n_in + k: n_out + v for k, v in side.aliases.items()},
        compiler_params=pltpu.CompilerParams(dimension_semantics=("arbitrary",) * len(grid),
                                             vmem_limit_bytes=V7X_VMEM_LIMIT, has_side_effects=True), **kw,
    )(*args, *side.srcs)
    return list(res[:n_out]), list(res[n_out:])


_DOT_DIMS = {"nn": (((1,), (0,)), ((), ())), "nt": (((1,), (1,)), ((), ())), "tn": (((0,), (0,)), ((), ()))}


MM_VMEM_BUDGET = 44 * 1024 * 1024
MM_STEP_BYTES = 1 << 20
MM_ACC_BYTES = 4
MM_EPILOGUE_COLS = 256


def _size(dtype):
    return jnp.dtype(dtype).itemsize


def _mm_fused(as_, bs, pairs, mode, out_dtypes, name, extras=(), epilogue=None, side=None, out_place=None):
    M = as_[0].shape[0]
    N = bs[0].shape[1] if mode == "nn" else bs[0].shape[0]
    ks_a = [a.shape[1] for a in as_]
    ks_b = [b.shape[0] if mode == "nn" else b.shape[1] for b in bs]
    chunked = epilogue is not None
    if epilogue is None:
        epilogue = lambda rs, es: rs
    offs = [off for _, off in extras]
    place = list(out_place) if out_place else [None] * len(out_dtypes)
    offs_all = offs + [p[1] for p in place if p is not None]
    best = None
    for tm in (2048, 1024, 512, 256, 128):
        for tn in (2048, 1024, 512, 256, 128):
            if M % tm or N % tn or any(off % tn for off in offs_all):
                continue
            vmem = (sum(2 * tm * k * 2 for k in ks_a) + sum(2 * k * tn * 2 for k in ks_b)
                    + sum(2 * tm * tn * _size(d) for d in out_dtypes) + sum(2 * tm * tn * _size(e.dtype) for e, _ in extras)
                    + len(pairs) * tm * tn * 4)
            cost = sum(k * N * 2 for k in ks_b) * (M // tm) + (M // tm) * (N // tn) * MM_STEP_BYTES
            if vmem <= MM_VMEM_BUDGET and (best is None or cost < best[0]):
                best = (cost, tm, tn)
    _, tm, tn = best
    na, nb, ne, no = len(as_), len(bs), len(extras), len(out_dtypes)
    dims = _DOT_DIMS[mode]

    sub = MM_EPILOGUE_COLS if chunked and tn % MM_EPILOGUE_COLS == 0 else tn

    def body(*refs):
        a_refs, b_refs = refs[:na], refs[na:na + nb]
        e_refs, o_refs = refs[na + nb:na + nb + ne], refs[na + nb + ne:]
        for c0 in range(0, tn, sub):
            cs = slice(c0, c0 + sub)
            rs = [lax.dot_general(a_refs[ai][...], b_refs[bi][:, cs] if mode == "nn" else b_refs[bi][cs, :], dims,
                                  preferred_element_type=F32) for ai, bi in pairs]
            outs = epilogue(rs, [e[:, cs] for e in e_refs])
            for o_ref, o in zip(o_refs, outs):
                o_ref[:, cs] = o.astype(o_ref.dtype)

    a_specs = [pl.BlockSpec((tm, k), lambda i, j: (i, 0)) for k in ks_a]
    if mode == "nn":
        b_specs = [pl.BlockSpec((k, tn), lambda i, j: (0, j)) for k in ks_b]
    else:
        b_specs = [pl.BlockSpec((tn, k), lambda i, j: (j, 0)) for k in ks_b]
    e_specs = [pl.BlockSpec((tm, tn), lambda i, j, o=off // tn: (i, o + j)) for off in offs]
    o_specs = [pl.BlockSpec((tm, tn), lambda i, j, o=(p[1] // tn if p else 0): (i, o + j)) for p in place]
    outs, carried = _call(
        body, name=name, grid=(M // tm, N // tn), in_specs=a_specs + b_specs + e_specs, out_specs=o_specs,
        out_shape=[jax.ShapeDtypeStruct((M, p[0] if p else N), d) for d, p in zip(out_dtypes, place)],
        semantics=("parallel", "arbitrary"),
        args=[*as_, *bs, *[e for e, _ in extras]], side=side)
    return outs if side is None else (outs, carried)


def _mm(a, b, mode, out_dtype, name, side=None):
    res = _mm_fused([a], [b], [(0, 0)], mode, [out_dtype], name, side=side)
    return res[0] if side is None else (res[0][0], res[1])


def _mm_kloop(a, b, mode, out_dtype, name, second=None, side=None):
    if mode == "nn":
        (M, K), (_, N) = a.shape, b.shape
    elif mode == "nt":
        (M, K), (N, _) = a.shape, b.shape
    else:
        (K, M), (_, N) = a.shape, b.shape
    products = 1 if second is None else 2
    best = None
    for tm in (2816, 2048, 1408, 1024, 512, 256, 128):
        for tn in (2816, 2432, 2048, 1408, 1024, 512, 256, 128):
            for tk in (2816, 2432, 2048, 1408, 1024, 512, 256, 128):
                if M % tm or N % tn or K % tk:
                    continue
                vmem = 2 * tm * tn * 4 + 2 * tm * tn * _size(out_dtype) + products * 2 * tk * (tm + tn) * 2
                steps = (M // tm) * (N // tn) * (K // tk)
                cost = (K * M * 2 * (N // tn) + K * N * 2 * (M // tm) + steps * MM_STEP_BYTES
                        + steps * tm * tn * MM_ACC_BYTES)
                if vmem <= MM_VMEM_BUDGET and (best is None or cost < best[0]):
                    best = (cost, tm, tn, tk)
    _, tm, tn, tk = best
    nk = K // tk
    dims = _DOT_DIMS[mode]

    def body(*refs):
        o_ref, acc_ref = refs[-2:]
        k = pl.program_id(2)

        @pl.when(k == 0)
        def _():
            acc_ref[...] = jnp.zeros_like(acc_ref)

        for p in range(products):
            @pl.when(jnp.logical_and(k >= p * nk, k < (p + 1) * nk))
            def _(p=p):
                acc_ref[...] += lax.dot_general(refs[2 * p][...], refs[2 * p + 1][...], dims, preferred_element_type=F32)

        @pl.when(k == products * nk - 1)
        def _():
            o_ref[...] = acc_ref[...].astype(o_ref.dtype)

    def a_spec(p):
        kk = lambda k: jnp.clip(k - p * nk, 0, nk - 1)
        if mode == "tn":
            return pl.BlockSpec((tk, tm), lambda i, j, k: (kk(k), i))
        return pl.BlockSpec((tm, tk), lambda i, j, k: (i, kk(k)))

    def b_spec(p):
        kk = lambda k: jnp.clip(k - p * nk, 0, nk - 1)
        if mode == "nt":
            return pl.BlockSpec((tn, tk), lambda i, j, k: (j, kk(k)))
        return pl.BlockSpec((tk, tn), lambda i, j, k: (kk(k), j))

    o_spec = pl.BlockSpec((tm, tn), lambda i, j, k: (i, j))
    operands = (a, b) + (tuple(second) if second is not None else ())
    outs, carried = _call(
        body, name=name, grid=(M // tm, N // tn, products * nk),
        in_specs=[spec(p) for p in range(products) for spec in (a_spec, b_spec)], out_specs=[o_spec],
        out_shape=[jax.ShapeDtypeStruct((M, N), out_dtype)], scratch_shapes=[pltpu.VMEM((tm, tn), F32)],
        semantics=("parallel", "parallel", "arbitrary"), args=operands, side=side)
    return outs[0] if side is None else (outs[0], carried)


def _sigmoid(v):
    return 0.5 * jnp.tanh(0.5 * v) + 0.5


_GELU_C = math.sqrt(2.0 / math.pi)


def _gelu(v):
    return 0.5 * v * (1.0 + jnp.tanh(_GELU_C * (v + 0.044715 * v * v * v)))


def _gelu_grad(v):
    t = jnp.tanh(_GELU_C * (v + 0.044715 * v * v * v))
    return 0.5 * (1.0 + t) + 0.5 * v * (1.0 - t * t) * _GELU_C * (1.0 + 3.0 * 0.044715 * v * v)


def _rms(v, gain):
    r = lax.rsqrt(jnp.mean(v * v, axis=-1, keepdims=True) + EPS)
    return v * r * gain


def _rms_bwd(v, gain, dy):
    r = lax.rsqrt(jnp.mean(v * v, axis=-1, keepdims=True) + EPS)
    a = dy * gain
    dv = r * a - v * (r * r * r) * jnp.mean(a * v, axis=-1, keepdims=True)
    return dv, dy * v * r


def _row_tile(s, few_operands=False):
    return _pick(s, ((512,) if few_operands else ()) + (256, 128, 64, 8))


def _norm_in(x, gain):
    s, d = x.shape
    tr = _row_tile(s)

    def body(x_ref, g_ref, h_ref):
        h_ref[...] = _rms(x_ref[...], g_ref[...]).astype(BF16)

    row = pl.BlockSpec((tr, d), lambda i: (i, 0))
    vec = pl.BlockSpec((1, d), lambda i: (0, 0))
    return pl.pallas_call(body, name="norm_in", grid=(s // tr,), in_specs=[row, vec], out_specs=row,
                          out_shape=jax.ShapeDtypeStruct((s, d), BF16), compiler_params=_params(("parallel",)))(x, gain)


def _prologue(x, gain, weights, side=None):
    s, d = x.shape
    tr = _row_tile(s)
    steps = s // tr
    names = list(weights)
    tiles = []
    for k in names:
        r, _ = weights[k].shape
        tiles.append(next(t for t in range(16, r + 1, 16) if r % t == 0 and r // t <= steps))

    def body(*refs):
        x_ref, g_ref = refs[:2]
        w_refs, h_ref, o_refs = refs[2:2 + len(names)], refs[2 + len(names)], refs[3 + len(names):]
        h_ref[...] = _rms(x_ref[...], g_ref[...]).astype(BF16)
        for w_ref, o_ref in zip(w_refs, o_refs):
            o_ref[...] = w_ref[...].astype(BF16)

    row = pl.BlockSpec((tr, d), lambda i: (i, 0))
    w_specs = [pl.BlockSpec((t, weights[k].shape[1]), lambda i, last=weights[k].shape[0] // t - 1: (jnp.minimum(i, last), 0))
               for k, t in zip(names, tiles)]
    outs, carried = _call(
        body, name="prologue", grid=(steps,), in_specs=[row, pl.BlockSpec((1, d), lambda i: (0, 0))] + w_specs,
        out_specs=[row] + w_specs,
        out_shape=[jax.ShapeDtypeStruct((s, d), BF16)] + [jax.ShapeDtypeStruct(weights[k].shape, BF16) for k in names],
        semantics=("arbitrary",), args=[x, gain] + [weights[k] for k in names], side=side)
    return outs[0], dict(zip(names, outs[1:])), carried


def _norm_mid(x, mo, g_post, g_pre):
    s, d = x.shape
    tr = _row_tile(s, few_operands=True)

    def body(x_ref, mo_ref, g2_ref, g3_ref, x2_ref, h2_ref):
        x2 = x_ref[...] + _rms(mo_ref[...], g2_ref[...])
        x2_ref[...] = x2
        h2_ref[...] = _rms(x2, g3_ref[...]).astype(BF16)

    row = pl.BlockSpec((tr, d), lambda i: (i, 0))
    vec = pl.BlockSpec((1, d), lambda i: (0, 0))
    return pl.pallas_call(
        body, name="norm_mid", grid=(s // tr,), in_specs=[row, row, vec, vec], out_specs=[row, row],
        out_shape=[jax.ShapeDtypeStruct((s, d), F32), jax.ShapeDtypeStruct((s, d), BF16)],
        compiler_params=_params(("parallel",)))(x, mo, g_post, g_pre)


def _loss_head(x2, f, g_post, target):
    s, d = x2.shape
    tr = _row_tile(s, few_operands=True)

    def body(x2_ref, f_ref, g_ref, t_ref, loss_ref, dout_ref, df_ref, dg_ref):
        @pl.when(pl.program_id(0) == 0)
        def _():
            loss_ref[...] = jnp.zeros_like(loss_ref)
            dg_ref[...] = jnp.zeros_like(dg_ref)

        fv = f_ref[...]
        g = g_ref[...]
        err = x2_ref[...] + _rms(fv, g) - t_ref[...]
        loss_ref[...] += 0.5 * jnp.sum(jnp.mean(err * err, axis=-1, keepdims=True), axis=0, keepdims=True)
        dout = err * (1.0 / d)
        dout_ref[...] = dout
        df, dg = _rms_bwd(fv, g, dout)
        df_ref[...] = df.astype(BF16)
        dg_ref[...] += jnp.sum(dg, axis=0, keepdims=True)

    row = pl.BlockSpec((tr, d), lambda i: (i, 0))
    vec = pl.BlockSpec((1, d), lambda i: (0, 0))
    one = pl.BlockSpec((1, 1), lambda i: (0, 0))
    return pl.pallas_call(
        body, name="loss_head", grid=(s // tr,), in_specs=[row, row, vec, row], out_specs=[one, row, row, vec],
        out_shape=[jax.ShapeDtypeStruct((1, 1), F32), jax.ShapeDtypeStruct((s, d), F32),
                   jax.ShapeDtypeStruct((s, d), BF16), jax.ShapeDtypeStruct((1, d), F32)],
        compiler_params=_params(("arbitrary",)))(x2, f, g_post, target)


def _norm_mid_bwd(x2, mo, g_post, g_pre, dout, dh2):
    s, d = x2.shape
    tr = _row_tile(s)

    def body(x2_ref, mo_ref, g2_ref, g3_ref, dout_ref, dh2_ref, dx2_ref, dmo_ref, dg2_ref, dg3_ref):
        @pl.when(pl.program_id(0) == 0)
        def _():
            dg2_ref[...] = jnp.zeros_like(dg2_ref)
            dg3_ref[...] = jnp.zeros_like(dg3_ref)

        dv, dg3 = _rms_bwd(x2_ref[...], g3_ref[...], dh2_ref[...])
        dx2 = dout_ref[...] + dv
        dx2_ref[...] = dx2
        dmo, dg2 = _rms_bwd(mo_ref[...], g2_ref[...], dx2)
        dmo_ref[...] = dmo.astype(BF16)
        dg2_ref[...] += jnp.sum(dg2, axis=0, keepdims=True)
        dg3_ref[...] += jnp.sum(dg3, axis=0, keepdims=True)

    row = pl.BlockSpec((tr, d), lambda i: (i, 0))
    vec = pl.BlockSpec((1, d), lambda i: (0, 0))
    return pl.pallas_call(
        body, name="norm_mid_bwd", grid=(s // tr,), in_specs=[row, row, vec, vec, row, row],
        out_specs=[row, row, vec, vec],
        out_shape=[jax.ShapeDtypeStruct((s, d), F32), jax.ShapeDtypeStruct((s, d), BF16),
                   jax.ShapeDtypeStruct((1, d), F32), jax.ShapeDtypeStruct((1, d), F32)],
        compiler_params=_params(("arbitrary",)))(x2, mo, g_post, g_pre, dout, dh2)


def _norm_in_bwd(x, gain, dh, dx2):
    s, d = x.shape
    tr = _row_tile(s, few_operands=True)

    def body(x_ref, g_ref, dh_ref, dx2_ref, dx_ref, dg_ref):
        @pl.when(pl.program_id(0) == 0)
        def _():
            dg_ref[...] = jnp.zeros_like(dg_ref)

        dv, dg = _rms_bwd(x_ref[...], g_ref[...], dh_ref[...])
        dx_ref[...] = dx2_ref[...] + dv
        dg_ref[...] += jnp.sum(dg, axis=0, keepdims=True)

    row = pl.BlockSpec((tr, d), lambda i: (i, 0))
    vec = pl.BlockSpec((1, d), lambda i: (0, 0))
    return pl.pallas_call(
        body, name="norm_in_bwd", grid=(s // tr,), in_specs=[row, vec, row, row], out_specs=[row, vec],
        out_shape=[jax.ShapeDtypeStruct((s, d), F32), jax.ShapeDtypeStruct((1, d), F32)],
        compiler_params=_params(("arbitrary",)))(x, gain, dh, dx2)


def _swiglu_epilogue(rs, es):
    g, u = rs
    return [g * _sigmoid(g) * u, g, u]


def _swiglu_bwd_epilogue(rs, es):
    d = rs[0]
    g, u = es[0].astype(F32), es[1].astype(F32)
    sg = _sigmoid(g)
    return [d * u * sg * (1.0 + g * (1.0 - sg)), d * g * sg]


def _sum_epilogue(rs, es):
    return [rs[0] + rs[1]]


def _gates_epilogue(rs, es):
    ab, gv, gg = rs
    ga, gs = es
    return [_sigmoid(ga) * ab + _sigmoid(gs) * gv * _sigmoid(gg), ab, gv, gg]


def _gates_bwd_epilogue(rs, es):
    dm = rs[0]
    ga, gs, ab, gv, gg = (e.astype(F32) for e in es)
    sa, ss, sg = _sigmoid(ga), _sigmoid(gs), _sigmoid(gg)
    dsb = dm * ss
    return [dm * ab * sa * (1.0 - sa), dm * gv * sg * ss * (1.0 - ss), dm * sa, dsb * sg, dsb * gv * sg * (1.0 - sg)]


ATTN_ROWS = 2048
RELAYOUT_STRIDE = 4


def _dilate_qkv(z, g, d):
    s = z.shape[0]
    tm = _pick(s, (2 * ATTN_ROWS, ATTN_ROWS))
    per = tm // d
    nh = HEADS_PER_GROUP

    quarter = tm // RELAYOUT_STRIDE

    def body(z_ref, o_ref, mid_ref):
        if d == RELAYOUT_STRIDE ** 2:
            for a in range(RELAYOUT_STRIDE):
                mid_ref[a * quarter:(a + 1) * quarter, :] = z_ref[pl.ds(a, quarter, stride=RELAYOUT_STRIDE), :]
            for r in range(d):
                a, b = r % RELAYOUT_STRIDE, r // RELAYOUT_STRIDE
                o_ref[0, r] = mid_ref[pl.ds(a * quarter + b, per, stride=RELAYOUT_STRIDE), :].astype(BF16)
        else:
            for r in range(d):
                rows = z_ref[...] if d == 1 else z_ref[pl.ds(r, per, stride=d), :]
                o_ref[0, r] = rows.astype(BF16)

    return pl.pallas_call(
        body, name=f"dilate_qkv_{g}", grid=(s // tm, 3, nh),
        in_specs=[pl.BlockSpec((tm, HEAD_DIM), lambda i, w, h: (i, (3 * w + g) * nh + h))],
        out_specs=pl.BlockSpec((1, d, per, HEAD_DIM), lambda i, w, h: (w, 0, i, h)),
        out_shape=jax.ShapeDtypeStruct((3, d, s // d, GROUP_W), BF16),
        scratch_shapes=[pltpu.VMEM((tm, HEAD_DIM), F32)],
        compiler_params=_params(("parallel", "parallel", "parallel")))(z)


def _undilate_dqkv(dqkv, dz, g, d):
    s = dz.shape[0]
    tm = _pick(s, (2 * ATTN_ROWS, ATTN_ROWS))
    per = tm // d
    nh = HEADS_PER_GROUP

    quarter = tm // RELAYOUT_STRIDE

    def body(i_ref, dz_ref, o_ref, nat_ref, mid_ref):
        del dz_ref
        if d == 1:
            o_ref[...] = i_ref[0, 0]
        elif d == RELAYOUT_STRIDE ** 2:
            for r in range(d):
                a, b = r % RELAYOUT_STRIDE, r // RELAYOUT_STRIDE
                mid_ref[pl.ds(a * quarter + b, per, stride=RELAYOUT_STRIDE), :] = i_ref[0, r].astype(F32)
            for a in range(RELAYOUT_STRIDE):
                nat_ref[pl.ds(a, quarter, stride=RELAYOUT_STRIDE), :] = mid_ref[a * quarter:(a + 1) * quarter, :]
            o_ref[...] = nat_ref[...].astype(BF16)
        else:
            for r in range(d):
                nat_ref[pl.ds(r, per, stride=d), :] = i_ref[0, r].astype(F32)
            o_ref[...] = nat_ref[...].astype(BF16)

    return pl.pallas_call(
        body, name=f"undilate_dqkv_{g}", grid=(s // tm, 3, nh),
        in_specs=[pl.BlockSpec((1, d, per, HEAD_DIM), lambda i, w, h: (w, 0, i, h)),
                  pl.BlockSpec(memory_space=pl.ANY)],
        out_specs=pl.BlockSpec((tm, HEAD_DIM), lambda i, w, h: (i, (3 * w + g) * nh + h)),
        out_shape=jax.ShapeDtypeStruct(dz.shape, dz.dtype), input_output_aliases={1: 0},
        scratch_shapes=[pltpu.VMEM((tm, HEAD_DIM), F32)] * 2,
        compiler_params=_params(("parallel", "parallel", "parallel")))(dqkv, dz)


def _alibi_slope(head):
    return 2.0 ** (-8.0 * (head + 1) / N_ATTN_HEADS)


def _dot_nt(a, b):
    return lax.dot_general(a, b, _DOT_DIMS["nt"], preferred_element_type=F32)


def _dot_tn(a, b):
    return lax.dot_general(a, b, _DOT_DIMS["tn"], preferred_element_type=F32)


def _dot(a, b):
    return jnp.dot(a, b, preferred_element_type=F32)


GROUP_ROWS = HEADS_PER_GROUP * ATTN_BLK


def _band_bias(g, d, pairs):
    qi = jnp.arange(ATTN_BLK)[:, None]
    ki = jnp.arange(ATTN_BLK)[None, :]
    rows = []
    for hh in range(HEADS_PER_GROUP):
        slope_d = _alibi_slope(g * HEADS_PER_GROUP + hh) * d
        tiles = []
        for kind in pairs:
            dist = qi - ki if kind == "cur" else ATTN_BLK + qi - ki
            ok = dist >= 0 if kind == "cur" else dist <= ATTN_BLK
            tiles.append(jnp.where(ok, -slope_d * dist.astype(F32), NEG_BIG))
        rows.append(jnp.concatenate(tiles, axis=1))
    return jnp.concatenate(rows, axis=0).astype(F32)


def _tile_cols(t):
    return slice(t * ATTN_BLK, (t + 1) * ATTN_BLK)


def _attn_fwd(qkv, g, d):
    _, _, L, _ = qkv.shape
    nb = L // ATTN_BLK
    scale = HEAD_DIM ** -0.5

    def body(q_ref, kc_ref, kp_ref, vc_ref, vp_ref, bias_ref, o_ref, lse_ref, s_ref, p_ref):
        n = pl.program_id(1)
        for hh in range(HEADS_PER_GROUP):
            cols, rows = _tile_cols(hh), _tile_cols(hh)
            q = q_ref[0, 0, :, cols]
            s_ref[rows, _tile_cols(0)] = _dot_nt(q, kp_ref[0, 0, :, cols])
            s_ref[rows, _tile_cols(1)] = _dot_nt(q, kc_ref[0, 0, :, cols])
        col = lax.broadcasted_iota(jnp.int32, (GROUP_ROWS, 2 * ATTN_BLK), 1)
        s = s_ref[...] * scale + bias_ref[...]
        s = jnp.where(jnp.logical_and(col < ATTN_BLK, n == 0), NEG_BIG, s)
        m = jnp.max(s, axis=-1, keepdims=True)
        e = jnp.exp(s - m)
        l = jnp.sum(e, axis=-1, keepdims=True)
        p_ref[...] = (e * (1.0 / l)).astype(BF16)
        lse = m + jnp.log(l)
        for hh in range(HEADS_PER_GROUP):
            cols, rows = _tile_cols(hh), _tile_cols(hh)
            o_ref[0, :, cols] = (_dot(p_ref[rows, _tile_cols(0)], vp_ref[0, 0, :, cols])
                                 + _dot(p_ref[rows, _tile_cols(1)], vc_ref[0, 0, :, cols]))
            lse_ref[0, :, cols] = jnp.broadcast_to(lse[rows], (ATTN_BLK, HEAD_DIM))

    def spec(w, shift):
        return pl.BlockSpec((1, 1, ATTN_BLK, GROUP_W), lambda r, n: (w, r, jnp.maximum(n + shift, 0), 0))

    out = pl.BlockSpec((1, ATTN_BLK, GROUP_W), lambda r, n: (r, n, 0))
    bias = _band_bias(g, d, ("prev", "cur"))
    return pl.pallas_call(
        body, name=f"attn_fwd_{g}", grid=(d, nb),
        in_specs=[spec(0, 0), spec(1, 0), spec(1, -1), spec(2, 0), spec(2, -1),
                  pl.BlockSpec(bias.shape, lambda r, n: (0, 0))],
        out_specs=[out, out], out_shape=[jax.ShapeDtypeStruct((d, L, GROUP_W), F32)] * 2,
        scratch_shapes=[pltpu.VMEM((GROUP_ROWS, 2 * ATTN_BLK), F32), pltpu.VMEM((GROUP_ROWS, 2 * ATTN_BLK), BF16)],
        compiler_params=_params(("parallel", "parallel")))(qkv, qkv, qkv, qkv, qkv, bias)


def _attn_bwd(qkv, do, lse, cc, g, d, side=None):
    _, _, L, _ = qkv.shape
    nb = L // ATTN_BLK
    scale = HEAD_DIM ** -0.5
    a_, b_, c_ = _tile_cols(0), _tile_cols(1), _tile_cols(2)

    def body(q0_ref, q1_ref, k0_ref, kp_ref, v0_ref, vp_ref, do0_ref, do1_ref, l0_ref, l1_ref, c0_ref, c1_ref,
             bias_ref, o_ref, s_ref, dp_ref, l_ref, c_ref, p_ref, ds_ref):
        n = pl.program_id(1)
        for hh in range(HEADS_PER_GROUP):
            cols, rows = _tile_cols(hh), _tile_cols(hh)
            q0, q1 = q0_ref[0, 0, :, cols], q1_ref[0, 0, :, cols]
            k0, kp = k0_ref[0, 0, :, cols], kp_ref[0, 0, :, cols]
            v0, vp = v0_ref[0, 0, :, cols], vp_ref[0, 0, :, cols]
            do0, do1 = do0_ref[0, :, cols], do1_ref[0, :, cols]
            s_ref[rows, a_], s_ref[rows, b_], s_ref[rows, c_] = _dot_nt(q0, k0), _dot_nt(q0, kp), _dot_nt(q1, k0)
            dp_ref[rows, a_], dp_ref[rows, b_], dp_ref[rows, c_] = _dot_nt(do0, v0), _dot_nt(do0, vp), _dot_nt(do1, v0)
            l_ref[rows, a_], l_ref[rows, b_], l_ref[rows, c_] = l0_ref[0, :, cols], l0_ref[0, :, cols], l1_ref[0, :, cols]
            c_ref[rows, a_], c_ref[rows, b_], c_ref[rows, c_] = c0_ref[0, :, cols], c0_ref[0, :, cols], c1_ref[0, :, cols]
        col = lax.broadcasted_iota(jnp.int32, (GROUP_ROWS, 3 * ATTN_BLK), 1)
        tile = col // ATTN_BLK
        gone = jnp.logical_or(jnp.logical_and(tile == 1, n == 0), jnp.logical_and(tile == 2, n == nb - 1))
        s = jnp.where(gone, NEG_BIG, s_ref[...] * scale + bias_ref[...])
        p = jnp.exp(s - l_ref[...])
        p_ref[...] = p.astype(BF16)
        ds_ref[...] = (p * (dp_ref[...] + c_ref[...])).astype(BF16)
        for hh in range(HEADS_PER_GROUP):
            cols, rows = _tile_cols(hh), _tile_cols(hh)
            q0, q1 = q0_ref[0, 0, :, cols], q1_ref[0, 0, :, cols]
            k0, kp = k0_ref[0, 0, :, cols], kp_ref[0, 0, :, cols]
            do0, do1 = do0_ref[0, :, cols], do1_ref[0, :, cols]
            o_ref[0, 0, :, cols] = ((_dot(ds_ref[rows, a_], k0) + _dot(ds_ref[rows, b_], kp)) * scale).astype(BF16)
            o_ref[1, 0, :, cols] = ((_dot_tn(ds_ref[rows, a_], q0) + _dot_tn(ds_ref[rows, c_], q1)) * scale).astype(BF16)
            o_ref[2, 0, :, cols] = (_dot_tn(p_ref[rows, a_], do0) + _dot_tn(p_ref[rows, c_], do1)).astype(BF16)

    def spec(w, shift):
        return pl.BlockSpec((1, 1, ATTN_BLK, GROUP_W), lambda r, n: (w, r, jnp.clip(n + shift, 0, nb - 1), 0))

    def spec3(shift):
        return pl.BlockSpec((1, ATTN_BLK, GROUP_W), lambda r, n: (r, jnp.clip(n + shift, 0, nb - 1), 0))

    bias = _band_bias(g, d, ("cur", "prev", "prev"))
    wide = (GROUP_ROWS, 3 * ATTN_BLK)
    outs, carried = _call(
        body, name=f"attn_bwd_{g}", grid=(d, nb),
        in_specs=[spec(0, 0), spec(0, 1), spec(1, 0), spec(1, -1), spec(2, 0), spec(2, -1),
                  spec3(0), spec3(1), spec3(0), spec3(1), spec3(0), spec3(1), pl.BlockSpec(wide, lambda r, n: (0, 0))],
        out_specs=[pl.BlockSpec((3, 1, ATTN_BLK, GROUP_W), lambda r, n: (0, r, n, 0))],
        out_shape=[jax.ShapeDtypeStruct((3, d, L, GROUP_W), BF16)],
        scratch_shapes=[pltpu.VMEM(wide, F32)] * 4 + [pltpu.VMEM(wide, BF16)] * 2, semantics=("parallel", "parallel"),
        args=[qkv, qkv, qkv, qkv, qkv, qkv, do, do, lse, lse, cc, cc, bias], side=side)
    return outs[0] if side is None else (outs[0], carried)


def _load_natural(refs, nat_refs):
    for g, d in enumerate(ATTN_DILATIONS):
        if d == 1:
            nat_refs[g][...] = refs[g][0]
        else:
            per = ATTN_ROWS // d
            for r in range(d):
                nat_refs[g][pl.ds(r, per, stride=d), :] = refs[g][r]


def _mix_weights(lse_nat):
    l0, l1, l2 = lse_nat[0][...], lse_nat[1][...], lse_nat[2][...]
    m = jnp.maximum(jnp.maximum(l0, l1), l2)
    e0, e1, e2 = jnp.exp(l0 - m), jnp.exp(l1 - m), jnp.exp(l2 - m)
    inv = 1.0 / (e0 + e1 + e2)
    return e0 * inv, e1 * inv, e2 * inv


def _dilated_specs(s):
    return [pl.BlockSpec((d, ATTN_ROWS // d, HEAD_DIM), lambda i, h: (0, i, h)) for d in ATTN_DILATIONS]


NATURAL_SCRATCH = [pltpu.VMEM((ATTN_ROWS, HEAD_DIM), F32)] * (2 * len(ATTN_DILATIONS))


def _attn_merge(outs, lses):
    s = outs[0].shape[0] * outs[0].shape[1]

    def body(o0, o1, o2, l0, l1, l2, a_ref, *nat):
        onat, lnat = nat[:3], nat[3:]
        _load_natural((o0, o1, o2), onat)
        _load_natural((l0, l1, l2), lnat)
        w0, w1, w2 = _mix_weights(lnat)
        a_ref[...] = (w0 * onat[0][...] + w1 * onat[1][...] + w2 * onat[2][...]).astype(BF16)

    return pl.pallas_call(
        body, name="attn_merge", grid=(s // ATTN_ROWS, HEADS_PER_GROUP), in_specs=_dilated_specs(s) * 2,
        out_specs=pl.BlockSpec((ATTN_ROWS, HEAD_DIM), lambda i, h: (i, h)),
        out_shape=jax.ShapeDtypeStruct((s, GROUP_W), BF16), scratch_shapes=NATURAL_SCRATCH,
        compiler_params=_params(("parallel", "parallel")))(*outs, *lses)


def _attn_merge_bwd(outs, lses, dattn):
    s = dattn.shape[0]

    def body(o0, o1, o2, l0, l1, l2, da_ref, do0, do1, do2, c0, c1, c2, *nat):
        onat, lnat = nat[:3], nat[3:]
        _load_natural((o0, o1, o2), onat)
        _load_natural((l0, l1, l2), lnat)
        ws = _mix_weights(lnat)
        da = da_ref[...]
        attn = ws[0] * onat[0][...] + ws[1] * onat[1][...] + ws[2] * onat[2][...]
        tot = jnp.broadcast_to(jnp.sum(da * attn, axis=-1, keepdims=True), (ATTN_ROWS, HEAD_DIM))
        for g, (d, do_ref, c_ref) in enumerate(zip(ATTN_DILATIONS, (do0, do1, do2), (c0, c1, c2))):
            if d == 1:
                do_ref[0] = (ws[g] * da).astype(BF16)
                c_ref[0] = -ws[g] * tot
            else:
                onat[g][...] = ws[g] * da
                lnat[g][...] = -ws[g] * tot
                per = ATTN_ROWS // d
                for r in range(d):
                    do_ref[r] = onat[g][pl.ds(r, per, stride=d), :].astype(BF16)
                    c_ref[r] = lnat[g][pl.ds(r, per, stride=d), :]

    dil = _dilated_specs(s)
    shapes = [jax.ShapeDtypeStruct(o.shape, BF16) for o in outs] + [jax.ShapeDtypeStruct(o.shape, F32) for o in outs]
    return pl.pallas_call(
        body, name="attn_merge_bwd", grid=(s // ATTN_ROWS, HEADS_PER_GROUP),
        in_specs=dil * 2 + [pl.BlockSpec((ATTN_ROWS, HEAD_DIM), lambda i, h: (i, h))], out_specs=dil * 2,
        out_shape=shapes, scratch_shapes=NATURAL_SCRATCH,
        compiler_params=_params(("parallel", "parallel")))(*outs, *lses, dattn)


def _ssm_prepare(a_re, a_im, log_dt, b_re, b_im, c_re, c_im):
    n_g = a_re.shape[0]
    nj = n_g * SSM_GROUP // SSM_TILE_CH
    gpt = SSM_TILE_CH // SSM_GROUP
    dt = jnp.exp(log_dt)[:, None]
    mag = jnp.exp(a_re * dt)
    lr, li = mag * jnp.cos(a_im * dt), mag * jnp.sin(a_im * dt)
    den = a_re * a_re + a_im * a_im
    cr = ((lr - 1.0) * a_re + li * a_im) / den
    ci = (li * a_re - (lr - 1.0) * a_im) / den
    bb_re = cr[..., None] * b_re - ci[..., None] * b_im
    bb_im = cr[..., None] * b_im + ci[..., None] * b_re
    eye = jnp.eye(gpt, dtype=F32)

    def b_tiles(t):
        t = t.transpose(0, 2, 1).reshape(nj, gpt, SSM_GROUP, SSM_STATE)
        return jnp.einsum("jgcp,gh->jgchp", t, eye).reshape(nj, SSM_TILE_CH, SSM_TILE_ST)

    def c_tiles(t):
        t = t.reshape(nj, gpt, SSM_GROUP, SSM_STATE)
        return jnp.einsum("jgcp,gh->jhpgc", t, eye).reshape(nj, SSM_TILE_ST, SSM_TILE_CH)

    lam = jnp.stack([lr.reshape(-1), li.reshape(-1)])
    bmat = jnp.concatenate([b_tiles(bb_re), b_tiles(bb_im)], axis=2)
    cmat = jnp.concatenate([c_tiles(c_re), -c_tiles(c_im)], axis=1)
    return lam, bmat, cmat


SSM_SEGMENTS = 8


def _to_segment_order(nat, perm_ref):
    per = nat.shape[0] // SSM_SEGMENTS
    for i in range(SSM_SEGMENTS):
        perm_ref[pl.ds(i, per, stride=SSM_SEGMENTS), :] = nat[i * per:(i + 1) * per, :]
    return perm_ref[...]


def _to_time_order(val, perm_ref, store):
    per = val.shape[0] // SSM_SEGMENTS
    perm_ref[...] = val
    for i in range(SSM_SEGMENTS):
        store(i, perm_ref[pl.ds(i, per, stride=SSM_SEGMENTS), :])


def _fill_powers(lam_ref, w_ref, nj, tau_n):
    for j in range(nj):
        st = slice(j * SSM_TILE_ST, (j + 1) * SSM_TILE_ST)
        lr = jnp.broadcast_to(lam_ref[0:1, st], (SSM_SEGMENTS, SSM_TILE_ST))
        li = jnp.broadcast_to(lam_ref[1:2, st], (SSM_SEGMENTS, SSM_TILE_ST))
        wr, wi = lr, li
        for tau in range(tau_n):
            rows = slice(tau * SSM_SEGMENTS, (tau + 1) * SSM_SEGMENTS)
            w_ref[j, rows, :SSM_TILE_ST] = wr
            w_ref[j, rows, SSM_TILE_ST:] = wi
            wr, wi = wr * lr - wi * li, wr * li + wi * lr


def _segment_scan(src, xs_ref, w_tile, lr, li, cr, ci, conj, reverse):
    seg, half = SSM_SEGMENTS, SSM_TILE_ST
    tau_n = src.shape[0] // seg
    sgn = -1.0 if conj else 1.0
    lr8 = jnp.broadcast_to(lr, (seg, half))
    li8 = jnp.broadcast_to(li, (seg, half)) * sgn
    xr = jnp.zeros((seg, half), F32)
    xi = jnp.zeros((seg, half), F32)
    order = range(tau_n - 1, -1, -1) if reverse else range(tau_n)
    for tau in order:
        rows = slice(tau * seg, (tau + 1) * seg)
        xr, xi = lr8 * xr - li8 * xi + src[rows, :half], lr8 * xi + li8 * xr + src[rows, half:]
        xs_ref[rows, :half] = xr
        xs_ref[rows, half:] = xi
    pr = w_tile[(tau_n - 1) * seg:(tau_n - 1) * seg + 1, :half]
    pi = w_tile[(tau_n - 1) * seg:(tau_n - 1) * seg + 1, half:] * sgn
    fr, fi = cr, ci
    ins_r, ins_i = [None] * seg, [None] * seg
    runs = range(seg - 1, -1, -1) if reverse else range(seg)
    for i in runs:
        ins_r[i], ins_i[i] = fr, fi
        fr, fi = xr[i:i + 1, :] + pr * fr - pi * fi, xi[i:i + 1, :] + pr * fi + pi * fr
    in_r = jnp.concatenate(ins_r, axis=0)
    in_i = jnp.concatenate(ins_i, axis=0)
    for tau in range(tau_n):
        rows = slice(tau * seg, (tau + 1) * seg)
        wrow = (tau_n - 1 - tau) if reverse else tau
        wr = w_tile[wrow * seg:(wrow + 1) * seg, :half]
        wi = w_tile[wrow * seg:(wrow + 1) * seg, half:] * sgn
        xs_ref[rows, :half] += wr * in_r - wi * in_i
        xs_ref[rows, half:] += wr * in_i + wi * in_r
    return (fr, fi), (in_r, in_i)


def _ssm_dims(z, bmat, u_off):
    s = z.shape[0]
    nj = bmat.shape[0]
    t_rows = _pick(s, (256, 128))
    return s, nj, nj * SSM_TILE_CH, nj * SSM_TILE_ST, t_rows


def _ssm_fwd(z, bmat, cmat, lam, dskip, u_off, side=None):
    s, nj, w, ns, t_rows = _ssm_dims(z, bmat, u_off)
    per = t_rows // SSM_SEGMENTS

    def body(*refs):
        u_refs = refs[:nj]
        b_ref, c_ref, lam_ref, d_ref, y_ref, yg_ref, xin_ref, xall_ref, carry_ref, w_ref, xs_ref, perm_ref = refs[nj:]

        @pl.when(pl.program_id(0) == 0)
        def _():
            carry_ref[...] = jnp.zeros_like(carry_ref)
            _fill_powers(lam_ref, w_ref, nj, per)

        xin_ref[0] = carry_ref[...]
        for j in range(nj):
            st = slice(j * SSM_TILE_ST, (j + 1) * SSM_TILE_ST)
            ch = slice(j * SSM_TILE_CH, (j + 1) * SSM_TILE_CH)
            up = _to_segment_order(u_refs[j], perm_ref)
            bu = _dot(up.astype(BF16), b_ref[j])
            (fr, fi), _ = _segment_scan(bu, xs_ref, w_ref.at[j], lam_ref[0:1, st], lam_ref[1:2, st],
                                        carry_ref[0:1, st], carry_ref[1:2, st], conj=False, reverse=False)
            carry_ref[0:1, st] = fr
            carry_ref[1:2, st] = fi
            xs = xs_ref[...].astype(BF16)
            xall_ref[:, j * 2 * SSM_TILE_ST:(j + 1) * 2 * SSM_TILE_ST] = xs
            yp = _dot(xs, c_ref[j]) + d_ref[:, ch] * up

            def store(i, rows, ch=ch):
                y_ref[i * per:(i + 1) * per, ch] = rows
                yg_ref[i * per:(i + 1) * per, ch] = _gelu(rows).astype(BF16)

            _to_time_order(yp, perm_ref, store)

    u_specs = [pl.BlockSpec((t_rows, SSM_TILE_CH), lambda c, k=k: (c, u_off // SSM_TILE_CH + k)) for k in range(nj)]
    full3 = lambda shape: pl.BlockSpec(shape, lambda c: (0, 0, 0))
    full2 = lambda shape: pl.BlockSpec(shape, lambda c: (0, 0))
    rows = pl.BlockSpec((t_rows, w), lambda c: (c, 0))
    outs, carried = _call(
        body, name="ssm_fwd", grid=(s // t_rows,),
        in_specs=u_specs + [full3(bmat.shape), full3(cmat.shape), full2(lam.shape), full2(dskip.shape)],
        out_specs=[rows, rows, pl.BlockSpec((1, 2, ns), lambda c: (c, 0, 0)), pl.BlockSpec((t_rows, 2 * ns), lambda c: (c, 0))],
        out_shape=[jax.ShapeDtypeStruct((s, w), F32), jax.ShapeDtypeStruct((s, w), BF16),
                   jax.ShapeDtypeStruct((s // t_rows, 2, ns), F32), jax.ShapeDtypeStruct((s, 2 * ns), BF16)],
        scratch_shapes=[pltpu.VMEM((2, ns), F32), pltpu.VMEM((nj, t_rows, 2 * SSM_TILE_ST), F32),
                        pltpu.VMEM((t_rows, 2 * SSM_TILE_ST), F32), pltpu.VMEM((t_rows, SSM_TILE_CH), F32)],
        semantics=("arbitrary",), args=[*([z] * nj), bmat, cmat, lam, dskip], side=side)
    return outs if side is None else (outs, carried)


def _ssm_bwd(z, y, dyg, xin, xall, bmat, cmat, lam, dskip, u_off, side=None):
    s, nj, w, ns, t_rows = _ssm_dims(z, bmat, u_off)
    nc = s // t_rows
    per = t_rows // SSM_SEGMENTS
    seg, half = SSM_SEGMENTS, SSM_TILE_ST

    def body(*refs):
        u_refs = refs[:nj]
        (y_ref, dyg_ref, xin_ref, xall_ref, b_ref, c_ref, lam_ref, d_ref, du_ref, db_ref, dc_ref, dlam_ref, dd_ref,
         carry_ref, w_ref, gs_ref, perm_ref, acc_ref) = refs[nj:]

        @pl.when(pl.program_id(0) == 0)
        def _():
            carry_ref[...] = jnp.zeros_like(carry_ref)
            db_ref[...] = jnp.zeros_like(db_ref)
            dc_ref[...] = jnp.zeros_like(dc_ref)
            dd_ref[...] = jnp.zeros_like(dd_ref)
            acc_ref[...] = jnp.zeros_like(acc_ref)
            _fill_powers(lam_ref, w_ref, nj, per)

        for j in range(nj):
            st = slice(j * SSM_TILE_ST, (j + 1) * SSM_TILE_ST)
            ch = slice(j * SSM_TILE_CH, (j + 1) * SSM_TILE_CH)
            lr, li = lam_ref[0:1, st], lam_ref[1:2, st]
            up = _to_segment_order(u_refs[j], perm_ref)
            upb = up.astype(BF16)
            dyp = _to_segment_order(dyg_ref[:, ch] * _gelu_grad(y_ref[:, ch]), perm_ref)
            dyb = dyp.astype(BF16)
            xs = xall_ref[:, j * 2 * half:(j + 1) * 2 * half]
            xf = xs.astype(F32)
            ends = xf[t_rows - seg:t_rows - 1, :]
            in_r = jnp.concatenate([xin_ref[0, 0:1, st], ends[:, :half]], axis=0)
            in_i = jnp.concatenate([xin_ref[0, 1:2, st], ends[:, half:]], axis=0)
            (gr, gi), _ = _segment_scan(_dot_nt(dyb, c_ref[j]), gs_ref, w_ref.at[j], lr, li,
                                        carry_ref[0:1, st], carry_ref[1:2, st], conj=True, reverse=True)
            carry_ref[0:1, st] = gr
            carry_ref[1:2, st] = gi
            gs = gs_ref[...]
            xsr, xsi, gsr, gsi = xf[:, :half], xf[:, half:], gs[:, :half], gs[:, half:]
            pxr = jnp.concatenate([in_r, xsr[:t_rows - seg]], axis=0)
            pxi = jnp.concatenate([in_i, xsi[:t_rows - seg]], axis=0)
            dl_r = gsr * pxr + gsi * pxi
            dl_i = gsi * pxr - gsr * pxi
            acc_ref[0, :, st] += jnp.sum(dl_r.reshape(per, seg, half), axis=0)
            acc_ref[1, :, st] += jnp.sum(dl_i.reshape(per, seg, half), axis=0)
            gx = gs.astype(BF16)
            dup = _dot_nt(gx, b_ref[j]) + d_ref[:, ch] * dyp

            def store(i, rows, ch=ch):
                du_ref[i * per:(i + 1) * per, ch] = rows.astype(BF16)

            _to_time_order(dup, perm_ref, store)
            db_ref[j] += _dot_tn(upb, gx)
            dc_ref[j] += _dot_tn(xs, dyb)
            dd_ref[:, ch] += jnp.sum(dyp * up, axis=0, keepdims=True)

        @pl.when(pl.program_id(0) == nc - 1)
        def _():
            dlam_ref[...] = jnp.sum(acc_ref[...], axis=1)

    rev = lambda c: nc - 1 - c
    u_specs = [pl.BlockSpec((t_rows, SSM_TILE_CH), lambda c, k=k: (rev(c), u_off // SSM_TILE_CH + k))
               for k in range(nj)]
    full3 = lambda shape: pl.BlockSpec(shape, lambda c: (0, 0, 0))
    full2 = lambda shape: pl.BlockSpec(shape, lambda c: (0, 0))
    rows = pl.BlockSpec((t_rows, w), lambda c: (rev(c), 0))
    outs, carried = _call(
        body, name="ssm_bwd", grid=(nc,),
        in_specs=u_specs + [rows, rows, pl.BlockSpec((1, 2, ns), lambda c: (rev(c), 0, 0)),
                            pl.BlockSpec((t_rows, 2 * ns), lambda c: (rev(c), 0)),
                            full3(bmat.shape), full3(cmat.shape), full2(lam.shape), full2(dskip.shape)],
        out_specs=[rows, full3(bmat.shape), full3(cmat.shape), full2(lam.shape), full2(dskip.shape)],
        out_shape=[jax.ShapeDtypeStruct((s, w), BF16), jax.ShapeDtypeStruct(bmat.shape, F32),
                   jax.ShapeDtypeStruct(cmat.shape, F32), jax.ShapeDtypeStruct(lam.shape, F32),
                   jax.ShapeDtypeStruct(dskip.shape, F32)],
        scratch_shapes=[pltpu.VMEM((2, ns), F32), pltpu.VMEM((nj, t_rows, 2 * SSM_TILE_ST), F32),
                        pltpu.VMEM((t_rows, 2 * SSM_TILE_ST), F32),
                        pltpu.VMEM((t_rows, SSM_TILE_CH), F32), pltpu.VMEM((2, SSM_SEGMENTS, ns), F32)],
        semantics=("arbitrary",), args=[*([z] * nj), y, dyg, xin, xall, bmat, cmat, lam, dskip], side=side)
    return outs if side is None else (outs, carried)


def _adam_math(w, g, m, v):
    m = ADAM_B1 * m + (1.0 - ADAM_B1) * g
    v = ADAM_B2 * v + (1.0 - ADAM_B2) * (g * g)
    m_hat = m / (1.0 - ADAM_B1 ** ADAM_STEP)
    v_hat = v / (1.0 - ADAM_B2 ** ADAM_STEP)
    delta = -ADAM_LR * (m_hat / (jnp.sqrt(v_hat) + ADAM_EPS) + ADAM_WD * w)
    return delta, m, v


def _adam_rows(r, c):
    for tr in (512, 256, 128, 64, 32, 16, 8):
        if r % tr == 0 and tr * c * 4 <= (2 << 20):
            return tr
    return r


def _adamw_big(w, p_mine, p_sib, m, v, name, side=None):
    r, c = w.shape
    tr = _adam_rows(r, c)

    def body(w_ref, a_ref, b_ref, m_ref, v_ref, g_ref, d_ref, nm_ref, nv_ref):
        g = a_ref[...] + b_ref[...]
        g_ref[...] = g
        d_ref[...], nm_ref[...], nv_ref[...] = _adam_math(w_ref[...], g, m_ref[...], v_ref[...])

    blk = pl.BlockSpec((tr, c), lambda i: (i, 0))
    outs, carried = _call(body, name=f"adamw_{name}", grid=(r // tr,), in_specs=[blk] * 5, out_specs=[blk] * 4,
                          out_shape=[jax.ShapeDtypeStruct((r, c), F32)] * 4, semantics=("parallel",),
                          args=[w, p_mine, p_sib, m, v], side=side)
    return outs if side is None else (outs, carried)


def _adamw_small(w, parts, m, v):
    r, c = w.shape
    n_dev = parts.shape[0]

    def body(w_ref, p_ref, m_ref, v_ref, g_ref, d_ref, nm_ref, nv_ref):
        g = p_ref[0]
        for k in range(1, n_dev):
            g = g + p_ref[k]
        g_ref[...] = g
        d_ref[...], nm_ref[...], nv_ref[...] = _adam_math(w_ref[...], g, m_ref[...], v_ref[...])

    blk = pl.BlockSpec((r, c), lambda i: (0, 0))
    return pl.pallas_call(body, name="adamw_small", grid=(1,),
                          in_specs=[blk, pl.BlockSpec((n_dev, r, c), lambda i: (0, 0, 0)), blk, blk],
                          out_specs=[blk] * 4, out_shape=[jax.ShapeDtypeStruct((r, c), F32)] * 4,
                          compiler_params=_params(("arbitrary",)))(w, parts, m, v)


def _cast_bf16(w, name):
    r, c = w.shape
    tr = _adam_rows(r, c)

    def body(w_ref, o_ref):
        o_ref[...] = w_ref[...].astype(BF16)

    blk = pl.BlockSpec((tr, c), lambda i: (i, 0))
    return pl.pallas_call(body, name=f"cast_{name}", grid=(r // tr,), in_specs=[blk], out_specs=blk,
                          out_shape=jax.ShapeDtypeStruct((r, c), BF16), compiler_params=_params(("parallel",)))(w)


def _sum_slots(recv, name):
    _, r, c = recv.shape
    tr = _adam_rows(r, c)

    def body(p_ref, o_ref):
        acc = p_ref[0].astype(F32)
        for k in range(1, N_CHIPS):
            acc = acc + p_ref[k].astype(F32)
        o_ref[...] = acc

    return pl.pallas_call(body, name=f"sum_{name}", grid=(r // tr,),
                          in_specs=[pl.BlockSpec((N_CHIPS, tr, c), lambda i: (0, i, 0))],
                          out_specs=pl.BlockSpec((tr, c), lambda i: (i, 0)),
                          out_shape=jax.ShapeDtypeStruct((r, c), F32), compiler_params=_params(("parallel",)))(recv)


BIG_WEIGHTS = ("w_in", "w_attn_up", "w_glu_v", "w_glu_g", "w_out", "w_ffn_gate", "w_ffn_up", "w_ffn_down")
COL_SHARDED = ("w_in", "w_attn_up", "w_glu_v", "w_glu_g", "w_ffn_gate", "w_ffn_up")


def _aligned(v, m):
    return v if isinstance(v, int) else pl.multiple_of(v, m)


def _shard_of(ref, name, j, shard_shape, half=None):
    r, c = shard_shape
    rows = r if half is None else r // 2
    row0 = 0 if half is None else half * rows
    if name in COL_SHARDED:
        return ref.at[pl.ds(_aligned(row0, 16), rows), pl.ds(_aligned(j * c, 128), c)]
    return ref.at[pl.ds(_aligned(j * r + row0, 16), rows), :]


def _other_chips():
    x, y = lax.axis_index("x"), lax.axis_index("y")
    return [(1 - x, y), (x, 1 - y), (1 - x, 1 - y)]


def _dma_sems(n, arrays):
    return [pltpu.SemaphoreType.DMA((n, 3))] * arrays + [pltpu.SemaphoreType.DMA((n,))]


def _gather_side(shards):
    names = list(shards)
    n = len(names)
    full_shapes = []
    for k in names:
        r, c = shards[k].shape
        full_shapes.append((r, c * N_CHIPS) if k in COL_SHARDED else (r * N_CHIPS, c))

    def build(src, dst, sems):
        send_sems, recv_sems, pass_send_sems, pass_recv_sems, local_sems = sems
        x, y, c = lax.axis_index("x"), lax.axis_index("y"), lax.axis_index("c")
        me = 2 * x + y
        locals_, sends, arrivals, forwards, passed_on = [], [], [], [], []
        for i, k in enumerate(names):
            shape = shards[k].shape
            half_rows = shape[0] // 2
            locals_.append(pltpu.make_async_copy(src[i], _shard_of(dst[i], k, me, shape), local_sems.at[i]))
            my_half = src[i].at[pl.ds(_aligned(c * half_rows, 16), half_rows), :]
            for p, (px, py) in enumerate(_other_chips()):
                peer = 2 * px + py
                landed = _shard_of(dst[i], k, peer, shape, half=c)
                sends.append(pltpu.make_async_remote_copy(
                    src_ref=my_half, dst_ref=_shard_of(dst[i], k, me, shape, half=c), send_sem=send_sems.at[i, p],
                    recv_sem=recv_sems.at[i, p], device_id=(px, py, c), device_id_type=MESH))
                arrivals.append(pltpu.make_async_remote_copy(
                    src_ref=my_half, dst_ref=landed, send_sem=send_sems.at[i, p],
                    recv_sem=recv_sems.at[i, p], device_id=(px, py, c), device_id_type=MESH))
                forwards.append(pltpu.make_async_remote_copy(
                    src_ref=landed, dst_ref=landed, send_sem=pass_send_sems.at[i, p],
                    recv_sem=pass_recv_sems.at[i, p], device_id=(x, y, 1 - c), device_id_type=MESH))
                passed_on.append(pltpu.make_async_remote_copy(
                    src_ref=landed, dst_ref=_shard_of(dst[i], k, peer, shape, half=1 - c),
                    send_sem=pass_send_sems.at[i, p], recv_sem=pass_recv_sems.at[i, p],
                    device_id=(x, y, 1 - c), device_id_type=MESH))
        return locals_, sends, arrivals, forwards, passed_on

    return _Side([shards[k] for k in names], [jax.ShapeDtypeStruct(s, BF16) for s in full_shapes], _dma_sems(n, 4), build,
                 relays=True)


def _scatter_side(grads, shard_shapes):
    names = list(grads)
    n = len(names)

    def build(src, dst, sems):
        send_sems, recv_sems, local_sems = sems
        x, y, c = lax.axis_index("x"), lax.axis_index("y"), lax.axis_index("c")
        me = 2 * x + y
        locals_, sends, arrivals = [], [], []
        for i, k in enumerate(names):
            shape = shard_shapes[k]
            locals_.append(pltpu.make_async_copy(_shard_of(src[i], k, me, shape), dst[i].at[me], local_sems.at[i]))
            for p, (px, py) in enumerate(_other_chips()):
                peer = 2 * px + py
                sends.append(pltpu.make_async_remote_copy(
                    src_ref=_shard_of(src[i], k, peer, shape), dst_ref=dst[i].at[me], send_sem=send_sems.at[i, p],
                    recv_sem=recv_sems.at[i, p], device_id=(px, py, c), device_id_type=MESH))
                arrivals.append(pltpu.make_async_remote_copy(
                    src_ref=_shard_of(src[i], k, peer, shape), dst_ref=dst[i].at[peer], send_sem=send_sems.at[i, p],
                    recv_sem=recv_sems.at[i, p], device_id=(px, py, c), device_id_type=MESH))
        return locals_, sends, arrivals, [None] * len(arrivals), []

    return _Side([grads[k] for k in names],
                 [jax.ShapeDtypeStruct((N_CHIPS,) + tuple(shard_shapes[k]), BF16) for k in names], _dma_sems(n, 2), build)


def _put_cols(dz, src, col_off):
    s, w = src.shape
    tr = _pick(s, (2048, 1024, 512, 256, 128, 64, 8))
    tc = _pick(math.gcd(w, col_off), (1024, 512, 256, 128))
    off = col_off // tc

    def body(src_ref, dz_ref, o_ref):
        del dz_ref
        o_ref[...] = src_ref[...].astype(o_ref.dtype)

    return pl.pallas_call(
        body, name="put_cols", grid=(s // tr, w // tc),
        in_specs=[pl.BlockSpec((tr, tc), lambda i, j: (i, j)), pl.BlockSpec(memory_space=pl.ANY)],
        out_specs=pl.BlockSpec((tr, tc), lambda i, j: (i, off + j)),
        out_shape=jax.ShapeDtypeStruct(dz.shape, dz.dtype), input_output_aliases={1: 0},
        compiler_params=_params(("parallel", "parallel")))(src, dz)


def _swap_side(parts):
    n = len(parts)

    def build(src, dst, sems):
        send_sems, recv_sems = sems
        sibling = (lax.axis_index("x"), lax.axis_index("y"), 1 - lax.axis_index("c"))
        copies = [pltpu.make_async_remote_copy(src_ref=src[i], dst_ref=dst[i], send_sem=send_sems.at[i],
                                               recv_sem=recv_sems.at[i], device_id=sibling, device_id_type=MESH)
                  for i in range(n)]
        return [], copies, copies, [None] * n, []

    return _Side(parts, [jax.ShapeDtypeStruct(p.shape, F32) for p in parts],
                 [pltpu.SemaphoreType.DMA((n,)), pltpu.SemaphoreType.DMA((n,))], build)


def _share_side(packed):
    r, c = packed.shape

    def build(src, dst, sems):
        send_sems, recv_sems, local_sem = sems
        x, y, cc = lax.axis_index("x"), lax.axis_index("y"), lax.axis_index("c")
        me = 4 * x + 2 * y + cc
        own = pltpu.make_async_copy(src[0], dst[0].at[me], local_sem)
        sends, arrivals = [], []
        flips = [(fx, fy, fc) for fx in range(2) for fy in range(2) for fc in range(2) if fx or fy or fc]
        for p, (fx, fy, fc) in enumerate(flips):
            px, py, pc = x ^ fx, y ^ fy, cc ^ fc
            sends.append(pltpu.make_async_remote_copy(
                src_ref=src[0], dst_ref=dst[0].at[me], send_sem=send_sems.at[p], recv_sem=recv_sems.at[p],
                device_id=(px, py, pc), device_id_type=MESH))
            arrivals.append(pltpu.make_async_remote_copy(
                src_ref=src[0], dst_ref=dst[0].at[4 * px + 2 * py + pc], send_sem=send_sems.at[p],
                recv_sem=recv_sems.at[p], device_id=(px, py, pc), device_id_type=MESH))
        return [own], sends, arrivals, [None] * len(arrivals), []

    return _Side([packed], [jax.ShapeDtypeStruct((8, r, c), F32)],
                 [pltpu.SemaphoreType.DMA((7,)), pltpu.SemaphoreType.DMA((7,)), pltpu.SemaphoreType.DMA], build)


SMALL_WEIGHTS = ("norm_mix_pre", "ssm_a_re", "ssm_a_im", "ssm_log_dt", "ssm_b_re", "ssm_b_im", "ssm_c_re", "ssm_c_im",
                 "ssm_d", "norm_mix_post", "norm_ffn_pre", "norm_ffn_post")
WEIGHT_ORDER = ("norm_mix_pre", "w_in", "w_attn_up", "ssm_a_re", "ssm_a_im", "ssm_log_dt", "ssm_b_re", "ssm_b_im",
                "ssm_c_re", "ssm_c_im", "ssm_d", "w_glu_v", "w_glu_g", "w_out", "norm_mix_post", "norm_ffn_pre",
                "w_ffn_gate", "w_ffn_up", "w_ffn_down", "norm_ffn_post")
PACK_LANES = 128
PACK_ROWS = 8
PACK_GROUPS = (SMALL_WEIGHTS[:1], SMALL_WEIGHTS[1:])


def _pack_group(arrs, names):
    flat = jnp.concatenate([arrs[k].reshape(-1) for k in names])
    pad = -flat.shape[0] % (PACK_LANES * PACK_ROWS)
    return jnp.pad(flat, (0, pad)).reshape(-1, PACK_LANES)


def _pack_small(arrs):
    return jnp.concatenate([_pack_group(arrs, names) for names in PACK_GROUPS], axis=0)


def _unpack_small(packed, like):
    out, row = {}, 0
    for names in PACK_GROUPS:
        rows = _pack_group(like, names).shape[0]
        flat, pos = packed[row:row + rows].reshape(-1), 0
        for k in names:
            n = like[k].size
            out[k] = flat[pos:pos + n].reshape(like[k].shape)
            pos += n
        row += rows
    return out


def _local_step(x, target, big, small, shards=None, shard_shapes=None, h1=None):
    s, d = x.shape
    big, grads, slots = dict(big), {}, {}
    carry = shards is not None

    def gathering(names, call):
        if not carry:
            return call(None)
        res, got = call(_gather_side({k: shards[k] for k in names}))
        big.update(zip(names, got))
        return res

    def scattering(names, call):
        if not carry:
            return call(None)
        res, got = call(_scatter_side({k: grads[k] for k in names}, shard_shapes))
        slots.update(zip(names, got))
        return res

    u_off = 3 * HQ
    gate_off = u_off + d // 2
    g1, g2, g3, g4 = (small[k][0:1] for k in ("norm_mix_pre", "norm_mix_post", "norm_ffn_pre", "norm_ffn_post"))
    ssm_names = ("ssm_a_re", "ssm_a_im", "ssm_log_dt", "ssm_b_re", "ssm_b_im", "ssm_c_re", "ssm_c_im")
    (lam, bmat, cmat), ssm_vjp = jax.vjp(_ssm_prepare, *[small[k][0] for k in ssm_names])
    bmat, cmat = bmat.astype(BF16), cmat.astype(BF16)
    dskip = small["ssm_d"][0:1]

    if h1 is None:
        h1 = _norm_in(x, g1)
    z = gathering(("w_attn_up", "w_glu_v", "w_glu_g", "w_out", "w_ffn_gate"),
                  lambda side: _mm(h1, big["w_in"], "nn", F32, "in_proj", side=side))
    y, yg, xin, xall = gathering(("w_ffn_up",), lambda side: _ssm_fwd(z, bmat, cmat, lam, dskip, u_off, side=side))
    qkv = [_dilate_qkv(z, g, dil) for g, dil in enumerate(ATTN_DILATIONS)]
    outs, lses = zip(*[_attn_fwd(qkv[g], g, dil) for g, dil in enumerate(ATTN_DILATIONS)])
    attn = _attn_merge(outs, lses)
    merged, ab, gv, gg = _mm_fused(
        [attn, yg], [big["w_attn_up"], big["w_glu_v"], big["w_glu_g"]], [(0, 0), (1, 1), (1, 2)], "nn",
        [BF16, BF16, BF16, BF16], "branches_merge", extras=[(z, gate_off), (z, gate_off + d)], epilogue=_gates_epilogue)
    mo = _mm(merged, big["w_out"], "nn", F32, "mix_out")
    x2, h2 = _norm_mid(x, mo, g2, g3)
    act, fg, fu = gathering(("w_ffn_down",), lambda side: _mm_fused(
        [h2], [big["w_ffn_gate"], big["w_ffn_up"]], [(0, 0), (0, 1)], "nn", [BF16, BF16, BF16], "ffn_up_act",
        epilogue=_swiglu_epilogue, side=side))
    f = _mm(act, big["w_ffn_down"], "nn", F32, "ffn_down")
    loss, dout, df, dg4 = _loss_head(x2, f, g4, target)

    grads["w_ffn_down"] = _mm_kloop(act, df, "tn", BF16, "dw_ffn_down")
    dfg, dfu = scattering(("w_ffn_down",), lambda side: _mm_fused(
        [df], [big["w_ffn_down"]], [(0, 0)], "nt", [BF16, BF16], "d_ffn_act", extras=[(fg, 0), (fu, 0)],
        epilogue=_swiglu_bwd_epilogue, side=side))
    grads["w_ffn_gate"] = _mm_kloop(h2, dfg, "tn", BF16, "dw_ffn_gate")
    dh2 = scattering(("w_ffn_gate",), lambda side: _mm_fused(
        [dfg, dfu], [big["w_ffn_gate"], big["w_ffn_up"]], [(0, 0), (1, 1)], "nt", [F32], "d_h2",
        epilogue=_sum_epilogue, side=side))[0]
    grads["w_ffn_up"] = _mm_kloop(h2, dfu, "tn", BF16, "dw_ffn_up")
    dx2, dmo, dg2, dg3 = _norm_mid_bwd(x2, mo, g2, g3, dout, dh2)
    dz, dgs, dab, dgv, dgg = _mm_fused(
        [dmo], [big["w_out"]], [(0, 0)], "nt", [BF16] * 5, "d_merged_gates",
        extras=[(z, gate_off), (z, gate_off + d), (ab, 0), (gv, 0), (gg, 0)], epilogue=_gates_bwd_epilogue,
        out_place=[(z.shape[1], gate_off), None, None, None, None])
    dz = _put_cols(dz, dgs, gate_off + d)
    grads["w_out"] = _mm_kloop(merged, dmo, "tn", BF16, "dw_out")
    dyg = _mm_fused([dgv, dgg], [big["w_glu_v"], big["w_glu_g"]], [(0, 0), (1, 1)], "nt", [F32], "d_yg",
                    epilogue=_sum_epilogue)[0]
    grads["w_glu_v"] = _mm_kloop(yg, dgv, "tn", BF16, "dw_glu_v")
    grads["w_glu_g"] = _mm_kloop(yg, dgg, "tn", BF16, "dw_glu_g")
    du, dbmat, dcmat, dlam, dd = scattering(
        ("w_ffn_up",),
        lambda side: _ssm_bwd(z, y, dyg, xin, xall, bmat, cmat, lam, dskip, u_off, side=side))
    dz = _put_cols(dz, du, u_off)
    dattn = _mm(dab, big["w_attn_up"], "nt", F32, "d_attn")
    grads["w_attn_up"] = _mm_kloop(attn, dab, "tn", BF16, "dw_attn_up")
    merged_bwd = _attn_merge_bwd(outs, lses, dattn)
    mine, theirs = {}, {}
    for g, dil in enumerate(ATTN_DILATIONS):
        side = None
        if carry and g == 0:
            mine = {k: _sum_slots(slots[k], k) for k in slots}
            side = _swap_side(list(mine.values()))
        dqkv = _attn_bwd(qkv[g], merged_bwd[g], lses[g], merged_bwd[3 + g], g, dil, side=side)
        if side is not None:
            dqkv, got = dqkv
            theirs = dict(zip(mine, got))
        dz = _undilate_dqkv(dqkv, dz, g, dil)
    small_grads = dict(zip(ssm_names, (t[None] for t in ssm_vjp((dlam, dbmat, dcmat)))))
    small_grads.update(norm_mix_post=dg2, norm_ffn_pre=dg3, norm_ffn_post=dg4, ssm_d=dd)
    if carry:
        late = ("w_attn_up", "w_out", "w_glu_v", "w_glu_g")
        side = _join_sides(_scatter_side({k: grads[k] for k in late}, shard_shapes),
                           _share_side(_pack_group(small_grads, PACK_GROUPS[1])))
        grads["w_in"], got = _mm_kloop(h1, dz, "tn", BF16, "dw_in", side=side)
        slots.update(zip(late, got[:-1]))
        shared = got[-1]
    else:
        grads["w_in"] = _mm_kloop(h1, dz, "tn", BF16, "dw_in")
    dh1 = scattering(("w_in",), lambda side: _mm_kloop(dz, big["w_in"], "nt", F32, "d_h1", side=side))
    grad_x, dg1 = _norm_in_bwd(x, g1, dh1, dx2)
    small_grads["norm_mix_pre"] = dg1
    if carry:
        return loss[0, 0], grad_x, (slots, mine, theirs), (dg1, shared)
    return loss[0, 0], grad_x, grads, small_grads


def kernel(x, norm_mix_pre, w_in, w_attn_up, ssm_a_re, ssm_a_im, ssm_log_dt, ssm_b_re, ssm_b_im, ssm_c_re, ssm_c_im, ssm_d, w_glu_v, w_glu_g, w_out, norm_mix_post, norm_ffn_pre, w_ffn_gate, w_ffn_up, w_ffn_down, norm_ffn_post, loss_target, m_norm_mix_pre, m_w_in, m_w_attn_up, m_ssm_a_re, m_ssm_a_im, m_ssm_log_dt, m_ssm_b_re, m_ssm_b_im, m_ssm_c_re, m_ssm_c_im, m_ssm_d, m_w_glu_v, m_w_glu_g, m_w_out, m_norm_mix_post, m_norm_ffn_pre, m_w_ffn_gate, m_w_ffn_up, m_w_ffn_down, m_norm_ffn_post, v_norm_mix_pre, v_w_in, v_w_attn_up, v_ssm_a_re, v_ssm_a_im, v_ssm_log_dt, v_ssm_b_re, v_ssm_b_im, v_ssm_c_re, v_ssm_c_im, v_ssm_d, v_w_glu_v, v_w_glu_g, v_w_out, v_norm_mix_post, v_norm_ffn_pre, v_w_ffn_gate, v_w_ffn_up, v_w_ffn_down, v_norm_ffn_post):
    given = dict(locals())
    w = {k: given[k] for k in WEIGHT_ORDER}
    m = {k: given["m_" + k] for k in WEIGHT_ORDER}
    v = {k: given["v_" + k] for k in WEIGHT_ORDER}

    shard_shapes = {k: w[k].shape[1:] for k in BIG_WEIGHTS}
    shards = {"w_in": _cast_bf16(w["w_in"][0], "w_in")}
    h1, casts, got = _prologue(x[0], norm_mix_pre[0:1], {k: w[k][0] for k in BIG_WEIGHTS if k != "w_in"},
                               side=_gather_side({"w_in": shards["w_in"]}))
    shards.update(casts)
    big = {"w_in": got[0]}

    loss, grad_x, (slots, mine, theirs), small_grads = _local_step(
        x[0], loss_target[0], big, {k: w[k] for k in SMALL_WEIGHTS}, shards, shard_shapes, h1)
    loss = lax.psum(loss, MESH_AXES)

    early = [k for k in BIG_WEIGHTS if k in mine]
    last = [k for k in BIG_WEIGHTS if k not in mine]
    mine.update({k: _sum_slots(slots[k], k) for k in last})
    dg1, shared = small_grads
    last_exchanges = _join_sides(_swap_side([mine[k] for k in last]),
                                 _share_side(_pack_group({"norm_mix_pre": dg1}, PACK_GROUPS[0])))
    out_g, out_d, out_m, out_v = {}, {}, {}, {}
    for k in early + last:
        if k == early[0]:
            res, got = _adamw_big(w[k][0], mine[k], theirs[k], m[k][0], v[k][0], k, side=last_exchanges)
            theirs.update(zip(last, got[:-1]))
            late = got[-1]
        else:
            res = _adamw_big(w[k][0], mine[k], theirs[k], m[k][0], v[k][0], k)
        out_g[k], out_d[k], out_m[k], out_v[k] = (t[None] for t in res)

    pick = lambda tree: {k: tree[k] for k in SMALL_WEIGHTS}
    parts = jnp.concatenate([late, shared], axis=1)
    res = _adamw_small(_pack_small(pick(w)), parts, _pack_small(pick(m)), _pack_small(pick(v)))
    for dst, packed in zip((out_g, out_d, out_m, out_v), res):
        dst.update(_unpack_small(packed, pick(w)))

    return (loss, grad_x[None], *[out_g[k] for k in WEIGHT_ORDER], *[out_d[k] for k in WEIGHT_ORDER],
            *[out_m[k] for k in WEIGHT_ORDER], *[out_v[k] for k in WEIGHT_ORDER])
```
